```python
import math
import jax, jax.numpy as jnp
from jax import lax
import numpy as np

D_MODEL = 1024
BATCH = 4
SEQ = 4096
DEPTH = 1

D_MIX = D_MODEL
D_RWKV = D_MIX // 2
RWKV_HEAD = 64
RWKV_HEADS = D_RWKV // RWKV_HEAD
D_GMLP = D_MIX - D_RWKV
GMLP_GROUPS = 4
GMLP_GROUP_W = D_GMLP // GMLP_GROUPS
CHUNK = 128
DECAY_LORA = 64
ICLR_LORA = 64
GATE_LORA = 128
N_SHIFT = 3 * D_RWKV + DECAY_LORA + ICLR_LORA + GATE_LORA
D_IN = N_SHIFT + 2 * D_GMLP
D_PLE = 256
N_GROUPS = 4
EXPERTS_PER_GROUP = 8
N_EXPERTS = N_GROUPS * EXPERTS_PER_GROUP
TOP_K = 2
D_EXPERT = 512
EXPERT_BLOCK = 128
ALPHA = (2.0 * DEPTH) ** 0.25
BETA = (8.0 * DEPTH) ** -0.25
LN_EPS = 1e-5
GN_EPS = 64e-5
DECAY_SCALE = math.exp(-0.5)

kernel_name = "hymba_rwkv7_gmlp_hmoe_deepnorm"


def layer_norm(x, g, b, eps=LN_EPS):
    xf = x.astype(jnp.float32)
    mu = xf.mean(-1, keepdims=True)
    var = jnp.square(xf - mu).mean(-1, keepdims=True)
    return ((xf - mu) * lax.rsqrt(var + eps) * g + b).astype(x.dtype)


def token_shift(h):
    return jnp.pad(h, ((0, 0), (1, 0), (0, 0)))[:, :-1]


def wkv7_scan(r, w, k, v, a, b):
    Bn, Sn, H, N = r.shape

    def step(state, inp):
        r_t, w_t, k_t, v_t, a_t, b_t = inp
        sa = jnp.einsum('bhij,bhj->bhi', state, a_t)
        state = (state * w_t[:, :, None, :] + sa[..., None] * b_t[:, :, None, :]
                 + v_t[..., None] * k_t[:, :, None, :])
        y = jnp.einsum('bhij,bhj->bhi', state, r_t)
        return state, y

    xs = (jnp.moveaxis(t, 1, 0) for t in (r, w, k, v, a, b))
    s0 = jnp.zeros((Bn, H, N, N), jnp.float32)
    _, y = lax.scan(step, s0, tuple(xs))
    return jnp.moveaxis(y, 0, 1)


def rwkv7_mix(proj, mu, w0, w_decay_up, a0, w_iclr_up, w_gate_up, k_k, k_a, r_k, gn_g, gn_b):
    Bn, Sn, _ = proj.shape
    f32 = jnp.float32
    h = proj[..., :N_SHIFT]
    h = h + (token_shift(h) - h) * mu
    c1, c2, c3 = D_RWKV, 2 * D_RWKV, 3 * D_RWKV
    r, k, v, xw, xa, xg = jnp.split(h, [c1, c2, c3, c3 + DECAY_LORA, c3 + DECAY_LORA + ICLR_LORA], axis=-1)
    d = (w0 + jnp.tanh(xw) @ w_decay_up).astype(f32)
    w = jnp.exp(-DECAY_SCALE * jax.nn.sigmoid(d))
    a = jax.nn.sigmoid(a0 + xa @ w_iclr_up)
    g = jax.nn.sigmoid(xg) @ w_gate_up
    heads = lambda t: t.reshape(Bn, Sn, RWKV_HEADS, RWKV_HEAD).astype(f32)
    kk = heads(k * k_k)
    kk = kk / jnp.maximum(jnp.sqrt(jnp.sum(kk * kk, -1, keepdims=True)), 1e-12)
    k = k * (1 + (a - 1) * k_a)
    rh, kh, vh, ah, wh = heads(r), heads(k), heads(v), heads(a), heads(w)
    y = wkv7_scan(rh, wh, kh, vh, -kk, kk * ah)
    m = y.mean(-1, keepdims=True)
    var = jnp.square(y - m).mean(-1, keepdims=True)
    y = ((y - m) * lax.rsqrt(var + GN_EPS)).reshape(Bn, Sn, D_RWKV) * gn_g + gn_b
    bonus = jnp.sum(rh * kh * r_k, -1, keepdims=True) * vh
    y = y + bonus.reshape(Bn, Sn, D_RWKV)
    return (y * g).astype(proj.dtype)


def gmlp_mix(proj, ln_g, ln_b, w_spatial, b_spatial):
    Bn, Sn, _ = proj.shape
    n_chunks = Sn // CHUNK
    z = jax.nn.gelu(proj[..., N_SHIFT:], approximate=False)
    zu, zv = jnp.split(z, 2, axis=-1)
    shp = (Bn, n_chunks, CHUNK, GMLP_GROUPS, GMLP_GROUP_W)
    zv = layer_norm(zv.reshape(shp), ln_g.reshape(GMLP_GROUPS, GMLP_GROUP_W),
                    ln_b.reshape(GMLP_GROUPS, GMLP_GROUP_W))
    causal = jnp.tril(jnp.ones((CHUNK, CHUNK), dtype=bool))
    ws = jnp.where(causal, w_spatial, 0)
    mixed = jnp.einsum('gts,bcsgd->bctgd', ws, zv) + b_spatial.T[:, :, None]
    return (zu.reshape(shp) * mixed).reshape(Bn, Sn, D_GMLP)


def hmoe(x, w_group_router, b_group_router, w_expert_router, b_expert_router, w_gate, w_up, w_down):
    Bn, Sn, D = x.shape
    f32 = jnp.float32
    T = Bn * Sn
    xf = x.reshape(T, D)
    g_logits = (xf @ w_group_router).astype(f32) + b_group_router
    g_prob = jax.nn.softmax(g_logits, -1)
    g_sel = jnp.argmax(g_logits, -1).astype(jnp.int32)
    p_group = jnp.take_along_axis(g_prob, g_sel[:, None], -1)
    e_all = (xf @ w_expert_router.reshape(D, N_GROUPS * EXPERTS_PER_GROUP)).astype(f32)
    e_all = e_all.reshape(T, N_GROUPS, EXPERTS_PER_GROUP) + b_expert_router
    e_logits = jnp.take_along_axis(e_all, g_sel[:, None, None], 1)[:, 0]
    top_v, top_i = lax.top_k(e_logits, TOP_K)
    gate_w = jax.nn.softmax(top_v, -1) * p_group
    expert_id = g_sel[:, None] * EXPERTS_PER_GROUP + top_i.astype(jnp.int32)
    A = T * TOP_K
    flat_e = expert_id.reshape(A)
    flat_w = gate_w.reshape(A)
    flat_tok = (jnp.arange(A, dtype=jnp.int32) // TOP_K)
    order = jnp.argsort(flat_e)
    e_sorted = flat_e[order]
    counts = jnp.zeros((N_EXPERTS,), jnp.int32).at[flat_e].add(1)
    padded = (counts + EXPERT_BLOCK - 1) // EXPERT_BLOCK * EXPERT_BLOCK
    pad_end = jnp.cumsum(padded)
    pad_start = pad_end - padded
    start = jnp.cumsum(counts) - counts
    dest = pad_start[e_sorted] + jnp.arange(A, dtype=jnp.int32) - start[e_sorted]
    n_blocks = -(-A // EXPERT_BLOCK) + N_EXPERTS
    M = n_blocks * EXPERT_BLOCK
    row_tok = jnp.full((M,), T, jnp.int32).at[dest].set(flat_tok[order])
    row_w = jnp.zeros((M,), f32).at[dest].set(flat_w[order])
    block_e = jnp.minimum(jnp.searchsorted(pad_end, jnp.arange(n_blocks) * EXPERT_BLOCK, side='right'),
                          N_EXPERTS - 1)
    x_pad = jnp.concatenate([xf, jnp.zeros((1, D), xf.dtype)], 0)
    xb = x_pad[row_tok].reshape(n_blocks, EXPERT_BLOCK, D)

    def expert_block(args):
        xblk, e = args
        hid = jax.nn.silu(xblk @ w_gate[e]) * (xblk @ w_up[e])
        return hid @ w_down[e]

    yb = lax.map(expert_block, (xb, block_e)).reshape(M, D)
    y = jnp.zeros((T + 1, D), f32).at[row_tok].add(yb.astype(f32) * row_w[:, None])
    return y[:T].reshape(Bn, Sn, D).astype(x.dtype)


def setup_inputs(seed: int = 0) -> dict:
    key = jax.random.key(seed)
    ks = iter(jax.random.split(key, 48))
    nrm = lambda shape, scale: jax.random.normal(next(ks), shape, jnp.float32) * scale
    L, D = DEPTH, D_MODEL
    return {
        "x": nrm((BATCH, SEQ, D), 1.0),
        "p": nrm((DEPTH, BATCH, SEQ, D_PLE), 1.0),
        "ln_emb_g": 1.0 + nrm((D,), 0.02),
        "ln_emb_b": nrm((D,), 0.02),
        "w_in": nrm((L, D, D_IN), D ** -0.5),
        "mu_shift": jax.random.uniform(next(ks), (L, N_SHIFT), jnp.float32),
        "w0": -2.0 + nrm((L, D_RWKV), 1.0),
        "w_decay_up": nrm((L, DECAY_LORA, D_RWKV), 0.1 * DECAY_LORA ** -0.5),
        "a0": nrm((L, D_RWKV), 0.5),
        "w_iclr_up": nrm((L, ICLR_LORA, D_RWKV), 0.5 * ICLR_LORA ** -0.5),
        "w_gate_up": nrm((L, GATE_LORA, D_RWKV), GATE_LORA ** -0.5),
        "k_k": 0.85 + nrm((L, D_RWKV), 0.02),
        "k_a": 1.0 + nrm((L, D_RWKV), 0.02),
        "r_k": nrm((L, RWKV_HEADS, RWKV_HEAD), 0.1),
        "gn_g": 1.0 + nrm((L, D_RWKV), 0.02),
        "gn_b": nrm((L, D_RWKV), 0.02),
        "gmlp_ln_g": 1.0 + nrm((L, D_GMLP), 0.02),
        "gmlp_ln_b": nrm((L, D_GMLP), 0.02),
        "w_spatial": nrm((L, GMLP_GROUPS, CHUNK, CHUNK), 0.02),
        "b_spatial": 1.0 + nrm((L, GMLP_GROUPS, CHUNK), 0.02),
        "w_out": nrm((L, D_MIX, D), BETA * D_MIX ** -0.5),
        "ln1_g": 1.0 + nrm((L, D), 0.02),
        "ln1_b": nrm((L, D), 0.02),
        "w_group_router": nrm((L, D, N_GROUPS), D ** -0.5),
        "b_group_router": nrm((L, N_GROUPS), 0.01),
        "w_expert_router": nrm((L, D, N_GROUPS, EXPERTS_PER_GROUP), D ** -0.5),
        "b_expert_router": nrm((L, N_GROUPS, EXPERTS_PER_GROUP), 0.01),
        "w_exp_gate": nrm((L, N_EXPERTS, D, D_EXPERT), D ** -0.5),
        "w_exp_up": nrm((L, N_EXPERTS, D, D_EXPERT), D ** -0.5),
        "w_exp_down": nrm((L, N_EXPERTS, D_EXPERT, D), BETA * D_EXPERT ** -0.5),
        "w_ple_gate": nrm((L, D, D), D ** -0.5),
        "b_ple_gate": nrm((L, D), 0.02),
        "w_ple_proj": nrm((L, D_PLE, D), BETA * D_PLE ** -0.5),
        "ln2_g": 1.0 + nrm((L, D), 0.02),
        "ln2_b": nrm((L, D), 0.02),
    }


def reference(x, p, ln_emb_g, ln_emb_b, w_in, mu_shift, w0, w_decay_up, a0, w_iclr_up, w_gate_up,
              k_k, k_a, r_k, gn_g, gn_b, gmlp_ln_g, gmlp_ln_b, w_spatial, b_spatial, w_out,
              ln1_g, ln1_b, w_group_router, b_group_router, w_expert_router, b_expert_router,
              w_exp_gate, w_exp_up, w_exp_down, w_ple_gate, b_ple_gate, w_ple_proj, ln2_g, ln2_b):
    x = layer_norm(x, ln_emb_g, ln_emb_b)
    for i in range(DEPTH):
        proj = x @ w_in[i]
        y_a = rwkv7_mix(proj, mu_shift[i], w0[i], w_decay_up[i], a0[i], w_iclr_up[i], w_gate_up[i],
                        k_k[i], k_a[i], r_k[i], gn_g[i], gn_b[i])
        y_b = gmlp_mix(proj, gmlp_ln_g[i], gmlp_ln_b[i], w_spatial[i], b_spatial[i])
        mix = jnp.concatenate([y_a, y_b], -1) @ w_out[i]
        x = layer_norm(ALPHA * x + mix, ln1_g[i], ln1_b[i])
        ffn = hmoe(x, w_group_router[i], b_group_router[i], w_expert_router[i], b_expert_router[i],
                   w_exp_gate[i], w_exp_up[i], w_exp_down[i])
        ple = jax.nn.sigmoid(x @ w_ple_gate[i] + b_ple_gate[i]) * (p[i] @ w_ple_proj[i])
        x = layer_norm(ALPHA * x + ffn + ple, ln2_g[i], ln2_b[i])
    return x
```

```python
import functools
import math

import jax
import jax.numpy as jnp
from jax import lax
from jax.experimental import pallas as pl
from jax.experimental.pallas import tpu as pltpu

F32 = jnp.float32
BF16 = jnp.bfloat16

D_MODEL = 1024
D_RWKV = 512
HEAD = 64
D_GMLP = 512
GMLP_GROUPS = 4
GROUP_W = 128
GCHUNK = 128
DECAY_LORA = 64
ICLR_LORA = 64
GATE_LORA = 128
N_SHIFT = 3 * D_RWKV + DECAY_LORA + ICLR_LORA + GATE_LORA
D_IN = N_SHIFT + 2 * D_GMLP
D_PLE = 256
N_GROUPS = 4
EXPERTS_PER_GROUP = 8
N_EXPERTS = 32
TOP_K = 2
D_EXPERT = 512
DEPTH = 1
ALPHA = (2.0 * DEPTH) ** 0.25
LN_EPS = 1e-5
GN_EPS = 64e-5
DECAY_SCALE = math.exp(-0.5)

LANES = 128
WKV_CHUNK = 64
N_PAIRS = D_RWKV // LANES
VMEM_LIMIT = 56 * 1024 * 1024

PREP_TM = 256
WKV_TB = 256
MIX_TM = 256
SLOT_TM = 512
EXPERT_ROWS = 256
DISPATCH_TM = 512
COMBINE_TM = 256
NEG = -1e30


def _dot(a, b):
    return jnp.dot(a.astype(BF16), b.astype(BF16), preferred_element_type=F32)


def _dot_nt(a, b):
    return lax.dot_general(a.astype(BF16), b.astype(BF16), (((1,), (1,)), ((), ())),
                           preferred_element_type=F32)


def _split3(x):
    hi = x.astype(BF16)
    r1 = x - hi.astype(F32)
    mid = r1.astype(BF16)
    lo = (r1 - mid.astype(F32)).astype(BF16)
    return hi, mid, lo


def _dot3_lhs(x, w):
    hi, mid, lo = _split3(x)
    w = w.astype(BF16)
    return (jnp.dot(hi, w, preferred_element_type=F32) + jnp.dot(mid, w, preferred_element_type=F32)
            + jnp.dot(lo, w, preferred_element_type=F32))


def _dot3_rhs(w, x):
    hi, mid, lo = _split3(x)
    w = w.astype(BF16)
    return (jnp.dot(w, hi, preferred_element_type=F32) + jnp.dot(w, mid, preferred_element_type=F32)
            + jnp.dot(w, lo, preferred_element_type=F32))


def _layer_norm(x, g, b, eps):
    mu = jnp.mean(x, axis=-1, keepdims=True)
    xc = x - mu
    var = jnp.mean(xc * xc, axis=-1, keepdims=True)
    return xc * lax.rsqrt(var + eps) * g + b


def _sigmoid(x):
    return 1.0 / (1.0 + jnp.exp(-x))


def _iota(shape, dim):
    return lax.broadcasted_iota(jnp.int32, shape, dim)


def _prep_kernel(x_ref, lng_ref, lnb_ref, win_ref, mu_ref, wwa_ref, w0a0_ref, wg_ref, kk_ref, ka_ref, rk_ref,
                 eones_ref, glng_ref, glnb_ref, wsp_ref, bsp_ref,
                 r_ref, lw_ref, k_ref, v_ref, a_ref, b_ref, g_ref, bonus_ref, yb_ref, carry_ref):
    tm = x_ref.shape[1]

    @pl.when(pl.program_id(1) == 0)
    def _():
        carry_ref[...] = jnp.zeros_like(carry_ref)

    x0 = _layer_norm(x_ref[0], lng_ref[...], lnb_ref[...], LN_EPS)
    proj = jnp.dot(x0.astype(BF16), win_ref[...], preferred_element_type=F32)

    h = proj[:, :N_SHIFT]
    rolled = pltpu.roll(h, 1, 0)
    first = _iota((tm, N_SHIFT), 0) == 0
    prev = jnp.where(first, jnp.broadcast_to(carry_ref[0:1, :], (tm, N_SHIFT)), rolled)
    carry_ref[0:1, :] = h[tm - 1:tm, :]
    h = h + (prev - h) * mu_ref[...]

    r = h[:, 0:D_RWKV]
    k = h[:, D_RWKV:2 * D_RWKV]
    v = h[:, 2 * D_RWKV:3 * D_RWKV]
    xwa = h[:, 3 * D_RWKV:3 * D_RWKV + LANES]
    xg = h[:, 3 * D_RWKV + LANES:N_SHIFT]

    lane = _iota((tm, LANES), 1)
    twa = jnp.where(lane < DECAY_LORA, jnp.tanh(xwa), xwa)
    da = _dot(twa, wwa_ref[...]) + w0a0_ref[...]
    logw = -DECAY_SCALE * _sigmoid(da[:, :D_RWKV])
    ag = _sigmoid(da[:, D_RWKV:])
    g = _dot(_sigmoid(xg), wg_ref[...])

    eones = eones_ref[...]
    kk = k * kk_ref[...]
    n2 = _dot3_lhs(kk * kk, eones)
    kk = kk / jnp.maximum(jnp.sqrt(n2), 1e-12)
    k = k * (1.0 + (ag - 1.0) * ka_ref[...])
    bonus = _dot3_lhs(r * k * rk_ref[...], eones) * v

    for p in range(N_PAIRS):
        sl = slice(p * LANES, (p + 1) * LANES)
        r_ref[0, p] = r[:, sl]
        lw_ref[0, p] = logw[:, sl]
        k_ref[0, p] = k[:, sl]
        v_ref[0, p] = v[:, sl]
        a_ref[0, p] = -kk[:, sl]
        b_ref[0, p] = (kk * ag)[:, sl]
        g_ref[0, p] = g[:, sl]
        bonus_ref[0, p] = bonus[:, sl]

    zin = proj[:, N_SHIFT:]
    z = 0.5 * zin * (1.0 + lax.erf(zin * (0.5 ** 0.5)))
    zu = z[:, :D_GMLP]
    zv = z[:, D_GMLP:]
    causal = _iota((GCHUNK, GCHUNK), 0) >= _iota((GCHUNK, GCHUNK), 1)
    for gi in range(GMLP_GROUPS):
        gs = slice(gi * GROUP_W, (gi + 1) * GROUP_W)
        zvn = _layer_norm(zv[:, gs], glng_ref[:, gs], glnb_ref[:, gs], LN_EPS)
        ws = jnp.where(causal, wsp_ref[gi], 0.0).astype(BF16)
        bcol = bsp_ref[:, gi:gi + 1]
        for c in range(tm // GCHUNK):
            ts = slice(c * GCHUNK, (c + 1) * GCHUNK)
            mixed = jnp.dot(ws, zvn[ts].astype(BF16), preferred_element_type=F32) + bcol
            yb_ref[0, ts, gs] = (zu[ts, gs] * mixed).astype(BF16)


def _prep(x, ln_g, ln_b, w_in, mu, wwa, w0a0, wg, k_k, k_a, r_k, eones, glng, glnb, wsp, bsp):
    B, S, _ = x.shape
    tm = PREP_TM
    const = lambda shape: pl.BlockSpec(shape, lambda b, s: (0,) * len(shape))
    pair_spec = pl.BlockSpec((1, N_PAIRS, tm, LANES), lambda b, s: (b, 0, s, 0))
    pair_shape = jax.ShapeDtypeStruct((B, N_PAIRS, S, LANES), F32)
    return pl.pallas_call(
        _prep_kernel,
        grid=(B, S // tm),
        in_specs=[
            pl.BlockSpec((1, tm, D_MODEL), lambda b, s: (b, s, 0)),
            const((1, D_MODEL)), const((1, D_MODEL)), const((D_MODEL, D_IN)), const((1, N_SHIFT)),
            const((LANES, 2 * D_RWKV)), const((1, 2 * D_RWKV)), const((GATE_LORA, D_RWKV)),
            const((1, D_RWKV)), const((1, D_RWKV)), const((1, D_RWKV)), const((D_RWKV, D_RWKV)),
            const((1, D_GMLP)), const((1, D_GMLP)), const((GMLP_GROUPS, GCHUNK, GCHUNK)),
            const((GCHUNK, GMLP_GROUPS)),
        ],
        out_specs=[pair_spec] * 8 + [pl.BlockSpec((1, tm, D_GMLP), lambda b, s: (b, s, 0))],
        out_shape=[pair_shape] * 8 + [jax.ShapeDtypeStruct((B, S, D_GMLP), BF16)],
        scratch_shapes=[pltpu.VMEM((8, N_SHIFT), F32)],
        compiler_params=pltpu.CompilerParams(dimension_semantics=("arbitrary", "arbitrary"),
                                             vmem_limit_bytes=VMEM_LIMIT),
        name="prep",
    )(x, ln_g, ln_b, w_in, mu, wwa, w0a0, wg, k_k, k_a, r_k, eones, glng, glnb, wsp, bsp)


def _wkv_kernel(r_ref, lw_ref, k_ref, v_ref, a_ref, b_ref, g_ref, bonus_ref, gng_ref, gnb_ref, emean_ref,
                o_ref, h_ref):
    C = WKV_CHUNK
    tb = r_ref.shape[2]

    @pl.when(pl.program_id(2) == 0)
    def _():
        h_ref[...] = jnp.zeros_like(h_ref)

    lane = _iota((C, LANES), 1)
    m0 = (lane < HEAD).astype(F32)
    m1 = 1.0 - m0
    rr = _iota((LANES, LANES), 0)
    cc = _iota((LANES, LANES), 1)
    strict = (rr % C) > (cc % C)
    incl = (rr % C) >= (cc % C)
    eye = (rr == cc).astype(F32)
    ltri = (_iota((C, C), 0) >= _iota((C, C), 1)).astype(BF16)

    def stack(x):
        return jnp.concatenate([x * m0, x * m1], axis=0)

    H = h_ref[...]
    ys = []
    for c in range(tb // C):
        ts = slice(c * C, (c + 1) * C)
        r = r_ref[0, 0, ts, :]
        lw = lw_ref[0, 0, ts, :]
        k = k_ref[0, 0, ts, :]
        v = v_ref[0, 0, ts, :]
        a = a_ref[0, 0, ts, :]
        b = b_ref[0, 0, ts, :]

        cum = _dot3_rhs(ltri, lw)
        cend = cum[C - 1:C, :]
        gfw = jnp.exp(cum)
        ginv = jnp.exp(-cum)
        a_st = stack(a * jnp.exp(cum - lw))
        r_st = stack(r * gfw)
        b_st = stack(b * ginv)
        k_st = stack(k * ginv)
        v_st = stack(v)
        bk_end = jnp.concatenate([stack(b * jnp.exp(cend - cum)), stack(k * jnp.exp(cend - cum))], axis=0)

        G = _dot_nt(jnp.concatenate([a_st, r_st], axis=0), jnp.concatenate([b_st, k_st], axis=0))
        n1 = jnp.where(strict, G[:LANES, :LANES], 0.0)
        aak = jnp.where(strict, G[:LANES, LANES:], 0.0)
        arb = jnp.where(incl, G[LANES:, :LANES], 0.0)
        ark = jnp.where(incl, G[LANES:, LANES:], 0.0)

        n2 = _dot(n1, n1)
        x = _dot(n2, jnp.concatenate([n1, n2], axis=1))
        t = eye + n1 + n2 + x[:, :LANES]
        npow = x[:, LANES:]
        for _ in range(3):
            x = _dot(npow, jnp.concatenate([t, npow], axis=1))
            t = t + x[:, :LANES]
            npow = x[:, LANES:]
        t = t + _dot(npow, t)

        av = _dot(jnp.concatenate([aak, ark], axis=0), v_st)
        x = _dot(t, jnp.concatenate([a_st, av[:LANES]], axis=1))
        z = _dot(arb, x)
        rp = r_st + z[:, :LANES]
        p3 = z[:, LANES:] + av[LANES:]
        rhs = jnp.concatenate([x, jnp.concatenate([jnp.zeros_like(v_st), v_st], axis=1)], axis=0)
        mq = _dot(bk_end.T, rhs)
        m = eye * jnp.exp(cend) + mq[:, :LANES]

        o_st = _dot(rp, H) + p3
        ys.append(o_st[:C] + o_st[C:])
        H = _dot(m, H) + mq[:, LANES:]

    h_ref[...] = H
    y = jnp.concatenate(ys, axis=0)
    emean = emean_ref[...]
    mu = _dot3_lhs(y, emean)
    yc = y - mu
    var = _dot3_lhs(yc * yc, emean)
    yn = yc * lax.rsqrt(var + GN_EPS) * gng_ref[0] + gnb_ref[0]
    o_ref[0, 0] = ((yn + bonus_ref[0, 0]) * g_ref[0, 0]).astype(BF16)


def _wkv(r, lw, k, v, a, b, g, bonus, gn_g, gn_b, emean):
    B, P, S, _ = r.shape
    tb = WKV_TB
    seq = pl.BlockSpec((1, 1, tb, LANES), lambda bi, p, s: (bi, p, s, 0))
    par = pl.BlockSpec((1, 1, LANES), lambda bi, p, s: (p, 0, 0))
    return pl.pallas_call(
        _wkv_kernel,
        grid=(B, P, S // tb),
        in_specs=[seq] * 8 + [par, par, pl.BlockSpec((LANES, LANES), lambda bi, p, s: (0, 0))],
        out_specs=seq,
        out_shape=jax.ShapeDtypeStruct((B, P, S, LANES), BF16),
        scratch_shapes=[pltpu.VMEM((LANES, LANES), F32)],
        compiler_params=pltpu.CompilerParams(dimension_semantics=("arbitrary", "arbitrary", "arbitrary"),
                                             vmem_limit_bytes=VMEM_LIMIT),
        name="wkv",
    )(r, lw, k, v, a, b, g, bonus, gn_g, gn_b, emean)


def _mixer_kernel(x_ref, lng_ref, lnb_ref, ya_ref, yb_ref, wout_ref, l1g_ref, l1b_ref, wr_ref, br_ref,
                  p_ref, wpg_ref, bpg_ref, wpp_ref, base_ref, x1_ref, route_ref):
    tm = x_ref.shape[1]
    x0 = _layer_norm(x_ref[0], lng_ref[...], lnb_ref[...], LN_EPS)
    ymix = jnp.concatenate([ya_ref[0, p] for p in range(N_PAIRS)] + [yb_ref[0]], axis=-1)
    mix = jnp.dot(ymix, wout_ref[...], preferred_element_type=F32)
    x1 = _layer_norm(ALPHA * x0 + mix, l1g_ref[...], l1b_ref[...], LN_EPS)
    x1_ref[0] = x1

    hi, mid, lo = _split3(x1)
    whi = wr_ref[0]
    wmid = wr_ref[1]
    wlo = wr_ref[2]
    d = lambda u, w: jnp.dot(u, w, preferred_element_type=F32)
    logits = (d(hi, whi) + d(hi, wmid) + d(mid, whi) + d(hi, wlo) + d(mid, wmid) + d(lo, whi)) + br_ref[...]
    lane = _iota((tm, LANES), 1).astype(F32)
    far = float(4 * LANES)
    is_g = jnp.where(lane >= N_EXPERTS, jnp.where(lane < N_EXPERTS + N_GROUPS, 1.0, 0.0), 0.0) > 0.5
    gl = jnp.where(is_g, logits, NEG)
    gmax = jnp.max(gl, axis=-1, keepdims=True)
    gsel = jnp.min(jnp.where(gl == gmax, lane, far), axis=-1, keepdims=True) - N_EXPERTS
    p_group = 1.0 / jnp.sum(jnp.where(is_g, jnp.exp(gl - gmax), 0.0), axis=-1, keepdims=True)
    grp_of_lane = jnp.floor(lane * (1.0 / EXPERTS_PER_GROUP))
    el = jnp.where(grp_of_lane == gsel, logits, NEG)
    v1 = jnp.max(el, axis=-1, keepdims=True)
    i1 = jnp.min(jnp.where(el == v1, lane, far), axis=-1, keepdims=True)
    el2 = jnp.where(lane == i1, NEG, el)
    v2 = jnp.max(el2, axis=-1, keepdims=True)
    i2 = jnp.min(jnp.where(el2 == v2, lane, far), axis=-1, keepdims=True)
    e21 = jnp.exp(v2 - v1)
    w1 = p_group / (1.0 + e21)
    w2 = p_group * e21 / (1.0 + e21)
    route_ref[0] = jnp.where(lane == 0, i1, jnp.where(lane == 1, i2, jnp.where(lane == 2, w1,
                                                                               jnp.where(lane == 3, w2, 0.0))))

    gate = _sigmoid(jnp.dot(x1.astype(BF16), wpg_ref[...], preferred_element_type=F32) + bpg_ref[...])
    ple = gate * jnp.dot(p_ref[0].astype(BF16), wpp_ref[...], preferred_element_type=F32)
    base_ref[0] = ALPHA * x1 + ple


def _mixer(x, ln_g, ln_b, ya, yb, w_out, l1g, l1b, wr3, br, p, wpg, bpg, wpp):
    B, S, _ = x.shape
    tm = MIX_TM
    const = lambda shape: pl.BlockSpec(shape, lambda b, s: (0,) * len(shape))
    row = lambda w: pl.BlockSpec((1, tm, w), lambda b, s: (b, s, 0))
    return pl.pallas_call(
        _mixer_kernel,
        grid=(B, S // tm),
        in_specs=[
            row(D_MODEL), const((1, D_MODEL)), const((1, D_MODEL)),
            pl.BlockSpec((1, N_PAIRS, tm, LANES), lambda b, s: (b, 0, s, 0)), row(D_GMLP),
            const((D_MODEL, D_MODEL)), const((1, D_MODEL)), const((1, D_MODEL)),
            const((3, D_MODEL, LANES)), const((1, LANES)),
            row(D_PLE), const((D_MODEL, D_MODEL)), const((1, D_MODEL)), const((D_PLE, D_MODEL)),
        ],
        out_specs=[row(D_MODEL), row(D_MODEL), row(LANES)],
        out_shape=[jax.ShapeDtypeStruct((B, S, D_MODEL), F32), jax.ShapeDtypeStruct((B, S, D_MODEL), F32),
                   jax.ShapeDtypeStruct((B, S, LANES), F32)],
        compiler_params=pltpu.CompilerParams(dimension_semantics=("arbitrary", "arbitrary"),
                                             vmem_limit_bytes=VMEM_LIMIT),
        name="mixer",
    )(x, ln_g, ln_b, ya, yb, w_out, l1g, l1b, wr3, br, p, wpg, bpg, wpp)


def _slots_kernel(route_ref, dest_ref, pend_ref, carry_ref, rank_ref):
    ph = pl.program_id(0)
    i = pl.program_id(1)
    tm = route_ref.shape[0]
    lane = _iota((tm, LANES), 1)
    route = route_ref[...]
    e1 = route[:, 0:1].astype(jnp.int32)
    e2 = route[:, 1:2].astype(jnp.int32)
    oh1 = lane == e1
    oh2 = lane == e2

    @pl.when((ph == 0) & (i == 0))
    def _():
        carry_ref[...] = jnp.zeros_like(carry_ref)

    @pl.when(ph == 0)
    def _():
        below = (_iota((tm, tm), 0) > _iota((tm, tm), 1)).astype(BF16)
        o1 = oh1.astype(BF16)
        o2 = oh2.astype(BF16)
        c1 = jnp.dot(below, o1, preferred_element_type=F32)
        c2 = jnp.dot(below, o2, preferred_element_type=F32)
        tot1 = jnp.sum(o1.astype(F32), axis=0, keepdims=True)
        tot2 = jnp.sum(o2.astype(F32), axis=0, keepdims=True)
        carry = carry_ref[0:1, :]
        rank1 = jnp.sum(jnp.where(oh1, c1 + carry, 0.0), axis=-1, keepdims=True)
        rank2 = jnp.sum(jnp.where(oh2, c2 + carry + tot1, 0.0), axis=-1, keepdims=True)
        rank_ref[i] = jnp.where(lane == 0, rank1, jnp.where(lane == 1, rank2, 0.0))
        carry_ref[0:1, :] = carry + tot1 + tot2
        dest_ref[...] = jnp.zeros_like(dest_ref)
        pend_ref[...] = jnp.zeros_like(pend_ref)

    @pl.when(ph == 1)
    def _():
        counts = carry_ref[0:1, :]
        padded = jnp.floor((counts + (EXPERT_ROWS - 1)) * (1.0 / EXPERT_ROWS)) * EXPERT_ROWS
        upper = (_iota((LANES, LANES), 0) <= _iota((LANES, LANES), 1)).astype(BF16)
        pend = _dot3_lhs(jnp.broadcast_to(padded, (8, LANES)), upper)[0:1, :]
        pstart = pend - padded
        rank = rank_ref[i]
        d1 = jnp.sum(jnp.where(oh1, pstart, 0.0), axis=-1, keepdims=True) + rank[:, 0:1]
        d2 = jnp.sum(jnp.where(oh2, pstart, 0.0), axis=-1, keepdims=True) + rank[:, 1:2]
        dest_ref[...] = jnp.where(lane == 0, d1, jnp.where(lane == 1, d2, 0.0)).astype(jnp.int32)
        pend_ref[...] = jnp.broadcast_to(pend, (8, LANES)).astype(jnp.int32)


def _slots(route):
    T = route.shape[0]
    tm = SLOT_TM
    nt = T // tm
    return pl.pallas_call(
        _slots_kernel,
        grid=(2, nt),
        in_specs=[pl.BlockSpec((tm, LANES), lambda ph, i: (i, 0))],
        out_specs=[pl.BlockSpec((tm, LANES), lambda ph, i: (i * ph, 0)),
                   pl.BlockSpec((8, LANES), lambda ph, i: (0, 0))],
        out_shape=[jax.ShapeDtypeStruct((T, LANES), jnp.int32), jax.ShapeDtypeStruct((8, LANES), jnp.int32)],
        scratch_shapes=[pltpu.VMEM((8, LANES), F32), pltpu.VMEM((nt, tm, LANES), F32)],
        compiler_params=pltpu.CompilerParams(dimension_semantics=("arbitrary", "arbitrary"),
                                             vmem_limit_bytes=VMEM_LIMIT),
        name="slots",
    )(route)


def _dispatch_kernel(dest_ref, x_ref, xs_in_ref, xs_ref, sem):
    del xs_in_ref
    tm = dest_ref.shape[0] // TOP_K
    base = pl.program_id(0) * tm

    def copy(t, j):
        return pltpu.make_async_copy(x_ref.at[pl.ds(base + t, 1)], xs_ref.at[pl.ds(dest_ref[TOP_K * t + j], 1)], sem)

    def start(t, _):
        copy(t, 0).start()
        copy(t, 1).start()
        return 0

    def wait(t, _):
        copy(t, 0).wait()
        copy(t, 1).wait()
        return 0

    lax.fori_loop(0, tm, start, 0)
    lax.fori_loop(0, tm, wait, 0)


def _dispatch(dest_flat, x1, n_rows):
    T = x1.shape[0]
    tm = DISPATCH_TM
    xs0 = jnp.zeros((n_rows, D_MODEL), F32)
    return pl.pallas_call(
        _dispatch_kernel,
        grid=(T // tm,),
        in_specs=[pl.BlockSpec((TOP_K * tm,), lambda i: (i,), memory_space=pltpu.SMEM),
                  pl.BlockSpec(memory_space=pl.ANY), pl.BlockSpec(memory_space=pl.ANY)],
        out_specs=pl.BlockSpec(memory_space=pl.ANY),
        out_shape=jax.ShapeDtypeStruct((n_rows, D_MODEL), F32),
        scratch_shapes=[pltpu.SemaphoreType.DMA],
        input_output_aliases={2: 0},
        compiler_params=pltpu.CompilerParams(dimension_semantics=("arbitrary",)),
        name="dispatch",
    )(dest_flat, x1, xs0)


def _block_expert(pend_ref, j):
    e = jnp.int32(0)
    for i in range(N_EXPERTS):
        e = e + (pend_ref[i] <= j * EXPERT_ROWS).astype(jnp.int32)
    return jnp.minimum(e, N_EXPERTS - 1)


def _experts_kernel(pend_ref, xs_ref, wg_ref, wu_ref, wd_ref, y_ref):
    j = pl.program_id(0)
    used = j * EXPERT_ROWS < pend_ref[N_EXPERTS - 1]

    @pl.when(used)
    def _():
        xb = xs_ref[...].astype(BF16)
        hg = jnp.dot(xb, wg_ref[0], preferred_element_type=F32)
        hu = jnp.dot(xb, wu_ref[0], preferred_element_type=F32)
        hid = hg * _sigmoid(hg) * hu
        y_ref[...] = jnp.dot(hid.astype(BF16), wd_ref[0], preferred_element_type=F32)

    @pl.when(jnp.logical_not(used))
    def _():
        y_ref[...] = jnp.zeros_like(y_ref)


def _experts(pend, xs, wg, wu, wd):
    n_rows = xs.shape[0]
    rows = EXPERT_ROWS
    wspec = lambda shape: pl.BlockSpec((1,) + shape, lambda j, pe: (_block_expert(pe, j), 0, 0))
    return pl.pallas_call(
        _experts_kernel,
        grid_spec=pltpu.PrefetchScalarGridSpec(
            num_scalar_prefetch=1,
            grid=(n_rows // rows,),
            in_specs=[pl.BlockSpec((rows, D_MODEL), lambda j, pe: (j, 0)),
                      wspec((D_MODEL, D_EXPERT)), wspec((D_MODEL, D_EXPERT)), wspec((D_EXPERT, D_MODEL))],
            out_specs=pl.BlockSpec((rows, D_MODEL), lambda j, pe: (j, 0)),
        ),
        out_shape=jax.ShapeDtypeStruct((n_rows, D_MODEL), F32),
        compiler_params=pltpu.CompilerParams(dimension_semantics=("arbitrary",), vmem_limit_bytes=VMEM_LIMIT),
        name="experts",
    )(pend, xs, wg, wu, wd)


def _combine_kernel(dest_ref, ys_ref, base_ref, route_ref, lg_ref, lb_ref, o_ref, buf_ref, sem):
    tm = base_ref.shape[0]

    def copy(t, j):
        return pltpu.make_async_copy(ys_ref.at[pl.ds(dest_ref[TOP_K * t + j], 1)], buf_ref.at[j, pl.ds(t, 1)], sem)

    def start(t, _):
        copy(t, 0).start()
        copy(t, 1).start()
        return 0

    def wait(t, _):
        copy(t, 0).wait()
        copy(t, 1).wait()
        return 0

    lax.fori_loop(0, tm, start, 0)
    lax.fori_loop(0, tm, wait, 0)
    route = route_ref[...]
    ffn = buf_ref[0] * route[:, 2:3] + buf_ref[1] * route[:, 3:4]
    o_ref[...] = _layer_norm(base_ref[...] + ffn, lg_ref[...], lb_ref[...], LN_EPS)


def _combine(dest_flat, ys, base, route, l2g, l2b):
    T = base.shape[0]
    tm = COMBINE_TM
    return pl.pallas_call(
        _combine_kernel,
        grid=(T // tm,),
        in_specs=[pl.BlockSpec((TOP_K * tm,), lambda i: (i,), memory_space=pltpu.SMEM),
                  pl.BlockSpec(memory_space=pl.ANY),
                  pl.BlockSpec((tm, D_MODEL), lambda i: (i, 0)), pl.BlockSpec((tm, LANES), lambda i: (i, 0)),
                  pl.BlockSpec((1, D_MODEL), lambda i: (0, 0)), pl.BlockSpec((1, D_MODEL), lambda i: (0, 0))],
        out_specs=pl.BlockSpec((tm, D_MODEL), lambda i: (i, 0)),
        out_shape=jax.ShapeDtypeStruct((T, D_MODEL), F32),
        scratch_shapes=[pltpu.VMEM((TOP_K, tm, D_MODEL), F32), pltpu.SemaphoreType.DMA],
        compiler_params=pltpu.CompilerParams(dimension_semantics=("arbitrary",), vmem_limit_bytes=VMEM_LIMIT),
        name="combine",
    )(dest_flat, ys, base, route, l2g, l2b)


def _block_diag_const(n, blk, val):
    idx = jnp.arange(n) // blk
    return jnp.where(idx[:, None] == idx[None, :], val, 0.0).astype(BF16)


def kernel(x, p, ln_emb_g, ln_emb_b, w_in, mu_shift, w0, w_decay_up, a0, w_iclr_up, w_gate_up, k_k, k_a, r_k, gn_g, gn_b, gmlp_ln_g, gmlp_ln_b, w_spatial, b_spatial, w_out, ln1_g, ln1_b, w_group_router, b_group_router, w_expert_router, b_expert_router, w_exp_gate, w_exp_up, w_exp_down, w_ple_gate, b_ple_gate, w_ple_proj, ln2_g, ln2_b):
    B, S, D = x.shape
    T = B * S
    row = lambda t: t.reshape(1, -1).astype(F32)

    zl = jnp.zeros((DECAY_LORA, D_RWKV), F32)
    wwa = jnp.concatenate([jnp.concatenate([w_decay_up[0], zl], axis=1),
                           jnp.concatenate([zl, w_iclr_up[0]], axis=1)], axis=0).astype(BF16)
    w0a0 = jnp.concatenate([w0[0], a0[0]]).reshape(1, -1)
    eones = _block_diag_const(D_RWKV, HEAD, 1.0)
    emean = _block_diag_const(LANES, HEAD, 1.0 / HEAD)

    r, lw, k, v, a, b, g, bonus, yb = _prep(
        x, row(ln_emb_g), row(ln_emb_b), w_in[0].astype(BF16), row(mu_shift[0]), wwa, w0a0,
        w_gate_up[0].astype(BF16), row(k_k[0]), row(k_a[0]), row(r_k[0]), eones,
        row(gmlp_ln_g[0]), row(gmlp_ln_b[0]), w_spatial[0], b_spatial[0].T)

    ya = _wkv(r, lw, k, v, a, b, g, bonus, gn_g[0].reshape(N_PAIRS, 1, LANES), gn_b[0].reshape(N_PAIRS, 1, LANES),
              emean)

    wr = jnp.concatenate([w_expert_router[0].reshape(D, N_EXPERTS), w_group_router[0],
                          jnp.zeros((D, LANES - N_EXPERTS - N_GROUPS), F32)], axis=1)
    wr3 = jnp.stack(_split3(wr))
    br = jnp.concatenate([b_expert_router[0].reshape(-1), b_group_router[0],
                          jnp.zeros((LANES - N_EXPERTS - N_GROUPS,), F32)]).reshape(1, LANES)
    base, x1, route = _mixer(x, row(ln_emb_g), row(ln_emb_b), ya, yb, w_out[0].astype(BF16), row(ln1_g[0]),
                             row(ln1_b[0]), wr3, br, p[0], w_ple_gate[0].astype(BF16), row(b_ple_gate[0]),
                             w_ple_proj[0].astype(BF16))
    base = base.reshape(T, D)
    x1 = x1.reshape(T, D)
    route = route.reshape(T, LANES)

    dest, pend = _slots(route)
    dest_flat = dest[:, :TOP_K].reshape(T * TOP_K)
    pend = pend[0, :N_EXPERTS]

    n_blocks = -(-(T * TOP_K) // EXPERT_ROWS) + N_EXPERTS
    xs = _dispatch(dest_flat, x1, n_blocks * EXPERT_ROWS)
    ys = _experts(pend, xs, w_exp_gate[0].astype(BF16), w_exp_up[0].astype(BF16), w_exp_down[0].astype(BF16))
    out = _combine(dest_flat, ys, base, route, row(ln2_g[0]), row(ln2_b[0]))
    return out.reshape(B, S, D)
```

```python
import functools
import math

import jax
import jax.numpy as jnp
from jax import lax
from jax.experimental import pallas as pl
from jax.experimental.pallas import tpu as pltpu

F32 = jnp.float32
BF16 = jnp.bfloat16

D_MODEL = 1024
D_RWKV = 512
HEAD = 64
D_GMLP = 512
GMLP_GROUPS = 4
GROUP_W = 128
GCHUNK = 128
DECAY_LORA = 64
ICLR_LORA = 64
GATE_LORA = 128
N_SHIFT = 3 * D_RWKV + DECAY_LORA + ICLR_LORA + GATE_LORA
D_IN = N_SHIFT + 2 * D_GMLP
D_PLE = 256
N_GROUPS = 4
EXPERTS_PER_GROUP = 8
N_EXPERTS = 32
TOP_K = 2
D_EXPERT = 512
DEPTH = 1
ALPHA = (2.0 * DEPTH) ** 0.25
LN_EPS = 1e-5
GN_EPS = 64e-5
DECAY_SCALE = math.exp(-0.5)

LANES = 128
WKV_CHUNK = 64
N_PAIRS = D_RWKV // LANES
VMEM_LIMIT = 56 * 1024 * 1024

PREP_TM = 256
WKV_TB = 256
WKV_PAIRS = 4
MIX_TM = 256
SLOT_TM = 512
EXPERT_ROWS = 256
DISPATCH_TM = 512
COMBINE_TM = 256
NEG = -1e30


def _dot(a, b):
    return jnp.dot(a.astype(BF16), b.astype(BF16), preferred_element_type=F32)


def _dot_nt(a, b):
    return lax.dot_general(a.astype(BF16), b.astype(BF16), (((1,), (1,)), ((), ())),
                           preferred_element_type=F32)


def _split3(x):
    hi = x.astype(BF16)
    r1 = x - hi.astype(F32)
    mid = r1.astype(BF16)
    lo = (r1 - mid.astype(F32)).astype(BF16)
    return hi, mid, lo


def _dot3_lhs(x, w):
    hi, mid, lo = _split3(x)
    w = w.astype(BF16)
    return (jnp.dot(hi, w, preferred_element_type=F32) + jnp.dot(mid, w, preferred_element_type=F32)
            + jnp.dot(lo, w, preferred_element_type=F32))


def _dot3_rhs(w, x):
    hi, mid, lo = _split3(x)
    w = w.astype(BF16)
    return (jnp.dot(w, hi, preferred_element_type=F32) + jnp.dot(w, mid, preferred_element_type=F32)
            + jnp.dot(w, lo, preferred_element_type=F32))


def _layer_norm(x, g, b, eps):
    mu = jnp.mean(x, axis=-1, keepdims=True)
    xc = x - mu
    var = jnp.mean(xc * xc, axis=-1, keepdims=True)
    return xc * lax.rsqrt(var + eps) * g + b


def _sigmoid(x):
    return 1.0 / (1.0 + jnp.exp(-x))


def _iota(shape, dim):
    return lax.broadcasted_iota(jnp.int32, shape, dim)


def _prep_kernel(x_ref, lng_ref, lnb_ref, win_ref, mu_ref, wwa_ref, w0a0_ref, wg_ref, kk_ref, ka_ref, rk_ref,
                 eones_ref, glng_ref, glnb_ref, wsp_ref, bsp_ref,
                 r_ref, lw_ref, k_ref, v_ref, a_ref, b_ref, g_ref, bonus_ref, yb_ref, carry_ref):
    tm = x_ref.shape[1]

    @pl.when(pl.program_id(1) == 0)
    def _():
        carry_ref[...] = jnp.zeros_like(carry_ref)

    x0 = _layer_norm(x_ref[0], lng_ref[...], lnb_ref[...], LN_EPS)
    proj = jnp.dot(x0.astype(BF16), win_ref[...], preferred_element_type=F32)

    h = proj[:, :N_SHIFT]
    rolled = pltpu.roll(h, 1, 0)
    first = _iota((tm, N_SHIFT), 0) == 0
    prev = jnp.where(first, jnp.broadcast_to(carry_ref[0:1, :], (tm, N_SHIFT)), rolled)
    carry_ref[0:1, :] = h[tm - 1:tm, :]
    h = h + (prev - h) * mu_ref[...]

    r = h[:, 0:D_RWKV]
    k = h[:, D_RWKV:2 * D_RWKV]
    v = h[:, 2 * D_RWKV:3 * D_RWKV]
    xwa = h[:, 3 * D_RWKV:3 * D_RWKV + LANES]
    xg = h[:, 3 * D_RWKV + LANES:N_SHIFT]

    lane = _iota((tm, LANES), 1)
    twa = jnp.where(lane < DECAY_LORA, jnp.tanh(xwa), xwa)
    da = _dot(twa, wwa_ref[...]) + w0a0_ref[...]
    logw = -DECAY_SCALE * _sigmoid(da[:, :D_RWKV])
    ag = _sigmoid(da[:, D_RWKV:])
    g = _dot(_sigmoid(xg), wg_ref[...])

    eones = eones_ref[...]
    kk = k * kk_ref[...]
    n2 = _dot3_lhs(kk * kk, eones)
    kk = kk / jnp.maximum(jnp.sqrt(n2), 1e-12)
    k = k * (1.0 + (ag - 1.0) * ka_ref[...])
    bonus = _dot3_lhs(r * k * rk_ref[...], eones) * v

    for p in range(N_PAIRS):
        sl = slice(p * LANES, (p + 1) * LANES)
        r_ref[0, p] = r[:, sl]
        lw_ref[0, p] = logw[:, sl]
        k_ref[0, p] = k[:, sl]
        v_ref[0, p] = v[:, sl]
        a_ref[0, p] = -kk[:, sl]
        b_ref[0, p] = (kk * ag)[:, sl]
        g_ref[0, p] = g[:, sl]
        bonus_ref[0, p] = bonus[:, sl]

    zin = proj[:, N_SHIFT:]
    z = 0.5 * zin * (1.0 + lax.erf(zin * (0.5 ** 0.5)))
    zu = z[:, :D_GMLP]
    zv = z[:, D_GMLP:]
    causal = _iota((GCHUNK, GCHUNK), 0) >= _iota((GCHUNK, GCHUNK), 1)
    for gi in range(GMLP_GROUPS):
        gs = slice(gi * GROUP_W, (gi + 1) * GROUP_W)
        zvn = _layer_norm(zv[:, gs], glng_ref[:, gs], glnb_ref[:, gs], LN_EPS)
        ws = jnp.where(causal, wsp_ref[gi], 0.0).astype(BF16)
        bcol = bsp_ref[:, gi:gi + 1]
        for c in range(tm // GCHUNK):
            ts = slice(c * GCHUNK, (c + 1) * GCHUNK)
            mixed = jnp.dot(ws, zvn[ts].astype(BF16), preferred_element_type=F32) + bcol
            yb_ref[0, ts, gs] = (zu[ts, gs] * mixed).astype(BF16)


def _prep(x, ln_g, ln_b, w_in, mu, wwa, w0a0, wg, k_k, k_a, r_k, eones, glng, glnb, wsp, bsp):
    B, S, _ = x.shape
    tm = PREP_TM
    const = lambda shape: pl.BlockSpec(shape, lambda b, s: (0,) * len(shape))
    pair_spec = pl.BlockSpec((1, N_PAIRS, tm, LANES), lambda b, s: (b, 0, s, 0))
    pair_shape = jax.ShapeDtypeStruct((B, N_PAIRS, S, LANES), F32)
    return pl.pallas_call(
        _prep_kernel,
        grid=(B, S // tm),
        in_specs=[
            pl.BlockSpec((1, tm, D_MODEL), lambda b, s: (b, s, 0)),
            const((1, D_MODEL)), const((1, D_MODEL)), const((D_MODEL, D_IN)), const((1, N_SHIFT)),
            const((LANES, 2 * D_RWKV)), const((1, 2 * D_RWKV)), const((GATE_LORA, D_RWKV)),
            const((1, D_RWKV)), const((1, D_RWKV)), const((1, D_RWKV)), const((D_RWKV, D_RWKV)),
            const((1, D_GMLP)), const((1, D_GMLP)), const((GMLP_GROUPS, GCHUNK, GCHUNK)),
            const((GCHUNK, GMLP_GROUPS)),
        ],
        out_specs=[pair_spec] * 8 + [pl.BlockSpec((1, tm, D_GMLP), lambda b, s: (b, s, 0))],
        out_shape=[pair_shape] * 8 + [jax.ShapeDtypeStruct((B, S, D_GMLP), BF16)],
        scratch_shapes=[pltpu.VMEM((8, N_SHIFT), F32)],
        compiler_params=pltpu.CompilerParams(dimension_semantics=("arbitrary", "arbitrary"),
                                             vmem_limit_bytes=VMEM_LIMIT),
        name="prep",
    )(x, ln_g, ln_b, w_in, mu, wwa, w0a0, wg, k_k, k_a, r_k, eones, glng, glnb, wsp, bsp)


def _wkv_kernel(r_ref, lw_ref, k_ref, v_ref, a_ref, b_ref, g_ref, bonus_ref, gng_ref, gnb_ref, emean_ref,
                o_ref, h_ref):
    C = WKV_CHUNK
    tb = r_ref.shape[2]

    @pl.when(pl.program_id(2) == 0)
    def _():
        h_ref[...] = jnp.zeros_like(h_ref)

    lane = _iota((C, LANES), 1)
    m0 = (lane < HEAD).astype(F32)
    m1 = 1.0 - m0
    rr = _iota((LANES, LANES), 0)
    cc = _iota((LANES, LANES), 1)
    strict = (rr % C) > (cc % C)
    incl = (rr % C) >= (cc % C)
    eye = (rr == cc).astype(F32)
    ltri = (_iota((C, C), 0) >= _iota((C, C), 1)).astype(BF16)

    def stack(x):
        return jnp.concatenate([x * m0, x * m1], axis=0)

    n_pairs = r_ref.shape[1]
    n_chunks = tb // C
    units = [(q, c) for q in range(n_pairs) for c in range(n_chunks)]

    def load(ref):
        return [ref[0, q, c * C:(c + 1) * C, :] for q, c in units]

    r_, lw_, k_, v_, a_, b_ = (load(ref) for ref in (r_ref, lw_ref, k_ref, v_ref, a_ref, b_ref))
    cum_ = [_dot3_rhs(ltri, lw) for lw in lw_]
    cend_ = [cum[C - 1:C, :] for cum in cum_]
    a_st_ = [stack(a * jnp.exp(cum - lw)) for a, cum, lw in zip(a_, cum_, lw_)]
    r_st_ = [stack(r * jnp.exp(cum)) for r, cum in zip(r_, cum_)]
    ginv_ = [jnp.exp(-cum) for cum in cum_]
    b_st_ = [stack(b * gi) for b, gi in zip(b_, ginv_)]
    k_st_ = [stack(k * gi) for k, gi in zip(k_, ginv_)]
    v_st_ = [stack(v) for v in v_]
    gend_ = [jnp.exp(cend - cum) for cend, cum in zip(cend_, cum_)]
    bk_end_ = [jnp.concatenate([stack(b * ge), stack(k * ge)], axis=0) for b, k, ge in zip(b_, k_, gend_)]

    G_ = [_dot_nt(jnp.concatenate([a_st, r_st], axis=0), jnp.concatenate([b_st, k_st], axis=0))
          for a_st, r_st, b_st, k_st in zip(a_st_, r_st_, b_st_, k_st_)]
    n1_ = [jnp.where(strict, G[:LANES, :LANES], 0.0) for G in G_]
    aak_ = [jnp.where(strict, G[:LANES, LANES:], 0.0) for G in G_]
    arb_ = [jnp.where(incl, G[LANES:, :LANES], 0.0) for G in G_]
    ark_ = [jnp.where(incl, G[LANES:, LANES:], 0.0) for G in G_]
    av_ = [_dot(jnp.concatenate([aak, ark], axis=0), v_st) for aak, ark, v_st in zip(aak_, ark_, v_st_)]

    n2_ = [_dot(n1, n1) for n1 in n1_]
    x_ = [_dot(n2, jnp.concatenate([n1, n2], axis=1)) for n1, n2 in zip(n1_, n2_)]
    t_ = [eye + n1 + n2 + x[:, :LANES] for n1, n2, x in zip(n1_, n2_, x_)]
    np_ = [x[:, LANES:] for x in x_]
    for _ in range(3):
        x_ = [_dot(npow, jnp.concatenate([t, npow], axis=1)) for t, npow in zip(t_, np_)]
        t_ = [t + x[:, :LANES] for t, x in zip(t_, x_)]
        np_ = [x[:, LANES:] for x in x_]
    t_ = [t + _dot(npow, t) for t, npow in zip(t_, np_)]

    x_ = [_dot(t, jnp.concatenate([a_st, av[:LANES]], axis=1)) for t, a_st, av in zip(t_, a_st_, av_)]
    z_ = [_dot(arb, x) for arb, x in zip(arb_, x_)]
    rp_ = [r_st + z[:, :LANES] for r_st, z in zip(r_st_, z_)]
    p3_ = [z[:, LANES:] + av[LANES:] for z, av in zip(z_, av_)]
    rhs_ = [jnp.concatenate([x, jnp.concatenate([jnp.zeros_like(v_st), v_st], axis=1)], axis=0)
            for x, v_st in zip(x_, v_st_)]
    mq_ = [_dot(bk_end.T, rhs) for bk_end, rhs in zip(bk_end_, rhs_)]
    m_ = [eye * jnp.exp(cend) + mq[:, :LANES] for cend, mq in zip(cend_, mq_)]

    emean = emean_ref[...]
    for q in range(n_pairs):
        H = h_ref[q]
        ys = []
        for c in range(n_chunks):
            u = q * n_chunks + c
            o_st = _dot(rp_[u], H) + p3_[u]
            ys.append(o_st[:C] + o_st[C:])
            H = _dot(m_[u], H) + mq_[u][:, LANES:]
        h_ref[q] = H
        y = jnp.concatenate(ys, axis=0)
        mu = _dot3_lhs(y, emean)
        yc = y - mu
        var = _dot3_lhs(yc * yc, emean)
        yn = yc * lax.rsqrt(var + GN_EPS) * gng_ref[q] + gnb_ref[q]
        o_ref[0, q] = ((yn + bonus_ref[0, q]) * g_ref[0, q]).astype(BF16)


def _wkv(r, lw, k, v, a, b, g, bonus, gn_g, gn_b, emean):
    B, P, S, _ = r.shape
    tb = WKV_TB
    pp = WKV_PAIRS
    seq = pl.BlockSpec((1, pp, tb, LANES), lambda bi, p, s: (bi, p, s, 0))
    par = pl.BlockSpec((pp, 1, LANES), lambda bi, p, s: (p, 0, 0))
    return pl.pallas_call(
        _wkv_kernel,
        grid=(B, P // pp, S // tb),
        in_specs=[seq] * 8 + [par, par, pl.BlockSpec((LANES, LANES), lambda bi, p, s: (0, 0))],
        out_specs=seq,
        out_shape=jax.ShapeDtypeStruct((B, P, S, LANES), BF16),
        scratch_shapes=[pltpu.VMEM((pp, LANES, LANES), F32)],
        compiler_params=pltpu.CompilerParams(dimension_semantics=("arbitrary", "arbitrary", "arbitrary"),
                                             vmem_limit_bytes=VMEM_LIMIT),
        name="wkv",
    )(r, lw, k, v, a, b, g, bonus, gn_g, gn_b, emean)


def _mixer_kernel(x_ref, lng_ref, lnb_ref, ya_ref, yb_ref, wout_ref, l1g_ref, l1b_ref, wr_ref, br_ref,
                  p_ref, wpg_ref, bpg_ref, wpp_ref, base_ref, x1_ref, route_ref):
    tm = x_ref.shape[1]
    x0 = _layer_norm(x_ref[0], lng_ref[...], lnb_ref[...], LN_EPS)
    ymix = jnp.concatenate([ya_ref[0, p] for p in range(N_PAIRS)] + [yb_ref[0]], axis=-1)
    mix = jnp.dot(ymix, wout_ref[...], preferred_element_type=F32)
    x1 = _layer_norm(ALPHA * x0 + mix, l1g_ref[...], l1b_ref[...], LN_EPS)
    x1_ref[0] = x1

    hi, mid, lo = _split3(x1)
    whi = wr_ref[0]
    wmid = wr_ref[1]
    wlo = wr_ref[2]
    d = lambda u, w: jnp.dot(u, w, preferred_element_type=F32)
    logits = (d(hi, whi) + d(hi, wmid) + d(mid, whi) + d(hi, wlo) + d(mid, wmid) + d(lo, whi)) + br_ref[...]
    lane = _iota((tm, LANES), 1).astype(F32)
    far = float(4 * LANES)
    is_g = jnp.where(lane >= N_EXPERTS, jnp.where(lane < N_EXPERTS + N_GROUPS, 1.0, 0.0), 0.0) > 0.5
    gl = jnp.where(is_g, logits, NEG)
    gmax = jnp.max(gl, axis=-1, keepdims=True)
    gsel = jnp.min(jnp.where(gl == gmax, lane, far), axis=-1, keepdims=True) - N_EXPERTS
    p_group = 1.0 / jnp.sum(jnp.where(is_g, jnp.exp(gl - gmax), 0.0), axis=-1, keepdims=True)
    grp_of_lane = jnp.floor(lane * (1.0 / EXPERTS_PER_GROUP))
    el = jnp.where(grp_of_lane == gsel, logits, NEG)
    v1 = jnp.max(el, axis=-1, keepdims=True)
    i1 = jnp.min(jnp.where(el == v1, lane, far), axis=-1, keepdims=True)
    el2 = jnp.where(lane == i1, NEG, el)
    v2 = jnp.max(el2, axis=-1, keepdims=True)
    i2 = jnp.min(jnp.where(el2 == v2, lane, far), axis=-1, keepdims=True)
    e21 = jnp.exp(v2 - v1)
    w1 = p_group / (1.0 + e21)
    w2 = p_group * e21 / (1.0 + e21)
    route_ref[0] = jnp.where(lane == 0, i1, jnp.where(lane == 1, i2, jnp.where(lane == 2, w1,
                                                                               jnp.where(lane == 3, w2, 0.0))))

    gate = _sigmoid(jnp.dot(x1.astype(BF16), wpg_ref[...], preferred_element_type=F32) + bpg_ref[...])
    ple = gate * jnp.dot(p_ref[0].astype(BF16), wpp_ref[...], preferred_element_type=F32)
    base_ref[0] = ALPHA * x1 + ple


def _mixer(x, ln_g, ln_b, ya, yb, w_out, l1g, l1b, wr3, br, p, wpg, bpg, wpp):
    B, S, _ = x.shape
    tm = MIX_TM
    const = lambda shape: pl.BlockSpec(shape, lambda b, s: (0,) * len(shape))
    row = lambda w: pl.BlockSpec((1, tm, w), lambda b, s: (b, s, 0))
    return pl.pallas_call(
        _mixer_kernel,
        grid=(B, S // tm),
        in_specs=[
            row(D_MODEL), const((1, D_MODEL)), const((1, D_MODEL)),
            pl.BlockSpec((1, N_PAIRS, tm, LANES), lambda b, s: (b, 0, s, 0)), row(D_GMLP),
            const((D_MODEL, D_MODEL)), const((1, D_MODEL)), const((1, D_MODEL)),
            const((3, D_MODEL, LANES)), const((1, LANES)),
            row(D_PLE), const((D_MODEL, D_MODEL)), const((1, D_MODEL)), const((D_PLE, D_MODEL)),
        ],
        out_specs=[row(D_MODEL), row(D_MODEL), row(LANES)],
        out_shape=[jax.ShapeDtypeStruct((B, S, D_MODEL), F32), jax.ShapeDtypeStruct((B, S, D_MODEL), F32),
                   jax.ShapeDtypeStruct((B, S, LANES), F32)],
        compiler_params=pltpu.CompilerParams(dimension_semantics=("arbitrary", "arbitrary"),
                                             vmem_limit_bytes=VMEM_LIMIT),
        name="mixer",
    )(x, ln_g, ln_b, ya, yb, w_out, l1g, l1b, wr3, br, p, wpg, bpg, wpp)


def _slots_kernel(route_ref, dest_ref, pend_ref, carry_ref, rank_ref):
    ph = pl.program_id(0)
    i = pl.program_id(1)
    tm = route_ref.shape[0]
    lane = _iota((tm, LANES), 1)
    route = route_ref[...]
    e1 = route[:, 0:1].astype(jnp.int32)
    e2 = route[:, 1:2].astype(jnp.int32)
    oh1 = lane == e1
    oh2 = lane == e2

    @pl.when((ph == 0) & (i == 0))
    def _():
        carry_ref[...] = jnp.zeros_like(carry_ref)

    @pl.when(ph == 0)
    def _():
        below = (_iota((tm, tm), 0) > _iota((tm, tm), 1)).astype(BF16)
        o1 = oh1.astype(BF16)
        o2 = oh2.astype(BF16)
        c1 = jnp.dot(below, o1, preferred_element_type=F32)
        c2 = jnp.dot(below, o2, preferred_element_type=F32)
        tot1 = jnp.sum(o1.astype(F32), axis=0, keepdims=True)
        tot2 = jnp.sum(o2.astype(F32), axis=0, keepdims=True)
        carry = carry_ref[0:1, :]
        rank1 = jnp.sum(jnp.where(oh1, c1 + carry, 0.0), axis=-1, keepdims=True)
        rank2 = jnp.sum(jnp.where(oh2, c2 + carry + tot1, 0.0), axis=-1, keepdims=True)
        rank_ref[i] = jnp.where(lane == 0, rank1, jnp.where(lane == 1, rank2, 0.0))
        carry_ref[0:1, :] = carry + tot1 + tot2
        dest_ref[...] = jnp.zeros_like(dest_ref)
        pend_ref[...] = jnp.zeros_like(pend_ref)

    @pl.when(ph == 1)
    def _():
        counts = carry_ref[0:1, :]
        padded = jnp.floor((counts + (EXPERT_ROWS - 1)) * (1.0 / EXPERT_ROWS)) * EXPERT_ROWS
        upper = (_iota((LANES, LANES), 0) <= _iota((LANES, LANES), 1)).astype(BF16)
        pend = _dot3_lhs(jnp.broadcast_to(padded, (8, LANES)), upper)[0:1, :]
        pstart = pend - padded
        rank = rank_ref[i]
        d1 = jnp.sum(jnp.where(oh1, pstart, 0.0), axis=-1, keepdims=True) + rank[:, 0:1]
        d2 = jnp.sum(jnp.where(oh2, pstart, 0.0), axis=-1, keepdims=True) + rank[:, 1:2]
        dest_ref[...] = jnp.where(lane == 0, d1, jnp.where(lane == 1, d2, 0.0)).astype(jnp.int32)
        pend_ref[...] = jnp.broadcast_to(pend, (8, LANES)).astype(jnp.int32)


def _slots(route):
    T = route.shape[0]
    tm = SLOT_TM
    nt = T // tm
    return pl.pallas_call(
        _slots_kernel,
        grid=(2, nt),
        in_specs=[pl.BlockSpec((tm, LANES), lambda ph, i: (i, 0))],
        out_specs=[pl.BlockSpec((tm, LANES), lambda ph, i: (i * ph, 0)),
                   pl.BlockSpec((8, LANES), lambda ph, i: (0, 0))],
        out_shape=[jax.ShapeDtypeStruct((T, LANES), jnp.int32), jax.ShapeDtypeStruct((8, LANES), jnp.int32)],
        scratch_shapes=[pltpu.VMEM((8, LANES), F32), pltpu.VMEM((nt, tm, LANES), F32)],
        compiler_params=pltpu.CompilerParams(dimension_semantics=("arbitrary", "arbitrary"),
                                             vmem_limit_bytes=VMEM_LIMIT),
        name="slots",
    )(route)


def _dispatch_kernel(dest_ref, x_ref, xs_in_ref, xs_ref, sem):
    del xs_in_ref
    tm = dest_ref.shape[0] // TOP_K

    def copy(t, j):
        return pltpu.make_async_copy(x_ref.at[pl.ds(t, 1)], xs_ref.at[pl.ds(dest_ref[TOP_K * t + j], 1)], sem)

    def start(t, _):
        copy(t, 0).start()
        copy(t, 1).start()
        return 0

    def wait(t, _):
        copy(t, 0).wait()
        copy(t, 1).wait()
        return 0

    lax.fori_loop(0, tm, start, 0)
    lax.fori_loop(0, tm, wait, 0)


def _dispatch(dest_flat, x1, n_rows):
    T = x1.shape[0]
    tm = DISPATCH_TM
    xs0 = jnp.zeros((n_rows, D_MODEL), F32)
    return pl.pallas_call(
        _dispatch_kernel,
        grid=(T // tm,),
        in_specs=[pl.BlockSpec((TOP_K * tm,), lambda i: (i,), memory_space=pltpu.SMEM),
                  pl.BlockSpec((tm, D_MODEL), lambda i: (i, 0)), pl.BlockSpec(memory_space=pl.ANY)],
        out_specs=pl.BlockSpec(memory_space=pl.ANY),
        out_shape=jax.ShapeDtypeStruct((n_rows, D_MODEL), F32),
        scratch_shapes=[pltpu.SemaphoreType.DMA],
        input_output_aliases={2: 0},
        compiler_params=pltpu.CompilerParams(dimension_semantics=("arbitrary",)),
        name="dispatch",
    )(dest_flat, x1, xs0)


def _block_expert(pend_ref, j):
    e = jnp.int32(0)
    for i in range(N_EXPERTS):
        e = e + (pend_ref[i] <= j * EXPERT_ROWS).astype(jnp.int32)
    return jnp.minimum(e, N_EXPERTS - 1)


def _experts_kernel(pend_ref, xs_ref, wg_ref, wu_ref, wd_ref, y_ref):
    j = pl.program_id(0)
    used = j * EXPERT_ROWS < pend_ref[N_EXPERTS - 1]

    @pl.when(used)
    def _():
        xb = xs_ref[...].astype(BF16)
        hg = jnp.dot(xb, wg_ref[0], preferred_element_type=F32)
        hu = jnp.dot(xb, wu_ref[0], preferred_element_type=F32)
        hid = hg * _sigmoid(hg) * hu
        y_ref[...] = jnp.dot(hid.astype(BF16), wd_ref[0], preferred_element_type=F32)

    @pl.when(jnp.logical_not(used))
    def _():
        y_ref[...] = jnp.zeros_like(y_ref)


def _experts(pend, xs, wg, wu, wd):
    n_rows = xs.shape[0]
    rows = EXPERT_ROWS
    wspec = lambda shape: pl.BlockSpec((1,) + shape, lambda j, pe: (_block_expert(pe, j), 0, 0))
    return pl.pallas_call(
        _experts_kernel,
        grid_spec=pltpu.PrefetchScalarGridSpec(
            num_scalar_prefetch=1,
            grid=(n_rows // rows,),
            in_specs=[pl.BlockSpec((rows, D_MODEL), lambda j, pe: (j, 0)),
                      wspec((D_MODEL, D_EXPERT)), wspec((D_MODEL, D_EXPERT)), wspec((D_EXPERT, D_MODEL))],
            out_specs=pl.BlockSpec((rows, D_MODEL), lambda j, pe: (j, 0)),
        ),
        out_shape=jax.ShapeDtypeStruct((n_rows, D_MODEL), F32),
        compiler_params=pltpu.CompilerParams(dimension_semantics=("arbitrary",), vmem_limit_bytes=VMEM_LIMIT),
        name="experts",
    )(pend, xs, wg, wu, wd)


def _combine_kernel(dest_ref, ys_ref, base_ref, route_ref, lg_ref, lb_ref, o_ref, buf_ref, sem):
    tm = base_ref.shape[0]

    def copy(t, j):
        return pltpu.make_async_copy(ys_ref.at[pl.ds(dest_ref[TOP_K * t + j], 1)], buf_ref.at[j, pl.ds(t, 1)], sem)

    def start(t, _):
        copy(t, 0).start()
        copy(t, 1).start()
        return 0

    def wait(t, _):
        copy(t, 0).wait()
        copy(t, 1).wait()
        return 0

    lax.fori_loop(0, tm, start, 0)
    lax.fori_loop(0, tm, wait, 0)
    route = route_ref[...]
    ffn = buf_ref[0] * route[:, 2:3] + buf_ref[1] * route[:, 3:4]
    o_ref[...] = _layer_norm(base_ref[...] + ffn, lg_ref[...], lb_ref[...], LN_EPS)


def _combine(dest_flat, ys, base, route, l2g, l2b):
    T = base.shape[0]
    tm = COMBINE_TM
    return pl.pallas_call(
        _combine_kernel,
        grid=(T // tm,),
        in_specs=[pl.BlockSpec((TOP_K * tm,), lambda i: (i,), memory_space=pltpu.SMEM),
                  pl.BlockSpec(memory_space=pl.ANY),
                  pl.BlockSpec((tm, D_MODEL), lambda i: (i, 0)), pl.BlockSpec((tm, LANES), lambda i: (i, 0)),
                  pl.BlockSpec((1, D_MODEL), lambda i: (0, 0)), pl.BlockSpec((1, D_MODEL), lambda i: (0, 0))],
        out_specs=pl.BlockSpec((tm, D_MODEL), lambda i: (i, 0)),
        out_shape=jax.ShapeDtypeStruct((T, D_MODEL), F32),
        scratch_shapes=[pltpu.VMEM((TOP_K, tm, D_MODEL), F32), pltpu.SemaphoreType.DMA],
        compiler_params=pltpu.CompilerParams(dimension_semantics=("arbitrary",), vmem_limit_bytes=VMEM_LIMIT),
        name="combine",
    )(dest_flat, ys, base, route, l2g, l2b)


def _block_diag_const(n, blk, val):
    idx = jnp.arange(n) // blk
    return jnp.where(idx[:, None] == idx[None, :], val, 0.0).astype(BF16)


def kernel(x, p, ln_emb_g, ln_emb_b, w_in, mu_shift, w0, w_decay_up, a0, w_iclr_up, w_gate_up, k_k, k_a, r_k, gn_g, gn_b, gmlp_ln_g, gmlp_ln_b, w_spatial, b_spatial, w_out, ln1_g, ln1_b, w_group_router, b_group_router, w_expert_router, b_expert_router, w_exp_gate, w_exp_up, w_exp_down, w_ple_gate, b_ple_gate, w_ple_proj, ln2_g, ln2_b):
    B, S, D = x.shape
    T = B * S
    row = lambda t: t.reshape(1, -1).astype(F32)

    zl = jnp.zeros((DECAY_LORA, D_RWKV), F32)
    wwa = jnp.concatenate([jnp.concatenate([w_decay_up[0], zl], axis=1),
                           jnp.concatenate([zl, w_iclr_up[0]], axis=1)], axis=0).astype(BF16)
    w0a0 = jnp.concatenate([w0[0], a0[0]]).reshape(1, -1)
    eones = _block_diag_const(D_RWKV, HEAD, 1.0)
    emean = _block_diag_const(LANES, HEAD, 1.0 / HEAD)

    r, lw, k, v, a, b, g, bonus, yb = _prep(
        x, row(ln_emb_g), row(ln_emb_b), w_in[0].astype(BF16), row(mu_shift[0]), wwa, w0a0,
        w_gate_up[0].astype(BF16), row(k_k[0]), row(k_a[0]), row(r_k[0]), eones,
        row(gmlp_ln_g[0]), row(gmlp_ln_b[0]), w_spatial[0], b_spatial[0].T)

    ya = _wkv(r, lw, k, v, a, b, g, bonus, gn_g[0].reshape(N_PAIRS, 1, LANES), gn_b[0].reshape(N_PAIRS, 1, LANES),
              emean)

    wr = jnp.concatenate([w_expert_router[0].reshape(D, N_EXPERTS), w_group_router[0],
                          jnp.zeros((D, LANES - N_EXPERTS - N_GROUPS), F32)], axis=1)
    wr3 = jnp.stack(_split3(wr))
    br = jnp.concatenate([b_expert_router[0].reshape(-1), b_group_router[0],
                          jnp.zeros((LANES - N_EXPERTS - N_GROUPS,), F32)]).reshape(1, LANES)
    base, x1, route = _mixer(x, row(ln_emb_g), row(ln_emb_b), ya, yb, w_out[0].astype(BF16), row(ln1_g[0]),
                             row(ln1_b[0]), wr3, br, p[0], w_ple_gate[0].astype(BF16), row(b_ple_gate[0]),
                             w_ple_proj[0].astype(BF16))
    base = base.reshape(T, D)
    x1 = x1.reshape(T, D)
    route = route.reshape(T, LANES)

    dest, pend = _slots(route)
    dest_flat = dest[:, :TOP_K].reshape(T * TOP_K)
    pend = pend[0, :N_EXPERTS]

    n_blocks = -(-(T * TOP_K) // EXPERT_ROWS) + N_EXPERTS
    xs = _dispatch(dest_flat, x1, n_blocks * EXPERT_ROWS)
    ys = _experts(pend, xs, w_exp_gate[0].astype(BF16), w_exp_up[0].astype(BF16), w_exp_down[0].astype(BF16))
    out = _combine(dest_flat, ys, base, route, row(ln2_g[0]), row(ln2_b[0]))
    return out.reshape(B, S, D)
```

```python
import functools
import math

import jax
import jax.numpy as jnp
from jax import lax
from jax.experimental import pallas as pl
from jax.experimental.pallas import tpu as pltpu

F32 = jnp.float32
BF16 = jnp.bfloat16

D_MODEL = 1024
D_RWKV = 512
HEAD = 64
D_GMLP = 512
GMLP_GROUPS = 4
GROUP_W = 128
GCHUNK = 128
DECAY_LORA = 64
ICLR_LORA = 64
GATE_LORA = 128
N_SHIFT = 3 * D_RWKV + DECAY_LORA + ICLR_LORA + GATE_LORA
D_IN = N_SHIFT + 2 * D_GMLP
D_PLE = 256
N_GROUPS = 4
EXPERTS_PER_GROUP = 8
N_EXPERTS = 32
TOP_K = 2
D_EXPERT = 512
DEPTH = 1
ALPHA = (2.0 * DEPTH) ** 0.25
LN_EPS = 1e-5
GN_EPS = 64e-5
DECAY_SCALE = math.exp(-0.5)

LANES = 128
WKV_CHUNK = 64
N_PAIRS = D_RWKV // LANES
VMEM_LIMIT = 56 * 1024 * 1024

PREP_TM = 256
WKV_TB = 256
WKV_PAIRS = 4
MIX_TM = 256
SLOT_TM = 512
EXPERT_ROWS = 256
DISPATCH_TM = 1024
COMBINE_TM = 256
DMA_UNROLL = 4
NEG = -1e30


def _dot(a, b):
    return jnp.dot(a.astype(BF16), b.astype(BF16), preferred_element_type=F32)


def _dot_nt(a, b):
    return lax.dot_general(a.astype(BF16), b.astype(BF16), (((1,), (1,)), ((), ())),
                           preferred_element_type=F32)


def _split3(x):
    hi = x.astype(BF16)
    r1 = x - hi.astype(F32)
    mid = r1.astype(BF16)
    lo = (r1 - mid.astype(F32)).astype(BF16)
    return hi, mid, lo


def _dot3_lhs(x, w):
    hi, mid, lo = _split3(x)
    w = w.astype(BF16)
    return (jnp.dot(hi, w, preferred_element_type=F32) + jnp.dot(mid, w, preferred_element_type=F32)
            + jnp.dot(lo, w, preferred_element_type=F32))


def _dot3_rhs(w, x):
    hi, mid, lo = _split3(x)
    w = w.astype(BF16)
    return (jnp.dot(w, hi, preferred_element_type=F32) + jnp.dot(w, mid, preferred_element_type=F32)
            + jnp.dot(w, lo, preferred_element_type=F32))


def _layer_norm(x, g, b, eps):
    mu = jnp.mean(x, axis=-1, keepdims=True)
    xc = x - mu
    var = jnp.mean(xc * xc, axis=-1, keepdims=True)
    return xc * lax.rsqrt(var + eps) * g + b


def _sigmoid(x):
    return 1.0 / (1.0 + jnp.exp(-x))


def _iota(shape, dim):
    return lax.broadcasted_iota(jnp.int32, shape, dim)


def _prep_kernel(x_ref, lng_ref, lnb_ref, win_ref, mu_ref, wwa_ref, w0a0_ref, wg_ref, kk_ref, ka_ref, rk_ref,
                 eones_ref, glng_ref, glnb_ref, wsp_ref, bsp_ref,
                 r_ref, lw_ref, k_ref, v_ref, a_ref, b_ref, g_ref, bonus_ref, yb_ref, carry_ref):
    tm = x_ref.shape[1]

    @pl.when(pl.program_id(1) == 0)
    def _():
        carry_ref[...] = jnp.zeros_like(carry_ref)

    x0 = _layer_norm(x_ref[0], lng_ref[...], lnb_ref[...], LN_EPS)
    proj = jnp.dot(x0.astype(BF16), win_ref[...], preferred_element_type=F32)

    h = proj[:, :N_SHIFT]
    rolled = pltpu.roll(h, 1, 0)
    first = _iota((tm, N_SHIFT), 0) == 0
    prev = jnp.where(first, jnp.broadcast_to(carry_ref[0:1, :], (tm, N_SHIFT)), rolled)
    carry_ref[0:1, :] = h[tm - 1:tm, :]
    h = h + (prev - h) * mu_ref[...]

    r = h[:, 0:D_RWKV]
    k = h[:, D_RWKV:2 * D_RWKV]
    v = h[:, 2 * D_RWKV:3 * D_RWKV]
    xwa = h[:, 3 * D_RWKV:3 * D_RWKV + LANES]
    xg = h[:, 3 * D_RWKV + LANES:N_SHIFT]

    lane = _iota((tm, LANES), 1)
    twa = jnp.where(lane < DECAY_LORA, jnp.tanh(xwa), xwa)
    da = _dot(twa, wwa_ref[...]) + w0a0_ref[...]
    logw = -DECAY_SCALE * _sigmoid(da[:, :D_RWKV])
    ag = _sigmoid(da[:, D_RWKV:])
    g = _dot(_sigmoid(xg), wg_ref[...])

    eones = eones_ref[...]
    kk = k * kk_ref[...]
    n2 = _dot3_lhs(kk * kk, eones)
    kk = kk / jnp.maximum(jnp.sqrt(n2), 1e-12)
    k = k * (1.0 + (ag - 1.0) * ka_ref[...])
    bonus = _dot3_lhs(r * k * rk_ref[...], eones) * v

    for p in range(N_PAIRS):
        sl = slice(p * LANES, (p + 1) * LANES)
        r_ref[0, p] = r[:, sl]
        lw_ref[0, p] = logw[:, sl]
        k_ref[0, p] = k[:, sl]
        v_ref[0, p] = v[:, sl]
        a_ref[0, p] = -kk[:, sl]
        b_ref[0, p] = (kk * ag)[:, sl]
        g_ref[0, p] = g[:, sl]
        bonus_ref[0, p] = bonus[:, sl]

    zin = proj[:, N_SHIFT:]
    z = 0.5 * zin * (1.0 + lax.erf(zin * (0.5 ** 0.5)))
    zu = z[:, :D_GMLP]
    zv = z[:, D_GMLP:]
    causal = _iota((GCHUNK, GCHUNK), 0) >= _iota((GCHUNK, GCHUNK), 1)
    for gi in range(GMLP_GROUPS):
        gs = slice(gi * GROUP_W, (gi + 1) * GROUP_W)
        zvn = _layer_norm(zv[:, gs], glng_ref[:, gs], glnb_ref[:, gs], LN_EPS)
        ws = jnp.where(causal, wsp_ref[gi], 0.0).astype(BF16)
        bcol = bsp_ref[:, gi:gi + 1]
        for c in range(tm // GCHUNK):
            ts = slice(c * GCHUNK, (c + 1) * GCHUNK)
            mixed = jnp.dot(ws, zvn[ts].astype(BF16), preferred_element_type=F32) + bcol
            yb_ref[0, ts, gs] = (zu[ts, gs] * mixed).astype(BF16)


def _prep(x, ln_g, ln_b, w_in, mu, wwa, w0a0, wg, k_k, k_a, r_k, eones, glng, glnb, wsp, bsp):
    B, S, _ = x.shape
    tm = PREP_TM
    const = lambda shape: pl.BlockSpec(shape, lambda b, s: (0,) * len(shape))
    pair_spec = pl.BlockSpec((1, N_PAIRS, tm, LANES), lambda b, s: (b, 0, s, 0))
    pair_shape = jax.ShapeDtypeStruct((B, N_PAIRS, S, LANES), F32)
    return pl.pallas_call(
        _prep_kernel,
        grid=(B, S // tm),
        in_specs=[
            pl.BlockSpec((1, tm, D_MODEL), lambda b, s: (b, s, 0)),
            const((1, D_MODEL)), const((1, D_MODEL)), const((D_MODEL, D_IN)), const((1, N_SHIFT)),
            const((LANES, 2 * D_RWKV)), const((1, 2 * D_RWKV)), const((GATE_LORA, D_RWKV)),
            const((1, D_RWKV)), const((1, D_RWKV)), const((1, D_RWKV)), const((D_RWKV, D_RWKV)),
            const((1, D_GMLP)), const((1, D_GMLP)), const((GMLP_GROUPS, GCHUNK, GCHUNK)),
            const((GCHUNK, GMLP_GROUPS)),
        ],
        out_specs=[pair_spec] * 8 + [pl.BlockSpec((1, tm, D_GMLP), lambda b, s: (b, s, 0))],
        out_shape=[pair_shape] * 8 + [jax.ShapeDtypeStruct((B, S, D_GMLP), BF16)],
        scratch_shapes=[pltpu.VMEM((8, N_SHIFT), F32)],
        compiler_params=pltpu.CompilerParams(dimension_semantics=("arbitrary", "arbitrary"),
                                             vmem_limit_bytes=VMEM_LIMIT),
        name="prep",
    )(x, ln_g, ln_b, w_in, mu, wwa, w0a0, wg, k_k, k_a, r_k, eones, glng, glnb, wsp, bsp)


def _wkv_kernel(r_ref, lw_ref, k_ref, v_ref, a_ref, b_ref, g_ref, bonus_ref, gng_ref, gnb_ref, emean_ref,
                o_ref, h_ref):
    C = WKV_CHUNK
    tb = r_ref.shape[2]

    @pl.when(pl.program_id(2) == 0)
    def _():
        h_ref[...] = jnp.zeros_like(h_ref)

    lane = _iota((C, LANES), 1)
    m0 = (lane < HEAD).astype(F32)
    m1 = 1.0 - m0
    rr = _iota((LANES, LANES), 0)
    cc = _iota((LANES, LANES), 1)
    strict = (rr % C) > (cc % C)
    incl = (rr % C) >= (cc % C)
    eye = (rr == cc).astype(F32)
    ltri = (_iota((C, C), 0) >= _iota((C, C), 1)).astype(BF16)

    def stack(x):
        return jnp.concatenate([x * m0, x * m1], axis=0)

    n_pairs = r_ref.shape[1]
    n_chunks = tb // C
    units = [(q, c) for q in range(n_pairs) for c in range(n_chunks)]

    def load(ref):
        return [ref[0, q, c * C:(c + 1) * C, :] for q, c in units]

    r_, lw_, k_, v_, a_, b_ = (load(ref) for ref in (r_ref, lw_ref, k_ref, v_ref, a_ref, b_ref))
    cum_ = [_dot3_rhs(ltri, lw) for lw in lw_]
    cend_ = [cum[C - 1:C, :] for cum in cum_]
    a_st_ = [stack(a * jnp.exp(cum - lw)) for a, cum, lw in zip(a_, cum_, lw_)]
    r_st_ = [stack(r * jnp.exp(cum)) for r, cum in zip(r_, cum_)]
    ginv_ = [jnp.exp(-cum) for cum in cum_]
    b_st_ = [stack(b * gi) for b, gi in zip(b_, ginv_)]
    k_st_ = [stack(k * gi) for k, gi in zip(k_, ginv_)]
    v_st_ = [stack(v) for v in v_]
    gend_ = [jnp.exp(cend - cum) for cend, cum in zip(cend_, cum_)]
    bk_end_ = [jnp.concatenate([stack(b * ge), stack(k * ge)], axis=0) for b, k, ge in zip(b_, k_, gend_)]

    G_ = [_dot_nt(jnp.concatenate([a_st, r_st], axis=0), jnp.concatenate([b_st, k_st], axis=0))
          for a_st, r_st, b_st, k_st in zip(a_st_, r_st_, b_st_, k_st_)]
    n1_ = [jnp.where(strict, G[:LANES, :LANES], 0.0) for G in G_]
    aak_ = [jnp.where(strict, G[:LANES, LANES:], 0.0) for G in G_]
    arb_ = [jnp.where(incl, G[LANES:, :LANES], 0.0) for G in G_]
    ark_ = [jnp.where(incl, G[LANES:, LANES:], 0.0) for G in G_]
    av_ = [_dot(jnp.concatenate([aak, ark], axis=0), v_st) for aak, ark, v_st in zip(aak_, ark_, v_st_)]

    n2_ = [_dot(n1, n1) for n1 in n1_]
    x_ = [_dot(n2, jnp.concatenate([n1, n2], axis=1)) for n1, n2 in zip(n1_, n2_)]
    t_ = [eye + n1 + n2 + x[:, :LANES] for n1, n2, x in zip(n1_, n2_, x_)]
    np_ = [x[:, LANES:] for x in x_]
    for _ in range(3):
        x_ = [_dot(npow, jnp.concatenate([t, npow], axis=1)) for t, npow in zip(t_, np_)]
        t_ = [t + x[:, :LANES] for t, x in zip(t_, x_)]
        np_ = [x[:, LANES:] for x in x_]
    t_ = [t + _dot(npow, t) for t, npow in zip(t_, np_)]

    x_ = [_dot(t, jnp.concatenate([a_st, av[:LANES]], axis=1)) for t, a_st, av in zip(t_, a_st_, av_)]
    z_ = [_dot(arb, x) for arb, x in zip(arb_, x_)]
    rp_ = [r_st + z[:, :LANES] for r_st, z in zip(r_st_, z_)]
    p3_ = [z[:, LANES:] + av[LANES:] for z, av in zip(z_, av_)]
    rhs_ = [jnp.concatenate([x, jnp.concatenate([jnp.zeros_like(v_st), v_st], axis=1)], axis=0)
            for x, v_st in zip(x_, v_st_)]
    mq_ = [_dot(bk_end.T, rhs) for bk_end, rhs in zip(bk_end_, rhs_)]
    m_ = [eye * jnp.exp(cend) + mq[:, :LANES] for cend, mq in zip(cend_, mq_)]

    emean = emean_ref[...]
    for q in range(n_pairs):
        H = h_ref[q]
        ys = []
        for c in range(n_chunks):
            u = q * n_chunks + c
            o_st = _dot(rp_[u], H) + p3_[u]
            ys.append(o_st[:C] + o_st[C:])
            H = _dot(m_[u], H) + mq_[u][:, LANES:]
        h_ref[q] = H
        y = jnp.concatenate(ys, axis=0)
        mu = _dot3_lhs(y, emean)
        yc = y - mu
        var = _dot3_lhs(yc * yc, emean)
        yn = yc * lax.rsqrt(var + GN_EPS) * gng_ref[q] + gnb_ref[q]
        o_ref[0, q] = ((yn + bonus_ref[0, q]) * g_ref[0, q]).astype(BF16)


def _wkv(r, lw, k, v, a, b, g, bonus, gn_g, gn_b, emean):
    B, P, S, _ = r.shape
    tb = WKV_TB
    pp = WKV_PAIRS
    seq = pl.BlockSpec((1, pp, tb, LANES), lambda bi, p, s: (bi, p, s, 0))
    par = pl.BlockSpec((pp, 1, LANES), lambda bi, p, s: (p, 0, 0))
    return pl.pallas_call(
        _wkv_kernel,
        grid=(B, P // pp, S // tb),
        in_specs=[seq] * 8 + [par, par, pl.BlockSpec((LANES, LANES), lambda bi, p, s: (0, 0))],
        out_specs=seq,
        out_shape=jax.ShapeDtypeStruct((B, P, S, LANES), BF16),
        scratch_shapes=[pltpu.VMEM((pp, LANES, LANES), F32)],
        compiler_params=pltpu.CompilerParams(dimension_semantics=("arbitrary", "arbitrary", "arbitrary"),
                                             vmem_limit_bytes=VMEM_LIMIT),
        name="wkv",
    )(r, lw, k, v, a, b, g, bonus, gn_g, gn_b, emean)


def _mixer_kernel(x_ref, lng_ref, lnb_ref, ya_ref, yb_ref, wout_ref, l1g_ref, l1b_ref, wr_ref, br_ref,
                  p_ref, wpg_ref, bpg_ref, wpp_ref, base_ref, x1_ref, route_ref):
    tm = x_ref.shape[1]
    x0 = _layer_norm(x_ref[0], lng_ref[...], lnb_ref[...], LN_EPS)
    ymix = jnp.concatenate([ya_ref[0, p] for p in range(N_PAIRS)] + [yb_ref[0]], axis=-1)
    mix = jnp.dot(ymix, wout_ref[...], preferred_element_type=F32)
    x1 = _layer_norm(ALPHA * x0 + mix, l1g_ref[...], l1b_ref[...], LN_EPS)
    x1_ref[0] = x1

    hi, mid, lo = _split3(x1)
    whi = wr_ref[0]
    wmid = wr_ref[1]
    wlo = wr_ref[2]
    d = lambda u, w: jnp.dot(u, w, preferred_element_type=F32)
    logits = (d(hi, whi) + d(hi, wmid) + d(mid, whi) + d(hi, wlo) + d(mid, wmid) + d(lo, whi)) + br_ref[...]
    lane = _iota((tm, LANES), 1).astype(F32)
    far = float(4 * LANES)
    is_g = jnp.where(lane >= N_EXPERTS, jnp.where(lane < N_EXPERTS + N_GROUPS, 1.0, 0.0), 0.0) > 0.5
    gl = jnp.where(is_g, logits, NEG)
    gmax = jnp.max(gl, axis=-1, keepdims=True)
    gsel = jnp.min(jnp.where(gl == gmax, lane, far), axis=-1, keepdims=True) - N_EXPERTS
    p_group = 1.0 / jnp.sum(jnp.where(is_g, jnp.exp(gl - gmax), 0.0), axis=-1, keepdims=True)
    grp_of_lane = jnp.floor(lane * (1.0 / EXPERTS_PER_GROUP))
    el = jnp.where(grp_of_lane == gsel, logits, NEG)
    v1 = jnp.max(el, axis=-1, keepdims=True)
    i1 = jnp.min(jnp.where(el == v1, lane, far), axis=-1, keepdims=True)
    el2 = jnp.where(lane == i1, NEG, el)
    v2 = jnp.max(el2, axis=-1, keepdims=True)
    i2 = jnp.min(jnp.where(el2 == v2, lane, far), axis=-1, keepdims=True)
    e21 = jnp.exp(v2 - v1)
    w1 = p_group / (1.0 + e21)
    w2 = p_group * e21 / (1.0 + e21)
    route_ref[0] = jnp.where(lane == 0, i1, jnp.where(lane == 1, i2, jnp.where(lane == 2, w1,
                                                                               jnp.where(lane == 3, w2, 0.0))))

    gate = _sigmoid(jnp.dot(x1.astype(BF16), wpg_ref[...], preferred_element_type=F32) + bpg_ref[...])
    ple = gate * jnp.dot(p_ref[0].astype(BF16), wpp_ref[...], preferred_element_type=F32)
    base_ref[0] = ALPHA * x1 + ple


def _mixer(x, ln_g, ln_b, ya, yb, w_out, l1g, l1b, wr3, br, p, wpg, bpg, wpp):
    B, S, _ = x.shape
    tm = MIX_TM
    const = lambda shape: pl.BlockSpec(shape, lambda b, s: (0,) * len(shape))
    row = lambda w: pl.BlockSpec((1, tm, w), lambda b, s: (b, s, 0))
    return pl.pallas_call(
        _mixer_kernel,
        grid=(B, S // tm),
        in_specs=[
            row(D_MODEL), const((1, D_MODEL)), const((1, D_MODEL)),
            pl.BlockSpec((1, N_PAIRS, tm, LANES), lambda b, s: (b, 0, s, 0)), row(D_GMLP),
            const((D_MODEL, D_MODEL)), const((1, D_MODEL)), const((1, D_MODEL)),
            const((3, D_MODEL, LANES)), const((1, LANES)),
            row(D_PLE), const((D_MODEL, D_MODEL)), const((1, D_MODEL)), const((D_PLE, D_MODEL)),
        ],
        out_specs=[row(D_MODEL), row(D_MODEL), row(LANES)],
        out_shape=[jax.ShapeDtypeStruct((B, S, D_MODEL), F32), jax.ShapeDtypeStruct((B, S, D_MODEL), F32),
                   jax.ShapeDtypeStruct((B, S, LANES), F32)],
        compiler_params=pltpu.CompilerParams(dimension_semantics=("arbitrary", "arbitrary"),
                                             vmem_limit_bytes=VMEM_LIMIT),
        name="mixer",
    )(x, ln_g, ln_b, ya, yb, w_out, l1g, l1b, wr3, br, p, wpg, bpg, wpp)


def _slots_kernel(route_ref, dest_ref, pend_ref, carry_ref, rank_ref):
    ph = pl.program_id(0)
    i = pl.program_id(1)
    tm = route_ref.shape[0]
    lane = _iota((tm, LANES), 1)
    route = route_ref[...]
    e1 = route[:, 0:1].astype(jnp.int32)
    e2 = route[:, 1:2].astype(jnp.int32)
    oh1 = lane == e1
    oh2 = lane == e2

    @pl.when((ph == 0) & (i == 0))
    def _():
        carry_ref[...] = jnp.zeros_like(carry_ref)

    @pl.when(ph == 0)
    def _():
        below = (_iota((tm, tm), 0) > _iota((tm, tm), 1)).astype(BF16)
        o1 = oh1.astype(BF16)
        o2 = oh2.astype(BF16)
        c1 = jnp.dot(below, o1, preferred_element_type=F32)
        c2 = jnp.dot(below, o2, preferred_element_type=F32)
        tot1 = jnp.sum(o1.astype(F32), axis=0, keepdims=True)
        tot2 = jnp.sum(o2.astype(F32), axis=0, keepdims=True)
        carry = carry_ref[0:1, :]
        rank1 = jnp.sum(jnp.where(oh1, c1 + carry, 0.0), axis=-1, keepdims=True)
        rank2 = jnp.sum(jnp.where(oh2, c2 + carry + tot1, 0.0), axis=-1, keepdims=True)
        rank_ref[i] = jnp.where(lane == 0, rank1, jnp.where(lane == 1, rank2, 0.0))
        carry_ref[0:1, :] = carry + tot1 + tot2
        dest_ref[...] = jnp.zeros_like(dest_ref)
        pend_ref[...] = jnp.zeros_like(pend_ref)

    @pl.when(ph == 1)
    def _():
        counts = carry_ref[0:1, :]
        padded = jnp.floor((counts + (EXPERT_ROWS - 1)) * (1.0 / EXPERT_ROWS)) * EXPERT_ROWS
        upper = (_iota((LANES, LANES), 0) <= _iota((LANES, LANES), 1)).astype(BF16)
        pend = _dot3_lhs(jnp.broadcast_to(padded, (8, LANES)), upper)[0:1, :]
        pstart = pend - padded
        rank = rank_ref[i]
        d1 = jnp.sum(jnp.where(oh1, pstart, 0.0), axis=-1, keepdims=True) + rank[:, 0:1]
        d2 = jnp.sum(jnp.where(oh2, pstart, 0.0), axis=-1, keepdims=True) + rank[:, 1:2]
        dest_ref[...] = jnp.where(lane == 0, d1, jnp.where(lane == 1, d2, 0.0)).astype(jnp.int32)
        pend_ref[...] = jnp.broadcast_to(pend, (8, LANES)).astype(jnp.int32)


def _slots(route):
    T = route.shape[0]
    tm = SLOT_TM
    nt = T // tm
    return pl.pallas_call(
        _slots_kernel,
        grid=(2, nt),
        in_specs=[pl.BlockSpec((tm, LANES), lambda ph, i: (i, 0))],
        out_specs=[pl.BlockSpec((tm, LANES), lambda ph, i: (i * ph, 0)),
                   pl.BlockSpec((8, LANES), lambda ph, i: (0, 0))],
        out_shape=[jax.ShapeDtypeStruct((T, LANES), jnp.int32), jax.ShapeDtypeStruct((8, LANES), jnp.int32)],
        scratch_shapes=[pltpu.VMEM((8, LANES), F32), pltpu.VMEM((nt, tm, LANES), F32)],
        compiler_params=pltpu.CompilerParams(dimension_semantics=("arbitrary", "arbitrary"),
                                             vmem_limit_bytes=VMEM_LIMIT),
        name="slots",
    )(route)


def _dispatch_kernel(pend_ref, dest_ref, x_ref, xs_ref, zero_ref, sem, zsem):
    tm = dest_ref.shape[0] // TOP_K

    @pl.when(pl.program_id(0) == 0)
    def _():
        zero_ref[...] = jnp.zeros_like(zero_ref)

        def tail(e):
            start = pl.multiple_of(jnp.maximum(pend_ref[e] - EXPERT_ROWS, 0), EXPERT_ROWS)
            return pltpu.make_async_copy(zero_ref, xs_ref.at[pl.ds(start, EXPERT_ROWS)], zsem)

        def unused(j):
            return pltpu.make_async_copy(
                zero_ref, xs_ref.at[pl.ds(pl.multiple_of(j * EXPERT_ROWS, EXPERT_ROWS), EXPERT_ROWS)], zsem)

        def start_unused(j, _):
            unused(j).start()
            return 0

        def wait_unused(j, _):
            unused(j).wait()
            return 0

        first_unused = pend_ref[N_EXPERTS - 1] // EXPERT_ROWS
        n_blocks = xs_ref.shape[0] // EXPERT_ROWS
        for e in range(N_EXPERTS):
            tail(e).start()
        lax.fori_loop(first_unused, n_blocks, start_unused, 0)
        for e in range(N_EXPERTS):
            tail(e).wait()
        lax.fori_loop(first_unused, n_blocks, wait_unused, 0)

    def start(t, _):
        for j in range(TOP_K):
            pltpu.make_async_copy(x_ref.at[pl.ds(t, 1)], xs_ref.at[pl.ds(dest_ref[TOP_K * t + j], 1)], sem).start()
        return 0

    lax.fori_loop(0, tm, start, 0, unroll=DMA_UNROLL)
    for j in range(TOP_K):
        pltpu.make_async_copy(x_ref, xs_ref.at[pl.ds(0, tm)], sem).wait()


def _dispatch(pend, dest_flat, x1, n_rows):
    T = x1.shape[0]
    tm = DISPATCH_TM
    return pl.pallas_call(
        _dispatch_kernel,
        grid_spec=pltpu.PrefetchScalarGridSpec(
            num_scalar_prefetch=1,
            grid=(T // tm,),
            in_specs=[pl.BlockSpec((TOP_K * tm,), lambda i, pe: (i,), memory_space=pltpu.SMEM),
                      pl.BlockSpec((tm, D_MODEL), lambda i, pe: (i, 0))],
            out_specs=pl.BlockSpec(memory_space=pl.ANY),
            scratch_shapes=[pltpu.VMEM((EXPERT_ROWS, D_MODEL), F32), pltpu.SemaphoreType.DMA,
                            pltpu.SemaphoreType.DMA],
        ),
        out_shape=jax.ShapeDtypeStruct((n_rows, D_MODEL), F32),
        compiler_params=pltpu.CompilerParams(dimension_semantics=("arbitrary",), vmem_limit_bytes=VMEM_LIMIT),
        name="dispatch",
    )(pend, dest_flat, x1)


def _block_expert(pend_ref, j):
    e = jnp.int32(0)
    for i in range(N_EXPERTS):
        e = e + (pend_ref[i] <= j * EXPERT_ROWS).astype(jnp.int32)
    return jnp.minimum(e, N_EXPERTS - 1)


def _last_used_block(pend_ref, j):
    return jnp.minimum(j, pend_ref[N_EXPERTS - 1] // EXPERT_ROWS - 1)


def _experts_kernel(pend_ref, xs_ref, wg_ref, wu_ref, wd_ref, y_ref, wgb_ref, wub_ref, wdb_ref):
    j = pl.program_id(0)
    used = j * EXPERT_ROWS < pend_ref[N_EXPERTS - 1]
    new_expert = (j == 0) | (_block_expert(pend_ref, j) != _block_expert(pend_ref, j - 1))

    @pl.when(used & new_expert)
    def _():
        wgb_ref[...] = wg_ref[0].astype(BF16)
        wub_ref[...] = wu_ref[0].astype(BF16)
        wdb_ref[...] = wd_ref[0].astype(BF16)

    @pl.when(used)
    def _():
        xb = xs_ref[...].astype(BF16)
        hg = jnp.dot(xb, wgb_ref[...], preferred_element_type=F32)
        hu = jnp.dot(xb, wub_ref[...], preferred_element_type=F32)
        hid = hg * _sigmoid(hg) * hu
        y_ref[...] = jnp.dot(hid.astype(BF16), wdb_ref[...], preferred_element_type=F32)

    @pl.when(jnp.logical_not(used))
    def _():
        y_ref[...] = jnp.zeros_like(y_ref)


def _experts(pend, xs, wg, wu, wd):
    n_rows = xs.shape[0]
    rows = EXPERT_ROWS
    wspec = lambda shape: pl.BlockSpec(
        (1,) + shape, lambda j, pe: (_block_expert(pe, _last_used_block(pe, j)), 0, 0))
    rspec = pl.BlockSpec((rows, D_MODEL), lambda j, pe: (_last_used_block(pe, j), 0))
    return pl.pallas_call(
        _experts_kernel,
        grid_spec=pltpu.PrefetchScalarGridSpec(
            num_scalar_prefetch=1,
            grid=(n_rows // rows,),
            in_specs=[rspec, wspec((D_MODEL, D_EXPERT)), wspec((D_MODEL, D_EXPERT)), wspec((D_EXPERT, D_MODEL))],
            out_specs=pl.BlockSpec((rows, D_MODEL), lambda j, pe: (j, 0)),
            scratch_shapes=[pltpu.VMEM((D_MODEL, D_EXPERT), BF16), pltpu.VMEM((D_MODEL, D_EXPERT), BF16),
                            pltpu.VMEM((D_EXPERT, D_MODEL), BF16)],
        ),
        out_shape=jax.ShapeDtypeStruct((n_rows, D_MODEL), F32),
        compiler_params=pltpu.CompilerParams(dimension_semantics=("arbitrary",), vmem_limit_bytes=VMEM_LIMIT),
        name="experts",
    )(pend, xs, wg, wu, wd)


def _combine_kernel(dest_ref, dest_next_ref, ys_ref, base_ref, route_ref, lg_ref, lb_ref, o_ref, buf_ref, sem):
    tm = base_ref.shape[0]
    i = pl.program_id(0)
    slot = i % 2

    def gather(dref, s):
        def start(t, _):
            for j in range(TOP_K):
                pltpu.make_async_copy(ys_ref.at[pl.ds(dref[TOP_K * t + j], 1)], buf_ref.at[s, j, pl.ds(t, 1)],
                                      sem.at[s]).start()
            return 0
        lax.fori_loop(0, tm, start, 0, unroll=DMA_UNROLL)

    @pl.when(i == 0)
    def _():
        gather(dest_ref, 0)

    @pl.when(i + 1 < pl.num_programs(0))
    def _():
        gather(dest_next_ref, 1 - slot)

    for j in range(TOP_K):
        pltpu.make_async_copy(ys_ref.at[pl.ds(0, tm)], buf_ref.at[slot, j], sem.at[slot]).wait()
    route = route_ref[...]
    ffn = buf_ref[slot, 0] * route[:, 2:3] + buf_ref[slot, 1] * route[:, 3:4]
    o_ref[...] = _layer_norm(base_ref[...] + ffn, lg_ref[...], lb_ref[...], LN_EPS)


def _combine(dest_flat, ys, base, route, l2g, l2b):
    T = base.shape[0]
    tm = COMBINE_TM
    nt = T // tm
    return pl.pallas_call(
        _combine_kernel,
        grid=(nt,),
        in_specs=[pl.BlockSpec((TOP_K * tm,), lambda i: (i,), memory_space=pltpu.SMEM),
                  pl.BlockSpec((TOP_K * tm,), lambda i: (jnp.minimum(i + 1, nt - 1),), memory_space=pltpu.SMEM),
                  pl.BlockSpec(memory_space=pl.ANY),
                  pl.BlockSpec((tm, D_MODEL), lambda i: (i, 0)), pl.BlockSpec((tm, LANES), lambda i: (i, 0)),
                  pl.BlockSpec((1, D_MODEL), lambda i: (0, 0)), pl.BlockSpec((1, D_MODEL), lambda i: (0, 0))],
        out_specs=pl.BlockSpec((tm, D_MODEL), lambda i: (i, 0)),
        out_shape=jax.ShapeDtypeStruct((T, D_MODEL), F32),
        scratch_shapes=[pltpu.VMEM((2, TOP_K, tm, D_MODEL), F32), pltpu.SemaphoreType.DMA((2,))],
        compiler_params=pltpu.CompilerParams(dimension_semantics=("arbitrary",), vmem_limit_bytes=VMEM_LIMIT),
        name="combine",
    )(dest_flat, dest_flat, ys, base, route, l2g, l2b)


def _block_diag_const(n, blk, val):
    idx = jnp.arange(n) // blk
    return jnp.where(idx[:, None] == idx[None, :], val, 0.0).astype(BF16)


def kernel(x, p, ln_emb_g, ln_emb_b, w_in, mu_shift, w0, w_decay_up, a0, w_iclr_up, w_gate_up, k_k, k_a, r_k, gn_g, gn_b, gmlp_ln_g, gmlp_ln_b, w_spatial, b_spatial, w_out, ln1_g, ln1_b, w_group_router, b_group_router, w_expert_router, b_expert_router, w_exp_gate, w_exp_up, w_exp_down, w_ple_gate, b_ple_gate, w_ple_proj, ln2_g, ln2_b):
    B, S, D = x.shape
    T = B * S
    row = lambda t: t.reshape(1, -1).astype(F32)

    zl = jnp.zeros((DECAY_LORA, D_RWKV), F32)
    wwa = jnp.concatenate([jnp.concatenate([w_decay_up[0], zl], axis=1),
                           jnp.concatenate([zl, w_iclr_up[0]], axis=1)], axis=0).astype(BF16)
    w0a0 = jnp.concatenate([w0[0], a0[0]]).reshape(1, -1)
    eones = _block_diag_const(D_RWKV, HEAD, 1.0)
    emean = _block_diag_const(LANES, HEAD, 1.0 / HEAD)

    r, lw, k, v, a, b, g, bonus, yb = _prep(
        x, row(ln_emb_g), row(ln_emb_b), w_in[0].astype(BF16), row(mu_shift[0]), wwa, w0a0,
        w_gate_up[0].astype(BF16), row(k_k[0]), row(k_a[0]), row(r_k[0]), eones,
        row(gmlp_ln_g[0]), row(gmlp_ln_b[0]), w_spatial[0], b_spatial[0].T)

    ya = _wkv(r, lw, k, v, a, b, g, bonus, gn_g[0].reshape(N_PAIRS, 1, LANES), gn_b[0].reshape(N_PAIRS, 1, LANES),
              emean)

    wr = jnp.concatenate([w_expert_router[0].reshape(D, N_EXPERTS), w_group_router[0],
                          jnp.zeros((D, LANES - N_EXPERTS - N_GROUPS), F32)], axis=1)
    wr3 = jnp.stack(_split3(wr))
    br = jnp.concatenate([b_expert_router[0].reshape(-1), b_group_router[0],
                          jnp.zeros((LANES - N_EXPERTS - N_GROUPS,), F32)]).reshape(1, LANES)
    base, x1, route = _mixer(x, row(ln_emb_g), row(ln_emb_b), ya, yb, w_out[0].astype(BF16), row(ln1_g[0]),
                             row(ln1_b[0]), wr3, br, p[0], w_ple_gate[0].astype(BF16), row(b_ple_gate[0]),
                             w_ple_proj[0].astype(BF16))
    base = base.reshape(T, D)
    x1 = x1.reshape(T, D)
    route = route.reshape(T, LANES)

    dest, pend = _slots(route)
    dest_flat = dest[:, :TOP_K].reshape(T * TOP_K)
    pend = pend[0, :N_EXPERTS]

    n_blocks = -(-(T * TOP_K) // EXPERT_ROWS) + N_EXPERTS
    xs = _dispatch(pend, dest_flat, x1, n_blocks * EXPERT_ROWS)
    ys = _experts(pend, xs, w_exp_gate[0], w_exp_up[0], w_exp_down[0])
    out = _combine(dest_flat, ys, base, route, row(ln2_g[0]), row(ln2_b[0]))
    return out.reshape(B, S, D)
```

```python
import functools
import math

import jax
import jax.numpy as jnp
from jax import lax
from jax.experimental import pallas as pl
from jax.experimental.pallas import tpu as pltpu

F32 = jnp.float32
BF16 = jnp.bfloat16

D_MODEL = 1024
D_RWKV = 512
HEAD = 64
D_GMLP = 512
GMLP_GROUPS = 4
GROUP_W = 128
GCHUNK = 128
DECAY_LORA = 64
ICLR_LORA = 64
GATE_LORA = 128
N_SHIFT = 3 * D_RWKV + DECAY_LORA + ICLR_LORA + GATE_LORA
D_IN = N_SHIFT + 2 * D_GMLP
D_PLE = 256
N_GROUPS = 4
EXPERTS_PER_GROUP = 8
N_EXPERTS = 32
TOP_K = 2
D_EXPERT = 512
DEPTH = 1
ALPHA = (2.0 * DEPTH) ** 0.25
LN_EPS = 1e-5
GN_EPS = 64e-5
DECAY_SCALE = math.exp(-0.5)

LANES = 128
WKV_CHUNK = 64
N_PAIRS = D_RWKV // LANES
VMEM_LIMIT = 56 * 1024 * 1024

PREP_TM = 256
WKV_TB = 256
WKV_PAIRS = 4
MIX_TM = 256
SLOT_TM = 512
EXPERT_ROWS = 256
DISPATCH_TM = 1024
COMBINE_TM = 256
DMA_UNROLL = 4
NEG = -1e30


def _dot(a, b):
    return jnp.dot(a.astype(BF16), b.astype(BF16), preferred_element_type=F32)


def _dot_nt(a, b):
    return lax.dot_general(a.astype(BF16), b.astype(BF16), (((1,), (1,)), ((), ())),
                           preferred_element_type=F32)


def _split3(x):
    hi = x.astype(BF16)
    r1 = x - hi.astype(F32)
    mid = r1.astype(BF16)
    lo = (r1 - mid.astype(F32)).astype(BF16)
    return hi, mid, lo


def _dot3_lhs(x, w):
    hi, mid, lo = _split3(x)
    w = w.astype(BF16)
    return (jnp.dot(hi, w, preferred_element_type=F32) + jnp.dot(mid, w, preferred_element_type=F32)
            + jnp.dot(lo, w, preferred_element_type=F32))


def _split2(x):
    hi = x.astype(BF16)
    return hi, (x - hi.astype(F32)).astype(BF16)


def _dot2_lhs(x, w2):
    hi, lo = _split2(x)
    return jnp.dot(jnp.concatenate([hi, lo], axis=1), w2, preferred_element_type=F32)


def _dot3_rhs(w3, x):
    hi, mid, lo = _split3(x)
    return jnp.dot(w3, jnp.concatenate([hi, mid, lo], axis=0), preferred_element_type=F32)


def _layer_norm(x, g, b, eps):
    mu = jnp.mean(x, axis=-1, keepdims=True)
    xc = x - mu
    var = jnp.mean(xc * xc, axis=-1, keepdims=True)
    return xc * lax.rsqrt(var + eps) * g + b


def _sigmoid(x):
    return 1.0 / (1.0 + jnp.exp(-x))


def _iota(shape, dim):
    return lax.broadcasted_iota(jnp.int32, shape, dim)


def _prep_kernel(x_ref, lng_ref, lnb_ref, win_ref, mu_ref, wwa_ref, w0a0_ref, wg_ref, kk_ref, ka_ref, rk_ref,
                 eones_ref, glng_ref, glnb_ref, wsp_ref, bsp_ref,
                 r_ref, lw_ref, k_ref, v_ref, a_ref, b_ref, g_ref, bonus_ref, yb_ref, carry_ref):
    tm = x_ref.shape[1]

    @pl.when(pl.program_id(1) == 0)
    def _():
        carry_ref[...] = jnp.zeros_like(carry_ref)

    x0 = _layer_norm(x_ref[0], lng_ref[...], lnb_ref[...], LN_EPS)
    proj = jnp.dot(x0.astype(BF16), win_ref[...], preferred_element_type=F32)

    h = proj[:, :N_SHIFT]
    rolled = pltpu.roll(h, 1, 0)
    first = _iota((tm, N_SHIFT), 0) == 0
    prev = jnp.where(first, jnp.broadcast_to(carry_ref[0:1, :], (tm, N_SHIFT)), rolled)
    carry_ref[0:1, :] = h[tm - 1:tm, :]
    h = h + (prev - h) * mu_ref[...]

    r = h[:, 0:D_RWKV]
    k = h[:, D_RWKV:2 * D_RWKV]
    v = h[:, 2 * D_RWKV:3 * D_RWKV]
    xwa = h[:, 3 * D_RWKV:3 * D_RWKV + LANES]
    xg = h[:, 3 * D_RWKV + LANES:N_SHIFT]

    lane = _iota((tm, LANES), 1)
    twa = jnp.where(lane < DECAY_LORA, jnp.tanh(xwa), xwa)
    da = _dot(twa, wwa_ref[...]) + w0a0_ref[...]
    logw = -DECAY_SCALE * _sigmoid(da[:, :D_RWKV])
    ag = _sigmoid(da[:, D_RWKV:])
    g = _dot(_sigmoid(xg), wg_ref[...])

    eones2 = eones_ref[...]

    def head_sum(t):
        half = 2 * LANES
        return jnp.concatenate([_dot2_lhs(t[:, :half], eones2), _dot2_lhs(t[:, half:], eones2)], axis=1)

    kk = k * kk_ref[...]
    kk = kk / jnp.maximum(jnp.sqrt(head_sum(kk * kk)), 1e-12)
    k = k * (1.0 + (ag - 1.0) * ka_ref[...])
    bonus = head_sum(r * k * rk_ref[...]) * v

    for p in range(N_PAIRS):
        sl = slice(p * LANES, (p + 1) * LANES)
        r_ref[0, p] = r[:, sl]
        lw_ref[0, p] = logw[:, sl]
        k_ref[0, p] = k[:, sl]
        v_ref[0, p] = v[:, sl]
        a_ref[0, p] = -kk[:, sl]
        b_ref[0, p] = (kk * ag)[:, sl]
        g_ref[0, p] = g[:, sl]
        bonus_ref[0, p] = bonus[:, sl]

    zin = proj[:, N_SHIFT:]
    z = 0.5 * zin * (1.0 + lax.erf(zin * (0.5 ** 0.5)))
    zu = z[:, :D_GMLP]
    zv = z[:, D_GMLP:]
    causal = _iota((GCHUNK, GCHUNK), 0) >= _iota((GCHUNK, GCHUNK), 1)
    for gi in range(GMLP_GROUPS):
        gs = slice(gi * GROUP_W, (gi + 1) * GROUP_W)
        zvn = _layer_norm(zv[:, gs], glng_ref[:, gs], glnb_ref[:, gs], LN_EPS)
        ws = jnp.where(causal, wsp_ref[gi], 0.0).astype(BF16)
        bcol = bsp_ref[:, gi:gi + 1]
        for c in range(tm // GCHUNK):
            ts = slice(c * GCHUNK, (c + 1) * GCHUNK)
            mixed = jnp.dot(ws, zvn[ts].astype(BF16), preferred_element_type=F32) + bcol
            yb_ref[0, ts, gs] = (zu[ts, gs] * mixed).astype(BF16)


def _prep(x, ln_g, ln_b, w_in, mu, wwa, w0a0, wg, k_k, k_a, r_k, eones, glng, glnb, wsp, bsp):
    B, S, _ = x.shape
    tm = PREP_TM
    const = lambda shape: pl.BlockSpec(shape, lambda b, s: (0,) * len(shape))
    pair_spec = pl.BlockSpec((1, N_PAIRS, tm, LANES), lambda b, s: (b, 0, s, 0))
    pair_shape = jax.ShapeDtypeStruct((B, N_PAIRS, S, LANES), F32)
    return pl.pallas_call(
        _prep_kernel,
        grid=(B, S // tm),
        in_specs=[
            pl.BlockSpec((1, tm, D_MODEL), lambda b, s: (b, s, 0)),
            const((1, D_MODEL)), const((1, D_MODEL)), const((D_MODEL, D_IN)), const((1, N_SHIFT)),
            const((LANES, 2 * D_RWKV)), const((1, 2 * D_RWKV)), const((GATE_LORA, D_RWKV)),
            const((1, D_RWKV)), const((1, D_RWKV)), const((1, D_RWKV)), const((4 * LANES, 2 * LANES)),
            const((1, D_GMLP)), const((1, D_GMLP)), const((GMLP_GROUPS, GCHUNK, GCHUNK)),
            const((GCHUNK, GMLP_GROUPS)),
        ],
        out_specs=[pair_spec] * 8 + [pl.BlockSpec((1, tm, D_GMLP), lambda b, s: (b, s, 0))],
        out_shape=[pair_shape] * 8 + [jax.ShapeDtypeStruct((B, S, D_GMLP), BF16)],
        scratch_shapes=[pltpu.VMEM((8, N_SHIFT), F32)],
        compiler_params=pltpu.CompilerParams(dimension_semantics=("arbitrary", "arbitrary"),
                                             vmem_limit_bytes=VMEM_LIMIT),
        name="prep",
    )(x, ln_g, ln_b, w_in, mu, wwa, w0a0, wg, k_k, k_a, r_k, eones, glng, glnb, wsp, bsp)


def _wkv_kernel(r_ref, lw_ref, k_ref, v_ref, a_ref, b_ref, g_ref, bonus_ref, gng_ref, gnb_ref, emean_ref,
                o_ref, h_ref):
    C = WKV_CHUNK
    tb = r_ref.shape[2]

    @pl.when(pl.program_id(2) == 0)
    def _():
        h_ref[...] = jnp.zeros_like(h_ref)

    tok = _iota((C, LANES), 0)
    lane = _iota((C, LANES), 1)
    head0 = lane < HEAD
    strict = tok > lane % HEAD
    incl = tok >= lane % HEAD
    eye_w = (tok == lane % HEAD).astype(F32)
    rr = _iota((LANES, LANES), 0)
    cc = _iota((LANES, LANES), 1)
    eye = (rr == cc).astype(F32)
    same_head = (rr < HEAD) == (cc < HEAD)
    ltri3 = (_iota((C, 3 * C), 0) >= _iota((C, 3 * C), 1) % C).astype(BF16)

    def stack(x):
        xb = x.astype(BF16)
        zero = jnp.zeros_like(xb)
        return jnp.concatenate([jnp.where(head0, xb, zero), jnp.where(head0, zero, xb)], axis=0)

    def stack2(x, y):
        return jnp.concatenate([stack(x), stack(y)], axis=1)

    n_pairs = r_ref.shape[1]
    n_chunks = tb // C
    units = [(q, c) for q in range(n_pairs) for c in range(n_chunks)]

    def load(ref):
        return [ref[0, q, c * C:(c + 1) * C, :] for q, c in units]

    r_, lw_, k_, v_, a_, b_ = (load(ref) for ref in (r_ref, lw_ref, k_ref, v_ref, a_ref, b_ref))
    cum_ = [_dot3_rhs(ltri3, lw) for lw in lw_]
    cend_ = [cum[C - 1:C, :] for cum in cum_]
    at_ = [a * jnp.exp(cum - lw) for a, cum, lw in zip(a_, cum_, lw_)]
    rt_ = [r * jnp.exp(cum) for r, cum in zip(r_, cum_)]
    ginv_ = [jnp.exp(-cum) for cum in cum_]
    gend_ = [jnp.exp(cend - cum) for cend, cum in zip(cend_, cum_)]
    bk_end_ = [jnp.concatenate([b * ge, k * ge], axis=0) for b, k, ge in zip(b_, k_, gend_)]
    vst_ = [stack(v) for v in v_]

    G_ = [_dot_nt(jnp.concatenate([at, rt], axis=0), jnp.concatenate([stack(b * gi), stack(k * gi)], axis=0))
          for at, rt, b, k, gi in zip(at_, rt_, b_, k_, ginv_)]
    n1_ = [jnp.where(strict, G[:C, :LANES], 0.0) for G in G_]
    aak_ = [jnp.where(strict, G[:C, LANES:], 0.0) for G in G_]
    arb_ = [jnp.where(incl, G[C:, :LANES], 0.0) for G in G_]
    ark_ = [jnp.where(incl, G[C:, LANES:], 0.0) for G in G_]
    av_ = [_dot(jnp.concatenate([aak, ark], axis=0), vst) for aak, ark, vst in zip(aak_, ark_, vst_)]

    s1_ = [stack(n1) for n1 in n1_]
    n2_ = [_dot(n1, s1) for n1, s1 in zip(n1_, s1_)]
    x_ = [_dot(n2, jnp.concatenate([s1, stack(n2)], axis=1)) for n2, s1 in zip(n2_, s1_)]
    t_ = [eye_w + n1 + n2 + x[:, :LANES] for n1, n2, x in zip(n1_, n2_, x_)]
    np_ = [x[:, LANES:] for x in x_]
    for _ in range(3):
        x_ = [_dot(npow, stack2(t, npow)) for t, npow in zip(t_, np_)]
        t_ = [t + x[:, :LANES] for t, x in zip(t_, x_)]
        np_ = [x[:, LANES:] for x in x_]
    t_ = [t + _dot(npow, stack(t)) for t, npow in zip(t_, np_)]

    x_ = [_dot(t, stack2(at, av[:C])) for t, at, av in zip(t_, at_, av_)]
    z_ = [_dot(arb, stack2(x[:, :LANES], x[:, LANES:])) for arb, x in zip(arb_, x_)]
    rp_ = [rt + z[:, :LANES] for rt, z in zip(rt_, z_)]
    p3_ = [z[:, LANES:] + av[C:] for z, av in zip(z_, av_)]
    rhs_ = [jnp.concatenate([x, jnp.concatenate([jnp.zeros_like(v), v], axis=1)], axis=0)
            for x, v in zip(x_, v_)]
    mq_ = [_dot(bk_end.T, rhs) for bk_end, rhs in zip(bk_end_, rhs_)]
    m_ = [eye * jnp.exp(cend) + jnp.where(same_head, mq[:, :LANES], 0.0) for cend, mq in zip(cend_, mq_)]
    q_ = [jnp.where(same_head, mq[:, LANES:], 0.0) for mq in mq_]

    emean = emean_ref[...]
    for q in range(n_pairs):
        H = h_ref[q]
        ys = []
        for c in range(n_chunks):
            u = q * n_chunks + c
            ys.append(_dot(rp_[u], H) + p3_[u])
            H = _dot(m_[u], H) + q_[u]
        h_ref[q] = H
        y = jnp.concatenate(ys, axis=0)
        mu = _dot2_lhs(y, emean)
        yc = y - mu
        var = _dot2_lhs(yc * yc, emean)
        yn = yc * lax.rsqrt(var + GN_EPS) * gng_ref[q] + gnb_ref[q]
        o_ref[0, q] = ((yn + bonus_ref[0, q]) * g_ref[0, q]).astype(BF16)


def _wkv(r, lw, k, v, a, b, g, bonus, gn_g, gn_b, emean):
    B, P, S, _ = r.shape
    tb = WKV_TB
    pp = WKV_PAIRS
    seq = pl.BlockSpec((1, pp, tb, LANES), lambda bi, p, s: (bi, p, s, 0))
    par = pl.BlockSpec((pp, 1, LANES), lambda bi, p, s: (p, 0, 0))
    return pl.pallas_call(
        _wkv_kernel,
        grid=(B, P // pp, S // tb),
        in_specs=[seq] * 8 + [par, par, pl.BlockSpec((2 * LANES, LANES), lambda bi, p, s: (0, 0))],
        out_specs=seq,
        out_shape=jax.ShapeDtypeStruct((B, P, S, LANES), BF16),
        scratch_shapes=[pltpu.VMEM((pp, LANES, LANES), F32)],
        compiler_params=pltpu.CompilerParams(dimension_semantics=("arbitrary", "arbitrary", "arbitrary"),
                                             vmem_limit_bytes=VMEM_LIMIT),
        name="wkv",
    )(r, lw, k, v, a, b, g, bonus, gn_g, gn_b, emean)


def _mixer_kernel(x_ref, lng_ref, lnb_ref, ya_ref, yb_ref, wout_ref, l1g_ref, l1b_ref, wr_ref, br_ref,
                  p_ref, wpg_ref, bpg_ref, wpp_ref, base_ref, x1_ref, route_ref):
    tm = x_ref.shape[1]
    x0 = _layer_norm(x_ref[0], lng_ref[...], lnb_ref[...], LN_EPS)
    ymix = jnp.concatenate([ya_ref[0, p] for p in range(N_PAIRS)] + [yb_ref[0]], axis=-1)
    mix = jnp.dot(ymix, wout_ref[...], preferred_element_type=F32)
    x1 = _layer_norm(ALPHA * x0 + mix, l1g_ref[...], l1b_ref[...], LN_EPS)
    x1_ref[0] = x1

    hi, mid = _split2(x1)
    whi = wr_ref[0]
    wmid = wr_ref[1]
    d = lambda u, w: jnp.dot(u, w, preferred_element_type=F32)
    logits = (d(hi, whi) + d(hi, wmid) + d(mid, whi)) + br_ref[...]
    lane = _iota((tm, LANES), 1).astype(F32)
    far = float(4 * LANES)
    is_g = jnp.where(lane >= N_EXPERTS, jnp.where(lane < N_EXPERTS + N_GROUPS, 1.0, 0.0), 0.0) > 0.5
    gl = jnp.where(is_g, logits, NEG)
    gmax = jnp.max(gl, axis=-1, keepdims=True)
    gsel = jnp.min(jnp.where(gl == gmax, lane, far), axis=-1, keepdims=True) - N_EXPERTS
    p_group = 1.0 / jnp.sum(jnp.where(is_g, jnp.exp(gl - gmax), 0.0), axis=-1, keepdims=True)
    grp_of_lane = jnp.floor(lane * (1.0 / EXPERTS_PER_GROUP))
    el = jnp.where(grp_of_lane == gsel, logits, NEG)
    v1 = jnp.max(el, axis=-1, keepdims=True)
    i1 = jnp.min(jnp.where(el == v1, lane, far), axis=-1, keepdims=True)
    el2 = jnp.where(lane == i1, NEG, el)
    v2 = jnp.max(el2, axis=-1, keepdims=True)
    i2 = jnp.min(jnp.where(el2 == v2, lane, far), axis=-1, keepdims=True)
    e21 = jnp.exp(v2 - v1)
    w1 = p_group / (1.0 + e21)
    w2 = p_group * e21 / (1.0 + e21)
    route_ref[0] = jnp.where(lane == 0, i1, jnp.where(lane == 1, i2, jnp.where(lane == 2, w1,
                                                                               jnp.where(lane == 3, w2, 0.0))))

    gate = _sigmoid(jnp.dot(x1.astype(BF16), wpg_ref[...], preferred_element_type=F32) + bpg_ref[...])
    ple = gate * jnp.dot(p_ref[0].astype(BF16), wpp_ref[...], preferred_element_type=F32)
    base_ref[0] = ALPHA * x1 + ple


def _mixer(x, ln_g, ln_b, ya, yb, w_out, l1g, l1b, wr3, br, p, wpg, bpg, wpp):
    B, S, _ = x.shape
    tm = MIX_TM
    const = lambda shape: pl.BlockSpec(shape, lambda b, s: (0,) * len(shape))
    row = lambda w: pl.BlockSpec((1, tm, w), lambda b, s: (b, s, 0))
    return pl.pallas_call(
        _mixer_kernel,
        grid=(B, S // tm),
        in_specs=[
            row(D_MODEL), const((1, D_MODEL)), const((1, D_MODEL)),
            pl.BlockSpec((1, N_PAIRS, tm, LANES), lambda b, s: (b, 0, s, 0)), row(D_GMLP),
            const((D_MODEL, D_MODEL)), const((1, D_MODEL)), const((1, D_MODEL)),
            const((2, D_MODEL, LANES)), const((1, LANES)),
            row(D_PLE), const((D_MODEL, D_MODEL)), const((1, D_MODEL)), const((D_PLE, D_MODEL)),
        ],
        out_specs=[row(D_MODEL), row(D_MODEL), row(LANES)],
        out_shape=[jax.ShapeDtypeStruct((B, S, D_MODEL), F32), jax.ShapeDtypeStruct((B, S, D_MODEL), F32),
                   jax.ShapeDtypeStruct((B, S, LANES), F32)],
        compiler_params=pltpu.CompilerParams(dimension_semantics=("arbitrary", "arbitrary"),
                                             vmem_limit_bytes=VMEM_LIMIT),
        name="mixer",
    )(x, ln_g, ln_b, ya, yb, w_out, l1g, l1b, wr3, br, p, wpg, bpg, wpp)


def _slots_kernel(route_ref, dest_ref, pend_ref, bexp_ref, carry_ref, rank_ref):
    ph = pl.program_id(0)
    i = pl.program_id(1)
    tm = route_ref.shape[0]
    lane = _iota((tm, LANES), 1)
    route = route_ref[...]
    e1 = route[:, 0:1].astype(jnp.int32)
    e2 = route[:, 1:2].astype(jnp.int32)
    oh1 = lane == e1
    oh2 = lane == e2

    @pl.when((ph == 0) & (i == 0))
    def _():
        carry_ref[...] = jnp.zeros_like(carry_ref)

    @pl.when(ph == 0)
    def _():
        below = (_iota((tm, tm), 0) > _iota((tm, tm), 1)).astype(BF16)
        o1 = oh1.astype(BF16)
        o2 = oh2.astype(BF16)
        c1 = jnp.dot(below, o1, preferred_element_type=F32)
        c2 = jnp.dot(below, o2, preferred_element_type=F32)
        tot1 = jnp.sum(o1.astype(F32), axis=0, keepdims=True)
        tot2 = jnp.sum(o2.astype(F32), axis=0, keepdims=True)
        carry = carry_ref[0:1, :]
        rank1 = jnp.sum(jnp.where(oh1, c1 + carry, 0.0), axis=-1, keepdims=True)
        rank2 = jnp.sum(jnp.where(oh2, c2 + carry + tot1, 0.0), axis=-1, keepdims=True)
        rank_ref[i] = jnp.where(lane == 0, rank1, jnp.where(lane == 1, rank2, 0.0))
        carry_ref[0:1, :] = carry + tot1 + tot2
        dest_ref[...] = jnp.zeros_like(dest_ref)
        pend_ref[...] = jnp.zeros_like(pend_ref)
        bexp_ref[...] = jnp.zeros_like(bexp_ref)

    @pl.when(ph == 1)
    def _():
        counts = carry_ref[0:1, :]
        padded = jnp.floor((counts + (EXPERT_ROWS - 1)) * (1.0 / EXPERT_ROWS)) * EXPERT_ROWS
        upper = (_iota((LANES, LANES), 0) <= _iota((LANES, LANES), 1)).astype(BF16)
        pend = _dot3_lhs(jnp.broadcast_to(padded, (8, LANES)), upper)[0:1, :]
        pstart = pend - padded
        rank = rank_ref[i]
        d1 = jnp.sum(jnp.where(oh1, pstart, 0.0), axis=-1, keepdims=True) + rank[:, 0:1]
        d2 = jnp.sum(jnp.where(oh2, pstart, 0.0), axis=-1, keepdims=True) + rank[:, 1:2]
        dest_ref[...] = jnp.where(lane == 0, d1, jnp.where(lane == 1, d2, 0.0)).astype(jnp.int32)
        pend_ref[...] = jnp.broadcast_to(pend, (8, LANES)).astype(jnp.int32)
        nb = bexp_ref.shape[0]
        last_used = pend[:, N_EXPERTS - 1:N_EXPERTS] * (1.0 / EXPERT_ROWS) - 1.0
        first_row = jnp.minimum(_iota((nb, LANES), 0).astype(F32), last_used) * EXPERT_ROWS
        ended = jnp.where(_iota((nb, LANES), 1) < N_EXPERTS, jnp.where(pend <= first_row, 1.0, 0.0), 0.0)
        bexp = jnp.minimum(jnp.sum(ended, axis=-1, keepdims=True), N_EXPERTS - 1.0)
        bexp_ref[...] = jnp.broadcast_to(bexp, (nb, LANES)).astype(jnp.int32)


def _slots(route, n_blocks):
    T = route.shape[0]
    tm = SLOT_TM
    nt = T // tm
    nb = -(-n_blocks // 8) * 8
    return pl.pallas_call(
        _slots_kernel,
        grid=(2, nt),
        in_specs=[pl.BlockSpec((tm, LANES), lambda ph, i: (i, 0))],
        out_specs=[pl.BlockSpec((tm, LANES), lambda ph, i: (i * ph, 0)),
                   pl.BlockSpec((8, LANES), lambda ph, i: (0, 0)),
                   pl.BlockSpec((nb, LANES), lambda ph, i: (0, 0))],
        out_shape=[jax.ShapeDtypeStruct((T, LANES), jnp.int32), jax.ShapeDtypeStruct((8, LANES), jnp.int32),
                   jax.ShapeDtypeStruct((nb, LANES), jnp.int32)],
        scratch_shapes=[pltpu.VMEM((8, LANES), F32), pltpu.VMEM((nt, tm, LANES), F32)],
        compiler_params=pltpu.CompilerParams(dimension_semantics=("arbitrary", "arbitrary"),
                                             vmem_limit_bytes=VMEM_LIMIT),
        name="slots",
    )(route)


def _dispatch_kernel(pend_ref, dest_ref, x_ref, xs_ref, zero_ref, sem, zsem):
    tm = dest_ref.shape[0] // TOP_K

    @pl.when(pl.program_id(0) == 0)
    def _():
        zero_ref[...] = jnp.zeros_like(zero_ref)

        def tail(e):
            start = pl.multiple_of(jnp.maximum(pend_ref[e] - EXPERT_ROWS, 0), EXPERT_ROWS)
            return pltpu.make_async_copy(zero_ref, xs_ref.at[pl.ds(start, EXPERT_ROWS)], zsem)

        def unused(j):
            return pltpu.make_async_copy(
                zero_ref, xs_ref.at[pl.ds(pl.multiple_of(j * EXPERT_ROWS, EXPERT_ROWS), EXPERT_ROWS)], zsem)

        def start_unused(j, _):
            unused(j).start()
            return 0

        def wait_unused(j, _):
            unused(j).wait()
            return 0

        first_unused = pend_ref[N_EXPERTS - 1] // EXPERT_ROWS
        n_blocks = xs_ref.shape[0] // EXPERT_ROWS
        for e in range(N_EXPERTS):
            tail(e).start()
        lax.fori_loop(first_unused, n_blocks, start_unused, 0)
        for e in range(N_EXPERTS):
            tail(e).wait()
        lax.fori_loop(first_unused, n_blocks, wait_unused, 0)

    def start(t, _):
        for j in range(TOP_K):
            pltpu.make_async_copy(x_ref.at[pl.ds(t, 1)], xs_ref.at[pl.ds(dest_ref[TOP_K * t + j], 1)], sem).start()
        return 0

    lax.fori_loop(0, tm, start, 0, unroll=DMA_UNROLL)
    for j in range(TOP_K):
        pltpu.make_async_copy(x_ref, xs_ref.at[pl.ds(0, tm)], sem).wait()


def _dispatch(pend, dest_flat, x1, n_rows):
    T = x1.shape[0]
    tm = DISPATCH_TM
    return pl.pallas_call(
        _dispatch_kernel,
        grid_spec=pltpu.PrefetchScalarGridSpec(
            num_scalar_prefetch=1,
            grid=(T // tm,),
            in_specs=[pl.BlockSpec((TOP_K * tm,), lambda i, pe: (i,), memory_space=pltpu.SMEM),
                      pl.BlockSpec((tm, D_MODEL), lambda i, pe: (i, 0))],
            out_specs=pl.BlockSpec(memory_space=pl.ANY),
            scratch_shapes=[pltpu.VMEM((EXPERT_ROWS, D_MODEL), F32), pltpu.SemaphoreType.DMA,
                            pltpu.SemaphoreType.DMA],
        ),
        out_shape=jax.ShapeDtypeStruct((n_rows, D_MODEL), F32),
        compiler_params=pltpu.CompilerParams(dimension_semantics=("arbitrary",), vmem_limit_bytes=VMEM_LIMIT),
        name="dispatch",
    )(pend, dest_flat, x1)


def _last_used_block(pend_ref, j):
    return jnp.minimum(j, pend_ref[N_EXPERTS - 1] // EXPERT_ROWS - 1)


def _experts_kernel(pend_ref, bexp_ref, xs_ref, wg_ref, wu_ref, wd_ref, y_ref, wgu_ref, wdb_ref):
    j = pl.program_id(0)
    used = j * EXPERT_ROWS < pend_ref[N_EXPERTS - 1]
    new_expert = (j == 0) | (bexp_ref[j] != bexp_ref[jnp.maximum(j - 1, 0)])

    @pl.when(used & new_expert)
    def _():
        wgu_ref[:, :D_EXPERT] = wg_ref[0].astype(BF16)
        wgu_ref[:, D_EXPERT:] = wu_ref[0].astype(BF16)
        wdb_ref[...] = wd_ref[0].astype(BF16)

    @pl.when(used)
    def _():
        h = jnp.dot(xs_ref[...].astype(BF16), wgu_ref[...], preferred_element_type=F32)
        hg = h[:, :D_EXPERT]
        hid = hg * _sigmoid(hg) * h[:, D_EXPERT:]
        y_ref[...] = jnp.dot(hid.astype(BF16), wdb_ref[...], preferred_element_type=F32)

    @pl.when(jnp.logical_not(used))
    def _():
        y_ref[...] = jnp.zeros_like(y_ref)


def _experts(pend, bexp, xs, wg, wu, wd):
    n_rows = xs.shape[0]
    rows = EXPERT_ROWS
    wspec = lambda shape: pl.BlockSpec((1,) + shape, lambda j, pe, be: (be[j], 0, 0))
    return pl.pallas_call(
        _experts_kernel,
        grid_spec=pltpu.PrefetchScalarGridSpec(
            num_scalar_prefetch=2,
            grid=(n_rows // rows,),
            in_specs=[pl.BlockSpec((rows, D_MODEL), lambda j, pe, be: (_last_used_block(pe, j), 0)),
                      wspec((D_MODEL, D_EXPERT)), wspec((D_MODEL, D_EXPERT)), wspec((D_EXPERT, D_MODEL))],
            out_specs=pl.BlockSpec((rows, D_MODEL), lambda j, pe, be: (j, 0)),
            scratch_shapes=[pltpu.VMEM((D_MODEL, 2 * D_EXPERT), BF16), pltpu.VMEM((D_EXPERT, D_MODEL), BF16)],
        ),
        out_shape=jax.ShapeDtypeStruct((n_rows, D_MODEL), F32),
        compiler_params=pltpu.CompilerParams(dimension_semantics=("arbitrary",), vmem_limit_bytes=VMEM_LIMIT),
        name="experts",
    )(pend, bexp, xs, wg, wu, wd)


def _combine_kernel(dest_ref, dest_next_ref, ys_ref, base_ref, route_ref, lg_ref, lb_ref, o_ref, buf_ref, sem):
    tm = base_ref.shape[0]
    i = pl.program_id(0)
    slot = i % 2

    def gather(dref, s):
        def start(t, _):
            for j in range(TOP_K):
                pltpu.make_async_copy(ys_ref.at[pl.ds(dref[TOP_K * t + j], 1)], buf_ref.at[s, j, pl.ds(t, 1)],
                                      sem.at[s]).start()
            return 0
        lax.fori_loop(0, tm, start, 0, unroll=DMA_UNROLL)

    @pl.when(i == 0)
    def _():
        gather(dest_ref, 0)

    @pl.when(i + 1 < pl.num_programs(0))
    def _():
        gather(dest_next_ref, 1 - slot)

    for j in range(TOP_K):
        pltpu.make_async_copy(ys_ref.at[pl.ds(0, tm)], buf_ref.at[slot, j], sem.at[slot]).wait()
    route = route_ref[...]
    ffn = buf_ref[slot, 0] * route[:, 2:3] + buf_ref[slot, 1] * route[:, 3:4]
    o_ref[...] = _layer_norm(base_ref[...] + ffn, lg_ref[...], lb_ref[...], LN_EPS)


def _combine(dest_flat, ys, base, route, l2g, l2b):
    T = base.shape[0]
    tm = COMBINE_TM
    nt = T // tm
    return pl.pallas_call(
        _combine_kernel,
        grid=(nt,),
        in_specs=[pl.BlockSpec((TOP_K * tm,), lambda i: (i,), memory_space=pltpu.SMEM),
                  pl.BlockSpec((TOP_K * tm,), lambda i: (jnp.minimum(i + 1, nt - 1),), memory_space=pltpu.SMEM),
                  pl.BlockSpec(memory_space=pl.ANY),
                  pl.BlockSpec((tm, D_MODEL), lambda i: (i, 0)), pl.BlockSpec((tm, LANES), lambda i: (i, 0)),
                  pl.BlockSpec((1, D_MODEL), lambda i: (0, 0)), pl.BlockSpec((1, D_MODEL), lambda i: (0, 0))],
        out_specs=pl.BlockSpec((tm, D_MODEL), lambda i: (i, 0)),
        out_shape=jax.ShapeDtypeStruct((T, D_MODEL), F32),
        scratch_shapes=[pltpu.VMEM((2, TOP_K, tm, D_MODEL), F32), pltpu.SemaphoreType.DMA((2,))],
        compiler_params=pltpu.CompilerParams(dimension_semantics=("arbitrary",), vmem_limit_bytes=VMEM_LIMIT),
        name="combine",
    )(dest_flat, dest_flat, ys, base, route, l2g, l2b)


def _block_diag_const(n, blk, val):
    idx = jnp.arange(n) // blk
    return jnp.where(idx[:, None] == idx[None, :], val, 0.0).astype(BF16)


def kernel(x, p, ln_emb_g, ln_emb_b, w_in, mu_shift, w0, w_decay_up, a0, w_iclr_up, w_gate_up, k_k, k_a, r_k, gn_g, gn_b, gmlp_ln_g, gmlp_ln_b, w_spatial, b_spatial, w_out, ln1_g, ln1_b, w_group_router, b_group_router, w_expert_router, b_expert_router, w_exp_gate, w_exp_up, w_exp_down, w_ple_gate, b_ple_gate, w_ple_proj, ln2_g, ln2_b):
    B, S, D = x.shape
    T = B * S
    row = lambda t: t.reshape(1, -1).astype(F32)

    zl = jnp.zeros((DECAY_LORA, D_RWKV), F32)
    wwa = jnp.concatenate([jnp.concatenate([w_decay_up[0], zl], axis=1),
                           jnp.concatenate([zl, w_iclr_up[0]], axis=1)], axis=0).astype(BF16)
    w0a0 = jnp.concatenate([w0[0], a0[0]]).reshape(1, -1)
    eones = jnp.tile(_block_diag_const(2 * LANES, HEAD, 1.0), (2, 1))
    emean = jnp.tile(_block_diag_const(LANES, HEAD, 1.0 / HEAD), (2, 1))

    r, lw, k, v, a, b, g, bonus, yb = _prep(
        x, row(ln_emb_g), row(ln_emb_b), w_in[0].astype(BF16), row(mu_shift[0]), wwa, w0a0,
        w_gate_up[0].astype(BF16), row(k_k[0]), row(k_a[0]), row(r_k[0]), eones,
        row(gmlp_ln_g[0]), row(gmlp_ln_b[0]), w_spatial[0], b_spatial[0].T)

    ya = _wkv(r, lw, k, v, a, b, g, bonus, gn_g[0].reshape(N_PAIRS, 1, LANES), gn_b[0].reshape(N_PAIRS, 1, LANES),
              emean)

    wr = jnp.concatenate([w_expert_router[0].reshape(D, N_EXPERTS), w_group_router[0],
                          jnp.zeros((D, LANES - N_EXPERTS - N_GROUPS), F32)], axis=1)
    wr3 = jnp.stack(_split2(wr))
    br = jnp.concatenate([b_expert_router[0].reshape(-1), b_group_router[0],
                          jnp.zeros((LANES - N_EXPERTS - N_GROUPS,), F32)]).reshape(1, LANES)
    base, x1, route = _mixer(x, row(ln_emb_g), row(ln_emb_b), ya, yb, w_out[0].astype(BF16), row(ln1_g[0]),
                             row(ln1_b[0]), wr3, br, p[0], w_ple_gate[0].astype(BF16), row(b_ple_gate[0]),
                             w_ple_proj[0].astype(BF16))
    base = base.reshape(T, D)
    x1 = x1.reshape(T, D)
    route = route.reshape(T, LANES)

    n_blocks = -(-(T * TOP_K) // EXPERT_ROWS) + N_EXPERTS
    dest, pend, bexp = _slots(route, n_blocks)
    dest_flat = dest[:, :TOP_K].reshape(T * TOP_K)
    pend = pend[0, :N_EXPERTS]
    bexp = bexp[:n_blocks, 0]

    xs = _dispatch(pend, dest_flat, x1, n_blocks * EXPERT_ROWS)
    ys = _experts(pend, bexp, xs, w_exp_gate[0], w_exp_up[0], w_exp_down[0])
    out = _combine(dest_flat, ys, base, route, row(ln2_g[0]), row(ln2_b[0]))
    return out.reshape(B, S, D)
```

```python
import functools
import math

import jax
import jax.numpy as jnp
from jax import lax
from jax.experimental import pallas as pl
from jax.experimental.pallas import tpu as pltpu

F32 = jnp.float32
BF16 = jnp.bfloat16

D_MODEL = 1024
D_RWKV = 512
HEAD = 64
D_GMLP = 512
GMLP_GROUPS = 4
GROUP_W = 128
GCHUNK = 128
DECAY_LORA = 64
ICLR_LORA = 64
GATE_LORA = 128
N_SHIFT = 3 * D_RWKV + DECAY_LORA + ICLR_LORA + GATE_LORA
D_IN = N_SHIFT + 2 * D_GMLP
D_PLE = 256
N_GROUPS = 4
EXPERTS_PER_GROUP = 8
N_EXPERTS = 32
TOP_K = 2
D_EXPERT = 512
DEPTH = 1
ALPHA = (2.0 * DEPTH) ** 0.25
LN_EPS = 1e-5
GN_EPS = 64e-5
DECAY_SCALE = math.exp(-0.5)

LANES = 128
WKV_CHUNK = 64
N_PAIRS = D_RWKV // LANES
VMEM_LIMIT = 56 * 1024 * 1024

PREP_TM = 256
WKV_TB = 256
WKV_PAIRS = 4
MIX_TM = 256
SLOT_TM = 512
EXPERT_ROWS = 256
DISPATCH_TM = 1024
COMBINE_TM = 256
DMA_UNROLL = 16
NEG = -1e30


def _dot(a, b):
    return jnp.dot(a.astype(BF16), b.astype(BF16), preferred_element_type=F32)


def _dot_nt(a, b):
    return lax.dot_general(a.astype(BF16), b.astype(BF16), (((1,), (1,)), ((), ())),
                           preferred_element_type=F32)


def _split3(x):
    hi = x.astype(BF16)
    r1 = x - hi.astype(F32)
    mid = r1.astype(BF16)
    lo = (r1 - mid.astype(F32)).astype(BF16)
    return hi, mid, lo


def _dot3_lhs(x, w):
    hi, mid, lo = _split3(x)
    w = w.astype(BF16)
    return (jnp.dot(hi, w, preferred_element_type=F32) + jnp.dot(mid, w, preferred_element_type=F32)
            + jnp.dot(lo, w, preferred_element_type=F32))


def _split2(x):
    hi = x.astype(BF16)
    return hi, (x - hi.astype(F32)).astype(BF16)


def _dot2_lhs(x, w2):
    hi, lo = _split2(x)
    return jnp.dot(jnp.concatenate([hi, lo], axis=1), w2, preferred_element_type=F32)


def _dot3_rhs(w3, x):
    hi, mid, lo = _split3(x)
    return jnp.dot(w3, jnp.concatenate([hi, mid, lo], axis=0), preferred_element_type=F32)


def _layer_norm(x, g, b, eps):
    mu = jnp.mean(x, axis=-1, keepdims=True)
    xc = x - mu
    var = jnp.mean(xc * xc, axis=-1, keepdims=True)
    return xc * lax.rsqrt(var + eps) * g + b


def _sigmoid(x):
    return 1.0 / (1.0 + jnp.exp(-x))


def _iota(shape, dim):
    return lax.broadcasted_iota(jnp.int32, shape, dim)


def _prep_kernel(x_ref, lng_ref, lnb_ref, win_ref, mu_ref, wwa_ref, w0a0_ref, wg_ref, kk_ref, ka_ref, rk_ref,
                 eones_ref, glng_ref, glnb_ref, wsp_ref, bsp_ref,
                 r_ref, lw_ref, k_ref, v_ref, a_ref, b_ref, g_ref, bonus_ref, yb_ref, carry_ref):
    tm = x_ref.shape[1]

    @pl.when(pl.program_id(1) == 0)
    def _():
        carry_ref[...] = jnp.zeros_like(carry_ref)

    x0 = _layer_norm(x_ref[0], lng_ref[...], lnb_ref[...], LN_EPS)
    proj = jnp.dot(x0.astype(BF16), win_ref[...], preferred_element_type=F32)

    h = proj[:, :N_SHIFT]
    rolled = pltpu.roll(h, 1, 0)
    first = _iota((tm, N_SHIFT), 0) == 0
    prev = jnp.where(first, jnp.broadcast_to(carry_ref[0:1, :], (tm, N_SHIFT)), rolled)
    carry_ref[0:1, :] = h[tm - 1:tm, :]
    h = h + (prev - h) * mu_ref[...]

    r = h[:, 0:D_RWKV]
    k = h[:, D_RWKV:2 * D_RWKV]
    v = h[:, 2 * D_RWKV:3 * D_RWKV]
    xwa = h[:, 3 * D_RWKV:3 * D_RWKV + LANES]
    xg = h[:, 3 * D_RWKV + LANES:N_SHIFT]

    lane = _iota((tm, LANES), 1)
    twa = jnp.where(lane < DECAY_LORA, jnp.tanh(xwa), xwa)
    da = _dot(twa, wwa_ref[...]) + w0a0_ref[...]
    logw = -DECAY_SCALE * _sigmoid(da[:, :D_RWKV])
    ag = _sigmoid(da[:, D_RWKV:])
    g = _dot(_sigmoid(xg), wg_ref[...])

    eones2 = eones_ref[...]

    def head_sum(t):
        half = 2 * LANES
        return jnp.concatenate([_dot2_lhs(t[:, :half], eones2), _dot2_lhs(t[:, half:], eones2)], axis=1)

    kk = k * kk_ref[...]
    kk = kk / jnp.maximum(jnp.sqrt(head_sum(kk * kk)), 1e-12)
    k = k * (1.0 + (ag - 1.0) * ka_ref[...])
    bonus = head_sum(r * k * rk_ref[...]) * v

    for p in range(N_PAIRS):
        sl = slice(p * LANES, (p + 1) * LANES)
        r_ref[0, p] = r[:, sl]
        lw_ref[0, p] = logw[:, sl]
        k_ref[0, p] = k[:, sl]
        v_ref[0, p] = v[:, sl]
        a_ref[0, p] = -kk[:, sl]
        b_ref[0, p] = (kk * ag)[:, sl]
        g_ref[0, p] = g[:, sl]
        bonus_ref[0, p] = bonus[:, sl]

    zin = proj[:, N_SHIFT:]
    z = 0.5 * zin * (1.0 + lax.erf(zin * (0.5 ** 0.5)))
    zu = z[:, :D_GMLP]
    zv = z[:, D_GMLP:]
    causal = _iota((GCHUNK, GCHUNK), 0) >= _iota((GCHUNK, GCHUNK), 1)
    for gi in range(GMLP_GROUPS):
        gs = slice(gi * GROUP_W, (gi + 1) * GROUP_W)
        zvn = _layer_norm(zv[:, gs], glng_ref[:, gs], glnb_ref[:, gs], LN_EPS)
        ws = jnp.where(causal, wsp_ref[gi], 0.0).astype(BF16)
        bcol = bsp_ref[:, gi:gi + 1]
        for c in range(tm // GCHUNK):
            ts = slice(c * GCHUNK, (c + 1) * GCHUNK)
            mixed = jnp.dot(ws, zvn[ts].astype(BF16), preferred_element_type=F32) + bcol
            yb_ref[0, ts, gs] = (zu[ts, gs] * mixed).astype(BF16)


def _prep(x, ln_g, ln_b, w_in, mu, wwa, w0a0, wg, k_k, k_a, r_k, eones, glng, glnb, wsp, bsp):
    B, S, _ = x.shape
    tm = PREP_TM
    const = lambda shape: pl.BlockSpec(shape, lambda b, s: (0,) * len(shape))
    pair_spec = pl.BlockSpec((1, N_PAIRS, tm, LANES), lambda b, s: (b, 0, s, 0))
    pair_shape = jax.ShapeDtypeStruct((B, N_PAIRS, S, LANES), F32)
    return pl.pallas_call(
        _prep_kernel,
        grid=(B, S // tm),
        in_specs=[
            pl.BlockSpec((1, tm, D_MODEL), lambda b, s: (b, s, 0)),
            const((1, D_MODEL)), const((1, D_MODEL)), const((D_MODEL, D_IN)), const((1, N_SHIFT)),
            const((LANES, 2 * D_RWKV)), const((1, 2 * D_RWKV)), const((GATE_LORA, D_RWKV)),
            const((1, D_RWKV)), const((1, D_RWKV)), const((1, D_RWKV)), const((4 * LANES, 2 * LANES)),
            const((1, D_GMLP)), const((1, D_GMLP)), const((GMLP_GROUPS, GCHUNK, GCHUNK)),
            const((GCHUNK, GMLP_GROUPS)),
        ],
        out_specs=[pair_spec] * 8 + [pl.BlockSpec((1, tm, D_GMLP), lambda b, s: (b, s, 0))],
        out_shape=[pair_shape] * 8 + [jax.ShapeDtypeStruct((B, S, D_GMLP), BF16)],
        scratch_shapes=[pltpu.VMEM((8, N_SHIFT), F32)],
        compiler_params=pltpu.CompilerParams(dimension_semantics=("arbitrary", "arbitrary"),
                                             vmem_limit_bytes=VMEM_LIMIT),
        name="prep",
    )(x, ln_g, ln_b, w_in, mu, wwa, w0a0, wg, k_k, k_a, r_k, eones, glng, glnb, wsp, bsp)


def _wkv_kernel(r_ref, lw_ref, k_ref, v_ref, a_ref, b_ref, g_ref, bonus_ref, gng_ref, gnb_ref, emean_ref,
                o_ref, h_ref):
    C = WKV_CHUNK
    tb = r_ref.shape[2]

    @pl.when(pl.program_id(2) == 0)
    def _():
        h_ref[...] = jnp.zeros_like(h_ref)

    tok = _iota((C, LANES), 0)
    lane = _iota((C, LANES), 1)
    head0 = lane < HEAD
    strict = tok > lane % HEAD
    incl = tok >= lane % HEAD
    eye_w = (tok == lane % HEAD).astype(F32)
    rr = _iota((LANES, LANES), 0)
    cc = _iota((LANES, LANES), 1)
    eye = (rr == cc).astype(F32)
    same_head = (rr < HEAD) == (cc < HEAD)
    ltri3 = (_iota((C, 3 * C), 0) >= _iota((C, 3 * C), 1) % C).astype(BF16)

    def stack(x):
        xb = x.astype(BF16)
        zero = jnp.zeros_like(xb)
        return jnp.concatenate([jnp.where(head0, xb, zero), jnp.where(head0, zero, xb)], axis=0)

    def stack2(x, y):
        return jnp.concatenate([stack(x), stack(y)], axis=1)

    n_pairs = r_ref.shape[1]
    n_chunks = tb // C
    units = [(q, c) for q in range(n_pairs) for c in range(n_chunks)]

    def load(ref):
        return [ref[0, q, c * C:(c + 1) * C, :] for q, c in units]

    r_, lw_, k_, v_, a_, b_ = (load(ref) for ref in (r_ref, lw_ref, k_ref, v_ref, a_ref, b_ref))
    cum_ = [_dot3_rhs(ltri3, lw) for lw in lw_]
    cend_ = [cum[C - 1:C, :] for cum in cum_]
    at_ = [a * jnp.exp(cum - lw) for a, cum, lw in zip(a_, cum_, lw_)]
    rt_ = [r * jnp.exp(cum) for r, cum in zip(r_, cum_)]
    ginv_ = [jnp.exp(-cum) for cum in cum_]
    gend_ = [jnp.exp(cend - cum) for cend, cum in zip(cend_, cum_)]
    bk_end_ = [jnp.concatenate([b * ge, k * ge], axis=0) for b, k, ge in zip(b_, k_, gend_)]
    vst_ = [stack(v) for v in v_]

    G_ = [_dot_nt(jnp.concatenate([at, rt], axis=0), jnp.concatenate([stack(b * gi), stack(k * gi)], axis=0))
          for at, rt, b, k, gi in zip(at_, rt_, b_, k_, ginv_)]
    n1_ = [jnp.where(strict, G[:C, :LANES], 0.0) for G in G_]
    aak_ = [jnp.where(strict, G[:C, LANES:], 0.0) for G in G_]
    arb_ = [jnp.where(incl, G[C:, :LANES], 0.0) for G in G_]
    ark_ = [jnp.where(incl, G[C:, LANES:], 0.0) for G in G_]
    av_ = [_dot(jnp.concatenate([aak, ark], axis=0), vst) for aak, ark, vst in zip(aak_, ark_, vst_)]

    s1_ = [stack(n1) for n1 in n1_]
    n2_ = [_dot(n1, s1) for n1, s1 in zip(n1_, s1_)]
    x_ = [_dot(n2, jnp.concatenate([s1, stack(n2)], axis=1)) for n2, s1 in zip(n2_, s1_)]
    t_ = [eye_w + n1 + n2 + x[:, :LANES] for n1, n2, x in zip(n1_, n2_, x_)]
    np_ = [x[:, LANES:] for x in x_]
    for _ in range(3):
        x_ = [_dot(npow, stack2(t, npow)) for t, npow in zip(t_, np_)]
        t_ = [t + x[:, :LANES] for t, x in zip(t_, x_)]
        np_ = [x[:, LANES:] for x in x_]
    t_ = [t + _dot(npow, stack(t)) for t, npow in zip(t_, np_)]

    x_ = [_dot(t, stack2(at, av[:C])) for t, at, av in zip(t_, at_, av_)]
    z_ = [_dot(arb, stack2(x[:, :LANES], x[:, LANES:])) for arb, x in zip(arb_, x_)]
    rp_ = [rt + z[:, :LANES] for rt, z in zip(rt_, z_)]
    p3_ = [z[:, LANES:] + av[C:] for z, av in zip(z_, av_)]
    rhs_ = [jnp.concatenate([x, jnp.concatenate([jnp.zeros_like(v), v], axis=1)], axis=0)
            for x, v in zip(x_, v_)]
    mq_ = [_dot(bk_end.T, rhs) for bk_end, rhs in zip(bk_end_, rhs_)]
    m_ = [eye * jnp.exp(cend) + jnp.where(same_head, mq[:, :LANES], 0.0) for cend, mq in zip(cend_, mq_)]
    q_ = [jnp.where(same_head, mq[:, LANES:], 0.0) for mq in mq_]

    emean = emean_ref[...]
    for q in range(n_pairs):
        H = h_ref[q]
        ys = []
        for c in range(n_chunks):
            u = q * n_chunks + c
            ys.append(_dot(rp_[u], H) + p3_[u])
            H = _dot(m_[u], H) + q_[u]
        h_ref[q] = H
        y = jnp.concatenate(ys, axis=0)
        mu = _dot2_lhs(y, emean)
        yc = y - mu
        var = _dot2_lhs(yc * yc, emean)
        yn = yc * lax.rsqrt(var + GN_EPS) * gng_ref[q] + gnb_ref[q]
        o_ref[0, q] = ((yn + bonus_ref[0, q]) * g_ref[0, q]).astype(BF16)


def _wkv(r, lw, k, v, a, b, g, bonus, gn_g, gn_b, emean):
    B, P, S, _ = r.shape
    tb = WKV_TB
    pp = WKV_PAIRS
    seq = pl.BlockSpec((1, pp, tb, LANES), lambda bi, p, s: (bi, p, s, 0))
    par = pl.BlockSpec((pp, 1, LANES), lambda bi, p, s: (p, 0, 0))
    return pl.pallas_call(
        _wkv_kernel,
        grid=(B, P // pp, S // tb),
        in_specs=[seq] * 8 + [par, par, pl.BlockSpec((2 * LANES, LANES), lambda bi, p, s: (0, 0))],
        out_specs=seq,
        out_shape=jax.ShapeDtypeStruct((B, P, S, LANES), BF16),
        scratch_shapes=[pltpu.VMEM((pp, LANES, LANES), F32)],
        compiler_params=pltpu.CompilerParams(dimension_semantics=("arbitrary", "arbitrary", "arbitrary"),
                                             vmem_limit_bytes=VMEM_LIMIT),
        name="wkv",
    )(r, lw, k, v, a, b, g, bonus, gn_g, gn_b, emean)


def _mixer_kernel(x_ref, lng_ref, lnb_ref, ya_ref, yb_ref, wout_ref, l1g_ref, l1b_ref, wr_ref, br_ref,
                  p_ref, wpg_ref, bpg_ref, wpp_ref, base_ref, x1_ref, route_ref, counts_ref, carry_ref):
    tm = x_ref.shape[1]

    @pl.when((pl.program_id(0) == 0) & (pl.program_id(1) == 0))
    def _():
        carry_ref[...] = jnp.zeros_like(carry_ref)

    x0 = _layer_norm(x_ref[0], lng_ref[...], lnb_ref[...], LN_EPS)
    ymix = jnp.concatenate([ya_ref[0, p] for p in range(N_PAIRS)] + [yb_ref[0]], axis=-1)
    mix = jnp.dot(ymix, wout_ref[...], preferred_element_type=F32)
    x1 = _layer_norm(ALPHA * x0 + mix, l1g_ref[...], l1b_ref[...], LN_EPS)
    x1b = x1.astype(BF16)
    half = D_MODEL // 2
    lo_bits = lax.bitcast_convert_type(x1b[:, :half].astype(F32), jnp.uint32)
    hi_bits = lax.bitcast_convert_type(x1b[:, half:].astype(F32), jnp.uint32)
    x1_ref[0] = (hi_bits & jnp.uint32(0xFFFF0000)) | (lo_bits >> 16)

    hi, mid = _split2(x1)
    whi = wr_ref[0]
    wmid = wr_ref[1]
    d = lambda u, w: jnp.dot(u, w, preferred_element_type=F32)
    logits = (d(hi, whi) + d(hi, wmid) + d(mid, whi)) + br_ref[...]
    lane = _iota((tm, LANES), 1).astype(F32)
    far = float(4 * LANES)
    is_g = jnp.where(lane >= N_EXPERTS, jnp.where(lane < N_EXPERTS + N_GROUPS, 1.0, 0.0), 0.0) > 0.5
    gl = jnp.where(is_g, logits, NEG)
    gmax = jnp.max(gl, axis=-1, keepdims=True)
    gsel = jnp.min(jnp.where(gl == gmax, lane, far), axis=-1, keepdims=True) - N_EXPERTS
    p_group = 1.0 / jnp.sum(jnp.where(is_g, jnp.exp(gl - gmax), 0.0), axis=-1, keepdims=True)
    grp_of_lane = jnp.floor(lane * (1.0 / EXPERTS_PER_GROUP))
    el = jnp.where(grp_of_lane == gsel, logits, NEG)
    v1 = jnp.max(el, axis=-1, keepdims=True)
    i1 = jnp.min(jnp.where(el == v1, lane, far), axis=-1, keepdims=True)
    el2 = jnp.where(lane == i1, NEG, el)
    v2 = jnp.max(el2, axis=-1, keepdims=True)
    i2 = jnp.min(jnp.where(el2 == v2, lane, far), axis=-1, keepdims=True)
    e21 = jnp.exp(v2 - v1)
    w1 = p_group / (1.0 + e21)
    w2 = p_group * e21 / (1.0 + e21)

    oh1 = lane == i1
    oh2 = lane == i2
    below = (_iota((tm, tm), 0) > _iota((tm, tm), 1)).astype(BF16)
    o1 = jnp.where(oh1, 1.0, 0.0)
    o2 = jnp.where(oh2, 1.0, 0.0)
    c1 = jnp.dot(below, o1.astype(BF16), preferred_element_type=F32)
    c2 = jnp.dot(below, o2.astype(BF16), preferred_element_type=F32)
    tot1 = jnp.sum(o1, axis=0, keepdims=True)
    carry = carry_ref[0:1, :]
    rank1 = jnp.sum(jnp.where(oh1, c1 + carry, 0.0), axis=-1, keepdims=True)
    rank2 = jnp.sum(jnp.where(oh2, c2 + carry + tot1, 0.0), axis=-1, keepdims=True)
    carry = carry + tot1 + jnp.sum(o2, axis=0, keepdims=True)
    carry_ref[0:1, :] = carry
    counts_ref[...] = jnp.broadcast_to(carry, counts_ref.shape)

    fields = (i1, i2, w1, w2, rank1, rank2)
    route = jnp.zeros((tm, LANES), F32)
    for n, f in enumerate(fields):
        route = jnp.where(lane == n, f, route)
    route_ref[0] = route

    gate = _sigmoid(jnp.dot(x1b, wpg_ref[...], preferred_element_type=F32) + bpg_ref[...])
    ple = gate * jnp.dot(p_ref[0].astype(BF16), wpp_ref[...], preferred_element_type=F32)
    base_ref[0] = ALPHA * x1 + ple


def _mixer(x, ln_g, ln_b, ya, yb, w_out, l1g, l1b, wr3, br, p, wpg, bpg, wpp):
    B, S, _ = x.shape
    tm = MIX_TM
    const = lambda shape: pl.BlockSpec(shape, lambda b, s: (0,) * len(shape))
    row = lambda w: pl.BlockSpec((1, tm, w), lambda b, s: (b, s, 0))
    return pl.pallas_call(
        _mixer_kernel,
        grid=(B, S // tm),
        in_specs=[
            row(D_MODEL), const((1, D_MODEL)), const((1, D_MODEL)),
            pl.BlockSpec((1, N_PAIRS, tm, LANES), lambda b, s: (b, 0, s, 0)), row(D_GMLP),
            const((D_MODEL, D_MODEL)), const((1, D_MODEL)), const((1, D_MODEL)),
            const((2, D_MODEL, LANES)), const((1, LANES)),
            row(D_PLE), const((D_MODEL, D_MODEL)), const((1, D_MODEL)), const((D_PLE, D_MODEL)),
        ],
        out_specs=[row(D_MODEL), row(D_MODEL // 2), row(LANES), const((8, LANES))],
        out_shape=[jax.ShapeDtypeStruct((B, S, D_MODEL), F32), jax.ShapeDtypeStruct((B, S, D_MODEL // 2), jnp.uint32),
                   jax.ShapeDtypeStruct((B, S, LANES), F32), jax.ShapeDtypeStruct((8, LANES), F32)],
        scratch_shapes=[pltpu.VMEM((8, LANES), F32)],
        compiler_params=pltpu.CompilerParams(dimension_semantics=("arbitrary", "arbitrary"),
                                             vmem_limit_bytes=VMEM_LIMIT),
        name="mixer",
    )(x, ln_g, ln_b, ya, yb, w_out, l1g, l1b, wr3, br, p, wpg, bpg, wpp)


def _slots_kernel(route_ref, counts_ref, dest_ref, pend_ref, bexp_ref):
    tm = route_ref.shape[0]
    lane = _iota((tm, LANES), 1)
    route = route_ref[...]
    oh1 = lane == route[:, 0:1].astype(jnp.int32)
    oh2 = lane == route[:, 1:2].astype(jnp.int32)

    counts = counts_ref[0:1, :]
    padded = jnp.floor((counts + (EXPERT_ROWS - 1)) * (1.0 / EXPERT_ROWS)) * EXPERT_ROWS
    upper = (_iota((LANES, LANES), 0) <= _iota((LANES, LANES), 1)).astype(BF16)
    pend = _dot3_lhs(jnp.broadcast_to(padded, (8, LANES)), upper)[0:1, :]
    pstart = pend - padded
    d1 = jnp.sum(jnp.where(oh1, pstart, 0.0), axis=-1, keepdims=True) + route[:, 4:5]
    d2 = jnp.sum(jnp.where(oh2, pstart, 0.0), axis=-1, keepdims=True) + route[:, 5:6]
    dest_ref[...] = jnp.where(lane == 0, d1, jnp.where(lane == 1, d2, 0.0)).astype(jnp.int32)
    pend_ref[...] = jnp.broadcast_to(pend, (8, LANES)).astype(jnp.int32)
    nb = bexp_ref.shape[0]
    last_used = pend[:, N_EXPERTS - 1:N_EXPERTS] * (1.0 / EXPERT_ROWS) - 1.0
    first_row = jnp.minimum(_iota((nb, LANES), 0).astype(F32), last_used) * EXPERT_ROWS
    ended = jnp.where(_iota((nb, LANES), 1) < N_EXPERTS, jnp.where(pend <= first_row, 1.0, 0.0), 0.0)
    bexp = jnp.minimum(jnp.sum(ended, axis=-1, keepdims=True), N_EXPERTS - 1.0)
    bexp_ref[...] = jnp.broadcast_to(bexp, (nb, LANES)).astype(jnp.int32)


def _slots(route, counts, n_blocks):
    T = route.shape[0]
    tm = SLOT_TM
    nb = -(-n_blocks // 8) * 8
    return pl.pallas_call(
        _slots_kernel,
        grid=(T // tm,),
        in_specs=[pl.BlockSpec((tm, LANES), lambda i: (i, 0)), pl.BlockSpec((8, LANES), lambda i: (0, 0))],
        out_specs=[pl.BlockSpec((tm, LANES), lambda i: (i, 0)),
                   pl.BlockSpec((8, LANES), lambda i: (0, 0)),
                   pl.BlockSpec((nb, LANES), lambda i: (0, 0))],
        out_shape=[jax.ShapeDtypeStruct((T, LANES), jnp.int32), jax.ShapeDtypeStruct((8, LANES), jnp.int32),
                   jax.ShapeDtypeStruct((nb, LANES), jnp.int32)],
        compiler_params=pltpu.CompilerParams(dimension_semantics=("arbitrary",), vmem_limit_bytes=VMEM_LIMIT),
        name="slots",
    )(route, counts)


def _dispatch_kernel(pend_ref, dest_ref, x_ref, xs_ref, zero_ref, sem, zsem):
    tm = dest_ref.shape[0] // TOP_K

    @pl.when(pl.program_id(0) == 0)
    def _():
        zero_ref[...] = jnp.zeros_like(zero_ref)

        def tail(e):
            start = pl.multiple_of(jnp.maximum(pend_ref[e] - EXPERT_ROWS, 0), EXPERT_ROWS)
            return pltpu.make_async_copy(zero_ref, xs_ref.at[pl.ds(start, EXPERT_ROWS)], zsem)

        def unused(j):
            return pltpu.make_async_copy(
                zero_ref, xs_ref.at[pl.ds(pl.multiple_of(j * EXPERT_ROWS, EXPERT_ROWS), EXPERT_ROWS)], zsem)

        def start_unused(j, _):
            unused(j).start()
            return 0

        def wait_unused(j, _):
            unused(j).wait()
            return 0

        first_unused = pend_ref[N_EXPERTS - 1] // EXPERT_ROWS
        n_blocks = xs_ref.shape[0] // EXPERT_ROWS
        for e in range(N_EXPERTS):
            tail(e).start()
        lax.fori_loop(first_unused, n_blocks, start_unused, 0)
        for e in range(N_EXPERTS):
            tail(e).wait()
        lax.fori_loop(first_unused, n_blocks, wait_unused, 0)

    def start(t, _):
        for j in range(TOP_K):
            pltpu.make_async_copy(x_ref.at[pl.ds(t, 1)], xs_ref.at[pl.ds(dest_ref[TOP_K * t + j], 1)], sem).start()
        return 0

    lax.fori_loop(0, tm, start, 0, unroll=DMA_UNROLL)
    for j in range(TOP_K):
        pltpu.make_async_copy(x_ref, xs_ref.at[pl.ds(0, tm)], sem).wait()


def _dispatch(pend, dest_flat, x1, n_rows):
    T, width = x1.shape
    tm = DISPATCH_TM
    return pl.pallas_call(
        _dispatch_kernel,
        grid_spec=pltpu.PrefetchScalarGridSpec(
            num_scalar_prefetch=1,
            grid=(T // tm,),
            in_specs=[pl.BlockSpec((TOP_K * tm,), lambda i, pe: (i,), memory_space=pltpu.SMEM),
                      pl.BlockSpec((tm, width), lambda i, pe: (i, 0))],
            out_specs=pl.BlockSpec(memory_space=pl.ANY),
            scratch_shapes=[pltpu.VMEM((EXPERT_ROWS, width), x1.dtype), pltpu.SemaphoreType.DMA,
                            pltpu.SemaphoreType.DMA],
        ),
        out_shape=jax.ShapeDtypeStruct((n_rows, width), x1.dtype),
        compiler_params=pltpu.CompilerParams(dimension_semantics=("arbitrary",), vmem_limit_bytes=VMEM_LIMIT),
        name="dispatch",
    )(pend, dest_flat, x1)


def _last_used_block(pend_ref, j):
    return jnp.minimum(j, pend_ref[N_EXPERTS - 1] // EXPERT_ROWS - 1)


def _experts_kernel(pend_ref, bexp_ref, xs_ref, wg_ref, wu_ref, wd_ref, y_ref, wgu_ref, wdb_ref):
    j = pl.program_id(0)
    used = j * EXPERT_ROWS < pend_ref[N_EXPERTS - 1]
    new_expert = (j == 0) | (bexp_ref[j] != bexp_ref[jnp.maximum(j - 1, 0)])

    @pl.when(used & new_expert)
    def _():
        wgu_ref[:, :D_EXPERT] = wg_ref[0].astype(BF16)
        wgu_ref[:, D_EXPERT:] = wu_ref[0].astype(BF16)
        wdb_ref[...] = wd_ref[0].astype(BF16)

    @pl.when(used)
    def _():
        xw = xs_ref[...]
        x_lo = lax.bitcast_convert_type(xw << 16, F32)
        x_hi = lax.bitcast_convert_type(xw & jnp.uint32(0xFFFF0000), F32)
        xb = jnp.concatenate([x_lo, x_hi], axis=1).astype(BF16)
        h = jnp.dot(xb, wgu_ref[...], preferred_element_type=F32)
        hg = h[:, :D_EXPERT]
        hid = hg * _sigmoid(hg) * h[:, D_EXPERT:]
        y_ref[...] = jnp.dot(hid.astype(BF16), wdb_ref[...], preferred_element_type=F32)

    @pl.when(jnp.logical_not(used))
    def _():
        y_ref[...] = jnp.zeros_like(y_ref)


def _experts(pend, bexp, xs, wg, wu, wd):
    n_rows = xs.shape[0]
    rows = EXPERT_ROWS
    wspec = lambda shape: pl.BlockSpec((1,) + shape, lambda j, pe, be: (be[j], 0, 0))
    return pl.pallas_call(
        _experts_kernel,
        grid_spec=pltpu.PrefetchScalarGridSpec(
            num_scalar_prefetch=2,
            grid=(n_rows // rows,),
            in_specs=[pl.BlockSpec((rows, D_MODEL // 2), lambda j, pe, be: (_last_used_block(pe, j), 0)),
                      wspec((D_MODEL, D_EXPERT)), wspec((D_MODEL, D_EXPERT)), wspec((D_EXPERT, D_MODEL))],
            out_specs=pl.BlockSpec((rows, D_MODEL), lambda j, pe, be: (j, 0)),
            scratch_shapes=[pltpu.VMEM((D_MODEL, 2 * D_EXPERT), BF16), pltpu.VMEM((D_EXPERT, D_MODEL), BF16)],
        ),
        out_shape=jax.ShapeDtypeStruct((n_rows, D_MODEL), F32),
        compiler_params=pltpu.CompilerParams(dimension_semantics=("arbitrary",), vmem_limit_bytes=VMEM_LIMIT),
        name="experts",
    )(pend, bexp, xs, wg, wu, wd)


def _combine_kernel(dest_ref, dest_next_ref, ys_ref, base_ref, route_ref, lg_ref, lb_ref, o_ref, buf_ref, sem):
    tm = base_ref.shape[0]
    i = pl.program_id(0)
    slot = i % 2

    def gather(dref, s):
        def start(t, _):
            for j in range(TOP_K):
                pltpu.make_async_copy(ys_ref.at[pl.ds(dref[TOP_K * t + j], 1)], buf_ref.at[s, j, pl.ds(t, 1)],
                                      sem.at[s]).start()
            return 0
        lax.fori_loop(0, tm, start, 0, unroll=DMA_UNROLL)

    @pl.when(i == 0)
    def _():
        gather(dest_ref, 0)

    @pl.when(i + 1 < pl.num_programs(0))
    def _():
        gather(dest_next_ref, 1 - slot)

    for j in range(TOP_K):
        pltpu.make_async_copy(ys_ref.at[pl.ds(0, tm)], buf_ref.at[slot, j], sem.at[slot]).wait()
    route = route_ref[...]
    ffn = buf_ref[slot, 0] * route[:, 2:3] + buf_ref[slot, 1] * route[:, 3:4]
    o_ref[...] = _layer_norm(base_ref[...] + ffn, lg_ref[...], lb_ref[...], LN_EPS)


def _combine(dest_flat, ys, base, route, l2g, l2b):
    T = base.shape[0]
    tm = COMBINE_TM
    nt = T // tm
    return pl.pallas_call(
        _combine_kernel,
        grid=(nt,),
        in_specs=[pl.BlockSpec((TOP_K * tm,), lambda i: (i,), memory_space=pltpu.SMEM),
                  pl.BlockSpec((TOP_K * tm,), lambda i: (jnp.minimum(i + 1, nt - 1),), memory_space=pltpu.SMEM),
                  pl.BlockSpec(memory_space=pl.ANY),
                  pl.BlockSpec((tm, D_MODEL), lambda i: (i, 0)), pl.BlockSpec((tm, LANES), lambda i: (i, 0)),
                  pl.BlockSpec((1, D_MODEL), lambda i: (0, 0)), pl.BlockSpec((1, D_MODEL), lambda i: (0, 0))],
        out_specs=pl.BlockSpec((tm, D_MODEL), lambda i: (i, 0)),
        out_shape=jax.ShapeDtypeStruct((T, D_MODEL), F32),
        scratch_shapes=[pltpu.VMEM((2, TOP_K, tm, D_MODEL), F32), pltpu.SemaphoreType.DMA((2,))],
        compiler_params=pltpu.CompilerParams(dimension_semantics=("arbitrary",), vmem_limit_bytes=VMEM_LIMIT),
        name="combine",
    )(dest_flat, dest_flat, ys, base, route, l2g, l2b)


def _block_diag_const(n, blk, val):
    idx = jnp.arange(n) // blk
    return jnp.where(idx[:, None] == idx[None, :], val, 0.0).astype(BF16)


def kernel(x, p, ln_emb_g, ln_emb_b, w_in, mu_shift, w0, w_decay_up, a0, w_iclr_up, w_gate_up, k_k, k_a, r_k, gn_g, gn_b, gmlp_ln_g, gmlp_ln_b, w_spatial, b_spatial, w_out, ln1_g, ln1_b, w_group_router, b_group_router, w_expert_router, b_expert_router, w_exp_gate, w_exp_up, w_exp_down, w_ple_gate, b_ple_gate, w_ple_proj, ln2_g, ln2_b):
    B, S, D = x.shape
    T = B * S
    row = lambda t: t.reshape(1, -1).astype(F32)

    zl = jnp.zeros((DECAY_LORA, D_RWKV), F32)
    wwa = jnp.concatenate([jnp.concatenate([w_decay_up[0], zl], axis=1),
                           jnp.concatenate([zl, w_iclr_up[0]], axis=1)], axis=0).astype(BF16)
    w0a0 = jnp.concatenate([w0[0], a0[0]]).reshape(1, -1)
    eones = jnp.tile(_block_diag_const(2 * LANES, HEAD, 1.0), (2, 1))
    emean = jnp.tile(_block_diag_const(LANES, HEAD, 1.0 / HEAD), (2, 1))

    r, lw, k, v, a, b, g, bonus, yb = _prep(
        x, row(ln_emb_g), row(ln_emb_b), w_in[0].astype(BF16), row(mu_shift[0]), wwa, w0a0,
        w_gate_up[0].astype(BF16), row(k_k[0]), row(k_a[0]), row(r_k[0]), eones,
        row(gmlp_ln_g[0]), row(gmlp_ln_b[0]), w_spatial[0], b_spatial[0].T)

    ya = _wkv(r, lw, k, v, a, b, g, bonus, gn_g[0].reshape(N_PAIRS, 1, LANES), gn_b[0].reshape(N_PAIRS, 1, LANES),
              emean)

    wr = jnp.concatenate([w_expert_router[0].reshape(D, N_EXPERTS), w_group_router[0],
                          jnp.zeros((D, LANES - N_EXPERTS - N_GROUPS), F32)], axis=1)
    wr3 = jnp.stack(_split2(wr))
    br = jnp.concatenate([b_expert_router[0].reshape(-1), b_group_router[0],
                          jnp.zeros((LANES - N_EXPERTS - N_GROUPS,), F32)]).reshape(1, LANES)
    base, x1, route, counts = _mixer(x, row(ln_emb_g), row(ln_emb_b), ya, yb, w_out[0].astype(BF16), row(ln1_g[0]),
                                     row(ln1_b[0]), wr3, br, p[0], w_ple_gate[0].astype(BF16),
                                     row(b_ple_gate[0]), w_ple_proj[0].astype(BF16))
    base = base.reshape(T, D)
    x1 = x1.reshape(T, D // 2)
    route = route.reshape(T, LANES)

    n_blocks = -(-(T * TOP_K) // EXPERT_ROWS) + N_EXPERTS
    dest, pend, bexp = _slots(route, counts, n_blocks)
    dest_flat = dest[:, :TOP_K].reshape(T * TOP_K)
    pend = pend[0, :N_EXPERTS]
    bexp = bexp[:n_blocks, 0]

    xs = _dispatch(pend, dest_flat, x1, n_blocks * EXPERT_ROWS)
    ys = _experts(pend, bexp, xs, w_exp_gate[0], w_exp_up[0], w_exp_down[0])
    out = _combine(dest_flat, ys, base, route, row(ln2_g[0]), row(ln2_b[0]))
    return out.reshape(B, S, D)
```

```python
import functools
import math

import jax
import jax.numpy as jnp
from jax import lax
from jax.experimental import pallas as pl
from jax.experimental.pallas import tpu as pltpu

F32 = jnp.float32
BF16 = jnp.bfloat16

D_MODEL = 1024
D_RWKV = 512
HEAD = 64
D_GMLP = 512
GMLP_GROUPS = 4
GROUP_W = 128
GCHUNK = 128
DECAY_LORA = 64
ICLR_LORA = 64
GATE_LORA = 128
N_SHIFT = 3 * D_RWKV + DECAY_LORA + ICLR_LORA + GATE_LORA
D_IN = N_SHIFT + 2 * D_GMLP
D_PLE = 256
N_GROUPS = 4
EXPERTS_PER_GROUP = 8
N_EXPERTS = 32
TOP_K = 2
D_EXPERT = 512
DEPTH = 1
ALPHA = (2.0 * DEPTH) ** 0.25
LN_EPS = 1e-5
GN_EPS = 64e-5
DECAY_SCALE = math.exp(-0.5)

LANES = 128
WKV_CHUNK = 64
N_PAIRS = D_RWKV // LANES
VMEM_LIMIT = 56 * 1024 * 1024

PREP_TM = 256
WKV_TB = 256
WKV_PAIRS = 4
MIX_TM = 256
SLOT_TM = 512
EXPERT_ROWS = 256
DISPATCH_TM = 512
COMBINE_TM = 256
NEG = -1e30


def _dot(a, b):
    return jnp.dot(a.astype(BF16), b.astype(BF16), preferred_element_type=F32)


def _dot_nt(a, b):
    return lax.dot_general(a.astype(BF16), b.astype(BF16), (((1,), (1,)), ((), ())),
                           preferred_element_type=F32)


def _split3(x):
    hi = x.astype(BF16)
    r1 = x - hi.astype(F32)
    mid = r1.astype(BF16)
    lo = (r1 - mid.astype(F32)).astype(BF16)
    return hi, mid, lo


def _dot3_lhs(x, w):
    hi, mid, lo = _split3(x)
    w = w.astype(BF16)
    return (jnp.dot(hi, w, preferred_element_type=F32) + jnp.dot(mid, w, preferred_element_type=F32)
            + jnp.dot(lo, w, preferred_element_type=F32))


def _split2(x):
    hi = x.astype(BF16)
    return hi, (x - hi.astype(F32)).astype(BF16)


def _dot2_lhs(x, w2):
    hi, lo = _split2(x)
    return jnp.dot(jnp.concatenate([hi, lo], axis=1), w2, preferred_element_type=F32)


def _dot3_rhs(w3, x):
    hi, mid, lo = _split3(x)
    return jnp.dot(w3, jnp.concatenate([hi, mid, lo], axis=0), preferred_element_type=F32)


def _layer_norm(x, g, b, eps):
    mu = jnp.mean(x, axis=-1, keepdims=True)
    xc = x - mu
    var = jnp.mean(xc * xc, axis=-1, keepdims=True)
    return xc * lax.rsqrt(var + eps) * g + b


def _sigmoid(x):
    return 1.0 / (1.0 + jnp.exp(-x))


def _iota(shape, dim):
    return lax.broadcasted_iota(jnp.int32, shape, dim)


def _prep_kernel(x_ref, lng_ref, lnb_ref, win_ref, mu_ref, wwa_ref, w0a0_ref, wg_ref, kk_ref, ka_ref, rk_ref,
                 eones_ref, glng_ref, glnb_ref, wsp_ref, bsp_ref,
                 r_ref, lw_ref, k_ref, v_ref, a_ref, b_ref, g_ref, bonus_ref, yb_ref, carry_ref):
    tm = x_ref.shape[1]

    @pl.when(pl.program_id(1) == 0)
    def _():
        carry_ref[...] = jnp.zeros_like(carry_ref)

    x0 = _layer_norm(x_ref[0], lng_ref[...], lnb_ref[...], LN_EPS)
    proj = jnp.dot(x0.astype(BF16), win_ref[...], preferred_element_type=F32)

    h = proj[:, :N_SHIFT]
    rolled = pltpu.roll(h, 1, 0)
    first = _iota((tm, N_SHIFT), 0) == 0
    prev = jnp.where(first, jnp.broadcast_to(carry_ref[0:1, :], (tm, N_SHIFT)), rolled)
    carry_ref[0:1, :] = h[tm - 1:tm, :]
    h = h + (prev - h) * mu_ref[...]

    r = h[:, 0:D_RWKV]
    k = h[:, D_RWKV:2 * D_RWKV]
    v = h[:, 2 * D_RWKV:3 * D_RWKV]
    xwa = h[:, 3 * D_RWKV:3 * D_RWKV + LANES]
    xg = h[:, 3 * D_RWKV + LANES:N_SHIFT]

    lane = _iota((tm, LANES), 1)
    twa = jnp.where(lane < DECAY_LORA, jnp.tanh(xwa), xwa)
    da = _dot(twa, wwa_ref[...]) + w0a0_ref[...]
    logw = -DECAY_SCALE * _sigmoid(da[:, :D_RWKV])
    ag = _sigmoid(da[:, D_RWKV:])
    g = _dot(_sigmoid(xg), wg_ref[...])

    eones2 = eones_ref[...]

    def head_sum(t):
        half = 2 * LANES
        return jnp.concatenate([_dot2_lhs(t[:, :half], eones2), _dot2_lhs(t[:, half:], eones2)], axis=1)

    kk = k * kk_ref[...]
    kk = kk / jnp.maximum(jnp.sqrt(head_sum(kk * kk)), 1e-12)
    k = k * (1.0 + (ag - 1.0) * ka_ref[...])
    bonus = head_sum(r * k * rk_ref[...]) * v

    for p in range(N_PAIRS):
        sl = slice(p * LANES, (p + 1) * LANES)
        r_ref[0, p] = r[:, sl]
        lw_ref[0, p] = logw[:, sl]
        k_ref[0, p] = k[:, sl]
        v_ref[0, p] = v[:, sl]
        a_ref[0, p] = -kk[:, sl]
        b_ref[0, p] = (kk * ag)[:, sl]
        g_ref[0, p] = g[:, sl]
        bonus_ref[0, p] = bonus[:, sl]

    zin = proj[:, N_SHIFT:]
    z = 0.5 * zin * (1.0 + lax.erf(zin * (0.5 ** 0.5)))
    zu = z[:, :D_GMLP]
    zv = z[:, D_GMLP:]
    causal = _iota((GCHUNK, GCHUNK), 0) >= _iota((GCHUNK, GCHUNK), 1)
    for gi in range(GMLP_GROUPS):
        gs = slice(gi * GROUP_W, (gi + 1) * GROUP_W)
        zvn = _layer_norm(zv[:, gs], glng_ref[:, gs], glnb_ref[:, gs], LN_EPS)
        ws = jnp.where(causal, wsp_ref[gi], 0.0).astype(BF16)
        bcol = bsp_ref[:, gi:gi + 1]
        for c in range(tm // GCHUNK):
            ts = slice(c * GCHUNK, (c + 1) * GCHUNK)
            mixed = jnp.dot(ws, zvn[ts].astype(BF16), preferred_element_type=F32) + bcol
            yb_ref[0, ts, gs] = (zu[ts, gs] * mixed).astype(BF16)


def _prep(x, ln_g, ln_b, w_in, mu, wwa, w0a0, wg, k_k, k_a, r_k, eones, glng, glnb, wsp, bsp):
    B, S, _ = x.shape
    tm = PREP_TM
    const = lambda shape: pl.BlockSpec(shape, lambda b, s: (0,) * len(shape))
    pair_spec = pl.BlockSpec((1, N_PAIRS, tm, LANES), lambda b, s: (b, 0, s, 0))
    pair_shape = jax.ShapeDtypeStruct((B, N_PAIRS, S, LANES), F32)
    return pl.pallas_call(
        _prep_kernel,
        grid=(B, S // tm),
        in_specs=[
            pl.BlockSpec((1, tm, D_MODEL), lambda b, s: (b, s, 0)),
            const((1, D_MODEL)), const((1, D_MODEL)), const((D_MODEL, D_IN)), const((1, N_SHIFT)),
            const((LANES, 2 * D_RWKV)), const((1, 2 * D_RWKV)), const((GATE_LORA, D_RWKV)),
            const((1, D_RWKV)), const((1, D_RWKV)), const((1, D_RWKV)), const((4 * LANES, 2 * LANES)),
            const((1, D_GMLP)), const((1, D_GMLP)), const((GMLP_GROUPS, GCHUNK, GCHUNK)),
            const((GCHUNK, GMLP_GROUPS)),
        ],
        out_specs=[pair_spec] * 8 + [pl.BlockSpec((1, tm, D_GMLP), lambda b, s: (b, s, 0))],
        out_shape=[pair_shape] * 8 + [jax.ShapeDtypeStruct((B, S, D_GMLP), BF16)],
        scratch_shapes=[pltpu.VMEM((8, N_SHIFT), F32)],
        compiler_params=pltpu.CompilerParams(dimension_semantics=("arbitrary", "arbitrary"),
                                             vmem_limit_bytes=VMEM_LIMIT),
        name="prep",
    )(x, ln_g, ln_b, w_in, mu, wwa, w0a0, wg, k_k, k_a, r_k, eones, glng, glnb, wsp, bsp)


def _wkv_kernel(r_ref, lw_ref, k_ref, v_ref, a_ref, b_ref, g_ref, bonus_ref, gng_ref, gnb_ref, emean_ref,
                o_ref, h_ref):
    C = WKV_CHUNK
    tb = r_ref.shape[2]

    @pl.when(pl.program_id(2) == 0)
    def _():
        h_ref[...] = jnp.zeros_like(h_ref)

    tok = _iota((C, LANES), 0)
    lane = _iota((C, LANES), 1)
    head0 = lane < HEAD
    strict = tok > lane % HEAD
    incl = tok >= lane % HEAD
    eye_w = (tok == lane % HEAD).astype(F32)
    rr = _iota((LANES, LANES), 0)
    cc = _iota((LANES, LANES), 1)
    eye = (rr == cc).astype(F32)
    same_head = (rr < HEAD) == (cc < HEAD)
    ltri3 = (_iota((C, 3 * C), 0) >= _iota((C, 3 * C), 1) % C).astype(BF16)

    def stack(x):
        xb = x.astype(BF16)
        zero = jnp.zeros_like(xb)
        return jnp.concatenate([jnp.where(head0, xb, zero), jnp.where(head0, zero, xb)], axis=0)

    def stack2(x, y):
        return jnp.concatenate([stack(x), stack(y)], axis=1)

    n_pairs = r_ref.shape[1]
    n_chunks = tb // C
    units = [(q, c) for q in range(n_pairs) for c in range(n_chunks)]

    def load(ref):
        return [ref[0, q, c * C:(c + 1) * C, :] for q, c in units]

    r_, lw_, k_, v_, a_, b_ = (load(ref) for ref in (r_ref, lw_ref, k_ref, v_ref, a_ref, b_ref))
    cum_ = [_dot3_rhs(ltri3, lw) for lw in lw_]
    cend_ = [cum[C - 1:C, :] for cum in cum_]
    at_ = [a * jnp.exp(cum - lw) for a, cum, lw in zip(a_, cum_, lw_)]
    rt_ = [r * jnp.exp(cum) for r, cum in zip(r_, cum_)]
    ginv_ = [jnp.exp(-cum) for cum in cum_]
    gend_ = [jnp.exp(cend - cum) for cend, cum in zip(cend_, cum_)]
    bk_end_ = [jnp.concatenate([b * ge, k * ge], axis=0) for b, k, ge in zip(b_, k_, gend_)]
    vst_ = [stack(v) for v in v_]

    G_ = [_dot_nt(jnp.concatenate([at, rt], axis=0), jnp.concatenate([stack(b * gi), stack(k * gi)], axis=0))
          for at, rt, b, k, gi in zip(at_, rt_, b_, k_, ginv_)]
    n1_ = [jnp.where(strict, G[:C, :LANES], 0.0) for G in G_]
    aak_ = [jnp.where(strict, G[:C, LANES:], 0.0) for G in G_]
    arb_ = [jnp.where(incl, G[C:, :LANES], 0.0) for G in G_]
    ark_ = [jnp.where(incl, G[C:, LANES:], 0.0) for G in G_]
    av_ = [_dot(jnp.concatenate([aak, ark], axis=0), vst) for aak, ark, vst in zip(aak_, ark_, vst_)]

    s1_ = [stack(n1) for n1 in n1_]
    n2_ = [_dot(n1, s1) for n1, s1 in zip(n1_, s1_)]
    x_ = [_dot(n2, jnp.concatenate([s1, stack(n2)], axis=1)) for n2, s1 in zip(n2_, s1_)]
    t_ = [eye_w + n1 + n2 + x[:, :LANES] for n1, n2, x in zip(n1_, n2_, x_)]
    np_ = [x[:, LANES:] for x in x_]
    for _ in range(3):
        x_ = [_dot(npow, stack2(t, npow)) for t, npow in zip(t_, np_)]
        t_ = [t + x[:, :LANES] for t, x in zip(t_, x_)]
        np_ = [x[:, LANES:] for x in x_]
    t_ = [t + _dot(npow, stack(t)) for t, npow in zip(t_, np_)]

    x_ = [_dot(t, stack2(at, av[:C])) for t, at, av in zip(t_, at_, av_)]
    z_ = [_dot(arb, stack2(x[:, :LANES], x[:, LANES:])) for arb, x in zip(arb_, x_)]
    rp_ = [rt + z[:, :LANES] for rt, z in zip(rt_, z_)]
    p3_ = [z[:, LANES:] + av[C:] for z, av in zip(z_, av_)]
    rhs_ = [jnp.concatenate([x, jnp.concatenate([jnp.zeros_like(v), v], axis=1)], axis=0)
            for x, v in zip(x_, v_)]
    mq_ = [_dot(bk_end.T, rhs) for bk_end, rhs in zip(bk_end_, rhs_)]
    m_ = [eye * jnp.exp(cend) + jnp.where(same_head, mq[:, :LANES], 0.0) for cend, mq in zip(cend_, mq_)]
    q_ = [jnp.where(same_head, mq[:, LANES:], 0.0) for mq in mq_]

    emean = emean_ref[...]
    for q in range(n_pairs):
        H = h_ref[q]
        ys = []
        for c in range(n_chunks):
            u = q * n_chunks + c
            ys.append(_dot(rp_[u], H) + p3_[u])
            H = _dot(m_[u], H) + q_[u]
        h_ref[q] = H
        y = jnp.concatenate(ys, axis=0)
        mu = _dot2_lhs(y, emean)
        yc = y - mu
        var = _dot2_lhs(yc * yc, emean)
        yn = yc * lax.rsqrt(var + GN_EPS) * gng_ref[q] + gnb_ref[q]
        o_ref[0, q] = ((yn + bonus_ref[0, q]) * g_ref[0, q]).astype(BF16)


def _wkv(r, lw, k, v, a, b, g, bonus, gn_g, gn_b, emean):
    B, P, S, _ = r.shape
    tb = WKV_TB
    pp = WKV_PAIRS
    seq = pl.BlockSpec((1, pp, tb, LANES), lambda bi, p, s: (bi, p, s, 0))
    par = pl.BlockSpec((pp, 1, LANES), lambda bi, p, s: (p, 0, 0))
    return pl.pallas_call(
        _wkv_kernel,
        grid=(B, P // pp, S // tb),
        in_specs=[seq] * 8 + [par, par, pl.BlockSpec((2 * LANES, LANES), lambda bi, p, s: (0, 0))],
        out_specs=seq,
        out_shape=jax.ShapeDtypeStruct((B, P, S, LANES), BF16),
        scratch_shapes=[pltpu.VMEM((pp, LANES, LANES), F32)],
        compiler_params=pltpu.CompilerParams(dimension_semantics=("arbitrary", "arbitrary", "arbitrary"),
                                             vmem_limit_bytes=VMEM_LIMIT),
        name="wkv",
    )(r, lw, k, v, a, b, g, bonus, gn_g, gn_b, emean)


def _mixer_kernel(x_ref, lng_ref, lnb_ref, ya_ref, yb_ref, wout_ref, l1g_ref, l1b_ref, wr_ref, br_ref,
                  p_ref, wpg_ref, bpg_ref, wpp_ref, base_ref, x1_ref, route_ref, counts_ref, carry_ref):
    tm = x_ref.shape[1]

    @pl.when((pl.program_id(0) == 0) & (pl.program_id(1) == 0))
    def _():
        carry_ref[...] = jnp.zeros_like(carry_ref)

    x0 = _layer_norm(x_ref[0], lng_ref[...], lnb_ref[...], LN_EPS)
    ymix = jnp.concatenate([ya_ref[0, p] for p in range(N_PAIRS)] + [yb_ref[0]], axis=-1)
    mix = jnp.dot(ymix, wout_ref[...], preferred_element_type=F32)
    x1 = _layer_norm(ALPHA * x0 + mix, l1g_ref[...], l1b_ref[...], LN_EPS)
    x1b = x1.astype(BF16)
    half = D_MODEL // 2
    lo_bits = lax.bitcast_convert_type(x1b[:, :half].astype(F32), jnp.uint32)
    hi_bits = lax.bitcast_convert_type(x1b[:, half:].astype(F32), jnp.uint32)
    x1_ref[0] = (hi_bits & jnp.uint32(0xFFFF0000)) | (lo_bits >> 16)

    hi, mid = _split2(x1)
    whi = wr_ref[0]
    wmid = wr_ref[1]
    d = lambda u, w: jnp.dot(u, w, preferred_element_type=F32)
    logits = (d(hi, whi) + d(hi, wmid) + d(mid, whi)) + br_ref[...]
    lane = _iota((tm, LANES), 1).astype(F32)
    far = float(4 * LANES)
    is_g = jnp.where(lane >= N_EXPERTS, jnp.where(lane < N_EXPERTS + N_GROUPS, 1.0, 0.0), 0.0) > 0.5
    gl = jnp.where(is_g, logits, NEG)
    gmax = jnp.max(gl, axis=-1, keepdims=True)
    gsel = jnp.min(jnp.where(gl == gmax, lane, far), axis=-1, keepdims=True) - N_EXPERTS
    p_group = 1.0 / jnp.sum(jnp.where(is_g, jnp.exp(gl - gmax), 0.0), axis=-1, keepdims=True)
    grp_of_lane = jnp.floor(lane * (1.0 / EXPERTS_PER_GROUP))
    el = jnp.where(grp_of_lane == gsel, logits, NEG)
    v1 = jnp.max(el, axis=-1, keepdims=True)
    i1 = jnp.min(jnp.where(el == v1, lane, far), axis=-1, keepdims=True)
    el2 = jnp.where(lane == i1, NEG, el)
    v2 = jnp.max(el2, axis=-1, keepdims=True)
    i2 = jnp.min(jnp.where(el2 == v2, lane, far), axis=-1, keepdims=True)
    e21 = jnp.exp(v2 - v1)
    w1 = p_group / (1.0 + e21)
    w2 = p_group * e21 / (1.0 + e21)

    oh1 = lane == i1
    oh2 = lane == i2
    below = (_iota((tm, tm), 0) > _iota((tm, tm), 1)).astype(BF16)
    o1 = jnp.where(oh1, 1.0, 0.0)
    o2 = jnp.where(oh2, 1.0, 0.0)
    c1 = jnp.dot(below, o1.astype(BF16), preferred_element_type=F32)
    c2 = jnp.dot(below, o2.astype(BF16), preferred_element_type=F32)
    tot1 = jnp.sum(o1, axis=0, keepdims=True)
    carry = carry_ref[0:1, :]
    rank1 = jnp.sum(jnp.where(oh1, c1 + carry, 0.0), axis=-1, keepdims=True)
    rank2 = jnp.sum(jnp.where(oh2, c2 + carry + tot1, 0.0), axis=-1, keepdims=True)
    carry = carry + tot1 + jnp.sum(o2, axis=0, keepdims=True)
    carry_ref[0:1, :] = carry
    counts_ref[...] = jnp.broadcast_to(carry, counts_ref.shape)

    fields = (i1, i2, w1, w2, rank1, rank2)
    route = jnp.zeros((tm, LANES), F32)
    for n, f in enumerate(fields):
        route = jnp.where(lane == n, f, route)
    route_ref[0] = route

    gate = _sigmoid(jnp.dot(x1b, wpg_ref[...], preferred_element_type=F32) + bpg_ref[...])
    ple = gate * jnp.dot(p_ref[0].astype(BF16), wpp_ref[...], preferred_element_type=F32)
    base_ref[0] = ALPHA * x1 + ple


def _mixer(x, ln_g, ln_b, ya, yb, w_out, l1g, l1b, wr3, br, p, wpg, bpg, wpp):
    B, S, _ = x.shape
    tm = MIX_TM
    const = lambda shape: pl.BlockSpec(shape, lambda b, s: (0,) * len(shape))
    row = lambda w: pl.BlockSpec((1, tm, w), lambda b, s: (b, s, 0))
    return pl.pallas_call(
        _mixer_kernel,
        grid=(B, S // tm),
        in_specs=[
            row(D_MODEL), const((1, D_MODEL)), const((1, D_MODEL)),
            pl.BlockSpec((1, N_PAIRS, tm, LANES), lambda b, s: (b, 0, s, 0)), row(D_GMLP),
            const((D_MODEL, D_MODEL)), const((1, D_MODEL)), const((1, D_MODEL)),
            const((2, D_MODEL, LANES)), const((1, LANES)),
            row(D_PLE), const((D_MODEL, D_MODEL)), const((1, D_MODEL)), const((D_PLE, D_MODEL)),
        ],
        out_specs=[row(D_MODEL), row(D_MODEL // 2), row(LANES), const((8, LANES))],
        out_shape=[jax.ShapeDtypeStruct((B, S, D_MODEL), F32), jax.ShapeDtypeStruct((B, S, D_MODEL // 2), jnp.uint32),
                   jax.ShapeDtypeStruct((B, S, LANES), F32), jax.ShapeDtypeStruct((8, LANES), F32)],
        scratch_shapes=[pltpu.VMEM((8, LANES), F32)],
        compiler_params=pltpu.CompilerParams(dimension_semantics=("arbitrary", "arbitrary"),
                                             vmem_limit_bytes=VMEM_LIMIT),
        name="mixer",
    )(x, ln_g, ln_b, ya, yb, w_out, l1g, l1b, wr3, br, p, wpg, bpg, wpp)


def _slots_kernel(route_ref, counts_ref, dest_ref, pend_ref):
    tm = route_ref.shape[0]
    lane = _iota((tm, LANES), 1)
    route = route_ref[...]
    oh1 = lane == route[:, 0:1].astype(jnp.int32)
    oh2 = lane == route[:, 1:2].astype(jnp.int32)

    counts = counts_ref[0:1, :]
    padded = jnp.floor((counts + (EXPERT_ROWS - 1)) * (1.0 / EXPERT_ROWS)) * EXPERT_ROWS
    upper = (_iota((LANES, LANES), 0) <= _iota((LANES, LANES), 1)).astype(BF16)
    pend = _dot3_lhs(jnp.broadcast_to(padded, (8, LANES)), upper)[0:1, :]
    pstart = pend - padded
    d1 = jnp.sum(jnp.where(oh1, pstart, 0.0), axis=-1, keepdims=True) + route[:, 4:5]
    d2 = jnp.sum(jnp.where(oh2, pstart, 0.0), axis=-1, keepdims=True) + route[:, 5:6]
    dest_ref[...] = jnp.where(lane == 0, d1, jnp.where(lane == 1, d2, 0.0)).astype(jnp.int32)
    pend_ref[...] = jnp.broadcast_to(pend, (8, LANES)).astype(jnp.int32)


def _slots(route, counts):
    T = route.shape[0]
    tm = SLOT_TM
    return pl.pallas_call(
        _slots_kernel,
        grid=(T // tm,),
        in_specs=[pl.BlockSpec((tm, LANES), lambda i: (i, 0)), pl.BlockSpec((8, LANES), lambda i: (0, 0))],
        out_specs=[pl.BlockSpec((tm, LANES), lambda i: (i, 0)),
                   pl.BlockSpec((8, LANES), lambda i: (0, 0))],
        out_shape=[jax.ShapeDtypeStruct((T, LANES), jnp.int32), jax.ShapeDtypeStruct((8, LANES), jnp.int32)],
        compiler_params=pltpu.CompilerParams(dimension_semantics=("arbitrary",), vmem_limit_bytes=VMEM_LIMIT),
        name="slots",
    )(route, counts)


def _dispatch_kernel(pend_ref, dest_ref, x_ref, xs_ref, zero_ref, sem, zsem):
    tm = dest_ref.shape[0] // TOP_K

    @pl.when(pl.program_id(0) == 0)
    def _():
        zero_ref[...] = jnp.zeros_like(zero_ref)

        def tail(e):
            start = pl.multiple_of(jnp.maximum(pend_ref[e] - EXPERT_ROWS, 0), EXPERT_ROWS)
            return pltpu.make_async_copy(zero_ref, xs_ref.at[pl.ds(start, EXPERT_ROWS)], zsem)

        def unused(j):
            return pltpu.make_async_copy(
                zero_ref, xs_ref.at[pl.ds(pl.multiple_of(j * EXPERT_ROWS, EXPERT_ROWS), EXPERT_ROWS)], zsem)

        def start_unused(j, _):
            unused(j).start()
            return 0

        def wait_unused(j, _):
            unused(j).wait()
            return 0

        first_unused = pend_ref[N_EXPERTS - 1] // EXPERT_ROWS
        n_blocks = xs_ref.shape[0] // EXPERT_ROWS
        for e in range(N_EXPERTS):
            tail(e).start()
        lax.fori_loop(first_unused, n_blocks, start_unused, 0)
        for e in range(N_EXPERTS):
            tail(e).wait()
        lax.fori_loop(first_unused, n_blocks, wait_unused, 0)

    for t in range(tm):
        for j in range(TOP_K):
            pltpu.make_async_copy(x_ref.at[pl.ds(t, 1)], xs_ref.at[pl.ds(dest_ref[TOP_K * t + j], 1)], sem).start()
    for j in range(TOP_K):
        pltpu.make_async_copy(x_ref, xs_ref.at[pl.ds(0, tm)], sem).wait()


def _dispatch(pend, dest_flat, x1, n_rows):
    T, width = x1.shape
    tm = DISPATCH_TM
    return pl.pallas_call(
        _dispatch_kernel,
        grid_spec=pltpu.PrefetchScalarGridSpec(
            num_scalar_prefetch=1,
            grid=(T // tm,),
            in_specs=[pl.BlockSpec((TOP_K * tm,), lambda i, pe: (i,), memory_space=pltpu.SMEM),
                      pl.BlockSpec((tm, width), lambda i, pe: (i, 0))],
            out_specs=pl.BlockSpec(memory_space=pl.ANY),
            scratch_shapes=[pltpu.VMEM((EXPERT_ROWS, width), x1.dtype), pltpu.SemaphoreType.DMA,
                            pltpu.SemaphoreType.DMA],
        ),
        out_shape=jax.ShapeDtypeStruct((n_rows, width), x1.dtype),
        compiler_params=pltpu.CompilerParams(dimension_semantics=("arbitrary",), vmem_limit_bytes=VMEM_LIMIT),
        name="dispatch",
    )(pend, dest_flat, x1)


def _experts_kernel(pend_ref, xs_ref, wg_ref, wu_ref, wd_ref, ys_ref, xbuf_ref, ybuf_ref, wgu_ref, wdb_ref,
                    in_sem, out_sem):
    rows = EXPERT_ROWS
    e = pl.program_id(0)
    first = jnp.where(e == 0, 0, pend_ref[jnp.maximum(e - 1, 0)]) // rows
    nblk = pend_ref[e] // rows - first

    def block_rows(ref, b):
        return ref.at[pl.ds(pl.multiple_of((first + b) * rows, rows), rows)]

    def x_copy(b, slot):
        return pltpu.make_async_copy(block_rows(xs_ref, b), xbuf_ref.at[slot], in_sem.at[slot])

    def y_copy(b, slot):
        return pltpu.make_async_copy(ybuf_ref.at[slot], block_rows(ys_ref, b), out_sem.at[slot])

    @pl.when(nblk > 0)
    def _():
        x_copy(0, 0).start()
        wgu_ref[:, :D_EXPERT] = wg_ref[0].astype(BF16)
        wgu_ref[:, D_EXPERT:] = wu_ref[0].astype(BF16)
        wdb_ref[...] = wd_ref[0].astype(BF16)

        def body(b, _):
            slot = b % 2

            @pl.when(b + 1 < nblk)
            def _():
                x_copy(b + 1, 1 - slot).start()

            x_copy(b, slot).wait()

            @pl.when(b >= 2)
            def _():
                y_copy(b - 2, slot).wait()

            xw = xbuf_ref[slot]
            x_lo = lax.bitcast_convert_type(xw << 16, F32)
            x_hi = lax.bitcast_convert_type(xw & jnp.uint32(0xFFFF0000), F32)
            xb = jnp.concatenate([x_lo, x_hi], axis=1).astype(BF16)
            h = jnp.dot(xb, wgu_ref[...], preferred_element_type=F32)
            hg = h[:, :D_EXPERT]
            hid = hg * _sigmoid(hg) * h[:, D_EXPERT:]
            ybuf_ref[slot] = jnp.dot(hid.astype(BF16), wdb_ref[...], preferred_element_type=F32)
            y_copy(b, slot).start()
            return 0

        lax.fori_loop(0, nblk, body, 0)

        @pl.when(nblk >= 2)
        def _():
            y_copy(nblk - 2, nblk % 2).wait()

        y_copy(nblk - 1, (nblk - 1) % 2).wait()

    @pl.when(e == N_EXPERTS - 1)
    def _():
        ybuf_ref[0] = jnp.zeros(ybuf_ref.shape[1:], F32)
        first_unused = pend_ref[N_EXPERTS - 1] // rows - first
        n_tail = ys_ref.shape[0] // rows - first

        def start_unused(b, _):
            y_copy(b, 0).start()
            return 0

        def wait_unused(b, _):
            y_copy(b, 0).wait()
            return 0

        lax.fori_loop(first_unused, n_tail, start_unused, 0)
        lax.fori_loop(first_unused, n_tail, wait_unused, 0)


def _experts(pend, xs, wg, wu, wd):
    n_rows = xs.shape[0]
    rows = EXPERT_ROWS
    wspec = lambda shape: pl.BlockSpec((1,) + shape, lambda e, pe: (e, 0, 0))
    return pl.pallas_call(
        _experts_kernel,
        grid_spec=pltpu.PrefetchScalarGridSpec(
            num_scalar_prefetch=1,
            grid=(N_EXPERTS,),
            in_specs=[pl.BlockSpec(memory_space=pl.ANY),
                      wspec((D_MODEL, D_EXPERT)), wspec((D_MODEL, D_EXPERT)), wspec((D_EXPERT, D_MODEL))],
            out_specs=pl.BlockSpec(memory_space=pl.ANY),
            scratch_shapes=[pltpu.VMEM((2, rows, D_MODEL // 2), jnp.uint32), pltpu.VMEM((2, rows, D_MODEL), F32),
                            pltpu.VMEM((D_MODEL, 2 * D_EXPERT), BF16), pltpu.VMEM((D_EXPERT, D_MODEL), BF16),
                            pltpu.SemaphoreType.DMA((2,)), pltpu.SemaphoreType.DMA((2,))],
        ),
        out_shape=jax.ShapeDtypeStruct((n_rows, D_MODEL), F32),
        compiler_params=pltpu.CompilerParams(dimension_semantics=("arbitrary",), vmem_limit_bytes=VMEM_LIMIT),
        name="experts",
    )(pend, xs, wg, wu, wd)


def _combine_kernel(dest_ref, dest_next_ref, ys_ref, base_ref, route_ref, lg_ref, lb_ref, o_ref, buf_ref, sem):
    tm = buf_ref.shape[2]
    i = pl.program_id(0)

    def gather(dref, offset, s):
        for t in range(tm):
            for j in range(TOP_K):
                pltpu.make_async_copy(ys_ref.at[pl.ds(dref[offset + TOP_K * t + j], 1)],
                                      buf_ref.at[s, j, pl.ds(t, 1)], sem.at[s]).start()

    def drain(s):
        for j in range(TOP_K):
            pltpu.make_async_copy(ys_ref.at[pl.ds(0, tm)], buf_ref.at[s, j], sem.at[s]).wait()

    def finish(s):
        rows = slice(s * tm, (s + 1) * tm)
        drain(s)
        route = route_ref[rows, :]
        ffn = buf_ref[s, 0] * route[:, 2:3] + buf_ref[s, 1] * route[:, 3:4]
        o_ref[rows, :] = _layer_norm(base_ref[rows, :] + ffn, lg_ref[...], lb_ref[...], LN_EPS)

    @pl.when(i == 0)
    def _():
        gather(dest_ref, 0, 0)

    gather(dest_ref, TOP_K * tm, 1)
    finish(0)
    gather(dest_next_ref, 0, 0)
    finish(1)

    @pl.when(i == pl.num_programs(0) - 1)
    def _():
        drain(0)


def _combine(dest_flat, ys, base, route, l2g, l2b):
    T = base.shape[0]
    tm = COMBINE_TM
    nt = T // tm
    return pl.pallas_call(
        _combine_kernel,
        grid=(nt // 2,),
        in_specs=[pl.BlockSpec((2 * TOP_K * tm,), lambda i: (i,), memory_space=pltpu.SMEM),
                  pl.BlockSpec((TOP_K * tm,), lambda i: (jnp.minimum(2 * i + 2, nt - 1),), memory_space=pltpu.SMEM),
                  pl.BlockSpec(memory_space=pl.ANY),
                  pl.BlockSpec((2 * tm, D_MODEL), lambda i: (i, 0)), pl.BlockSpec((2 * tm, LANES), lambda i: (i, 0)),
                  pl.BlockSpec((1, D_MODEL), lambda i: (0, 0)), pl.BlockSpec((1, D_MODEL), lambda i: (0, 0))],
        out_specs=pl.BlockSpec((2 * tm, D_MODEL), lambda i: (i, 0)),
        out_shape=jax.ShapeDtypeStruct((T, D_MODEL), F32),
        scratch_shapes=[pltpu.VMEM((2, TOP_K, tm, D_MODEL), F32), pltpu.SemaphoreType.DMA((2,))],
        compiler_params=pltpu.CompilerParams(dimension_semantics=("arbitrary",), vmem_limit_bytes=VMEM_LIMIT),
        name="combine",
    )(dest_flat, dest_flat, ys, base, route, l2g, l2b)


def _block_diag_const(n, blk, val):
    idx = jnp.arange(n) // blk
    return jnp.where(idx[:, None] == idx[None, :], val, 0.0).astype(BF16)


def kernel(x, p, ln_emb_g, ln_emb_b, w_in, mu_shift, w0, w_decay_up, a0, w_iclr_up, w_gate_up, k_k, k_a, r_k, gn_g, gn_b, gmlp_ln_g, gmlp_ln_b, w_spatial, b_spatial, w_out, ln1_g, ln1_b, w_group_router, b_group_router, w_expert_router, b_expert_router, w_exp_gate, w_exp_up, w_exp_down, w_ple_gate, b_ple_gate, w_ple_proj, ln2_g, ln2_b):
    B, S, D = x.shape
    T = B * S
    row = lambda t: t.reshape(1, -1).astype(F32)

    zl = jnp.zeros((DECAY_LORA, D_RWKV), F32)
    wwa = jnp.concatenate([jnp.concatenate([w_decay_up[0], zl], axis=1),
                           jnp.concatenate([zl, w_iclr_up[0]], axis=1)], axis=0).astype(BF16)
    w0a0 = jnp.concatenate([w0[0], a0[0]]).reshape(1, -1)
    eones = jnp.tile(_block_diag_const(2 * LANES, HEAD, 1.0), (2, 1))
    emean = jnp.tile(_block_diag_const(LANES, HEAD, 1.0 / HEAD), (2, 1))

    r, lw, k, v, a, b, g, bonus, yb = _prep(
        x, row(ln_emb_g), row(ln_emb_b), w_in[0].astype(BF16), row(mu_shift[0]), wwa, w0a0,
        w_gate_up[0].astype(BF16), row(k_k[0]), row(k_a[0]), row(r_k[0]), eones,
        row(gmlp_ln_g[0]), row(gmlp_ln_b[0]), w_spatial[0], b_spatial[0].T)

    ya = _wkv(r, lw, k, v, a, b, g, bonus, gn_g[0].reshape(N_PAIRS, 1, LANES), gn_b[0].reshape(N_PAIRS, 1, LANES),
              emean)

    wr = jnp.concatenate([w_expert_router[0].reshape(D, N_EXPERTS), w_group_router[0],
                          jnp.zeros((D, LANES - N_EXPERTS - N_GROUPS), F32)], axis=1)
    wr3 = jnp.stack(_split2(wr))
    br = jnp.concatenate([b_expert_router[0].reshape(-1), b_group_router[0],
                          jnp.zeros((LANES - N_EXPERTS - N_GROUPS,), F32)]).reshape(1, LANES)
    base, x1, route, counts = _mixer(x, row(ln_emb_g), row(ln_emb_b), ya, yb, w_out[0].astype(BF16), row(ln1_g[0]),
                                     row(ln1_b[0]), wr3, br, p[0], w_ple_gate[0].astype(BF16),
                                     row(b_ple_gate[0]), w_ple_proj[0].astype(BF16))
    base = base.reshape(T, D)
    x1 = x1.reshape(T, D // 2)
    route = route.reshape(T, LANES)

    n_blocks = -(-(T * TOP_K) // EXPERT_ROWS) + N_EXPERTS
    dest, pend = _slots(route, counts)
    dest_flat = dest[:, :TOP_K].reshape(T * TOP_K)
    pend = pend[0, :N_EXPERTS]

    xs = _dispatch(pend, dest_flat, x1, n_blocks * EXPERT_ROWS)
    ys = _experts(pend, xs, w_exp_gate[0], w_exp_up[0], w_exp_down[0])
    out = _combine(dest_flat, ys, base, route, row(ln2_g[0]), row(ln2_b[0]))
    return out.reshape(B, S, D)
```

```python
import functools
import math

import jax
import jax.numpy as jnp
from jax import lax
from jax.experimental import pallas as pl
from jax.experimental.pallas import tpu as pltpu

F32 = jnp.float32
BF16 = jnp.bfloat16

D_MODEL = 1024
D_RWKV = 512
HEAD = 64
D_GMLP = 512
GMLP_GROUPS = 4
GROUP_W = 128
GCHUNK = 128
DECAY_LORA = 64
ICLR_LORA = 64
GATE_LORA = 128
N_SHIFT = 3 * D_RWKV + DECAY_LORA + ICLR_LORA + GATE_LORA
D_IN = N_SHIFT + 2 * D_GMLP
D_PLE = 256
N_GROUPS = 4
EXPERTS_PER_GROUP = 8
N_EXPERTS = 32
TOP_K = 2
D_EXPERT = 512
DEPTH = 1
ALPHA = (2.0 * DEPTH) ** 0.25
LN_EPS = 1e-5
GN_EPS = 64e-5
DECAY_SCALE = math.exp(-0.5)

LANES = 128
WKV_CHUNK = 64
N_PAIRS = D_RWKV // LANES
VMEM_LIMIT = 56 * 1024 * 1024

PREP_TM = 256
WKV_TB = 256
WKV_PAIRS = 4
MIX_TM = 256
SLOT_TM = 512
EXPERT_ROWS = 256
DISPATCH_TM = 512
COMBINE_TM = 256
NEG = -1e30


def _dot(a, b):
    return jnp.dot(a.astype(BF16), b.astype(BF16), preferred_element_type=F32)


def _dot_nt(a, b):
    return lax.dot_general(a.astype(BF16), b.astype(BF16), (((1,), (1,)), ((), ())),
                           preferred_element_type=F32)


def _split3(x):
    hi = x.astype(BF16)
    r1 = x - hi.astype(F32)
    mid = r1.astype(BF16)
    lo = (r1 - mid.astype(F32)).astype(BF16)
    return hi, mid, lo


def _dot3_lhs(x, w):
    hi, mid, lo = _split3(x)
    w = w.astype(BF16)
    return (jnp.dot(hi, w, preferred_element_type=F32) + jnp.dot(mid, w, preferred_element_type=F32)
            + jnp.dot(lo, w, preferred_element_type=F32))


def _split2(x):
    hi = x.astype(BF16)
    return hi, (x - hi.astype(F32)).astype(BF16)


def _dot2_lhs(x, w2):
    hi, lo = _split2(x)
    return jnp.dot(jnp.concatenate([hi, lo], axis=1), w2, preferred_element_type=F32)


def _dot3_rhs(w3, x):
    hi, mid, lo = _split3(x)
    return jnp.dot(w3, jnp.concatenate([hi, mid, lo], axis=0), preferred_element_type=F32)


def _layer_norm(x, g, b, eps):
    mu = jnp.mean(x, axis=-1, keepdims=True)
    xc = x - mu
    var = jnp.mean(xc * xc, axis=-1, keepdims=True)
    return xc * lax.rsqrt(var + eps) * g + b


def _sigmoid(x):
    return 1.0 / (1.0 + jnp.exp(-x))


def _iota(shape, dim):
    return lax.broadcasted_iota(jnp.int32, shape, dim)


def _prep_kernel(x_ref, lng_ref, lnb_ref, win_ref, mu_ref, wwa_ref, w0a0_ref, wg_ref, kk_ref, ka_ref, rk_ref,
                 eones_ref, glng_ref, glnb_ref, wsp_ref, bsp_ref,
                 r_ref, lw_ref, k_ref, v_ref, a_ref, b_ref, g_ref, bonus_ref, yb_ref, carry_ref):
    tm = x_ref.shape[1]

    @pl.when(pl.program_id(1) == 0)
    def _():
        carry_ref[...] = jnp.zeros_like(carry_ref)

    x0 = _layer_norm(x_ref[0], lng_ref[...], lnb_ref[...], LN_EPS)
    proj = jnp.dot(x0.astype(BF16), win_ref[...], preferred_element_type=F32)

    h = proj[:, :N_SHIFT]
    rolled = pltpu.roll(h, 1, 0)
    first = _iota((tm, N_SHIFT), 0) == 0
    prev = jnp.where(first, jnp.broadcast_to(carry_ref[0:1, :], (tm, N_SHIFT)), rolled)
    carry_ref[0:1, :] = h[tm - 1:tm, :]
    h = h + (prev - h) * mu_ref[...]

    r = h[:, 0:D_RWKV]
    k = h[:, D_RWKV:2 * D_RWKV]
    v = h[:, 2 * D_RWKV:3 * D_RWKV]
    xwa = h[:, 3 * D_RWKV:3 * D_RWKV + LANES]
    xg = h[:, 3 * D_RWKV + LANES:N_SHIFT]

    lane = _iota((tm, LANES), 1)
    twa = jnp.where(lane < DECAY_LORA, jnp.tanh(xwa), xwa)
    da = _dot(twa, wwa_ref[...]) + w0a0_ref[...]
    logw = -DECAY_SCALE * _sigmoid(da[:, :D_RWKV])
    ag = _sigmoid(da[:, D_RWKV:])
    g = _dot(_sigmoid(xg), wg_ref[...])

    eones2 = eones_ref[...]

    def head_sum(t):
        half = 2 * LANES
        return jnp.concatenate([_dot2_lhs(t[:, :half], eones2), _dot2_lhs(t[:, half:], eones2)], axis=1)

    kk = k * kk_ref[...]
    kk = kk / jnp.maximum(jnp.sqrt(head_sum(kk * kk)), 1e-12)
    k = k * (1.0 + (ag - 1.0) * ka_ref[...])
    bonus = head_sum(r * k * rk_ref[...]) * v

    for p in range(N_PAIRS):
        sl = slice(p * LANES, (p + 1) * LANES)
        r_ref[0, p] = r[:, sl]
        lw_ref[0, p] = logw[:, sl]
        k_ref[0, p] = k[:, sl]
        v_ref[0, p] = v[:, sl]
        a_ref[0, p] = -kk[:, sl]
        b_ref[0, p] = (kk * ag)[:, sl]
        g_ref[0, p] = g[:, sl]
        bonus_ref[0, p] = bonus[:, sl]

    zin = proj[:, N_SHIFT:]
    z = 0.5 * zin * (1.0 + lax.erf(zin * (0.5 ** 0.5)))
    zu = z[:, :D_GMLP]
    zv = z[:, D_GMLP:]
    causal = _iota((GCHUNK, GCHUNK), 0) >= _iota((GCHUNK, GCHUNK), 1)
    for gi in range(GMLP_GROUPS):
        gs = slice(gi * GROUP_W, (gi + 1) * GROUP_W)
        zvn = _layer_norm(zv[:, gs], glng_ref[:, gs], glnb_ref[:, gs], LN_EPS)
        ws = jnp.where(causal, wsp_ref[gi], 0.0).astype(BF16)
        bcol = bsp_ref[:, gi:gi + 1]
        for c in range(tm // GCHUNK):
            ts = slice(c * GCHUNK, (c + 1) * GCHUNK)
            mixed = jnp.dot(ws, zvn[ts].astype(BF16), preferred_element_type=F32) + bcol
            yb_ref[0, ts, gs] = (zu[ts, gs] * mixed).astype(BF16)


def _prep(x, ln_g, ln_b, w_in, mu, wwa, w0a0, wg, k_k, k_a, r_k, eones, glng, glnb, wsp, bsp):
    B, S, _ = x.shape
    tm = PREP_TM
    const = lambda shape: pl.BlockSpec(shape, lambda b, s: (0,) * len(shape))
    pair_spec = pl.BlockSpec((1, N_PAIRS, tm, LANES), lambda b, s: (b, 0, s, 0))
    pair_shape = jax.ShapeDtypeStruct((B, N_PAIRS, S, LANES), F32)
    return pl.pallas_call(
        _prep_kernel,
        grid=(B, S // tm),
        in_specs=[
            pl.BlockSpec((1, tm, D_MODEL), lambda b, s: (b, s, 0)),
            const((1, D_MODEL)), const((1, D_MODEL)), const((D_MODEL, D_IN)), const((1, N_SHIFT)),
            const((LANES, 2 * D_RWKV)), const((1, 2 * D_RWKV)), const((GATE_LORA, D_RWKV)),
            const((1, D_RWKV)), const((1, D_RWKV)), const((1, D_RWKV)), const((4 * LANES, 2 * LANES)),
            const((1, D_GMLP)), const((1, D_GMLP)), const((GMLP_GROUPS, GCHUNK, GCHUNK)),
            const((GCHUNK, GMLP_GROUPS)),
        ],
        out_specs=[pair_spec] * 8 + [pl.BlockSpec((1, tm, D_GMLP), lambda b, s: (b, s, 0))],
        out_shape=[pair_shape] * 8 + [jax.ShapeDtypeStruct((B, S, D_GMLP), BF16)],
        scratch_shapes=[pltpu.VMEM((8, N_SHIFT), F32)],
        compiler_params=pltpu.CompilerParams(dimension_semantics=("arbitrary", "arbitrary"),
                                             vmem_limit_bytes=VMEM_LIMIT),
        name="prep",
    )(x, ln_g, ln_b, w_in, mu, wwa, w0a0, wg, k_k, k_a, r_k, eones, glng, glnb, wsp, bsp)


def _wkv_kernel(r_ref, lw_ref, k_ref, v_ref, a_ref, b_ref, g_ref, bonus_ref, gng_ref, gnb_ref, emean_ref,
                o_ref, h_ref):
    C = WKV_CHUNK
    tb = r_ref.shape[2]

    @pl.when(pl.program_id(2) == 0)
    def _():
        h_ref[...] = jnp.zeros_like(h_ref)

    tok = _iota((C, LANES), 0)
    lane = _iota((C, LANES), 1)
    head0 = lane < HEAD
    strict = tok > lane % HEAD
    incl = tok >= lane % HEAD
    eye_w = (tok == lane % HEAD).astype(F32)
    rr = _iota((LANES, LANES), 0)
    cc = _iota((LANES, LANES), 1)
    eye = (rr == cc).astype(F32)
    same_head = (rr < HEAD) == (cc < HEAD)
    ltri3 = (_iota((C, 3 * C), 0) >= _iota((C, 3 * C), 1) % C).astype(BF16)

    def stack(x):
        xb = x.astype(BF16)
        zero = jnp.zeros_like(xb)
        return jnp.concatenate([jnp.where(head0, xb, zero), jnp.where(head0, zero, xb)], axis=0)

    def stack2(x, y):
        return jnp.concatenate([stack(x), stack(y)], axis=1)

    n_pairs = r_ref.shape[1]
    n_chunks = tb // C
    units = [(q, c) for q in range(n_pairs) for c in range(n_chunks)]

    def load(ref):
        return [ref[0, q, c * C:(c + 1) * C, :] for q, c in units]

    r_, lw_, k_, v_, a_, b_ = (load(ref) for ref in (r_ref, lw_ref, k_ref, v_ref, a_ref, b_ref))
    cum_ = [_dot3_rhs(ltri3, lw) for lw in lw_]
    cend_ = [cum[C - 1:C, :] for cum in cum_]
    at_ = [a * jnp.exp(cum - lw) for a, cum, lw in zip(a_, cum_, lw_)]
    rt_ = [r * jnp.exp(cum) for r, cum in zip(r_, cum_)]
    ginv_ = [jnp.exp(-cum) for cum in cum_]
    gend_ = [jnp.exp(cend - cum) for cend, cum in zip(cend_, cum_)]
    bk_end_ = [jnp.concatenate([b * ge, k * ge], axis=0) for b, k, ge in zip(b_, k_, gend_)]
    vst_ = [stack(v) for v in v_]

    G_ = [_dot_nt(jnp.concatenate([at, rt], axis=0), jnp.concatenate([stack(b * gi), stack(k * gi)], axis=0))
          for at, rt, b, k, gi in zip(at_, rt_, b_, k_, ginv_)]
    n1_ = [jnp.where(strict, G[:C, :LANES], 0.0) for G in G_]
    aak_ = [jnp.where(strict, G[:C, LANES:], 0.0) for G in G_]
    arb_ = [jnp.where(incl, G[C:, :LANES], 0.0) for G in G_]
    ark_ = [jnp.where(incl, G[C:, LANES:], 0.0) for G in G_]
    av_ = [_dot(jnp.concatenate([aak, ark], axis=0), vst) for aak, ark, vst in zip(aak_, ark_, vst_)]

    s1_ = [stack(n1) for n1 in n1_]
    n2_ = [_dot(n1, s1) for n1, s1 in zip(n1_, s1_)]
    x_ = [_dot(n2, jnp.concatenate([s1, stack(n2)], axis=1)) for n2, s1 in zip(n2_, s1_)]
    t_ = [eye_w + n1 + n2 + x[:, :LANES] for n1, n2, x in zip(n1_, n2_, x_)]
    np_ = [x[:, LANES:] for x in x_]
    for _ in range(3):
        x_ = [_dot(npow, stack2(t, npow)) for t, npow in zip(t_, np_)]
        t_ = [t + x[:, :LANES] for t, x in zip(t_, x_)]
        np_ = [x[:, LANES:] for x in x_]
    t_ = [t + _dot(npow, stack(t)) for t, npow in zip(t_, np_)]

    x_ = [_dot(t, stack2(at, av[:C])) for t, at, av in zip(t_, at_, av_)]
    z_ = [_dot(arb, stack2(x[:, :LANES], x[:, LANES:])) for arb, x in zip(arb_, x_)]
    rp_ = [rt + z[:, :LANES] for rt, z in zip(rt_, z_)]
    p3_ = [z[:, LANES:] + av[C:] for z, av in zip(z_, av_)]
    rhs_ = [jnp.concatenate([x, jnp.concatenate([jnp.zeros_like(v), v], axis=1)], axis=0)
            for x, v in zip(x_, v_)]
    mq_ = [_dot(bk_end.T, rhs) for bk_end, rhs in zip(bk_end_, rhs_)]
    m_ = [eye * jnp.exp(cend) + jnp.where(same_head, mq[:, :LANES], 0.0) for cend, mq in zip(cend_, mq_)]
    q_ = [jnp.where(same_head, mq[:, LANES:], 0.0) for mq in mq_]

    emean = emean_ref[...]
    for q in range(n_pairs):
        H = h_ref[q]
        ys = []
        for c in range(n_chunks):
            u = q * n_chunks + c
            ys.append(_dot(rp_[u], H) + p3_[u])
            H = _dot(m_[u], H) + q_[u]
        h_ref[q] = H
        y = jnp.concatenate(ys, axis=0)
        mu = _dot2_lhs(y, emean)
        yc = y - mu
        var = _dot2_lhs(yc * yc, emean)
        yn = yc * lax.rsqrt(var + GN_EPS) * gng_ref[q] + gnb_ref[q]
        o_ref[0, q] = ((yn + bonus_ref[0, q]) * g_ref[0, q]).astype(BF16)


def _wkv(r, lw, k, v, a, b, g, bonus, gn_g, gn_b, emean):
    B, P, S, _ = r.shape
    tb = WKV_TB
    pp = WKV_PAIRS
    seq = pl.BlockSpec((1, pp, tb, LANES), lambda bi, p, s: (bi, p, s, 0))
    par = pl.BlockSpec((pp, 1, LANES), lambda bi, p, s: (p, 0, 0))
    return pl.pallas_call(
        _wkv_kernel,
        grid=(B, P // pp, S // tb),
        in_specs=[seq] * 8 + [par, par, pl.BlockSpec((2 * LANES, LANES), lambda bi, p, s: (0, 0))],
        out_specs=seq,
        out_shape=jax.ShapeDtypeStruct((B, P, S, LANES), BF16),
        scratch_shapes=[pltpu.VMEM((pp, LANES, LANES), F32)],
        compiler_params=pltpu.CompilerParams(dimension_semantics=("arbitrary", "arbitrary", "arbitrary"),
                                             vmem_limit_bytes=VMEM_LIMIT),
        name="wkv",
    )(r, lw, k, v, a, b, g, bonus, gn_g, gn_b, emean)


def _mixer_kernel(x_ref, lng_ref, lnb_ref, ya_ref, yb_ref, wout_ref, l1g_ref, l1b_ref, wr_ref, br_ref,
                  p_ref, wpg_ref, bpg_ref, wpp_ref, base_ref, x1_ref, route_ref, counts_ref, carry_ref):
    tm = x_ref.shape[1]

    @pl.when((pl.program_id(0) == 0) & (pl.program_id(1) == 0))
    def _():
        carry_ref[...] = jnp.zeros_like(carry_ref)

    x0 = _layer_norm(x_ref[0], lng_ref[...], lnb_ref[...], LN_EPS)
    ymix = jnp.concatenate([ya_ref[0, p] for p in range(N_PAIRS)] + [yb_ref[0]], axis=-1)
    mix = jnp.dot(ymix, wout_ref[...], preferred_element_type=F32)
    x1 = _layer_norm(ALPHA * x0 + mix, l1g_ref[...], l1b_ref[...], LN_EPS)
    x1b = x1.astype(BF16)
    half = D_MODEL // 2
    lo_bits = lax.bitcast_convert_type(x1b[:, :half].astype(F32), jnp.uint32)
    hi_bits = lax.bitcast_convert_type(x1b[:, half:].astype(F32), jnp.uint32)
    x1_ref[0] = (hi_bits & jnp.uint32(0xFFFF0000)) | (lo_bits >> 16)

    hi, mid = _split2(x1)
    whi = wr_ref[0]
    wmid = wr_ref[1]
    d = lambda u, w: jnp.dot(u, w, preferred_element_type=F32)
    logits = (d(hi, whi) + d(hi, wmid) + d(mid, whi)) + br_ref[...]
    lane = _iota((tm, LANES), 1).astype(F32)
    far = float(4 * LANES)
    is_g = jnp.where(lane >= N_EXPERTS, jnp.where(lane < N_EXPERTS + N_GROUPS, 1.0, 0.0), 0.0) > 0.5
    gl = jnp.where(is_g, logits, NEG)
    gmax = jnp.max(gl, axis=-1, keepdims=True)
    gsel = jnp.min(jnp.where(gl == gmax, lane, far), axis=-1, keepdims=True) - N_EXPERTS
    p_group = 1.0 / jnp.sum(jnp.where(is_g, jnp.exp(gl - gmax), 0.0), axis=-1, keepdims=True)
    grp_of_lane = jnp.floor(lane * (1.0 / EXPERTS_PER_GROUP))
    el = jnp.where(grp_of_lane == gsel, logits, NEG)
    v1 = jnp.max(el, axis=-1, keepdims=True)
    i1 = jnp.min(jnp.where(el == v1, lane, far), axis=-1, keepdims=True)
    el2 = jnp.where(lane == i1, NEG, el)
    v2 = jnp.max(el2, axis=-1, keepdims=True)
    i2 = jnp.min(jnp.where(el2 == v2, lane, far), axis=-1, keepdims=True)
    e21 = jnp.exp(v2 - v1)
    w1 = p_group / (1.0 + e21)
    w2 = p_group * e21 / (1.0 + e21)

    oh1 = lane == i1
    oh2 = lane == i2
    below = (_iota((tm, tm), 0) > _iota((tm, tm), 1)).astype(BF16)
    o1 = jnp.where(oh1, 1.0, 0.0)
    o2 = jnp.where(oh2, 1.0, 0.0)
    c1 = jnp.dot(below, o1.astype(BF16), preferred_element_type=F32)
    c2 = jnp.dot(below, o2.astype(BF16), preferred_element_type=F32)
    tot1 = jnp.sum(o1, axis=0, keepdims=True)
    carry = carry_ref[0:1, :]
    rank1 = jnp.sum(jnp.where(oh1, c1 + carry, 0.0), axis=-1, keepdims=True)
    rank2 = jnp.sum(jnp.where(oh2, c2 + carry + tot1, 0.0), axis=-1, keepdims=True)
    carry = carry + tot1 + jnp.sum(o2, axis=0, keepdims=True)
    carry_ref[0:1, :] = carry
    counts_ref[...] = jnp.broadcast_to(carry, counts_ref.shape)

    fields = (i1, i2, w1, w2, rank1, rank2)
    route = jnp.zeros((tm, LANES), F32)
    for n, f in enumerate(fields):
        route = jnp.where(lane == n, f, route)
    route_ref[0] = route

    gate = _sigmoid(jnp.dot(x1b, wpg_ref[...], preferred_element_type=F32) + bpg_ref[...])
    ple = gate * jnp.dot(p_ref[0].astype(BF16), wpp_ref[...], preferred_element_type=F32)
    base_ref[0] = ALPHA * x1 + ple


def _mixer(x, ln_g, ln_b, ya, yb, w_out, l1g, l1b, wr3, br, p, wpg, bpg, wpp):
    B, S, _ = x.shape
    tm = MIX_TM
    const = lambda shape: pl.BlockSpec(shape, lambda b, s: (0,) * len(shape))
    row = lambda w: pl.BlockSpec((1, tm, w), lambda b, s: (b, s, 0))
    return pl.pallas_call(
        _mixer_kernel,
        grid=(B, S // tm),
        in_specs=[
            row(D_MODEL), const((1, D_MODEL)), const((1, D_MODEL)),
            pl.BlockSpec((1, N_PAIRS, tm, LANES), lambda b, s: (b, 0, s, 0)), row(D_GMLP),
            const((D_MODEL, D_MODEL)), const((1, D_MODEL)), const((1, D_MODEL)),
            const((2, D_MODEL, LANES)), const((1, LANES)),
            row(D_PLE), const((D_MODEL, D_MODEL)), const((1, D_MODEL)), const((D_PLE, D_MODEL)),
        ],
        out_specs=[row(D_MODEL), row(D_MODEL // 2), row(LANES), const((8, LANES))],
        out_shape=[jax.ShapeDtypeStruct((B, S, D_MODEL), F32), jax.ShapeDtypeStruct((B, S, D_MODEL // 2), jnp.uint32),
                   jax.ShapeDtypeStruct((B, S, LANES), F32), jax.ShapeDtypeStruct((8, LANES), F32)],
        scratch_shapes=[pltpu.VMEM((8, LANES), F32)],
        compiler_params=pltpu.CompilerParams(dimension_semantics=("arbitrary", "arbitrary"),
                                             vmem_limit_bytes=VMEM_LIMIT),
        name="mixer",
    )(x, ln_g, ln_b, ya, yb, w_out, l1g, l1b, wr3, br, p, wpg, bpg, wpp)


def _slots_kernel(route_ref, counts_ref, dest_ref, pend_ref):
    tm = route_ref.shape[0]
    lane = _iota((tm, LANES), 1)
    route = route_ref[...]
    oh1 = lane == route[:, 0:1].astype(jnp.int32)
    oh2 = lane == route[:, 1:2].astype(jnp.int32)

    counts = counts_ref[0:1, :]
    padded = jnp.floor((counts + (EXPERT_ROWS - 1)) * (1.0 / EXPERT_ROWS)) * EXPERT_ROWS
    upper = (_iota((LANES, LANES), 0) <= _iota((LANES, LANES), 1)).astype(BF16)
    pend = _dot3_lhs(jnp.broadcast_to(padded, (8, LANES)), upper)[0:1, :]
    pstart = pend - padded
    d1 = jnp.sum(jnp.where(oh1, pstart, 0.0), axis=-1, keepdims=True) + route[:, 4:5]
    d2 = jnp.sum(jnp.where(oh2, pstart, 0.0), axis=-1, keepdims=True) + route[:, 5:6]
    dest_ref[...] = jnp.where(lane == 0, d1, jnp.where(lane == 1, d2, 0.0)).astype(jnp.int32)
    pend_ref[...] = jnp.broadcast_to(pend, (8, LANES)).astype(jnp.int32)


def _slots(route, counts):
    T = route.shape[0]
    tm = SLOT_TM
    return pl.pallas_call(
        _slots_kernel,
        grid=(T // tm,),
        in_specs=[pl.BlockSpec((tm, LANES), lambda i: (i, 0)), pl.BlockSpec((8, LANES), lambda i: (0, 0))],
        out_specs=[pl.BlockSpec((tm, LANES), lambda i: (i, 0)),
                   pl.BlockSpec((8, LANES), lambda i: (0, 0))],
        out_shape=[jax.ShapeDtypeStruct((T, LANES), jnp.int32), jax.ShapeDtypeStruct((8, LANES), jnp.int32)],
        compiler_params=pltpu.CompilerParams(dimension_semantics=("arbitrary",), vmem_limit_bytes=VMEM_LIMIT),
        name="slots",
    )(route, counts)


def _dispatch_kernel(pend_ref, dest_ref, x_ref, xs_ref, zero_ref, sem, zsem):
    tm = dest_ref.shape[0] // TOP_K

    @pl.when(pl.program_id(0) == 0)
    def _():
        zero_ref[...] = jnp.zeros_like(zero_ref)

        def tail(e):
            start = pl.multiple_of(jnp.maximum(pend_ref[e] - EXPERT_ROWS, 0), EXPERT_ROWS)
            return pltpu.make_async_copy(zero_ref, xs_ref.at[pl.ds(start, EXPERT_ROWS)], zsem)

        def unused(j):
            return pltpu.make_async_copy(
                zero_ref, xs_ref.at[pl.ds(pl.multiple_of(j * EXPERT_ROWS, EXPERT_ROWS), EXPERT_ROWS)], zsem)

        def start_unused(j, _):
            unused(j).start()
            return 0

        def wait_unused(j, _):
            unused(j).wait()
            return 0

        first_unused = pend_ref[N_EXPERTS - 1] // EXPERT_ROWS
        n_blocks = xs_ref.shape[0] // EXPERT_ROWS
        for e in range(N_EXPERTS):
            tail(e).start()
        lax.fori_loop(first_unused, n_blocks, start_unused, 0)
        for e in range(N_EXPERTS):
            tail(e).wait()
        lax.fori_loop(first_unused, n_blocks, wait_unused, 0)

    for t in range(tm):
        for j in range(TOP_K):
            pltpu.make_async_copy(x_ref.at[pl.ds(t, 1)], xs_ref.at[pl.ds(dest_ref[TOP_K * t + j], 1)], sem).start()
    for j in range(TOP_K):
        pltpu.make_async_copy(x_ref, xs_ref.at[pl.ds(0, tm)], sem).wait()


def _dispatch(pend, dest_flat, x1, n_rows):
    T, width = x1.shape
    tm = DISPATCH_TM
    return pl.pallas_call(
        _dispatch_kernel,
        grid_spec=pltpu.PrefetchScalarGridSpec(
            num_scalar_prefetch=1,
            grid=(T // tm,),
            in_specs=[pl.BlockSpec((TOP_K * tm,), lambda i, pe: (i,), memory_space=pltpu.SMEM),
                      pl.BlockSpec((tm, width), lambda i, pe: (i, 0))],
            out_specs=pl.BlockSpec(memory_space=pl.ANY),
            scratch_shapes=[pltpu.VMEM((EXPERT_ROWS, width), x1.dtype), pltpu.SemaphoreType.DMA,
                            pltpu.SemaphoreType.DMA],
        ),
        out_shape=jax.ShapeDtypeStruct((n_rows, width), x1.dtype),
        compiler_params=pltpu.CompilerParams(dimension_semantics=("arbitrary",), vmem_limit_bytes=VMEM_LIMIT),
        name="dispatch",
    )(pend, dest_flat, x1)


def _experts_kernel(pend_ref, xs_ref, wg_ref, wu_ref, wd_ref, ys_ref, xbuf_ref, ybuf_ref, wgu_ref, wdb_ref,
                    in_sem, out_sem):
    rows = EXPERT_ROWS
    e = pl.program_id(0)
    first = jnp.where(e == 0, 0, pend_ref[jnp.maximum(e - 1, 0)]) // rows
    last = pend_ref[e] // rows
    n_used = pend_ref[N_EXPERTS - 1] // rows

    def block_rows(ref, b):
        return ref.at[pl.ds(pl.multiple_of(b * rows, rows), rows)]

    def x_copy(b, slot):
        return pltpu.make_async_copy(block_rows(xs_ref, b), xbuf_ref.at[slot], in_sem.at[slot])

    def y_copy(b, slot):
        return pltpu.make_async_copy(ybuf_ref.at[slot], block_rows(ys_ref, b), out_sem.at[slot])

    @pl.when((e == 0) & (n_used > 0))
    def _():
        x_copy(0, 0).start()

    @pl.when(last > first)
    def _():
        wgu_ref[:, :D_EXPERT] = wg_ref[0].astype(BF16)
        wgu_ref[:, D_EXPERT:] = wu_ref[0].astype(BF16)
        wdb_ref[...] = wd_ref[0].astype(BF16)

        def body(b, _):
            slot = b % 2

            @pl.when(b + 1 < n_used)
            def _():
                x_copy(b + 1, 1 - slot).start()

            x_copy(b, slot).wait()

            @pl.when(b >= 2)
            def _():
                y_copy(b - 2, slot).wait()

            xw = xbuf_ref[slot]
            x_lo = lax.bitcast_convert_type(xw << 16, F32)
            x_hi = lax.bitcast_convert_type(xw & jnp.uint32(0xFFFF0000), F32)
            xb = jnp.concatenate([x_lo, x_hi], axis=1).astype(BF16)
            h = jnp.dot(xb, wgu_ref[...], preferred_element_type=F32)
            hg = h[:, :D_EXPERT]
            hid = hg * _sigmoid(hg) * h[:, D_EXPERT:]
            ybuf_ref[slot] = jnp.dot(hid.astype(BF16), wdb_ref[...], preferred_element_type=F32)
            y_copy(b, slot).start()
            return 0

        lax.fori_loop(first, last, body, 0)

    @pl.when(e == N_EXPERTS - 1)
    def _():
        @pl.when(n_used >= 2)
        def _():
            y_copy(n_used - 2, n_used % 2).wait()

        @pl.when(n_used >= 1)
        def _():
            y_copy(n_used - 1, (n_used - 1) % 2).wait()

        ybuf_ref[0] = jnp.zeros(ybuf_ref.shape[1:], F32)
        first_unused = n_used
        n_tail = ys_ref.shape[0] // rows

        def start_unused(b, _):
            y_copy(b, 0).start()
            return 0

        def wait_unused(b, _):
            y_copy(b, 0).wait()
            return 0

        lax.fori_loop(first_unused, n_tail, start_unused, 0)
        lax.fori_loop(first_unused, n_tail, wait_unused, 0)


def _experts(pend, xs, wg, wu, wd):
    n_rows = xs.shape[0]
    rows = EXPERT_ROWS
    wspec = lambda shape: pl.BlockSpec((1,) + shape, lambda e, pe: (e, 0, 0))
    return pl.pallas_call(
        _experts_kernel,
        grid_spec=pltpu.PrefetchScalarGridSpec(
            num_scalar_prefetch=1,
            grid=(N_EXPERTS,),
            in_specs=[pl.BlockSpec(memory_space=pl.ANY),
                      wspec((D_MODEL, D_EXPERT)), wspec((D_MODEL, D_EXPERT)), wspec((D_EXPERT, D_MODEL))],
            out_specs=pl.BlockSpec(memory_space=pl.ANY),
            scratch_shapes=[pltpu.VMEM((2, rows, D_MODEL // 2), jnp.uint32), pltpu.VMEM((2, rows, D_MODEL), F32),
                            pltpu.VMEM((D_MODEL, 2 * D_EXPERT), BF16), pltpu.VMEM((D_EXPERT, D_MODEL), BF16),
                            pltpu.SemaphoreType.DMA((2,)), pltpu.SemaphoreType.DMA((2,))],
        ),
        out_shape=jax.ShapeDtypeStruct((n_rows, D_MODEL), F32),
        compiler_params=pltpu.CompilerParams(dimension_semantics=("arbitrary",), vmem_limit_bytes=VMEM_LIMIT),
        name="experts",
    )(pend, xs, wg, wu, wd)


def _combine_kernel(dest_ref, dest_next_ref, ys_ref, base_ref, route_ref, lg_ref, lb_ref, o_ref, buf_ref, sem):
    tm = buf_ref.shape[2]
    i = pl.program_id(0)

    def gather(dref, offset, s):
        for t in range(tm):
            for j in range(TOP_K):
                pltpu.make_async_copy(ys_ref.at[pl.ds(dref[offset + TOP_K * t + j], 1)],
                                      buf_ref.at[s, j, pl.ds(t, 1)], sem.at[s]).start()

    def drain(s):
        for j in range(TOP_K):
            pltpu.make_async_copy(ys_ref.at[pl.ds(0, tm)], buf_ref.at[s, j], sem.at[s]).wait()

    def finish(s):
        rows = slice(s * tm, (s + 1) * tm)
        drain(s)
        route = route_ref[rows, :]
        ffn = buf_ref[s, 0] * route[:, 2:3] + buf_ref[s, 1] * route[:, 3:4]
        o_ref[rows, :] = _layer_norm(base_ref[rows, :] + ffn, lg_ref[...], lb_ref[...], LN_EPS)

    @pl.when(i == 0)
    def _():
        gather(dest_ref, 0, 0)

    gather(dest_ref, TOP_K * tm, 1)
    finish(0)
    gather(dest_next_ref, 0, 0)
    finish(1)

    @pl.when(i == pl.num_programs(0) - 1)
    def _():
        drain(0)


def _combine(dest_flat, ys, base, route, l2g, l2b):
    T = base.shape[0]
    tm = COMBINE_TM
    nt = T // tm
    return pl.pallas_call(
        _combine_kernel,
        grid=(nt // 2,),
        in_specs=[pl.BlockSpec((2 * TOP_K * tm,), lambda i: (i,), memory_space=pltpu.SMEM),
                  pl.BlockSpec((TOP_K * tm,), lambda i: (jnp.minimum(2 * i + 2, nt - 1),), memory_space=pltpu.SMEM),
                  pl.BlockSpec(memory_space=pl.ANY),
                  pl.BlockSpec((2 * tm, D_MODEL), lambda i: (i, 0)), pl.BlockSpec((2 * tm, LANES), lambda i: (i, 0)),
                  pl.BlockSpec((1, D_MODEL), lambda i: (0, 0)), pl.BlockSpec((1, D_MODEL), lambda i: (0, 0))],
        out_specs=pl.BlockSpec((2 * tm, D_MODEL), lambda i: (i, 0)),
        out_shape=jax.ShapeDtypeStruct((T, D_MODEL), F32),
        scratch_shapes=[pltpu.VMEM((2, TOP_K, tm, D_MODEL), F32), pltpu.SemaphoreType.DMA((2,))],
        compiler_params=pltpu.CompilerParams(dimension_semantics=("arbitrary",), vmem_limit_bytes=VMEM_LIMIT),
        name="combine",
    )(dest_flat, dest_flat, ys, base, route, l2g, l2b)


def _block_diag_const(n, blk, val):
    idx = jnp.arange(n) // blk
    return jnp.where(idx[:, None] == idx[None, :], val, 0.0).astype(BF16)


def kernel(x, p, ln_emb_g, ln_emb_b, w_in, mu_shift, w0, w_decay_up, a0, w_iclr_up, w_gate_up, k_k, k_a, r_k, gn_g, gn_b, gmlp_ln_g, gmlp_ln_b, w_spatial, b_spatial, w_out, ln1_g, ln1_b, w_group_router, b_group_router, w_expert_router, b_expert_router, w_exp_gate, w_exp_up, w_exp_down, w_ple_gate, b_ple_gate, w_ple_proj, ln2_g, ln2_b):
    B, S, D = x.shape
    T = B * S
    row = lambda t: t.reshape(1, -1).astype(F32)

    zl = jnp.zeros((DECAY_LORA, D_RWKV), F32)
    wwa = jnp.concatenate([jnp.concatenate([w_decay_up[0], zl], axis=1),
                           jnp.concatenate([zl, w_iclr_up[0]], axis=1)], axis=0).astype(BF16)
    w0a0 = jnp.concatenate([w0[0], a0[0]]).reshape(1, -1)
    eones = jnp.tile(_block_diag_const(2 * LANES, HEAD, 1.0), (2, 1))
    emean = jnp.tile(_block_diag_const(LANES, HEAD, 1.0 / HEAD), (2, 1))

    r, lw, k, v, a, b, g, bonus, yb = _prep(
        x, row(ln_emb_g), row(ln_emb_b), w_in[0].astype(BF16), row(mu_shift[0]), wwa, w0a0,
        w_gate_up[0].astype(BF16), row(k_k[0]), row(k_a[0]), row(r_k[0]), eones,
        row(gmlp_ln_g[0]), row(gmlp_ln_b[0]), w_spatial[0], b_spatial[0].T)

    ya = _wkv(r, lw, k, v, a, b, g, bonus, gn_g[0].reshape(N_PAIRS, 1, LANES), gn_b[0].reshape(N_PAIRS, 1, LANES),
              emean)

    wr = jnp.concatenate([w_expert_router[0].reshape(D, N_EXPERTS), w_group_router[0],
                          jnp.zeros((D, LANES - N_EXPERTS - N_GROUPS), F32)], axis=1)
    wr3 = jnp.stack(_split2(wr))
    br = jnp.concatenate([b_expert_router[0].reshape(-1), b_group_router[0],
                          jnp.zeros((LANES - N_EXPERTS - N_GROUPS,), F32)]).reshape(1, LANES)
    base, x1, route, counts = _mixer(x, row(ln_emb_g), row(ln_emb_b), ya, yb, w_out[0].astype(BF16), row(ln1_g[0]),
                                     row(ln1_b[0]), wr3, br, p[0], w_ple_gate[0].astype(BF16),
                                     row(b_ple_gate[0]), w_ple_proj[0].astype(BF16))
    base = base.reshape(T, D)
    x1 = x1.reshape(T, D // 2)
    route = route.reshape(T, LANES)

    n_blocks = -(-(T * TOP_K) // EXPERT_ROWS) + N_EXPERTS
    dest, pend = _slots(route, counts)
    dest_flat = dest[:, :TOP_K].reshape(T * TOP_K)
    pend = pend[0, :N_EXPERTS]

    xs = _dispatch(pend, dest_flat, x1, n_blocks * EXPERT_ROWS)
    ys = _experts(pend, xs, w_exp_gate[0], w_exp_up[0], w_exp_down[0])
    out = _combine(dest_flat, ys, base, route, row(ln2_g[0]), row(ln2_b[0]))
    return out.reshape(B, S, D)
```

```python
import functools
import math

import jax
import jax.numpy as jnp
from jax import lax
from jax.experimental import pallas as pl
from jax.experimental.pallas import tpu as pltpu

F32 = jnp.float32
BF16 = jnp.bfloat16

D_MODEL = 1024
D_RWKV = 512
HEAD = 64
D_GMLP = 512
GMLP_GROUPS = 4
GROUP_W = 128
GCHUNK = 128
DECAY_LORA = 64
ICLR_LORA = 64
GATE_LORA = 128
N_SHIFT = 3 * D_RWKV + DECAY_LORA + ICLR_LORA + GATE_LORA
D_IN = N_SHIFT + 2 * D_GMLP
D_PLE = 256
N_GROUPS = 4
EXPERTS_PER_GROUP = 8
N_EXPERTS = 32
TOP_K = 2
D_EXPERT = 512
DEPTH = 1
ALPHA = (2.0 * DEPTH) ** 0.25
LN_EPS = 1e-5
GN_EPS = 64e-5
DECAY_SCALE = math.exp(-0.5)

LANES = 128
WKV_CHUNK = 64
N_PAIRS = D_RWKV // LANES
VMEM_LIMIT = 56 * 1024 * 1024

PREP_TM = 256
WKV_TB = 256
WKV_PAIRS = 4
MIX_TM = 256
SLOT_TM = 512
EXPERT_ROWS = 256
DISPATCH_TM = 512
COMBINE_TM = 256
NEG = -1e30


def _dot(a, b):
    return jnp.dot(a.astype(BF16), b.astype(BF16), preferred_element_type=F32)


def _dot_nt(a, b):
    return lax.dot_general(a.astype(BF16), b.astype(BF16), (((1,), (1,)), ((), ())),
                           preferred_element_type=F32)


def _split3(x):
    hi = x.astype(BF16)
    r1 = x - hi.astype(F32)
    mid = r1.astype(BF16)
    lo = (r1 - mid.astype(F32)).astype(BF16)
    return hi, mid, lo


def _dot3_lhs(x, w):
    hi, mid, lo = _split3(x)
    w = w.astype(BF16)
    return (jnp.dot(hi, w, preferred_element_type=F32) + jnp.dot(mid, w, preferred_element_type=F32)
            + jnp.dot(lo, w, preferred_element_type=F32))


def _split2(x):
    hi = x.astype(BF16)
    return hi, (x - hi.astype(F32)).astype(BF16)


def _dot2_lhs(x, w2):
    hi, lo = _split2(x)
    return jnp.dot(jnp.concatenate([hi, lo], axis=1), w2, preferred_element_type=F32)


def _dot3_rhs(w3, x):
    hi, mid, lo = _split3(x)
    return jnp.dot(w3, jnp.concatenate([hi, mid, lo], axis=0), preferred_element_type=F32)


def _layer_norm(x, g, b, eps):
    mu = jnp.mean(x, axis=-1, keepdims=True)
    xc = x - mu
    var = jnp.mean(xc * xc, axis=-1, keepdims=True)
    return xc * lax.rsqrt(var + eps) * g + b


def _sigmoid(x):
    return 1.0 / (1.0 + jnp.exp(-x))


def _iota(shape, dim):
    return lax.broadcasted_iota(jnp.int32, shape, dim)


def _prep_kernel(x_ref, lng_ref, lnb_ref, win_ref, mu_ref, wwa_ref, w0a0_ref, wg_ref, kk_ref, ka_ref, rk_ref,
                 eones_ref, glng_ref, glnb_ref, wsp_ref, bsp_ref,
                 r_ref, lw_ref, k_ref, v_ref, a_ref, b_ref, g_ref, bonus_ref, yb_ref, carry_ref):
    tm = x_ref.shape[1]

    @pl.when(pl.program_id(1) == 0)
    def _():
        carry_ref[...] = jnp.zeros_like(carry_ref)

    x0 = _layer_norm(x_ref[0], lng_ref[...], lnb_ref[...], LN_EPS)
    proj = jnp.dot(x0.astype(BF16), win_ref[...], preferred_element_type=F32)

    h = proj[:, :N_SHIFT]
    rolled = pltpu.roll(h, 1, 0)
    first = _iota((tm, N_SHIFT), 0) == 0
    prev = jnp.where(first, jnp.broadcast_to(carry_ref[0:1, :], (tm, N_SHIFT)), rolled)
    carry_ref[0:1, :] = h[tm - 1:tm, :]
    h = h + (prev - h) * mu_ref[...]

    r = h[:, 0:D_RWKV]
    k = h[:, D_RWKV:2 * D_RWKV]
    v = h[:, 2 * D_RWKV:3 * D_RWKV]
    xwa = h[:, 3 * D_RWKV:3 * D_RWKV + LANES]
    xg = h[:, 3 * D_RWKV + LANES:N_SHIFT]

    lane = _iota((tm, LANES), 1)
    twa = jnp.where(lane < DECAY_LORA, jnp.tanh(xwa), xwa)
    da = _dot(twa, wwa_ref[...]) + w0a0_ref[...]
    logw = -DECAY_SCALE * _sigmoid(da[:, :D_RWKV])
    ag = _sigmoid(da[:, D_RWKV:])
    g = _dot(_sigmoid(xg), wg_ref[...])

    eones2 = eones_ref[...]

    def head_sum(t):
        half = 2 * LANES
        return jnp.concatenate([_dot2_lhs(t[:, :half], eones2), _dot2_lhs(t[:, half:], eones2)], axis=1)

    kk = k * kk_ref[...]
    kk = kk / jnp.maximum(jnp.sqrt(head_sum(kk * kk)), 1e-12)
    k = k * (1.0 + (ag - 1.0) * ka_ref[...])
    bonus = head_sum(r * k * rk_ref[...]) * v

    for p in range(N_PAIRS):
        sl = slice(p * LANES, (p + 1) * LANES)
        r_ref[0, p] = r[:, sl]
        lw_ref[0, p] = logw[:, sl]
        k_ref[0, p] = k[:, sl]
        v_ref[0, p] = v[:, sl]
        a_ref[0, p] = -kk[:, sl]
        b_ref[0, p] = (kk * ag)[:, sl]
        g_ref[0, p] = g[:, sl]
        bonus_ref[0, p] = bonus[:, sl]

    zin = proj[:, N_SHIFT:]
    z = 0.5 * zin * (1.0 + lax.erf(zin * (0.5 ** 0.5)))
    zu = z[:, :D_GMLP]
    zv = z[:, D_GMLP:]
    causal = _iota((GCHUNK, GCHUNK), 0) >= _iota((GCHUNK, GCHUNK), 1)
    for gi in range(GMLP_GROUPS):
        gs = slice(gi * GROUP_W, (gi + 1) * GROUP_W)
        zvn = _layer_norm(zv[:, gs], glng_ref[:, gs], glnb_ref[:, gs], LN_EPS)
        ws = jnp.where(causal, wsp_ref[gi], 0.0).astype(BF16)
        bcol = bsp_ref[:, gi:gi + 1]
        for c in range(tm // GCHUNK):
            ts = slice(c * GCHUNK, (c + 1) * GCHUNK)
            mixed = jnp.dot(ws, zvn[ts].astype(BF16), preferred_element_type=F32) + bcol
            yb_ref[0, ts, gs] = (zu[ts, gs] * mixed).astype(BF16)


def _prep(x, ln_g, ln_b, w_in, mu, wwa, w0a0, wg, k_k, k_a, r_k, eones, glng, glnb, wsp, bsp):
    B, S, _ = x.shape
    tm = PREP_TM
    const = lambda shape: pl.BlockSpec(shape, lambda b, s: (0,) * len(shape))
    pair_spec = pl.BlockSpec((1, N_PAIRS, tm, LANES), lambda b, s: (b, 0, s, 0))
    pair_shape = jax.ShapeDtypeStruct((B, N_PAIRS, S, LANES), F32)
    return pl.pallas_call(
        _prep_kernel,
        grid=(B, S // tm),
        in_specs=[
            pl.BlockSpec((1, tm, D_MODEL), lambda b, s: (b, s, 0)),
            const((1, D_MODEL)), const((1, D_MODEL)), const((D_MODEL, D_IN)), const((1, N_SHIFT)),
            const((LANES, 2 * D_RWKV)), const((1, 2 * D_RWKV)), const((GATE_LORA, D_RWKV)),
            const((1, D_RWKV)), const((1, D_RWKV)), const((1, D_RWKV)), const((4 * LANES, 2 * LANES)),
            const((1, D_GMLP)), const((1, D_GMLP)), const((GMLP_GROUPS, GCHUNK, GCHUNK)),
            const((GCHUNK, GMLP_GROUPS)),
        ],
        out_specs=[pair_spec] * 8 + [pl.BlockSpec((1, tm, D_GMLP), lambda b, s: (b, s, 0))],
        out_shape=[pair_shape] * 8 + [jax.ShapeDtypeStruct((B, S, D_GMLP), BF16)],
        scratch_shapes=[pltpu.VMEM((8, N_SHIFT), F32)],
        compiler_params=pltpu.CompilerParams(dimension_semantics=("arbitrary", "arbitrary"),
                                             vmem_limit_bytes=VMEM_LIMIT),
        name="prep",
    )(x, ln_g, ln_b, w_in, mu, wwa, w0a0, wg, k_k, k_a, r_k, eones, glng, glnb, wsp, bsp)


def _wkv_kernel(r_ref, lw_ref, k_ref, v_ref, a_ref, b_ref, g_ref, bonus_ref, gng_ref, gnb_ref, emean_ref,
                o_ref, h_ref):
    C = WKV_CHUNK
    tb = r_ref.shape[2]

    @pl.when(pl.program_id(2) == 0)
    def _():
        h_ref[...] = jnp.zeros_like(h_ref)

    tok = _iota((C, LANES), 0)
    lane = _iota((C, LANES), 1)
    head0 = lane < HEAD
    strict = tok > lane % HEAD
    incl = tok >= lane % HEAD
    eye_w = (tok == lane % HEAD).astype(F32)
    rr = _iota((LANES, LANES), 0)
    cc = _iota((LANES, LANES), 1)
    eye = (rr == cc).astype(F32)
    same_head = (rr < HEAD) == (cc < HEAD)
    ltri3 = (_iota((C, 3 * C), 0) >= _iota((C, 3 * C), 1) % C).astype(BF16)

    def stack(x):
        xb = x.astype(BF16)
        zero = jnp.zeros_like(xb)
        return jnp.concatenate([jnp.where(head0, xb, zero), jnp.where(head0, zero, xb)], axis=0)

    def stack2(x, y):
        return jnp.concatenate([stack(x), stack(y)], axis=1)

    n_pairs = r_ref.shape[1]
    n_chunks = tb // C
    units = [(q, c) for q in range(n_pairs) for c in range(n_chunks)]

    def load(ref):
        return [ref[0, q, c * C:(c + 1) * C, :] for q, c in units]

    r_, lw_, k_, v_, a_, b_ = (load(ref) for ref in (r_ref, lw_ref, k_ref, v_ref, a_ref, b_ref))
    cum_ = [_dot3_rhs(ltri3, lw) for lw in lw_]
    cend_ = [cum[C - 1:C, :] for cum in cum_]
    at_ = [a * jnp.exp(cum - lw) for a, cum, lw in zip(a_, cum_, lw_)]
    rt_ = [r * jnp.exp(cum) for r, cum in zip(r_, cum_)]
    ginv_ = [jnp.exp(-cum) for cum in cum_]
    gend_ = [jnp.exp(cend - cum) for cend, cum in zip(cend_, cum_)]
    bk_end_ = [jnp.concatenate([b * ge, k * ge], axis=0) for b, k, ge in zip(b_, k_, gend_)]
    vst_ = [stack(v) for v in v_]

    G_ = [_dot_nt(jnp.concatenate([at, rt], axis=0), jnp.concatenate([stack(b * gi), stack(k * gi)], axis=0))
          for at, rt, b, k, gi in zip(at_, rt_, b_, k_, ginv_)]
    n1_ = [jnp.where(strict, G[:C, :LANES], 0.0) for G in G_]
    aak_ = [jnp.where(strict, G[:C, LANES:], 0.0) for G in G_]
    arb_ = [jnp.where(incl, G[C:, :LANES], 0.0) for G in G_]
    ark_ = [jnp.where(incl, G[C:, LANES:], 0.0) for G in G_]
    av_ = [_dot(jnp.concatenate([aak, ark], axis=0), vst) for aak, ark, vst in zip(aak_, ark_, vst_)]

    s1_ = [stack(n1) for n1 in n1_]
    n2_ = [_dot(n1, s1) for n1, s1 in zip(n1_, s1_)]
    x_ = [_dot(n2, jnp.concatenate([s1, stack(n2)], axis=1)) for n2, s1 in zip(n2_, s1_)]
    t_ = [eye_w + n1 + n2 + x[:, :LANES] for n1, n2, x in zip(n1_, n2_, x_)]
    np_ = [x[:, LANES:] for x in x_]
    for _ in range(3):
        x_ = [_dot(npow, stack2(t, npow)) for t, npow in zip(t_, np_)]
        t_ = [t + x[:, :LANES] for t, x in zip(t_, x_)]
        np_ = [x[:, LANES:] for x in x_]
    t_ = [t + _dot(npow, stack(t)) for t, npow in zip(t_, np_)]

    x_ = [_dot(t, stack2(at, av[:C])) for t, at, av in zip(t_, at_, av_)]
    z_ = [_dot(arb, stack2(x[:, :LANES], x[:, LANES:])) for arb, x in zip(arb_, x_)]
    rp_ = [rt + z[:, :LANES] for rt, z in zip(rt_, z_)]
    p3_ = [z[:, LANES:] + av[C:] for z, av in zip(z_, av_)]
    rhs_ = [jnp.concatenate([x, jnp.concatenate([jnp.zeros_like(v), v], axis=1)], axis=0)
            for x, v in zip(x_, v_)]
    mq_ = [_dot(bk_end.T, rhs) for bk_end, rhs in zip(bk_end_, rhs_)]
    m_ = [eye * jnp.exp(cend) + jnp.where(same_head, mq[:, :LANES], 0.0) for cend, mq in zip(cend_, mq_)]
    q_ = [jnp.where(same_head, mq[:, LANES:], 0.0) for mq in mq_]

    pairs = range(n_pairs)
    H_ = [h_ref[q] for q in pairs]
    ys_ = [[] for _ in pairs]
    for c in range(n_chunks):
        for q in pairs:
            u = q * n_chunks + c
            ys_[q].append(_dot(rp_[u], H_[q]) + p3_[u])
        H_ = [_dot(m_[q * n_chunks + c], H_[q]) + q_[q * n_chunks + c] for q in pairs]
    for q in pairs:
        h_ref[q] = H_[q]

    emean = emean_ref[...]
    y_ = [jnp.concatenate(ys, axis=0) for ys in ys_]
    mu_ = [_dot2_lhs(y, emean) for y in y_]
    yc_ = [y - mu for y, mu in zip(y_, mu_)]
    var_ = [_dot2_lhs(yc * yc, emean) for yc in yc_]
    for q in pairs:
        yn = yc_[q] * lax.rsqrt(var_[q] + GN_EPS) * gng_ref[q] + gnb_ref[q]
        o_ref[0, q] = ((yn + bonus_ref[0, q]) * g_ref[0, q]).astype(BF16)


def _wkv(r, lw, k, v, a, b, g, bonus, gn_g, gn_b, emean):
    B, P, S, _ = r.shape
    tb = WKV_TB
    pp = WKV_PAIRS
    seq = pl.BlockSpec((1, pp, tb, LANES), lambda bi, p, s: (bi, p, s, 0))
    par = pl.BlockSpec((pp, 1, LANES), lambda bi, p, s: (p, 0, 0))
    return pl.pallas_call(
        _wkv_kernel,
        grid=(B, P // pp, S // tb),
        in_specs=[seq] * 8 + [par, par, pl.BlockSpec((2 * LANES, LANES), lambda bi, p, s: (0, 0))],
        out_specs=seq,
        out_shape=jax.ShapeDtypeStruct((B, P, S, LANES), BF16),
        scratch_shapes=[pltpu.VMEM((pp, LANES, LANES), F32)],
        compiler_params=pltpu.CompilerParams(dimension_semantics=("arbitrary", "arbitrary", "arbitrary"),
                                             vmem_limit_bytes=VMEM_LIMIT),
        name="wkv",
    )(r, lw, k, v, a, b, g, bonus, gn_g, gn_b, emean)


def _mixer_kernel(x_ref, lng_ref, lnb_ref, ya_ref, yb_ref, wout_ref, l1g_ref, l1b_ref, wr_ref, br_ref,
                  p_ref, wpg_ref, bpg_ref, wpp_ref, base_ref, x1_ref, route_ref, counts_ref, carry_ref):
    tm = x_ref.shape[1]

    @pl.when((pl.program_id(0) == 0) & (pl.program_id(1) == 0))
    def _():
        carry_ref[...] = jnp.zeros_like(carry_ref)

    x0 = _layer_norm(x_ref[0], lng_ref[...], lnb_ref[...], LN_EPS)
    ymix = jnp.concatenate([ya_ref[0, p] for p in range(N_PAIRS)] + [yb_ref[0]], axis=-1)
    mix = jnp.dot(ymix, wout_ref[...], preferred_element_type=F32)
    x1 = _layer_norm(ALPHA * x0 + mix, l1g_ref[...], l1b_ref[...], LN_EPS)
    x1b = x1.astype(BF16)
    half = D_MODEL // 2
    lo_bits = lax.bitcast_convert_type(x1b[:, :half].astype(F32), jnp.uint32)
    hi_bits = lax.bitcast_convert_type(x1b[:, half:].astype(F32), jnp.uint32)
    x1_ref[0] = (hi_bits & jnp.uint32(0xFFFF0000)) | (lo_bits >> 16)

    hi, mid = _split2(x1)
    whi = wr_ref[0]
    wmid = wr_ref[1]
    d = lambda u, w: jnp.dot(u, w, preferred_element_type=F32)
    logits = (d(hi, whi) + d(hi, wmid) + d(mid, whi)) + br_ref[...]
    lane = _iota((tm, LANES), 1).astype(F32)
    far = float(4 * LANES)
    is_g = jnp.where(lane >= N_EXPERTS, jnp.where(lane < N_EXPERTS + N_GROUPS, 1.0, 0.0), 0.0) > 0.5
    gl = jnp.where(is_g, logits, NEG)
    gmax = jnp.max(gl, axis=-1, keepdims=True)
    gsel = jnp.min(jnp.where(gl == gmax, lane, far), axis=-1, keepdims=True) - N_EXPERTS
    p_group = 1.0 / jnp.sum(jnp.where(is_g, jnp.exp(gl - gmax), 0.0), axis=-1, keepdims=True)
    grp_of_lane = jnp.floor(lane * (1.0 / EXPERTS_PER_GROUP))
    el = jnp.where(grp_of_lane == gsel, logits, NEG)
    v1 = jnp.max(el, axis=-1, keepdims=True)
    i1 = jnp.min(jnp.where(el == v1, lane, far), axis=-1, keepdims=True)
    el2 = jnp.where(lane == i1, NEG, el)
    v2 = jnp.max(el2, axis=-1, keepdims=True)
    i2 = jnp.min(jnp.where(el2 == v2, lane, far), axis=-1, keepdims=True)
    e21 = jnp.exp(v2 - v1)
    w1 = p_group / (1.0 + e21)
    w2 = p_group * e21 / (1.0 + e21)

    oh1 = lane == i1
    oh2 = lane == i2
    below = (_iota((tm, tm), 0) > _iota((tm, tm), 1)).astype(BF16)
    o1 = jnp.where(oh1, 1.0, 0.0)
    o2 = jnp.where(oh2, 1.0, 0.0)
    c1 = jnp.dot(below, o1.astype(BF16), preferred_element_type=F32)
    c2 = jnp.dot(below, o2.astype(BF16), preferred_element_type=F32)
    tot1 = jnp.sum(o1, axis=0, keepdims=True)
    carry = carry_ref[0:1, :]
    rank1 = jnp.sum(jnp.where(oh1, c1 + carry, 0.0), axis=-1, keepdims=True)
    rank2 = jnp.sum(jnp.where(oh2, c2 + carry + tot1, 0.0), axis=-1, keepdims=True)
    carry = carry + tot1 + jnp.sum(o2, axis=0, keepdims=True)
    carry_ref[0:1, :] = carry
    counts_ref[...] = jnp.broadcast_to(carry, counts_ref.shape)

    fields = (i1, i2, w1, w2, rank1, rank2)
    route = jnp.zeros((tm, LANES), F32)
    for n, f in enumerate(fields):
        route = jnp.where(lane == n, f, route)
    route_ref[0] = route

    gate = _sigmoid(jnp.dot(x1b, wpg_ref[...], preferred_element_type=F32) + bpg_ref[...])
    ple = gate * jnp.dot(p_ref[0].astype(BF16), wpp_ref[...], preferred_element_type=F32)
    base_ref[0] = ALPHA * x1 + ple


def _mixer(x, ln_g, ln_b, ya, yb, w_out, l1g, l1b, wr3, br, p, wpg, bpg, wpp):
    B, S, _ = x.shape
    tm = MIX_TM
    const = lambda shape: pl.BlockSpec(shape, lambda b, s: (0,) * len(shape))
    row = lambda w: pl.BlockSpec((1, tm, w), lambda b, s: (b, s, 0))
    return pl.pallas_call(
        _mixer_kernel,
        grid=(B, S // tm),
        in_specs=[
            row(D_MODEL), const((1, D_MODEL)), const((1, D_MODEL)),
            pl.BlockSpec((1, N_PAIRS, tm, LANES), lambda b, s: (b, 0, s, 0)), row(D_GMLP),
            const((D_MODEL, D_MODEL)), const((1, D_MODEL)), const((1, D_MODEL)),
            const((2, D_MODEL, LANES)), const((1, LANES)),
            row(D_PLE), const((D_MODEL, D_MODEL)), const((1, D_MODEL)), const((D_PLE, D_MODEL)),
        ],
        out_specs=[row(D_MODEL), row(D_MODEL // 2), row(LANES), const((8, LANES))],
        out_shape=[jax.ShapeDtypeStruct((B, S, D_MODEL), F32), jax.ShapeDtypeStruct((B, S, D_MODEL // 2), jnp.uint32),
                   jax.ShapeDtypeStruct((B, S, LANES), F32), jax.ShapeDtypeStruct((8, LANES), F32)],
        scratch_shapes=[pltpu.VMEM((8, LANES), F32)],
        compiler_params=pltpu.CompilerParams(dimension_semantics=("arbitrary", "arbitrary"),
                                             vmem_limit_bytes=VMEM_LIMIT),
        name="mixer",
    )(x, ln_g, ln_b, ya, yb, w_out, l1g, l1b, wr3, br, p, wpg, bpg, wpp)


def _slots_kernel(route_ref, counts_ref, dest_ref, pend_ref):
    tm = route_ref.shape[0]
    lane = _iota((tm, LANES), 1)
    route = route_ref[...]
    oh1 = lane == route[:, 0:1].astype(jnp.int32)
    oh2 = lane == route[:, 1:2].astype(jnp.int32)

    counts = counts_ref[0:1, :]
    padded = jnp.floor((counts + (EXPERT_ROWS - 1)) * (1.0 / EXPERT_ROWS)) * EXPERT_ROWS
    upper = (_iota((LANES, LANES), 0) <= _iota((LANES, LANES), 1)).astype(BF16)
    pend = _dot3_lhs(jnp.broadcast_to(padded, (8, LANES)), upper)[0:1, :]
    pstart = pend - padded
    d1 = jnp.sum(jnp.where(oh1, pstart, 0.0), axis=-1, keepdims=True) + route[:, 4:5]
    d2 = jnp.sum(jnp.where(oh2, pstart, 0.0), axis=-1, keepdims=True) + route[:, 5:6]
    dest_ref[...] = jnp.where(lane == 0, d1, jnp.where(lane == 1, d2, 0.0)).astype(jnp.int32)
    pend_ref[...] = jnp.broadcast_to(pend, (8, LANES)).astype(jnp.int32)


def _slots(route, counts):
    T = route.shape[0]
    tm = SLOT_TM
    return pl.pallas_call(
        _slots_kernel,
        grid=(T // tm,),
        in_specs=[pl.BlockSpec((tm, LANES), lambda i: (i, 0)), pl.BlockSpec((8, LANES), lambda i: (0, 0))],
        out_specs=[pl.BlockSpec((tm, LANES), lambda i: (i, 0)),
                   pl.BlockSpec((8, LANES), lambda i: (0, 0))],
        out_shape=[jax.ShapeDtypeStruct((T, LANES), jnp.int32), jax.ShapeDtypeStruct((8, LANES), jnp.int32)],
        compiler_params=pltpu.CompilerParams(dimension_semantics=("arbitrary",), vmem_limit_bytes=VMEM_LIMIT),
        name="slots",
    )(route, counts)


def _dispatch_kernel(pend_ref, dest_ref, x_ref, xs_ref, zero_ref, sem, zsem):
    tm = dest_ref.shape[0] // TOP_K

    @pl.when(pl.program_id(0) == 0)
    def _():
        zero_ref[...] = jnp.zeros_like(zero_ref)

        def tail(e):
            start = pl.multiple_of(jnp.maximum(pend_ref[e] - EXPERT_ROWS, 0), EXPERT_ROWS)
            return pltpu.make_async_copy(zero_ref, xs_ref.at[pl.ds(start, EXPERT_ROWS)], zsem)

        def unused(j):
            return pltpu.make_async_copy(
                zero_ref, xs_ref.at[pl.ds(pl.multiple_of(j * EXPERT_ROWS, EXPERT_ROWS), EXPERT_ROWS)], zsem)

        def start_unused(j, _):
            unused(j).start()
            return 0

        def wait_unused(j, _):
            unused(j).wait()
            return 0

        first_unused = pend_ref[N_EXPERTS - 1] // EXPERT_ROWS
        n_blocks = xs_ref.shape[0] // EXPERT_ROWS
        for e in range(N_EXPERTS):
            tail(e).start()
        lax.fori_loop(first_unused, n_blocks, start_unused, 0)
        for e in range(N_EXPERTS):
            tail(e).wait()
        lax.fori_loop(first_unused, n_blocks, wait_unused, 0)

    for t in range(tm):
        for j in range(TOP_K):
            pltpu.make_async_copy(x_ref.at[pl.ds(t, 1)], xs_ref.at[pl.ds(dest_ref[TOP_K * t + j], 1)], sem).start()
    for j in range(TOP_K):
        pltpu.make_async_copy(x_ref, xs_ref.at[pl.ds(0, tm)], sem).wait()


def _dispatch(pend, dest_flat, x1, n_rows):
    T, width = x1.shape
    tm = DISPATCH_TM
    return pl.pallas_call(
        _dispatch_kernel,
        grid_spec=pltpu.PrefetchScalarGridSpec(
            num_scalar_prefetch=1,
            grid=(T // tm,),
            in_specs=[pl.BlockSpec((TOP_K * tm,), lambda i, pe: (i,), memory_space=pltpu.SMEM),
                      pl.BlockSpec((tm, width), lambda i, pe: (i, 0))],
            out_specs=pl.BlockSpec(memory_space=pl.ANY),
            scratch_shapes=[pltpu.VMEM((EXPERT_ROWS, width), x1.dtype), pltpu.SemaphoreType.DMA,
                            pltpu.SemaphoreType.DMA],
        ),
        out_shape=jax.ShapeDtypeStruct((n_rows, width), x1.dtype),
        compiler_params=pltpu.CompilerParams(dimension_semantics=("arbitrary",), vmem_limit_bytes=VMEM_LIMIT),
        name="dispatch",
    )(pend, dest_flat, x1)


def _experts_kernel(pend_ref, xs_ref, wg_ref, wu_ref, wd_ref, ys_ref, xbuf_ref, ybuf_ref, wgu_ref, wdb_ref,
                    in_sem, out_sem):
    rows = EXPERT_ROWS
    e = pl.program_id(0)
    first = jnp.where(e == 0, 0, pend_ref[jnp.maximum(e - 1, 0)]) // rows
    last = pend_ref[e] // rows
    n_used = pend_ref[N_EXPERTS - 1] // rows

    def block_rows(ref, b):
        return ref.at[pl.ds(pl.multiple_of(b * rows, rows), rows)]

    def x_copy(b, slot):
        return pltpu.make_async_copy(block_rows(xs_ref, b), xbuf_ref.at[slot], in_sem.at[slot])

    def y_copy(b, slot):
        return pltpu.make_async_copy(ybuf_ref.at[slot], block_rows(ys_ref, b), out_sem.at[slot])

    @pl.when((e == 0) & (n_used > 0))
    def _():
        x_copy(0, 0).start()

    @pl.when(last > first)
    def _():
        wgu_ref[:, :D_EXPERT] = wg_ref[0].astype(BF16)
        wgu_ref[:, D_EXPERT:] = wu_ref[0].astype(BF16)
        wdb_ref[...] = wd_ref[0].astype(BF16)

        def body(b, _):
            slot = b % 2

            @pl.when(b + 1 < n_used)
            def _():
                x_copy(b + 1, 1 - slot).start()

            x_copy(b, slot).wait()

            @pl.when(b >= 2)
            def _():
                y_copy(b - 2, slot).wait()

            xw = xbuf_ref[slot]
            x_lo = lax.bitcast_convert_type(xw << 16, F32)
            x_hi = lax.bitcast_convert_type(xw & jnp.uint32(0xFFFF0000), F32)
            xb = jnp.concatenate([x_lo, x_hi], axis=1).astype(BF16)
            h = jnp.dot(xb, wgu_ref[...], preferred_element_type=F32)
            hg = h[:, :D_EXPERT]
            hid = hg * _sigmoid(hg) * h[:, D_EXPERT:]
            ybuf_ref[slot] = jnp.dot(hid.astype(BF16), wdb_ref[...], preferred_element_type=F32)
            y_copy(b, slot).start()
            return 0

        lax.fori_loop(first, last, body, 0)

    @pl.when(e == N_EXPERTS - 1)
    def _():
        @pl.when(n_used >= 2)
        def _():
            y_copy(n_used - 2, n_used % 2).wait()

        @pl.when(n_used >= 1)
        def _():
            y_copy(n_used - 1, (n_used - 1) % 2).wait()

        ybuf_ref[0] = jnp.zeros(ybuf_ref.shape[1:], F32)
        first_unused = n_used
        n_tail = ys_ref.shape[0] // rows

        def start_unused(b, _):
            y_copy(b, 0).start()
            return 0

        def wait_unused(b, _):
            y_copy(b, 0).wait()
            return 0

        lax.fori_loop(first_unused, n_tail, start_unused, 0)
        lax.fori_loop(first_unused, n_tail, wait_unused, 0)


def _experts(pend, xs, wg, wu, wd):
    n_rows = xs.shape[0]
    rows = EXPERT_ROWS
    wspec = lambda shape: pl.BlockSpec((1,) + shape, lambda e, pe: (e, 0, 0))
    return pl.pallas_call(
        _experts_kernel,
        grid_spec=pltpu.PrefetchScalarGridSpec(
            num_scalar_prefetch=1,
            grid=(N_EXPERTS,),
            in_specs=[pl.BlockSpec(memory_space=pl.ANY),
                      wspec((D_MODEL, D_EXPERT)), wspec((D_MODEL, D_EXPERT)), wspec((D_EXPERT, D_MODEL))],
            out_specs=pl.BlockSpec(memory_space=pl.ANY),
            scratch_shapes=[pltpu.VMEM((2, rows, D_MODEL // 2), jnp.uint32), pltpu.VMEM((2, rows, D_MODEL), F32),
                            pltpu.VMEM((D_MODEL, 2 * D_EXPERT), BF16), pltpu.VMEM((D_EXPERT, D_MODEL), BF16),
                            pltpu.SemaphoreType.DMA((2,)), pltpu.SemaphoreType.DMA((2,))],
        ),
        out_shape=jax.ShapeDtypeStruct((n_rows, D_MODEL), F32),
        compiler_params=pltpu.CompilerParams(dimension_semantics=("arbitrary",), vmem_limit_bytes=VMEM_LIMIT),
        name="experts",
    )(pend, xs, wg, wu, wd)


def _combine_kernel(dest_ref, dest_next_ref, ys_ref, base_ref, route_ref, lg_ref, lb_ref, o_ref, buf_ref, sem):
    tm = buf_ref.shape[2]
    i = pl.program_id(0)

    def gather(dref, offset, s):
        for t in range(tm):
            for j in range(TOP_K):
                pltpu.make_async_copy(ys_ref.at[pl.ds(dref[offset + TOP_K * t + j], 1)],
                                      buf_ref.at[s, j, pl.ds(t, 1)], sem.at[s]).start()

    def drain(s):
        for j in range(TOP_K):
            pltpu.make_async_copy(ys_ref.at[pl.ds(0, tm)], buf_ref.at[s, j], sem.at[s]).wait()

    def finish(s):
        rows = slice(s * tm, (s + 1) * tm)
        drain(s)
        route = route_ref[rows, :]
        ffn = buf_ref[s, 0] * route[:, 2:3] + buf_ref[s, 1] * route[:, 3:4]
        o_ref[rows, :] = _layer_norm(base_ref[rows, :] + ffn, lg_ref[...], lb_ref[...], LN_EPS)

    @pl.when(i == 0)
    def _():
        gather(dest_ref, 0, 0)

    gather(dest_ref, TOP_K * tm, 1)
    finish(0)
    gather(dest_next_ref, 0, 0)
    finish(1)

    @pl.when(i == pl.num_programs(0) - 1)
    def _():
        drain(0)


def _combine(dest_flat, ys, base, route, l2g, l2b):
    T = base.shape[0]
    tm = COMBINE_TM
    nt = T // tm
    return pl.pallas_call(
        _combine_kernel,
        grid=(nt // 2,),
        in_specs=[pl.BlockSpec((2 * TOP_K * tm,), lambda i: (i,), memory_space=pltpu.SMEM),
                  pl.BlockSpec((TOP_K * tm,), lambda i: (jnp.minimum(2 * i + 2, nt - 1),), memory_space=pltpu.SMEM),
                  pl.BlockSpec(memory_space=pl.ANY),
                  pl.BlockSpec((2 * tm, D_MODEL), lambda i: (i, 0)), pl.BlockSpec((2 * tm, LANES), lambda i: (i, 0)),
                  pl.BlockSpec((1, D_MODEL), lambda i: (0, 0)), pl.BlockSpec((1, D_MODEL), lambda i: (0, 0))],
        out_specs=pl.BlockSpec((2 * tm, D_MODEL), lambda i: (i, 0)),
        out_shape=jax.ShapeDtypeStruct((T, D_MODEL), F32),
        scratch_shapes=[pltpu.VMEM((2, TOP_K, tm, D_MODEL), F32), pltpu.SemaphoreType.DMA((2,))],
        compiler_params=pltpu.CompilerParams(dimension_semantics=("arbitrary",), vmem_limit_bytes=VMEM_LIMIT),
        name="combine",
    )(dest_flat, dest_flat, ys, base, route, l2g, l2b)


def _block_diag_const(n, blk, val):
    idx = jnp.arange(n) // blk
    return jnp.where(idx[:, None] == idx[None, :], val, 0.0).astype(BF16)


def kernel(x, p, ln_emb_g, ln_emb_b, w_in, mu_shift, w0, w_decay_up, a0, w_iclr_up, w_gate_up, k_k, k_a, r_k, gn_g, gn_b, gmlp_ln_g, gmlp_ln_b, w_spatial, b_spatial, w_out, ln1_g, ln1_b, w_group_router, b_group_router, w_expert_router, b_expert_router, w_exp_gate, w_exp_up, w_exp_down, w_ple_gate, b_ple_gate, w_ple_proj, ln2_g, ln2_b):
    B, S, D = x.shape
    T = B * S
    row = lambda t: t.reshape(1, -1).astype(F32)

    zl = jnp.zeros((DECAY_LORA, D_RWKV), F32)
    wwa = jnp.concatenate([jnp.concatenate([w_decay_up[0], zl], axis=1),
                           jnp.concatenate([zl, w_iclr_up[0]], axis=1)], axis=0).astype(BF16)
    w0a0 = jnp.concatenate([w0[0], a0[0]]).reshape(1, -1)
    eones = jnp.tile(_block_diag_const(2 * LANES, HEAD, 1.0), (2, 1))
    emean = jnp.tile(_block_diag_const(LANES, HEAD, 1.0 / HEAD), (2, 1))

    r, lw, k, v, a, b, g, bonus, yb = _prep(
        x, row(ln_emb_g), row(ln_emb_b), w_in[0].astype(BF16), row(mu_shift[0]), wwa, w0a0,
        w_gate_up[0].astype(BF16), row(k_k[0]), row(k_a[0]), row(r_k[0]), eones,
        row(gmlp_ln_g[0]), row(gmlp_ln_b[0]), w_spatial[0], b_spatial[0].T)

    ya = _wkv(r, lw, k, v, a, b, g, bonus, gn_g[0].reshape(N_PAIRS, 1, LANES), gn_b[0].reshape(N_PAIRS, 1, LANES),
              emean)

    wr = jnp.concatenate([w_expert_router[0].reshape(D, N_EXPERTS), w_group_router[0],
                          jnp.zeros((D, LANES - N_EXPERTS - N_GROUPS), F32)], axis=1)
    wr3 = jnp.stack(_split2(wr))
    br = jnp.concatenate([b_expert_router[0].reshape(-1), b_group_router[0],
                          jnp.zeros((LANES - N_EXPERTS - N_GROUPS,), F32)]).reshape(1, LANES)
    base, x1, route, counts = _mixer(x, row(ln_emb_g), row(ln_emb_b), ya, yb, w_out[0].astype(BF16), row(ln1_g[0]),
                                     row(ln1_b[0]), wr3, br, p[0], w_ple_gate[0].astype(BF16),
                                     row(b_ple_gate[0]), w_ple_proj[0].astype(BF16))
    base = base.reshape(T, D)
    x1 = x1.reshape(T, D // 2)
    route = route.reshape(T, LANES)

    n_blocks = -(-(T * TOP_K) // EXPERT_ROWS) + N_EXPERTS
    dest, pend = _slots(route, counts)
    dest_flat = dest[:, :TOP_K].reshape(T * TOP_K)
    pend = pend[0, :N_EXPERTS]

    xs = _dispatch(pend, dest_flat, x1, n_blocks * EXPERT_ROWS)
    ys = _experts(pend, xs, w_exp_gate[0], w_exp_up[0], w_exp_down[0])
    out = _combine(dest_flat, ys, base, route, row(ln2_g[0]), row(ln2_b[0]))
    return out.reshape(B, S, D)
```

```python
import functools
import math

import jax
import jax.numpy as jnp
from jax import lax
from jax.experimental import pallas as pl
from jax.experimental.pallas import tpu as pltpu

F32 = jnp.float32
BF16 = jnp.bfloat16

D_MODEL = 1024
D_RWKV = 512
HEAD = 64
D_GMLP = 512
GMLP_GROUPS = 4
GROUP_W = 128
GCHUNK = 128
DECAY_LORA = 64
ICLR_LORA = 64
GATE_LORA = 128
N_SHIFT = 3 * D_RWKV + DECAY_LORA + ICLR_LORA + GATE_LORA
D_IN = N_SHIFT + 2 * D_GMLP
D_PLE = 256
N_GROUPS = 4
EXPERTS_PER_GROUP = 8
N_EXPERTS = 32
TOP_K = 2
D_EXPERT = 512
DEPTH = 1
ALPHA = (2.0 * DEPTH) ** 0.25
LN_EPS = 1e-5
GN_EPS = 64e-5
DECAY_SCALE = math.exp(-0.5)

LANES = 128
WKV_CHUNK = 64
N_PAIRS = D_RWKV // LANES
VMEM_LIMIT = 56 * 1024 * 1024

PREP_TM = 512
WKV_TB = 256
WKV_PAIRS = 4
MIX_TM = 512
SLOT_TM = 2048
EXPERT_ROWS = 256
DISPATCH_TM = 512
COMBINE_TM = 256
NEG = -1e30


def _dot(a, b):
    return jnp.dot(a.astype(BF16), b.astype(BF16), preferred_element_type=F32)


def _dot_nt(a, b):
    return lax.dot_general(a.astype(BF16), b.astype(BF16), (((1,), (1,)), ((), ())),
                           preferred_element_type=F32)


def _split3(x):
    hi = x.astype(BF16)
    r1 = x - hi.astype(F32)
    mid = r1.astype(BF16)
    lo = (r1 - mid.astype(F32)).astype(BF16)
    return hi, mid, lo


def _dot3_lhs(x, w):
    hi, mid, lo = _split3(x)
    w = w.astype(BF16)
    return (jnp.dot(hi, w, preferred_element_type=F32) + jnp.dot(mid, w, preferred_element_type=F32)
            + jnp.dot(lo, w, preferred_element_type=F32))


def _split2(x):
    hi = x.astype(BF16)
    return hi, (x - hi.astype(F32)).astype(BF16)


def _dot2_lhs(x, w2):
    hi, lo = _split2(x)
    return jnp.dot(jnp.concatenate([hi, lo], axis=1), w2, preferred_element_type=F32)


def _dot3_rhs(w3, x):
    hi, mid, lo = _split3(x)
    return jnp.dot(w3, jnp.concatenate([hi, mid, lo], axis=0), preferred_element_type=F32)


def _layer_norm(x, g, b, eps):
    mu = jnp.mean(x, axis=-1, keepdims=True)
    xc = x - mu
    var = jnp.mean(xc * xc, axis=-1, keepdims=True)
    return xc * lax.rsqrt(var + eps) * g + b


def _sigmoid(x):
    return 1.0 / (1.0 + jnp.exp(-x))


def _iota(shape, dim):
    return lax.broadcasted_iota(jnp.int32, shape, dim)


def _prep_kernel(x_ref, lng_ref, lnb_ref, win_ref, mu_ref, wwa_ref, w0a0_ref, wg_ref, kk_ref, ka_ref, rk_ref,
                 eones_ref, glng_ref, glnb_ref, wsp_ref, bsp_ref,
                 r_ref, lw_ref, k_ref, v_ref, a_ref, b_ref, g_ref, bonus_ref, yb_ref, carry_ref):
    tm = x_ref.shape[1]

    @pl.when(pl.program_id(1) == 0)
    def _():
        carry_ref[...] = jnp.zeros_like(carry_ref)

    x0 = _layer_norm(x_ref[0], lng_ref[...], lnb_ref[...], LN_EPS)
    proj = jnp.dot(x0.astype(BF16), win_ref[...], preferred_element_type=F32)

    h = proj[:, :N_SHIFT]
    rolled = pltpu.roll(h, 1, 0)
    first = _iota((tm, N_SHIFT), 0) == 0
    prev = jnp.where(first, jnp.broadcast_to(carry_ref[0:1, :], (tm, N_SHIFT)), rolled)
    carry_ref[0:1, :] = h[tm - 1:tm, :]
    h = h + (prev - h) * mu_ref[...]

    r = h[:, 0:D_RWKV]
    k = h[:, D_RWKV:2 * D_RWKV]
    v = h[:, 2 * D_RWKV:3 * D_RWKV]
    xwa = h[:, 3 * D_RWKV:3 * D_RWKV + LANES]
    xg = h[:, 3 * D_RWKV + LANES:N_SHIFT]

    lane = _iota((tm, LANES), 1)
    twa = jnp.where(lane < DECAY_LORA, jnp.tanh(xwa), xwa)
    da = _dot(twa, wwa_ref[...]) + w0a0_ref[...]
    logw = -DECAY_SCALE * _sigmoid(da[:, :D_RWKV])
    ag = _sigmoid(da[:, D_RWKV:])
    g = _dot(_sigmoid(xg), wg_ref[...])

    eones2 = eones_ref[...]

    def head_sum(t):
        half = 2 * LANES
        return jnp.concatenate([_dot2_lhs(t[:, :half], eones2), _dot2_lhs(t[:, half:], eones2)], axis=1)

    kk = k * kk_ref[...]
    kk = kk / jnp.maximum(jnp.sqrt(head_sum(kk * kk)), 1e-12)
    k = k * (1.0 + (ag - 1.0) * ka_ref[...])
    bonus = head_sum(r * k * rk_ref[...]) * v

    for p in range(N_PAIRS):
        sl = slice(p * LANES, (p + 1) * LANES)
        r_ref[0, p] = r[:, sl]
        lw_ref[0, p] = logw[:, sl]
        k_ref[0, p] = k[:, sl]
        v_ref[0, p] = v[:, sl]
        a_ref[0, p] = -kk[:, sl]
        b_ref[0, p] = (kk * ag)[:, sl]
        g_ref[0, p] = g[:, sl]
        bonus_ref[0, p] = bonus[:, sl]

    zin = proj[:, N_SHIFT:]
    z = 0.5 * zin * (1.0 + lax.erf(zin * (0.5 ** 0.5)))
    zu = z[:, :D_GMLP]
    zv = z[:, D_GMLP:]
    causal = _iota((GCHUNK, GCHUNK), 0) >= _iota((GCHUNK, GCHUNK), 1)
    for gi in range(GMLP_GROUPS):
        gs = slice(gi * GROUP_W, (gi + 1) * GROUP_W)
        zvn = _layer_norm(zv[:, gs], glng_ref[:, gs], glnb_ref[:, gs], LN_EPS)
        ws = jnp.where(causal, wsp_ref[gi], 0.0).astype(BF16)
        bcol = bsp_ref[:, gi:gi + 1]
        for c in range(tm // GCHUNK):
            ts = slice(c * GCHUNK, (c + 1) * GCHUNK)
            mixed = jnp.dot(ws, zvn[ts].astype(BF16), preferred_element_type=F32) + bcol
            yb_ref[0, ts, gs] = (zu[ts, gs] * mixed).astype(BF16)


def _prep(x, ln_g, ln_b, w_in, mu, wwa, w0a0, wg, k_k, k_a, r_k, eones, glng, glnb, wsp, bsp):
    B, S, _ = x.shape
    tm = PREP_TM
    const = lambda shape: pl.BlockSpec(shape, lambda b, s: (0,) * len(shape))
    pair_spec = pl.BlockSpec((1, N_PAIRS, tm, LANES), lambda b, s: (b, 0, s, 0))
    pair_shape = jax.ShapeDtypeStruct((B, N_PAIRS, S, LANES), F32)
    return pl.pallas_call(
        _prep_kernel,
        grid=(B, S // tm),
        in_specs=[
            pl.BlockSpec((1, tm, D_MODEL), lambda b, s: (b, s, 0)),
            const((1, D_MODEL)), const((1, D_MODEL)), const((D_MODEL, D_IN)), const((1, N_SHIFT)),
            const((LANES, 2 * D_RWKV)), const((1, 2 * D_RWKV)), const((GATE_LORA, D_RWKV)),
            const((1, D_RWKV)), const((1, D_RWKV)), const((1, D_RWKV)), const((4 * LANES, 2 * LANES)),
            const((1, D_GMLP)), const((1, D_GMLP)), const((GMLP_GROUPS, GCHUNK, GCHUNK)),
            const((GCHUNK, GMLP_GROUPS)),
        ],
        out_specs=[pair_spec] * 8 + [pl.BlockSpec((1, tm, D_GMLP), lambda b, s: (b, s, 0))],
        out_shape=[pair_shape] * 8 + [jax.ShapeDtypeStruct((B, S, D_GMLP), BF16)],
        scratch_shapes=[pltpu.VMEM((8, N_SHIFT), F32)],
        compiler_params=pltpu.CompilerParams(dimension_semantics=("arbitrary", "arbitrary"),
                                             vmem_limit_bytes=VMEM_LIMIT),
        name="prep",
    )(x, ln_g, ln_b, w_in, mu, wwa, w0a0, wg, k_k, k_a, r_k, eones, glng, glnb, wsp, bsp)


def _wkv_kernel(r_ref, lw_ref, k_ref, v_ref, a_ref, b_ref, g_ref, bonus_ref, gng_ref, gnb_ref, emean_ref,
                o_ref, h_ref):
    C = WKV_CHUNK
    tb = r_ref.shape[2]

    @pl.when(pl.program_id(2) == 0)
    def _():
        h_ref[...] = jnp.zeros_like(h_ref)

    tok = _iota((C, LANES), 0)
    lane = _iota((C, LANES), 1)
    head0 = lane < HEAD
    strict = tok > lane % HEAD
    incl = tok >= lane % HEAD
    eye_w = (tok == lane % HEAD).astype(F32)
    rr = _iota((LANES, LANES), 0)
    cc = _iota((LANES, LANES), 1)
    eye = (rr == cc).astype(F32)
    same_head = (rr < HEAD) == (cc < HEAD)
    ltri3 = (_iota((C, 3 * C), 0) >= _iota((C, 3 * C), 1) % C).astype(BF16)

    def stack(x):
        xb = x.astype(BF16)
        zero = jnp.zeros_like(xb)
        return jnp.concatenate([jnp.where(head0, xb, zero), jnp.where(head0, zero, xb)], axis=0)

    def stack2(x, y):
        return jnp.concatenate([stack(x), stack(y)], axis=1)

    n_pairs = r_ref.shape[1]
    n_chunks = tb // C
    units = [(q, c) for q in range(n_pairs) for c in range(n_chunks)]

    def load(ref):
        return [ref[0, q, c * C:(c + 1) * C, :] for q, c in units]

    r_, lw_, k_, v_, a_, b_ = (load(ref) for ref in (r_ref, lw_ref, k_ref, v_ref, a_ref, b_ref))
    cum_ = [_dot3_rhs(ltri3, lw) for lw in lw_]
    cend_ = [cum[C - 1:C, :] for cum in cum_]
    at_ = [a * jnp.exp(cum - lw) for a, cum, lw in zip(a_, cum_, lw_)]
    rt_ = [r * jnp.exp(cum) for r, cum in zip(r_, cum_)]
    ginv_ = [jnp.exp(-cum) for cum in cum_]
    gend_ = [jnp.exp(cend - cum) for cend, cum in zip(cend_, cum_)]
    bk_end_ = [jnp.concatenate([b * ge, k * ge], axis=0) for b, k, ge in zip(b_, k_, gend_)]
    vst_ = [stack(v) for v in v_]

    G_ = [_dot_nt(jnp.concatenate([at, rt], axis=0), jnp.concatenate([stack(b * gi), stack(k * gi)], axis=0))
          for at, rt, b, k, gi in zip(at_, rt_, b_, k_, ginv_)]
    n1_ = [jnp.where(strict, G[:C, :LANES], 0.0) for G in G_]
    aak_ = [jnp.where(strict, G[:C, LANES:], 0.0) for G in G_]
    arb_ = [jnp.where(incl, G[C:, :LANES], 0.0) for G in G_]
    ark_ = [jnp.where(incl, G[C:, LANES:], 0.0) for G in G_]
    av_ = [_dot(jnp.concatenate([aak, ark], axis=0), vst) for aak, ark, vst in zip(aak_, ark_, vst_)]

    s1_ = [stack(n1) for n1 in n1_]
    n2_ = [_dot(n1, s1) for n1, s1 in zip(n1_, s1_)]
    x_ = [_dot(n2, jnp.concatenate([s1, stack(n2)], axis=1)) for n2, s1 in zip(n2_, s1_)]
    t_ = [eye_w + n1 + n2 + x[:, :LANES] for n1, n2, x in zip(n1_, n2_, x_)]
    np_ = [x[:, LANES:] for x in x_]
    for _ in range(3):
        x_ = [_dot(npow, stack2(t, npow)) for t, npow in zip(t_, np_)]
        t_ = [t + x[:, :LANES] for t, x in zip(t_, x_)]
        np_ = [x[:, LANES:] for x in x_]
    t_ = [t + _dot(npow, stack(t)) for t, npow in zip(t_, np_)]

    x_ = [_dot(t, stack2(at, av[:C])) for t, at, av in zip(t_, at_, av_)]
    z_ = [_dot(arb, stack2(x[:, :LANES], x[:, LANES:])) for arb, x in zip(arb_, x_)]
    rp_ = [rt + z[:, :LANES] for rt, z in zip(rt_, z_)]
    p3_ = [z[:, LANES:] + av[C:] for z, av in zip(z_, av_)]
    rhs_ = [jnp.concatenate([x, jnp.concatenate([jnp.zeros_like(v), v], axis=1)], axis=0)
            for x, v in zip(x_, v_)]
    mq_ = [_dot(bk_end.T, rhs) for bk_end, rhs in zip(bk_end_, rhs_)]
    m_ = [eye * jnp.exp(cend) + jnp.where(same_head, mq[:, :LANES], 0.0) for cend, mq in zip(cend_, mq_)]
    q_ = [jnp.where(same_head, mq[:, LANES:], 0.0) for mq in mq_]

    pairs = range(n_pairs)
    H_ = [h_ref[q] for q in pairs]
    ys_ = [[] for _ in pairs]
    for c in range(n_chunks):
        for q in pairs:
            u = q * n_chunks + c
            ys_[q].append(_dot(rp_[u], H_[q]) + p3_[u])
        H_ = [_dot(m_[q * n_chunks + c], H_[q]) + q_[q * n_chunks + c] for q in pairs]
    for q in pairs:
        h_ref[q] = H_[q]

    emean = emean_ref[...]
    y_ = [jnp.concatenate(ys, axis=0) for ys in ys_]
    mu_ = [_dot2_lhs(y, emean) for y in y_]
    yc_ = [y - mu for y, mu in zip(y_, mu_)]
    var_ = [_dot2_lhs(yc * yc, emean) for yc in yc_]
    for q in pairs:
        yn = yc_[q] * lax.rsqrt(var_[q] + GN_EPS) * gng_ref[q] + gnb_ref[q]
        o_ref[0, q] = ((yn + bonus_ref[0, q]) * g_ref[0, q]).astype(BF16)


def _wkv(r, lw, k, v, a, b, g, bonus, gn_g, gn_b, emean):
    B, P, S, _ = r.shape
    tb = WKV_TB
    pp = WKV_PAIRS
    seq = pl.BlockSpec((1, pp, tb, LANES), lambda bi, p, s: (bi, p, s, 0))
    par = pl.BlockSpec((pp, 1, LANES), lambda bi, p, s: (p, 0, 0))
    return pl.pallas_call(
        _wkv_kernel,
        grid=(B, P // pp, S // tb),
        in_specs=[seq] * 8 + [par, par, pl.BlockSpec((2 * LANES, LANES), lambda bi, p, s: (0, 0))],
        out_specs=seq,
        out_shape=jax.ShapeDtypeStruct((B, P, S, LANES), BF16),
        scratch_shapes=[pltpu.VMEM((pp, LANES, LANES), F32)],
        compiler_params=pltpu.CompilerParams(dimension_semantics=("arbitrary", "arbitrary", "arbitrary"),
                                             vmem_limit_bytes=VMEM_LIMIT),
        name="wkv",
    )(r, lw, k, v, a, b, g, bonus, gn_g, gn_b, emean)


def _mixer_kernel(x_ref, lng_ref, lnb_ref, ya_ref, yb_ref, wout_ref, l1g_ref, l1b_ref, wr_ref, br_ref,
                  p_ref, wpg_ref, bpg_ref, wpp_ref, base_ref, x1_ref, route_ref, counts_ref, carry_ref):
    tm = x_ref.shape[1]

    @pl.when((pl.program_id(0) == 0) & (pl.program_id(1) == 0))
    def _():
        carry_ref[...] = jnp.zeros_like(carry_ref)

    x0 = _layer_norm(x_ref[0], lng_ref[...], lnb_ref[...], LN_EPS)
    ymix = jnp.concatenate([ya_ref[0, p] for p in range(N_PAIRS)] + [yb_ref[0]], axis=-1)
    mix = jnp.dot(ymix, wout_ref[...], preferred_element_type=F32)
    x1 = _layer_norm(ALPHA * x0 + mix, l1g_ref[...], l1b_ref[...], LN_EPS)
    x1b = x1.astype(BF16)
    half = D_MODEL // 2
    lo_bits = lax.bitcast_convert_type(x1b[:, :half].astype(F32), jnp.uint32)
    hi_bits = lax.bitcast_convert_type(x1b[:, half:].astype(F32), jnp.uint32)
    x1_ref[0] = (hi_bits & jnp.uint32(0xFFFF0000)) | (lo_bits >> 16)

    hi, mid = _split2(x1)
    whi = wr_ref[0]
    wmid = wr_ref[1]
    d = lambda u, w: jnp.dot(u, w, preferred_element_type=F32)
    logits = (d(hi, whi) + d(hi, wmid) + d(mid, whi)) + br_ref[...]
    lane = _iota((tm, LANES), 1).astype(F32)
    far = float(4 * LANES)
    is_g = jnp.where(lane >= N_EXPERTS, jnp.where(lane < N_EXPERTS + N_GROUPS, 1.0, 0.0), 0.0) > 0.5
    gl = jnp.where(is_g, logits, NEG)
    gmax = jnp.max(gl, axis=-1, keepdims=True)
    gsel = jnp.min(jnp.where(gl == gmax, lane, far), axis=-1, keepdims=True) - N_EXPERTS
    p_group = 1.0 / jnp.sum(jnp.where(is_g, jnp.exp(gl - gmax), 0.0), axis=-1, keepdims=True)
    grp_of_lane = jnp.floor(lane * (1.0 / EXPERTS_PER_GROUP))
    el = jnp.where(grp_of_lane == gsel, logits, NEG)
    v1 = jnp.max(el, axis=-1, keepdims=True)
    i1 = jnp.min(jnp.where(el == v1, lane, far), axis=-1, keepdims=True)
    el2 = jnp.where(lane == i1, NEG, el)
    v2 = jnp.max(el2, axis=-1, keepdims=True)
    i2 = jnp.min(jnp.where(el2 == v2, lane, far), axis=-1, keepdims=True)
    e21 = jnp.exp(v2 - v1)
    w1 = p_group / (1.0 + e21)
    w2 = p_group * e21 / (1.0 + e21)

    oh1 = lane == i1
    oh2 = lane == i2
    below = (_iota((tm, tm), 0) > _iota((tm, tm), 1)).astype(BF16)
    o1 = jnp.where(oh1, 1.0, 0.0)
    o2 = jnp.where(oh2, 1.0, 0.0)
    c1 = jnp.dot(below, o1.astype(BF16), preferred_element_type=F32)
    c2 = jnp.dot(below, o2.astype(BF16), preferred_element_type=F32)
    tot1 = jnp.sum(o1, axis=0, keepdims=True)
    carry = carry_ref[0:1, :]
    rank1 = jnp.sum(jnp.where(oh1, c1 + carry, 0.0), axis=-1, keepdims=True)
    rank2 = jnp.sum(jnp.where(oh2, c2 + carry + tot1, 0.0), axis=-1, keepdims=True)
    carry = carry + tot1 + jnp.sum(o2, axis=0, keepdims=True)
    carry_ref[0:1, :] = carry
    counts_ref[...] = jnp.broadcast_to(carry, counts_ref.shape)

    fields = (i1, i2, w1, w2, rank1, rank2)
    route = jnp.zeros((tm, LANES), F32)
    for n, f in enumerate(fields):
        route = jnp.where(lane == n, f, route)
    route_ref[0] = route

    gate = _sigmoid(jnp.dot(x1b, wpg_ref[...], preferred_element_type=F32) + bpg_ref[...])
    ple = gate * jnp.dot(p_ref[0].astype(BF16), wpp_ref[...], preferred_element_type=F32)
    base_ref[0] = ALPHA * x1 + ple


def _mixer(x, ln_g, ln_b, ya, yb, w_out, l1g, l1b, wr3, br, p, wpg, bpg, wpp):
    B, S, _ = x.shape
    tm = MIX_TM
    const = lambda shape: pl.BlockSpec(shape, lambda b, s: (0,) * len(shape))
    row = lambda w: pl.BlockSpec((1, tm, w), lambda b, s: (b, s, 0))
    return pl.pallas_call(
        _mixer_kernel,
        grid=(B, S // tm),
        in_specs=[
            row(D_MODEL), const((1, D_MODEL)), const((1, D_MODEL)),
            pl.BlockSpec((1, N_PAIRS, tm, LANES), lambda b, s: (b, 0, s, 0)), row(D_GMLP),
            const((D_MODEL, D_MODEL)), const((1, D_MODEL)), const((1, D_MODEL)),
            const((2, D_MODEL, LANES)), const((1, LANES)),
            row(D_PLE), const((D_MODEL, D_MODEL)), const((1, D_MODEL)), const((D_PLE, D_MODEL)),
        ],
        out_specs=[row(D_MODEL), row(D_MODEL // 2), row(LANES), const((8, LANES))],
        out_shape=[jax.ShapeDtypeStruct((B, S, D_MODEL), F32), jax.ShapeDtypeStruct((B, S, D_MODEL // 2), jnp.uint32),
                   jax.ShapeDtypeStruct((B, S, LANES), F32), jax.ShapeDtypeStruct((8, LANES), F32)],
        scratch_shapes=[pltpu.VMEM((8, LANES), F32)],
        compiler_params=pltpu.CompilerParams(dimension_semantics=("arbitrary", "arbitrary"),
                                             vmem_limit_bytes=VMEM_LIMIT),
        name="mixer",
    )(x, ln_g, ln_b, ya, yb, w_out, l1g, l1b, wr3, br, p, wpg, bpg, wpp)


def _slots_kernel(route_ref, counts_ref, dest_ref, pend_ref):
    tm = route_ref.shape[0]
    lane = _iota((tm, LANES), 1)
    route = route_ref[...]
    oh1 = lane == route[:, 0:1].astype(jnp.int32)
    oh2 = lane == route[:, 1:2].astype(jnp.int32)

    counts = counts_ref[0:1, :]
    padded = jnp.floor((counts + (EXPERT_ROWS - 1)) * (1.0 / EXPERT_ROWS)) * EXPERT_ROWS
    upper = (_iota((LANES, LANES), 0) <= _iota((LANES, LANES), 1)).astype(BF16)
    pend = _dot3_lhs(jnp.broadcast_to(padded, (8, LANES)), upper)[0:1, :]
    pstart = pend - padded
    d1 = jnp.sum(jnp.where(oh1, pstart, 0.0), axis=-1, keepdims=True) + route[:, 4:5]
    d2 = jnp.sum(jnp.where(oh2, pstart, 0.0), axis=-1, keepdims=True) + route[:, 5:6]
    dest_ref[...] = jnp.where(lane == 0, d1, jnp.where(lane == 1, d2, 0.0)).astype(jnp.int32)
    pend_ref[...] = jnp.broadcast_to(pend, (8, LANES)).astype(jnp.int32)


def _slots(route, counts):
    T = route.shape[0]
    tm = SLOT_TM
    return pl.pallas_call(
        _slots_kernel,
        grid=(T // tm,),
        in_specs=[pl.BlockSpec((tm, LANES), lambda i: (i, 0)), pl.BlockSpec((8, LANES), lambda i: (0, 0))],
        out_specs=[pl.BlockSpec((tm, LANES), lambda i: (i, 0)),
                   pl.BlockSpec((8, LANES), lambda i: (0, 0))],
        out_shape=[jax.ShapeDtypeStruct((T, LANES), jnp.int32), jax.ShapeDtypeStruct((8, LANES), jnp.int32)],
        compiler_params=pltpu.CompilerParams(dimension_semantics=("arbitrary",), vmem_limit_bytes=VMEM_LIMIT),
        name="slots",
    )(route, counts)


def _dispatch_kernel(pend_ref, dest_ref, x_ref, xs_ref, zero_ref, sem, zsem):
    tm = dest_ref.shape[0] // TOP_K

    @pl.when(pl.program_id(0) == 0)
    def _():
        zero_ref[...] = jnp.zeros_like(zero_ref)

        def tail(e):
            start = pl.multiple_of(jnp.maximum(pend_ref[e] - EXPERT_ROWS, 0), EXPERT_ROWS)
            return pltpu.make_async_copy(zero_ref, xs_ref.at[pl.ds(start, EXPERT_ROWS)], zsem)

        def unused(j):
            return pltpu.make_async_copy(
                zero_ref, xs_ref.at[pl.ds(pl.multiple_of(j * EXPERT_ROWS, EXPERT_ROWS), EXPERT_ROWS)], zsem)

        def start_unused(j, _):
            unused(j).start()
            return 0

        def wait_unused(j, _):
            unused(j).wait()
            return 0

        first_unused = pend_ref[N_EXPERTS - 1] // EXPERT_ROWS
        n_blocks = xs_ref.shape[0] // EXPERT_ROWS
        for e in range(N_EXPERTS):
            tail(e).start()
        lax.fori_loop(first_unused, n_blocks, start_unused, 0)
        for e in range(N_EXPERTS):
            tail(e).wait()
        lax.fori_loop(first_unused, n_blocks, wait_unused, 0)

    for t in range(tm):
        for j in range(TOP_K):
            pltpu.make_async_copy(x_ref.at[pl.ds(t, 1)], xs_ref.at[pl.ds(dest_ref[TOP_K * t + j], 1)],
                                  sem).start(priority=j)
    for j in range(TOP_K):
        pltpu.make_async_copy(x_ref, xs_ref.at[pl.ds(0, tm)], sem).wait()


def _dispatch(pend, dest_flat, x1, n_rows):
    T, width = x1.shape
    tm = DISPATCH_TM
    return pl.pallas_call(
        _dispatch_kernel,
        grid_spec=pltpu.PrefetchScalarGridSpec(
            num_scalar_prefetch=1,
            grid=(T // tm,),
            in_specs=[pl.BlockSpec((TOP_K * tm,), lambda i, pe: (i,), memory_space=pltpu.SMEM),
                      pl.BlockSpec((tm, width), lambda i, pe: (i, 0))],
            out_specs=pl.BlockSpec(memory_space=pl.ANY),
            scratch_shapes=[pltpu.VMEM((EXPERT_ROWS, width), x1.dtype), pltpu.SemaphoreType.DMA,
                            pltpu.SemaphoreType.DMA],
        ),
        out_shape=jax.ShapeDtypeStruct((n_rows, width), x1.dtype),
        compiler_params=pltpu.CompilerParams(dimension_semantics=("arbitrary",), vmem_limit_bytes=VMEM_LIMIT),
        name="dispatch",
    )(pend, dest_flat, x1)


def _experts_kernel(pend_ref, xs_ref, wg_ref, wu_ref, wd_ref, ys_ref, xbuf_ref, ybuf_ref, wgu_ref, wdb_ref,
                    in_sem, out_sem):
    rows = EXPERT_ROWS
    e = pl.program_id(0)
    first = jnp.where(e == 0, 0, pend_ref[jnp.maximum(e - 1, 0)]) // rows
    last = pend_ref[e] // rows
    n_used = pend_ref[N_EXPERTS - 1] // rows

    def block_rows(ref, b):
        return ref.at[pl.ds(pl.multiple_of(b * rows, rows), rows)]

    def x_copy(b, slot):
        return pltpu.make_async_copy(block_rows(xs_ref, b), xbuf_ref.at[slot], in_sem.at[slot])

    def y_copy(b, slot):
        return pltpu.make_async_copy(ybuf_ref.at[slot], block_rows(ys_ref, b), out_sem.at[slot])

    @pl.when((e == 0) & (n_used > 0))
    def _():
        x_copy(0, 0).start()

    @pl.when(last > first)
    def _():
        wgu_ref[:, :D_EXPERT] = wg_ref[0].astype(BF16)
        wgu_ref[:, D_EXPERT:] = wu_ref[0].astype(BF16)
        wdb_ref[...] = wd_ref[0].astype(BF16)

        def body(b, _):
            slot = b % 2

            @pl.when(b + 1 < n_used)
            def _():
                x_copy(b + 1, 1 - slot).start()

            x_copy(b, slot).wait()

            @pl.when(b >= 2)
            def _():
                y_copy(b - 2, slot).wait()

            xw = xbuf_ref[slot]
            x_lo = lax.bitcast_convert_type(xw << 16, F32)
            x_hi = lax.bitcast_convert_type(xw & jnp.uint32(0xFFFF0000), F32)
            xb = jnp.concatenate([x_lo, x_hi], axis=1).astype(BF16)
            h = jnp.dot(xb, wgu_ref[...], preferred_element_type=F32)
            hg = h[:, :D_EXPERT]
            hid = hg * _sigmoid(hg) * h[:, D_EXPERT:]
            ybuf_ref[slot] = jnp.dot(hid.astype(BF16), wdb_ref[...], preferred_element_type=F32)
            y_copy(b, slot).start()
            return 0

        lax.fori_loop(first, last, body, 0)

    @pl.when(e == N_EXPERTS - 1)
    def _():
        @pl.when(n_used >= 2)
        def _():
            y_copy(n_used - 2, n_used % 2).wait()

        @pl.when(n_used >= 1)
        def _():
            y_copy(n_used - 1, (n_used - 1) % 2).wait()

        ybuf_ref[0] = jnp.zeros(ybuf_ref.shape[1:], F32)
        first_unused = n_used
        n_tail = ys_ref.shape[0] // rows

        def start_unused(b, _):
            y_copy(b, 0).start()
            return 0

        def wait_unused(b, _):
            y_copy(b, 0).wait()
            return 0

        lax.fori_loop(first_unused, n_tail, start_unused, 0)
        lax.fori_loop(first_unused, n_tail, wait_unused, 0)


def _experts(pend, xs, wg, wu, wd):
    n_rows = xs.shape[0]
    rows = EXPERT_ROWS
    wspec = lambda shape: pl.BlockSpec((1,) + shape, lambda e, pe: (e, 0, 0))
    return pl.pallas_call(
        _experts_kernel,
        grid_spec=pltpu.PrefetchScalarGridSpec(
            num_scalar_prefetch=1,
            grid=(N_EXPERTS,),
            in_specs=[pl.BlockSpec(memory_space=pl.ANY),
                      wspec((D_MODEL, D_EXPERT)), wspec((D_MODEL, D_EXPERT)), wspec((D_EXPERT, D_MODEL))],
            out_specs=pl.BlockSpec(memory_space=pl.ANY),
            scratch_shapes=[pltpu.VMEM((2, rows, D_MODEL // 2), jnp.uint32), pltpu.VMEM((2, rows, D_MODEL), F32),
                            pltpu.VMEM((D_MODEL, 2 * D_EXPERT), BF16), pltpu.VMEM((D_EXPERT, D_MODEL), BF16),
                            pltpu.SemaphoreType.DMA((2,)), pltpu.SemaphoreType.DMA((2,))],
        ),
        out_shape=jax.ShapeDtypeStruct((n_rows, D_MODEL), F32),
        compiler_params=pltpu.CompilerParams(dimension_semantics=("arbitrary",), vmem_limit_bytes=VMEM_LIMIT),
        name="experts",
    )(pend, xs, wg, wu, wd)


def _combine_kernel(dest_ref, dest_next_ref, ys_ref, base_ref, route_ref, lg_ref, lb_ref, o_ref, buf_ref, sem):
    tm = buf_ref.shape[2]
    i = pl.program_id(0)

    def gather(dref, offset, s):
        for t in range(tm):
            for j in range(TOP_K):
                pltpu.make_async_copy(ys_ref.at[pl.ds(dref[offset + TOP_K * t + j], 1)],
                                      buf_ref.at[s, j, pl.ds(t, 1)], sem.at[s]).start(priority=j)

    def drain(s):
        for j in range(TOP_K):
            pltpu.make_async_copy(ys_ref.at[pl.ds(0, tm)], buf_ref.at[s, j], sem.at[s]).wait()

    def finish(s):
        rows = slice(s * tm, (s + 1) * tm)
        drain(s)
        route = route_ref[rows, :]
        ffn = buf_ref[s, 0] * route[:, 2:3] + buf_ref[s, 1] * route[:, 3:4]
        o_ref[rows, :] = _layer_norm(base_ref[rows, :] + ffn, lg_ref[...], lb_ref[...], LN_EPS)

    @pl.when(i == 0)
    def _():
        gather(dest_ref, 0, 0)

    gather(dest_ref, TOP_K * tm, 1)
    finish(0)
    gather(dest_next_ref, 0, 0)
    finish(1)

    @pl.when(i == pl.num_programs(0) - 1)
    def _():
        drain(0)


def _combine(dest_flat, ys, base, route, l2g, l2b):
    T = base.shape[0]
    tm = COMBINE_TM
    nt = T // tm
    return pl.pallas_call(
        _combine_kernel,
        grid=(nt // 2,),
        in_specs=[pl.BlockSpec((2 * TOP_K * tm,), lambda i: (i,), memory_space=pltpu.SMEM),
                  pl.BlockSpec((TOP_K * tm,), lambda i: (jnp.minimum(2 * i + 2, nt - 1),), memory_space=pltpu.SMEM),
                  pl.BlockSpec(memory_space=pl.ANY),
                  pl.BlockSpec((2 * tm, D_MODEL), lambda i: (i, 0)), pl.BlockSpec((2 * tm, LANES), lambda i: (i, 0)),
                  pl.BlockSpec((1, D_MODEL), lambda i: (0, 0)), pl.BlockSpec((1, D_MODEL), lambda i: (0, 0))],
        out_specs=pl.BlockSpec((2 * tm, D_MODEL), lambda i: (i, 0)),
        out_shape=jax.ShapeDtypeStruct((T, D_MODEL), F32),
        scratch_shapes=[pltpu.VMEM((2, TOP_K, tm, D_MODEL), F32), pltpu.SemaphoreType.DMA((2,))],
        compiler_params=pltpu.CompilerParams(dimension_semantics=("arbitrary",), vmem_limit_bytes=VMEM_LIMIT),
        name="combine",
    )(dest_flat, dest_flat, ys, base, route, l2g, l2b)


def _block_diag_const(n, blk, val):
    idx = jnp.arange(n) // blk
    return jnp.where(idx[:, None] == idx[None, :], val, 0.0).astype(BF16)


def kernel(x, p, ln_emb_g, ln_emb_b, w_in, mu_shift, w0, w_decay_up, a0, w_iclr_up, w_gate_up, k_k, k_a, r_k, gn_g, gn_b, gmlp_ln_g, gmlp_ln_b, w_spatial, b_spatial, w_out, ln1_g, ln1_b, w_group_router, b_group_router, w_expert_router, b_expert_router, w_exp_gate, w_exp_up, w_exp_down, w_ple_gate, b_ple_gate, w_ple_proj, ln2_g, ln2_b):
    B, S, D = x.shape
    T = B * S
    row = lambda t: t.reshape(1, -1).astype(F32)

    zl = jnp.zeros((DECAY_LORA, D_RWKV), F32)
    wwa = jnp.concatenate([jnp.concatenate([w_decay_up[0], zl], axis=1),
                           jnp.concatenate([zl, w_iclr_up[0]], axis=1)], axis=0).astype(BF16)
    w0a0 = jnp.concatenate([w0[0], a0[0]]).reshape(1, -1)
    eones = jnp.tile(_block_diag_const(2 * LANES, HEAD, 1.0), (2, 1))
    emean = jnp.tile(_block_diag_const(LANES, HEAD, 1.0 / HEAD), (2, 1))

    r, lw, k, v, a, b, g, bonus, yb = _prep(
        x, row(ln_emb_g), row(ln_emb_b), w_in[0].astype(BF16), row(mu_shift[0]), wwa, w0a0,
        w_gate_up[0].astype(BF16), row(k_k[0]), row(k_a[0]), row(r_k[0]), eones,
        row(gmlp_ln_g[0]), row(gmlp_ln_b[0]), w_spatial[0], b_spatial[0].T)

    ya = _wkv(r, lw, k, v, a, b, g, bonus, gn_g[0].reshape(N_PAIRS, 1, LANES), gn_b[0].reshape(N_PAIRS, 1, LANES),
              emean)

    wr = jnp.concatenate([w_expert_router[0].reshape(D, N_EXPERTS), w_group_router[0],
                          jnp.zeros((D, LANES - N_EXPERTS - N_GROUPS), F32)], axis=1)
    wr3 = jnp.stack(_split2(wr))
    br = jnp.concatenate([b_expert_router[0].reshape(-1), b_group_router[0],
                          jnp.zeros((LANES - N_EXPERTS - N_GROUPS,), F32)]).reshape(1, LANES)
    base, x1, route, counts = _mixer(x, row(ln_emb_g), row(ln_emb_b), ya, yb, w_out[0].astype(BF16), row(ln1_g[0]),
                                     row(ln1_b[0]), wr3, br, p[0], w_ple_gate[0].astype(BF16),
                                     row(b_ple_gate[0]), w_ple_proj[0].astype(BF16))
    base = base.reshape(T, D)
    x1 = x1.reshape(T, D // 2)
    route = route.reshape(T, LANES)

    n_blocks = -(-(T * TOP_K) // EXPERT_ROWS) + N_EXPERTS
    dest, pend = _slots(route, counts)
    dest_flat = dest[:, :TOP_K].reshape(T * TOP_K)
    pend = pend[0, :N_EXPERTS]

    xs = _dispatch(pend, dest_flat, x1, n_blocks * EXPERT_ROWS)
    ys = _experts(pend, xs, w_exp_gate[0], w_exp_up[0], w_exp_down[0])
    out = _combine(dest_flat, ys, base, route, row(ln2_g[0]), row(ln2_b[0]))
    return out.reshape(B, S, D)
```

```python
import functools
import math

import jax
import jax.numpy as jnp
from jax import lax
from jax.experimental import pallas as pl
from jax.experimental.pallas import tpu as pltpu

F32 = jnp.float32
BF16 = jnp.bfloat16

D_MODEL = 1024
D_RWKV = 512
HEAD = 64
D_GMLP = 512
GMLP_GROUPS = 4
GROUP_W = 128
GCHUNK = 128
DECAY_LORA = 64
ICLR_LORA = 64
GATE_LORA = 128
N_SHIFT = 3 * D_RWKV + DECAY_LORA + ICLR_LORA + GATE_LORA
D_IN = N_SHIFT + 2 * D_GMLP
D_PLE = 256
N_GROUPS = 4
EXPERTS_PER_GROUP = 8
N_EXPERTS = 32
TOP_K = 2
D_EXPERT = 512
DEPTH = 1
ALPHA = (2.0 * DEPTH) ** 0.25
LN_EPS = 1e-5
GN_EPS = 64e-5
DECAY_SCALE = math.exp(-0.5)

LANES = 128
WKV_CHUNK = 64
N_PAIRS = D_RWKV // LANES
VMEM_LIMIT = 56 * 1024 * 1024

PREP_TM = 512
WKV_TB = 256
WKV_PAIRS = 4
MIX_TM = 512
SLOT_TM = 2048
EXPERT_ROWS = 256
DISPATCH_TM = 512
COMBINE_TM = 256
NEG = -1e30


def _dot(a, b):
    return jnp.dot(a.astype(BF16), b.astype(BF16), preferred_element_type=F32)


def _dot_nt(a, b):
    return lax.dot_general(a.astype(BF16), b.astype(BF16), (((1,), (1,)), ((), ())),
                           preferred_element_type=F32)


def _split3(x):
    hi = x.astype(BF16)
    r1 = x - hi.astype(F32)
    mid = r1.astype(BF16)
    lo = (r1 - mid.astype(F32)).astype(BF16)
    return hi, mid, lo


def _dot3_lhs(x, w):
    hi, mid, lo = _split3(x)
    w = w.astype(BF16)
    return (jnp.dot(hi, w, preferred_element_type=F32) + jnp.dot(mid, w, preferred_element_type=F32)
            + jnp.dot(lo, w, preferred_element_type=F32))


def _split2(x):
    hi = x.astype(BF16)
    return hi, (x - hi.astype(F32)).astype(BF16)


def _dot2_lhs(x, w2):
    hi, lo = _split2(x)
    return jnp.dot(jnp.concatenate([hi, lo], axis=1), w2, preferred_element_type=F32)


def _dot3_rhs(w3, x):
    hi, mid, lo = _split3(x)
    return jnp.dot(w3, jnp.concatenate([hi, mid, lo], axis=0), preferred_element_type=F32)


def _layer_norm(x, g, b, eps):
    mu = jnp.mean(x, axis=-1, keepdims=True)
    xc = x - mu
    var = jnp.mean(xc * xc, axis=-1, keepdims=True)
    return xc * lax.rsqrt(var + eps) * g + b


def _sigmoid(x):
    return 1.0 / (1.0 + jnp.exp(-x))


def _iota(shape, dim):
    return lax.broadcasted_iota(jnp.int32, shape, dim)


def _prep_kernel(x_ref, lng_ref, lnb_ref, win_ref, mu_ref, wwa_ref, w0a0_ref, wg_ref, kk_ref, ka_ref, rk_ref,
                 eones_ref, glng_ref, glnb_ref, wsp_ref, bsp_ref,
                 r_ref, lw_ref, k_ref, v_ref, a_ref, b_ref, g_ref, bonus_ref, yb_ref, carry_ref):
    tm = x_ref.shape[1]

    @pl.when(pl.program_id(1) == 0)
    def _():
        carry_ref[...] = jnp.zeros_like(carry_ref)

    x0 = _layer_norm(x_ref[0], lng_ref[...], lnb_ref[...], LN_EPS)
    proj = jnp.dot(x0.astype(BF16), win_ref[...], preferred_element_type=F32)

    h = proj[:, :N_SHIFT]
    rolled = pltpu.roll(h, 1, 0)
    first = _iota((tm, N_SHIFT), 0) == 0
    prev = jnp.where(first, jnp.broadcast_to(carry_ref[0:1, :], (tm, N_SHIFT)), rolled)
    carry_ref[0:1, :] = h[tm - 1:tm, :]
    h = h + (prev - h) * mu_ref[...]

    r = h[:, 0:D_RWKV]
    k = h[:, D_RWKV:2 * D_RWKV]
    v = h[:, 2 * D_RWKV:3 * D_RWKV]
    xwa = h[:, 3 * D_RWKV:3 * D_RWKV + LANES]
    xg = h[:, 3 * D_RWKV + LANES:N_SHIFT]

    lane = _iota((tm, LANES), 1)
    twa = jnp.where(lane < DECAY_LORA, jnp.tanh(xwa), xwa)
    da = _dot(twa, wwa_ref[...]) + w0a0_ref[...]
    logw = -DECAY_SCALE * _sigmoid(da[:, :D_RWKV])
    ag = _sigmoid(da[:, D_RWKV:])
    g = _dot(_sigmoid(xg), wg_ref[...])

    eones2 = eones_ref[...]

    def head_sum(t):
        half = 2 * LANES
        return jnp.concatenate([_dot2_lhs(t[:, :half], eones2), _dot2_lhs(t[:, half:], eones2)], axis=1)

    kk = k * kk_ref[...]
    kk = kk / jnp.maximum(jnp.sqrt(head_sum(kk * kk)), 1e-12)
    k = k * (1.0 + (ag - 1.0) * ka_ref[...])
    bonus = head_sum(r * k * rk_ref[...]) * v

    for p in range(N_PAIRS):
        sl = slice(p * LANES, (p + 1) * LANES)
        r_ref[0, p] = r[:, sl]
        lw_ref[0, p] = logw[:, sl]
        k_ref[0, p] = k[:, sl]
        v_ref[0, p] = v[:, sl]
        a_ref[0, p] = -kk[:, sl]
        b_ref[0, p] = (kk * ag)[:, sl]
        g_ref[0, p] = g[:, sl]
        bonus_ref[0, p] = bonus[:, sl]

    zin = proj[:, N_SHIFT:]
    z = 0.5 * zin * (1.0 + lax.erf(zin * (0.5 ** 0.5)))
    zu = z[:, :D_GMLP]
    zv = z[:, D_GMLP:]
    causal = _iota((GCHUNK, GCHUNK), 0) >= _iota((GCHUNK, GCHUNK), 1)
    for gi in range(GMLP_GROUPS):
        gs = slice(gi * GROUP_W, (gi + 1) * GROUP_W)
        zvn = _layer_norm(zv[:, gs], glng_ref[:, gs], glnb_ref[:, gs], LN_EPS)
        ws = jnp.where(causal, wsp_ref[gi], 0.0).astype(BF16)
        bcol = bsp_ref[:, gi:gi + 1]
        for c in range(tm // GCHUNK):
            ts = slice(c * GCHUNK, (c + 1) * GCHUNK)
            mixed = jnp.dot(ws, zvn[ts].astype(BF16), preferred_element_type=F32) + bcol
            yb_ref[0, ts, gs] = (zu[ts, gs] * mixed).astype(BF16)


def _prep(x, ln_g, ln_b, w_in, mu, wwa, w0a0, wg, k_k, k_a, r_k, eones, glng, glnb, wsp, bsp):
    B, S, _ = x.shape
    tm = PREP_TM
    const = lambda shape: pl.BlockSpec(shape, lambda b, s: (0,) * len(shape))
    pair_spec = pl.BlockSpec((1, N_PAIRS, tm, LANES), lambda b, s: (b, 0, s, 0))
    pair_shape = jax.ShapeDtypeStruct((B, N_PAIRS, S, LANES), F32)
    return pl.pallas_call(
        _prep_kernel,
        grid=(B, S // tm),
        in_specs=[
            pl.BlockSpec((1, tm, D_MODEL), lambda b, s: (b, s, 0)),
            const((1, D_MODEL)), const((1, D_MODEL)), const((D_MODEL, D_IN)), const((1, N_SHIFT)),
            const((LANES, 2 * D_RWKV)), const((1, 2 * D_RWKV)), const((GATE_LORA, D_RWKV)),
            const((1, D_RWKV)), const((1, D_RWKV)), const((1, D_RWKV)), const((4 * LANES, 2 * LANES)),
            const((1, D_GMLP)), const((1, D_GMLP)), const((GMLP_GROUPS, GCHUNK, GCHUNK)),
            const((GCHUNK, GMLP_GROUPS)),
        ],
        out_specs=[pair_spec] * 8 + [pl.BlockSpec((1, tm, D_GMLP), lambda b, s: (b, s, 0))],
        out_shape=[pair_shape] * 8 + [jax.ShapeDtypeStruct((B, S, D_GMLP), BF16)],
        scratch_shapes=[pltpu.VMEM((8, N_SHIFT), F32)],
        compiler_params=pltpu.CompilerParams(dimension_semantics=("arbitrary", "arbitrary"),
                                             vmem_limit_bytes=VMEM_LIMIT),
        name="prep",
    )(x, ln_g, ln_b, w_in, mu, wwa, w0a0, wg, k_k, k_a, r_k, eones, glng, glnb, wsp, bsp)


def _wkv_kernel(r_ref, lw_ref, k_ref, v_ref, a_ref, b_ref, g_ref, bonus_ref, gng_ref, gnb_ref, emean_ref,
                o_ref, h_ref):
    C = WKV_CHUNK
    tb = r_ref.shape[2]

    @pl.when(pl.program_id(2) == 0)
    def _():
        h_ref[...] = jnp.zeros_like(h_ref)

    tok = _iota((C, LANES), 0)
    lane = _iota((C, LANES), 1)
    head0 = lane < HEAD
    strict = tok > lane % HEAD
    incl = tok >= lane % HEAD
    eye_w = (tok == lane % HEAD).astype(F32)
    rr = _iota((LANES, LANES), 0)
    cc = _iota((LANES, LANES), 1)
    eye = (rr == cc).astype(F32)
    same_head = (rr < HEAD) == (cc < HEAD)
    ltri3 = (_iota((C, 3 * C), 0) >= _iota((C, 3 * C), 1) % C).astype(BF16)

    def stack(x):
        xb = x.astype(BF16)
        zero = jnp.zeros_like(xb)
        return jnp.concatenate([jnp.where(head0, xb, zero), jnp.where(head0, zero, xb)], axis=0)

    def stack2(x, y):
        return jnp.concatenate([stack(x), stack(y)], axis=1)

    n_pairs = r_ref.shape[1]
    n_chunks = tb // C
    units = [(q, c) for q in range(n_pairs) for c in range(n_chunks)]

    def load(ref):
        return [ref[0, q, c * C:(c + 1) * C, :] for q, c in units]

    r_, lw_, k_, v_, a_, b_ = (load(ref) for ref in (r_ref, lw_ref, k_ref, v_ref, a_ref, b_ref))
    cum_ = [_dot3_rhs(ltri3, lw) for lw in lw_]
    cend_ = [cum[C - 1:C, :] for cum in cum_]
    at_ = [a * jnp.exp(cum - lw) for a, cum, lw in zip(a_, cum_, lw_)]
    rt_ = [r * jnp.exp(cum) for r, cum in zip(r_, cum_)]
    ginv_ = [jnp.exp(-cum) for cum in cum_]
    gend_ = [jnp.exp(cend - cum) for cend, cum in zip(cend_, cum_)]
    bk_end_ = [jnp.concatenate([b * ge, k * ge], axis=0) for b, k, ge in zip(b_, k_, gend_)]
    vst_ = [stack(v) for v in v_]

    G_ = [_dot_nt(jnp.concatenate([at, rt], axis=0), jnp.concatenate([stack(b * gi), stack(k * gi)], axis=0))
          for at, rt, b, k, gi in zip(at_, rt_, b_, k_, ginv_)]
    n1_ = [jnp.where(strict, G[:C, :LANES], 0.0) for G in G_]
    aak_ = [jnp.where(strict, G[:C, LANES:], 0.0) for G in G_]
    arb_ = [jnp.where(incl, G[C:, :LANES], 0.0) for G in G_]
    ark_ = [jnp.where(incl, G[C:, LANES:], 0.0) for G in G_]
    av_ = [_dot(jnp.concatenate([aak, ark], axis=0), vst) for aak, ark, vst in zip(aak_, ark_, vst_)]

    s1_ = [stack(n1) for n1 in n1_]
    n2_ = [_dot(n1, s1) for n1, s1 in zip(n1_, s1_)]
    x_ = [_dot(n2, jnp.concatenate([s1, stack(n2)], axis=1)) for n2, s1 in zip(n2_, s1_)]
    t_ = [eye_w + n1 + n2 + x[:, :LANES] for n1, n2, x in zip(n1_, n2_, x_)]
    np_ = [x[:, LANES:] for x in x_]
    for _ in range(3):
        x_ = [_dot(npow, stack2(t, npow)) for t, npow in zip(t_, np_)]
        t_ = [t + x[:, :LANES] for t, x in zip(t_, x_)]
        np_ = [x[:, LANES:] for x in x_]
    t_ = [t + _dot(npow, stack(t)) for t, npow in zip(t_, np_)]

    x_ = [_dot(t, stack2(at, av[:C])) for t, at, av in zip(t_, at_, av_)]
    z_ = [_dot(arb, stack2(x[:, :LANES], x[:, LANES:])) for arb, x in zip(arb_, x_)]
    rp_ = [rt + z[:, :LANES] for rt, z in zip(rt_, z_)]
    p3_ = [z[:, LANES:] + av[C:] for z, av in zip(z_, av_)]
    rhs_ = [jnp.concatenate([x, jnp.concatenate([jnp.zeros_like(v), v], axis=1)], axis=0)
            for x, v in zip(x_, v_)]
    mq_ = [_dot(bk_end.T, rhs) for bk_end, rhs in zip(bk_end_, rhs_)]
    m_ = [eye * jnp.exp(cend) + jnp.where(same_head, mq[:, :LANES], 0.0) for cend, mq in zip(cend_, mq_)]
    q_ = [jnp.where(same_head, mq[:, LANES:], 0.0) for mq in mq_]

    pairs = range(n_pairs)
    H_ = [h_ref[q] for q in pairs]
    ys_ = [[] for _ in pairs]
    for c in range(n_chunks):
        for q in pairs:
            u = q * n_chunks + c
            ys_[q].append(_dot(rp_[u], H_[q]) + p3_[u])
        H_ = [_dot(m_[q * n_chunks + c], H_[q]) + q_[q * n_chunks + c] for q in pairs]
    for q in pairs:
        h_ref[q] = H_[q]

    emean = emean_ref[...]
    y_ = [jnp.concatenate(ys, axis=0) for ys in ys_]
    mu_ = [_dot2_lhs(y, emean) for y in y_]
    yc_ = [y - mu for y, mu in zip(y_, mu_)]
    var_ = [_dot2_lhs(yc * yc, emean) for yc in yc_]
    for q in pairs:
        yn = yc_[q] * lax.rsqrt(var_[q] + GN_EPS) * gng_ref[q] + gnb_ref[q]
        o_ref[0, q] = ((yn + bonus_ref[0, q]) * g_ref[0, q]).astype(BF16)


def _wkv(r, lw, k, v, a, b, g, bonus, gn_g, gn_b, emean):
    B, P, S, _ = r.shape
    tb = WKV_TB
    pp = WKV_PAIRS
    seq = pl.BlockSpec((1, pp, tb, LANES), lambda bi, p, s: (bi, p, s, 0))
    par = pl.BlockSpec((pp, 1, LANES), lambda bi, p, s: (p, 0, 0))
    return pl.pallas_call(
        _wkv_kernel,
        grid=(B, P // pp, S // tb),
        in_specs=[seq] * 8 + [par, par, pl.BlockSpec((2 * LANES, LANES), lambda bi, p, s: (0, 0))],
        out_specs=seq,
        out_shape=jax.ShapeDtypeStruct((B, P, S, LANES), BF16),
        scratch_shapes=[pltpu.VMEM((pp, LANES, LANES), F32)],
        compiler_params=pltpu.CompilerParams(dimension_semantics=("arbitrary", "arbitrary", "arbitrary"),
                                             vmem_limit_bytes=VMEM_LIMIT),
        name="wkv",
    )(r, lw, k, v, a, b, g, bonus, gn_g, gn_b, emean)


def _mixer_kernel(x_ref, lng_ref, lnb_ref, ya_ref, yb_ref, wout_ref, l1g_ref, l1b_ref, wr_ref, br_ref,
                  p_ref, wpg_ref, bpg_ref, wpp_ref, base_ref, x1_ref, route_ref, counts_ref, carry_ref):
    tm = x_ref.shape[1]

    @pl.when((pl.program_id(0) == 0) & (pl.program_id(1) == 0))
    def _():
        carry_ref[...] = jnp.zeros_like(carry_ref)

    x0 = _layer_norm(x_ref[0], lng_ref[...], lnb_ref[...], LN_EPS)
    ymix = jnp.concatenate([ya_ref[0, p] for p in range(N_PAIRS)] + [yb_ref[0]], axis=-1)
    mix = jnp.dot(ymix, wout_ref[...], preferred_element_type=F32)
    x1 = _layer_norm(ALPHA * x0 + mix, l1g_ref[...], l1b_ref[...], LN_EPS)
    x1b = x1.astype(BF16)
    half = D_MODEL // 2
    lo_bits = lax.bitcast_convert_type(x1b[:, :half].astype(F32), jnp.uint32)
    hi_bits = lax.bitcast_convert_type(x1b[:, half:].astype(F32), jnp.uint32)
    x1_ref[0] = (hi_bits & jnp.uint32(0xFFFF0000)) | (lo_bits >> 16)

    hi, mid = _split2(x1)
    whi = wr_ref[0]
    wmid = wr_ref[1]
    d = lambda u, w: jnp.dot(u, w, preferred_element_type=F32)
    logits = (d(hi, whi) + d(hi, wmid) + d(mid, whi)) + br_ref[...]
    lane = _iota((tm, LANES), 1).astype(F32)
    far = float(4 * LANES)
    is_g = jnp.where(lane >= N_EXPERTS, jnp.where(lane < N_EXPERTS + N_GROUPS, 1.0, 0.0), 0.0) > 0.5
    gl = jnp.where(is_g, logits, NEG)
    gmax = jnp.max(gl, axis=-1, keepdims=True)
    gsel = jnp.min(jnp.where(gl == gmax, lane, far), axis=-1, keepdims=True) - N_EXPERTS
    p_group = 1.0 / jnp.sum(jnp.where(is_g, jnp.exp(gl - gmax), 0.0), axis=-1, keepdims=True)
    grp_of_lane = jnp.floor(lane * (1.0 / EXPERTS_PER_GROUP))
    el = jnp.where(grp_of_lane == gsel, logits, NEG)
    v1 = jnp.max(el, axis=-1, keepdims=True)
    i1 = jnp.min(jnp.where(el == v1, lane, far), axis=-1, keepdims=True)
    el2 = jnp.where(lane == i1, NEG, el)
    v2 = jnp.max(el2, axis=-1, keepdims=True)
    i2 = jnp.min(jnp.where(el2 == v2, lane, far), axis=-1, keepdims=True)
    e21 = jnp.exp(v2 - v1)
    w1 = p_group / (1.0 + e21)
    w2 = p_group * e21 / (1.0 + e21)

    oh1 = lane == i1
    oh2 = lane == i2
    below = (_iota((tm, tm), 0) > _iota((tm, tm), 1)).astype(BF16)
    o1 = jnp.where(oh1, 1.0, 0.0)
    o2 = jnp.where(oh2, 1.0, 0.0)
    c1 = jnp.dot(below, o1.astype(BF16), preferred_element_type=F32)
    c2 = jnp.dot(below, o2.astype(BF16), preferred_element_type=F32)
    tot1 = jnp.sum(o1, axis=0, keepdims=True)
    carry = carry_ref[0:1, :]
    rank1 = jnp.sum(jnp.where(oh1, c1 + carry, 0.0), axis=-1, keepdims=True)
    rank2 = jnp.sum(jnp.where(oh2, c2 + carry + tot1, 0.0), axis=-1, keepdims=True)
    carry = carry + tot1 + jnp.sum(o2, axis=0, keepdims=True)
    carry_ref[0:1, :] = carry
    counts_ref[...] = jnp.broadcast_to(carry, counts_ref.shape)

    fields = (i1, i2, w1, w2, rank1, rank2)
    route = jnp.zeros((tm, LANES), F32)
    for n, f in enumerate(fields):
        route = jnp.where(lane == n, f, route)
    route_ref[0] = route

    gate = _sigmoid(jnp.dot(x1b, wpg_ref[...], preferred_element_type=F32) + bpg_ref[...])
    ple = gate * jnp.dot(p_ref[0].astype(BF16), wpp_ref[...], preferred_element_type=F32)
    base_ref[0] = ALPHA * x1 + ple


def _mixer(x, ln_g, ln_b, ya, yb, w_out, l1g, l1b, wr3, br, p, wpg, bpg, wpp):
    B, S, _ = x.shape
    tm = MIX_TM
    const = lambda shape: pl.BlockSpec(shape, lambda b, s: (0,) * len(shape))
    row = lambda w: pl.BlockSpec((1, tm, w), lambda b, s: (b, s, 0))
    return pl.pallas_call(
        _mixer_kernel,
        grid=(B, S // tm),
        in_specs=[
            row(D_MODEL), const((1, D_MODEL)), const((1, D_MODEL)),
            pl.BlockSpec((1, N_PAIRS, tm, LANES), lambda b, s: (b, 0, s, 0)), row(D_GMLP),
            const((D_MODEL, D_MODEL)), const((1, D_MODEL)), const((1, D_MODEL)),
            const((2, D_MODEL, LANES)), const((1, LANES)),
            row(D_PLE), const((D_MODEL, D_MODEL)), const((1, D_MODEL)), const((D_PLE, D_MODEL)),
        ],
        out_specs=[row(D_MODEL), row(D_MODEL // 2), row(LANES), const((8, LANES))],
        out_shape=[jax.ShapeDtypeStruct((B, S, D_MODEL), F32), jax.ShapeDtypeStruct((B, S, D_MODEL // 2), jnp.uint32),
                   jax.ShapeDtypeStruct((B, S, LANES), F32), jax.ShapeDtypeStruct((8, LANES), F32)],
        scratch_shapes=[pltpu.VMEM((8, LANES), F32)],
        compiler_params=pltpu.CompilerParams(dimension_semantics=("arbitrary", "arbitrary"),
                                             vmem_limit_bytes=VMEM_LIMIT),
        name="mixer",
    )(x, ln_g, ln_b, ya, yb, w_out, l1g, l1b, wr3, br, p, wpg, bpg, wpp)


def _slots_kernel(route_ref, counts_ref, dest_ref, pend_ref):
    tm = route_ref.shape[0]
    lane = _iota((tm, LANES), 1)
    route = route_ref[...]
    oh1 = lane == route[:, 0:1].astype(jnp.int32)
    oh2 = lane == route[:, 1:2].astype(jnp.int32)

    counts = counts_ref[0:1, :]
    padded = jnp.floor((counts + (EXPERT_ROWS - 1)) * (1.0 / EXPERT_ROWS)) * EXPERT_ROWS
    upper = (_iota((LANES, LANES), 0) <= _iota((LANES, LANES), 1)).astype(BF16)
    pend = _dot3_lhs(jnp.broadcast_to(padded, (8, LANES)), upper)[0:1, :]
    pstart = pend - padded
    d1 = jnp.sum(jnp.where(oh1, pstart, 0.0), axis=-1, keepdims=True) + route[:, 4:5]
    d2 = jnp.sum(jnp.where(oh2, pstart, 0.0), axis=-1, keepdims=True) + route[:, 5:6]
    dest_ref[...] = jnp.where(lane == 0, d1, jnp.where(lane == 1, d2, 0.0)).astype(jnp.int32)
    pend_ref[...] = jnp.broadcast_to(pend, (8, LANES)).astype(jnp.int32)


def _slots(route, counts):
    T = route.shape[0]
    tm = SLOT_TM
    return pl.pallas_call(
        _slots_kernel,
        grid=(T // tm,),
        in_specs=[pl.BlockSpec((tm, LANES), lambda i: (i, 0)), pl.BlockSpec((8, LANES), lambda i: (0, 0))],
        out_specs=[pl.BlockSpec((tm, LANES), lambda i: (i, 0)),
                   pl.BlockSpec((8, LANES), lambda i: (0, 0))],
        out_shape=[jax.ShapeDtypeStruct((T, LANES), jnp.int32), jax.ShapeDtypeStruct((8, LANES), jnp.int32)],
        compiler_params=pltpu.CompilerParams(dimension_semantics=("arbitrary",), vmem_limit_bytes=VMEM_LIMIT),
        name="slots",
    )(route, counts)


def _dispatch_kernel(pend_ref, dest_ref, x_ref, xs_ref, zero_ref, sem, zsem):
    tm = dest_ref.shape[0] // TOP_K

    @pl.when(pl.program_id(0) == 0)
    def _():
        zero_ref[...] = jnp.zeros_like(zero_ref)

        def tail(e):
            start = pl.multiple_of(jnp.maximum(pend_ref[e] - EXPERT_ROWS, 0), EXPERT_ROWS)
            return pltpu.make_async_copy(zero_ref, xs_ref.at[pl.ds(start, EXPERT_ROWS)], zsem)

        def unused(j):
            return pltpu.make_async_copy(
                zero_ref, xs_ref.at[pl.ds(pl.multiple_of(j * EXPERT_ROWS, EXPERT_ROWS), EXPERT_ROWS)], zsem)

        def start_unused(j, _):
            unused(j).start()
            return 0

        def wait_unused(j, _):
            unused(j).wait()
            return 0

        first_unused = pend_ref[N_EXPERTS - 1] // EXPERT_ROWS
        n_blocks = xs_ref.shape[0] // EXPERT_ROWS
        for e in range(N_EXPERTS):
            tail(e).start()
        lax.fori_loop(first_unused, n_blocks, start_unused, 0)
        for e in range(N_EXPERTS):
            tail(e).wait()
        lax.fori_loop(first_unused, n_blocks, wait_unused, 0)

    for t in range(tm):
        for j in range(TOP_K):
            pltpu.make_async_copy(x_ref.at[pl.ds(t, 1)], xs_ref.at[pl.ds(dest_ref[TOP_K * t + j], 1)],
                                  sem).start(priority=j)
    for j in range(TOP_K):
        pltpu.make_async_copy(x_ref, xs_ref.at[pl.ds(0, tm)], sem).wait()


def _dispatch(pend, dest_flat, x1, n_rows):
    T, width = x1.shape
    tm = DISPATCH_TM
    return pl.pallas_call(
        _dispatch_kernel,
        grid_spec=pltpu.PrefetchScalarGridSpec(
            num_scalar_prefetch=1,
            grid=(T // tm,),
            in_specs=[pl.BlockSpec((TOP_K * tm,), lambda i, pe: (i,), memory_space=pltpu.SMEM),
                      pl.BlockSpec((tm, width), lambda i, pe: (i, 0))],
            out_specs=pl.BlockSpec(memory_space=pl.ANY),
            scratch_shapes=[pltpu.VMEM((EXPERT_ROWS, width), x1.dtype), pltpu.SemaphoreType.DMA,
                            pltpu.SemaphoreType.DMA],
        ),
        out_shape=jax.ShapeDtypeStruct((n_rows, width), x1.dtype),
        compiler_params=pltpu.CompilerParams(dimension_semantics=("arbitrary",), vmem_limit_bytes=VMEM_LIMIT),
        name="dispatch",
    )(pend, dest_flat, x1)


def _experts_kernel(pend_ref, xs_ref, wg_ref, wu_ref, wd_ref, ys_ref, xbuf_ref, ybuf_ref, wgu_ref, wdb_ref,
                    in_sem, out_sem):
    rows = EXPERT_ROWS
    e = pl.program_id(0)
    first = jnp.where(e == 0, 0, pend_ref[jnp.maximum(e - 1, 0)]) // rows
    last = pend_ref[e] // rows
    n_used = pend_ref[N_EXPERTS - 1] // rows

    def block_rows(ref, b):
        return ref.at[pl.ds(pl.multiple_of(b * rows, rows), rows)]

    row_priority = 1

    def x_copy(b, slot):
        return pltpu.make_async_copy(block_rows(xs_ref, b), xbuf_ref.at[slot], in_sem.at[slot])

    def y_copy(b, slot):
        return pltpu.make_async_copy(ybuf_ref.at[slot], block_rows(ys_ref, b), out_sem.at[slot])

    @pl.when((e == 0) & (n_used > 0))
    def _():
        x_copy(0, 0).start(priority=row_priority)

    @pl.when(last > first)
    def _():
        wgu_ref[:, :D_EXPERT] = wg_ref[0].astype(BF16)
        wgu_ref[:, D_EXPERT:] = wu_ref[0].astype(BF16)
        wdb_ref[...] = wd_ref[0].astype(BF16)

        def body(b, _):
            slot = b % 2

            @pl.when(b + 1 < n_used)
            def _():
                x_copy(b + 1, 1 - slot).start(priority=row_priority)

            x_copy(b, slot).wait()

            @pl.when(b >= 2)
            def _():
                y_copy(b - 2, slot).wait()

            xw = xbuf_ref[slot]
            x_lo = lax.bitcast_convert_type(xw << 16, F32)
            x_hi = lax.bitcast_convert_type(xw & jnp.uint32(0xFFFF0000), F32)
            xb = jnp.concatenate([x_lo, x_hi], axis=1).astype(BF16)
            h = jnp.dot(xb, wgu_ref[...], preferred_element_type=F32)
            hg = h[:, :D_EXPERT]
            hid = hg * _sigmoid(hg) * h[:, D_EXPERT:]
            ybuf_ref[slot] = jnp.dot(hid.astype(BF16), wdb_ref[...], preferred_element_type=F32)
            y_copy(b, slot).start(priority=row_priority)
            return 0

        lax.fori_loop(first, last, body, 0)

    @pl.when(e == N_EXPERTS - 1)
    def _():
        @pl.when(n_used >= 2)
        def _():
            y_copy(n_used - 2, n_used % 2).wait()

        @pl.when(n_used >= 1)
        def _():
            y_copy(n_used - 1, (n_used - 1) % 2).wait()

        ybuf_ref[0] = jnp.zeros(ybuf_ref.shape[1:], F32)
        first_unused = n_used
        n_tail = ys_ref.shape[0] // rows

        def start_unused(b, _):
            y_copy(b, 0).start()
            return 0

        def wait_unused(b, _):
            y_copy(b, 0).wait()
            return 0

        lax.fori_loop(first_unused, n_tail, start_unused, 0)
        lax.fori_loop(first_unused, n_tail, wait_unused, 0)


def _experts(pend, xs, wg, wu, wd):
    n_rows = xs.shape[0]
    rows = EXPERT_ROWS
    wspec = lambda shape: pl.BlockSpec((1,) + shape, lambda e, pe: (e, 0, 0))
    return pl.pallas_call(
        _experts_kernel,
        grid_spec=pltpu.PrefetchScalarGridSpec(
            num_scalar_prefetch=1,
            grid=(N_EXPERTS,),
            in_specs=[pl.BlockSpec(memory_space=pl.ANY),
                      wspec((D_MODEL, D_EXPERT)), wspec((D_MODEL, D_EXPERT)), wspec((D_EXPERT, D_MODEL))],
            out_specs=pl.BlockSpec(memory_space=pl.ANY),
            scratch_shapes=[pltpu.VMEM((2, rows, D_MODEL // 2), jnp.uint32), pltpu.VMEM((2, rows, D_MODEL), F32),
                            pltpu.VMEM((D_MODEL, 2 * D_EXPERT), BF16), pltpu.VMEM((D_EXPERT, D_MODEL), BF16),
                            pltpu.SemaphoreType.DMA((2,)), pltpu.SemaphoreType.DMA((2,))],
        ),
        out_shape=jax.ShapeDtypeStruct((n_rows, D_MODEL), F32),
        compiler_params=pltpu.CompilerParams(dimension_semantics=("arbitrary",), vmem_limit_bytes=VMEM_LIMIT),
        name="experts",
    )(pend, xs, wg, wu, wd)


def _combine_kernel(dest_ref, dest_next_ref, ys_ref, base_ref, route_ref, lg_ref, lb_ref, o_ref, buf_ref, sem):
    tm = buf_ref.shape[2]
    i = pl.program_id(0)

    def gather(dref, offset, s):
        for t in range(tm):
            for j in range(TOP_K):
                pltpu.make_async_copy(ys_ref.at[pl.ds(dref[offset + TOP_K * t + j], 1)],
                                      buf_ref.at[s, j, pl.ds(t, 1)], sem.at[s]).start(priority=j)

    def drain(s):
        for j in range(TOP_K):
            pltpu.make_async_copy(ys_ref.at[pl.ds(0, tm)], buf_ref.at[s, j], sem.at[s]).wait()

    def finish(s):
        rows = slice(s * tm, (s + 1) * tm)
        drain(s)
        route = route_ref[rows, :]
        ffn = buf_ref[s, 0] * route[:, 2:3] + buf_ref[s, 1] * route[:, 3:4]
        o_ref[rows, :] = _layer_norm(base_ref[rows, :] + ffn, lg_ref[...], lb_ref[...], LN_EPS)

    @pl.when(i == 0)
    def _():
        gather(dest_ref, 0, 0)

    gather(dest_ref, TOP_K * tm, 1)
    finish(0)
    gather(dest_next_ref, 0, 0)
    finish(1)

    @pl.when(i == pl.num_programs(0) - 1)
    def _():
        drain(0)


def _combine(dest_flat, ys, base, route, l2g, l2b):
    T = base.shape[0]
    tm = COMBINE_TM
    nt = T // tm
    return pl.pallas_call(
        _combine_kernel,
        grid=(nt // 2,),
        in_specs=[pl.BlockSpec((2 * TOP_K * tm,), lambda i: (i,), memory_space=pltpu.SMEM),
                  pl.BlockSpec((TOP_K * tm,), lambda i: (jnp.minimum(2 * i + 2, nt - 1),), memory_space=pltpu.SMEM),
                  pl.BlockSpec(memory_space=pl.ANY),
                  pl.BlockSpec((2 * tm, D_MODEL), lambda i: (i, 0)), pl.BlockSpec((2 * tm, LANES), lambda i: (i, 0)),
                  pl.BlockSpec((1, D_MODEL), lambda i: (0, 0)), pl.BlockSpec((1, D_MODEL), lambda i: (0, 0))],
        out_specs=pl.BlockSpec((2 * tm, D_MODEL), lambda i: (i, 0)),
        out_shape=jax.ShapeDtypeStruct((T, D_MODEL), F32),
        scratch_shapes=[pltpu.VMEM((2, TOP_K, tm, D_MODEL), F32), pltpu.SemaphoreType.DMA((2,))],
        compiler_params=pltpu.CompilerParams(dimension_semantics=("arbitrary",), vmem_limit_bytes=VMEM_LIMIT),
        name="combine",
    )(dest_flat, dest_flat, ys, base, route, l2g, l2b)


def _block_diag_const(n, blk, val):
    idx = jnp.arange(n) // blk
    return jnp.where(idx[:, None] == idx[None, :], val, 0.0).astype(BF16)


def kernel(x, p, ln_emb_g, ln_emb_b, w_in, mu_shift, w0, w_decay_up, a0, w_iclr_up, w_gate_up, k_k, k_a, r_k, gn_g, gn_b, gmlp_ln_g, gmlp_ln_b, w_spatial, b_spatial, w_out, ln1_g, ln1_b, w_group_router, b_group_router, w_expert_router, b_expert_router, w_exp_gate, w_exp_up, w_exp_down, w_ple_gate, b_ple_gate, w_ple_proj, ln2_g, ln2_b):
    B, S, D = x.shape
    T = B * S
    row = lambda t: t.reshape(1, -1).astype(F32)

    zl = jnp.zeros((DECAY_LORA, D_RWKV), F32)
    wwa = jnp.concatenate([jnp.concatenate([w_decay_up[0], zl], axis=1),
                           jnp.concatenate([zl, w_iclr_up[0]], axis=1)], axis=0).astype(BF16)
    w0a0 = jnp.concatenate([w0[0], a0[0]]).reshape(1, -1)
    eones = jnp.tile(_block_diag_const(2 * LANES, HEAD, 1.0), (2, 1))
    emean = jnp.tile(_block_diag_const(LANES, HEAD, 1.0 / HEAD), (2, 1))

    r, lw, k, v, a, b, g, bonus, yb = _prep(
        x, row(ln_emb_g), row(ln_emb_b), w_in[0].astype(BF16), row(mu_shift[0]), wwa, w0a0,
        w_gate_up[0].astype(BF16), row(k_k[0]), row(k_a[0]), row(r_k[0]), eones,
        row(gmlp_ln_g[0]), row(gmlp_ln_b[0]), w_spatial[0], b_spatial[0].T)

    ya = _wkv(r, lw, k, v, a, b, g, bonus, gn_g[0].reshape(N_PAIRS, 1, LANES), gn_b[0].reshape(N_PAIRS, 1, LANES),
              emean)

    wr = jnp.concatenate([w_expert_router[0].reshape(D, N_EXPERTS), w_group_router[0],
                          jnp.zeros((D, LANES - N_EXPERTS - N_GROUPS), F32)], axis=1)
    wr3 = jnp.stack(_split2(wr))
    br = jnp.concatenate([b_expert_router[0].reshape(-1), b_group_router[0],
                          jnp.zeros((LANES - N_EXPERTS - N_GROUPS,), F32)]).reshape(1, LANES)
    base, x1, route, counts = _mixer(x, row(ln_emb_g), row(ln_emb_b), ya, yb, w_out[0].astype(BF16), row(ln1_g[0]),
                                     row(ln1_b[0]), wr3, br, p[0], w_ple_gate[0].astype(BF16),
                                     row(b_ple_gate[0]), w_ple_proj[0].astype(BF16))
    base = base.reshape(T, D)
    x1 = x1.reshape(T, D // 2)
    route = route.reshape(T, LANES)

    n_blocks = -(-(T * TOP_K) // EXPERT_ROWS) + N_EXPERTS
    dest, pend = _slots(route, counts)
    dest_flat = dest[:, :TOP_K].reshape(T * TOP_K)
    pend = pend[0, :N_EXPERTS]

    xs = _dispatch(pend, dest_flat, x1, n_blocks * EXPERT_ROWS)
    ys = _experts(pend, xs, w_exp_gate[0], w_exp_up[0], w_exp_down[0])
    out = _combine(dest_flat, ys, base, route, row(ln2_g[0]), row(ln2_b[0]))
    return out.reshape(B, S, D)
```

```python
import functools
import math

import jax
import jax.numpy as jnp
from jax import lax
from jax.experimental import pallas as pl
from jax.experimental.pallas import tpu as pltpu

F32 = jnp.float32
BF16 = jnp.bfloat16

D_MODEL = 1024
D_RWKV = 512
HEAD = 64
D_GMLP = 512
GMLP_GROUPS = 4
GROUP_W = 128
GCHUNK = 128
DECAY_LORA = 64
ICLR_LORA = 64
GATE_LORA = 128
N_SHIFT = 3 * D_RWKV + DECAY_LORA + ICLR_LORA + GATE_LORA
D_IN = N_SHIFT + 2 * D_GMLP
D_PLE = 256
N_GROUPS = 4
EXPERTS_PER_GROUP = 8
N_EXPERTS = 32
TOP_K = 2
D_EXPERT = 512
DEPTH = 1
ALPHA = (2.0 * DEPTH) ** 0.25
LN_EPS = 1e-5
GN_EPS = 64e-5
DECAY_SCALE = math.exp(-0.5)

LANES = 128
WKV_CHUNK = 64
N_PAIRS = D_RWKV // LANES
VMEM_LIMIT = 56 * 1024 * 1024

PREP_TM = 512
WKV_TB = 256
WKV_PAIRS = 4
MIX_TM = 512
SLOT_TM = 2048
EXPERT_ROWS = 256
EXPERT_DEPTH = 4
DISPATCH_TM = 512
COMBINE_TM = 256
NEG = -1e30


def _dot(a, b):
    return jnp.dot(a.astype(BF16), b.astype(BF16), preferred_element_type=F32)


def _dot_nt(a, b):
    return lax.dot_general(a.astype(BF16), b.astype(BF16), (((1,), (1,)), ((), ())),
                           preferred_element_type=F32)


def _split3(x):
    hi = x.astype(BF16)
    r1 = x - hi.astype(F32)
    mid = r1.astype(BF16)
    lo = (r1 - mid.astype(F32)).astype(BF16)
    return hi, mid, lo


def _dot3_lhs(x, w):
    hi, mid, lo = _split3(x)
    w = w.astype(BF16)
    return (jnp.dot(hi, w, preferred_element_type=F32) + jnp.dot(mid, w, preferred_element_type=F32)
            + jnp.dot(lo, w, preferred_element_type=F32))


def _split2(x):
    hi = x.astype(BF16)
    return hi, (x - hi.astype(F32)).astype(BF16)


def _dot2_lhs(x, w2):
    hi, lo = _split2(x)
    return jnp.dot(jnp.concatenate([hi, lo], axis=1), w2, preferred_element_type=F32)


def _dot3_rhs(w3, x):
    hi, mid, lo = _split3(x)
    return jnp.dot(w3, jnp.concatenate([hi, mid, lo], axis=0), preferred_element_type=F32)


def _layer_norm(x, g, b, eps):
    mu = jnp.mean(x, axis=-1, keepdims=True)
    xc = x - mu
    var = jnp.mean(xc * xc, axis=-1, keepdims=True)
    return xc * lax.rsqrt(var + eps) * g + b


def _sigmoid(x):
    return 1.0 / (1.0 + jnp.exp(-x))


def _iota(shape, dim):
    return lax.broadcasted_iota(jnp.int32, shape, dim)


def _prep_kernel(x_ref, lng_ref, lnb_ref, win_ref, mu_ref, wwa_ref, w0a0_ref, wg_ref, kk_ref, ka_ref, rk_ref,
                 eones_ref, glng_ref, glnb_ref, wsp_ref, bsp_ref,
                 r_ref, lw_ref, k_ref, v_ref, a_ref, b_ref, g_ref, bonus_ref, yb_ref, carry_ref):
    tm = x_ref.shape[1]

    @pl.when(pl.program_id(1) == 0)
    def _():
        carry_ref[...] = jnp.zeros_like(carry_ref)

    x0 = _layer_norm(x_ref[0], lng_ref[...], lnb_ref[...], LN_EPS)
    proj = jnp.dot(x0.astype(BF16), win_ref[...], preferred_element_type=F32)

    h = proj[:, :N_SHIFT]
    rolled = pltpu.roll(h, 1, 0)
    first = _iota((tm, N_SHIFT), 0) == 0
    prev = jnp.where(first, jnp.broadcast_to(carry_ref[0:1, :], (tm, N_SHIFT)), rolled)
    carry_ref[0:1, :] = h[tm - 1:tm, :]
    h = h + (prev - h) * mu_ref[...]

    r = h[:, 0:D_RWKV]
    k = h[:, D_RWKV:2 * D_RWKV]
    v = h[:, 2 * D_RWKV:3 * D_RWKV]
    xwa = h[:, 3 * D_RWKV:3 * D_RWKV + LANES]
    xg = h[:, 3 * D_RWKV + LANES:N_SHIFT]

    lane = _iota((tm, LANES), 1)
    twa = jnp.where(lane < DECAY_LORA, jnp.tanh(xwa), xwa)
    da = _dot(twa, wwa_ref[...]) + w0a0_ref[...]
    logw = -DECAY_SCALE * _sigmoid(da[:, :D_RWKV])
    ag = _sigmoid(da[:, D_RWKV:])
    g = _dot(_sigmoid(xg), wg_ref[...])

    eones2 = eones_ref[...]

    def head_sum(t):
        half = 2 * LANES
        return jnp.concatenate([_dot2_lhs(t[:, :half], eones2), _dot2_lhs(t[:, half:], eones2)], axis=1)

    kk = k * kk_ref[...]
    kk = kk / jnp.maximum(jnp.sqrt(head_sum(kk * kk)), 1e-12)
    k = k * (1.0 + (ag - 1.0) * ka_ref[...])
    bonus = head_sum(r * k * rk_ref[...]) * v

    for p in range(N_PAIRS):
        sl = slice(p * LANES, (p + 1) * LANES)
        r_ref[0, p] = r[:, sl]
        lw_ref[0, p] = logw[:, sl]
        k_ref[0, p] = k[:, sl]
        v_ref[0, p] = v[:, sl]
        a_ref[0, p] = -kk[:, sl]
        b_ref[0, p] = (kk * ag)[:, sl]
        g_ref[0, p] = g[:, sl]
        bonus_ref[0, p] = bonus[:, sl]

    zin = proj[:, N_SHIFT:]
    z = 0.5 * zin * (1.0 + lax.erf(zin * (0.5 ** 0.5)))
    zu = z[:, :D_GMLP]
    zv = z[:, D_GMLP:]
    causal = _iota((GCHUNK, GCHUNK), 0) >= _iota((GCHUNK, GCHUNK), 1)
    for gi in range(GMLP_GROUPS):
        gs = slice(gi * GROUP_W, (gi + 1) * GROUP_W)
        zvn = _layer_norm(zv[:, gs], glng_ref[:, gs], glnb_ref[:, gs], LN_EPS)
        ws = jnp.where(causal, wsp_ref[gi], 0.0).astype(BF16)
        bcol = bsp_ref[:, gi:gi + 1]
        for c in range(tm // GCHUNK):
            ts = slice(c * GCHUNK, (c + 1) * GCHUNK)
            mixed = jnp.dot(ws, zvn[ts].astype(BF16), preferred_element_type=F32) + bcol
            yb_ref[0, ts, gs] = (zu[ts, gs] * mixed).astype(BF16)


def _prep(x, ln_g, ln_b, w_in, mu, wwa, w0a0, wg, k_k, k_a, r_k, eones, glng, glnb, wsp, bsp):
    B, S, _ = x.shape
    tm = PREP_TM
    const = lambda shape: pl.BlockSpec(shape, lambda b, s: (0,) * len(shape))
    pair_spec = pl.BlockSpec((1, N_PAIRS, tm, LANES), lambda b, s: (b, 0, s, 0))
    pair_shape = jax.ShapeDtypeStruct((B, N_PAIRS, S, LANES), F32)
    return pl.pallas_call(
        _prep_kernel,
        grid=(B, S // tm),
        in_specs=[
            pl.BlockSpec((1, tm, D_MODEL), lambda b, s: (b, s, 0)),
            const((1, D_MODEL)), const((1, D_MODEL)), const((D_MODEL, D_IN)), const((1, N_SHIFT)),
            const((LANES, 2 * D_RWKV)), const((1, 2 * D_RWKV)), const((GATE_LORA, D_RWKV)),
            const((1, D_RWKV)), const((1, D_RWKV)), const((1, D_RWKV)), const((4 * LANES, 2 * LANES)),
            const((1, D_GMLP)), const((1, D_GMLP)), const((GMLP_GROUPS, GCHUNK, GCHUNK)),
            const((GCHUNK, GMLP_GROUPS)),
        ],
        out_specs=[pair_spec] * 8 + [pl.BlockSpec((1, tm, D_GMLP), lambda b, s: (b, s, 0))],
        out_shape=[pair_shape] * 8 + [jax.ShapeDtypeStruct((B, S, D_GMLP), BF16)],
        scratch_shapes=[pltpu.VMEM((8, N_SHIFT), F32)],
        compiler_params=pltpu.CompilerParams(dimension_semantics=("arbitrary", "arbitrary"),
                                             vmem_limit_bytes=VMEM_LIMIT),
        name="prep",
    )(x, ln_g, ln_b, w_in, mu, wwa, w0a0, wg, k_k, k_a, r_k, eones, glng, glnb, wsp, bsp)


def _wkv_kernel(r_ref, lw_ref, k_ref, v_ref, a_ref, b_ref, g_ref, bonus_ref, gng_ref, gnb_ref, emean_ref,
                o_ref, h_ref):
    C = WKV_CHUNK
    tb = r_ref.shape[2]

    @pl.when(pl.program_id(2) == 0)
    def _():
        h_ref[...] = jnp.zeros_like(h_ref)

    tok = _iota((C, LANES), 0)
    lane = _iota((C, LANES), 1)
    head0 = lane < HEAD
    strict = tok > lane % HEAD
    incl = tok >= lane % HEAD
    eye_w = (tok == lane % HEAD).astype(F32)
    rr = _iota((LANES, LANES), 0)
    cc = _iota((LANES, LANES), 1)
    eye = (rr == cc).astype(F32)
    same_head = (rr < HEAD) == (cc < HEAD)
    ltri3 = (_iota((C, 3 * C), 0) >= _iota((C, 3 * C), 1) % C).astype(BF16)

    def stack(x):
        xb = x.astype(BF16)
        zero = jnp.zeros_like(xb)
        return jnp.concatenate([jnp.where(head0, xb, zero), jnp.where(head0, zero, xb)], axis=0)

    def stack2(x, y):
        return jnp.concatenate([stack(x), stack(y)], axis=1)

    n_pairs = r_ref.shape[1]
    n_chunks = tb // C
    units = [(q, c) for q in range(n_pairs) for c in range(n_chunks)]

    def load(ref):
        return [ref[0, q, c * C:(c + 1) * C, :] for q, c in units]

    r_, lw_, k_, v_, a_, b_ = (load(ref) for ref in (r_ref, lw_ref, k_ref, v_ref, a_ref, b_ref))
    cum_ = [_dot3_rhs(ltri3, lw) for lw in lw_]
    cend_ = [cum[C - 1:C, :] for cum in cum_]
    at_ = [a * jnp.exp(cum - lw) for a, cum, lw in zip(a_, cum_, lw_)]
    rt_ = [r * jnp.exp(cum) for r, cum in zip(r_, cum_)]
    ginv_ = [jnp.exp(-cum) for cum in cum_]
    gend_ = [jnp.exp(cend - cum) for cend, cum in zip(cend_, cum_)]
    bk_end_ = [jnp.concatenate([b * ge, k * ge], axis=0) for b, k, ge in zip(b_, k_, gend_)]
    vst_ = [stack(v) for v in v_]

    G_ = [_dot_nt(jnp.concatenate([at, rt], axis=0), jnp.concatenate([stack(b * gi), stack(k * gi)], axis=0))
          for at, rt, b, k, gi in zip(at_, rt_, b_, k_, ginv_)]
    n1_ = [jnp.where(strict, G[:C, :LANES], 0.0) for G in G_]
    aak_ = [jnp.where(strict, G[:C, LANES:], 0.0) for G in G_]
    arb_ = [jnp.where(incl, G[C:, :LANES], 0.0) for G in G_]
    ark_ = [jnp.where(incl, G[C:, LANES:], 0.0) for G in G_]
    av_ = [_dot(jnp.concatenate([aak, ark], axis=0), vst) for aak, ark, vst in zip(aak_, ark_, vst_)]

    s1_ = [stack(n1) for n1 in n1_]
    n2_ = [_dot(n1, s1) for n1, s1 in zip(n1_, s1_)]
    x_ = [_dot(n2, jnp.concatenate([s1, stack(n2)], axis=1)) for n2, s1 in zip(n2_, s1_)]
    t_ = [eye_w + n1 + n2 + x[:, :LANES] for n1, n2, x in zip(n1_, n2_, x_)]
    np_ = [x[:, LANES:] for x in x_]
    for _ in range(3):
        x_ = [_dot(npow, stack2(t, npow)) for t, npow in zip(t_, np_)]
        t_ = [t + x[:, :LANES] for t, x in zip(t_, x_)]
        np_ = [x[:, LANES:] for x in x_]
    t_ = [t + _dot(npow, stack(t)) for t, npow in zip(t_, np_)]

    x_ = [_dot(t, stack2(at, av[:C])) for t, at, av in zip(t_, at_, av_)]
    z_ = [_dot(arb, stack2(x[:, :LANES], x[:, LANES:])) for arb, x in zip(arb_, x_)]
    rp_ = [rt + z[:, :LANES] for rt, z in zip(rt_, z_)]
    p3_ = [z[:, LANES:] + av[C:] for z, av in zip(z_, av_)]
    rhs_ = [jnp.concatenate([x, jnp.concatenate([jnp.zeros_like(v), v], axis=1)], axis=0)
            for x, v in zip(x_, v_)]
    mq_ = [_dot(bk_end.T, rhs) for bk_end, rhs in zip(bk_end_, rhs_)]
    m_ = [eye * jnp.exp(cend) + jnp.where(same_head, mq[:, :LANES], 0.0) for cend, mq in zip(cend_, mq_)]
    q_ = [jnp.where(same_head, mq[:, LANES:], 0.0) for mq in mq_]

    pairs = range(n_pairs)
    H_ = [h_ref[q] for q in pairs]
    ys_ = [[] for _ in pairs]
    for c in range(n_chunks):
        for q in pairs:
            u = q * n_chunks + c
            ys_[q].append(_dot(rp_[u], H_[q]) + p3_[u])
        H_ = [_dot(m_[q * n_chunks + c], H_[q]) + q_[q * n_chunks + c] for q in pairs]
    for q in pairs:
        h_ref[q] = H_[q]

    emean = emean_ref[...]
    y_ = [jnp.concatenate(ys, axis=0) for ys in ys_]
    mu_ = [_dot2_lhs(y, emean) for y in y_]
    yc_ = [y - mu for y, mu in zip(y_, mu_)]
    var_ = [_dot2_lhs(yc * yc, emean) for yc in yc_]
    for q in pairs:
        yn = yc_[q] * lax.rsqrt(var_[q] + GN_EPS) * gng_ref[q] + gnb_ref[q]
        o_ref[0, q] = ((yn + bonus_ref[0, q]) * g_ref[0, q]).astype(BF16)


def _wkv(r, lw, k, v, a, b, g, bonus, gn_g, gn_b, emean):
    B, P, S, _ = r.shape
    tb = WKV_TB
    pp = WKV_PAIRS
    seq = pl.BlockSpec((1, pp, tb, LANES), lambda bi, p, s: (bi, p, s, 0))
    par = pl.BlockSpec((pp, 1, LANES), lambda bi, p, s: (p, 0, 0))
    return pl.pallas_call(
        _wkv_kernel,
        grid=(B, P // pp, S // tb),
        in_specs=[seq] * 8 + [par, par, pl.BlockSpec((2 * LANES, LANES), lambda bi, p, s: (0, 0))],
        out_specs=seq,
        out_shape=jax.ShapeDtypeStruct((B, P, S, LANES), BF16),
        scratch_shapes=[pltpu.VMEM((pp, LANES, LANES), F32)],
        compiler_params=pltpu.CompilerParams(dimension_semantics=("arbitrary", "arbitrary", "arbitrary"),
                                             vmem_limit_bytes=VMEM_LIMIT),
        name="wkv",
    )(r, lw, k, v, a, b, g, bonus, gn_g, gn_b, emean)


def _mixer_kernel(x_ref, lng_ref, lnb_ref, ya_ref, yb_ref, wout_ref, l1g_ref, l1b_ref, wr_ref, br_ref,
                  p_ref, wpg_ref, bpg_ref, wpp_ref, base_ref, x1_ref, route_ref, counts_ref, carry_ref):
    tm = x_ref.shape[1]

    @pl.when((pl.program_id(0) == 0) & (pl.program_id(1) == 0))
    def _():
        carry_ref[...] = jnp.zeros_like(carry_ref)

    x0 = _layer_norm(x_ref[0], lng_ref[...], lnb_ref[...], LN_EPS)
    ymix = jnp.concatenate([ya_ref[0, p] for p in range(N_PAIRS)] + [yb_ref[0]], axis=-1)
    mix = jnp.dot(ymix, wout_ref[...], preferred_element_type=F32)
    x1 = _layer_norm(ALPHA * x0 + mix, l1g_ref[...], l1b_ref[...], LN_EPS)
    x1b = x1.astype(BF16)
    half = D_MODEL // 2
    lo_bits = lax.bitcast_convert_type(x1b[:, :half].astype(F32), jnp.uint32)
    hi_bits = lax.bitcast_convert_type(x1b[:, half:].astype(F32), jnp.uint32)
    x1_ref[0] = (hi_bits & jnp.uint32(0xFFFF0000)) | (lo_bits >> 16)

    hi, mid = _split2(x1)
    whi = wr_ref[0]
    wmid = wr_ref[1]
    d = lambda u, w: jnp.dot(u, w, preferred_element_type=F32)
    logits = (d(hi, whi) + d(hi, wmid) + d(mid, whi)) + br_ref[...]
    lane = _iota((tm, LANES), 1).astype(F32)
    far = float(4 * LANES)
    is_g = jnp.where(lane >= N_EXPERTS, jnp.where(lane < N_EXPERTS + N_GROUPS, 1.0, 0.0), 0.0) > 0.5
    gl = jnp.where(is_g, logits, NEG)
    gmax = jnp.max(gl, axis=-1, keepdims=True)
    gsel = jnp.min(jnp.where(gl == gmax, lane, far), axis=-1, keepdims=True) - N_EXPERTS
    p_group = 1.0 / jnp.sum(jnp.where(is_g, jnp.exp(gl - gmax), 0.0), axis=-1, keepdims=True)
    grp_of_lane = jnp.floor(lane * (1.0 / EXPERTS_PER_GROUP))
    el = jnp.where(grp_of_lane == gsel, logits, NEG)
    v1 = jnp.max(el, axis=-1, keepdims=True)
    i1 = jnp.min(jnp.where(el == v1, lane, far), axis=-1, keepdims=True)
    el2 = jnp.where(lane == i1, NEG, el)
    v2 = jnp.max(el2, axis=-1, keepdims=True)
    i2 = jnp.min(jnp.where(el2 == v2, lane, far), axis=-1, keepdims=True)
    e21 = jnp.exp(v2 - v1)
    w1 = p_group / (1.0 + e21)
    w2 = p_group * e21 / (1.0 + e21)

    oh1 = lane == i1
    oh2 = lane == i2
    below = (_iota((tm, tm), 0) > _iota((tm, tm), 1)).astype(BF16)
    o1 = jnp.where(oh1, 1.0, 0.0)
    o2 = jnp.where(oh2, 1.0, 0.0)
    c1 = jnp.dot(below, o1.astype(BF16), preferred_element_type=F32)
    c2 = jnp.dot(below, o2.astype(BF16), preferred_element_type=F32)
    tot1 = jnp.sum(o1, axis=0, keepdims=True)
    carry = carry_ref[0:1, :]
    rank1 = jnp.sum(jnp.where(oh1, c1 + carry, 0.0), axis=-1, keepdims=True)
    rank2 = jnp.sum(jnp.where(oh2, c2 + carry + tot1, 0.0), axis=-1, keepdims=True)
    carry = carry + tot1 + jnp.sum(o2, axis=0, keepdims=True)
    carry_ref[0:1, :] = carry
    counts_ref[...] = jnp.broadcast_to(carry, counts_ref.shape)

    fields = (i1, i2, w1, w2, rank1, rank2)
    route = jnp.zeros((tm, LANES), F32)
    for n, f in enumerate(fields):
        route = jnp.where(lane == n, f, route)
    route_ref[0] = route

    gate = _sigmoid(jnp.dot(x1b, wpg_ref[...], preferred_element_type=F32) + bpg_ref[...])
    ple = gate * jnp.dot(p_ref[0].astype(BF16), wpp_ref[...], preferred_element_type=F32)
    base_ref[0] = ALPHA * x1 + ple


def _mixer(x, ln_g, ln_b, ya, yb, w_out, l1g, l1b, wr3, br, p, wpg, bpg, wpp):
    B, S, _ = x.shape
    tm = MIX_TM
    const = lambda shape: pl.BlockSpec(shape, lambda b, s: (0,) * len(shape))
    row = lambda w: pl.BlockSpec((1, tm, w), lambda b, s: (b, s, 0))
    return pl.pallas_call(
        _mixer_kernel,
        grid=(B, S // tm),
        in_specs=[
            row(D_MODEL), const((1, D_MODEL)), const((1, D_MODEL)),
            pl.BlockSpec((1, N_PAIRS, tm, LANES), lambda b, s: (b, 0, s, 0)), row(D_GMLP),
            const((D_MODEL, D_MODEL)), const((1, D_MODEL)), const((1, D_MODEL)),
            const((2, D_MODEL, LANES)), const((1, LANES)),
            row(D_PLE), const((D_MODEL, D_MODEL)), const((1, D_MODEL)), const((D_PLE, D_MODEL)),
        ],
        out_specs=[row(D_MODEL), row(D_MODEL // 2), row(LANES), const((8, LANES))],
        out_shape=[jax.ShapeDtypeStruct((B, S, D_MODEL), F32), jax.ShapeDtypeStruct((B, S, D_MODEL // 2), jnp.uint32),
                   jax.ShapeDtypeStruct((B, S, LANES), F32), jax.ShapeDtypeStruct((8, LANES), F32)],
        scratch_shapes=[pltpu.VMEM((8, LANES), F32)],
        compiler_params=pltpu.CompilerParams(dimension_semantics=("arbitrary", "arbitrary"),
                                             vmem_limit_bytes=VMEM_LIMIT),
        name="mixer",
    )(x, ln_g, ln_b, ya, yb, w_out, l1g, l1b, wr3, br, p, wpg, bpg, wpp)


def _slots_kernel(route_ref, counts_ref, dest_ref, pend_ref):
    tm = route_ref.shape[0]
    lane = _iota((tm, LANES), 1)
    route = route_ref[...]
    oh1 = lane == route[:, 0:1].astype(jnp.int32)
    oh2 = lane == route[:, 1:2].astype(jnp.int32)

    counts = counts_ref[0:1, :]
    padded = jnp.floor((counts + (EXPERT_ROWS - 1)) * (1.0 / EXPERT_ROWS)) * EXPERT_ROWS
    upper = (_iota((LANES, LANES), 0) <= _iota((LANES, LANES), 1)).astype(BF16)
    pend = _dot3_lhs(jnp.broadcast_to(padded, (8, LANES)), upper)[0:1, :]
    pstart = pend - padded
    d1 = jnp.sum(jnp.where(oh1, pstart, 0.0), axis=-1, keepdims=True) + route[:, 4:5]
    d2 = jnp.sum(jnp.where(oh2, pstart, 0.0), axis=-1, keepdims=True) + route[:, 5:6]
    dest_ref[...] = jnp.where(lane == 0, d1, jnp.where(lane == 1, d2, 0.0)).astype(jnp.int32)
    pend_ref[...] = jnp.broadcast_to(pend, (8, LANES)).astype(jnp.int32)


def _slots(route, counts):
    T = route.shape[0]
    tm = SLOT_TM
    return pl.pallas_call(
        _slots_kernel,
        grid=(T // tm,),
        in_specs=[pl.BlockSpec((tm, LANES), lambda i: (i, 0)), pl.BlockSpec((8, LANES), lambda i: (0, 0))],
        out_specs=[pl.BlockSpec((tm, LANES), lambda i: (i, 0)),
                   pl.BlockSpec((8, LANES), lambda i: (0, 0))],
        out_shape=[jax.ShapeDtypeStruct((T, LANES), jnp.int32), jax.ShapeDtypeStruct((8, LANES), jnp.int32)],
        compiler_params=pltpu.CompilerParams(dimension_semantics=("arbitrary",), vmem_limit_bytes=VMEM_LIMIT),
        name="slots",
    )(route, counts)


def _dispatch_kernel(pend_ref, dest_ref, x_ref, xs_ref, zero_ref, sem, zsem):
    tm = dest_ref.shape[0] // TOP_K

    @pl.when(pl.program_id(0) == 0)
    def _():
        zero_ref[...] = jnp.zeros_like(zero_ref)

        def tail(e):
            start = pl.multiple_of(jnp.maximum(pend_ref[e] - EXPERT_ROWS, 0), EXPERT_ROWS)
            return pltpu.make_async_copy(zero_ref, xs_ref.at[pl.ds(start, EXPERT_ROWS)], zsem)

        def unused(j):
            return pltpu.make_async_copy(
                zero_ref, xs_ref.at[pl.ds(pl.multiple_of(j * EXPERT_ROWS, EXPERT_ROWS), EXPERT_ROWS)], zsem)

        def start_unused(j, _):
            unused(j).start()
            return 0

        def wait_unused(j, _):
            unused(j).wait()
            return 0

        first_unused = pend_ref[N_EXPERTS - 1] // EXPERT_ROWS
        n_blocks = xs_ref.shape[0] // EXPERT_ROWS
        for e in range(N_EXPERTS):
            tail(e).start()
        lax.fori_loop(first_unused, n_blocks, start_unused, 0)
        for e in range(N_EXPERTS):
            tail(e).wait()
        lax.fori_loop(first_unused, n_blocks, wait_unused, 0)

    for t in range(tm):
        for j in range(TOP_K):
            pltpu.make_async_copy(x_ref.at[pl.ds(t, 1)], xs_ref.at[pl.ds(dest_ref[TOP_K * t + j], 1)],
                                  sem).start(priority=j)
    for j in range(TOP_K):
        pltpu.make_async_copy(x_ref, xs_ref.at[pl.ds(0, tm)], sem).wait()


def _dispatch(pend, dest_flat, x1, n_rows):
    T, width = x1.shape
    tm = DISPATCH_TM
    return pl.pallas_call(
        _dispatch_kernel,
        grid_spec=pltpu.PrefetchScalarGridSpec(
            num_scalar_prefetch=1,
            grid=(T // tm,),
            in_specs=[pl.BlockSpec((TOP_K * tm,), lambda i, pe: (i,), memory_space=pltpu.SMEM),
                      pl.BlockSpec((tm, width), lambda i, pe: (i, 0))],
            out_specs=pl.BlockSpec(memory_space=pl.ANY),
            scratch_shapes=[pltpu.VMEM((EXPERT_ROWS, width), x1.dtype), pltpu.SemaphoreType.DMA,
                            pltpu.SemaphoreType.DMA],
        ),
        out_shape=jax.ShapeDtypeStruct((n_rows, width), x1.dtype),
        compiler_params=pltpu.CompilerParams(dimension_semantics=("arbitrary",), vmem_limit_bytes=VMEM_LIMIT),
        name="dispatch",
    )(pend, dest_flat, x1)


def _experts_kernel(pend_ref, xs_ref, wg_ref, wu_ref, wd_ref, ys_ref, xbuf_ref, ybuf_ref, wgu_ref, wdb_ref,
                    in_sem, out_sem):
    rows = EXPERT_ROWS
    e = pl.program_id(0)
    first = jnp.where(e == 0, 0, pend_ref[jnp.maximum(e - 1, 0)]) // rows
    last = pend_ref[e] // rows
    n_used = pend_ref[N_EXPERTS - 1] // rows

    def block_rows(ref, b):
        return ref.at[pl.ds(pl.multiple_of(b * rows, rows), rows)]

    depth = xbuf_ref.shape[0]
    row_priority = 1

    def x_copy(b):
        slot = b % depth
        return pltpu.make_async_copy(block_rows(xs_ref, b), xbuf_ref.at[slot], in_sem.at[slot])

    def y_copy(b):
        slot = b % depth
        return pltpu.make_async_copy(ybuf_ref.at[slot], block_rows(ys_ref, b), out_sem.at[slot])

    @pl.when(e == 0)
    def _():
        for ahead in range(depth - 1):
            @pl.when(ahead < n_used)
            def _():
                x_copy(ahead).start(priority=row_priority)

    @pl.when(last > first)
    def _():
        wgu_ref[:, :D_EXPERT] = wg_ref[0].astype(BF16)
        wgu_ref[:, D_EXPERT:] = wu_ref[0].astype(BF16)
        wdb_ref[...] = wd_ref[0].astype(BF16)

        def body(b, _):
            slot = b % depth

            @pl.when(b + depth - 1 < n_used)
            def _():
                x_copy(b + depth - 1).start(priority=row_priority)

            x_copy(b).wait()

            @pl.when(b >= depth)
            def _():
                y_copy(b - depth).wait()

            xw = xbuf_ref[slot]
            x_lo = lax.bitcast_convert_type(xw << 16, F32)
            x_hi = lax.bitcast_convert_type(xw & jnp.uint32(0xFFFF0000), F32)
            xb = jnp.concatenate([x_lo, x_hi], axis=1).astype(BF16)
            h = jnp.dot(xb, wgu_ref[...], preferred_element_type=F32)
            hg = h[:, :D_EXPERT]
            hid = hg * _sigmoid(hg) * h[:, D_EXPERT:]
            ybuf_ref[slot] = jnp.dot(hid.astype(BF16), wdb_ref[...], preferred_element_type=F32)
            y_copy(b).start(priority=row_priority)
            return 0

        lax.fori_loop(first, last, body, 0)

    @pl.when(e == N_EXPERTS - 1)
    def _():
        for back in range(depth, 0, -1):
            @pl.when(n_used >= back)
            def _():
                y_copy(n_used - back).wait()

        ybuf_ref[0] = jnp.zeros(ybuf_ref.shape[1:], F32)

        def unused(b):
            return pltpu.make_async_copy(ybuf_ref.at[0], block_rows(ys_ref, b), out_sem.at[0])

        def start_unused(b, _):
            unused(b).start()
            return 0

        def wait_unused(b, _):
            unused(b).wait()
            return 0

        n_blocks = ys_ref.shape[0] // rows
        lax.fori_loop(n_used, n_blocks, start_unused, 0)
        lax.fori_loop(n_used, n_blocks, wait_unused, 0)


def _experts(pend, xs, wg, wu, wd):
    n_rows = xs.shape[0]
    rows = EXPERT_ROWS
    wspec = lambda shape: pl.BlockSpec((1,) + shape, lambda e, pe: (e, 0, 0))
    return pl.pallas_call(
        _experts_kernel,
        grid_spec=pltpu.PrefetchScalarGridSpec(
            num_scalar_prefetch=1,
            grid=(N_EXPERTS,),
            in_specs=[pl.BlockSpec(memory_space=pl.ANY),
                      wspec((D_MODEL, D_EXPERT)), wspec((D_MODEL, D_EXPERT)), wspec((D_EXPERT, D_MODEL))],
            out_specs=pl.BlockSpec(memory_space=pl.ANY),
            scratch_shapes=[pltpu.VMEM((EXPERT_DEPTH, rows, D_MODEL // 2), jnp.uint32),
                            pltpu.VMEM((EXPERT_DEPTH, rows, D_MODEL), F32),
                            pltpu.VMEM((D_MODEL, 2 * D_EXPERT), BF16), pltpu.VMEM((D_EXPERT, D_MODEL), BF16),
                            pltpu.SemaphoreType.DMA((EXPERT_DEPTH,)), pltpu.SemaphoreType.DMA((EXPERT_DEPTH,))],
        ),
        out_shape=jax.ShapeDtypeStruct((n_rows, D_MODEL), F32),
        compiler_params=pltpu.CompilerParams(dimension_semantics=("arbitrary",), vmem_limit_bytes=VMEM_LIMIT),
        name="experts",
    )(pend, xs, wg, wu, wd)


def _combine_kernel(dest_ref, dest_next_ref, ys_ref, base_ref, route_ref, lg_ref, lb_ref, o_ref, buf_ref, sem):
    tm = buf_ref.shape[2]
    i = pl.program_id(0)

    def gather(dref, offset, s):
        for t in range(tm):
            for j in range(TOP_K):
                pltpu.make_async_copy(ys_ref.at[pl.ds(dref[offset + TOP_K * t + j], 1)],
                                      buf_ref.at[s, j, pl.ds(t, 1)], sem.at[s]).start(priority=j)

    def drain(s):
        for j in range(TOP_K):
            pltpu.make_async_copy(ys_ref.at[pl.ds(0, tm)], buf_ref.at[s, j], sem.at[s]).wait()

    def finish(s):
        rows = slice(s * tm, (s + 1) * tm)
        drain(s)
        route = route_ref[rows, :]
        ffn = buf_ref[s, 0] * route[:, 2:3] + buf_ref[s, 1] * route[:, 3:4]
        o_ref[rows, :] = _layer_norm(base_ref[rows, :] + ffn, lg_ref[...], lb_ref[...], LN_EPS)

    @pl.when(i == 0)
    def _():
        gather(dest_ref, 0, 0)

    gather(dest_ref, TOP_K * tm, 1)
    finish(0)
    gather(dest_next_ref, 0, 0)
    finish(1)

    @pl.when(i == pl.num_programs(0) - 1)
    def _():
        drain(0)


def _combine(dest_flat, ys, base, route, l2g, l2b):
    T = base.shape[0]
    tm = COMBINE_TM
    nt = T // tm
    return pl.pallas_call(
        _combine_kernel,
        grid=(nt // 2,),
        in_specs=[pl.BlockSpec((2 * TOP_K * tm,), lambda i: (i,), memory_space=pltpu.SMEM),
                  pl.BlockSpec((TOP_K * tm,), lambda i: (jnp.minimum(2 * i + 2, nt - 1),), memory_space=pltpu.SMEM),
                  pl.BlockSpec(memory_space=pl.ANY),
                  pl.BlockSpec((2 * tm, D_MODEL), lambda i: (i, 0)), pl.BlockSpec((2 * tm, LANES), lambda i: (i, 0)),
                  pl.BlockSpec((1, D_MODEL), lambda i: (0, 0)), pl.BlockSpec((1, D_MODEL), lambda i: (0, 0))],
        out_specs=pl.BlockSpec((2 * tm, D_MODEL), lambda i: (i, 0)),
        out_shape=jax.ShapeDtypeStruct((T, D_MODEL), F32),
        scratch_shapes=[pltpu.VMEM((2, TOP_K, tm, D_MODEL), F32), pltpu.SemaphoreType.DMA((2,))],
        compiler_params=pltpu.CompilerParams(dimension_semantics=("arbitrary",), vmem_limit_bytes=VMEM_LIMIT),
        name="combine",
    )(dest_flat, dest_flat, ys, base, route, l2g, l2b)


def _block_diag_const(n, blk, val):
    idx = jnp.arange(n) // blk
    return jnp.where(idx[:, None] == idx[None, :], val, 0.0).astype(BF16)


def kernel(x, p, ln_emb_g, ln_emb_b, w_in, mu_shift, w0, w_decay_up, a0, w_iclr_up, w_gate_up, k_k, k_a, r_k, gn_g, gn_b, gmlp_ln_g, gmlp_ln_b, w_spatial, b_spatial, w_out, ln1_g, ln1_b, w_group_router, b_group_router, w_expert_router, b_expert_router, w_exp_gate, w_exp_up, w_exp_down, w_ple_gate, b_ple_gate, w_ple_proj, ln2_g, ln2_b):
    B, S, D = x.shape
    T = B * S
    row = lambda t: t.reshape(1, -1).astype(F32)

    zl = jnp.zeros((DECAY_LORA, D_RWKV), F32)
    wwa = jnp.concatenate([jnp.concatenate([w_decay_up[0], zl], axis=1),
                           jnp.concatenate([zl, w_iclr_up[0]], axis=1)], axis=0).astype(BF16)
    w0a0 = jnp.concatenate([w0[0], a0[0]]).reshape(1, -1)
    eones = jnp.tile(_block_diag_const(2 * LANES, HEAD, 1.0), (2, 1))
    emean = jnp.tile(_block_diag_const(LANES, HEAD, 1.0 / HEAD), (2, 1))

    r, lw, k, v, a, b, g, bonus, yb = _prep(
        x, row(ln_emb_g), row(ln_emb_b), w_in[0].astype(BF16), row(mu_shift[0]), wwa, w0a0,
        w_gate_up[0].astype(BF16), row(k_k[0]), row(k_a[0]), row(r_k[0]), eones,
        row(gmlp_ln_g[0]), row(gmlp_ln_b[0]), w_spatial[0], b_spatial[0].T)

    ya = _wkv(r, lw, k, v, a, b, g, bonus, gn_g[0].reshape(N_PAIRS, 1, LANES), gn_b[0].reshape(N_PAIRS, 1, LANES),
              emean)

    wr = jnp.concatenate([w_expert_router[0].reshape(D, N_EXPERTS), w_group_router[0],
                          jnp.zeros((D, LANES - N_EXPERTS - N_GROUPS), F32)], axis=1)
    wr3 = jnp.stack(_split2(wr))
    br = jnp.concatenate([b_expert_router[0].reshape(-1), b_group_router[0],
                          jnp.zeros((LANES - N_EXPERTS - N_GROUPS,), F32)]).reshape(1, LANES)
    base, x1, route, counts = _mixer(x, row(ln_emb_g), row(ln_emb_b), ya, yb, w_out[0].astype(BF16), row(ln1_g[0]),
                                     row(ln1_b[0]), wr3, br, p[0], w_ple_gate[0].astype(BF16),
                                     row(b_ple_gate[0]), w_ple_proj[0].astype(BF16))
    base = base.reshape(T, D)
    x1 = x1.reshape(T, D // 2)
    route = route.reshape(T, LANES)

    n_blocks = -(-(T * TOP_K) // EXPERT_ROWS) + N_EXPERTS
    dest, pend = _slots(route, counts)
    dest_flat = dest[:, :TOP_K].reshape(T * TOP_K)
    pend = pend[0, :N_EXPERTS]

    xs = _dispatch(pend, dest_flat, x1, n_blocks * EXPERT_ROWS)
    ys = _experts(pend, xs, w_exp_gate[0], w_exp_up[0], w_exp_down[0])
    out = _combine(dest_flat, ys, base, route, row(ln2_g[0]), row(ln2_b[0]))
    return out.reshape(B, S, D)
```

```python
import functools
import math

import jax
import jax.numpy as jnp
from jax import lax
from jax.experimental import pallas as pl
from jax.experimental.pallas import tpu as pltpu

F32 = jnp.float32
BF16 = jnp.bfloat16

D_MODEL = 1024
D_RWKV = 512
HEAD = 64
D_GMLP = 512
GMLP_GROUPS = 4
GROUP_W = 128
GCHUNK = 128
DECAY_LORA = 64
ICLR_LORA = 64
GATE_LORA = 128
N_SHIFT = 3 * D_RWKV + DECAY_LORA + ICLR_LORA + GATE_LORA
D_IN = N_SHIFT + 2 * D_GMLP
D_PLE = 256
N_GROUPS = 4
EXPERTS_PER_GROUP = 8
N_EXPERTS = 32
TOP_K = 2
D_EXPERT = 512
DEPTH = 1
ALPHA = (2.0 * DEPTH) ** 0.25
LN_EPS = 1e-5
GN_EPS = 64e-5
DECAY_SCALE = math.exp(-0.5)

LANES = 128
WKV_CHUNK = 64
N_PAIRS = D_RWKV // LANES
VMEM_LIMIT = 56 * 1024 * 1024

PREP_TM = 512
WKV_TB = 512
WKV_PAIRS = 4
MIX_TM = 512
SLOT_TM = 2048
EXPERT_ROWS = 256
EXPERT_DEPTH = 4
DISPATCH_TM = 512
COMBINE_TM = 256
NEG = -1e30


def _dot(a, b):
    return jnp.dot(a.astype(BF16), b.astype(BF16), preferred_element_type=F32)


def _dot_nt(a, b):
    return lax.dot_general(a.astype(BF16), b.astype(BF16), (((1,), (1,)), ((), ())),
                           preferred_element_type=F32)


def _split3(x):
    hi = x.astype(BF16)
    r1 = x - hi.astype(F32)
    mid = r1.astype(BF16)
    lo = (r1 - mid.astype(F32)).astype(BF16)
    return hi, mid, lo


def _dot3_lhs(x, w):
    hi, mid, lo = _split3(x)
    w = w.astype(BF16)
    return (jnp.dot(hi, w, preferred_element_type=F32) + jnp.dot(mid, w, preferred_element_type=F32)
            + jnp.dot(lo, w, preferred_element_type=F32))


def _split2(x):
    hi = x.astype(BF16)
    return hi, (x - hi.astype(F32)).astype(BF16)


def _dot2_lhs(x, w2):
    hi, lo = _split2(x)
    return jnp.dot(jnp.concatenate([hi, lo], axis=1), w2, preferred_element_type=F32)


def _dot3_rhs(w3, x):
    hi, mid, lo = _split3(x)
    return jnp.dot(w3, jnp.concatenate([hi, mid, lo], axis=0), preferred_element_type=F32)


def _layer_norm(x, g, b, eps):
    mu = jnp.mean(x, axis=-1, keepdims=True)
    xc = x - mu
    var = jnp.mean(xc * xc, axis=-1, keepdims=True)
    return xc * lax.rsqrt(var + eps) * g + b


def _sigmoid(x):
    return 1.0 / (1.0 + jnp.exp(-x))


def _iota(shape, dim):
    return lax.broadcasted_iota(jnp.int32, shape, dim)


def _prep_kernel(x_ref, lng_ref, lnb_ref, win_ref, mu_ref, wwa_ref, w0a0_ref, wg_ref, kk_ref, ka_ref, rk_ref,
                 eones_ref, glng_ref, glnb_ref, wsp_ref, bsp_ref,
                 r_ref, lw_ref, k_ref, v_ref, a_ref, b_ref, g_ref, bonus_ref, yb_ref, carry_ref):
    tm = x_ref.shape[1]

    @pl.when(pl.program_id(1) == 0)
    def _():
        carry_ref[...] = jnp.zeros_like(carry_ref)

    x0 = _layer_norm(x_ref[0], lng_ref[...], lnb_ref[...], LN_EPS)
    proj = jnp.dot(x0.astype(BF16), win_ref[...], preferred_element_type=F32)

    h = proj[:, :N_SHIFT]
    rolled = pltpu.roll(h, 1, 0)
    first = _iota((tm, N_SHIFT), 0) == 0
    prev = jnp.where(first, jnp.broadcast_to(carry_ref[0:1, :], (tm, N_SHIFT)), rolled)
    carry_ref[0:1, :] = h[tm - 1:tm, :]
    h = h + (prev - h) * mu_ref[...]

    r = h[:, 0:D_RWKV]
    k = h[:, D_RWKV:2 * D_RWKV]
    v = h[:, 2 * D_RWKV:3 * D_RWKV]
    xwa = h[:, 3 * D_RWKV:3 * D_RWKV + LANES]
    xg = h[:, 3 * D_RWKV + LANES:N_SHIFT]

    lane = _iota((tm, LANES), 1)
    twa = jnp.where(lane < DECAY_LORA, jnp.tanh(xwa), xwa)
    da = _dot(twa, wwa_ref[...]) + w0a0_ref[...]
    logw = -DECAY_SCALE * _sigmoid(da[:, :D_RWKV])
    ag = _sigmoid(da[:, D_RWKV:])
    g = _dot(_sigmoid(xg), wg_ref[...])

    eones2 = eones_ref[...]

    def head_sum(t):
        half = 2 * LANES
        return jnp.concatenate([_dot2_lhs(t[:, :half], eones2), _dot2_lhs(t[:, half:], eones2)], axis=1)

    kk = k * kk_ref[...]
    kk = kk / jnp.maximum(jnp.sqrt(head_sum(kk * kk)), 1e-12)
    k = k * (1.0 + (ag - 1.0) * ka_ref[...])
    bonus = head_sum(r * k * rk_ref[...]) * v

    for p in range(N_PAIRS):
        sl = slice(p * LANES, (p + 1) * LANES)
        r_ref[0, p] = r[:, sl]
        lw_ref[0, p] = logw[:, sl]
        k_ref[0, p] = k[:, sl]
        v_ref[0, p] = v[:, sl]
        a_ref[0, p] = -kk[:, sl]
        b_ref[0, p] = (kk * ag)[:, sl]
        g_ref[0, p] = g[:, sl]
        bonus_ref[0, p] = bonus[:, sl]

    zin = proj[:, N_SHIFT:]
    z = 0.5 * zin * (1.0 + lax.erf(zin * (0.5 ** 0.5)))
    zu = z[:, :D_GMLP]
    zv = z[:, D_GMLP:]
    causal = _iota((GCHUNK, GCHUNK), 0) >= _iota((GCHUNK, GCHUNK), 1)
    for gi in range(GMLP_GROUPS):
        gs = slice(gi * GROUP_W, (gi + 1) * GROUP_W)
        zvn = _layer_norm(zv[:, gs], glng_ref[:, gs], glnb_ref[:, gs], LN_EPS)
        ws = jnp.where(causal, wsp_ref[gi], 0.0).astype(BF16)
        bcol = bsp_ref[:, gi:gi + 1]
        for c in range(tm // GCHUNK):
            ts = slice(c * GCHUNK, (c + 1) * GCHUNK)
            mixed = jnp.dot(ws, zvn[ts].astype(BF16), preferred_element_type=F32) + bcol
            yb_ref[0, ts, gs] = (zu[ts, gs] * mixed).astype(BF16)


def _prep(x, ln_g, ln_b, w_in, mu, wwa, w0a0, wg, k_k, k_a, r_k, eones, glng, glnb, wsp, bsp):
    B, S, _ = x.shape
    tm = PREP_TM
    const = lambda shape: pl.BlockSpec(shape, lambda b, s: (0,) * len(shape))
    pair_spec = pl.BlockSpec((1, N_PAIRS, tm, LANES), lambda b, s: (b, 0, s, 0))
    pair_shape = jax.ShapeDtypeStruct((B, N_PAIRS, S, LANES), F32)
    return pl.pallas_call(
        _prep_kernel,
        grid=(B, S // tm),
        in_specs=[
            pl.BlockSpec((1, tm, D_MODEL), lambda b, s: (b, s, 0)),
            const((1, D_MODEL)), const((1, D_MODEL)), const((D_MODEL, D_IN)), const((1, N_SHIFT)),
            const((LANES, 2 * D_RWKV)), const((1, 2 * D_RWKV)), const((GATE_LORA, D_RWKV)),
            const((1, D_RWKV)), const((1, D_RWKV)), const((1, D_RWKV)), const((4 * LANES, 2 * LANES)),
            const((1, D_GMLP)), const((1, D_GMLP)), const((GMLP_GROUPS, GCHUNK, GCHUNK)),
            const((GCHUNK, GMLP_GROUPS)),
        ],
        out_specs=[pair_spec] * 8 + [pl.BlockSpec((1, tm, D_GMLP), lambda b, s: (b, s, 0))],
        out_shape=[pair_shape] * 8 + [jax.ShapeDtypeStruct((B, S, D_GMLP), BF16)],
        scratch_shapes=[pltpu.VMEM((8, N_SHIFT), F32)],
        compiler_params=pltpu.CompilerParams(dimension_semantics=("arbitrary", "arbitrary"),
                                             vmem_limit_bytes=VMEM_LIMIT),
        name="prep",
    )(x, ln_g, ln_b, w_in, mu, wwa, w0a0, wg, k_k, k_a, r_k, eones, glng, glnb, wsp, bsp)


def _wkv_kernel(r_ref, lw_ref, k_ref, v_ref, a_ref, b_ref, g_ref, bonus_ref, gng_ref, gnb_ref, emean_ref,
                o_ref, h_ref):
    C = WKV_CHUNK
    tb = r_ref.shape[2]

    @pl.when(pl.program_id(2) == 0)
    def _():
        h_ref[...] = jnp.zeros_like(h_ref)

    tok = _iota((C, LANES), 0)
    lane = _iota((C, LANES), 1)
    head0 = lane < HEAD
    strict = tok > lane % HEAD
    incl = tok >= lane % HEAD
    eye_w = (tok == lane % HEAD).astype(F32)
    rr = _iota((LANES, LANES), 0)
    cc = _iota((LANES, LANES), 1)
    eye = (rr == cc).astype(F32)
    same_head = (rr < HEAD) == (cc < HEAD)
    ltri3 = (_iota((C, 3 * C), 0) >= _iota((C, 3 * C), 1) % C).astype(BF16)

    def stack(x):
        xb = x.astype(BF16)
        zero = jnp.zeros_like(xb)
        return jnp.concatenate([jnp.where(head0, xb, zero), jnp.where(head0, zero, xb)], axis=0)

    def stack2(x, y):
        return jnp.concatenate([stack(x), stack(y)], axis=1)

    n_pairs = r_ref.shape[1]
    n_chunks = tb // C
    units = [(q, c) for q in range(n_pairs) for c in range(n_chunks)]

    def load(ref):
        return [ref[0, q, c * C:(c + 1) * C, :] for q, c in units]

    r_, lw_, k_, v_, a_, b_ = (load(ref) for ref in (r_ref, lw_ref, k_ref, v_ref, a_ref, b_ref))
    cum_ = [_dot3_rhs(ltri3, lw) for lw in lw_]
    cend_ = [cum[C - 1:C, :] for cum in cum_]
    at_ = [a * jnp.exp(cum - lw) for a, cum, lw in zip(a_, cum_, lw_)]
    rt_ = [r * jnp.exp(cum) for r, cum in zip(r_, cum_)]
    ginv_ = [jnp.exp(-cum) for cum in cum_]
    gend_ = [jnp.exp(cend - cum) for cend, cum in zip(cend_, cum_)]
    bk_end_ = [jnp.concatenate([b * ge, k * ge], axis=0) for b, k, ge in zip(b_, k_, gend_)]
    vst_ = [stack(v) for v in v_]

    G_ = [_dot_nt(jnp.concatenate([at, rt], axis=0), jnp.concatenate([stack(b * gi), stack(k * gi)], axis=0))
          for at, rt, b, k, gi in zip(at_, rt_, b_, k_, ginv_)]
    n1_ = [jnp.where(strict, G[:C, :LANES], 0.0) for G in G_]
    aak_ = [jnp.where(strict, G[:C, LANES:], 0.0) for G in G_]
    arb_ = [jnp.where(incl, G[C:, :LANES], 0.0) for G in G_]
    ark_ = [jnp.where(incl, G[C:, LANES:], 0.0) for G in G_]
    av_ = [_dot(jnp.concatenate([aak, ark], axis=0), vst) for aak, ark, vst in zip(aak_, ark_, vst_)]

    s1_ = [stack(n1) for n1 in n1_]
    n2_ = [_dot(n1, s1) for n1, s1 in zip(n1_, s1_)]
    x_ = [_dot(n2, jnp.concatenate([s1, stack(n2)], axis=1)) for n2, s1 in zip(n2_, s1_)]
    t_ = [eye_w + n1 + n2 + x[:, :LANES] for n1, n2, x in zip(n1_, n2_, x_)]
    np_ = [x[:, LANES:] for x in x_]
    for _ in range(3):
        x_ = [_dot(npow, stack2(t, npow)) for t, npow in zip(t_, np_)]
        t_ = [t + x[:, :LANES] for t, x in zip(t_, x_)]
        np_ = [x[:, LANES:] for x in x_]
    t_ = [t + _dot(npow, stack(t)) for t, npow in zip(t_, np_)]

    x_ = [_dot(t, stack2(at, av[:C])) for t, at, av in zip(t_, at_, av_)]
    z_ = [_dot(arb, stack2(x[:, :LANES], x[:, LANES:])) for arb, x in zip(arb_, x_)]
    rp_ = [rt + z[:, :LANES] for rt, z in zip(rt_, z_)]
    p3_ = [z[:, LANES:] + av[C:] for z, av in zip(z_, av_)]
    rhs_ = [jnp.concatenate([x, jnp.concatenate([jnp.zeros_like(v), v], axis=1)], axis=0)
            for x, v in zip(x_, v_)]
    mq_ = [_dot(bk_end.T, rhs) for bk_end, rhs in zip(bk_end_, rhs_)]
    m_ = [eye * jnp.exp(cend) + jnp.where(same_head, mq[:, :LANES], 0.0) for cend, mq in zip(cend_, mq_)]
    q_ = [jnp.where(same_head, mq[:, LANES:], 0.0) for mq in mq_]

    pairs = range(n_pairs)
    H_ = [h_ref[q] for q in pairs]
    ys_ = [[] for _ in pairs]
    for c in range(n_chunks):
        for q in pairs:
            u = q * n_chunks + c
            ys_[q].append(_dot(rp_[u], H_[q]) + p3_[u])
        H_ = [_dot(m_[q * n_chunks + c], H_[q]) + q_[q * n_chunks + c] for q in pairs]
    for q in pairs:
        h_ref[q] = H_[q]

    emean = emean_ref[...]
    y_ = [jnp.concatenate(ys, axis=0) for ys in ys_]
    mu_ = [_dot2_lhs(y, emean) for y in y_]
    yc_ = [y - mu for y, mu in zip(y_, mu_)]
    var_ = [_dot2_lhs(yc * yc, emean) for yc in yc_]
    for q in pairs:
        yn = yc_[q] * lax.rsqrt(var_[q] + GN_EPS) * gng_ref[q] + gnb_ref[q]
        o_ref[0, q] = ((yn + bonus_ref[0, q]) * g_ref[0, q]).astype(BF16)


def _wkv(r, lw, k, v, a, b, g, bonus, gn_g, gn_b, emean):
    B, P, S, _ = r.shape
    tb = WKV_TB
    pp = WKV_PAIRS
    seq = pl.BlockSpec((1, pp, tb, LANES), lambda bi, p, s: (bi, p, s, 0))
    par = pl.BlockSpec((pp, 1, LANES), lambda bi, p, s: (p, 0, 0))
    return pl.pallas_call(
        _wkv_kernel,
        grid=(B, P // pp, S // tb),
        in_specs=[seq] * 8 + [par, par, pl.BlockSpec((2 * LANES, LANES), lambda bi, p, s: (0, 0))],
        out_specs=seq,
        out_shape=jax.ShapeDtypeStruct((B, P, S, LANES), BF16),
        scratch_shapes=[pltpu.VMEM((pp, LANES, LANES), F32)],
        compiler_params=pltpu.CompilerParams(dimension_semantics=("arbitrary", "arbitrary", "arbitrary"),
                                             vmem_limit_bytes=VMEM_LIMIT),
        name="wkv",
    )(r, lw, k, v, a, b, g, bonus, gn_g, gn_b, emean)


def _mixer_kernel(x_ref, lng_ref, lnb_ref, ya_ref, yb_ref, wout_ref, l1g_ref, l1b_ref, wr_ref, br_ref,
                  base_ref, x1_ref, route_ref, counts_ref, carry_ref):
    tm = x_ref.shape[1]

    @pl.when((pl.program_id(0) == 0) & (pl.program_id(1) == 0))
    def _():
        carry_ref[...] = jnp.zeros_like(carry_ref)

    x0 = _layer_norm(x_ref[0], lng_ref[...], lnb_ref[...], LN_EPS)
    ymix = jnp.concatenate([ya_ref[0, p] for p in range(N_PAIRS)] + [yb_ref[0]], axis=-1)
    mix = jnp.dot(ymix, wout_ref[...], preferred_element_type=F32)
    x1 = _layer_norm(ALPHA * x0 + mix, l1g_ref[...], l1b_ref[...], LN_EPS)
    x1b = x1.astype(BF16)
    half = D_MODEL // 2
    lo_bits = lax.bitcast_convert_type(x1b[:, :half].astype(F32), jnp.uint32)
    hi_bits = lax.bitcast_convert_type(x1b[:, half:].astype(F32), jnp.uint32)
    x1_ref[0] = (hi_bits & jnp.uint32(0xFFFF0000)) | (lo_bits >> 16)

    hi, mid = _split2(x1)
    whi = wr_ref[0]
    wmid = wr_ref[1]
    d = lambda u, w: jnp.dot(u, w, preferred_element_type=F32)
    logits = (d(hi, whi) + d(hi, wmid) + d(mid, whi)) + br_ref[...]
    lane = _iota((tm, LANES), 1).astype(F32)
    far = float(4 * LANES)
    is_g = jnp.where(lane >= N_EXPERTS, jnp.where(lane < N_EXPERTS + N_GROUPS, 1.0, 0.0), 0.0) > 0.5
    gl = jnp.where(is_g, logits, NEG)
    gmax = jnp.max(gl, axis=-1, keepdims=True)
    gsel = jnp.min(jnp.where(gl == gmax, lane, far), axis=-1, keepdims=True) - N_EXPERTS
    p_group = 1.0 / jnp.sum(jnp.where(is_g, jnp.exp(gl - gmax), 0.0), axis=-1, keepdims=True)
    grp_of_lane = jnp.floor(lane * (1.0 / EXPERTS_PER_GROUP))
    el = jnp.where(grp_of_lane == gsel, logits, NEG)
    v1 = jnp.max(el, axis=-1, keepdims=True)
    i1 = jnp.min(jnp.where(el == v1, lane, far), axis=-1, keepdims=True)
    el2 = jnp.where(lane == i1, NEG, el)
    v2 = jnp.max(el2, axis=-1, keepdims=True)
    i2 = jnp.min(jnp.where(el2 == v2, lane, far), axis=-1, keepdims=True)
    e21 = jnp.exp(v2 - v1)
    w1 = p_group / (1.0 + e21)
    w2 = p_group * e21 / (1.0 + e21)

    oh1 = lane == i1
    oh2 = lane == i2
    below = (_iota((tm, tm), 0) > _iota((tm, tm), 1)).astype(BF16)
    o1 = jnp.where(oh1, 1.0, 0.0)
    o2 = jnp.where(oh2, 1.0, 0.0)
    c1 = jnp.dot(below, o1.astype(BF16), preferred_element_type=F32)
    c2 = jnp.dot(below, o2.astype(BF16), preferred_element_type=F32)
    tot1 = jnp.sum(o1, axis=0, keepdims=True)
    carry = carry_ref[0:1, :]
    rank1 = jnp.sum(jnp.where(oh1, c1 + carry, 0.0), axis=-1, keepdims=True)
    rank2 = jnp.sum(jnp.where(oh2, c2 + carry + tot1, 0.0), axis=-1, keepdims=True)
    carry = carry + tot1 + jnp.sum(o2, axis=0, keepdims=True)
    carry_ref[0:1, :] = carry
    counts_ref[...] = jnp.broadcast_to(carry, counts_ref.shape)

    fields = (i1, i2, w1, w2, rank1, rank2)
    route = jnp.zeros((tm, LANES), F32)
    for n, f in enumerate(fields):
        route = jnp.where(lane == n, f, route)
    route_ref[0] = route

    base_ref[0] = ALPHA * x1


def _mixer(x, ln_g, ln_b, ya, yb, w_out, l1g, l1b, wr3, br):
    B, S, _ = x.shape
    tm = MIX_TM
    const = lambda shape: pl.BlockSpec(shape, lambda b, s: (0,) * len(shape))
    row = lambda w: pl.BlockSpec((1, tm, w), lambda b, s: (b, s, 0))
    return pl.pallas_call(
        _mixer_kernel,
        grid=(B, S // tm),
        in_specs=[
            row(D_MODEL), const((1, D_MODEL)), const((1, D_MODEL)),
            pl.BlockSpec((1, N_PAIRS, tm, LANES), lambda b, s: (b, 0, s, 0)), row(D_GMLP),
            const((D_MODEL, D_MODEL)), const((1, D_MODEL)), const((1, D_MODEL)),
            const((2, D_MODEL, LANES)), const((1, LANES)),
        ],
        out_specs=[row(D_MODEL), row(D_MODEL // 2), row(LANES), const((8, LANES))],
        out_shape=[jax.ShapeDtypeStruct((B, S, D_MODEL), F32), jax.ShapeDtypeStruct((B, S, D_MODEL // 2), jnp.uint32),
                   jax.ShapeDtypeStruct((B, S, LANES), F32), jax.ShapeDtypeStruct((8, LANES), F32)],
        scratch_shapes=[pltpu.VMEM((8, LANES), F32)],
        compiler_params=pltpu.CompilerParams(dimension_semantics=("arbitrary", "arbitrary"),
                                             vmem_limit_bytes=VMEM_LIMIT),
        name="mixer",
    )(x, ln_g, ln_b, ya, yb, w_out, l1g, l1b, wr3, br)


def _slots_kernel(route_ref, counts_ref, dest_ref, pend_ref):
    tm = route_ref.shape[0]
    lane = _iota((tm, LANES), 1)
    route = route_ref[...]
    oh1 = lane == route[:, 0:1].astype(jnp.int32)
    oh2 = lane == route[:, 1:2].astype(jnp.int32)

    counts = counts_ref[0:1, :]
    padded = jnp.floor((counts + (EXPERT_ROWS - 1)) * (1.0 / EXPERT_ROWS)) * EXPERT_ROWS
    upper = (_iota((LANES, LANES), 0) <= _iota((LANES, LANES), 1)).astype(BF16)
    pend = _dot3_lhs(jnp.broadcast_to(padded, (8, LANES)), upper)[0:1, :]
    pstart = pend - padded
    d1 = jnp.sum(jnp.where(oh1, pstart, 0.0), axis=-1, keepdims=True) + route[:, 4:5]
    d2 = jnp.sum(jnp.where(oh2, pstart, 0.0), axis=-1, keepdims=True) + route[:, 5:6]
    dest_ref[...] = jnp.where(lane == 0, d1, jnp.where(lane == 1, d2, 0.0)).astype(jnp.int32)
    pend_ref[...] = jnp.broadcast_to(pend, (8, LANES)).astype(jnp.int32)


def _slots(route, counts):
    T = route.shape[0]
    tm = SLOT_TM
    return pl.pallas_call(
        _slots_kernel,
        grid=(T // tm,),
        in_specs=[pl.BlockSpec((tm, LANES), lambda i: (i, 0)), pl.BlockSpec((8, LANES), lambda i: (0, 0))],
        out_specs=[pl.BlockSpec((tm, LANES), lambda i: (i, 0)),
                   pl.BlockSpec((8, LANES), lambda i: (0, 0))],
        out_shape=[jax.ShapeDtypeStruct((T, LANES), jnp.int32), jax.ShapeDtypeStruct((8, LANES), jnp.int32)],
        compiler_params=pltpu.CompilerParams(dimension_semantics=("arbitrary",), vmem_limit_bytes=VMEM_LIMIT),
        name="slots",
    )(route, counts)


def _dispatch_kernel(pend_ref, dest_ref, x_ref, xs_ref, zero_ref, sem, zsem):
    tm = dest_ref.shape[0] // TOP_K

    @pl.when(pl.program_id(0) == 0)
    def _():
        zero_ref[...] = jnp.zeros_like(zero_ref)

        def tail(e):
            start = pl.multiple_of(jnp.maximum(pend_ref[e] - EXPERT_ROWS, 0), EXPERT_ROWS)
            return pltpu.make_async_copy(zero_ref, xs_ref.at[pl.ds(start, EXPERT_ROWS)], zsem)

        def unused(j):
            return pltpu.make_async_copy(
                zero_ref, xs_ref.at[pl.ds(pl.multiple_of(j * EXPERT_ROWS, EXPERT_ROWS), EXPERT_ROWS)], zsem)

        def start_unused(j, _):
            unused(j).start()
            return 0

        def wait_unused(j, _):
            unused(j).wait()
            return 0

        first_unused = pend_ref[N_EXPERTS - 1] // EXPERT_ROWS
        n_blocks = xs_ref.shape[0] // EXPERT_ROWS
        for e in range(N_EXPERTS):
            tail(e).start()
        lax.fori_loop(first_unused, n_blocks, start_unused, 0)
        for e in range(N_EXPERTS):
            tail(e).wait()
        lax.fori_loop(first_unused, n_blocks, wait_unused, 0)

    for t in range(tm):
        for j in range(TOP_K):
            pltpu.make_async_copy(x_ref.at[pl.ds(t, 1)], xs_ref.at[pl.ds(dest_ref[TOP_K * t + j], 1)],
                                  sem).start(priority=j)
    for j in range(TOP_K):
        pltpu.make_async_copy(x_ref, xs_ref.at[pl.ds(0, tm)], sem).wait()


def _dispatch(pend, dest_flat, x1, n_rows):
    T, width = x1.shape
    tm = DISPATCH_TM
    return pl.pallas_call(
        _dispatch_kernel,
        grid_spec=pltpu.PrefetchScalarGridSpec(
            num_scalar_prefetch=1,
            grid=(T // tm,),
            in_specs=[pl.BlockSpec((TOP_K * tm,), lambda i, pe: (i,), memory_space=pltpu.SMEM),
                      pl.BlockSpec((tm, width), lambda i, pe: (i, 0))],
            out_specs=pl.BlockSpec(memory_space=pl.ANY),
            scratch_shapes=[pltpu.VMEM((EXPERT_ROWS, width), x1.dtype), pltpu.SemaphoreType.DMA,
                            pltpu.SemaphoreType.DMA],
        ),
        out_shape=jax.ShapeDtypeStruct((n_rows, width), x1.dtype),
        compiler_params=pltpu.CompilerParams(dimension_semantics=("arbitrary",), vmem_limit_bytes=VMEM_LIMIT),
        name="dispatch",
    )(pend, dest_flat, x1)


def _experts_kernel(pend_ref, xs_ref, wg_ref, wu_ref, wd_ref, ys_ref, xbuf_ref, ybuf_ref, wgu_ref, wdb_ref,
                    in_sem, out_sem):
    rows = EXPERT_ROWS
    e = pl.program_id(0)
    first = jnp.where(e == 0, 0, pend_ref[jnp.maximum(e - 1, 0)]) // rows
    last = pend_ref[e] // rows
    n_used = pend_ref[N_EXPERTS - 1] // rows

    def block_rows(ref, b):
        return ref.at[pl.ds(pl.multiple_of(b * rows, rows), rows)]

    depth = xbuf_ref.shape[0]
    row_priority = 1

    def x_copy(b):
        slot = b % depth
        return pltpu.make_async_copy(block_rows(xs_ref, b), xbuf_ref.at[slot], in_sem.at[slot])

    def y_copy(b):
        slot = b % depth
        return pltpu.make_async_copy(ybuf_ref.at[slot], block_rows(ys_ref, b), out_sem.at[slot])

    @pl.when(e == 0)
    def _():
        for ahead in range(depth - 1):
            @pl.when(ahead < n_used)
            def _():
                x_copy(ahead).start(priority=row_priority)

    @pl.when(last > first)
    def _():
        wgu_ref[:, :D_EXPERT] = wg_ref[0].astype(BF16)
        wgu_ref[:, D_EXPERT:] = wu_ref[0].astype(BF16)
        wdb_ref[...] = wd_ref[0].astype(BF16)

        def body(b, _):
            slot = b % depth

            @pl.when(b + depth - 1 < n_used)
            def _():
                x_copy(b + depth - 1).start(priority=row_priority)

            x_copy(b).wait()

            @pl.when(b >= depth)
            def _():
                y_copy(b - depth).wait()

            xw = xbuf_ref[slot]
            x_lo = lax.bitcast_convert_type(xw << 16, F32)
            x_hi = lax.bitcast_convert_type(xw & jnp.uint32(0xFFFF0000), F32)
            xb = jnp.concatenate([x_lo, x_hi], axis=1).astype(BF16)
            h = jnp.dot(xb, wgu_ref[...], preferred_element_type=F32)
            hg = h[:, :D_EXPERT]
            hid = hg * _sigmoid(hg) * h[:, D_EXPERT:]
            ybuf_ref[slot] = jnp.dot(hid.astype(BF16), wdb_ref[...], preferred_element_type=F32)
            y_copy(b).start(priority=row_priority)
            return 0

        lax.fori_loop(first, last, body, 0)

    @pl.when(e == N_EXPERTS - 1)
    def _():
        for back in range(depth, 0, -1):
            @pl.when(n_used >= back)
            def _():
                y_copy(n_used - back).wait()

        ybuf_ref[0] = jnp.zeros(ybuf_ref.shape[1:], F32)

        def unused(b):
            return pltpu.make_async_copy(ybuf_ref.at[0], block_rows(ys_ref, b), out_sem.at[0])

        def start_unused(b, _):
            unused(b).start()
            return 0

        def wait_unused(b, _):
            unused(b).wait()
            return 0

        n_blocks = ys_ref.shape[0] // rows
        lax.fori_loop(n_used, n_blocks, start_unused, 0)
        lax.fori_loop(n_used, n_blocks, wait_unused, 0)


def _experts(pend, xs, wg, wu, wd):
    n_rows = xs.shape[0]
    rows = EXPERT_ROWS
    wspec = lambda shape: pl.BlockSpec((1,) + shape, lambda e, pe: (e, 0, 0))
    return pl.pallas_call(
        _experts_kernel,
        grid_spec=pltpu.PrefetchScalarGridSpec(
            num_scalar_prefetch=1,
            grid=(N_EXPERTS,),
            in_specs=[pl.BlockSpec(memory_space=pl.ANY),
                      wspec((D_MODEL, D_EXPERT)), wspec((D_MODEL, D_EXPERT)), wspec((D_EXPERT, D_MODEL))],
            out_specs=pl.BlockSpec(memory_space=pl.ANY),
            scratch_shapes=[pltpu.VMEM((EXPERT_DEPTH, rows, D_MODEL // 2), jnp.uint32),
                            pltpu.VMEM((EXPERT_DEPTH, rows, D_MODEL), F32),
                            pltpu.VMEM((D_MODEL, 2 * D_EXPERT), BF16), pltpu.VMEM((D_EXPERT, D_MODEL), BF16),
                            pltpu.SemaphoreType.DMA((EXPERT_DEPTH,)), pltpu.SemaphoreType.DMA((EXPERT_DEPTH,))],
        ),
        out_shape=jax.ShapeDtypeStruct((n_rows, D_MODEL), F32),
        compiler_params=pltpu.CompilerParams(dimension_semantics=("arbitrary",), vmem_limit_bytes=VMEM_LIMIT),
        name="experts",
    )(pend, xs, wg, wu, wd)


def _combine_kernel(dest_ref, dest_next_ref, ys_ref, base_ref, route_ref, x1_ref, p_ref, wpg_ref, bpg_ref, wpp_ref,
                    lg_ref, lb_ref, o_ref, buf_ref, sem):
    tm = buf_ref.shape[2]
    i = pl.program_id(0)

    def gather(dref, offset, s):
        for t in range(tm):
            for j in range(TOP_K):
                pltpu.make_async_copy(ys_ref.at[pl.ds(dref[offset + TOP_K * t + j], 1)],
                                      buf_ref.at[s, j, pl.ds(t, 1)], sem.at[s]).start(priority=j)

    def drain(s):
        for j in range(TOP_K):
            pltpu.make_async_copy(ys_ref.at[pl.ds(0, tm)], buf_ref.at[s, j], sem.at[s]).wait()

    def ple(s):
        rows = slice(s * tm, (s + 1) * tm)
        xw = x1_ref[rows, :]
        x_lo = lax.bitcast_convert_type(xw << 16, F32)
        x_hi = lax.bitcast_convert_type(xw & jnp.uint32(0xFFFF0000), F32)
        x1b = jnp.concatenate([x_lo, x_hi], axis=1).astype(BF16)
        gate = _sigmoid(jnp.dot(x1b, wpg_ref[...], preferred_element_type=F32) + bpg_ref[...])
        return gate * jnp.dot(p_ref[rows, :].astype(BF16), wpp_ref[...], preferred_element_type=F32)

    def finish(s, ple_s):
        rows = slice(s * tm, (s + 1) * tm)
        drain(s)
        route = route_ref[rows, :]
        ffn = buf_ref[s, 0] * route[:, 2:3] + buf_ref[s, 1] * route[:, 3:4]
        o_ref[rows, :] = _layer_norm(base_ref[rows, :] + ffn + ple_s, lg_ref[...], lb_ref[...], LN_EPS)

    @pl.when(i == 0)
    def _():
        gather(dest_ref, 0, 0)

    ple0 = ple(0)
    gather(dest_ref, TOP_K * tm, 1)
    finish(0, ple0)
    ple1 = ple(1)
    gather(dest_next_ref, 0, 0)
    finish(1, ple1)

    @pl.when(i == pl.num_programs(0) - 1)
    def _():
        drain(0)


def _combine(dest_flat, ys, base, route, x1, p, wpg, bpg, wpp, l2g, l2b):
    T = base.shape[0]
    tm = COMBINE_TM
    nt = T // tm
    tile = lambda w: pl.BlockSpec((2 * tm, w), lambda i: (i, 0))
    const = lambda shape: pl.BlockSpec(shape, lambda i: (0,) * len(shape))
    return pl.pallas_call(
        _combine_kernel,
        grid=(nt // 2,),
        in_specs=[pl.BlockSpec((2 * TOP_K * tm,), lambda i: (i,), memory_space=pltpu.SMEM),
                  pl.BlockSpec((TOP_K * tm,), lambda i: (jnp.minimum(2 * i + 2, nt - 1),), memory_space=pltpu.SMEM),
                  pl.BlockSpec(memory_space=pl.ANY),
                  tile(D_MODEL), tile(LANES), tile(D_MODEL // 2), tile(D_PLE),
                  const((D_MODEL, D_MODEL)), const((1, D_MODEL)), const((D_PLE, D_MODEL)),
                  const((1, D_MODEL)), const((1, D_MODEL))],
        out_specs=pl.BlockSpec((2 * tm, D_MODEL), lambda i: (i, 0)),
        out_shape=jax.ShapeDtypeStruct((T, D_MODEL), F32),
        scratch_shapes=[pltpu.VMEM((2, TOP_K, tm, D_MODEL), F32), pltpu.SemaphoreType.DMA((2,))],
        compiler_params=pltpu.CompilerParams(dimension_semantics=("arbitrary",), vmem_limit_bytes=VMEM_LIMIT),
        name="combine",
    )(dest_flat, dest_flat, ys, base, route, x1, p, wpg, bpg, wpp, l2g, l2b)


def _block_diag_const(n, blk, val):
    idx = jnp.arange(n) // blk
    return jnp.where(idx[:, None] == idx[None, :], val, 0.0).astype(BF16)


def kernel(x, p, ln_emb_g, ln_emb_b, w_in, mu_shift, w0, w_decay_up, a0, w_iclr_up, w_gate_up, k_k, k_a, r_k, gn_g, gn_b, gmlp_ln_g, gmlp_ln_b, w_spatial, b_spatial, w_out, ln1_g, ln1_b, w_group_router, b_group_router, w_expert_router, b_expert_router, w_exp_gate, w_exp_up, w_exp_down, w_ple_gate, b_ple_gate, w_ple_proj, ln2_g, ln2_b):
    B, S, D = x.shape
    T = B * S
    row = lambda t: t.reshape(1, -1).astype(F32)

    zl = jnp.zeros((DECAY_LORA, D_RWKV), F32)
    wwa = jnp.concatenate([jnp.concatenate([w_decay_up[0], zl], axis=1),
                           jnp.concatenate([zl, w_iclr_up[0]], axis=1)], axis=0).astype(BF16)
    w0a0 = jnp.concatenate([w0[0], a0[0]]).reshape(1, -1)
    eones = jnp.tile(_block_diag_const(2 * LANES, HEAD, 1.0), (2, 1))
    emean = jnp.tile(_block_diag_const(LANES, HEAD, 1.0 / HEAD), (2, 1))

    r, lw, k, v, a, b, g, bonus, yb = _prep(
        x, row(ln_emb_g), row(ln_emb_b), w_in[0].astype(BF16), row(mu_shift[0]), wwa, w0a0,
        w_gate_up[0].astype(BF16), row(k_k[0]), row(k_a[0]), row(r_k[0]), eones,
        row(gmlp_ln_g[0]), row(gmlp_ln_b[0]), w_spatial[0], b_spatial[0].T)

    ya = _wkv(r, lw, k, v, a, b, g, bonus, gn_g[0].reshape(N_PAIRS, 1, LANES), gn_b[0].reshape(N_PAIRS, 1, LANES),
              emean)

    wr = jnp.concatenate([w_expert_router[0].reshape(D, N_EXPERTS), w_group_router[0],
                          jnp.zeros((D, LANES - N_EXPERTS - N_GROUPS), F32)], axis=1)
    wr3 = jnp.stack(_split2(wr))
    br = jnp.concatenate([b_expert_router[0].reshape(-1), b_group_router[0],
                          jnp.zeros((LANES - N_EXPERTS - N_GROUPS,), F32)]).reshape(1, LANES)
    base, x1, route, counts = _mixer(x, row(ln_emb_g), row(ln_emb_b), ya, yb, w_out[0].astype(BF16), row(ln1_g[0]),
                                     row(ln1_b[0]), wr3, br)
    base = base.reshape(T, D)
    x1 = x1.reshape(T, D // 2)
    route = route.reshape(T, LANES)

    n_blocks = -(-(T * TOP_K) // EXPERT_ROWS) + N_EXPERTS
    dest, pend = _slots(route, counts)
    dest_flat = dest[:, :TOP_K].reshape(T * TOP_K)
    pend = pend[0, :N_EXPERTS]

    xs = _dispatch(pend, dest_flat, x1, n_blocks * EXPERT_ROWS)
    ys = _experts(pend, xs, w_exp_gate[0], w_exp_up[0], w_exp_down[0])
    out = _combine(dest_flat, ys, base, route, x1, p[0].reshape(T, D_PLE), w_ple_gate[0].astype(BF16),
                   row(b_ple_gate[0]), w_ple_proj[0].astype(BF16), row(ln2_g[0]), row(ln2_b[0]))
    return out.reshape(B, S, D)
```

```python
import functools
import math

import jax
import jax.numpy as jnp
from jax import lax
from jax.experimental import pallas as pl
from jax.experimental.pallas import tpu as pltpu

F32 = jnp.float32
BF16 = jnp.bfloat16

D_MODEL = 1024
D_RWKV = 512
HEAD = 64
D_GMLP = 512
GMLP_GROUPS = 4
GROUP_W = 128
GCHUNK = 128
DECAY_LORA = 64
ICLR_LORA = 64
GATE_LORA = 128
N_SHIFT = 3 * D_RWKV + DECAY_LORA + ICLR_LORA + GATE_LORA
D_IN = N_SHIFT + 2 * D_GMLP
D_PLE = 256
N_GROUPS = 4
EXPERTS_PER_GROUP = 8
N_EXPERTS = 32
TOP_K = 2
D_EXPERT = 512
DEPTH = 1
ALPHA = (2.0 * DEPTH) ** 0.25
LN_EPS = 1e-5
GN_EPS = 64e-5
DECAY_SCALE = math.exp(-0.5)

LANES = 128
WKV_CHUNK = 64
N_PAIRS = D_RWKV // LANES
VMEM_LIMIT = 56 * 1024 * 1024

PREP_TM = 512
WKV_TB = 512
WKV_PAIRS = 4
MIX_TM = 512
SLOT_TM = 2048
EXPERT_ROWS = 256
EXPERT_DEPTH = 4
DISPATCH_TM = 512
COMBINE_TM = 256
NEG = -1e30


def _dot(a, b):
    return jnp.dot(a.astype(BF16), b.astype(BF16), preferred_element_type=F32)


def _dot_nt(a, b):
    return lax.dot_general(a.astype(BF16), b.astype(BF16), (((1,), (1,)), ((), ())),
                           preferred_element_type=F32)


def _split3(x):
    hi = x.astype(BF16)
    r1 = x - hi.astype(F32)
    mid = r1.astype(BF16)
    lo = (r1 - mid.astype(F32)).astype(BF16)
    return hi, mid, lo


def _dot3_lhs(x, w):
    hi, mid, lo = _split3(x)
    w = w.astype(BF16)
    return (jnp.dot(hi, w, preferred_element_type=F32) + jnp.dot(mid, w, preferred_element_type=F32)
            + jnp.dot(lo, w, preferred_element_type=F32))


def _split2(x):
    hi = x.astype(BF16)
    return hi, (x - hi.astype(F32)).astype(BF16)


def _dot2_lhs(x, w2):
    hi, lo = _split2(x)
    return jnp.dot(jnp.concatenate([hi, lo], axis=1), w2, preferred_element_type=F32)


def _dot3_rhs(w3, x):
    hi, mid, lo = _split3(x)
    return jnp.dot(w3, jnp.concatenate([hi, mid, lo], axis=0), preferred_element_type=F32)


def _layer_norm(x, g, b, eps):
    mu = jnp.mean(x, axis=-1, keepdims=True)
    xc = x - mu
    var = jnp.mean(xc * xc, axis=-1, keepdims=True)
    return xc * lax.rsqrt(var + eps) * g + b


def _sigmoid(x):
    return 1.0 / (1.0 + jnp.exp(-x))


def _iota(shape, dim):
    return lax.broadcasted_iota(jnp.int32, shape, dim)


def _prep_kernel(x_ref, lng_ref, lnb_ref, win_ref, mu_ref, wwa_ref, w0a0_ref, wg_ref, kk_ref, ka_ref, rk_ref,
                 eones_ref, glng_ref, glnb_ref, wsp_ref, bsp_ref,
                 r_ref, lw_ref, k_ref, v_ref, a_ref, b_ref, g_ref, bonus_ref, yb_ref, carry_ref):
    tm = x_ref.shape[1]

    @pl.when(pl.program_id(1) == 0)
    def _():
        carry_ref[...] = jnp.zeros_like(carry_ref)

    x0 = _layer_norm(x_ref[0], lng_ref[...], lnb_ref[...], LN_EPS)
    proj = jnp.dot(x0.astype(BF16), win_ref[...], preferred_element_type=F32)

    h = proj[:, :N_SHIFT]
    rolled = pltpu.roll(h, 1, 0)
    first = _iota((tm, N_SHIFT), 0) == 0
    prev = jnp.where(first, jnp.broadcast_to(carry_ref[0:1, :], (tm, N_SHIFT)), rolled)
    carry_ref[0:1, :] = h[tm - 1:tm, :]
    h = h + (prev - h) * mu_ref[...]

    r = h[:, 0:D_RWKV]
    k = h[:, D_RWKV:2 * D_RWKV]
    v = h[:, 2 * D_RWKV:3 * D_RWKV]
    xwa = h[:, 3 * D_RWKV:3 * D_RWKV + LANES]
    xg = h[:, 3 * D_RWKV + LANES:N_SHIFT]

    lane = _iota((tm, LANES), 1)
    twa = jnp.where(lane < DECAY_LORA, jnp.tanh(xwa), xwa)
    da = _dot(twa, wwa_ref[...]) + w0a0_ref[...]
    logw = -DECAY_SCALE * _sigmoid(da[:, :D_RWKV])
    ag = _sigmoid(da[:, D_RWKV:])
    g = _dot(_sigmoid(xg), wg_ref[...])

    eones2 = eones_ref[...]

    def head_sum(t):
        half = 2 * LANES
        return jnp.concatenate([_dot2_lhs(t[:, :half], eones2), _dot2_lhs(t[:, half:], eones2)], axis=1)

    kk = k * kk_ref[...]
    kk = kk / jnp.maximum(jnp.sqrt(head_sum(kk * kk)), 1e-12)
    k = k * (1.0 + (ag - 1.0) * ka_ref[...])
    bonus = head_sum(r * k * rk_ref[...]) * v

    for p in range(N_PAIRS):
        sl = slice(p * LANES, (p + 1) * LANES)
        r_ref[0, p] = r[:, sl]
        lw_ref[0, p] = logw[:, sl]
        k_ref[0, p] = k[:, sl]
        v_ref[0, p] = v[:, sl]
        a_ref[0, p] = -kk[:, sl]
        b_ref[0, p] = (kk * ag)[:, sl]
        g_ref[0, p] = g[:, sl]
        bonus_ref[0, p] = bonus[:, sl]

    zin = proj[:, N_SHIFT:]
    z = 0.5 * zin * (1.0 + lax.erf(zin * (0.5 ** 0.5)))
    zu = z[:, :D_GMLP]
    zv = z[:, D_GMLP:]
    causal = _iota((GCHUNK, GCHUNK), 0) >= _iota((GCHUNK, GCHUNK), 1)
    for gi in range(GMLP_GROUPS):
        gs = slice(gi * GROUP_W, (gi + 1) * GROUP_W)
        zvn = _layer_norm(zv[:, gs], glng_ref[:, gs], glnb_ref[:, gs], LN_EPS)
        ws = jnp.where(causal, wsp_ref[gi], 0.0).astype(BF16)
        bcol = bsp_ref[:, gi:gi + 1]
        for c in range(tm // GCHUNK):
            ts = slice(c * GCHUNK, (c + 1) * GCHUNK)
            mixed = jnp.dot(ws, zvn[ts].astype(BF16), preferred_element_type=F32) + bcol
            yb_ref[0, ts, gs] = (zu[ts, gs] * mixed).astype(BF16)


def _prep(x, ln_g, ln_b, w_in, mu, wwa, w0a0, wg, k_k, k_a, r_k, eones, glng, glnb, wsp, bsp):
    B, S, _ = x.shape
    tm = PREP_TM
    const = lambda shape: pl.BlockSpec(shape, lambda b, s: (0,) * len(shape))
    pair_spec = pl.BlockSpec((1, N_PAIRS, tm, LANES), lambda b, s: (b, 0, s, 0))
    pair_shape = jax.ShapeDtypeStruct((B, N_PAIRS, S, LANES), F32)
    return pl.pallas_call(
        _prep_kernel,
        grid=(B, S // tm),
        in_specs=[
            pl.BlockSpec((1, tm, D_MODEL), lambda b, s: (b, s, 0)),
            const((1, D_MODEL)), const((1, D_MODEL)), const((D_MODEL, D_IN)), const((1, N_SHIFT)),
            const((LANES, 2 * D_RWKV)), const((1, 2 * D_RWKV)), const((GATE_LORA, D_RWKV)),
            const((1, D_RWKV)), const((1, D_RWKV)), const((1, D_RWKV)), const((4 * LANES, 2 * LANES)),
            const((1, D_GMLP)), const((1, D_GMLP)), const((GMLP_GROUPS, GCHUNK, GCHUNK)),
            const((GCHUNK, GMLP_GROUPS)),
        ],
        out_specs=[pair_spec] * 8 + [pl.BlockSpec((1, tm, D_GMLP), lambda b, s: (b, s, 0))],
        out_shape=[pair_shape] * 8 + [jax.ShapeDtypeStruct((B, S, D_GMLP), BF16)],
        scratch_shapes=[pltpu.VMEM((8, N_SHIFT), F32)],
        compiler_params=pltpu.CompilerParams(dimension_semantics=("arbitrary", "arbitrary"),
                                             vmem_limit_bytes=VMEM_LIMIT),
        name="prep",
    )(x, ln_g, ln_b, w_in, mu, wwa, w0a0, wg, k_k, k_a, r_k, eones, glng, glnb, wsp, bsp)


def _wkv_kernel(r_ref, lw_ref, k_ref, v_ref, a_ref, b_ref, g_ref, bonus_ref, gng_ref, gnb_ref, emean_ref,
                o_ref, h_ref):
    C = WKV_CHUNK
    tb = r_ref.shape[2]

    @pl.when(pl.program_id(2) == 0)
    def _():
        h_ref[...] = jnp.zeros_like(h_ref)

    tok = _iota((C, LANES), 0)
    lane = _iota((C, LANES), 1)
    head0 = lane < HEAD
    strict = tok > lane % HEAD
    incl = tok >= lane % HEAD
    eye_w = (tok == lane % HEAD).astype(F32)
    rr = _iota((LANES, LANES), 0)
    cc = _iota((LANES, LANES), 1)
    eye = (rr == cc).astype(F32)
    same_head = (rr < HEAD) == (cc < HEAD)
    ltri3 = (_iota((C, 3 * C), 0) >= _iota((C, 3 * C), 1) % C).astype(BF16)

    def stack(x):
        xb = x.astype(BF16)
        zero = jnp.zeros_like(xb)
        return jnp.concatenate([jnp.where(head0, xb, zero), jnp.where(head0, zero, xb)], axis=0)

    def stack2(x, y):
        return jnp.concatenate([stack(x), stack(y)], axis=1)

    n_pairs = r_ref.shape[1]
    n_chunks = tb // C
    units = [(q, c) for q in range(n_pairs) for c in range(n_chunks)]

    def load(ref):
        return [ref[0, q, c * C:(c + 1) * C, :] for q, c in units]

    r_, lw_, k_, v_, a_, b_ = (load(ref) for ref in (r_ref, lw_ref, k_ref, v_ref, a_ref, b_ref))
    cum_ = [_dot3_rhs(ltri3, lw) for lw in lw_]
    cend_ = [cum[C - 1:C, :] for cum in cum_]
    at_ = [a * jnp.exp(cum - lw) for a, cum, lw in zip(a_, cum_, lw_)]
    rt_ = [r * jnp.exp(cum) for r, cum in zip(r_, cum_)]
    ginv_ = [jnp.exp(-cum) for cum in cum_]
    gend_ = [jnp.exp(cend - cum) for cend, cum in zip(cend_, cum_)]
    bk_end_ = [jnp.concatenate([b * ge, k * ge], axis=0) for b, k, ge in zip(b_, k_, gend_)]
    vst_ = [stack(v) for v in v_]

    G_ = [_dot_nt(jnp.concatenate([at, rt], axis=0), jnp.concatenate([stack(b * gi), stack(k * gi)], axis=0))
          for at, rt, b, k, gi in zip(at_, rt_, b_, k_, ginv_)]
    n1_ = [jnp.where(strict, G[:C, :LANES], 0.0) for G in G_]
    aak_ = [jnp.where(strict, G[:C, LANES:], 0.0) for G in G_]
    arb_ = [jnp.where(incl, G[C:, :LANES], 0.0) for G in G_]
    ark_ = [jnp.where(incl, G[C:, LANES:], 0.0) for G in G_]
    av_ = [_dot(jnp.concatenate([aak, ark], axis=0), vst) for aak, ark, vst in zip(aak_, ark_, vst_)]

    s1_ = [stack(n1) for n1 in n1_]
    n2_ = [_dot(n1, s1) for n1, s1 in zip(n1_, s1_)]
    x_ = [_dot(n2, jnp.concatenate([s1, stack(n2)], axis=1)) for n2, s1 in zip(n2_, s1_)]
    t_ = [eye_w + n1 + n2 + x[:, :LANES] for n1, n2, x in zip(n1_, n2_, x_)]
    np_ = [x[:, LANES:] for x in x_]
    for _ in range(3):
        x_ = [_dot(npow, stack2(t, npow)) for t, npow in zip(t_, np_)]
        t_ = [t + x[:, :LANES] for t, x in zip(t_, x_)]
        np_ = [x[:, LANES:] for x in x_]
    t_ = [t + _dot(npow, stack(t)) for t, npow in zip(t_, np_)]

    x_ = [_dot(t, stack2(at, av[:C])) for t, at, av in zip(t_, at_, av_)]
    z_ = [_dot(arb, stack2(x[:, :LANES], x[:, LANES:])) for arb, x in zip(arb_, x_)]
    rp_ = [rt + z[:, :LANES] for rt, z in zip(rt_, z_)]
    p3_ = [z[:, LANES:] + av[C:] for z, av in zip(z_, av_)]
    rhs_ = [jnp.concatenate([x, jnp.concatenate([jnp.zeros_like(v), v], axis=1)], axis=0)
            for x, v in zip(x_, v_)]
    mq_ = [_dot(bk_end.T, rhs) for bk_end, rhs in zip(bk_end_, rhs_)]
    m_ = [eye * jnp.exp(cend) + jnp.where(same_head, mq[:, :LANES], 0.0) for cend, mq in zip(cend_, mq_)]
    q_ = [jnp.where(same_head, mq[:, LANES:], 0.0) for mq in mq_]

    pairs = range(n_pairs)
    H_ = [h_ref[q] for q in pairs]
    ys_ = [[] for _ in pairs]
    for c in range(n_chunks):
        for q in pairs:
            u = q * n_chunks + c
            ys_[q].append(_dot(rp_[u], H_[q]) + p3_[u])
        H_ = [_dot(m_[q * n_chunks + c], H_[q]) + q_[q * n_chunks + c] for q in pairs]
    for q in pairs:
        h_ref[q] = H_[q]

    emean = emean_ref[...]
    y_ = [jnp.concatenate(ys, axis=0) for ys in ys_]
    mu_ = [_dot2_lhs(y, emean) for y in y_]
    yc_ = [y - mu for y, mu in zip(y_, mu_)]
    var_ = [_dot2_lhs(yc * yc, emean) for yc in yc_]
    for q in pairs:
        yn = yc_[q] * lax.rsqrt(var_[q] + GN_EPS) * gng_ref[q] + gnb_ref[q]
        o_ref[0, q] = ((yn + bonus_ref[0, q]) * g_ref[0, q]).astype(BF16)


def _wkv(r, lw, k, v, a, b, g, bonus, gn_g, gn_b, emean):
    B, P, S, _ = r.shape
    tb = WKV_TB
    pp = WKV_PAIRS
    seq = pl.BlockSpec((1, pp, tb, LANES), lambda bi, p, s: (bi, p, s, 0))
    par = pl.BlockSpec((pp, 1, LANES), lambda bi, p, s: (p, 0, 0))
    return pl.pallas_call(
        _wkv_kernel,
        grid=(B, P // pp, S // tb),
        in_specs=[seq] * 8 + [par, par, pl.BlockSpec((2 * LANES, LANES), lambda bi, p, s: (0, 0))],
        out_specs=seq,
        out_shape=jax.ShapeDtypeStruct((B, P, S, LANES), BF16),
        scratch_shapes=[pltpu.VMEM((pp, LANES, LANES), F32)],
        compiler_params=pltpu.CompilerParams(dimension_semantics=("arbitrary", "arbitrary", "arbitrary"),
                                             vmem_limit_bytes=VMEM_LIMIT),
        name="wkv",
    )(r, lw, k, v, a, b, g, bonus, gn_g, gn_b, emean)


def _mixer_kernel(x_ref, lng_ref, lnb_ref, ya_ref, yb_ref, wout_ref, l1g_ref, l1b_ref, wr_ref, br_ref,
                  base_ref, x1_ref, route_ref, counts_ref, carry_ref):
    tm = x_ref.shape[1]

    @pl.when((pl.program_id(0) == 0) & (pl.program_id(1) == 0))
    def _():
        carry_ref[...] = jnp.zeros_like(carry_ref)

    x0 = _layer_norm(x_ref[0], lng_ref[...], lnb_ref[...], LN_EPS)
    ymix = jnp.concatenate([ya_ref[0, p] for p in range(N_PAIRS)] + [yb_ref[0]], axis=-1)
    mix = jnp.dot(ymix, wout_ref[...], preferred_element_type=F32)
    x1 = _layer_norm(ALPHA * x0 + mix, l1g_ref[...], l1b_ref[...], LN_EPS)
    x1b = x1.astype(BF16)
    half = D_MODEL // 2
    lo_bits = lax.bitcast_convert_type(x1b[:, :half].astype(F32), jnp.uint32)
    hi_bits = lax.bitcast_convert_type(x1b[:, half:].astype(F32), jnp.uint32)
    x1_ref[0] = (hi_bits & jnp.uint32(0xFFFF0000)) | (lo_bits >> 16)

    hi, mid = _split2(x1)
    whi = wr_ref[0]
    wmid = wr_ref[1]
    d = lambda u, w: jnp.dot(u, w, preferred_element_type=F32)
    logits = (d(hi, whi) + d(hi, wmid) + d(mid, whi)) + br_ref[...]
    lane = _iota((tm, LANES), 1).astype(F32)
    far = float(4 * LANES)
    is_g = jnp.where(lane >= N_EXPERTS, jnp.where(lane < N_EXPERTS + N_GROUPS, 1.0, 0.0), 0.0) > 0.5
    gl = jnp.where(is_g, logits, NEG)
    gmax = jnp.max(gl, axis=-1, keepdims=True)
    gsel = jnp.min(jnp.where(gl == gmax, lane, far), axis=-1, keepdims=True) - N_EXPERTS
    p_group = 1.0 / jnp.sum(jnp.where(is_g, jnp.exp(gl - gmax), 0.0), axis=-1, keepdims=True)
    grp_of_lane = jnp.floor(lane * (1.0 / EXPERTS_PER_GROUP))
    el = jnp.where(grp_of_lane == gsel, logits, NEG)
    v1 = jnp.max(el, axis=-1, keepdims=True)
    i1 = jnp.min(jnp.where(el == v1, lane, far), axis=-1, keepdims=True)
    el2 = jnp.where(lane == i1, NEG, el)
    v2 = jnp.max(el2, axis=-1, keepdims=True)
    i2 = jnp.min(jnp.where(el2 == v2, lane, far), axis=-1, keepdims=True)
    e21 = jnp.exp(v2 - v1)
    w1 = p_group / (1.0 + e21)
    w2 = p_group * e21 / (1.0 + e21)

    oh1 = lane == i1
    oh2 = lane == i2
    below = (_iota((tm, tm), 0) > _iota((tm, tm), 1)).astype(BF16)
    o1 = jnp.where(oh1, 1.0, 0.0)
    o2 = jnp.where(oh2, 1.0, 0.0)
    c1 = jnp.dot(below, o1.astype(BF16), preferred_element_type=F32)
    c2 = jnp.dot(below, o2.astype(BF16), preferred_element_type=F32)
    tot1 = jnp.sum(o1, axis=0, keepdims=True)
    carry = carry_ref[0:1, :]
    rank1 = jnp.sum(jnp.where(oh1, c1 + carry, 0.0), axis=-1, keepdims=True)
    rank2 = jnp.sum(jnp.where(oh2, c2 + carry + tot1, 0.0), axis=-1, keepdims=True)
    carry = carry + tot1 + jnp.sum(o2, axis=0, keepdims=True)
    carry_ref[0:1, :] = carry
    counts_ref[...] = jnp.broadcast_to(carry, counts_ref.shape)

    fields = (i1, i2, w1, w2, rank1, rank2)
    route = jnp.zeros((tm, LANES), F32)
    for n, f in enumerate(fields):
        route = jnp.where(lane == n, f, route)
    route_ref[0] = route

    base_ref[0] = ALPHA * x1


def _mixer(x, ln_g, ln_b, ya, yb, w_out, l1g, l1b, wr3, br):
    B, S, _ = x.shape
    tm = MIX_TM
    const = lambda shape: pl.BlockSpec(shape, lambda b, s: (0,) * len(shape))
    row = lambda w: pl.BlockSpec((1, tm, w), lambda b, s: (b, s, 0))
    return pl.pallas_call(
        _mixer_kernel,
        grid=(B, S // tm),
        in_specs=[
            row(D_MODEL), const((1, D_MODEL)), const((1, D_MODEL)),
            pl.BlockSpec((1, N_PAIRS, tm, LANES), lambda b, s: (b, 0, s, 0)), row(D_GMLP),
            const((D_MODEL, D_MODEL)), const((1, D_MODEL)), const((1, D_MODEL)),
            const((2, D_MODEL, LANES)), const((1, LANES)),
        ],
        out_specs=[row(D_MODEL), row(D_MODEL // 2), row(LANES), const((8, LANES))],
        out_shape=[jax.ShapeDtypeStruct((B, S, D_MODEL), F32), jax.ShapeDtypeStruct((B, S, D_MODEL // 2), jnp.uint32),
                   jax.ShapeDtypeStruct((B, S, LANES), F32), jax.ShapeDtypeStruct((8, LANES), F32)],
        scratch_shapes=[pltpu.VMEM((8, LANES), F32)],
        compiler_params=pltpu.CompilerParams(dimension_semantics=("arbitrary", "arbitrary"),
                                             vmem_limit_bytes=VMEM_LIMIT),
        name="mixer",
    )(x, ln_g, ln_b, ya, yb, w_out, l1g, l1b, wr3, br)


def _slots_kernel(route_ref, counts_ref, dest_ref, pend_ref):
    tm = route_ref.shape[0]
    lane = _iota((tm, LANES), 1)
    route = route_ref[...]
    oh1 = lane == route[:, 0:1].astype(jnp.int32)
    oh2 = lane == route[:, 1:2].astype(jnp.int32)

    counts = counts_ref[0:1, :]
    padded = jnp.floor((counts + (EXPERT_ROWS - 1)) * (1.0 / EXPERT_ROWS)) * EXPERT_ROWS
    upper = (_iota((LANES, LANES), 0) <= _iota((LANES, LANES), 1)).astype(BF16)
    pend = _dot3_lhs(jnp.broadcast_to(padded, (8, LANES)), upper)[0:1, :]
    pstart = pend - padded
    d1 = jnp.sum(jnp.where(oh1, pstart, 0.0), axis=-1, keepdims=True) + route[:, 4:5]
    d2 = jnp.sum(jnp.where(oh2, pstart, 0.0), axis=-1, keepdims=True) + route[:, 5:6]
    dest_ref[...] = jnp.where(lane == 0, d1, jnp.where(lane == 1, d2, 0.0)).astype(jnp.int32)
    pend_ref[...] = jnp.broadcast_to(pend, (8, LANES)).astype(jnp.int32)


def _slots(route, counts):
    T = route.shape[0]
    tm = SLOT_TM
    return pl.pallas_call(
        _slots_kernel,
        grid=(T // tm,),
        in_specs=[pl.BlockSpec((tm, LANES), lambda i: (i, 0)), pl.BlockSpec((8, LANES), lambda i: (0, 0))],
        out_specs=[pl.BlockSpec((tm, LANES), lambda i: (i, 0)),
                   pl.BlockSpec((8, LANES), lambda i: (0, 0))],
        out_shape=[jax.ShapeDtypeStruct((T, LANES), jnp.int32), jax.ShapeDtypeStruct((8, LANES), jnp.int32)],
        compiler_params=pltpu.CompilerParams(dimension_semantics=("arbitrary",), vmem_limit_bytes=VMEM_LIMIT),
        name="slots",
    )(route, counts)


def _dispatch_kernel(pend_ref, dest_ref, x_ref, p_ref, wpg_ref, bpg_ref, wpp_ref, xs_ref, ple_ref, zero_ref, sem, zsem):
    tm = dest_ref.shape[0] // TOP_K

    @pl.when(pl.program_id(0) == 0)
    def _():
        zero_ref[...] = jnp.zeros_like(zero_ref)

        def tail(e):
            start = pl.multiple_of(jnp.maximum(pend_ref[e] - EXPERT_ROWS, 0), EXPERT_ROWS)
            return pltpu.make_async_copy(zero_ref, xs_ref.at[pl.ds(start, EXPERT_ROWS)], zsem)

        def unused(j):
            return pltpu.make_async_copy(
                zero_ref, xs_ref.at[pl.ds(pl.multiple_of(j * EXPERT_ROWS, EXPERT_ROWS), EXPERT_ROWS)], zsem)

        def start_unused(j, _):
            unused(j).start()
            return 0

        def wait_unused(j, _):
            unused(j).wait()
            return 0

        first_unused = pend_ref[N_EXPERTS - 1] // EXPERT_ROWS
        n_blocks = xs_ref.shape[0] // EXPERT_ROWS
        for e in range(N_EXPERTS):
            tail(e).start()
        lax.fori_loop(first_unused, n_blocks, start_unused, 0)
        for e in range(N_EXPERTS):
            tail(e).wait()
        lax.fori_loop(first_unused, n_blocks, wait_unused, 0)

    for t in range(tm):
        for j in range(TOP_K):
            pltpu.make_async_copy(x_ref.at[pl.ds(t, 1)], xs_ref.at[pl.ds(dest_ref[TOP_K * t + j], 1)],
                                  sem).start(priority=j)

    xw = x_ref[...]
    x_lo = lax.bitcast_convert_type(xw << 16, F32)
    x_hi = lax.bitcast_convert_type(xw & jnp.uint32(0xFFFF0000), F32)
    x1b = jnp.concatenate([x_lo, x_hi], axis=1).astype(BF16)
    gate = _sigmoid(jnp.dot(x1b, wpg_ref[...], preferred_element_type=F32) + bpg_ref[...])
    ple_ref[...] = gate * jnp.dot(p_ref[...].astype(BF16), wpp_ref[...], preferred_element_type=F32)

    for j in range(TOP_K):
        pltpu.make_async_copy(x_ref, xs_ref.at[pl.ds(0, tm)], sem).wait()


def _dispatch(pend, dest_flat, x1, p, wpg, bpg, wpp, n_rows):
    T, width = x1.shape
    tm = DISPATCH_TM
    const = lambda shape: pl.BlockSpec(shape, lambda i, pe: (0,) * len(shape))
    return pl.pallas_call(
        _dispatch_kernel,
        grid_spec=pltpu.PrefetchScalarGridSpec(
            num_scalar_prefetch=1,
            grid=(T // tm,),
            in_specs=[pl.BlockSpec((TOP_K * tm,), lambda i, pe: (i,), memory_space=pltpu.SMEM),
                      pl.BlockSpec((tm, width), lambda i, pe: (i, 0)),
                      pl.BlockSpec((tm, D_PLE), lambda i, pe: (i, 0)),
                      const((D_MODEL, D_MODEL)), const((1, D_MODEL)), const((D_PLE, D_MODEL))],
            out_specs=[pl.BlockSpec(memory_space=pl.ANY), pl.BlockSpec((tm, D_MODEL), lambda i, pe: (i, 0))],
            scratch_shapes=[pltpu.VMEM((EXPERT_ROWS, width), x1.dtype), pltpu.SemaphoreType.DMA,
                            pltpu.SemaphoreType.DMA],
        ),
        out_shape=[jax.ShapeDtypeStruct((n_rows, width), x1.dtype), jax.ShapeDtypeStruct((T, D_MODEL), F32)],
        compiler_params=pltpu.CompilerParams(dimension_semantics=("arbitrary",), vmem_limit_bytes=VMEM_LIMIT),
        name="dispatch",
    )(pend, dest_flat, x1, p, wpg, bpg, wpp)


def _experts_kernel(pend_ref, xs_ref, wg_ref, wu_ref, wd_ref, ys_ref, xbuf_ref, ybuf_ref, wgu_ref, wdb_ref,
                    in_sem, out_sem):
    rows = EXPERT_ROWS
    e = pl.program_id(0)
    first = jnp.where(e == 0, 0, pend_ref[jnp.maximum(e - 1, 0)]) // rows
    last = pend_ref[e] // rows
    n_used = pend_ref[N_EXPERTS - 1] // rows

    def block_rows(ref, b):
        return ref.at[pl.ds(pl.multiple_of(b * rows, rows), rows)]

    depth = xbuf_ref.shape[0]
    row_priority = 1

    def x_copy(b):
        slot = b % depth
        return pltpu.make_async_copy(block_rows(xs_ref, b), xbuf_ref.at[slot], in_sem.at[slot])

    def y_copy(b):
        slot = b % depth
        return pltpu.make_async_copy(ybuf_ref.at[slot], block_rows(ys_ref, b), out_sem.at[slot])

    @pl.when(e == 0)
    def _():
        for ahead in range(depth - 1):
            @pl.when(ahead < n_used)
            def _():
                x_copy(ahead).start(priority=row_priority)

    @pl.when(last > first)
    def _():
        wgu_ref[:, :D_EXPERT] = wg_ref[0].astype(BF16)
        wgu_ref[:, D_EXPERT:] = wu_ref[0].astype(BF16)
        wdb_ref[...] = wd_ref[0].astype(BF16)

        def body(b, _):
            slot = b % depth

            @pl.when(b + depth - 1 < n_used)
            def _():
                x_copy(b + depth - 1).start(priority=row_priority)

            x_copy(b).wait()

            @pl.when(b >= depth)
            def _():
                y_copy(b - depth).wait()

            xw = xbuf_ref[slot]
            x_lo = lax.bitcast_convert_type(xw << 16, F32)
            x_hi = lax.bitcast_convert_type(xw & jnp.uint32(0xFFFF0000), F32)
            xb = jnp.concatenate([x_lo, x_hi], axis=1).astype(BF16)
            h = jnp.dot(xb, wgu_ref[...], preferred_element_type=F32)
            hg = h[:, :D_EXPERT]
            hid = hg * _sigmoid(hg) * h[:, D_EXPERT:]
            ybuf_ref[slot] = jnp.dot(hid.astype(BF16), wdb_ref[...], preferred_element_type=F32)
            y_copy(b).start(priority=row_priority)
            return 0

        lax.fori_loop(first, last, body, 0)

    @pl.when(e == N_EXPERTS - 1)
    def _():
        for back in range(depth, 0, -1):
            @pl.when(n_used >= back)
            def _():
                y_copy(n_used - back).wait()

        ybuf_ref[0] = jnp.zeros(ybuf_ref.shape[1:], F32)

        def unused(b):
            return pltpu.make_async_copy(ybuf_ref.at[0], block_rows(ys_ref, b), out_sem.at[0])

        def start_unused(b, _):
            unused(b).start()
            return 0

        def wait_unused(b, _):
            unused(b).wait()
            return 0

        n_blocks = ys_ref.shape[0] // rows
        lax.fori_loop(n_used, n_blocks, start_unused, 0)
        lax.fori_loop(n_used, n_blocks, wait_unused, 0)


def _experts(pend, xs, wg, wu, wd):
    n_rows = xs.shape[0]
    rows = EXPERT_ROWS
    wspec = lambda shape: pl.BlockSpec((1,) + shape, lambda e, pe: (e, 0, 0))
    return pl.pallas_call(
        _experts_kernel,
        grid_spec=pltpu.PrefetchScalarGridSpec(
            num_scalar_prefetch=1,
            grid=(N_EXPERTS,),
            in_specs=[pl.BlockSpec(memory_space=pl.ANY),
                      wspec((D_MODEL, D_EXPERT)), wspec((D_MODEL, D_EXPERT)), wspec((D_EXPERT, D_MODEL))],
            out_specs=pl.BlockSpec(memory_space=pl.ANY),
            scratch_shapes=[pltpu.VMEM((EXPERT_DEPTH, rows, D_MODEL // 2), jnp.uint32),
                            pltpu.VMEM((EXPERT_DEPTH, rows, D_MODEL), F32),
                            pltpu.VMEM((D_MODEL, 2 * D_EXPERT), BF16), pltpu.VMEM((D_EXPERT, D_MODEL), BF16),
                            pltpu.SemaphoreType.DMA((EXPERT_DEPTH,)), pltpu.SemaphoreType.DMA((EXPERT_DEPTH,))],
        ),
        out_shape=jax.ShapeDtypeStruct((n_rows, D_MODEL), F32),
        compiler_params=pltpu.CompilerParams(dimension_semantics=("arbitrary",), vmem_limit_bytes=VMEM_LIMIT),
        name="experts",
    )(pend, xs, wg, wu, wd)


def _combine_kernel(dest_ref, dest_next_ref, ys_ref, base_ref, route_ref, ple_ref, lg_ref, lb_ref, o_ref, buf_ref, sem):
    tm = buf_ref.shape[2]
    i = pl.program_id(0)

    def gather(dref, offset, s):
        for t in range(tm):
            for j in range(TOP_K):
                pltpu.make_async_copy(ys_ref.at[pl.ds(dref[offset + TOP_K * t + j], 1)],
                                      buf_ref.at[s, j, pl.ds(t, 1)], sem.at[s]).start(priority=j)

    def drain(s):
        for j in range(TOP_K):
            pltpu.make_async_copy(ys_ref.at[pl.ds(0, tm)], buf_ref.at[s, j], sem.at[s]).wait()

    def finish(s):
        rows = slice(s * tm, (s + 1) * tm)
        drain(s)
        route = route_ref[rows, :]
        ffn = buf_ref[s, 0] * route[:, 2:3] + buf_ref[s, 1] * route[:, 3:4]
        o_ref[rows, :] = _layer_norm(base_ref[rows, :] + ffn + ple_ref[rows, :], lg_ref[...], lb_ref[...], LN_EPS)

    @pl.when(i == 0)
    def _():
        gather(dest_ref, 0, 0)

    gather(dest_ref, TOP_K * tm, 1)
    finish(0)
    gather(dest_next_ref, 0, 0)
    finish(1)

    @pl.when(i == pl.num_programs(0) - 1)
    def _():
        drain(0)


def _combine(dest_flat, ys, base, route, ple, l2g, l2b):
    T = base.shape[0]
    tm = COMBINE_TM
    nt = T // tm
    tile = lambda w: pl.BlockSpec((2 * tm, w), lambda i: (i, 0))
    const = lambda shape: pl.BlockSpec(shape, lambda i: (0,) * len(shape))
    return pl.pallas_call(
        _combine_kernel,
        grid=(nt // 2,),
        in_specs=[pl.BlockSpec((2 * TOP_K * tm,), lambda i: (i,), memory_space=pltpu.SMEM),
                  pl.BlockSpec((TOP_K * tm,), lambda i: (jnp.minimum(2 * i + 2, nt - 1),), memory_space=pltpu.SMEM),
                  pl.BlockSpec(memory_space=pl.ANY),
                  tile(D_MODEL), tile(LANES), tile(D_MODEL),
                  const((1, D_MODEL)), const((1, D_MODEL))],
        out_specs=pl.BlockSpec((2 * tm, D_MODEL), lambda i: (i, 0)),
        out_shape=jax.ShapeDtypeStruct((T, D_MODEL), F32),
        scratch_shapes=[pltpu.VMEM((2, TOP_K, tm, D_MODEL), F32), pltpu.SemaphoreType.DMA((2,))],
        compiler_params=pltpu.CompilerParams(dimension_semantics=("arbitrary",), vmem_limit_bytes=VMEM_LIMIT),
        name="combine",
    )(dest_flat, dest_flat, ys, base, route, ple, l2g, l2b)


def _block_diag_const(n, blk, val):
    idx = jnp.arange(n) // blk
    return jnp.where(idx[:, None] == idx[None, :], val, 0.0).astype(BF16)


def kernel(x, p, ln_emb_g, ln_emb_b, w_in, mu_shift, w0, w_decay_up, a0, w_iclr_up, w_gate_up, k_k, k_a, r_k, gn_g, gn_b, gmlp_ln_g, gmlp_ln_b, w_spatial, b_spatial, w_out, ln1_g, ln1_b, w_group_router, b_group_router, w_expert_router, b_expert_router, w_exp_gate, w_exp_up, w_exp_down, w_ple_gate, b_ple_gate, w_ple_proj, ln2_g, ln2_b):
    B, S, D = x.shape
    T = B * S
    row = lambda t: t.reshape(1, -1).astype(F32)

    zl = jnp.zeros((DECAY_LORA, D_RWKV), F32)
    wwa = jnp.concatenate([jnp.concatenate([w_decay_up[0], zl], axis=1),
                           jnp.concatenate([zl, w_iclr_up[0]], axis=1)], axis=0).astype(BF16)
    w0a0 = jnp.concatenate([w0[0], a0[0]]).reshape(1, -1)
    eones = jnp.tile(_block_diag_const(2 * LANES, HEAD, 1.0), (2, 1))
    emean = jnp.tile(_block_diag_const(LANES, HEAD, 1.0 / HEAD), (2, 1))

    r, lw, k, v, a, b, g, bonus, yb = _prep(
        x, row(ln_emb_g), row(ln_emb_b), w_in[0].astype(BF16), row(mu_shift[0]), wwa, w0a0,
        w_gate_up[0].astype(BF16), row(k_k[0]), row(k_a[0]), row(r_k[0]), eones,
        row(gmlp_ln_g[0]), row(gmlp_ln_b[0]), w_spatial[0], b_spatial[0].T)

    ya = _wkv(r, lw, k, v, a, b, g, bonus, gn_g[0].reshape(N_PAIRS, 1, LANES), gn_b[0].reshape(N_PAIRS, 1, LANES),
              emean)

    wr = jnp.concatenate([w_expert_router[0].reshape(D, N_EXPERTS), w_group_router[0],
                          jnp.zeros((D, LANES - N_EXPERTS - N_GROUPS), F32)], axis=1)
    wr3 = jnp.stack(_split2(wr))
    br = jnp.concatenate([b_expert_router[0].reshape(-1), b_group_router[0],
                          jnp.zeros((LANES - N_EXPERTS - N_GROUPS,), F32)]).reshape(1, LANES)
    base, x1, route, counts = _mixer(x, row(ln_emb_g), row(ln_emb_b), ya, yb, w_out[0].astype(BF16), row(ln1_g[0]),
                                     row(ln1_b[0]), wr3, br)
    base = base.reshape(T, D)
    x1 = x1.reshape(T, D // 2)
    route = route.reshape(T, LANES)

    n_blocks = -(-(T * TOP_K) // EXPERT_ROWS) + N_EXPERTS
    dest, pend = _slots(route, counts)
    dest_flat = dest[:, :TOP_K].reshape(T * TOP_K)
    pend = pend[0, :N_EXPERTS]

    xs, ple = _dispatch(pend, dest_flat, x1, p[0].reshape(T, D_PLE), w_ple_gate[0].astype(BF16),
                        row(b_ple_gate[0]), w_ple_proj[0].astype(BF16), n_blocks * EXPERT_ROWS)
    ys = _experts(pend, xs, w_exp_gate[0], w_exp_up[0], w_exp_down[0])
    out = _combine(dest_flat, ys, base, route, ple, row(ln2_g[0]), row(ln2_b[0]))
    return out.reshape(B, S, D)
```

```python
import functools
import math

import jax
import jax.numpy as jnp
from jax import lax
from jax.experimental import pallas as pl
from jax.experimental.pallas import tpu as pltpu

F32 = jnp.float32
BF16 = jnp.bfloat16

D_MODEL = 1024
D_RWKV = 512
HEAD = 64
D_GMLP = 512
GMLP_GROUPS = 4
GROUP_W = 128
GCHUNK = 128
DECAY_LORA = 64
ICLR_LORA = 64
GATE_LORA = 128
N_SHIFT = 3 * D_RWKV + DECAY_LORA + ICLR_LORA + GATE_LORA
D_IN = N_SHIFT + 2 * D_GMLP
D_PLE = 256
N_GROUPS = 4
EXPERTS_PER_GROUP = 8
N_EXPERTS = 32
TOP_K = 2
D_EXPERT = 512
DEPTH = 1
ALPHA = (2.0 * DEPTH) ** 0.25
LN_EPS = 1e-5
GN_EPS = 64e-5
DECAY_SCALE = math.exp(-0.5)

LANES = 128
WKV_CHUNK = 64
N_PAIRS = D_RWKV // LANES
VMEM_LIMIT = 56 * 1024 * 1024

PREP_TM = 512
WKV_TB = 512
WKV_PAIRS = 4
MIX_TM = 512
SLOT_TM = 2048
EXPERT_ROWS = 256
EXPERT_DEPTH = 4
DISPATCH_TM = 512
COMBINE_TM = 256
NEG = -1e30


def _dot(a, b):
    return jnp.dot(a.astype(BF16), b.astype(BF16), preferred_element_type=F32)


def _dot_nt(a, b):
    return lax.dot_general(a.astype(BF16), b.astype(BF16), (((1,), (1,)), ((), ())),
                           preferred_element_type=F32)


def _split3(x):
    hi = x.astype(BF16)
    r1 = x - hi.astype(F32)
    mid = r1.astype(BF16)
    lo = (r1 - mid.astype(F32)).astype(BF16)
    return hi, mid, lo


def _dot3_lhs(x, w):
    hi, mid, lo = _split3(x)
    w = w.astype(BF16)
    return (jnp.dot(hi, w, preferred_element_type=F32) + jnp.dot(mid, w, preferred_element_type=F32)
            + jnp.dot(lo, w, preferred_element_type=F32))


def _split2(x):
    hi = x.astype(BF16)
    return hi, (x - hi.astype(F32)).astype(BF16)


def _dot2_lhs(x, w2):
    hi, lo = _split2(x)
    return jnp.dot(jnp.concatenate([hi, lo], axis=1), w2, preferred_element_type=F32)


def _dot3_rhs(w3, x):
    hi, mid, lo = _split3(x)
    return jnp.dot(w3, jnp.concatenate([hi, mid, lo], axis=0), preferred_element_type=F32)


def _layer_norm(x, g, b, eps):
    mu = jnp.mean(x, axis=-1, keepdims=True)
    xc = x - mu
    var = jnp.mean(xc * xc, axis=-1, keepdims=True)
    return xc * lax.rsqrt(var + eps) * g + b


def _sigmoid(x):
    return 1.0 / (1.0 + jnp.exp(-x))


def _iota(shape, dim):
    return lax.broadcasted_iota(jnp.int32, shape, dim)


def _prep_kernel(x_ref, lng_ref, lnb_ref, win_ref, mu_ref, wwa_ref, w0a0_ref, wg_ref, kk_ref, ka_ref, rk_ref,
                 eones_ref, glng_ref, glnb_ref, wsp_ref, bsp_ref,
                 r_ref, lw_ref, k_ref, v_ref, a_ref, b_ref, g_ref, bonus_ref, yb_ref, carry_ref):
    tm = x_ref.shape[1]

    @pl.when(pl.program_id(1) == 0)
    def _():
        carry_ref[...] = jnp.zeros_like(carry_ref)

    x0 = _layer_norm(x_ref[0], lng_ref[...], lnb_ref[...], LN_EPS)
    proj = jnp.dot(x0.astype(BF16), win_ref[...], preferred_element_type=F32)

    h = proj[:, :N_SHIFT]
    rolled = pltpu.roll(h, 1, 0)
    first = _iota((tm, N_SHIFT), 0) == 0
    prev = jnp.where(first, jnp.broadcast_to(carry_ref[0:1, :], (tm, N_SHIFT)), rolled)
    carry_ref[0:1, :] = h[tm - 1:tm, :]
    h = h + (prev - h) * mu_ref[...]

    r = h[:, 0:D_RWKV]
    k = h[:, D_RWKV:2 * D_RWKV]
    v = h[:, 2 * D_RWKV:3 * D_RWKV]
    xwa = h[:, 3 * D_RWKV:3 * D_RWKV + LANES]
    xg = h[:, 3 * D_RWKV + LANES:N_SHIFT]

    lane = _iota((tm, LANES), 1)
    twa = jnp.where(lane < DECAY_LORA, jnp.tanh(xwa), xwa)
    da = _dot(twa, wwa_ref[...]) + w0a0_ref[...]
    logw = -DECAY_SCALE * _sigmoid(da[:, :D_RWKV])
    ag = _sigmoid(da[:, D_RWKV:])
    g = _dot(_sigmoid(xg), wg_ref[...])

    eones2 = eones_ref[...]

    def head_sum(t):
        half = 2 * LANES
        return jnp.concatenate([_dot2_lhs(t[:, :half], eones2), _dot2_lhs(t[:, half:], eones2)], axis=1)

    kk = k * kk_ref[...]
    kk = kk / jnp.maximum(jnp.sqrt(head_sum(kk * kk)), 1e-12)
    k = k * (1.0 + (ag - 1.0) * ka_ref[...])
    bonus = head_sum(r * k * rk_ref[...]) * v

    for p in range(N_PAIRS):
        sl = slice(p * LANES, (p + 1) * LANES)
        r_ref[0, p] = r[:, sl]
        lw_ref[0, p] = logw[:, sl]
        k_ref[0, p] = k[:, sl]
        v_ref[0, p] = v[:, sl]
        a_ref[0, p] = -kk[:, sl]
        b_ref[0, p] = (kk * ag)[:, sl]
        g_ref[0, p] = g[:, sl]
        bonus_ref[0, p] = bonus[:, sl]

    zin = proj[:, N_SHIFT:]
    z = 0.5 * zin * (1.0 + lax.erf(zin * (0.5 ** 0.5)))
    zu = z[:, :D_GMLP]
    zv = z[:, D_GMLP:]
    causal = _iota((GCHUNK, GCHUNK), 0) >= _iota((GCHUNK, GCHUNK), 1)
    for gi in range(GMLP_GROUPS):
        gs = slice(gi * GROUP_W, (gi + 1) * GROUP_W)
        zvn = _layer_norm(zv[:, gs], glng_ref[:, gs], glnb_ref[:, gs], LN_EPS)
        ws = jnp.where(causal, wsp_ref[gi], 0.0).astype(BF16)
        bcol = bsp_ref[:, gi:gi + 1]
        for c in range(tm // GCHUNK):
            ts = slice(c * GCHUNK, (c + 1) * GCHUNK)
            mixed = jnp.dot(ws, zvn[ts].astype(BF16), preferred_element_type=F32) + bcol
            yb_ref[0, ts, gs] = (zu[ts, gs] * mixed).astype(BF16)


def _prep(x, ln_g, ln_b, w_in, mu, wwa, w0a0, wg, k_k, k_a, r_k, eones, glng, glnb, wsp, bsp):
    B, S, _ = x.shape
    tm = PREP_TM
    const = lambda shape: pl.BlockSpec(shape, lambda b, s: (0,) * len(shape))
    pair_spec = pl.BlockSpec((1, N_PAIRS, tm, LANES), lambda b, s: (b, 0, s, 0))
    pair_shape = jax.ShapeDtypeStruct((B, N_PAIRS, S, LANES), F32)
    return pl.pallas_call(
        _prep_kernel,
        grid=(B, S // tm),
        in_specs=[
            pl.BlockSpec((1, tm, D_MODEL), lambda b, s: (b, s, 0)),
            const((1, D_MODEL)), const((1, D_MODEL)), const((D_MODEL, D_IN)), const((1, N_SHIFT)),
            const((LANES, 2 * D_RWKV)), const((1, 2 * D_RWKV)), const((GATE_LORA, D_RWKV)),
            const((1, D_RWKV)), const((1, D_RWKV)), const((1, D_RWKV)), const((4 * LANES, 2 * LANES)),
            const((1, D_GMLP)), const((1, D_GMLP)), const((GMLP_GROUPS, GCHUNK, GCHUNK)),
            const((GCHUNK, GMLP_GROUPS)),
        ],
        out_specs=[pair_spec] * 8 + [pl.BlockSpec((1, tm, D_GMLP), lambda b, s: (b, s, 0))],
        out_shape=[pair_shape] * 8 + [jax.ShapeDtypeStruct((B, S, D_GMLP), BF16)],
        scratch_shapes=[pltpu.VMEM((8, N_SHIFT), F32)],
        compiler_params=pltpu.CompilerParams(dimension_semantics=("arbitrary", "arbitrary"),
                                             vmem_limit_bytes=VMEM_LIMIT),
        name="prep",
    )(x, ln_g, ln_b, w_in, mu, wwa, w0a0, wg, k_k, k_a, r_k, eones, glng, glnb, wsp, bsp)


def _wkv_kernel(r_ref, lw_ref, k_ref, v_ref, a_ref, b_ref, g_ref, bonus_ref, gng_ref, gnb_ref, emean_ref,
                o_ref, h_ref):
    C = WKV_CHUNK
    tb = r_ref.shape[2]

    @pl.when(pl.program_id(2) == 0)
    def _():
        h_ref[...] = jnp.zeros_like(h_ref)

    tok = _iota((C, LANES), 0)
    lane = _iota((C, LANES), 1)
    head0 = lane < HEAD
    strict = tok > lane % HEAD
    incl = tok >= lane % HEAD
    eye_w = (tok == lane % HEAD).astype(F32)
    rr = _iota((LANES, LANES), 0)
    cc = _iota((LANES, LANES), 1)
    eye = (rr == cc).astype(F32)
    same_head = (rr < HEAD) == (cc < HEAD)
    ltri3 = (_iota((C, 3 * C), 0) >= _iota((C, 3 * C), 1) % C).astype(BF16)

    def stack(x):
        xb = x.astype(BF16)
        zero = jnp.zeros_like(xb)
        return jnp.concatenate([jnp.where(head0, xb, zero), jnp.where(head0, zero, xb)], axis=0)

    def stack2(x, y):
        return jnp.concatenate([stack(x), stack(y)], axis=1)

    n_pairs = r_ref.shape[1]
    n_chunks = tb // C
    units = [(q, c) for q in range(n_pairs) for c in range(n_chunks)]

    def load(ref):
        return [ref[0, q, c * C:(c + 1) * C, :] for q, c in units]

    r_, lw_, k_, v_, a_, b_ = (load(ref) for ref in (r_ref, lw_ref, k_ref, v_ref, a_ref, b_ref))
    cum_ = [_dot3_rhs(ltri3, lw) for lw in lw_]
    cend_ = [cum[C - 1:C, :] for cum in cum_]
    at_ = [a * jnp.exp(cum - lw) for a, cum, lw in zip(a_, cum_, lw_)]
    rt_ = [r * jnp.exp(cum) for r, cum in zip(r_, cum_)]
    ginv_ = [jnp.exp(-cum) for cum in cum_]
    gend_ = [jnp.exp(cend - cum) for cend, cum in zip(cend_, cum_)]
    bk_end_ = [jnp.concatenate([b * ge, k * ge], axis=0) for b, k, ge in zip(b_, k_, gend_)]
    vst_ = [stack(v) for v in v_]

    G_ = [_dot_nt(jnp.concatenate([at, rt], axis=0), jnp.concatenate([stack(b * gi), stack(k * gi)], axis=0))
          for at, rt, b, k, gi in zip(at_, rt_, b_, k_, ginv_)]
    n1_ = [jnp.where(strict, G[:C, :LANES], 0.0) for G in G_]
    aak_ = [jnp.where(strict, G[:C, LANES:], 0.0) for G in G_]
    arb_ = [jnp.where(incl, G[C:, :LANES], 0.0) for G in G_]
    ark_ = [jnp.where(incl, G[C:, LANES:], 0.0) for G in G_]
    av_ = [_dot(jnp.concatenate([aak, ark], axis=0), vst) for aak, ark, vst in zip(aak_, ark_, vst_)]

    s1_ = [stack(n1) for n1 in n1_]
    n2_ = [_dot(n1, s1) for n1, s1 in zip(n1_, s1_)]
    x_ = [_dot(n2, jnp.concatenate([s1, stack(n2)], axis=1)) for n2, s1 in zip(n2_, s1_)]
    t_ = [eye_w + n1 + n2 + x[:, :LANES] for n1, n2, x in zip(n1_, n2_, x_)]
    np_ = [x[:, LANES:] for x in x_]
    for _ in range(3):
        x_ = [_dot(npow, stack2(t, npow)) for t, npow in zip(t_, np_)]
        t_ = [t + x[:, :LANES] for t, x in zip(t_, x_)]
        np_ = [x[:, LANES:] for x in x_]
    t_ = [t + _dot(npow, stack(t)) for t, npow in zip(t_, np_)]

    x_ = [_dot(t, stack2(at, av[:C])) for t, at, av in zip(t_, at_, av_)]
    z_ = [_dot(arb, stack2(x[:, :LANES], x[:, LANES:])) for arb, x in zip(arb_, x_)]
    rp_ = [rt + z[:, :LANES] for rt, z in zip(rt_, z_)]
    p3_ = [z[:, LANES:] + av[C:] for z, av in zip(z_, av_)]
    rhs_ = [jnp.concatenate([x, jnp.concatenate([jnp.zeros_like(v), v], axis=1)], axis=0)
            for x, v in zip(x_, v_)]
    mq_ = [_dot(bk_end.T, rhs) for bk_end, rhs in zip(bk_end_, rhs_)]
    m_ = [eye * jnp.exp(cend) + jnp.where(same_head, mq[:, :LANES], 0.0) for cend, mq in zip(cend_, mq_)]
    q_ = [jnp.where(same_head, mq[:, LANES:], 0.0) for mq in mq_]

    pairs = range(n_pairs)
    H_ = [h_ref[q] for q in pairs]
    ys_ = [[] for _ in pairs]
    for c in range(n_chunks):
        for q in pairs:
            u = q * n_chunks + c
            ys_[q].append(_dot(rp_[u], H_[q]) + p3_[u])
        H_ = [_dot(m_[q * n_chunks + c], H_[q]) + q_[q * n_chunks + c] for q in pairs]
    for q in pairs:
        h_ref[q] = H_[q]

    emean = emean_ref[...]
    y_ = [jnp.concatenate(ys, axis=0) for ys in ys_]
    mu_ = [_dot2_lhs(y, emean) for y in y_]
    yc_ = [y - mu for y, mu in zip(y_, mu_)]
    var_ = [_dot2_lhs(yc * yc, emean) for yc in yc_]
    for q in pairs:
        yn = yc_[q] * lax.rsqrt(var_[q] + GN_EPS) * gng_ref[q] + gnb_ref[q]
        o_ref[0, q] = ((yn + bonus_ref[0, q]) * g_ref[0, q]).astype(BF16)


def _wkv(r, lw, k, v, a, b, g, bonus, gn_g, gn_b, emean):
    B, P, S, _ = r.shape
    tb = WKV_TB
    pp = WKV_PAIRS
    seq = pl.BlockSpec((1, pp, tb, LANES), lambda bi, p, s: (bi, p, s, 0))
    par = pl.BlockSpec((pp, 1, LANES), lambda bi, p, s: (p, 0, 0))
    return pl.pallas_call(
        _wkv_kernel,
        grid=(B, P // pp, S // tb),
        in_specs=[seq] * 8 + [par, par, pl.BlockSpec((2 * LANES, LANES), lambda bi, p, s: (0, 0))],
        out_specs=seq,
        out_shape=jax.ShapeDtypeStruct((B, P, S, LANES), BF16),
        scratch_shapes=[pltpu.VMEM((pp, LANES, LANES), F32)],
        compiler_params=pltpu.CompilerParams(dimension_semantics=("arbitrary", "arbitrary", "arbitrary"),
                                             vmem_limit_bytes=VMEM_LIMIT),
        name="wkv",
    )(r, lw, k, v, a, b, g, bonus, gn_g, gn_b, emean)


def _mixer_kernel(x_ref, lng_ref, lnb_ref, ya_ref, yb_ref, wout_ref, l1g_ref, l1b_ref, wr_ref, br_ref,
                  base_ref, x1_ref, route_ref, counts_ref, carry_ref):
    tm = x_ref.shape[1]

    @pl.when((pl.program_id(0) == 0) & (pl.program_id(1) == 0))
    def _():
        carry_ref[...] = jnp.zeros_like(carry_ref)

    x0 = _layer_norm(x_ref[0], lng_ref[...], lnb_ref[...], LN_EPS)
    ymix = jnp.concatenate([ya_ref[0, p] for p in range(N_PAIRS)] + [yb_ref[0]], axis=-1)
    mix = jnp.dot(ymix, wout_ref[...], preferred_element_type=F32)
    x1 = _layer_norm(ALPHA * x0 + mix, l1g_ref[...], l1b_ref[...], LN_EPS)
    x1b = x1.astype(BF16)
    half = D_MODEL // 2
    lo_bits = lax.bitcast_convert_type(x1b[:, :half].astype(F32), jnp.uint32)
    hi_bits = lax.bitcast_convert_type(x1b[:, half:].astype(F32), jnp.uint32)
    x1_ref[0] = (hi_bits & jnp.uint32(0xFFFF0000)) | (lo_bits >> 16)

    hi, mid = _split2(x1)
    whi = wr_ref[0]
    wmid = wr_ref[1]
    d = lambda u, w: jnp.dot(u, w, preferred_element_type=F32)
    logits = (d(hi, whi) + d(hi, wmid) + d(mid, whi)) + br_ref[...]
    lane = _iota((tm, LANES), 1).astype(F32)
    far = float(4 * LANES)
    is_g = jnp.where(lane >= N_EXPERTS, jnp.where(lane < N_EXPERTS + N_GROUPS, 1.0, 0.0), 0.0) > 0.5
    gl = jnp.where(is_g, logits, NEG)
    gmax = jnp.max(gl, axis=-1, keepdims=True)
    gsel = jnp.min(jnp.where(gl == gmax, lane, far), axis=-1, keepdims=True) - N_EXPERTS
    p_group = 1.0 / jnp.sum(jnp.where(is_g, jnp.exp(gl - gmax), 0.0), axis=-1, keepdims=True)
    grp_of_lane = jnp.floor(lane * (1.0 / EXPERTS_PER_GROUP))
    el = jnp.where(grp_of_lane == gsel, logits, NEG)
    v1 = jnp.max(el, axis=-1, keepdims=True)
    i1 = jnp.min(jnp.where(el == v1, lane, far), axis=-1, keepdims=True)
    el2 = jnp.where(lane == i1, NEG, el)
    v2 = jnp.max(el2, axis=-1, keepdims=True)
    i2 = jnp.min(jnp.where(el2 == v2, lane, far), axis=-1, keepdims=True)
    e21 = jnp.exp(v2 - v1)
    w1 = p_group / (1.0 + e21)
    w2 = p_group * e21 / (1.0 + e21)

    oh1 = lane == i1
    oh2 = lane == i2
    below = (_iota((tm, tm), 0) > _iota((tm, tm), 1)).astype(BF16)
    o1 = jnp.where(oh1, 1.0, 0.0)
    o2 = jnp.where(oh2, 1.0, 0.0)
    c1 = jnp.dot(below, o1.astype(BF16), preferred_element_type=F32)
    c2 = jnp.dot(below, o2.astype(BF16), preferred_element_type=F32)
    tot1 = jnp.sum(o1, axis=0, keepdims=True)
    carry = carry_ref[0:1, :]
    rank1 = jnp.sum(jnp.where(oh1, c1 + carry, 0.0), axis=-1, keepdims=True)
    rank2 = jnp.sum(jnp.where(oh2, c2 + carry + tot1, 0.0), axis=-1, keepdims=True)
    carry = carry + tot1 + jnp.sum(o2, axis=0, keepdims=True)
    carry_ref[0:1, :] = carry
    counts_ref[...] = jnp.broadcast_to(carry, counts_ref.shape)

    fields = (i1, i2, w1, w2, rank1, rank2)
    route = jnp.zeros((tm, LANES), F32)
    for n, f in enumerate(fields):
        route = jnp.where(lane == n, f, route)
    route_ref[0] = route

    base_ref[0] = ALPHA * x1


def _mixer(x, ln_g, ln_b, ya, yb, w_out, l1g, l1b, wr3, br):
    B, S, _ = x.shape
    tm = MIX_TM
    const = lambda shape: pl.BlockSpec(shape, lambda b, s: (0,) * len(shape))
    row = lambda w: pl.BlockSpec((1, tm, w), lambda b, s: (b, s, 0))
    return pl.pallas_call(
        _mixer_kernel,
        grid=(B, S // tm),
        in_specs=[
            row(D_MODEL), const((1, D_MODEL)), const((1, D_MODEL)),
            pl.BlockSpec((1, N_PAIRS, tm, LANES), lambda b, s: (b, 0, s, 0)), row(D_GMLP),
            const((D_MODEL, D_MODEL)), const((1, D_MODEL)), const((1, D_MODEL)),
            const((2, D_MODEL, LANES)), const((1, LANES)),
        ],
        out_specs=[row(D_MODEL), row(D_MODEL // 2), row(LANES), const((8, LANES))],
        out_shape=[jax.ShapeDtypeStruct((B, S, D_MODEL), F32), jax.ShapeDtypeStruct((B, S, D_MODEL // 2), jnp.uint32),
                   jax.ShapeDtypeStruct((B, S, LANES), F32), jax.ShapeDtypeStruct((8, LANES), F32)],
        scratch_shapes=[pltpu.VMEM((8, LANES), F32)],
        compiler_params=pltpu.CompilerParams(dimension_semantics=("arbitrary", "arbitrary"),
                                             vmem_limit_bytes=VMEM_LIMIT),
        name="mixer",
    )(x, ln_g, ln_b, ya, yb, w_out, l1g, l1b, wr3, br)


def _slots_kernel(route_ref, counts_ref, dest_ref, pend_ref):
    tm = route_ref.shape[0]
    lane = _iota((tm, LANES), 1)
    route = route_ref[...]
    oh1 = lane == route[:, 0:1].astype(jnp.int32)
    oh2 = lane == route[:, 1:2].astype(jnp.int32)

    counts = counts_ref[0:1, :]
    padded = jnp.floor((counts + (EXPERT_ROWS - 1)) * (1.0 / EXPERT_ROWS)) * EXPERT_ROWS
    upper = (_iota((LANES, LANES), 0) <= _iota((LANES, LANES), 1)).astype(BF16)
    pend = _dot3_lhs(jnp.broadcast_to(padded, (8, LANES)), upper)[0:1, :]
    pstart = pend - padded
    d1 = jnp.sum(jnp.where(oh1, pstart, 0.0), axis=-1, keepdims=True) + route[:, 4:5]
    d2 = jnp.sum(jnp.where(oh2, pstart, 0.0), axis=-1, keepdims=True) + route[:, 5:6]
    dest_ref[...] = jnp.where(lane == 0, d1, jnp.where(lane == 1, d2, 0.0)).astype(jnp.int32)
    pend_ref[...] = jnp.broadcast_to(pend, (8, LANES)).astype(jnp.int32)


def _slots(route, counts):
    T = route.shape[0]
    tm = SLOT_TM
    return pl.pallas_call(
        _slots_kernel,
        grid=(T // tm,),
        in_specs=[pl.BlockSpec((tm, LANES), lambda i: (i, 0)), pl.BlockSpec((8, LANES), lambda i: (0, 0))],
        out_specs=[pl.BlockSpec((tm, LANES), lambda i: (i, 0)),
                   pl.BlockSpec((8, LANES), lambda i: (0, 0))],
        out_shape=[jax.ShapeDtypeStruct((T, LANES), jnp.int32), jax.ShapeDtypeStruct((8, LANES), jnp.int32)],
        compiler_params=pltpu.CompilerParams(dimension_semantics=("arbitrary",), vmem_limit_bytes=VMEM_LIMIT),
        name="slots",
    )(route, counts)


def _dispatch_kernel(pend_ref, dest_ref, x_ref, base_ref, p_ref, wpg_ref, bpg_ref, wpp_ref, xs_ref, resid_ref, zero_ref,
                     sem, zsem):
    tm = dest_ref.shape[0] // TOP_K

    @pl.when(pl.program_id(0) == 0)
    def _():
        zero_ref[...] = jnp.zeros_like(zero_ref)

        def tail(e):
            start = pl.multiple_of(jnp.maximum(pend_ref[e] - EXPERT_ROWS, 0), EXPERT_ROWS)
            return pltpu.make_async_copy(zero_ref, xs_ref.at[pl.ds(start, EXPERT_ROWS)], zsem)

        def unused(j):
            return pltpu.make_async_copy(
                zero_ref, xs_ref.at[pl.ds(pl.multiple_of(j * EXPERT_ROWS, EXPERT_ROWS), EXPERT_ROWS)], zsem)

        def start_unused(j, _):
            unused(j).start()
            return 0

        def wait_unused(j, _):
            unused(j).wait()
            return 0

        first_unused = pend_ref[N_EXPERTS - 1] // EXPERT_ROWS
        n_blocks = xs_ref.shape[0] // EXPERT_ROWS
        for e in range(N_EXPERTS):
            tail(e).start()
        lax.fori_loop(first_unused, n_blocks, start_unused, 0)
        for e in range(N_EXPERTS):
            tail(e).wait()
        lax.fori_loop(first_unused, n_blocks, wait_unused, 0)

    for t in range(tm):
        for j in range(TOP_K):
            pltpu.make_async_copy(x_ref.at[pl.ds(t, 1)], xs_ref.at[pl.ds(dest_ref[TOP_K * t + j], 1)],
                                  sem).start(priority=j)

    xw = x_ref[...]
    x_lo = lax.bitcast_convert_type(xw << 16, F32)
    x_hi = lax.bitcast_convert_type(xw & jnp.uint32(0xFFFF0000), F32)
    x1b = jnp.concatenate([x_lo, x_hi], axis=1).astype(BF16)
    gate = _sigmoid(jnp.dot(x1b, wpg_ref[...], preferred_element_type=F32) + bpg_ref[...])
    ple = gate * jnp.dot(p_ref[...].astype(BF16), wpp_ref[...], preferred_element_type=F32)
    resid_ref[...] = base_ref[...] + ple

    for j in range(TOP_K):
        pltpu.make_async_copy(x_ref, xs_ref.at[pl.ds(0, tm)], sem).wait()


def _dispatch(pend, dest_flat, x1, base, p, wpg, bpg, wpp, n_rows):
    T, width = x1.shape
    tm = DISPATCH_TM
    const = lambda shape: pl.BlockSpec(shape, lambda i, pe: (0,) * len(shape))
    tile = lambda w: pl.BlockSpec((tm, w), lambda i, pe: (i, 0))
    return pl.pallas_call(
        _dispatch_kernel,
        grid_spec=pltpu.PrefetchScalarGridSpec(
            num_scalar_prefetch=1,
            grid=(T // tm,),
            in_specs=[pl.BlockSpec((TOP_K * tm,), lambda i, pe: (i,), memory_space=pltpu.SMEM),
                      tile(width), tile(D_MODEL), tile(D_PLE),
                      const((D_MODEL, D_MODEL)), const((1, D_MODEL)), const((D_PLE, D_MODEL))],
            out_specs=[pl.BlockSpec(memory_space=pl.ANY), tile(D_MODEL)],
            scratch_shapes=[pltpu.VMEM((EXPERT_ROWS, width), x1.dtype), pltpu.SemaphoreType.DMA,
                            pltpu.SemaphoreType.DMA],
        ),
        out_shape=[jax.ShapeDtypeStruct((n_rows, width), x1.dtype), jax.ShapeDtypeStruct((T, D_MODEL), F32)],
        compiler_params=pltpu.CompilerParams(dimension_semantics=("arbitrary",), vmem_limit_bytes=VMEM_LIMIT),
        name="dispatch",
    )(pend, dest_flat, x1, base, p, wpg, bpg, wpp)


def _experts_kernel(pend_ref, xs_ref, wg_ref, wu_ref, wd_ref, ys_ref, xbuf_ref, ybuf_ref, wgu_ref, wdb_ref,
                    in_sem, out_sem):
    rows = EXPERT_ROWS
    e = pl.program_id(0)
    first = jnp.where(e == 0, 0, pend_ref[jnp.maximum(e - 1, 0)]) // rows
    last = pend_ref[e] // rows
    n_used = pend_ref[N_EXPERTS - 1] // rows

    def block_rows(ref, b):
        return ref.at[pl.ds(pl.multiple_of(b * rows, rows), rows)]

    depth = xbuf_ref.shape[0]
    row_priority = 1

    def x_copy(b):
        slot = b % depth
        return pltpu.make_async_copy(block_rows(xs_ref, b), xbuf_ref.at[slot], in_sem.at[slot])

    def y_copy(b):
        slot = b % depth
        return pltpu.make_async_copy(ybuf_ref.at[slot], block_rows(ys_ref, b), out_sem.at[slot])

    @pl.when(e == 0)
    def _():
        for ahead in range(depth - 1):
            @pl.when(ahead < n_used)
            def _():
                x_copy(ahead).start(priority=row_priority)

    @pl.when(last > first)
    def _():
        wgu_ref[:, :D_EXPERT] = wg_ref[0].astype(BF16)
        wgu_ref[:, D_EXPERT:] = wu_ref[0].astype(BF16)
        wdb_ref[...] = wd_ref[0].astype(BF16)

        def body(b, _):
            slot = b % depth

            @pl.when(b + depth - 1 < n_used)
            def _():
                x_copy(b + depth - 1).start(priority=row_priority)

            x_copy(b).wait()

            @pl.when(b >= depth)
            def _():
                y_copy(b - depth).wait()

            xw = xbuf_ref[slot]
            x_lo = lax.bitcast_convert_type(xw << 16, F32)
            x_hi = lax.bitcast_convert_type(xw & jnp.uint32(0xFFFF0000), F32)
            xb = jnp.concatenate([x_lo, x_hi], axis=1).astype(BF16)
            h = jnp.dot(xb, wgu_ref[...], preferred_element_type=F32)
            hg = h[:, :D_EXPERT]
            hid = hg * _sigmoid(hg) * h[:, D_EXPERT:]
            ybuf_ref[slot] = jnp.dot(hid.astype(BF16), wdb_ref[...], preferred_element_type=F32)
            y_copy(b).start(priority=row_priority)
            return 0

        lax.fori_loop(first, last, body, 0)

    @pl.when(e == N_EXPERTS - 1)
    def _():
        for back in range(depth, 0, -1):
            @pl.when(n_used >= back)
            def _():
                y_copy(n_used - back).wait()

        ybuf_ref[0] = jnp.zeros(ybuf_ref.shape[1:], F32)

        def unused(b):
            return pltpu.make_async_copy(ybuf_ref.at[0], block_rows(ys_ref, b), out_sem.at[0])

        def start_unused(b, _):
            unused(b).start()
            return 0

        def wait_unused(b, _):
            unused(b).wait()
            return 0

        n_blocks = ys_ref.shape[0] // rows
        lax.fori_loop(n_used, n_blocks, start_unused, 0)
        lax.fori_loop(n_used, n_blocks, wait_unused, 0)


def _experts(pend, xs, wg, wu, wd):
    n_rows = xs.shape[0]
    rows = EXPERT_ROWS
    wspec = lambda shape: pl.BlockSpec((1,) + shape, lambda e, pe: (e, 0, 0))
    return pl.pallas_call(
        _experts_kernel,
        grid_spec=pltpu.PrefetchScalarGridSpec(
            num_scalar_prefetch=1,
            grid=(N_EXPERTS,),
            in_specs=[pl.BlockSpec(memory_space=pl.ANY),
                      wspec((D_MODEL, D_EXPERT)), wspec((D_MODEL, D_EXPERT)), wspec((D_EXPERT, D_MODEL))],
            out_specs=pl.BlockSpec(memory_space=pl.ANY),
            scratch_shapes=[pltpu.VMEM((EXPERT_DEPTH, rows, D_MODEL // 2), jnp.uint32),
                            pltpu.VMEM((EXPERT_DEPTH, rows, D_MODEL), F32),
                            pltpu.VMEM((D_MODEL, 2 * D_EXPERT), BF16), pltpu.VMEM((D_EXPERT, D_MODEL), BF16),
                            pltpu.SemaphoreType.DMA((EXPERT_DEPTH,)), pltpu.SemaphoreType.DMA((EXPERT_DEPTH,))],
        ),
        out_shape=jax.ShapeDtypeStruct((n_rows, D_MODEL), F32),
        compiler_params=pltpu.CompilerParams(dimension_semantics=("arbitrary",), vmem_limit_bytes=VMEM_LIMIT),
        name="experts",
    )(pend, xs, wg, wu, wd)


def _combine_kernel(dest_ref, dest_next_ref, ys_ref, resid_ref, route_ref, lg_ref, lb_ref, o_ref, buf_ref, sem):
    tm = buf_ref.shape[2]
    i = pl.program_id(0)

    def gather(dref, offset, s):
        for t in range(tm):
            for j in range(TOP_K):
                pltpu.make_async_copy(ys_ref.at[pl.ds(dref[offset + TOP_K * t + j], 1)],
                                      buf_ref.at[s, j, pl.ds(t, 1)], sem.at[s]).start(priority=j)

    def drain(s):
        for j in range(TOP_K):
            pltpu.make_async_copy(ys_ref.at[pl.ds(0, tm)], buf_ref.at[s, j], sem.at[s]).wait()

    def finish(s):
        rows = slice(s * tm, (s + 1) * tm)
        drain(s)
        route = route_ref[rows, :]
        ffn = buf_ref[s, 0] * route[:, 2:3] + buf_ref[s, 1] * route[:, 3:4]
        o_ref[rows, :] = _layer_norm(resid_ref[rows, :] + ffn, lg_ref[...], lb_ref[...], LN_EPS)

    @pl.when(i == 0)
    def _():
        gather(dest_ref, 0, 0)

    gather(dest_ref, TOP_K * tm, 1)
    finish(0)
    gather(dest_next_ref, 0, 0)
    finish(1)

    @pl.when(i == pl.num_programs(0) - 1)
    def _():
        drain(0)


def _combine(dest_flat, ys, resid, route, l2g, l2b):
    T = resid.shape[0]
    tm = COMBINE_TM
    nt = T // tm
    tile = lambda w: pl.BlockSpec((2 * tm, w), lambda i: (i, 0))
    const = lambda shape: pl.BlockSpec(shape, lambda i: (0,) * len(shape))
    return pl.pallas_call(
        _combine_kernel,
        grid=(nt // 2,),
        in_specs=[pl.BlockSpec((2 * TOP_K * tm,), lambda i: (i,), memory_space=pltpu.SMEM),
                  pl.BlockSpec((TOP_K * tm,), lambda i: (jnp.minimum(2 * i + 2, nt - 1),), memory_space=pltpu.SMEM),
                  pl.BlockSpec(memory_space=pl.ANY),
                  tile(D_MODEL), tile(LANES), const((1, D_MODEL)), const((1, D_MODEL))],
        out_specs=pl.BlockSpec((2 * tm, D_MODEL), lambda i: (i, 0)),
        out_shape=jax.ShapeDtypeStruct((T, D_MODEL), F32),
        scratch_shapes=[pltpu.VMEM((2, TOP_K, tm, D_MODEL), F32), pltpu.SemaphoreType.DMA((2,))],
        compiler_params=pltpu.CompilerParams(dimension_semantics=("arbitrary",), vmem_limit_bytes=VMEM_LIMIT),
        name="combine",
    )(dest_flat, dest_flat, ys, resid, route, l2g, l2b)


def _block_diag_const(n, blk, val):
    idx = jnp.arange(n) // blk
    return jnp.where(idx[:, None] == idx[None, :], val, 0.0).astype(BF16)


def kernel(x, p, ln_emb_g, ln_emb_b, w_in, mu_shift, w0, w_decay_up, a0, w_iclr_up, w_gate_up, k_k, k_a, r_k, gn_g, gn_b, gmlp_ln_g, gmlp_ln_b, w_spatial, b_spatial, w_out, ln1_g, ln1_b, w_group_router, b_group_router, w_expert_router, b_expert_router, w_exp_gate, w_exp_up, w_exp_down, w_ple_gate, b_ple_gate, w_ple_proj, ln2_g, ln2_b):
    B, S, D = x.shape
    T = B * S
    row = lambda t: t.reshape(1, -1).astype(F32)

    zl = jnp.zeros((DECAY_LORA, D_RWKV), F32)
    wwa = jnp.concatenate([jnp.concatenate([w_decay_up[0], zl], axis=1),
                           jnp.concatenate([zl, w_iclr_up[0]], axis=1)], axis=0).astype(BF16)
    w0a0 = jnp.concatenate([w0[0], a0[0]]).reshape(1, -1)
    eones = jnp.tile(_block_diag_const(2 * LANES, HEAD, 1.0), (2, 1))
    emean = jnp.tile(_block_diag_const(LANES, HEAD, 1.0 / HEAD), (2, 1))

    r, lw, k, v, a, b, g, bonus, yb = _prep(
        x, row(ln_emb_g), row(ln_emb_b), w_in[0].astype(BF16), row(mu_shift[0]), wwa, w0a0,
        w_gate_up[0].astype(BF16), row(k_k[0]), row(k_a[0]), row(r_k[0]), eones,
        row(gmlp_ln_g[0]), row(gmlp_ln_b[0]), w_spatial[0], b_spatial[0].T)

    ya = _wkv(r, lw, k, v, a, b, g, bonus, gn_g[0].reshape(N_PAIRS, 1, LANES), gn_b[0].reshape(N_PAIRS, 1, LANES),
              emean)

    wr = jnp.concatenate([w_expert_router[0].reshape(D, N_EXPERTS), w_group_router[0],
                          jnp.zeros((D, LANES - N_EXPERTS - N_GROUPS), F32)], axis=1)
    wr3 = jnp.stack(_split2(wr))
    br = jnp.concatenate([b_expert_router[0].reshape(-1), b_group_router[0],
                          jnp.zeros((LANES - N_EXPERTS - N_GROUPS,), F32)]).reshape(1, LANES)
    base, x1, route, counts = _mixer(x, row(ln_emb_g), row(ln_emb_b), ya, yb, w_out[0].astype(BF16), row(ln1_g[0]),
                                     row(ln1_b[0]), wr3, br)
    base = base.reshape(T, D)
    x1 = x1.reshape(T, D // 2)
    route = route.reshape(T, LANES)

    n_blocks = -(-(T * TOP_K) // EXPERT_ROWS) + N_EXPERTS
    dest, pend = _slots(route, counts)
    dest_flat = dest[:, :TOP_K].reshape(T * TOP_K)
    pend = pend[0, :N_EXPERTS]

    xs, resid = _dispatch(pend, dest_flat, x1, base, p[0].reshape(T, D_PLE), w_ple_gate[0].astype(BF16),
                          row(b_ple_gate[0]), w_ple_proj[0].astype(BF16), n_blocks * EXPERT_ROWS)
    ys = _experts(pend, xs, w_exp_gate[0], w_exp_up[0], w_exp_down[0])
    out = _combine(dest_flat, ys, resid, route, row(ln2_g[0]), row(ln2_b[0]))
    return out.reshape(B, S, D)
```

```python
import functools
import math

import jax
import jax.numpy as jnp
from jax import lax
from jax.experimental import pallas as pl
from jax.experimental.pallas import tpu as pltpu

F32 = jnp.float32
BF16 = jnp.bfloat16

D_MODEL = 1024
D_RWKV = 512
HEAD = 64
D_GMLP = 512
GMLP_GROUPS = 4
GROUP_W = 128
GCHUNK = 128
DECAY_LORA = 64
ICLR_LORA = 64
GATE_LORA = 128
N_SHIFT = 3 * D_RWKV + DECAY_LORA + ICLR_LORA + GATE_LORA
D_IN = N_SHIFT + 2 * D_GMLP
D_PLE = 256
N_GROUPS = 4
EXPERTS_PER_GROUP = 8
N_EXPERTS = 32
TOP_K = 2
D_EXPERT = 512
DEPTH = 1
ALPHA = (2.0 * DEPTH) ** 0.25
LN_EPS = 1e-5
GN_EPS = 64e-5
DECAY_SCALE = math.exp(-0.5)

LANES = 128
WKV_CHUNK = 64
N_PAIRS = D_RWKV // LANES
VMEM_LIMIT = 56 * 1024 * 1024

PREP_TM = 512
WKV_TB = 512
WKV_PAIRS = 4
MIX_TM = 512
SLOT_TM = 2048
EXPERT_ROWS = 256
EXPERT_DEPTH = 4
DISPATCH_TM = 512
COMBINE_TM = 256
NEG = -1e30


def _dot(a, b):
    return jnp.dot(a.astype(BF16), b.astype(BF16), preferred_element_type=F32)


def _dot_nt(a, b):
    return lax.dot_general(a.astype(BF16), b.astype(BF16), (((1,), (1,)), ((), ())),
                           preferred_element_type=F32)


def _split3(x):
    hi = x.astype(BF16)
    r1 = x - hi.astype(F32)
    mid = r1.astype(BF16)
    lo = (r1 - mid.astype(F32)).astype(BF16)
    return hi, mid, lo


def _dot3_lhs(x, w):
    hi, mid, lo = _split3(x)
    w = w.astype(BF16)
    return (jnp.dot(hi, w, preferred_element_type=F32) + jnp.dot(mid, w, preferred_element_type=F32)
            + jnp.dot(lo, w, preferred_element_type=F32))


def _split2(x):
    hi = x.astype(BF16)
    return hi, (x - hi.astype(F32)).astype(BF16)


def _dot2_lhs(x, w2):
    hi, lo = _split2(x)
    return jnp.dot(jnp.concatenate([hi, lo], axis=1), w2, preferred_element_type=F32)


def _dot3_rhs(w3, x):
    hi, mid, lo = _split3(x)
    return jnp.dot(w3, jnp.concatenate([hi, mid, lo], axis=0), preferred_element_type=F32)


def _layer_norm(x, g, b, eps):
    mu = jnp.mean(x, axis=-1, keepdims=True)
    xc = x - mu
    var = jnp.mean(xc * xc, axis=-1, keepdims=True)
    return xc * lax.rsqrt(var + eps) * g + b


def _sigmoid(x):
    return 1.0 / (1.0 + jnp.exp(-x))


def _iota(shape, dim):
    return lax.broadcasted_iota(jnp.int32, shape, dim)


def _prep_kernel(x_ref, lng_ref, lnb_ref, win_ref, mu_ref, wwa_ref, w0a0_ref, wg_ref, kk_ref, ka_ref, rk_ref,
                 eones_ref, glng_ref, glnb_ref, wsp_ref, bsp_ref,
                 r_ref, lw_ref, k_ref, v_ref, a_ref, b_ref, g_ref, bonus_ref, yb_ref, carry_ref):
    tm = x_ref.shape[1]

    @pl.when(pl.program_id(1) == 0)
    def _():
        carry_ref[...] = jnp.zeros_like(carry_ref)

    x0 = _layer_norm(x_ref[0], lng_ref[...], lnb_ref[...], LN_EPS)
    proj = jnp.dot(x0.astype(BF16), win_ref[...], preferred_element_type=F32)

    h = proj[:, :N_SHIFT]
    rolled = pltpu.roll(h, 1, 0)
    first = _iota((tm, N_SHIFT), 0) == 0
    prev = jnp.where(first, jnp.broadcast_to(carry_ref[0:1, :], (tm, N_SHIFT)), rolled)
    carry_ref[0:1, :] = h[tm - 1:tm, :]
    h = h + (prev - h) * mu_ref[...]

    r = h[:, 0:D_RWKV]
    k = h[:, D_RWKV:2 * D_RWKV]
    v = h[:, 2 * D_RWKV:3 * D_RWKV]
    xwa = h[:, 3 * D_RWKV:3 * D_RWKV + LANES]
    xg = h[:, 3 * D_RWKV + LANES:N_SHIFT]

    lane = _iota((tm, LANES), 1)
    twa = jnp.where(lane < DECAY_LORA, jnp.tanh(xwa), xwa)
    da = _dot(twa, wwa_ref[...]) + w0a0_ref[...]
    logw = -DECAY_SCALE * _sigmoid(da[:, :D_RWKV])
    ag = _sigmoid(da[:, D_RWKV:])
    g = _dot(_sigmoid(xg), wg_ref[...])

    eones2 = eones_ref[...]

    def head_sum(t):
        half = 2 * LANES
        return jnp.concatenate([_dot2_lhs(t[:, :half], eones2), _dot2_lhs(t[:, half:], eones2)], axis=1)

    kk = k * kk_ref[...]
    kk = kk / jnp.maximum(jnp.sqrt(head_sum(kk * kk)), 1e-12)
    k = k * (1.0 + (ag - 1.0) * ka_ref[...])
    bonus = head_sum(r * k * rk_ref[...]) * v

    for p in range(N_PAIRS):
        sl = slice(p * LANES, (p + 1) * LANES)
        r_ref[0, p] = r[:, sl]
        lw_ref[0, p] = logw[:, sl]
        k_ref[0, p] = k[:, sl]
        v_ref[0, p] = v[:, sl]
        a_ref[0, p] = -kk[:, sl]
        b_ref[0, p] = (kk * ag)[:, sl]
        g_ref[0, p] = g[:, sl]
        bonus_ref[0, p] = bonus[:, sl]

    zin = proj[:, N_SHIFT:]
    z = 0.5 * zin * (1.0 + lax.erf(zin * (0.5 ** 0.5)))
    zu = z[:, :D_GMLP]
    zv = z[:, D_GMLP:]
    causal = _iota((GCHUNK, GCHUNK), 0) >= _iota((GCHUNK, GCHUNK), 1)
    for gi in range(GMLP_GROUPS):
        gs = slice(gi * GROUP_W, (gi + 1) * GROUP_W)
        zvn = _layer_norm(zv[:, gs], glng_ref[:, gs], glnb_ref[:, gs], LN_EPS)
        ws = jnp.where(causal, wsp_ref[gi], 0.0).astype(BF16)
        bcol = bsp_ref[:, gi:gi + 1]
        for c in range(tm // GCHUNK):
            ts = slice(c * GCHUNK, (c + 1) * GCHUNK)
            mixed = jnp.dot(ws, zvn[ts].astype(BF16), preferred_element_type=F32) + bcol
            yb_ref[0, ts, gs] = (zu[ts, gs] * mixed).astype(BF16)


def _prep(x, ln_g, ln_b, w_in, mu, wwa, w0a0, wg, k_k, k_a, r_k, eones, glng, glnb, wsp, bsp):
    B, S, _ = x.shape
    tm = PREP_TM
    const = lambda shape: pl.BlockSpec(shape, lambda b, s: (0,) * len(shape))
    pair_spec = pl.BlockSpec((1, N_PAIRS, tm, LANES), lambda b, s: (b, 0, s, 0))
    pair_shape = jax.ShapeDtypeStruct((B, N_PAIRS, S, LANES), F32)
    return pl.pallas_call(
        _prep_kernel,
        grid=(B, S // tm),
        in_specs=[
            pl.BlockSpec((1, tm, D_MODEL), lambda b, s: (b, s, 0)),
            const((1, D_MODEL)), const((1, D_MODEL)), const((D_MODEL, D_IN)), const((1, N_SHIFT)),
            const((LANES, 2 * D_RWKV)), const((1, 2 * D_RWKV)), const((GATE_LORA, D_RWKV)),
            const((1, D_RWKV)), const((1, D_RWKV)), const((1, D_RWKV)), const((4 * LANES, 2 * LANES)),
            const((1, D_GMLP)), const((1, D_GMLP)), const((GMLP_GROUPS, GCHUNK, GCHUNK)),
            const((GCHUNK, GMLP_GROUPS)),
        ],
        out_specs=[pair_spec] * 8 + [pl.BlockSpec((1, tm, D_GMLP), lambda b, s: (b, s, 0))],
        out_shape=[pair_shape] * 8 + [jax.ShapeDtypeStruct((B, S, D_GMLP), BF16)],
        scratch_shapes=[pltpu.VMEM((8, N_SHIFT), F32)],
        compiler_params=pltpu.CompilerParams(dimension_semantics=("arbitrary", "arbitrary"),
                                             vmem_limit_bytes=VMEM_LIMIT),
        name="prep",
    )(x, ln_g, ln_b, w_in, mu, wwa, w0a0, wg, k_k, k_a, r_k, eones, glng, glnb, wsp, bsp)


def _wkv_kernel(r_ref, lw_ref, k_ref, v_ref, a_ref, b_ref, g_ref, bonus_ref, gng_ref, gnb_ref, emean_ref,
                o_ref, h_ref):
    C = WKV_CHUNK
    tb = r_ref.shape[2]

    @pl.when(pl.program_id(2) == 0)
    def _():
        h_ref[...] = jnp.zeros_like(h_ref)

    tok = _iota((C, LANES), 0)
    lane = _iota((C, LANES), 1)
    head0 = lane < HEAD
    strict = tok > lane % HEAD
    incl = tok >= lane % HEAD
    eye_w = (tok == lane % HEAD).astype(F32)
    rr = _iota((LANES, LANES), 0)
    cc = _iota((LANES, LANES), 1)
    eye = (rr == cc).astype(F32)
    same_head = (rr < HEAD) == (cc < HEAD)
    ltri3 = (_iota((C, 3 * C), 0) >= _iota((C, 3 * C), 1) % C).astype(BF16)

    def stack(x):
        xb = x.astype(BF16)
        zero = jnp.zeros_like(xb)
        return jnp.concatenate([jnp.where(head0, xb, zero), jnp.where(head0, zero, xb)], axis=0)

    def stack2(x, y):
        return jnp.concatenate([stack(x), stack(y)], axis=1)

    n_pairs = r_ref.shape[1]
    n_chunks = tb // C
    units = [(q, c) for q in range(n_pairs) for c in range(n_chunks)]

    def load(ref):
        return [ref[0, q, c * C:(c + 1) * C, :] for q, c in units]

    r_, lw_, k_, v_, a_, b_ = (load(ref) for ref in (r_ref, lw_ref, k_ref, v_ref, a_ref, b_ref))
    cum_ = [_dot3_rhs(ltri3, lw) for lw in lw_]
    cend_ = [cum[C - 1:C, :] for cum in cum_]
    at_ = [a * jnp.exp(cum - lw) for a, cum, lw in zip(a_, cum_, lw_)]
    rt_ = [r * jnp.exp(cum) for r, cum in zip(r_, cum_)]
    ginv_ = [jnp.exp(-cum) for cum in cum_]
    gend_ = [jnp.exp(cend - cum) for cend, cum in zip(cend_, cum_)]
    bk_end_ = [jnp.concatenate([b * ge, k * ge], axis=0) for b, k, ge in zip(b_, k_, gend_)]
    vst_ = [stack(v) for v in v_]

    G_ = [_dot_nt(jnp.concatenate([at, rt], axis=0), jnp.concatenate([stack(b * gi), stack(k * gi)], axis=0))
          for at, rt, b, k, gi in zip(at_, rt_, b_, k_, ginv_)]
    n1_ = [jnp.where(strict, G[:C, :LANES], 0.0) for G in G_]
    aak_ = [jnp.where(strict, G[:C, LANES:], 0.0) for G in G_]
    arb_ = [jnp.where(incl, G[C:, :LANES], 0.0) for G in G_]
    ark_ = [jnp.where(incl, G[C:, LANES:], 0.0) for G in G_]
    av_ = [_dot(jnp.concatenate([aak, ark], axis=0), vst) for aak, ark, vst in zip(aak_, ark_, vst_)]

    s1_ = [stack(n1) for n1 in n1_]
    n2_ = [_dot(n1, s1) for n1, s1 in zip(n1_, s1_)]
    x_ = [_dot(n2, jnp.concatenate([s1, stack(n2)], axis=1)) for n2, s1 in zip(n2_, s1_)]
    t_ = [eye_w + n1 + n2 + x[:, :LANES] for n1, n2, x in zip(n1_, n2_, x_)]
    np_ = [x[:, LANES:] for x in x_]
    for _ in range(3):
        x_ = [_dot(npow, stack2(t, npow)) for t, npow in zip(t_, np_)]
        t_ = [t + x[:, :LANES] for t, x in zip(t_, x_)]
        np_ = [x[:, LANES:] for x in x_]
    t_ = [t + _dot(npow, stack(t)) for t, npow in zip(t_, np_)]

    x_ = [_dot(t, stack2(at, av[:C])) for t, at, av in zip(t_, at_, av_)]
    z_ = [_dot(arb, stack2(x[:, :LANES], x[:, LANES:])) for arb, x in zip(arb_, x_)]
    rp_ = [rt + z[:, :LANES] for rt, z in zip(rt_, z_)]
    p3_ = [z[:, LANES:] + av[C:] for z, av in zip(z_, av_)]
    rhs_ = [jnp.concatenate([x, jnp.concatenate([jnp.zeros_like(v), v], axis=1)], axis=0)
            for x, v in zip(x_, v_)]
    mq_ = [_dot(bk_end.T, rhs) for bk_end, rhs in zip(bk_end_, rhs_)]
    m_ = [eye * jnp.exp(cend) + jnp.where(same_head, mq[:, :LANES], 0.0) for cend, mq in zip(cend_, mq_)]
    q_ = [jnp.where(same_head, mq[:, LANES:], 0.0) for mq in mq_]

    pairs = range(n_pairs)
    H_ = [h_ref[q] for q in pairs]
    ys_ = [[] for _ in pairs]
    for c in range(n_chunks):
        for q in pairs:
            u = q * n_chunks + c
            ys_[q].append(_dot(rp_[u], H_[q]) + p3_[u])
        H_ = [_dot(m_[q * n_chunks + c], H_[q]) + q_[q * n_chunks + c] for q in pairs]
    for q in pairs:
        h_ref[q] = H_[q]

    emean = emean_ref[...]
    y_ = [jnp.concatenate(ys, axis=0) for ys in ys_]
    mu_ = [_dot2_lhs(y, emean) for y in y_]
    yc_ = [y - mu for y, mu in zip(y_, mu_)]
    var_ = [_dot2_lhs(yc * yc, emean) for yc in yc_]
    for q in pairs:
        yn = yc_[q] * lax.rsqrt(var_[q] + GN_EPS) * gng_ref[q] + gnb_ref[q]
        o_ref[0, q] = ((yn + bonus_ref[0, q]) * g_ref[0, q]).astype(BF16)


def _wkv(r, lw, k, v, a, b, g, bonus, gn_g, gn_b, emean):
    B, P, S, _ = r.shape
    tb = WKV_TB
    pp = WKV_PAIRS
    seq = pl.BlockSpec((1, pp, tb, LANES), lambda bi, p, s: (bi, p, s, 0))
    par = pl.BlockSpec((pp, 1, LANES), lambda bi, p, s: (p, 0, 0))
    return pl.pallas_call(
        _wkv_kernel,
        grid=(B, P // pp, S // tb),
        in_specs=[seq] * 8 + [par, par, pl.BlockSpec((2 * LANES, LANES), lambda bi, p, s: (0, 0))],
        out_specs=seq,
        out_shape=jax.ShapeDtypeStruct((B, P, S, LANES), BF16),
        scratch_shapes=[pltpu.VMEM((pp, LANES, LANES), F32)],
        compiler_params=pltpu.CompilerParams(dimension_semantics=("arbitrary", "arbitrary", "arbitrary"),
                                             vmem_limit_bytes=VMEM_LIMIT),
        name="wkv",
    )(r, lw, k, v, a, b, g, bonus, gn_g, gn_b, emean)


def _mixer_kernel(x_ref, lng_ref, lnb_ref, ya_ref, yb_ref, wout_ref, l1g_ref, l1b_ref, wr_ref, br_ref, below_ref,
                  base_ref, x1_ref, route_ref, counts_ref, carry_ref):
    tm = x_ref.shape[1]

    @pl.when((pl.program_id(0) == 0) & (pl.program_id(1) == 0))
    def _():
        carry_ref[...] = jnp.zeros_like(carry_ref)

    x0 = _layer_norm(x_ref[0], lng_ref[...], lnb_ref[...], LN_EPS)
    ymix = jnp.concatenate([ya_ref[0, p] for p in range(N_PAIRS)] + [yb_ref[0]], axis=-1)
    mix = jnp.dot(ymix, wout_ref[...], preferred_element_type=F32)
    x1 = _layer_norm(ALPHA * x0 + mix, l1g_ref[...], l1b_ref[...], LN_EPS)
    x1b = x1.astype(BF16)
    half = D_MODEL // 2
    lo_bits = lax.bitcast_convert_type(x1b[:, :half].astype(F32), jnp.uint32)
    hi_bits = lax.bitcast_convert_type(x1b[:, half:].astype(F32), jnp.uint32)
    x1_ref[0] = (hi_bits & jnp.uint32(0xFFFF0000)) | (lo_bits >> 16)

    hi, mid = _split2(x1)
    whi = wr_ref[0]
    wmid = wr_ref[1]
    d = lambda u, w: jnp.dot(u, w, preferred_element_type=F32)
    logits = (d(hi, whi) + d(hi, wmid) + d(mid, whi)) + br_ref[...]
    lane = _iota((tm, LANES), 1).astype(F32)
    far = float(4 * LANES)
    is_g = jnp.where(lane >= N_EXPERTS, jnp.where(lane < N_EXPERTS + N_GROUPS, 1.0, 0.0), 0.0) > 0.5
    gl = jnp.where(is_g, logits, NEG)
    gmax = jnp.max(gl, axis=-1, keepdims=True)
    gsel = jnp.min(jnp.where(gl == gmax, lane, far), axis=-1, keepdims=True) - N_EXPERTS
    p_group = 1.0 / jnp.sum(jnp.where(is_g, jnp.exp(gl - gmax), 0.0), axis=-1, keepdims=True)
    grp_of_lane = jnp.floor(lane * (1.0 / EXPERTS_PER_GROUP))
    el = jnp.where(grp_of_lane == gsel, logits, NEG)
    v1 = jnp.max(el, axis=-1, keepdims=True)
    i1 = jnp.min(jnp.where(el == v1, lane, far), axis=-1, keepdims=True)
    el2 = jnp.where(lane == i1, NEG, el)
    v2 = jnp.max(el2, axis=-1, keepdims=True)
    i2 = jnp.min(jnp.where(el2 == v2, lane, far), axis=-1, keepdims=True)
    e21 = jnp.exp(v2 - v1)
    w1 = p_group / (1.0 + e21)
    w2 = p_group * e21 / (1.0 + e21)

    oh1 = lane == i1
    oh2 = lane == i2
    below = below_ref[...]
    o1 = jnp.where(oh1, 1.0, 0.0)
    o2 = jnp.where(oh2, 1.0, 0.0)
    c1 = jnp.dot(below, o1.astype(BF16), preferred_element_type=F32)
    c2 = jnp.dot(below, o2.astype(BF16), preferred_element_type=F32)
    tot1 = jnp.sum(o1, axis=0, keepdims=True)
    carry = carry_ref[0:1, :]
    rank1 = jnp.sum(jnp.where(oh1, c1 + carry, 0.0), axis=-1, keepdims=True)
    rank2 = jnp.sum(jnp.where(oh2, c2 + carry + tot1, 0.0), axis=-1, keepdims=True)
    carry = carry + tot1 + jnp.sum(o2, axis=0, keepdims=True)
    carry_ref[0:1, :] = carry
    counts_ref[...] = jnp.broadcast_to(carry, counts_ref.shape)

    fields = (i1, i2, w1, w2, rank1, rank2)
    route = jnp.zeros((tm, LANES), F32)
    for n, f in enumerate(fields):
        route = jnp.where(lane == n, f, route)
    route_ref[0] = route

    base_ref[0] = ALPHA * x1


def _mixer(x, ln_g, ln_b, ya, yb, w_out, l1g, l1b, wr3, br):
    B, S, _ = x.shape
    tm = MIX_TM
    const = lambda shape: pl.BlockSpec(shape, lambda b, s: (0,) * len(shape))
    row = lambda w: pl.BlockSpec((1, tm, w), lambda b, s: (b, s, 0))
    below = (jnp.arange(tm)[:, None] > jnp.arange(tm)[None, :]).astype(BF16)
    return pl.pallas_call(
        _mixer_kernel,
        grid=(B, S // tm),
        in_specs=[
            row(D_MODEL), const((1, D_MODEL)), const((1, D_MODEL)),
            pl.BlockSpec((1, N_PAIRS, tm, LANES), lambda b, s: (b, 0, s, 0)), row(D_GMLP),
            const((D_MODEL, D_MODEL)), const((1, D_MODEL)), const((1, D_MODEL)),
            const((2, D_MODEL, LANES)), const((1, LANES)), const((tm, tm)),
        ],
        out_specs=[row(D_MODEL), row(D_MODEL // 2), row(LANES), const((8, LANES))],
        out_shape=[jax.ShapeDtypeStruct((B, S, D_MODEL), F32), jax.ShapeDtypeStruct((B, S, D_MODEL // 2), jnp.uint32),
                   jax.ShapeDtypeStruct((B, S, LANES), F32), jax.ShapeDtypeStruct((8, LANES), F32)],
        scratch_shapes=[pltpu.VMEM((8, LANES), F32)],
        compiler_params=pltpu.CompilerParams(dimension_semantics=("arbitrary", "arbitrary"),
                                             vmem_limit_bytes=VMEM_LIMIT),
        name="mixer",
    )(x, ln_g, ln_b, ya, yb, w_out, l1g, l1b, wr3, br, below)


def _slots_kernel(route_ref, counts_ref, dest_ref, pend_ref):
    tm = route_ref.shape[0]
    lane = _iota((tm, LANES), 1)
    route = route_ref[...]
    oh1 = lane == route[:, 0:1].astype(jnp.int32)
    oh2 = lane == route[:, 1:2].astype(jnp.int32)

    counts = counts_ref[0:1, :]
    padded = jnp.floor((counts + (EXPERT_ROWS - 1)) * (1.0 / EXPERT_ROWS)) * EXPERT_ROWS
    upper = (_iota((LANES, LANES), 0) <= _iota((LANES, LANES), 1)).astype(BF16)
    pend = _dot3_lhs(jnp.broadcast_to(padded, (8, LANES)), upper)[0:1, :]
    pstart = pend - padded
    d1 = jnp.sum(jnp.where(oh1, pstart, 0.0), axis=-1, keepdims=True) + route[:, 4:5]
    d2 = jnp.sum(jnp.where(oh2, pstart, 0.0), axis=-1, keepdims=True) + route[:, 5:6]
    dest = jnp.where(lane == 0, d1, jnp.where(lane == 1, d2, 0.0))
    dest_ref[...] = jnp.transpose(dest)[0:dest_ref.shape[0], :].astype(jnp.int32)
    pend_ref[...] = jnp.broadcast_to(pend, (8, LANES)).astype(jnp.int32)


def _slots(route, counts):
    T = route.shape[0]
    tm = SLOT_TM
    return pl.pallas_call(
        _slots_kernel,
        grid=(T // tm,),
        in_specs=[pl.BlockSpec((tm, LANES), lambda i: (i, 0)), pl.BlockSpec((8, LANES), lambda i: (0, 0))],
        out_specs=[pl.BlockSpec((8, tm), lambda i: (0, i)),
                   pl.BlockSpec((8, LANES), lambda i: (0, 0))],
        out_shape=[jax.ShapeDtypeStruct((8, T), jnp.int32), jax.ShapeDtypeStruct((8, LANES), jnp.int32)],
        compiler_params=pltpu.CompilerParams(dimension_semantics=("arbitrary",), vmem_limit_bytes=VMEM_LIMIT),
        name="slots",
    )(route, counts)


def _dispatch_kernel(pend_ref, dest0_ref, dest1_ref, x_ref, base_ref, p_ref, wpg_ref, bpg_ref, wpp_ref, xs_ref, resid_ref,
                     zero_ref, sem, zsem):
    tm = x_ref.shape[0]
    dest_refs = (dest0_ref, dest1_ref)

    @pl.when(pl.program_id(0) == 0)
    def _():
        zero_ref[...] = jnp.zeros_like(zero_ref)

        def tail(e):
            start = pl.multiple_of(jnp.maximum(pend_ref[e] - EXPERT_ROWS, 0), EXPERT_ROWS)
            return pltpu.make_async_copy(zero_ref, xs_ref.at[pl.ds(start, EXPERT_ROWS)], zsem)

        def unused(j):
            return pltpu.make_async_copy(
                zero_ref, xs_ref.at[pl.ds(pl.multiple_of(j * EXPERT_ROWS, EXPERT_ROWS), EXPERT_ROWS)], zsem)

        def start_unused(j, _):
            unused(j).start()
            return 0

        def wait_unused(j, _):
            unused(j).wait()
            return 0

        first_unused = pend_ref[N_EXPERTS - 1] // EXPERT_ROWS
        n_blocks = xs_ref.shape[0] // EXPERT_ROWS
        for e in range(N_EXPERTS):
            tail(e).start()
        lax.fori_loop(first_unused, n_blocks, start_unused, 0)
        for e in range(N_EXPERTS):
            tail(e).wait()
        lax.fori_loop(first_unused, n_blocks, wait_unused, 0)

    for t in range(tm):
        for j in range(TOP_K):
            pltpu.make_async_copy(x_ref.at[pl.ds(t, 1)], xs_ref.at[pl.ds(dest_refs[j][t], 1)],
                                  sem).start(priority=j)

    xw = x_ref[...]
    x_lo = lax.bitcast_convert_type(xw << 16, F32)
    x_hi = lax.bitcast_convert_type(xw & jnp.uint32(0xFFFF0000), F32)
    x1b = jnp.concatenate([x_lo, x_hi], axis=1).astype(BF16)
    gate = _sigmoid(jnp.dot(x1b, wpg_ref[...], preferred_element_type=F32) + bpg_ref[...])
    ple = gate * jnp.dot(p_ref[...].astype(BF16), wpp_ref[...], preferred_element_type=F32)
    resid_ref[...] = base_ref[...] + ple

    for j in range(TOP_K):
        pltpu.make_async_copy(x_ref, xs_ref.at[pl.ds(0, tm)], sem).wait()


def _dispatch(pend, dests, x1, base, p, wpg, bpg, wpp, n_rows):
    T, width = x1.shape
    tm = DISPATCH_TM
    const = lambda shape: pl.BlockSpec(shape, lambda i, pe: (0,) * len(shape))
    tile = lambda w: pl.BlockSpec((tm, w), lambda i, pe: (i, 0))
    index_list = pl.BlockSpec((tm,), lambda i, pe: (i,), memory_space=pltpu.SMEM)
    return pl.pallas_call(
        _dispatch_kernel,
        grid_spec=pltpu.PrefetchScalarGridSpec(
            num_scalar_prefetch=1,
            grid=(T // tm,),
            in_specs=[index_list, index_list,
                      tile(width), tile(D_MODEL), tile(D_PLE),
                      const((D_MODEL, D_MODEL)), const((1, D_MODEL)), const((D_PLE, D_MODEL))],
            out_specs=[pl.BlockSpec(memory_space=pl.ANY), tile(D_MODEL)],
            scratch_shapes=[pltpu.VMEM((EXPERT_ROWS, width), x1.dtype), pltpu.SemaphoreType.DMA,
                            pltpu.SemaphoreType.DMA],
        ),
        out_shape=[jax.ShapeDtypeStruct((n_rows, width), x1.dtype), jax.ShapeDtypeStruct((T, D_MODEL), F32)],
        compiler_params=pltpu.CompilerParams(dimension_semantics=("arbitrary",), vmem_limit_bytes=VMEM_LIMIT),
        name="dispatch",
    )(pend, dests[0], dests[1], x1, base, p, wpg, bpg, wpp)


def _experts_kernel(pend_ref, xs_ref, wg_ref, wu_ref, wd_ref, ys_ref, xbuf_ref, ybuf_ref, wgu_ref, wdb_ref,
                    in_sem, out_sem):
    rows = EXPERT_ROWS
    e = pl.program_id(0)
    first = jnp.where(e == 0, 0, pend_ref[jnp.maximum(e - 1, 0)]) // rows
    last = pend_ref[e] // rows
    n_used = pend_ref[N_EXPERTS - 1] // rows

    def block_rows(ref, b):
        return ref.at[pl.ds(pl.multiple_of(b * rows, rows), rows)]

    depth = xbuf_ref.shape[0]
    row_priority = 1

    def x_copy(b):
        slot = b % depth
        return pltpu.make_async_copy(block_rows(xs_ref, b), xbuf_ref.at[slot], in_sem.at[slot])

    def y_copy(b):
        slot = b % depth
        return pltpu.make_async_copy(ybuf_ref.at[slot], block_rows(ys_ref, b), out_sem.at[slot])

    @pl.when(e == 0)
    def _():
        for ahead in range(depth - 1):
            @pl.when(ahead < n_used)
            def _():
                x_copy(ahead).start(priority=row_priority)

    @pl.when(last > first)
    def _():
        wgu_ref[:, :D_EXPERT] = wg_ref[0].astype(BF16)
        wgu_ref[:, D_EXPERT:] = wu_ref[0].astype(BF16)
        wdb_ref[...] = wd_ref[0].astype(BF16)

        def body(b, _):
            slot = b % depth

            @pl.when(b + depth - 1 < n_used)
            def _():
                x_copy(b + depth - 1).start(priority=row_priority)

            x_copy(b).wait()

            @pl.when(b >= depth)
            def _():
                y_copy(b - depth).wait()

            xw = xbuf_ref[slot]
            x_lo = lax.bitcast_convert_type(xw << 16, F32)
            x_hi = lax.bitcast_convert_type(xw & jnp.uint32(0xFFFF0000), F32)
            xb = jnp.concatenate([x_lo, x_hi], axis=1).astype(BF16)
            h = jnp.dot(xb, wgu_ref[...], preferred_element_type=F32)
            hg = h[:, :D_EXPERT]
            hid = hg * _sigmoid(hg) * h[:, D_EXPERT:]
            ybuf_ref[slot] = jnp.dot(hid.astype(BF16), wdb_ref[...], preferred_element_type=F32)
            y_copy(b).start(priority=row_priority)
            return 0

        lax.fori_loop(first, last, body, 0)

    @pl.when(e == N_EXPERTS - 1)
    def _():
        for back in range(depth, 0, -1):
            @pl.when(n_used >= back)
            def _():
                y_copy(n_used - back).wait()

        ybuf_ref[0] = jnp.zeros(ybuf_ref.shape[1:], F32)

        def unused(b):
            return pltpu.make_async_copy(ybuf_ref.at[0], block_rows(ys_ref, b), out_sem.at[0])

        def start_unused(b, _):
            unused(b).start()
            return 0

        def wait_unused(b, _):
            unused(b).wait()
            return 0

        n_blocks = ys_ref.shape[0] // rows
        lax.fori_loop(n_used, n_blocks, start_unused, 0)
        lax.fori_loop(n_used, n_blocks, wait_unused, 0)


def _experts(pend, xs, wg, wu, wd):
    n_rows = xs.shape[0]
    rows = EXPERT_ROWS
    wspec = lambda shape: pl.BlockSpec((1,) + shape, lambda e, pe: (e, 0, 0))
    return pl.pallas_call(
        _experts_kernel,
        grid_spec=pltpu.PrefetchScalarGridSpec(
            num_scalar_prefetch=1,
            grid=(N_EXPERTS,),
            in_specs=[pl.BlockSpec(memory_space=pl.ANY),
                      wspec((D_MODEL, D_EXPERT)), wspec((D_MODEL, D_EXPERT)), wspec((D_EXPERT, D_MODEL))],
            out_specs=pl.BlockSpec(memory_space=pl.ANY),
            scratch_shapes=[pltpu.VMEM((EXPERT_DEPTH, rows, D_MODEL // 2), jnp.uint32),
                            pltpu.VMEM((EXPERT_DEPTH, rows, D_MODEL), F32),
                            pltpu.VMEM((D_MODEL, 2 * D_EXPERT), BF16), pltpu.VMEM((D_EXPERT, D_MODEL), BF16),
                            pltpu.SemaphoreType.DMA((EXPERT_DEPTH,)), pltpu.SemaphoreType.DMA((EXPERT_DEPTH,))],
        ),
        out_shape=jax.ShapeDtypeStruct((n_rows, D_MODEL), F32),
        compiler_params=pltpu.CompilerParams(dimension_semantics=("arbitrary",), vmem_limit_bytes=VMEM_LIMIT),
        name="experts",
    )(pend, xs, wg, wu, wd)


def _combine_kernel(dest0_ref, dest1_ref, dest0_next_ref, dest1_next_ref, ys_ref, resid_ref, route_ref, lg_ref, lb_ref,
                    o_ref, buf_ref, sem):
    tm = buf_ref.shape[2]
    i = pl.program_id(0)
    dest_refs = (dest0_ref, dest1_ref)
    dest_next_refs = (dest0_next_ref, dest1_next_ref)

    def gather(drefs, offset, s):
        for t in range(tm):
            for j in range(TOP_K):
                pltpu.make_async_copy(ys_ref.at[pl.ds(drefs[j][offset + t], 1)],
                                      buf_ref.at[s, j, pl.ds(t, 1)], sem.at[s]).start(priority=j)

    def drain(s):
        for j in range(TOP_K):
            pltpu.make_async_copy(ys_ref.at[pl.ds(0, tm)], buf_ref.at[s, j], sem.at[s]).wait()

    def finish(s):
        rows = slice(s * tm, (s + 1) * tm)
        drain(s)
        route = route_ref[rows, :]
        ffn = buf_ref[s, 0] * route[:, 2:3] + buf_ref[s, 1] * route[:, 3:4]
        o_ref[rows, :] = _layer_norm(resid_ref[rows, :] + ffn, lg_ref[...], lb_ref[...], LN_EPS)

    @pl.when(i == 0)
    def _():
        gather(dest_refs, 0, 0)

    gather(dest_refs, tm, 1)
    finish(0)
    gather(dest_next_refs, 0, 0)
    finish(1)

    @pl.when(i == pl.num_programs(0) - 1)
    def _():
        drain(0)


def _combine(dests, ys, resid, route, l2g, l2b):
    T = resid.shape[0]
    tm = COMBINE_TM
    nt = T // tm
    tile = lambda w: pl.BlockSpec((2 * tm, w), lambda i: (i, 0))
    const = lambda shape: pl.BlockSpec(shape, lambda i: (0,) * len(shape))
    pair_list = pl.BlockSpec((2 * tm,), lambda i: (i,), memory_space=pltpu.SMEM)
    next_list = pl.BlockSpec((tm,), lambda i: (jnp.minimum(2 * i + 2, nt - 1),), memory_space=pltpu.SMEM)
    return pl.pallas_call(
        _combine_kernel,
        grid=(nt // 2,),
        in_specs=[pair_list, pair_list, next_list, next_list,
                  pl.BlockSpec(memory_space=pl.ANY),
                  tile(D_MODEL), tile(LANES), const((1, D_MODEL)), const((1, D_MODEL))],
        out_specs=pl.BlockSpec((2 * tm, D_MODEL), lambda i: (i, 0)),
        out_shape=jax.ShapeDtypeStruct((T, D_MODEL), F32),
        scratch_shapes=[pltpu.VMEM((2, TOP_K, tm, D_MODEL), F32), pltpu.SemaphoreType.DMA((2,))],
        compiler_params=pltpu.CompilerParams(dimension_semantics=("arbitrary",), vmem_limit_bytes=VMEM_LIMIT),
        name="combine",
    )(dests[0], dests[1], dests[0], dests[1], ys, resid, route, l2g, l2b)


def _block_diag_const(n, blk, val):
    idx = jnp.arange(n) // blk
    return jnp.where(idx[:, None] == idx[None, :], val, 0.0).astype(BF16)


def kernel(x, p, ln_emb_g, ln_emb_b, w_in, mu_shift, w0, w_decay_up, a0, w_iclr_up, w_gate_up, k_k, k_a, r_k, gn_g, gn_b, gmlp_ln_g, gmlp_ln_b, w_spatial, b_spatial, w_out, ln1_g, ln1_b, w_group_router, b_group_router, w_expert_router, b_expert_router, w_exp_gate, w_exp_up, w_exp_down, w_ple_gate, b_ple_gate, w_ple_proj, ln2_g, ln2_b):
    B, S, D = x.shape
    T = B * S
    row = lambda t: t.reshape(1, -1).astype(F32)

    zl = jnp.zeros((DECAY_LORA, D_RWKV), F32)
    wwa = jnp.concatenate([jnp.concatenate([w_decay_up[0], zl], axis=1),
                           jnp.concatenate([zl, w_iclr_up[0]], axis=1)], axis=0).astype(BF16)
    w0a0 = jnp.concatenate([w0[0], a0[0]]).reshape(1, -1)
    eones = jnp.tile(_block_diag_const(2 * LANES, HEAD, 1.0), (2, 1))
    emean = jnp.tile(_block_diag_const(LANES, HEAD, 1.0 / HEAD), (2, 1))

    r, lw, k, v, a, b, g, bonus, yb = _prep(
        x, row(ln_emb_g), row(ln_emb_b), w_in[0].astype(BF16), row(mu_shift[0]), wwa, w0a0,
        w_gate_up[0].astype(BF16), row(k_k[0]), row(k_a[0]), row(r_k[0]), eones,
        row(gmlp_ln_g[0]), row(gmlp_ln_b[0]), w_spatial[0], b_spatial[0].T)

    ya = _wkv(r, lw, k, v, a, b, g, bonus, gn_g[0].reshape(N_PAIRS, 1, LANES), gn_b[0].reshape(N_PAIRS, 1, LANES),
              emean)

    wr = jnp.concatenate([w_expert_router[0].reshape(D, N_EXPERTS), w_group_router[0],
                          jnp.zeros((D, LANES - N_EXPERTS - N_GROUPS), F32)], axis=1)
    wr3 = jnp.stack(_split2(wr))
    br = jnp.concatenate([b_expert_router[0].reshape(-1), b_group_router[0],
                          jnp.zeros((LANES - N_EXPERTS - N_GROUPS,), F32)]).reshape(1, LANES)
    base, x1, route, counts = _mixer(x, row(ln_emb_g), row(ln_emb_b), ya, yb, w_out[0].astype(BF16), row(ln1_g[0]),
                                     row(ln1_b[0]), wr3, br)
    base = base.reshape(T, D)
    x1 = x1.reshape(T, D // 2)
    route = route.reshape(T, LANES)

    n_blocks = -(-(T * TOP_K) // EXPERT_ROWS) + N_EXPERTS
    dest, pend = _slots(route, counts)
    dests = (dest[0], dest[1])
    pend = pend[0, :N_EXPERTS]

    xs, resid = _dispatch(pend, dests, x1, base, p[0].reshape(T, D_PLE), w_ple_gate[0].astype(BF16),
                          row(b_ple_gate[0]), w_ple_proj[0].astype(BF16), n_blocks * EXPERT_ROWS)
    ys = _experts(pend, xs, w_exp_gate[0], w_exp_up[0], w_exp_down[0])
    out = _combine(dests, ys, resid, route, row(ln2_g[0]), row(ln2_b[0]))
    return out.reshape(B, S, D)
```

```python
import functools
import math

import jax
import jax.numpy as jnp
from jax import lax
from jax.experimental import pallas as pl
from jax.experimental.pallas import tpu as pltpu

F32 = jnp.float32
BF16 = jnp.bfloat16

D_MODEL = 1024
D_RWKV = 512
HEAD = 64
D_GMLP = 512
GMLP_GROUPS = 4
GROUP_W = 128
GCHUNK = 128
DECAY_LORA = 64
ICLR_LORA = 64
GATE_LORA = 128
N_SHIFT = 3 * D_RWKV + DECAY_LORA + ICLR_LORA + GATE_LORA
D_IN = N_SHIFT + 2 * D_GMLP
D_PLE = 256
N_GROUPS = 4
EXPERTS_PER_GROUP = 8
N_EXPERTS = 32
TOP_K = 2
D_EXPERT = 512
DEPTH = 1
ALPHA = (2.0 * DEPTH) ** 0.25
LN_EPS = 1e-5
GN_EPS = 64e-5
DECAY_SCALE = math.exp(-0.5)

LANES = 128
WKV_CHUNK = 64
N_PAIRS = D_RWKV // LANES
VMEM_LIMIT = 56 * 1024 * 1024

PREP_TM = 512
WKV_TB = 512
WKV_PAIRS = 4
MIX_TM = 512
SLOT_TM = 2048
EXPERT_ROWS = 256
EXPERT_DEPTH = 6
DISPATCH_TM = 512
COMBINE_TM = 256
NEG = -1e30


def _dot(a, b):
    return jnp.dot(a.astype(BF16), b.astype(BF16), preferred_element_type=F32)


def _dot_nt(a, b):
    return lax.dot_general(a.astype(BF16), b.astype(BF16), (((1,), (1,)), ((), ())),
                           preferred_element_type=F32)


def _split3(x):
    hi = x.astype(BF16)
    r1 = x - hi.astype(F32)
    mid = r1.astype(BF16)
    lo = (r1 - mid.astype(F32)).astype(BF16)
    return hi, mid, lo


def _dot3_lhs(x, w):
    hi, mid, lo = _split3(x)
    w = w.astype(BF16)
    return (jnp.dot(hi, w, preferred_element_type=F32) + jnp.dot(mid, w, preferred_element_type=F32)
            + jnp.dot(lo, w, preferred_element_type=F32))


def _split2(x):
    hi = x.astype(BF16)
    return hi, (x - hi.astype(F32)).astype(BF16)


def _dot2_lhs(x, w2):
    hi, lo = _split2(x)
    return jnp.dot(jnp.concatenate([hi, lo], axis=1), w2, preferred_element_type=F32)


def _dot3_rhs(w3, x):
    hi, mid, lo = _split3(x)
    return jnp.dot(w3, jnp.concatenate([hi, mid, lo], axis=0), preferred_element_type=F32)


def _layer_norm(x, g, b, eps):
    mu = jnp.mean(x, axis=-1, keepdims=True)
    xc = x - mu
    var = jnp.mean(xc * xc, axis=-1, keepdims=True)
    return xc * lax.rsqrt(var + eps) * g + b


def _sigmoid(x):
    return 1.0 / (1.0 + jnp.exp(-x))


def _iota(shape, dim):
    return lax.broadcasted_iota(jnp.int32, shape, dim)


def _prep_kernel(x_ref, lng_ref, lnb_ref, win_ref, mu_ref, wwa_ref, w0a0_ref, wg_ref, kk_ref, ka_ref, rk_ref,
                 eones_ref, glng_ref, glnb_ref, wsp_ref, bsp_ref,
                 r_ref, lw_ref, k_ref, v_ref, a_ref, b_ref, g_ref, bonus_ref, yb_ref, carry_ref):
    tm = x_ref.shape[1]

    @pl.when(pl.program_id(1) == 0)
    def _():
        carry_ref[...] = jnp.zeros_like(carry_ref)

    x0 = _layer_norm(x_ref[0], lng_ref[...], lnb_ref[...], LN_EPS)
    proj = jnp.dot(x0.astype(BF16), win_ref[...], preferred_element_type=F32)

    h = proj[:, :N_SHIFT]
    rolled = pltpu.roll(h, 1, 0)
    first = _iota((tm, N_SHIFT), 0) == 0
    prev = jnp.where(first, jnp.broadcast_to(carry_ref[0:1, :], (tm, N_SHIFT)), rolled)
    carry_ref[0:1, :] = h[tm - 1:tm, :]
    h = h + (prev - h) * mu_ref[...]

    r = h[:, 0:D_RWKV]
    k = h[:, D_RWKV:2 * D_RWKV]
    v = h[:, 2 * D_RWKV:3 * D_RWKV]
    xwa = h[:, 3 * D_RWKV:3 * D_RWKV + LANES]
    xg = h[:, 3 * D_RWKV + LANES:N_SHIFT]

    lane = _iota((tm, LANES), 1)
    twa = jnp.where(lane < DECAY_LORA, jnp.tanh(xwa), xwa)
    da = _dot(twa, wwa_ref[...]) + w0a0_ref[...]
    logw = -DECAY_SCALE * _sigmoid(da[:, :D_RWKV])
    ag = _sigmoid(da[:, D_RWKV:])
    g = _dot(_sigmoid(xg), wg_ref[...])

    eones2 = eones_ref[...]

    def head_sum(t):
        half = 2 * LANES
        return jnp.concatenate([_dot2_lhs(t[:, :half], eones2), _dot2_lhs(t[:, half:], eones2)], axis=1)

    kk = k * kk_ref[...]
    kk = kk * lax.rsqrt(jnp.maximum(head_sum(kk * kk), 1e-24))
    k = k * (1.0 + (ag - 1.0) * ka_ref[...])
    bonus = head_sum(r * k * rk_ref[...]) * v

    for p in range(N_PAIRS):
        sl = slice(p * LANES, (p + 1) * LANES)
        r_ref[0, p] = r[:, sl]
        lw_ref[0, p] = logw[:, sl]
        k_ref[0, p] = k[:, sl]
        v_ref[0, p] = v[:, sl]
        a_ref[0, p] = -kk[:, sl]
        b_ref[0, p] = (kk * ag)[:, sl]
        g_ref[0, p] = g[:, sl]
        bonus_ref[0, p] = bonus[:, sl]

    zin = proj[:, N_SHIFT:]
    z = 0.5 * zin * (1.0 + lax.erf(zin * (0.5 ** 0.5)))
    zu = z[:, :D_GMLP]
    zv = z[:, D_GMLP:]
    causal = _iota((GCHUNK, GCHUNK), 0) >= _iota((GCHUNK, GCHUNK), 1)
    for gi in range(GMLP_GROUPS):
        gs = slice(gi * GROUP_W, (gi + 1) * GROUP_W)
        zvn = _layer_norm(zv[:, gs], glng_ref[:, gs], glnb_ref[:, gs], LN_EPS)
        ws = jnp.where(causal, wsp_ref[gi], 0.0).astype(BF16)
        bcol = bsp_ref[:, gi:gi + 1]
        for c in range(tm // GCHUNK):
            ts = slice(c * GCHUNK, (c + 1) * GCHUNK)
            mixed = jnp.dot(ws, zvn[ts].astype(BF16), preferred_element_type=F32) + bcol
            yb_ref[0, ts, gs] = (zu[ts, gs] * mixed).astype(BF16)


def _prep(x, ln_g, ln_b, w_in, mu, wwa, w0a0, wg, k_k, k_a, r_k, eones, glng, glnb, wsp, bsp):
    B, S, _ = x.shape
    tm = PREP_TM
    const = lambda shape: pl.BlockSpec(shape, lambda b, s: (0,) * len(shape))
    pair_spec = pl.BlockSpec((1, N_PAIRS, tm, LANES), lambda b, s: (b, 0, s, 0))
    pair_shape = jax.ShapeDtypeStruct((B, N_PAIRS, S, LANES), F32)
    return pl.pallas_call(
        _prep_kernel,
        grid=(B, S // tm),
        in_specs=[
            pl.BlockSpec((1, tm, D_MODEL), lambda b, s: (b, s, 0)),
            const((1, D_MODEL)), const((1, D_MODEL)), const((D_MODEL, D_IN)), const((1, N_SHIFT)),
            const((LANES, 2 * D_RWKV)), const((1, 2 * D_RWKV)), const((GATE_LORA, D_RWKV)),
            const((1, D_RWKV)), const((1, D_RWKV)), const((1, D_RWKV)), const((4 * LANES, 2 * LANES)),
            const((1, D_GMLP)), const((1, D_GMLP)), const((GMLP_GROUPS, GCHUNK, GCHUNK)),
            const((GCHUNK, GMLP_GROUPS)),
        ],
        out_specs=[pair_spec] * 8 + [pl.BlockSpec((1, tm, D_GMLP), lambda b, s: (b, s, 0))],
        out_shape=[pair_shape] * 8 + [jax.ShapeDtypeStruct((B, S, D_GMLP), BF16)],
        scratch_shapes=[pltpu.VMEM((8, N_SHIFT), F32)],
        compiler_params=pltpu.CompilerParams(dimension_semantics=("arbitrary", "arbitrary"),
                                             vmem_limit_bytes=VMEM_LIMIT),
        name="prep",
    )(x, ln_g, ln_b, w_in, mu, wwa, w0a0, wg, k_k, k_a, r_k, eones, glng, glnb, wsp, bsp)


def _wkv_kernel(r_ref, lw_ref, k_ref, v_ref, a_ref, b_ref, g_ref, bonus_ref, gng_ref, gnb_ref, emean_ref,
                o_ref, h_ref):
    C = WKV_CHUNK
    tb = r_ref.shape[2]

    @pl.when(pl.program_id(2) == 0)
    def _():
        h_ref[...] = jnp.zeros_like(h_ref)

    tok = _iota((C, LANES), 0)
    lane = _iota((C, LANES), 1)
    head0 = lane < HEAD
    strict = tok > lane % HEAD
    incl = tok >= lane % HEAD
    eye_w = (tok == lane % HEAD).astype(F32)
    rr = _iota((LANES, LANES), 0)
    cc = _iota((LANES, LANES), 1)
    eye = (rr == cc).astype(F32)
    same_head = (rr < HEAD) == (cc < HEAD)
    ltri3 = (_iota((C, 3 * C), 0) >= _iota((C, 3 * C), 1) % C).astype(BF16)

    def stack(x):
        xb = x.astype(BF16)
        zero = jnp.zeros_like(xb)
        return jnp.concatenate([jnp.where(head0, xb, zero), jnp.where(head0, zero, xb)], axis=0)

    def stack2(x, y):
        return jnp.concatenate([stack(x), stack(y)], axis=1)

    n_pairs = r_ref.shape[1]
    n_chunks = tb // C
    units = [(q, c) for q in range(n_pairs) for c in range(n_chunks)]

    def load(ref):
        return [ref[0, q, c * C:(c + 1) * C, :] for q, c in units]

    r_, lw_, k_, v_, a_, b_ = (load(ref) for ref in (r_ref, lw_ref, k_ref, v_ref, a_ref, b_ref))
    cum_ = [_dot3_rhs(ltri3, lw) for lw in lw_]
    cend_ = [cum[C - 1:C, :] for cum in cum_]
    at_ = [a * jnp.exp(cum - lw) for a, cum, lw in zip(a_, cum_, lw_)]
    rt_ = [r * jnp.exp(cum) for r, cum in zip(r_, cum_)]
    ginv_ = [jnp.exp(-cum) for cum in cum_]
    gend_ = [jnp.exp(cend - cum) for cend, cum in zip(cend_, cum_)]
    bk_end_ = [jnp.concatenate([b * ge, k * ge], axis=0) for b, k, ge in zip(b_, k_, gend_)]
    vst_ = [stack(v) for v in v_]

    G_ = [_dot_nt(jnp.concatenate([at, rt], axis=0), jnp.concatenate([stack(b * gi), stack(k * gi)], axis=0))
          for at, rt, b, k, gi in zip(at_, rt_, b_, k_, ginv_)]
    n1_ = [jnp.where(strict, G[:C, :LANES], 0.0) for G in G_]
    aak_ = [jnp.where(strict, G[:C, LANES:], 0.0) for G in G_]
    arb_ = [jnp.where(incl, G[C:, :LANES], 0.0) for G in G_]
    ark_ = [jnp.where(incl, G[C:, LANES:], 0.0) for G in G_]
    av_ = [_dot(jnp.concatenate([aak, ark], axis=0), vst) for aak, ark, vst in zip(aak_, ark_, vst_)]

    s1_ = [stack(n1) for n1 in n1_]
    n2_ = [_dot(n1, s1) for n1, s1 in zip(n1_, s1_)]
    x_ = [_dot(n2, jnp.concatenate([s1, stack(n2)], axis=1)) for n2, s1 in zip(n2_, s1_)]
    t_ = [eye_w + n1 + n2 + x[:, :LANES] for n1, n2, x in zip(n1_, n2_, x_)]
    np_ = [x[:, LANES:] for x in x_]
    for _ in range(3):
        x_ = [_dot(npow, stack2(t, npow)) for t, npow in zip(t_, np_)]
        t_ = [t + x[:, :LANES] for t, x in zip(t_, x_)]
        np_ = [x[:, LANES:] for x in x_]
    t_ = [t + _dot(npow, stack(t)) for t, npow in zip(t_, np_)]

    x_ = [_dot(t, stack2(at, av[:C])) for t, at, av in zip(t_, at_, av_)]
    z_ = [_dot(arb, stack2(x[:, :LANES], x[:, LANES:])) for arb, x in zip(arb_, x_)]
    rp_ = [rt + z[:, :LANES] for rt, z in zip(rt_, z_)]
    p3_ = [z[:, LANES:] + av[C:] for z, av in zip(z_, av_)]
    rhs_ = [jnp.concatenate([x, jnp.concatenate([jnp.zeros_like(v), v], axis=1)], axis=0)
            for x, v in zip(x_, v_)]
    mq_ = [_dot(bk_end.T, rhs) for bk_end, rhs in zip(bk_end_, rhs_)]
    m_ = [eye * jnp.exp(cend) + jnp.where(same_head, mq[:, :LANES], 0.0) for cend, mq in zip(cend_, mq_)]
    q_ = [jnp.where(same_head, mq[:, LANES:], 0.0) for mq in mq_]

    pairs = range(n_pairs)
    H_ = [h_ref[q] for q in pairs]
    ys_ = [[] for _ in pairs]
    for c in range(n_chunks):
        for q in pairs:
            u = q * n_chunks + c
            ys_[q].append(_dot(rp_[u], H_[q]) + p3_[u])
        H_ = [_dot(m_[q * n_chunks + c], H_[q]) + q_[q * n_chunks + c] for q in pairs]
    for q in pairs:
        h_ref[q] = H_[q]

    emean = emean_ref[...]
    y_ = [jnp.concatenate(ys, axis=0) for ys in ys_]
    mu_ = [_dot2_lhs(y, emean) for y in y_]
    yc_ = [y - mu for y, mu in zip(y_, mu_)]
    var_ = [_dot2_lhs(yc * yc, emean) for yc in yc_]
    for q in pairs:
        yn = yc_[q] * lax.rsqrt(var_[q] + GN_EPS) * gng_ref[q] + gnb_ref[q]
        o_ref[0, q] = ((yn + bonus_ref[0, q]) * g_ref[0, q]).astype(BF16)


def _wkv(r, lw, k, v, a, b, g, bonus, gn_g, gn_b, emean):
    B, P, S, _ = r.shape
    tb = WKV_TB
    pp = WKV_PAIRS
    seq = pl.BlockSpec((1, pp, tb, LANES), lambda bi, p, s: (bi, p, s, 0))
    par = pl.BlockSpec((pp, 1, LANES), lambda bi, p, s: (p, 0, 0))
    return pl.pallas_call(
        _wkv_kernel,
        grid=(B, P // pp, S // tb),
        in_specs=[seq] * 8 + [par, par, pl.BlockSpec((2 * LANES, LANES), lambda bi, p, s: (0, 0))],
        out_specs=seq,
        out_shape=jax.ShapeDtypeStruct((B, P, S, LANES), BF16),
        scratch_shapes=[pltpu.VMEM((pp, LANES, LANES), F32)],
        compiler_params=pltpu.CompilerParams(dimension_semantics=("arbitrary", "arbitrary", "arbitrary"),
                                             vmem_limit_bytes=VMEM_LIMIT),
        name="wkv",
    )(r, lw, k, v, a, b, g, bonus, gn_g, gn_b, emean)


def _mixer_kernel(x_ref, lng_ref, lnb_ref, ya_ref, yb_ref, wout_ref, l1g_ref, l1b_ref, wr_ref, br_ref, below_ref,
                  base_ref, x1_ref, route_ref, counts_ref, carry_ref):
    tm = x_ref.shape[1]

    @pl.when((pl.program_id(0) == 0) & (pl.program_id(1) == 0))
    def _():
        carry_ref[...] = jnp.zeros_like(carry_ref)

    x0 = _layer_norm(x_ref[0], lng_ref[...], lnb_ref[...], LN_EPS)
    ymix = jnp.concatenate([ya_ref[0, p] for p in range(N_PAIRS)] + [yb_ref[0]], axis=-1)
    mix = jnp.dot(ymix, wout_ref[...], preferred_element_type=F32)
    x1 = _layer_norm(ALPHA * x0 + mix, l1g_ref[...], l1b_ref[...], LN_EPS)
    x1b = x1.astype(BF16)
    half = D_MODEL // 2
    lo_bits = lax.bitcast_convert_type(x1b[:, :half].astype(F32), jnp.uint32)
    hi_bits = lax.bitcast_convert_type(x1b[:, half:].astype(F32), jnp.uint32)
    x1_ref[0] = (hi_bits & jnp.uint32(0xFFFF0000)) | (lo_bits >> 16)

    hi, mid = _split2(x1)
    whi = wr_ref[0]
    wmid = wr_ref[1]
    d = lambda u, w: jnp.dot(u, w, preferred_element_type=F32)
    logits = (d(hi, whi) + d(hi, wmid) + d(mid, whi)) + br_ref[...]
    lane = _iota((tm, LANES), 1).astype(F32)
    far = float(4 * LANES)
    is_g = jnp.where(lane >= N_EXPERTS, jnp.where(lane < N_EXPERTS + N_GROUPS, 1.0, 0.0), 0.0) > 0.5
    gl = jnp.where(is_g, logits, NEG)
    gmax = jnp.max(gl, axis=-1, keepdims=True)
    gsel = jnp.min(jnp.where(gl == gmax, lane, far), axis=-1, keepdims=True) - N_EXPERTS
    p_group = 1.0 / jnp.sum(jnp.where(is_g, jnp.exp(gl - gmax), 0.0), axis=-1, keepdims=True)
    grp_of_lane = jnp.floor(lane * (1.0 / EXPERTS_PER_GROUP))
    el = jnp.where(grp_of_lane == gsel, logits, NEG)
    v1 = jnp.max(el, axis=-1, keepdims=True)
    i1 = jnp.min(jnp.where(el == v1, lane, far), axis=-1, keepdims=True)
    el2 = jnp.where(lane == i1, NEG, el)
    v2 = jnp.max(el2, axis=-1, keepdims=True)
    i2 = jnp.min(jnp.where(el2 == v2, lane, far), axis=-1, keepdims=True)
    e21 = jnp.exp(v2 - v1)
    w1 = p_group / (1.0 + e21)
    w2 = p_group * e21 / (1.0 + e21)

    oh1 = lane == i1
    oh2 = lane == i2
    below = below_ref[...]
    o1 = jnp.where(oh1, 1.0, 0.0)
    o2 = jnp.where(oh2, 1.0, 0.0)
    c1 = jnp.dot(below, o1.astype(BF16), preferred_element_type=F32)
    c2 = jnp.dot(below, o2.astype(BF16), preferred_element_type=F32)
    tot1 = jnp.sum(o1, axis=0, keepdims=True)
    carry = carry_ref[0:1, :]
    rank1 = jnp.sum(jnp.where(oh1, c1 + carry, 0.0), axis=-1, keepdims=True)
    rank2 = jnp.sum(jnp.where(oh2, c2 + carry + tot1, 0.0), axis=-1, keepdims=True)
    carry = carry + tot1 + jnp.sum(o2, axis=0, keepdims=True)
    carry_ref[0:1, :] = carry
    counts_ref[...] = jnp.broadcast_to(carry, counts_ref.shape)

    fields = (i1, i2, w1, w2, rank1, rank2)
    route = jnp.zeros((tm, LANES), F32)
    for n, f in enumerate(fields):
        route = jnp.where(lane == n, f, route)
    route_ref[0] = route

    base_ref[0] = ALPHA * x1


def _mixer(x, ln_g, ln_b, ya, yb, w_out, l1g, l1b, wr3, br):
    B, S, _ = x.shape
    tm = MIX_TM
    const = lambda shape: pl.BlockSpec(shape, lambda b, s: (0,) * len(shape))
    row = lambda w: pl.BlockSpec((1, tm, w), lambda b, s: (b, s, 0))
    below = (jnp.arange(tm)[:, None] > jnp.arange(tm)[None, :]).astype(BF16)
    return pl.pallas_call(
        _mixer_kernel,
        grid=(B, S // tm),
        in_specs=[
            row(D_MODEL), const((1, D_MODEL)), const((1, D_MODEL)),
            pl.BlockSpec((1, N_PAIRS, tm, LANES), lambda b, s: (b, 0, s, 0)), row(D_GMLP),
            const((D_MODEL, D_MODEL)), const((1, D_MODEL)), const((1, D_MODEL)),
            const((2, D_MODEL, LANES)), const((1, LANES)), const((tm, tm)),
        ],
        out_specs=[row(D_MODEL), row(D_MODEL // 2), row(LANES), const((8, LANES))],
        out_shape=[jax.ShapeDtypeStruct((B, S, D_MODEL), F32), jax.ShapeDtypeStruct((B, S, D_MODEL // 2), jnp.uint32),
                   jax.ShapeDtypeStruct((B, S, LANES), F32), jax.ShapeDtypeStruct((8, LANES), F32)],
        scratch_shapes=[pltpu.VMEM((8, LANES), F32)],
        compiler_params=pltpu.CompilerParams(dimension_semantics=("arbitrary", "arbitrary"),
                                             vmem_limit_bytes=VMEM_LIMIT),
        name="mixer",
    )(x, ln_g, ln_b, ya, yb, w_out, l1g, l1b, wr3, br, below)


def _slots_kernel(route_ref, counts_ref, dest_ref, pend_ref):
    tm = route_ref.shape[0]
    lane = _iota((tm, LANES), 1)
    route = route_ref[...]
    oh1 = lane == route[:, 0:1].astype(jnp.int32)
    oh2 = lane == route[:, 1:2].astype(jnp.int32)

    counts = counts_ref[0:1, :]
    padded = jnp.floor((counts + (EXPERT_ROWS - 1)) * (1.0 / EXPERT_ROWS)) * EXPERT_ROWS
    upper = (_iota((LANES, LANES), 0) <= _iota((LANES, LANES), 1)).astype(BF16)
    pend = _dot3_lhs(jnp.broadcast_to(padded, (8, LANES)), upper)[0:1, :]
    pstart = pend - padded
    d1 = jnp.sum(jnp.where(oh1, pstart, 0.0), axis=-1, keepdims=True) + route[:, 4:5]
    d2 = jnp.sum(jnp.where(oh2, pstart, 0.0), axis=-1, keepdims=True) + route[:, 5:6]
    dest = jnp.where(lane == 0, d1, jnp.where(lane == 1, d2, 0.0))
    dest_ref[...] = jnp.transpose(dest)[0:dest_ref.shape[0], :].astype(jnp.int32)
    pend_ref[...] = jnp.broadcast_to(pend, (8, LANES)).astype(jnp.int32)


def _slots(route, counts):
    T = route.shape[0]
    tm = SLOT_TM
    return pl.pallas_call(
        _slots_kernel,
        grid=(T // tm,),
        in_specs=[pl.BlockSpec((tm, LANES), lambda i: (i, 0)), pl.BlockSpec((8, LANES), lambda i: (0, 0))],
        out_specs=[pl.BlockSpec((8, tm), lambda i: (0, i)),
                   pl.BlockSpec((8, LANES), lambda i: (0, 0))],
        out_shape=[jax.ShapeDtypeStruct((8, T), jnp.int32), jax.ShapeDtypeStruct((8, LANES), jnp.int32)],
        compiler_params=pltpu.CompilerParams(dimension_semantics=("arbitrary",), vmem_limit_bytes=VMEM_LIMIT),
        name="slots",
    )(route, counts)


def _dispatch_kernel(pend_ref, dest0_ref, dest1_ref, x_ref, base_ref, p_ref, wpg_ref, bpg_ref, wpp_ref, xs_ref, resid_ref,
                     zero_ref, sem, zsem):
    tm = x_ref.shape[0]
    dest_refs = (dest0_ref, dest1_ref)

    @pl.when(pl.program_id(0) == 0)
    def _():
        zero_ref[...] = jnp.zeros_like(zero_ref)

        def tail(e):
            start = pl.multiple_of(jnp.maximum(pend_ref[e] - EXPERT_ROWS, 0), EXPERT_ROWS)
            return pltpu.make_async_copy(zero_ref, xs_ref.at[pl.ds(start, EXPERT_ROWS)], zsem)

        def unused(j):
            return pltpu.make_async_copy(
                zero_ref, xs_ref.at[pl.ds(pl.multiple_of(j * EXPERT_ROWS, EXPERT_ROWS), EXPERT_ROWS)], zsem)

        def start_unused(j, _):
            unused(j).start()
            return 0

        def wait_unused(j, _):
            unused(j).wait()
            return 0

        first_unused = pend_ref[N_EXPERTS - 1] // EXPERT_ROWS
        n_blocks = xs_ref.shape[0] // EXPERT_ROWS
        for e in range(N_EXPERTS):
            tail(e).start()
        lax.fori_loop(first_unused, n_blocks, start_unused, 0)
        for e in range(N_EXPERTS):
            tail(e).wait()
        lax.fori_loop(first_unused, n_blocks, wait_unused, 0)

    for t in range(tm):
        for j in range(TOP_K):
            pltpu.make_async_copy(x_ref.at[pl.ds(t, 1)], xs_ref.at[pl.ds(dest_refs[j][t], 1)],
                                  sem).start(priority=j)

    xw = x_ref[...]
    x_lo = lax.bitcast_convert_type(xw << 16, F32)
    x_hi = lax.bitcast_convert_type(xw & jnp.uint32(0xFFFF0000), F32)
    x1b = jnp.concatenate([x_lo, x_hi], axis=1).astype(BF16)
    gate = _sigmoid(jnp.dot(x1b, wpg_ref[...], preferred_element_type=F32) + bpg_ref[...])
    ple = gate * jnp.dot(p_ref[...].astype(BF16), wpp_ref[...], preferred_element_type=F32)
    resid_ref[...] = base_ref[...] + ple

    for j in range(TOP_K):
        pltpu.make_async_copy(x_ref, xs_ref.at[pl.ds(0, tm)], sem).wait()


def _dispatch(pend, dests, x1, base, p, wpg, bpg, wpp, n_rows):
    T, width = x1.shape
    tm = DISPATCH_TM
    const = lambda shape: pl.BlockSpec(shape, lambda i, pe: (0,) * len(shape))
    tile = lambda w: pl.BlockSpec((tm, w), lambda i, pe: (i, 0))
    index_list = pl.BlockSpec((tm,), lambda i, pe: (i,), memory_space=pltpu.SMEM)
    return pl.pallas_call(
        _dispatch_kernel,
        grid_spec=pltpu.PrefetchScalarGridSpec(
            num_scalar_prefetch=1,
            grid=(T // tm,),
            in_specs=[index_list, index_list,
                      tile(width), tile(D_MODEL), tile(D_PLE),
                      const((D_MODEL, D_MODEL)), const((1, D_MODEL)), const((D_PLE, D_MODEL))],
            out_specs=[pl.BlockSpec(memory_space=pl.ANY), tile(D_MODEL)],
            scratch_shapes=[pltpu.VMEM((EXPERT_ROWS, width), x1.dtype), pltpu.SemaphoreType.DMA,
                            pltpu.SemaphoreType.DMA],
        ),
        out_shape=[jax.ShapeDtypeStruct((n_rows, width), x1.dtype), jax.ShapeDtypeStruct((T, D_MODEL), F32)],
        compiler_params=pltpu.CompilerParams(dimension_semantics=("arbitrary",), vmem_limit_bytes=VMEM_LIMIT),
        name="dispatch",
    )(pend, dests[0], dests[1], x1, base, p, wpg, bpg, wpp)


def _experts_kernel(pend_ref, xs_ref, wg_ref, wu_ref, wd_ref, ys_ref, xbuf_ref, ybuf_ref, wgu_ref, wdb_ref,
                    in_sem, out_sem):
    rows = EXPERT_ROWS
    e = pl.program_id(0)
    first = jnp.where(e == 0, 0, pend_ref[jnp.maximum(e - 1, 0)]) // rows
    last = pend_ref[e] // rows
    n_used = pend_ref[N_EXPERTS - 1] // rows

    def block_rows(ref, b):
        return ref.at[pl.ds(pl.multiple_of(b * rows, rows), rows)]

    depth = xbuf_ref.shape[0]
    row_priority = 1

    def x_copy(b):
        slot = b % depth
        return pltpu.make_async_copy(block_rows(xs_ref, b), xbuf_ref.at[slot], in_sem.at[slot])

    def y_copy(b):
        slot = b % depth
        return pltpu.make_async_copy(ybuf_ref.at[slot], block_rows(ys_ref, b), out_sem.at[slot])

    @pl.when(e == 0)
    def _():
        for ahead in range(depth - 1):
            @pl.when(ahead < n_used)
            def _():
                x_copy(ahead).start(priority=row_priority)

    @pl.when(last > first)
    def _():
        wgu_ref[:, :D_EXPERT] = wg_ref[0].astype(BF16)
        wgu_ref[:, D_EXPERT:] = wu_ref[0].astype(BF16)
        wdb_ref[...] = wd_ref[0].astype(BF16)

        def body(b, _):
            slot = b % depth

            @pl.when(b + depth - 1 < n_used)
            def _():
                x_copy(b + depth - 1).start(priority=row_priority)

            x_copy(b).wait()

            @pl.when(b >= depth)
            def _():
                y_copy(b - depth).wait()

            xw = xbuf_ref[slot]
            x_lo = lax.bitcast_convert_type(xw << 16, F32)
            x_hi = lax.bitcast_convert_type(xw & jnp.uint32(0xFFFF0000), F32)
            xb = jnp.concatenate([x_lo, x_hi], axis=1).astype(BF16)
            h = jnp.dot(xb, wgu_ref[...], preferred_element_type=F32)
            hg = h[:, :D_EXPERT]
            hid = hg * _sigmoid(hg) * h[:, D_EXPERT:]
            ybuf_ref[slot] = jnp.dot(hid.astype(BF16), wdb_ref[...], preferred_element_type=F32)
            y_copy(b).start(priority=row_priority)
            return 0

        lax.fori_loop(first, last, body, 0)

    @pl.when(e == N_EXPERTS - 1)
    def _():
        for back in range(depth, 0, -1):
            @pl.when(n_used >= back)
            def _():
                y_copy(n_used - back).wait()

        ybuf_ref[0] = jnp.zeros(ybuf_ref.shape[1:], F32)

        def unused(b):
            return pltpu.make_async_copy(ybuf_ref.at[0], block_rows(ys_ref, b), out_sem.at[0])

        def start_unused(b, _):
            unused(b).start()
            return 0

        def wait_unused(b, _):
            unused(b).wait()
            return 0

        n_blocks = ys_ref.shape[0] // rows
        lax.fori_loop(n_used, n_blocks, start_unused, 0)
        lax.fori_loop(n_used, n_blocks, wait_unused, 0)


def _experts(pend, xs, wg, wu, wd):
    n_rows = xs.shape[0]
    rows = EXPERT_ROWS
    wspec = lambda shape: pl.BlockSpec((1,) + shape, lambda e, pe: (e, 0, 0))
    return pl.pallas_call(
        _experts_kernel,
        grid_spec=pltpu.PrefetchScalarGridSpec(
            num_scalar_prefetch=1,
            grid=(N_EXPERTS,),
            in_specs=[pl.BlockSpec(memory_space=pl.ANY),
                      wspec((D_MODEL, D_EXPERT)), wspec((D_MODEL, D_EXPERT)), wspec((D_EXPERT, D_MODEL))],
            out_specs=pl.BlockSpec(memory_space=pl.ANY),
            scratch_shapes=[pltpu.VMEM((EXPERT_DEPTH, rows, D_MODEL // 2), jnp.uint32),
                            pltpu.VMEM((EXPERT_DEPTH, rows, D_MODEL), F32),
                            pltpu.VMEM((D_MODEL, 2 * D_EXPERT), BF16), pltpu.VMEM((D_EXPERT, D_MODEL), BF16),
                            pltpu.SemaphoreType.DMA((EXPERT_DEPTH,)), pltpu.SemaphoreType.DMA((EXPERT_DEPTH,))],
        ),
        out_shape=jax.ShapeDtypeStruct((n_rows, D_MODEL), F32),
        compiler_params=pltpu.CompilerParams(dimension_semantics=("arbitrary",), vmem_limit_bytes=VMEM_LIMIT),
        name="experts",
    )(pend, xs, wg, wu, wd)


def _combine_kernel(dest0_ref, dest1_ref, dest0_next_ref, dest1_next_ref, ys_ref, resid_ref, route_ref, lg_ref, lb_ref,
                    o_ref, buf_ref, sem):
    tm = buf_ref.shape[2]
    i = pl.program_id(0)
    dest_refs = (dest0_ref, dest1_ref)
    dest_next_refs = (dest0_next_ref, dest1_next_ref)

    def gather(drefs, offset, s):
        for t in range(tm):
            for j in range(TOP_K):
                pltpu.make_async_copy(ys_ref.at[pl.ds(drefs[j][offset + t], 1)],
                                      buf_ref.at[s, j, pl.ds(t, 1)], sem.at[s]).start(priority=j)

    def drain(s):
        for j in range(TOP_K):
            pltpu.make_async_copy(ys_ref.at[pl.ds(0, tm)], buf_ref.at[s, j], sem.at[s]).wait()

    def finish(s):
        rows = slice(s * tm, (s + 1) * tm)
        drain(s)
        route = route_ref[rows, :]
        ffn = buf_ref[s, 0] * route[:, 2:3] + buf_ref[s, 1] * route[:, 3:4]
        o_ref[rows, :] = _layer_norm(resid_ref[rows, :] + ffn, lg_ref[...], lb_ref[...], LN_EPS)

    @pl.when(i == 0)
    def _():
        gather(dest_refs, 0, 0)

    gather(dest_refs, tm, 1)
    finish(0)
    gather(dest_next_refs, 0, 0)
    finish(1)

    @pl.when(i == pl.num_programs(0) - 1)
    def _():
        drain(0)


def _combine(dests, ys, resid, route, l2g, l2b):
    T = resid.shape[0]
    tm = COMBINE_TM
    nt = T // tm
    tile = lambda w: pl.BlockSpec((2 * tm, w), lambda i: (i, 0))
    const = lambda shape: pl.BlockSpec(shape, lambda i: (0,) * len(shape))
    pair_list = pl.BlockSpec((2 * tm,), lambda i: (i,), memory_space=pltpu.SMEM)
    next_list = pl.BlockSpec((tm,), lambda i: (jnp.minimum(2 * i + 2, nt - 1),), memory_space=pltpu.SMEM)
    return pl.pallas_call(
        _combine_kernel,
        grid=(nt // 2,),
        in_specs=[pair_list, pair_list, next_list, next_list,
                  pl.BlockSpec(memory_space=pl.ANY),
                  tile(D_MODEL), tile(LANES), const((1, D_MODEL)), const((1, D_MODEL))],
        out_specs=pl.BlockSpec((2 * tm, D_MODEL), lambda i: (i, 0)),
        out_shape=jax.ShapeDtypeStruct((T, D_MODEL), F32),
        scratch_shapes=[pltpu.VMEM((2, TOP_K, tm, D_MODEL), F32), pltpu.SemaphoreType.DMA((2,))],
        compiler_params=pltpu.CompilerParams(dimension_semantics=("arbitrary",), vmem_limit_bytes=VMEM_LIMIT),
        name="combine",
    )(dests[0], dests[1], dests[0], dests[1], ys, resid, route, l2g, l2b)


def _block_diag_const(n, blk, val):
    idx = jnp.arange(n) // blk
    return jnp.where(idx[:, None] == idx[None, :], val, 0.0).astype(BF16)


def kernel(x, p, ln_emb_g, ln_emb_b, w_in, mu_shift, w0, w_decay_up, a0, w_iclr_up, w_gate_up, k_k, k_a, r_k, gn_g, gn_b, gmlp_ln_g, gmlp_ln_b, w_spatial, b_spatial, w_out, ln1_g, ln1_b, w_group_router, b_group_router, w_expert_router, b_expert_router, w_exp_gate, w_exp_up, w_exp_down, w_ple_gate, b_ple_gate, w_ple_proj, ln2_g, ln2_b):
    B, S, D = x.shape
    T = B * S
    row = lambda t: t.reshape(1, -1).astype(F32)

    zl = jnp.zeros((DECAY_LORA, D_RWKV), F32)
    wwa = jnp.concatenate([jnp.concatenate([w_decay_up[0], zl], axis=1),
                           jnp.concatenate([zl, w_iclr_up[0]], axis=1)], axis=0).astype(BF16)
    w0a0 = jnp.concatenate([w0[0], a0[0]]).reshape(1, -1)
    eones = jnp.tile(_block_diag_const(2 * LANES, HEAD, 1.0), (2, 1))
    emean = jnp.tile(_block_diag_const(LANES, HEAD, 1.0 / HEAD), (2, 1))

    r, lw, k, v, a, b, g, bonus, yb = _prep(
        x, row(ln_emb_g), row(ln_emb_b), w_in[0].astype(BF16), row(mu_shift[0]), wwa, w0a0,
        w_gate_up[0].astype(BF16), row(k_k[0]), row(k_a[0]), row(r_k[0]), eones,
        row(gmlp_ln_g[0]), row(gmlp_ln_b[0]), w_spatial[0], b_spatial[0].T)

    ya = _wkv(r, lw, k, v, a, b, g, bonus, gn_g[0].reshape(N_PAIRS, 1, LANES), gn_b[0].reshape(N_PAIRS, 1, LANES),
              emean)

    wr = jnp.concatenate([w_expert_router[0].reshape(D, N_EXPERTS), w_group_router[0],
                          jnp.zeros((D, LANES - N_EXPERTS - N_GROUPS), F32)], axis=1)
    wr3 = jnp.stack(_split2(wr))
    br = jnp.concatenate([b_expert_router[0].reshape(-1), b_group_router[0],
                          jnp.zeros((LANES - N_EXPERTS - N_GROUPS,), F32)]).reshape(1, LANES)
    base, x1, route, counts = _mixer(x, row(ln_emb_g), row(ln_emb_b), ya, yb, w_out[0].astype(BF16), row(ln1_g[0]),
                                     row(ln1_b[0]), wr3, br)
    base = base.reshape(T, D)
    x1 = x1.reshape(T, D // 2)
    route = route.reshape(T, LANES)

    n_blocks = -(-(T * TOP_K) // EXPERT_ROWS) + N_EXPERTS
    dest, pend = _slots(route, counts)
    dests = (dest[0], dest[1])
    pend = pend[0, :N_EXPERTS]

    xs, resid = _dispatch(pend, dests, x1, base, p[0].reshape(T, D_PLE), w_ple_gate[0].astype(BF16),
                          row(b_ple_gate[0]), w_ple_proj[0].astype(BF16), n_blocks * EXPERT_ROWS)
    ys = _experts(pend, xs, w_exp_gate[0], w_exp_up[0], w_exp_down[0])
    out = _combine(dests, ys, resid, route, row(ln2_g[0]), row(ln2_b[0]))
    return out.reshape(B, S, D)
```

```python
import math

import jax
import jax.numpy as jnp
from jax import lax
from jax.experimental import pallas as pl
from jax.experimental.pallas import tpu as pltpu

F32 = jnp.float32
BF16 = jnp.bfloat16

D_MODEL = 1024
D_RWKV = 512
HEAD = 64
D_GMLP = 512
GMLP_GROUPS = 4
GROUP_W = 128
GCHUNK = 128
DECAY_LORA = 64
ICLR_LORA = 64
GATE_LORA = 128
N_SHIFT = 3 * D_RWKV + DECAY_LORA + ICLR_LORA + GATE_LORA
D_IN = N_SHIFT + 2 * D_GMLP
D_PLE = 256
N_GROUPS = 4
EXPERTS_PER_GROUP = 8
N_EXPERTS = 32
TOP_K = 2
D_EXPERT = 512
DEPTH = 1
ALPHA = (2.0 * DEPTH) ** 0.25
LN_EPS = 1e-5
GN_EPS = 64e-5
DECAY_SCALE = math.exp(-0.5)

LANES = 128
WKV_CHUNK = 64
N_PAIRS = D_RWKV // LANES
VMEM_LIMIT = 56 * 1024 * 1024

PREP_TM = 512
WKV_TB = 512
WKV_PAIRS = 4
MIX_TM = 512
SLOT_TM = 2048
EXPERT_ROWS = 256
EXPERT_DEPTH = 4
DISPATCH_TM = 512
COMBINE_TM = 256
NEG = -1e30


def _dot(a, b):
    return jnp.dot(a.astype(BF16), b.astype(BF16), preferred_element_type=F32)


def _dot_nt(a, b):
    return lax.dot_general(a.astype(BF16), b.astype(BF16), (((1,), (1,)), ((), ())),
                           preferred_element_type=F32)


def _split3(x):
    hi = x.astype(BF16)
    r1 = x - hi.astype(F32)
    mid = r1.astype(BF16)
    lo = (r1 - mid.astype(F32)).astype(BF16)
    return hi, mid, lo


def _dot3_lhs(x, w):
    hi, mid, lo = _split3(x)
    w = w.astype(BF16)
    return (jnp.dot(hi, w, preferred_element_type=F32) + jnp.dot(mid, w, preferred_element_type=F32)
            + jnp.dot(lo, w, preferred_element_type=F32))


def _split2(x):
    hi = x.astype(BF16)
    return hi, (x - hi.astype(F32)).astype(BF16)


def _dot2_lhs(x, w2):
    hi, lo = _split2(x)
    return jnp.dot(jnp.concatenate([hi, lo], axis=1), w2, preferred_element_type=F32)


def _dot3_rhs(w3, x):
    hi, mid, lo = _split3(x)
    return jnp.dot(w3, jnp.concatenate([hi, mid, lo], axis=0), preferred_element_type=F32)


def _layer_norm(x, g, b, eps):
    mu = jnp.mean(x, axis=-1, keepdims=True)
    xc = x - mu
    var = jnp.mean(xc * xc, axis=-1, keepdims=True)
    return xc * lax.rsqrt(var + eps) * g + b


def _sigmoid(x):
    return 1.0 / (1.0 + jnp.exp(-x))


def _iota(shape, dim):
    return lax.broadcasted_iota(jnp.int32, shape, dim)


def _prep_kernel(x_ref, lng_ref, lnb_ref, win_ref, mu_ref, wwa_ref, w0a0_ref, wg_ref, kk_ref, ka_ref, rk_ref,
                 eones_ref, glng_ref, glnb_ref, wsp_ref, bsp_ref,
                 r_ref, lw_ref, k_ref, v_ref, a_ref, b_ref, g_ref, bonus_ref, yb_ref, carry_ref):
    tm = x_ref.shape[1]

    @pl.when(pl.program_id(1) == 0)
    def _():
        carry_ref[...] = jnp.zeros_like(carry_ref)

    x0 = _layer_norm(x_ref[0], lng_ref[...], lnb_ref[...], LN_EPS)
    proj = jnp.dot(x0.astype(BF16), win_ref[...], preferred_element_type=F32)

    h = proj[:, :N_SHIFT]
    rolled = pltpu.roll(h, 1, 0)
    first = _iota((tm, N_SHIFT), 0) == 0
    prev = jnp.where(first, jnp.broadcast_to(carry_ref[0:1, :], (tm, N_SHIFT)), rolled)
    carry_ref[0:1, :] = h[tm - 1:tm, :]
    h = h + (prev - h) * mu_ref[...]

    r = h[:, 0:D_RWKV]
    k = h[:, D_RWKV:2 * D_RWKV]
    v = h[:, 2 * D_RWKV:3 * D_RWKV]
    xwa = h[:, 3 * D_RWKV:3 * D_RWKV + LANES]
    xg = h[:, 3 * D_RWKV + LANES:N_SHIFT]

    lane = _iota((tm, LANES), 1)
    twa = jnp.where(lane < DECAY_LORA, jnp.tanh(xwa), xwa)
    da = _dot(twa, wwa_ref[...]) + w0a0_ref[...]
    logw = -DECAY_SCALE * _sigmoid(da[:, :D_RWKV])
    ag = _sigmoid(da[:, D_RWKV:])
    g = _dot(_sigmoid(xg), wg_ref[...])

    eones2 = eones_ref[...]

    def head_sum(t):
        half = 2 * LANES
        return jnp.concatenate([_dot2_lhs(t[:, :half], eones2), _dot2_lhs(t[:, half:], eones2)], axis=1)

    kk = k * kk_ref[...]
    kk = kk * lax.rsqrt(jnp.maximum(head_sum(kk * kk), 1e-24))
    k = k * (1.0 + (ag - 1.0) * ka_ref[...])
    bonus = head_sum(r * k * rk_ref[...]) * v

    for p in range(N_PAIRS):
        sl = slice(p * LANES, (p + 1) * LANES)
        r_ref[0, p] = r[:, sl]
        lw_ref[0, p] = logw[:, sl]
        k_ref[0, p] = k[:, sl]
        v_ref[0, p] = v[:, sl]
        a_ref[0, p] = -kk[:, sl]
        b_ref[0, p] = (kk * ag)[:, sl]
        g_ref[0, p] = g[:, sl]
        bonus_ref[0, p] = bonus[:, sl]

    zin = proj[:, N_SHIFT:]
    z = 0.5 * zin * (1.0 + lax.erf(zin * (0.5 ** 0.5)))
    zu = z[:, :D_GMLP]
    zv = z[:, D_GMLP:]
    causal = _iota((GCHUNK, GCHUNK), 0) >= _iota((GCHUNK, GCHUNK), 1)
    for gi in range(GMLP_GROUPS):
        gs = slice(gi * GROUP_W, (gi + 1) * GROUP_W)
        zvn = _layer_norm(zv[:, gs], glng_ref[:, gs], glnb_ref[:, gs], LN_EPS)
        ws = jnp.where(causal, wsp_ref[gi], 0.0).astype(BF16)
        bcol = bsp_ref[:, gi:gi + 1]
        chunks = [slice(c * GCHUNK, (c + 1) * GCHUNK) for c in range(tm // GCHUNK)]
        zcat = jnp.concatenate([zvn[ts] for ts in chunks], axis=1).astype(BF16)
        mixed = jnp.dot(ws, zcat, preferred_element_type=F32) + bcol
        for ts in chunks:
            yb_ref[0, ts, gs] = (zu[ts, gs] * mixed[:, ts]).astype(BF16)


def _prep(x, ln_g, ln_b, w_in, mu, wwa, w0a0, wg, k_k, k_a, r_k, eones, glng, glnb, wsp, bsp):
    B, S, _ = x.shape
    tm = PREP_TM
    const = lambda shape: pl.BlockSpec(shape, lambda b, s: (0,) * len(shape))
    pair_spec = pl.BlockSpec((1, N_PAIRS, tm, LANES), lambda b, s: (b, 0, s, 0))
    pair_shape = jax.ShapeDtypeStruct((B, N_PAIRS, S, LANES), F32)
    return pl.pallas_call(
        _prep_kernel,
        grid=(B, S // tm),
        in_specs=[
            pl.BlockSpec((1, tm, D_MODEL), lambda b, s: (b, s, 0)),
            const((1, D_MODEL)), const((1, D_MODEL)), const((D_MODEL, D_IN)), const((1, N_SHIFT)),
            const((LANES, 2 * D_RWKV)), const((1, 2 * D_RWKV)), const((GATE_LORA, D_RWKV)),
            const((1, D_RWKV)), const((1, D_RWKV)), const((1, D_RWKV)), const((4 * LANES, 2 * LANES)),
            const((1, D_GMLP)), const((1, D_GMLP)), const((GMLP_GROUPS, GCHUNK, GCHUNK)),
            const((GCHUNK, GMLP_GROUPS)),
        ],
        out_specs=[pair_spec] * 8 + [pl.BlockSpec((1, tm, D_GMLP), lambda b, s: (b, s, 0))],
        out_shape=[pair_shape] * 8 + [jax.ShapeDtypeStruct((B, S, D_GMLP), BF16)],
        scratch_shapes=[pltpu.VMEM((8, N_SHIFT), F32)],
        compiler_params=pltpu.CompilerParams(dimension_semantics=("arbitrary", "arbitrary"),
                                             vmem_limit_bytes=VMEM_LIMIT),
        name="prep",
    )(x, ln_g, ln_b, w_in, mu, wwa, w0a0, wg, k_k, k_a, r_k, eones, glng, glnb, wsp, bsp)


def _wkv_kernel(r_ref, lw_ref, k_ref, v_ref, a_ref, b_ref, g_ref, bonus_ref, gng_ref, gnb_ref, emean_ref,
                o_ref, h_ref):
    C = WKV_CHUNK
    tb = r_ref.shape[2]

    @pl.when(pl.program_id(2) == 0)
    def _():
        h_ref[...] = jnp.zeros_like(h_ref)

    tok = _iota((C, LANES), 0)
    lane = _iota((C, LANES), 1)
    head0 = lane < HEAD
    strict = tok > lane % HEAD
    incl = tok >= lane % HEAD
    eye_w = (tok == lane % HEAD).astype(F32)
    rr = _iota((LANES, LANES), 0)
    cc = _iota((LANES, LANES), 1)
    eye = (rr == cc).astype(F32)
    same_head = (rr < HEAD) == (cc < HEAD)
    ltri3 = (_iota((C, 3 * C), 0) >= _iota((C, 3 * C), 1) % C).astype(BF16)

    def stack(x):
        xb = x.astype(BF16)
        zero = jnp.zeros_like(xb)
        return jnp.concatenate([jnp.where(head0, xb, zero), jnp.where(head0, zero, xb)], axis=0)

    def stack2(x, y):
        return jnp.concatenate([stack(x), stack(y)], axis=1)

    n_pairs = r_ref.shape[1]
    n_chunks = tb // C
    units = [(q, c) for q in range(n_pairs) for c in range(n_chunks)]

    def load(ref):
        return [ref[0, q, c * C:(c + 1) * C, :] for q, c in units]

    r_, lw_, k_, v_, a_, b_ = (load(ref) for ref in (r_ref, lw_ref, k_ref, v_ref, a_ref, b_ref))
    cum_ = [_dot3_rhs(ltri3, lw) for lw in lw_]
    cend_ = [cum[C - 1:C, :] for cum in cum_]
    at_ = [a * jnp.exp(cum - lw) for a, cum, lw in zip(a_, cum_, lw_)]
    rt_ = [r * jnp.exp(cum) for r, cum in zip(r_, cum_)]
    ginv_ = [jnp.exp(-cum) for cum in cum_]
    gend_ = [jnp.exp(cend - cum) for cend, cum in zip(cend_, cum_)]
    bk_end_ = [jnp.concatenate([b * ge, k * ge], axis=0) for b, k, ge in zip(b_, k_, gend_)]
    vst_ = [stack(v) for v in v_]

    G_ = [_dot_nt(jnp.concatenate([at, rt], axis=0), jnp.concatenate([stack(b * gi), stack(k * gi)], axis=0))
          for at, rt, b, k, gi in zip(at_, rt_, b_, k_, ginv_)]
    n1_ = [jnp.where(strict, G[:C, :LANES], 0.0) for G in G_]
    aak_ = [jnp.where(strict, G[:C, LANES:], 0.0) for G in G_]
    arb_ = [jnp.where(incl, G[C:, :LANES], 0.0) for G in G_]
    ark_ = [jnp.where(incl, G[C:, LANES:], 0.0) for G in G_]
    av_ = [_dot(jnp.concatenate([aak, ark], axis=0), vst) for aak, ark, vst in zip(aak_, ark_, vst_)]

    s1_ = [stack(n1) for n1 in n1_]
    n2_ = [_dot(n1, s1) for n1, s1 in zip(n1_, s1_)]
    x_ = [_dot(n2, jnp.concatenate([s1, stack(n2)], axis=1)) for n2, s1 in zip(n2_, s1_)]
    t_ = [eye_w + n1 + n2 + x[:, :LANES] for n1, n2, x in zip(n1_, n2_, x_)]
    np_ = [x[:, LANES:] for x in x_]
    for _ in range(3):
        x_ = [_dot(npow, stack2(t, npow)) for t, npow in zip(t_, np_)]
        t_ = [t + x[:, :LANES] for t, x in zip(t_, x_)]
        np_ = [x[:, LANES:] for x in x_]
    t_ = [t + _dot(npow, stack(t)) for t, npow in zip(t_, np_)]

    x_ = [_dot(t, stack2(at, av[:C])) for t, at, av in zip(t_, at_, av_)]
    z_ = [_dot(arb, stack2(x[:, :LANES], x[:, LANES:])) for arb, x in zip(arb_, x_)]
    rp_ = [rt + z[:, :LANES] for rt, z in zip(rt_, z_)]
    p3_ = [z[:, LANES:] + av[C:] for z, av in zip(z_, av_)]
    rhs_ = [jnp.concatenate([x, jnp.concatenate([jnp.zeros_like(v), v], axis=1)], axis=0)
            for x, v in zip(x_, v_)]
    mq_ = [_dot(bk_end.T, rhs) for bk_end, rhs in zip(bk_end_, rhs_)]
    m_ = [eye * jnp.exp(cend) + jnp.where(same_head, mq[:, :LANES], 0.0) for cend, mq in zip(cend_, mq_)]
    q_ = [jnp.where(same_head, mq[:, LANES:], 0.0) for mq in mq_]

    pairs = range(n_pairs)
    H_ = [h_ref[q] for q in pairs]
    ys_ = [[] for _ in pairs]
    rm_ = [jnp.concatenate([rp, m], axis=0) for rp, m in zip(rp_, m_)]
    for c in range(n_chunks):
        both_ = [_dot(rm_[q * n_chunks + c], H_[q]) for q in pairs]
        for q in pairs:
            ys_[q].append(both_[q][:C] + p3_[q * n_chunks + c])
        H_ = [both_[q][C:] + q_[q * n_chunks + c] for q in pairs]
    for q in pairs:
        h_ref[q] = H_[q]

    emean = emean_ref[...]
    y_ = [jnp.concatenate(ys, axis=0) for ys in ys_]
    mu_ = [_dot2_lhs(y, emean) for y in y_]
    yc_ = [y - mu for y, mu in zip(y_, mu_)]
    var_ = [_dot2_lhs(yc * yc, emean) for yc in yc_]
    for q in pairs:
        yn = yc_[q] * lax.rsqrt(var_[q] + GN_EPS) * gng_ref[q] + gnb_ref[q]
        o_ref[0, q] = ((yn + bonus_ref[0, q]) * g_ref[0, q]).astype(BF16)


def _wkv(r, lw, k, v, a, b, g, bonus, gn_g, gn_b, emean):
    B, P, S, _ = r.shape
    tb = WKV_TB
    pp = WKV_PAIRS
    seq = pl.BlockSpec((1, pp, tb, LANES), lambda bi, p, s: (bi, p, s, 0))
    par = pl.BlockSpec((pp, 1, LANES), lambda bi, p, s: (p, 0, 0))
    return pl.pallas_call(
        _wkv_kernel,
        grid=(B, P // pp, S // tb),
        in_specs=[seq] * 8 + [par, par, pl.BlockSpec((2 * LANES, LANES), lambda bi, p, s: (0, 0))],
        out_specs=seq,
        out_shape=jax.ShapeDtypeStruct((B, P, S, LANES), BF16),
        scratch_shapes=[pltpu.VMEM((pp, LANES, LANES), F32)],
        compiler_params=pltpu.CompilerParams(dimension_semantics=("arbitrary", "arbitrary", "arbitrary"),
                                             vmem_limit_bytes=VMEM_LIMIT),
        name="wkv",
    )(r, lw, k, v, a, b, g, bonus, gn_g, gn_b, emean)


def _mixer_kernel(x_ref, lng_ref, lnb_ref, ya_ref, yb_ref, wout_ref, l1g_ref, l1b_ref, wr_ref, br_ref, below_ref,
                  base_ref, x1_ref, route_ref, counts_ref, carry_ref):
    tm = x_ref.shape[1]

    @pl.when((pl.program_id(0) == 0) & (pl.program_id(1) == 0))
    def _():
        carry_ref[...] = jnp.zeros_like(carry_ref)

    x0 = _layer_norm(x_ref[0], lng_ref[...], lnb_ref[...], LN_EPS)
    ymix = jnp.concatenate([ya_ref[0, p] for p in range(N_PAIRS)] + [yb_ref[0]], axis=-1)
    mix = jnp.dot(ymix, wout_ref[...], preferred_element_type=F32)
    x1 = _layer_norm(ALPHA * x0 + mix, l1g_ref[...], l1b_ref[...], LN_EPS)
    x1b = x1.astype(BF16)
    half = D_MODEL // 2
    lo_bits = lax.bitcast_convert_type(x1b[:, :half].astype(F32), jnp.uint32)
    hi_bits = lax.bitcast_convert_type(x1b[:, half:].astype(F32), jnp.uint32)
    x1_ref[0] = (hi_bits & jnp.uint32(0xFFFF0000)) | (lo_bits >> 16)

    hi, mid = _split2(x1)
    wide = jnp.dot(hi, wr_ref[...], preferred_element_type=F32)
    logits = (wide[:, :LANES] + wide[:, LANES:]
              + jnp.dot(mid, wr_ref[:, :LANES], preferred_element_type=F32)) + br_ref[...]
    lane = _iota((tm, LANES), 1).astype(F32)
    far = float(4 * LANES)
    is_g = jnp.where(lane >= N_EXPERTS, jnp.where(lane < N_EXPERTS + N_GROUPS, 1.0, 0.0), 0.0) > 0.5
    gl = jnp.where(is_g, logits, NEG)
    gmax = jnp.max(gl, axis=-1, keepdims=True)
    gsel = jnp.min(jnp.where(gl == gmax, lane, far), axis=-1, keepdims=True) - N_EXPERTS
    p_group = 1.0 / jnp.sum(jnp.where(is_g, jnp.exp(gl - gmax), 0.0), axis=-1, keepdims=True)
    grp_of_lane = jnp.floor(lane * (1.0 / EXPERTS_PER_GROUP))
    el = jnp.where(grp_of_lane == gsel, logits, NEG)
    v1 = jnp.max(el, axis=-1, keepdims=True)
    i1 = jnp.min(jnp.where(el == v1, lane, far), axis=-1, keepdims=True)
    el2 = jnp.where(lane == i1, NEG, el)
    v2 = jnp.max(el2, axis=-1, keepdims=True)
    i2 = jnp.min(jnp.where(el2 == v2, lane, far), axis=-1, keepdims=True)
    e21 = jnp.exp(v2 - v1)
    w1 = p_group / (1.0 + e21)
    w2 = p_group * e21 / (1.0 + e21)

    oh1 = lane == i1
    oh2 = lane == i2
    below = below_ref[...]
    o1 = jnp.where(oh1, 1.0, 0.0)
    o2 = jnp.where(oh2, 1.0, 0.0)
    c12 = jnp.dot(below, jnp.concatenate([o1, o2], axis=1).astype(BF16), preferred_element_type=F32)
    c1 = c12[:, :LANES]
    c2 = c12[:, LANES:]
    tot1 = jnp.sum(o1, axis=0, keepdims=True)
    carry = carry_ref[0:1, :]
    rank1 = jnp.sum(jnp.where(oh1, c1 + carry, 0.0), axis=-1, keepdims=True)
    rank2 = jnp.sum(jnp.where(oh2, c2 + carry + tot1, 0.0), axis=-1, keepdims=True)
    carry = carry + tot1 + jnp.sum(o2, axis=0, keepdims=True)
    carry_ref[0:1, :] = carry
    counts_ref[...] = jnp.broadcast_to(carry, counts_ref.shape)

    fields = (i1, i2, w1, w2, rank1, rank2)
    route = jnp.zeros((tm, LANES), F32)
    for n, f in enumerate(fields):
        route = jnp.where(lane == n, f, route)
    route_ref[0] = route

    base_ref[0] = ALPHA * x1


def _mixer(x, ln_g, ln_b, ya, yb, w_out, l1g, l1b, wr3, br):
    B, S, _ = x.shape
    tm = MIX_TM
    const = lambda shape: pl.BlockSpec(shape, lambda b, s: (0,) * len(shape))
    row = lambda w: pl.BlockSpec((1, tm, w), lambda b, s: (b, s, 0))
    below = (jnp.arange(tm)[:, None] > jnp.arange(tm)[None, :]).astype(BF16)
    return pl.pallas_call(
        _mixer_kernel,
        grid=(B, S // tm),
        in_specs=[
            row(D_MODEL), const((1, D_MODEL)), const((1, D_MODEL)),
            pl.BlockSpec((1, N_PAIRS, tm, LANES), lambda b, s: (b, 0, s, 0)), row(D_GMLP),
            const((D_MODEL, D_MODEL)), const((1, D_MODEL)), const((1, D_MODEL)),
            const((D_MODEL, 2 * LANES)), const((1, LANES)), const((tm, tm)),
        ],
        out_specs=[row(D_MODEL), row(D_MODEL // 2), row(LANES), const((8, LANES))],
        out_shape=[jax.ShapeDtypeStruct((B, S, D_MODEL), F32), jax.ShapeDtypeStruct((B, S, D_MODEL // 2), jnp.uint32),
                   jax.ShapeDtypeStruct((B, S, LANES), F32), jax.ShapeDtypeStruct((8, LANES), F32)],
        scratch_shapes=[pltpu.VMEM((8, LANES), F32)],
        compiler_params=pltpu.CompilerParams(dimension_semantics=("arbitrary", "arbitrary"),
                                             vmem_limit_bytes=VMEM_LIMIT),
        name="mixer",
    )(x, ln_g, ln_b, ya, yb, w_out, l1g, l1b, wr3, br, below)


def _slots_kernel(route_ref, counts_ref, dest_ref, pend_ref):
    tm = route_ref.shape[0]
    lane = _iota((tm, LANES), 1)
    route = route_ref[...]
    oh1 = lane == route[:, 0:1].astype(jnp.int32)
    oh2 = lane == route[:, 1:2].astype(jnp.int32)

    counts = counts_ref[0:1, :]
    padded = jnp.floor((counts + (EXPERT_ROWS - 1)) * (1.0 / EXPERT_ROWS)) * EXPERT_ROWS
    upper = (_iota((LANES, LANES), 0) <= _iota((LANES, LANES), 1)).astype(BF16)
    pend = _dot3_lhs(jnp.broadcast_to(padded, (8, LANES)), upper)[0:1, :]
    pstart = pend - padded
    d1 = jnp.sum(jnp.where(oh1, pstart, 0.0), axis=-1, keepdims=True) + route[:, 4:5]
    d2 = jnp.sum(jnp.where(oh2, pstart, 0.0), axis=-1, keepdims=True) + route[:, 5:6]
    dest = jnp.where(lane == 0, d1, jnp.where(lane == 1, d2, 0.0))
    dest_ref[...] = jnp.transpose(dest)[0:dest_ref.shape[0], :].astype(jnp.int32)
    pend_ref[...] = jnp.broadcast_to(pend, (8, LANES)).astype(jnp.int32)


def _slots(route, counts):
    T = route.shape[0]
    tm = SLOT_TM
    return pl.pallas_call(
        _slots_kernel,
        grid=(T // tm,),
        in_specs=[pl.BlockSpec((tm, LANES), lambda i: (i, 0)), pl.BlockSpec((8, LANES), lambda i: (0, 0))],
        out_specs=[pl.BlockSpec((8, tm), lambda i: (0, i)),
                   pl.BlockSpec((8, LANES), lambda i: (0, 0))],
        out_shape=[jax.ShapeDtypeStruct((8, T), jnp.int32), jax.ShapeDtypeStruct((8, LANES), jnp.int32)],
        compiler_params=pltpu.CompilerParams(dimension_semantics=("arbitrary",), vmem_limit_bytes=VMEM_LIMIT),
        name="slots",
    )(route, counts)


def _dispatch_kernel(pend_ref, dest0_ref, dest1_ref, x_ref, base_ref, p_ref, wpg_ref, bpg_ref, wpp_ref, xs_ref, resid_ref,
                     zero_ref, sem, zsem):
    tm = x_ref.shape[0]
    dest_refs = (dest0_ref, dest1_ref)

    @pl.when(pl.program_id(0) == 0)
    def _():
        zero_ref[...] = jnp.zeros_like(zero_ref)

        def tail(e):
            start = pl.multiple_of(jnp.maximum(pend_ref[e] - EXPERT_ROWS, 0), EXPERT_ROWS)
            return pltpu.make_async_copy(zero_ref, xs_ref.at[pl.ds(start, EXPERT_ROWS)], zsem)

        def unused(j):
            return pltpu.make_async_copy(
                zero_ref, xs_ref.at[pl.ds(pl.multiple_of(j * EXPERT_ROWS, EXPERT_ROWS), EXPERT_ROWS)], zsem)

        def start_unused(j, _):
            unused(j).start()
            return 0

        def wait_unused(j, _):
            unused(j).wait()
            return 0

        first_unused = pend_ref[N_EXPERTS - 1] // EXPERT_ROWS
        n_blocks = xs_ref.shape[0] // EXPERT_ROWS
        for e in range(N_EXPERTS):
            tail(e).start()
        lax.fori_loop(first_unused, n_blocks, start_unused, 0)
        for e in range(N_EXPERTS):
            tail(e).wait()
        lax.fori_loop(first_unused, n_blocks, wait_unused, 0)

    for t in range(tm):
        for j in range(TOP_K):
            pltpu.make_async_copy(x_ref.at[pl.ds(t, 1)], xs_ref.at[pl.ds(dest_refs[j][t], 1)],
                                  sem).start(priority=j)

    xw = x_ref[...]
    x_lo = lax.bitcast_convert_type(xw << 16, F32)
    x_hi = lax.bitcast_convert_type(xw & jnp.uint32(0xFFFF0000), F32)
    x1b = jnp.concatenate([x_lo, x_hi], axis=1).astype(BF16)
    gate = _sigmoid(jnp.dot(x1b, wpg_ref[...], preferred_element_type=F32) + bpg_ref[...])
    ple = gate * jnp.dot(p_ref[...].astype(BF16), wpp_ref[...], preferred_element_type=F32)
    resid_ref[...] = base_ref[...] + ple

    for j in range(TOP_K):
        pltpu.make_async_copy(x_ref, xs_ref.at[pl.ds(0, tm)], sem).wait()


def _dispatch(pend, dests, x1, base, p, wpg, bpg, wpp, n_rows):
    T, width = x1.shape
    tm = DISPATCH_TM
    const = lambda shape: pl.BlockSpec(shape, lambda i, pe: (0,) * len(shape))
    tile = lambda w: pl.BlockSpec((tm, w), lambda i, pe: (i, 0))
    index_list = pl.BlockSpec((tm,), lambda i, pe: (i,), memory_space=pltpu.SMEM)
    return pl.pallas_call(
        _dispatch_kernel,
        grid_spec=pltpu.PrefetchScalarGridSpec(
            num_scalar_prefetch=1,
            grid=(T // tm,),
            in_specs=[index_list, index_list,
                      tile(width), tile(D_MODEL), tile(D_PLE),
                      const((D_MODEL, D_MODEL)), const((1, D_MODEL)), const((D_PLE, D_MODEL))],
            out_specs=[pl.BlockSpec(memory_space=pl.ANY), tile(D_MODEL)],
            scratch_shapes=[pltpu.VMEM((EXPERT_ROWS, width), x1.dtype), pltpu.SemaphoreType.DMA,
                            pltpu.SemaphoreType.DMA],
        ),
        out_shape=[jax.ShapeDtypeStruct((n_rows, width), x1.dtype), jax.ShapeDtypeStruct((T, D_MODEL), F32)],
        compiler_params=pltpu.CompilerParams(dimension_semantics=("arbitrary",), vmem_limit_bytes=VMEM_LIMIT),
        name="dispatch",
    )(pend, dests[0], dests[1], x1, base, p, wpg, bpg, wpp)


def _experts_kernel(pend_ref, xs_ref, wg_ref, wu_ref, wd_ref, ys_ref, xbuf_ref, ybuf_ref, wgu_ref, wdb_ref,
                    in_sem, out_sem):
    rows = EXPERT_ROWS
    e = pl.program_id(0)
    first = jnp.where(e == 0, 0, pend_ref[jnp.maximum(e - 1, 0)]) // rows
    last = pend_ref[e] // rows
    n_used = pend_ref[N_EXPERTS - 1] // rows

    def block_rows(ref, b):
        return ref.at[pl.ds(pl.multiple_of(b * rows, rows), rows)]

    depth = xbuf_ref.shape[0]
    row_priority = 1

    def x_copy(b):
        slot = b % depth
        return pltpu.make_async_copy(block_rows(xs_ref, b), xbuf_ref.at[slot], in_sem.at[slot])

    def y_copy(b):
        slot = b % depth
        return pltpu.make_async_copy(ybuf_ref.at[slot], block_rows(ys_ref, b), out_sem.at[slot])

    @pl.when(e == 0)
    def _():
        for ahead in range(depth - 1):
            @pl.when(ahead < n_used)
            def _():
                x_copy(ahead).start(priority=row_priority)

    @pl.when(last > first)
    def _():
        wgu_ref[:, :D_EXPERT] = wg_ref[0].astype(BF16)
        wgu_ref[:, D_EXPERT:] = wu_ref[0].astype(BF16)
        wdb_ref[...] = wd_ref[0].astype(BF16)

        def body(b, _):
            slot = b % depth

            @pl.when(b + depth - 1 < n_used)
            def _():
                x_copy(b + depth - 1).start(priority=row_priority)

            x_copy(b).wait()

            @pl.when(b >= depth)
            def _():
                y_copy(b - depth).wait()

            xw = xbuf_ref[slot]
            x_lo = lax.bitcast_convert_type(xw << 16, F32)
            x_hi = lax.bitcast_convert_type(xw & jnp.uint32(0xFFFF0000), F32)
            xb = jnp.concatenate([x_lo, x_hi], axis=1).astype(BF16)
            h = jnp.dot(xb, wgu_ref[...], preferred_element_type=F32)
            hg = h[:, :D_EXPERT]
            hid = hg * _sigmoid(hg) * h[:, D_EXPERT:]
            ybuf_ref[slot] = jnp.dot(hid.astype(BF16), wdb_ref[...], preferred_element_type=F32)
            y_copy(b).start(priority=row_priority)
            return 0

        lax.fori_loop(first, last, body, 0)

    @pl.when(e == N_EXPERTS - 1)
    def _():
        for back in range(depth, 0, -1):
            @pl.when(n_used >= back)
            def _():
                y_copy(n_used - back).wait()

        ybuf_ref[0] = jnp.zeros(ybuf_ref.shape[1:], F32)

        def unused(b):
            return pltpu.make_async_copy(ybuf_ref.at[0], block_rows(ys_ref, b), out_sem.at[0])

        def start_unused(b, _):
            unused(b).start()
            return 0

        def wait_unused(b, _):
            unused(b).wait()
            return 0

        n_blocks = ys_ref.shape[0] // rows
        lax.fori_loop(n_used, n_blocks, start_unused, 0)
        lax.fori_loop(n_used, n_blocks, wait_unused, 0)


def _experts(pend, xs, wg, wu, wd):
    n_rows = xs.shape[0]
    rows = EXPERT_ROWS
    wspec = lambda shape: pl.BlockSpec((1,) + shape, lambda e, pe: (e, 0, 0))
    return pl.pallas_call(
        _experts_kernel,
        grid_spec=pltpu.PrefetchScalarGridSpec(
            num_scalar_prefetch=1,
            grid=(N_EXPERTS,),
            in_specs=[pl.BlockSpec(memory_space=pl.ANY),
                      wspec((D_MODEL, D_EXPERT)), wspec((D_MODEL, D_EXPERT)), wspec((D_EXPERT, D_MODEL))],
            out_specs=pl.BlockSpec(memory_space=pl.ANY),
            scratch_shapes=[pltpu.VMEM((EXPERT_DEPTH, rows, D_MODEL // 2), jnp.uint32),
                            pltpu.VMEM((EXPERT_DEPTH, rows, D_MODEL), F32),
                            pltpu.VMEM((D_MODEL, 2 * D_EXPERT), BF16), pltpu.VMEM((D_EXPERT, D_MODEL), BF16),
                            pltpu.SemaphoreType.DMA((EXPERT_DEPTH,)), pltpu.SemaphoreType.DMA((EXPERT_DEPTH,))],
        ),
        out_shape=jax.ShapeDtypeStruct((n_rows, D_MODEL), F32),
        compiler_params=pltpu.CompilerParams(dimension_semantics=("arbitrary",), vmem_limit_bytes=VMEM_LIMIT),
        name="experts",
    )(pend, xs, wg, wu, wd)


def _combine_kernel(dest0_ref, dest1_ref, dest0_next_ref, dest1_next_ref, ys_ref, resid_ref, route_ref, lg_ref, lb_ref,
                    o_ref, buf_ref, sem):
    tm = buf_ref.shape[2]
    i = pl.program_id(0)
    dest_refs = (dest0_ref, dest1_ref)
    dest_next_refs = (dest0_next_ref, dest1_next_ref)

    def gather(drefs, offset, s):
        for t in range(tm):
            for j in range(TOP_K):
                pltpu.make_async_copy(ys_ref.at[pl.ds(drefs[j][offset + t], 1)],
                                      buf_ref.at[s, j, pl.ds(t, 1)], sem.at[s]).start(priority=j)

    def drain(s):
        for j in range(TOP_K):
            pltpu.make_async_copy(ys_ref.at[pl.ds(0, tm)], buf_ref.at[s, j], sem.at[s]).wait()

    def finish(s):
        rows = slice(s * tm, (s + 1) * tm)
        drain(s)
        route = route_ref[rows, :]
        ffn = buf_ref[s, 0] * route[:, 2:3] + buf_ref[s, 1] * route[:, 3:4]
        o_ref[rows, :] = _layer_norm(resid_ref[rows, :] + ffn, lg_ref[...], lb_ref[...], LN_EPS)

    @pl.when(i == 0)
    def _():
        gather(dest_refs, 0, 0)

    gather(dest_refs, tm, 1)
    finish(0)
    gather(dest_next_refs, 0, 0)
    finish(1)

    @pl.when(i == pl.num_programs(0) - 1)
    def _():
        drain(0)


def _combine(dests, ys, resid, route, l2g, l2b):
    T = resid.shape[0]
    tm = COMBINE_TM
    nt = T // tm
    tile = lambda w: pl.BlockSpec((2 * tm, w), lambda i: (i, 0))
    const = lambda shape: pl.BlockSpec(shape, lambda i: (0,) * len(shape))
    pair_list = pl.BlockSpec((2 * tm,), lambda i: (i,), memory_space=pltpu.SMEM)
    next_list = pl.BlockSpec((tm,), lambda i: (jnp.minimum(2 * i + 2, nt - 1),), memory_space=pltpu.SMEM)
    return pl.pallas_call(
        _combine_kernel,
        grid=(nt // 2,),
        in_specs=[pair_list, pair_list, next_list, next_list,
                  pl.BlockSpec(memory_space=pl.ANY),
                  tile(D_MODEL), tile(LANES), const((1, D_MODEL)), const((1, D_MODEL))],
        out_specs=pl.BlockSpec((2 * tm, D_MODEL), lambda i: (i, 0)),
        out_shape=jax.ShapeDtypeStruct((T, D_MODEL), F32),
        scratch_shapes=[pltpu.VMEM((2, TOP_K, tm, D_MODEL), F32), pltpu.SemaphoreType.DMA((2,))],
        compiler_params=pltpu.CompilerParams(dimension_semantics=("arbitrary",), vmem_limit_bytes=VMEM_LIMIT),
        name="combine",
    )(dests[0], dests[1], dests[0], dests[1], ys, resid, route, l2g, l2b)


def _block_diag_const(n, blk, val):
    idx = jnp.arange(n) // blk
    return jnp.where(idx[:, None] == idx[None, :], val, 0.0).astype(BF16)


def kernel(x, p, ln_emb_g, ln_emb_b, w_in, mu_shift, w0, w_decay_up, a0, w_iclr_up, w_gate_up, k_k, k_a, r_k, gn_g, gn_b, gmlp_ln_g, gmlp_ln_b, w_spatial, b_spatial, w_out, ln1_g, ln1_b, w_group_router, b_group_router, w_expert_router, b_expert_router, w_exp_gate, w_exp_up, w_exp_down, w_ple_gate, b_ple_gate, w_ple_proj, ln2_g, ln2_b):
    B, S, D = x.shape
    T = B * S
    row = lambda t: t.reshape(1, -1).astype(F32)

    zl = jnp.zeros((DECAY_LORA, D_RWKV), F32)
    wwa = jnp.concatenate([jnp.concatenate([w_decay_up[0], zl], axis=1),
                           jnp.concatenate([zl, w_iclr_up[0]], axis=1)], axis=0).astype(BF16)
    w0a0 = jnp.concatenate([w0[0], a0[0]]).reshape(1, -1)
    eones = jnp.tile(_block_diag_const(2 * LANES, HEAD, 1.0), (2, 1))
    emean = jnp.tile(_block_diag_const(LANES, HEAD, 1.0 / HEAD), (2, 1))

    r, lw, k, v, a, b, g, bonus, yb = _prep(
        x, row(ln_emb_g), row(ln_emb_b), w_in[0].astype(BF16), row(mu_shift[0]), wwa, w0a0,
        w_gate_up[0].astype(BF16), row(k_k[0]), row(k_a[0]), row(r_k[0]), eones,
        row(gmlp_ln_g[0]), row(gmlp_ln_b[0]), w_spatial[0], b_spatial[0].T)

    ya = _wkv(r, lw, k, v, a, b, g, bonus, gn_g[0].reshape(N_PAIRS, 1, LANES), gn_b[0].reshape(N_PAIRS, 1, LANES),
              emean)

    wr = jnp.concatenate([w_expert_router[0].reshape(D, N_EXPERTS), w_group_router[0],
                          jnp.zeros((D, LANES - N_EXPERTS - N_GROUPS), F32)], axis=1)
    wr3 = jnp.concatenate(_split2(wr), axis=1)
    br = jnp.concatenate([b_expert_router[0].reshape(-1), b_group_router[0],
                          jnp.zeros((LANES - N_EXPERTS - N_GROUPS,), F32)]).reshape(1, LANES)
    base, x1, route, counts = _mixer(x, row(ln_emb_g), row(ln_emb_b), ya, yb, w_out[0].astype(BF16), row(ln1_g[0]),
                                     row(ln1_b[0]), wr3, br)
    base = base.reshape(T, D)
    x1 = x1.reshape(T, D // 2)
    route = route.reshape(T, LANES)

    n_blocks = -(-(T * TOP_K) // EXPERT_ROWS) + N_EXPERTS
    dest, pend = _slots(route, counts)
    dests = (dest[0], dest[1])
    pend = pend[0, :N_EXPERTS]

    xs, resid = _dispatch(pend, dests, x1, base, p[0].reshape(T, D_PLE), w_ple_gate[0].astype(BF16),
                          row(b_ple_gate[0]), w_ple_proj[0].astype(BF16), n_blocks * EXPERT_ROWS)
    ys = _experts(pend, xs, w_exp_gate[0], w_exp_up[0], w_exp_down[0])
    out = _combine(dests, ys, resid, route, row(ln2_g[0]), row(ln2_b[0]))
    return out.reshape(B, S, D)
```

```python
import math

import jax
import jax.numpy as jnp
from jax import lax
from jax.experimental import pallas as pl
from jax.experimental.pallas import tpu as pltpu

F32 = jnp.float32
BF16 = jnp.bfloat16

D_MODEL = 1024
D_RWKV = 512
HEAD = 64
D_GMLP = 512
GMLP_GROUPS = 4
GROUP_W = 128
GCHUNK = 128
DECAY_LORA = 64
ICLR_LORA = 64
GATE_LORA = 128
N_SHIFT = 3 * D_RWKV + DECAY_LORA + ICLR_LORA + GATE_LORA
D_IN = N_SHIFT + 2 * D_GMLP
D_PLE = 256
N_GROUPS = 4
EXPERTS_PER_GROUP = 8
N_EXPERTS = 32
TOP_K = 2
D_EXPERT = 512
DEPTH = 1
ALPHA = (2.0 * DEPTH) ** 0.25
LN_EPS = 1e-5
GN_EPS = 64e-5
DECAY_SCALE = math.exp(-0.5)

LANES = 128
WKV_CHUNK = 64
N_PAIRS = D_RWKV // LANES
VMEM_LIMIT = 56 * 1024 * 1024

PREP_TM = 512
WKV_TB = 512
WKV_PAIRS = 4
MIX_TM = 512
SLOT_TM = 2048
EXPERT_ROWS = 256
EXPERT_DEPTH = 4
DISPATCH_TM = 1024
COMBINE_TM = 256
NEG = -1e30


def _dot(a, b):
    return jnp.dot(a.astype(BF16), b.astype(BF16), preferred_element_type=F32)


def _dot_nt(a, b):
    return lax.dot_general(a.astype(BF16), b.astype(BF16), (((1,), (1,)), ((), ())),
                           preferred_element_type=F32)


def _split3(x):
    hi = x.astype(BF16)
    r1 = x - hi.astype(F32)
    mid = r1.astype(BF16)
    lo = (r1 - mid.astype(F32)).astype(BF16)
    return hi, mid, lo


def _dot3_lhs(x, w):
    hi, mid, lo = _split3(x)
    w = w.astype(BF16)
    return (jnp.dot(hi, w, preferred_element_type=F32) + jnp.dot(mid, w, preferred_element_type=F32)
            + jnp.dot(lo, w, preferred_element_type=F32))


def _split2(x):
    hi = x.astype(BF16)
    return hi, (x - hi.astype(F32)).astype(BF16)


def _dot2_lhs(x, w2):
    hi, lo = _split2(x)
    return jnp.dot(jnp.concatenate([hi, lo], axis=1), w2, preferred_element_type=F32)


def _dot3_rhs(w3, x):
    hi, mid, lo = _split3(x)
    return jnp.dot(w3, jnp.concatenate([hi, mid, lo], axis=0), preferred_element_type=F32)


def _layer_norm(x, g, b, eps):
    mu = jnp.mean(x, axis=-1, keepdims=True)
    xc = x - mu
    var = jnp.mean(xc * xc, axis=-1, keepdims=True)
    return xc * lax.rsqrt(var + eps) * g + b


def _sigmoid(x):
    return 1.0 / (1.0 + jnp.exp(-x))


def _iota(shape, dim):
    return lax.broadcasted_iota(jnp.int32, shape, dim)


def _prep_kernel(x_ref, lng_ref, lnb_ref, win_ref, mu_ref, wwa_ref, w0a0_ref, wg_ref, kk_ref, ka_ref, rk_ref,
                 eones_ref, glng_ref, glnb_ref, wsp_ref, bsp_ref,
                 r_ref, lw_ref, k_ref, v_ref, a_ref, b_ref, g_ref, bonus_ref, yb_ref, carry_ref):
    tm = x_ref.shape[1]

    @pl.when(pl.program_id(1) == 0)
    def _():
        carry_ref[...] = jnp.zeros_like(carry_ref)

    x0 = _layer_norm(x_ref[0], lng_ref[...], lnb_ref[...], LN_EPS)
    proj = jnp.dot(x0.astype(BF16), win_ref[...], preferred_element_type=F32)

    h = proj[:, :N_SHIFT]
    rolled = pltpu.roll(h, 1, 0)
    first = _iota((tm, N_SHIFT), 0) == 0
    prev = jnp.where(first, jnp.broadcast_to(carry_ref[0:1, :], (tm, N_SHIFT)), rolled)
    carry_ref[0:1, :] = h[tm - 1:tm, :]
    h = h + (prev - h) * mu_ref[...]

    r = h[:, 0:D_RWKV]
    k = h[:, D_RWKV:2 * D_RWKV]
    v = h[:, 2 * D_RWKV:3 * D_RWKV]
    xwa = h[:, 3 * D_RWKV:3 * D_RWKV + LANES]
    xg = h[:, 3 * D_RWKV + LANES:N_SHIFT]

    lane = _iota((tm, LANES), 1)
    twa = jnp.where(lane < DECAY_LORA, jnp.tanh(xwa), xwa)
    da = _dot(twa, wwa_ref[...]) + w0a0_ref[...]
    logw = -DECAY_SCALE * _sigmoid(da[:, :D_RWKV])
    ag = _sigmoid(da[:, D_RWKV:])
    g = _dot(_sigmoid(xg), wg_ref[...])

    eones2 = eones_ref[...]

    def head_sum(t):
        half = 2 * LANES
        return jnp.concatenate([_dot2_lhs(t[:, :half], eones2), _dot2_lhs(t[:, half:], eones2)], axis=1)

    kk = k * kk_ref[...]
    kk = kk * lax.rsqrt(jnp.maximum(head_sum(kk * kk), 1e-24))
    k = k * (1.0 + (ag - 1.0) * ka_ref[...])
    bonus = head_sum(r * k * rk_ref[...]) * v

    for p in range(N_PAIRS):
        sl = slice(p * LANES, (p + 1) * LANES)
        r_ref[0, p] = r[:, sl]
        lw_ref[0, p] = logw[:, sl]
        k_ref[0, p] = k[:, sl]
        v_ref[0, p] = v[:, sl]
        a_ref[0, p] = -kk[:, sl]
        b_ref[0, p] = (kk * ag)[:, sl]
        g_ref[0, p] = g[:, sl]
        bonus_ref[0, p] = bonus[:, sl]

    zin = proj[:, N_SHIFT:]
    z = 0.5 * zin * (1.0 + lax.erf(zin * (0.5 ** 0.5)))
    zu = z[:, :D_GMLP]
    zv = z[:, D_GMLP:]
    causal = _iota((GCHUNK, GCHUNK), 0) >= _iota((GCHUNK, GCHUNK), 1)
    for gi in range(GMLP_GROUPS):
        gs = slice(gi * GROUP_W, (gi + 1) * GROUP_W)
        zvn = _layer_norm(zv[:, gs], glng_ref[:, gs], glnb_ref[:, gs], LN_EPS)
        ws = jnp.where(causal, wsp_ref[gi], 0.0).astype(BF16)
        bcol = bsp_ref[:, gi:gi + 1]
        chunks = [slice(c * GCHUNK, (c + 1) * GCHUNK) for c in range(tm // GCHUNK)]
        zcat = jnp.concatenate([zvn[ts] for ts in chunks], axis=1).astype(BF16)
        mixed = jnp.dot(ws, zcat, preferred_element_type=F32) + bcol
        for ts in chunks:
            yb_ref[0, ts, gs] = (zu[ts, gs] * mixed[:, ts]).astype(BF16)


def _prep(x, ln_g, ln_b, w_in, mu, wwa, w0a0, wg, k_k, k_a, r_k, eones, glng, glnb, wsp, bsp):
    B, S, _ = x.shape
    tm = PREP_TM
    const = lambda shape: pl.BlockSpec(shape, lambda b, s: (0,) * len(shape))
    pair_spec = pl.BlockSpec((1, N_PAIRS, tm, LANES), lambda b, s: (b, 0, s, 0))
    pair_shape = jax.ShapeDtypeStruct((B, N_PAIRS, S, LANES), F32)
    return pl.pallas_call(
        _prep_kernel,
        grid=(B, S // tm),
        in_specs=[
            pl.BlockSpec((1, tm, D_MODEL), lambda b, s: (b, s, 0)),
            const((1, D_MODEL)), const((1, D_MODEL)), const((D_MODEL, D_IN)), const((1, N_SHIFT)),
            const((LANES, 2 * D_RWKV)), const((1, 2 * D_RWKV)), const((GATE_LORA, D_RWKV)),
            const((1, D_RWKV)), const((1, D_RWKV)), const((1, D_RWKV)), const((4 * LANES, 2 * LANES)),
            const((1, D_GMLP)), const((1, D_GMLP)), const((GMLP_GROUPS, GCHUNK, GCHUNK)),
            const((GCHUNK, GMLP_GROUPS)),
        ],
        out_specs=[pair_spec] * 8 + [pl.BlockSpec((1, tm, D_GMLP), lambda b, s: (b, s, 0))],
        out_shape=[pair_shape] * 8 + [jax.ShapeDtypeStruct((B, S, D_GMLP), BF16)],
        scratch_shapes=[pltpu.VMEM((8, N_SHIFT), F32)],
        compiler_params=pltpu.CompilerParams(dimension_semantics=("arbitrary", "arbitrary"),
                                             vmem_limit_bytes=VMEM_LIMIT),
        name="prep",
    )(x, ln_g, ln_b, w_in, mu, wwa, w0a0, wg, k_k, k_a, r_k, eones, glng, glnb, wsp, bsp)


def _wkv_kernel(r_ref, lw_ref, k_ref, v_ref, a_ref, b_ref, g_ref, bonus_ref, gng_ref, gnb_ref, emean_ref,
                o_ref, h_ref):
    C = WKV_CHUNK
    tb = r_ref.shape[2]

    @pl.when(pl.program_id(2) == 0)
    def _():
        h_ref[...] = jnp.zeros_like(h_ref)

    tok = _iota((C, LANES), 0)
    lane = _iota((C, LANES), 1)
    head0 = lane < HEAD
    strict = tok > lane % HEAD
    incl = tok >= lane % HEAD
    eye_w = (tok == lane % HEAD).astype(F32)
    rr = _iota((LANES, LANES), 0)
    cc = _iota((LANES, LANES), 1)
    eye = (rr == cc).astype(F32)
    same_head = (rr < HEAD) == (cc < HEAD)
    ltri3 = (_iota((C, 3 * C), 0) >= _iota((C, 3 * C), 1) % C).astype(BF16)

    def stack(x):
        xb = x.astype(BF16)
        zero = jnp.zeros_like(xb)
        return jnp.concatenate([jnp.where(head0, xb, zero), jnp.where(head0, zero, xb)], axis=0)

    def stack2(x, y):
        return jnp.concatenate([stack(x), stack(y)], axis=1)

    n_pairs = r_ref.shape[1]
    n_chunks = tb // C
    units = [(q, c) for q in range(n_pairs) for c in range(n_chunks)]

    def load(ref):
        return [ref[0, q, c * C:(c + 1) * C, :] for q, c in units]

    r_, lw_, k_, v_, a_, b_ = (load(ref) for ref in (r_ref, lw_ref, k_ref, v_ref, a_ref, b_ref))
    cum_ = [_dot3_rhs(ltri3, lw) for lw in lw_]
    cend_ = [cum[C - 1:C, :] for cum in cum_]
    at_ = [a * jnp.exp(cum - lw) for a, cum, lw in zip(a_, cum_, lw_)]
    rt_ = [r * jnp.exp(cum) for r, cum in zip(r_, cum_)]
    ginv_ = [jnp.exp(-cum) for cum in cum_]
    gend_ = [jnp.exp(cend - cum) for cend, cum in zip(cend_, cum_)]
    bk_end_ = [jnp.concatenate([b * ge, k * ge], axis=0) for b, k, ge in zip(b_, k_, gend_)]
    vst_ = [stack(v) for v in v_]

    G_ = [_dot_nt(jnp.concatenate([at, rt], axis=0), jnp.concatenate([stack(b * gi), stack(k * gi)], axis=0))
          for at, rt, b, k, gi in zip(at_, rt_, b_, k_, ginv_)]
    n1_ = [jnp.where(strict, G[:C, :LANES], 0.0) for G in G_]
    aak_ = [jnp.where(strict, G[:C, LANES:], 0.0) for G in G_]
    arb_ = [jnp.where(incl, G[C:, :LANES], 0.0) for G in G_]
    ark_ = [jnp.where(incl, G[C:, LANES:], 0.0) for G in G_]
    av_ = [_dot(jnp.concatenate([aak, ark], axis=0), vst) for aak, ark, vst in zip(aak_, ark_, vst_)]

    s1_ = [stack(n1) for n1 in n1_]
    n2_ = [_dot(n1, s1) for n1, s1 in zip(n1_, s1_)]
    x_ = [_dot(n2, jnp.concatenate([s1, stack(n2)], axis=1)) for n2, s1 in zip(n2_, s1_)]
    t_ = [eye_w + n1 + n2 + x[:, :LANES] for n1, n2, x in zip(n1_, n2_, x_)]
    np_ = [x[:, LANES:] for x in x_]
    for _ in range(3):
        x_ = [_dot(npow, stack2(t, npow)) for t, npow in zip(t_, np_)]
        t_ = [t + x[:, :LANES] for t, x in zip(t_, x_)]
        np_ = [x[:, LANES:] for x in x_]
    t_ = [t + _dot(npow, stack(t)) for t, npow in zip(t_, np_)]

    x_ = [_dot(t, stack2(at, av[:C])) for t, at, av in zip(t_, at_, av_)]
    z_ = [_dot(arb, stack2(x[:, :LANES], x[:, LANES:])) for arb, x in zip(arb_, x_)]
    rp_ = [rt + z[:, :LANES] for rt, z in zip(rt_, z_)]
    p3_ = [z[:, LANES:] + av[C:] for z, av in zip(z_, av_)]
    rhs_ = [jnp.concatenate([x, jnp.concatenate([jnp.zeros_like(v), v], axis=1)], axis=0)
            for x, v in zip(x_, v_)]
    mq_ = [_dot(bk_end.T, rhs) for bk_end, rhs in zip(bk_end_, rhs_)]
    m_ = [eye * jnp.exp(cend) + jnp.where(same_head, mq[:, :LANES], 0.0) for cend, mq in zip(cend_, mq_)]
    q_ = [jnp.where(same_head, mq[:, LANES:], 0.0) for mq in mq_]

    pairs = range(n_pairs)
    H_ = [h_ref[q] for q in pairs]
    ys_ = [[] for _ in pairs]
    rm_ = [jnp.concatenate([rp, m], axis=0) for rp, m in zip(rp_, m_)]
    for c in range(n_chunks):
        both_ = [_dot(rm_[q * n_chunks + c], H_[q]) for q in pairs]
        for q in pairs:
            ys_[q].append(both_[q][:C] + p3_[q * n_chunks + c])
        H_ = [both_[q][C:] + q_[q * n_chunks + c] for q in pairs]
    for q in pairs:
        h_ref[q] = H_[q]

    emean = emean_ref[...]
    y_ = [jnp.concatenate(ys, axis=0) for ys in ys_]
    mu_ = [_dot2_lhs(y, emean) for y in y_]
    yc_ = [y - mu for y, mu in zip(y_, mu_)]
    var_ = [_dot2_lhs(yc * yc, emean) for yc in yc_]
    for q in pairs:
        yn = yc_[q] * lax.rsqrt(var_[q] + GN_EPS) * gng_ref[q] + gnb_ref[q]
        o_ref[0, q] = ((yn + bonus_ref[0, q]) * g_ref[0, q]).astype(BF16)


def _wkv(r, lw, k, v, a, b, g, bonus, gn_g, gn_b, emean):
    B, P, S, _ = r.shape
    tb = WKV_TB
    pp = WKV_PAIRS
    seq = pl.BlockSpec((1, pp, tb, LANES), lambda bi, p, s: (bi, p, s, 0))
    par = pl.BlockSpec((pp, 1, LANES), lambda bi, p, s: (p, 0, 0))
    return pl.pallas_call(
        _wkv_kernel,
        grid=(B, P // pp, S // tb),
        in_specs=[seq] * 8 + [par, par, pl.BlockSpec((2 * LANES, LANES), lambda bi, p, s: (0, 0))],
        out_specs=seq,
        out_shape=jax.ShapeDtypeStruct((B, P, S, LANES), BF16),
        scratch_shapes=[pltpu.VMEM((pp, LANES, LANES), F32)],
        compiler_params=pltpu.CompilerParams(dimension_semantics=("arbitrary", "arbitrary", "arbitrary"),
                                             vmem_limit_bytes=VMEM_LIMIT),
        name="wkv",
    )(r, lw, k, v, a, b, g, bonus, gn_g, gn_b, emean)


def _mixer_kernel(x_ref, lng_ref, lnb_ref, ya_ref, yb_ref, wout_ref, l1g_ref, l1b_ref, wr_ref, br_ref, below_ref,
                  base_ref, x1_ref, route_ref, counts_ref, carry_ref):
    tm = x_ref.shape[1]

    @pl.when((pl.program_id(0) == 0) & (pl.program_id(1) == 0))
    def _():
        carry_ref[...] = jnp.zeros_like(carry_ref)

    x0 = _layer_norm(x_ref[0], lng_ref[...], lnb_ref[...], LN_EPS)
    ymix = jnp.concatenate([ya_ref[0, p] for p in range(N_PAIRS)] + [yb_ref[0]], axis=-1)
    mix = jnp.dot(ymix, wout_ref[...], preferred_element_type=F32)
    x1 = _layer_norm(ALPHA * x0 + mix, l1g_ref[...], l1b_ref[...], LN_EPS)
    x1b = x1.astype(BF16)
    half = D_MODEL // 2
    lo_bits = lax.bitcast_convert_type(x1b[:, :half].astype(F32), jnp.uint32)
    hi_bits = lax.bitcast_convert_type(x1b[:, half:].astype(F32), jnp.uint32)
    x1_ref[0] = (hi_bits & jnp.uint32(0xFFFF0000)) | (lo_bits >> 16)

    hi, mid = _split2(x1)
    wide = jnp.dot(hi, wr_ref[...], preferred_element_type=F32)
    logits = (wide[:, :LANES] + wide[:, LANES:]
              + jnp.dot(mid, wr_ref[:, :LANES], preferred_element_type=F32)) + br_ref[...]
    lane = _iota((tm, LANES), 1).astype(F32)
    far = float(4 * LANES)
    is_g = jnp.where(lane >= N_EXPERTS, jnp.where(lane < N_EXPERTS + N_GROUPS, 1.0, 0.0), 0.0) > 0.5
    gl = jnp.where(is_g, logits, NEG)
    gmax = jnp.max(gl, axis=-1, keepdims=True)
    gsel = jnp.min(jnp.where(gl == gmax, lane, far), axis=-1, keepdims=True) - N_EXPERTS
    p_group = 1.0 / jnp.sum(jnp.where(is_g, jnp.exp(gl - gmax), 0.0), axis=-1, keepdims=True)
    grp_of_lane = jnp.floor(lane * (1.0 / EXPERTS_PER_GROUP))
    el = jnp.where(grp_of_lane == gsel, logits, NEG)
    v1 = jnp.max(el, axis=-1, keepdims=True)
    i1 = jnp.min(jnp.where(el == v1, lane, far), axis=-1, keepdims=True)
    el2 = jnp.where(lane == i1, NEG, el)
    v2 = jnp.max(el2, axis=-1, keepdims=True)
    i2 = jnp.min(jnp.where(el2 == v2, lane, far), axis=-1, keepdims=True)
    e21 = jnp.exp(v2 - v1)
    w1 = p_group / (1.0 + e21)
    w2 = p_group * e21 / (1.0 + e21)

    oh1 = lane == i1
    oh2 = lane == i2
    below = below_ref[...]
    o1 = jnp.where(oh1, 1.0, 0.0)
    o2 = jnp.where(oh2, 1.0, 0.0)
    c12 = jnp.dot(below, jnp.concatenate([o1, o2], axis=1).astype(BF16), preferred_element_type=F32)
    c1 = c12[:, :LANES]
    c2 = c12[:, LANES:]
    tot1 = jnp.sum(o1, axis=0, keepdims=True)
    carry = carry_ref[0:1, :]
    rank1 = jnp.sum(jnp.where(oh1, c1 + carry, 0.0), axis=-1, keepdims=True)
    rank2 = jnp.sum(jnp.where(oh2, c2 + carry + tot1, 0.0), axis=-1, keepdims=True)
    carry = carry + tot1 + jnp.sum(o2, axis=0, keepdims=True)
    carry_ref[0:1, :] = carry
    counts_ref[...] = jnp.broadcast_to(carry, counts_ref.shape)

    fields = (i1, i2, w1, w2, rank1, rank2)
    route = jnp.zeros((tm, LANES), F32)
    for n, f in enumerate(fields):
        route = jnp.where(lane == n, f, route)
    route_ref[0] = route

    base_ref[0] = ALPHA * x1


def _mixer(x, ln_g, ln_b, ya, yb, w_out, l1g, l1b, wr3, br):
    B, S, _ = x.shape
    tm = MIX_TM
    const = lambda shape: pl.BlockSpec(shape, lambda b, s: (0,) * len(shape))
    row = lambda w: pl.BlockSpec((1, tm, w), lambda b, s: (b, s, 0))
    below = (jnp.arange(tm)[:, None] > jnp.arange(tm)[None, :]).astype(BF16)
    return pl.pallas_call(
        _mixer_kernel,
        grid=(B, S // tm),
        in_specs=[
            row(D_MODEL), const((1, D_MODEL)), const((1, D_MODEL)),
            pl.BlockSpec((1, N_PAIRS, tm, LANES), lambda b, s: (b, 0, s, 0)), row(D_GMLP),
            const((D_MODEL, D_MODEL)), const((1, D_MODEL)), const((1, D_MODEL)),
            const((D_MODEL, 2 * LANES)), const((1, LANES)), const((tm, tm)),
        ],
        out_specs=[row(D_MODEL), row(D_MODEL // 2), row(LANES), const((8, LANES))],
        out_shape=[jax.ShapeDtypeStruct((B, S, D_MODEL), F32), jax.ShapeDtypeStruct((B, S, D_MODEL // 2), jnp.uint32),
                   jax.ShapeDtypeStruct((B, S, LANES), F32), jax.ShapeDtypeStruct((8, LANES), F32)],
        scratch_shapes=[pltpu.VMEM((8, LANES), F32)],
        compiler_params=pltpu.CompilerParams(dimension_semantics=("arbitrary", "arbitrary"),
                                             vmem_limit_bytes=VMEM_LIMIT),
        name="mixer",
    )(x, ln_g, ln_b, ya, yb, w_out, l1g, l1b, wr3, br, below)


def _slots_kernel(route_ref, counts_ref, dest_ref, pend_ref):
    tm = route_ref.shape[0]
    lane = _iota((tm, LANES), 1)
    route = route_ref[...]
    oh1 = lane == route[:, 0:1].astype(jnp.int32)
    oh2 = lane == route[:, 1:2].astype(jnp.int32)

    counts = counts_ref[0:1, :]
    padded = jnp.floor((counts + (EXPERT_ROWS - 1)) * (1.0 / EXPERT_ROWS)) * EXPERT_ROWS
    upper = (_iota((LANES, LANES), 0) <= _iota((LANES, LANES), 1)).astype(BF16)
    pend = _dot3_lhs(jnp.broadcast_to(padded, (8, LANES)), upper)[0:1, :]
    pstart = pend - padded
    d1 = jnp.sum(jnp.where(oh1, pstart, 0.0), axis=-1, keepdims=True) + route[:, 4:5]
    d2 = jnp.sum(jnp.where(oh2, pstart, 0.0), axis=-1, keepdims=True) + route[:, 5:6]
    dest = jnp.where(lane == 0, d1, jnp.where(lane == 1, d2, 0.0))
    dest_ref[...] = jnp.transpose(dest)[0:dest_ref.shape[0], :].astype(jnp.int32)
    pend_ref[...] = jnp.broadcast_to(pend, (8, LANES)).astype(jnp.int32)


def _slots(route, counts):
    T = route.shape[0]
    tm = SLOT_TM
    return pl.pallas_call(
        _slots_kernel,
        grid=(T // tm,),
        in_specs=[pl.BlockSpec((tm, LANES), lambda i: (i, 0)), pl.BlockSpec((8, LANES), lambda i: (0, 0))],
        out_specs=[pl.BlockSpec((8, tm), lambda i: (0, i)),
                   pl.BlockSpec((8, LANES), lambda i: (0, 0))],
        out_shape=[jax.ShapeDtypeStruct((8, T), jnp.int32), jax.ShapeDtypeStruct((8, LANES), jnp.int32)],
        compiler_params=pltpu.CompilerParams(dimension_semantics=("arbitrary",), vmem_limit_bytes=VMEM_LIMIT),
        name="slots",
    )(route, counts)


def _dispatch_kernel(pend_ref, dest0_ref, dest1_ref, x_ref, base_ref, p_ref, wpg_ref, bpg_ref, wpp_ref, xs_ref, resid_ref,
                     zero_ref, sem, zsem):
    tm = x_ref.shape[0]
    dest_refs = (dest0_ref, dest1_ref)

    @pl.when(pl.program_id(0) == 0)
    def _():
        zero_ref[...] = jnp.zeros_like(zero_ref)

        def tail(e):
            start = pl.multiple_of(jnp.maximum(pend_ref[e] - EXPERT_ROWS, 0), EXPERT_ROWS)
            return pltpu.make_async_copy(zero_ref, xs_ref.at[pl.ds(start, EXPERT_ROWS)], zsem)

        def unused(j):
            return pltpu.make_async_copy(
                zero_ref, xs_ref.at[pl.ds(pl.multiple_of(j * EXPERT_ROWS, EXPERT_ROWS), EXPERT_ROWS)], zsem)

        def start_unused(j, _):
            unused(j).start()
            return 0

        def wait_unused(j, _):
            unused(j).wait()
            return 0

        first_unused = pend_ref[N_EXPERTS - 1] // EXPERT_ROWS
        n_blocks = xs_ref.shape[0] // EXPERT_ROWS
        for e in range(N_EXPERTS):
            tail(e).start()
        lax.fori_loop(first_unused, n_blocks, start_unused, 0)
        for e in range(N_EXPERTS):
            tail(e).wait()
        lax.fori_loop(first_unused, n_blocks, wait_unused, 0)

    for t in range(tm):
        for j in range(TOP_K):
            pltpu.make_async_copy(x_ref.at[pl.ds(t, 1)], xs_ref.at[pl.ds(dest_refs[j][t], 1)],
                                  sem).start(priority=j)

    xw = x_ref[...]
    x_lo = lax.bitcast_convert_type(xw << 16, F32)
    x_hi = lax.bitcast_convert_type(xw & jnp.uint32(0xFFFF0000), F32)
    x1b = jnp.concatenate([x_lo, x_hi], axis=1).astype(BF16)
    gate = _sigmoid(jnp.dot(x1b, wpg_ref[...], preferred_element_type=F32) + bpg_ref[...])
    ple = gate * jnp.dot(p_ref[...].astype(BF16), wpp_ref[...], preferred_element_type=F32)
    resid_ref[...] = base_ref[...] + ple

    for j in range(TOP_K):
        pltpu.make_async_copy(x_ref, xs_ref.at[pl.ds(0, tm)], sem).wait()


def _dispatch(pend, dests, x1, base, p, wpg, bpg, wpp, n_rows):
    T, width = x1.shape
    tm = DISPATCH_TM
    const = lambda shape: pl.BlockSpec(shape, lambda i, pe: (0,) * len(shape))
    tile = lambda w: pl.BlockSpec((tm, w), lambda i, pe: (i, 0))
    index_list = pl.BlockSpec((tm,), lambda i, pe: (i,), memory_space=pltpu.SMEM)
    return pl.pallas_call(
        _dispatch_kernel,
        grid_spec=pltpu.PrefetchScalarGridSpec(
            num_scalar_prefetch=1,
            grid=(T // tm,),
            in_specs=[index_list, index_list,
                      tile(width), tile(D_MODEL), tile(D_PLE),
                      const((D_MODEL, D_MODEL)), const((1, D_MODEL)), const((D_PLE, D_MODEL))],
            out_specs=[pl.BlockSpec(memory_space=pl.ANY), tile(D_MODEL)],
            scratch_shapes=[pltpu.VMEM((EXPERT_ROWS, width), x1.dtype), pltpu.SemaphoreType.DMA,
                            pltpu.SemaphoreType.DMA],
        ),
        out_shape=[jax.ShapeDtypeStruct((n_rows, width), x1.dtype), jax.ShapeDtypeStruct((T, D_MODEL), F32)],
        compiler_params=pltpu.CompilerParams(dimension_semantics=("arbitrary",), vmem_limit_bytes=VMEM_LIMIT),
        name="dispatch",
    )(pend, dests[0], dests[1], x1, base, p, wpg, bpg, wpp)


def _experts_kernel(pend_ref, xs_ref, wg_ref, wu_ref, wd_ref, ys_ref, xbuf_ref, ybuf_ref, wgu_ref, wdb_ref,
                    in_sem, out_sem):
    rows = EXPERT_ROWS
    e = pl.program_id(0)
    first = jnp.where(e == 0, 0, pend_ref[jnp.maximum(e - 1, 0)]) // rows
    last = pend_ref[e] // rows
    n_used = pend_ref[N_EXPERTS - 1] // rows

    def block_rows(ref, b):
        return ref.at[pl.ds(pl.multiple_of(b * rows, rows), rows)]

    depth = xbuf_ref.shape[0]
    row_priority = 1

    def x_copy(b):
        slot = b % depth
        return pltpu.make_async_copy(block_rows(xs_ref, b), xbuf_ref.at[slot], in_sem.at[slot])

    def y_copy(b):
        slot = b % depth
        return pltpu.make_async_copy(ybuf_ref.at[slot], block_rows(ys_ref, b), out_sem.at[slot])

    @pl.when(e == 0)
    def _():
        for ahead in range(depth - 1):
            @pl.when(ahead < n_used)
            def _():
                x_copy(ahead).start(priority=row_priority)

    @pl.when(last > first)
    def _():
        wgu_ref[:, :D_EXPERT] = wg_ref[0].astype(BF16)
        wgu_ref[:, D_EXPERT:] = wu_ref[0].astype(BF16)
        wdb_ref[...] = wd_ref[0].astype(BF16)

        def body(b, _):
            slot = b % depth

            @pl.when(b + depth - 1 < n_used)
            def _():
                x_copy(b + depth - 1).start(priority=row_priority)

            x_copy(b).wait()

            @pl.when(b >= depth)
            def _():
                y_copy(b - depth).wait()

            xw = xbuf_ref[slot]
            x_lo = lax.bitcast_convert_type(xw << 16, F32)
            x_hi = lax.bitcast_convert_type(xw & jnp.uint32(0xFFFF0000), F32)
            xb = jnp.concatenate([x_lo, x_hi], axis=1).astype(BF16)
            h = jnp.dot(xb, wgu_ref[...], preferred_element_type=F32)
            hg = h[:, :D_EXPERT]
            hid = hg * _sigmoid(hg) * h[:, D_EXPERT:]
            ybuf_ref[slot] = jnp.dot(hid.astype(BF16), wdb_ref[...], preferred_element_type=F32)
            y_copy(b).start(priority=row_priority)
            return 0

        lax.fori_loop(first, last, body, 0)

    @pl.when(e == N_EXPERTS - 1)
    def _():
        for back in range(depth, 0, -1):
            @pl.when(n_used >= back)
            def _():
                y_copy(n_used - back).wait()

        ybuf_ref[0] = jnp.zeros(ybuf_ref.shape[1:], F32)

        def unused(b):
            return pltpu.make_async_copy(ybuf_ref.at[0], block_rows(ys_ref, b), out_sem.at[0])

        def start_unused(b, _):
            unused(b).start()
            return 0

        def wait_unused(b, _):
            unused(b).wait()
            return 0

        n_blocks = ys_ref.shape[0] // rows
        lax.fori_loop(n_used, n_blocks, start_unused, 0)
        lax.fori_loop(n_used, n_blocks, wait_unused, 0)


def _experts(pend, xs, wg, wu, wd):
    n_rows = xs.shape[0]
    rows = EXPERT_ROWS
    wspec = lambda shape: pl.BlockSpec((1,) + shape, lambda e, pe: (e, 0, 0))
    return pl.pallas_call(
        _experts_kernel,
        grid_spec=pltpu.PrefetchScalarGridSpec(
            num_scalar_prefetch=1,
            grid=(N_EXPERTS,),
            in_specs=[pl.BlockSpec(memory_space=pl.ANY),
                      wspec((D_MODEL, D_EXPERT)), wspec((D_MODEL, D_EXPERT)), wspec((D_EXPERT, D_MODEL))],
            out_specs=pl.BlockSpec(memory_space=pl.ANY),
            scratch_shapes=[pltpu.VMEM((EXPERT_DEPTH, rows, D_MODEL // 2), jnp.uint32),
                            pltpu.VMEM((EXPERT_DEPTH, rows, D_MODEL), F32),
                            pltpu.VMEM((D_MODEL, 2 * D_EXPERT), BF16), pltpu.VMEM((D_EXPERT, D_MODEL), BF16),
                            pltpu.SemaphoreType.DMA((EXPERT_DEPTH,)), pltpu.SemaphoreType.DMA((EXPERT_DEPTH,))],
        ),
        out_shape=jax.ShapeDtypeStruct((n_rows, D_MODEL), F32),
        compiler_params=pltpu.CompilerParams(dimension_semantics=("arbitrary",), vmem_limit_bytes=VMEM_LIMIT),
        name="experts",
    )(pend, xs, wg, wu, wd)


def _combine_kernel(dest0_ref, dest1_ref, dest0_next_ref, dest1_next_ref, ys_ref, resid_ref, route_ref, lg_ref, lb_ref,
                    o_ref, buf_ref, sem):
    tm = buf_ref.shape[2]
    i = pl.program_id(0)
    dest_refs = (dest0_ref, dest1_ref)
    dest_next_refs = (dest0_next_ref, dest1_next_ref)

    def gather(drefs, offset, s):
        for t in range(tm):
            for j in range(TOP_K):
                pltpu.make_async_copy(ys_ref.at[pl.ds(drefs[j][offset + t], 1)],
                                      buf_ref.at[s, j, pl.ds(t, 1)], sem.at[s]).start(priority=j)

    def drain(s):
        for j in range(TOP_K):
            pltpu.make_async_copy(ys_ref.at[pl.ds(0, tm)], buf_ref.at[s, j], sem.at[s]).wait()

    def finish(s):
        rows = slice(s * tm, (s + 1) * tm)
        drain(s)
        route = route_ref[rows, :]
        ffn = buf_ref[s, 0] * route[:, 2:3] + buf_ref[s, 1] * route[:, 3:4]
        o_ref[rows, :] = _layer_norm(resid_ref[rows, :] + ffn, lg_ref[...], lb_ref[...], LN_EPS)

    @pl.when(i == 0)
    def _():
        gather(dest_refs, 0, 0)

    gather(dest_refs, tm, 1)
    finish(0)
    gather(dest_next_refs, 0, 0)
    finish(1)

    @pl.when(i == pl.num_programs(0) - 1)
    def _():
        drain(0)


def _combine(dests, ys, resid, route, l2g, l2b):
    T = resid.shape[0]
    tm = COMBINE_TM
    nt = T // tm
    tile = lambda w: pl.BlockSpec((2 * tm, w), lambda i: (i, 0))
    const = lambda shape: pl.BlockSpec(shape, lambda i: (0,) * len(shape))
    pair_list = pl.BlockSpec((2 * tm,), lambda i: (i,), memory_space=pltpu.SMEM)
    next_list = pl.BlockSpec((tm,), lambda i: (jnp.minimum(2 * i + 2, nt - 1),), memory_space=pltpu.SMEM)
    return pl.pallas_call(
        _combine_kernel,
        grid=(nt // 2,),
        in_specs=[pair_list, pair_list, next_list, next_list,
                  pl.BlockSpec(memory_space=pl.ANY),
                  tile(D_MODEL), tile(LANES), const((1, D_MODEL)), const((1, D_MODEL))],
        out_specs=pl.BlockSpec((2 * tm, D_MODEL), lambda i: (i, 0)),
        out_shape=jax.ShapeDtypeStruct((T, D_MODEL), F32),
        scratch_shapes=[pltpu.VMEM((2, TOP_K, tm, D_MODEL), F32), pltpu.SemaphoreType.DMA((2,))],
        compiler_params=pltpu.CompilerParams(dimension_semantics=("arbitrary",), vmem_limit_bytes=VMEM_LIMIT),
        name="combine",
    )(dests[0], dests[1], dests[0], dests[1], ys, resid, route, l2g, l2b)


def _block_diag_const(n, blk, val):
    idx = jnp.arange(n) // blk
    return jnp.where(idx[:, None] == idx[None, :], val, 0.0).astype(BF16)


def kernel(x, p, ln_emb_g, ln_emb_b, w_in, mu_shift, w0, w_decay_up, a0, w_iclr_up, w_gate_up, k_k, k_a, r_k, gn_g, gn_b, gmlp_ln_g, gmlp_ln_b, w_spatial, b_spatial, w_out, ln1_g, ln1_b, w_group_router, b_group_router, w_expert_router, b_expert_router, w_exp_gate, w_exp_up, w_exp_down, w_ple_gate, b_ple_gate, w_ple_proj, ln2_g, ln2_b):
    B, S, D = x.shape
    T = B * S
    row = lambda t: t.reshape(1, -1).astype(F32)

    zl = jnp.zeros((DECAY_LORA, D_RWKV), F32)
    wwa = jnp.concatenate([jnp.concatenate([w_decay_up[0], zl], axis=1),
                           jnp.concatenate([zl, w_iclr_up[0]], axis=1)], axis=0).astype(BF16)
    w0a0 = jnp.concatenate([w0[0], a0[0]]).reshape(1, -1)
    eones = jnp.tile(_block_diag_const(2 * LANES, HEAD, 1.0), (2, 1))
    emean = jnp.tile(_block_diag_const(LANES, HEAD, 1.0 / HEAD), (2, 1))

    r, lw, k, v, a, b, g, bonus, yb = _prep(
        x, row(ln_emb_g), row(ln_emb_b), w_in[0].astype(BF16), row(mu_shift[0]), wwa, w0a0,
        w_gate_up[0].astype(BF16), row(k_k[0]), row(k_a[0]), row(r_k[0]), eones,
        row(gmlp_ln_g[0]), row(gmlp_ln_b[0]), w_spatial[0], b_spatial[0].T)

    ya = _wkv(r, lw, k, v, a, b, g, bonus, gn_g[0].reshape(N_PAIRS, 1, LANES), gn_b[0].reshape(N_PAIRS, 1, LANES),
              emean)

    wr = jnp.concatenate([w_expert_router[0].reshape(D, N_EXPERTS), w_group_router[0],
                          jnp.zeros((D, LANES - N_EXPERTS - N_GROUPS), F32)], axis=1)
    wr3 = jnp.concatenate(_split2(wr), axis=1)
    br = jnp.concatenate([b_expert_router[0].reshape(-1), b_group_router[0],
                          jnp.zeros((LANES - N_EXPERTS - N_GROUPS,), F32)]).reshape(1, LANES)
    base, x1, route, counts = _mixer(x, row(ln_emb_g), row(ln_emb_b), ya, yb, w_out[0].astype(BF16), row(ln1_g[0]),
                                     row(ln1_b[0]), wr3, br)
    base = base.reshape(T, D)
    x1 = x1.reshape(T, D // 2)
    route = route.reshape(T, LANES)

    n_blocks = -(-(T * TOP_K) // EXPERT_ROWS) + N_EXPERTS
    dest, pend = _slots(route, counts)
    dests = (dest[0], dest[1])
    pend = pend[0, :N_EXPERTS]

    xs, resid = _dispatch(pend, dests, x1, base, p[0].reshape(T, D_PLE), w_ple_gate[0].astype(BF16),
                          row(b_ple_gate[0]), w_ple_proj[0].astype(BF16), n_blocks * EXPERT_ROWS)
    ys = _experts(pend, xs, w_exp_gate[0], w_exp_up[0], w_exp_down[0])
    out = _combine(dests, ys, resid, route, row(ln2_g[0]), row(ln2_b[0]))
    return out.reshape(B, S, D)
```

```python
import math

import jax
import jax.numpy as jnp
from jax import lax
from jax.experimental import pallas as pl
from jax.experimental.pallas import tpu as pltpu

F32 = jnp.float32
BF16 = jnp.bfloat16

D_MODEL = 1024
D_RWKV = 512
HEAD = 64
D_GMLP = 512
GMLP_GROUPS = 4
GROUP_W = 128
GCHUNK = 128
DECAY_LORA = 64
ICLR_LORA = 64
GATE_LORA = 128
N_SHIFT = 3 * D_RWKV + DECAY_LORA + ICLR_LORA + GATE_LORA
D_IN = N_SHIFT + 2 * D_GMLP
D_PLE = 256
N_GROUPS = 4
EXPERTS_PER_GROUP = 8
N_EXPERTS = 32
TOP_K = 2
D_EXPERT = 512
DEPTH = 1
ALPHA = (2.0 * DEPTH) ** 0.25
LN_EPS = 1e-5
GN_EPS = 64e-5
DECAY_SCALE = math.exp(-0.5)

LANES = 128
WKV_CHUNK = 64
N_PAIRS = D_RWKV // LANES
VMEM_LIMIT = 56 * 1024 * 1024

PREP_TM = 512
WKV_TB = 512
WKV_PAIRS = 4
MIX_TM = 512
SLOT_TM = 2048
EXPERT_ROWS = 256
EXPERT_DEPTH = 4
DISPATCH_TM = 512
COMBINE_TM = 256
NEG = -1e30


def _dot(a, b):
    return jnp.dot(a.astype(BF16), b.astype(BF16), preferred_element_type=F32)


def _dot_nt(a, b):
    return lax.dot_general(a.astype(BF16), b.astype(BF16), (((1,), (1,)), ((), ())),
                           preferred_element_type=F32)


def _split3(x):
    hi = x.astype(BF16)
    r1 = x - hi.astype(F32)
    mid = r1.astype(BF16)
    lo = (r1 - mid.astype(F32)).astype(BF16)
    return hi, mid, lo


def _dot3_lhs(x, w):
    hi, mid, lo = _split3(x)
    w = w.astype(BF16)
    return (jnp.dot(hi, w, preferred_element_type=F32) + jnp.dot(mid, w, preferred_element_type=F32)
            + jnp.dot(lo, w, preferred_element_type=F32))


def _split2(x):
    hi = x.astype(BF16)
    return hi, (x - hi.astype(F32)).astype(BF16)


def _dot2_lhs(x, w2):
    hi, lo = _split2(x)
    return jnp.dot(jnp.concatenate([hi, lo], axis=1), w2, preferred_element_type=F32)


def _dot3_rhs(w3, x):
    hi, mid, lo = _split3(x)
    return jnp.dot(w3, jnp.concatenate([hi, mid, lo], axis=0), preferred_element_type=F32)


def _layer_norm(x, g, b, eps):
    mu = jnp.mean(x, axis=-1, keepdims=True)
    xc = x - mu
    var = jnp.mean(xc * xc, axis=-1, keepdims=True)
    return xc * lax.rsqrt(var + eps) * g + b


def _sigmoid(x):
    return 1.0 / (1.0 + jnp.exp(-x))


def _iota(shape, dim):
    return lax.broadcasted_iota(jnp.int32, shape, dim)


def _prep_kernel(x_ref, lng_ref, lnb_ref, win_ref, mu_ref, wwa_ref, w0a0_ref, wg_ref, kk_ref, ka_ref, rk_ref,
                 eones_ref, glng_ref, glnb_ref, wsp_ref, bsp_ref,
                 r_ref, lw_ref, k_ref, v_ref, a_ref, b_ref, g_ref, bonus_ref, yb_ref, carry_ref, win_bf_ref):
    tm = x_ref.shape[1]

    @pl.when((pl.program_id(0) == 0) & (pl.program_id(1) == 0))
    def _():
        win_bf_ref[...] = win_ref[...].astype(BF16)

    @pl.when(pl.program_id(1) == 0)
    def _():
        carry_ref[...] = jnp.zeros_like(carry_ref)

    x0 = _layer_norm(x_ref[0], lng_ref[...], lnb_ref[...], LN_EPS)
    proj = jnp.dot(x0.astype(BF16), win_bf_ref[...], preferred_element_type=F32)

    h = proj[:, :N_SHIFT]
    rolled = pltpu.roll(h, 1, 0)
    first = _iota((tm, N_SHIFT), 0) == 0
    prev = jnp.where(first, jnp.broadcast_to(carry_ref[0:1, :], (tm, N_SHIFT)), rolled)
    carry_ref[0:1, :] = h[tm - 1:tm, :]
    h = h + (prev - h) * mu_ref[...]

    r = h[:, 0:D_RWKV]
    k = h[:, D_RWKV:2 * D_RWKV]
    v = h[:, 2 * D_RWKV:3 * D_RWKV]
    xwa = h[:, 3 * D_RWKV:3 * D_RWKV + LANES]
    xg = h[:, 3 * D_RWKV + LANES:N_SHIFT]

    lane = _iota((tm, LANES), 1)
    twa = jnp.where(lane < DECAY_LORA, jnp.tanh(xwa), xwa)
    da = _dot(twa, wwa_ref[...]) + w0a0_ref[...]
    logw = -DECAY_SCALE * _sigmoid(da[:, :D_RWKV])
    ag = _sigmoid(da[:, D_RWKV:])
    g = _dot(_sigmoid(xg), wg_ref[...])

    eones2 = eones_ref[...]

    def head_sum(t):
        half = 2 * LANES
        return jnp.concatenate([_dot2_lhs(t[:, :half], eones2), _dot2_lhs(t[:, half:], eones2)], axis=1)

    kk = k * kk_ref[...]
    kk = kk * lax.rsqrt(jnp.maximum(head_sum(kk * kk), 1e-24))
    k = k * (1.0 + (ag - 1.0) * ka_ref[...])
    bonus = head_sum(r * k * rk_ref[...]) * v

    for p in range(N_PAIRS):
        sl = slice(p * LANES, (p + 1) * LANES)
        r_ref[0, p] = r[:, sl]
        lw_ref[0, p] = logw[:, sl]
        k_ref[0, p] = k[:, sl]
        v_ref[0, p] = v[:, sl]
        a_ref[0, p] = -kk[:, sl]
        b_ref[0, p] = (kk * ag)[:, sl]
        g_ref[0, p] = g[:, sl]
        bonus_ref[0, p] = bonus[:, sl]

    zin = proj[:, N_SHIFT:]
    z = 0.5 * zin * (1.0 + lax.erf(zin * (0.5 ** 0.5)))
    zu = z[:, :D_GMLP]
    zv = z[:, D_GMLP:]
    causal = _iota((GCHUNK, GCHUNK), 0) >= _iota((GCHUNK, GCHUNK), 1)
    for gi in range(GMLP_GROUPS):
        gs = slice(gi * GROUP_W, (gi + 1) * GROUP_W)
        zvn = _layer_norm(zv[:, gs], glng_ref[:, gs], glnb_ref[:, gs], LN_EPS)
        ws = jnp.where(causal, wsp_ref[gi], 0.0).astype(BF16)
        bcol = bsp_ref[:, gi:gi + 1]
        chunks = [slice(c * GCHUNK, (c + 1) * GCHUNK) for c in range(tm // GCHUNK)]
        zcat = jnp.concatenate([zvn[ts] for ts in chunks], axis=1).astype(BF16)
        mixed = jnp.dot(ws, zcat, preferred_element_type=F32) + bcol
        for ts in chunks:
            yb_ref[0, ts, gs] = (zu[ts, gs] * mixed[:, ts]).astype(BF16)


def _prep(x, ln_g, ln_b, w_in, mu, wwa, w0a0, wg, k_k, k_a, r_k, eones, glng, glnb, wsp, bsp):
    B, S, _ = x.shape
    tm = PREP_TM
    const = lambda shape: pl.BlockSpec(shape, lambda b, s: (0,) * len(shape))
    pair_spec = pl.BlockSpec((1, N_PAIRS, tm, LANES), lambda b, s: (b, 0, s, 0))
    pair_shape = jax.ShapeDtypeStruct((B, N_PAIRS, S, LANES), F32)
    return pl.pallas_call(
        _prep_kernel,
        grid=(B, S // tm),
        in_specs=[
            pl.BlockSpec((1, tm, D_MODEL), lambda b, s: (b, s, 0)),
            const((1, D_MODEL)), const((1, D_MODEL)),
            pl.BlockSpec((D_MODEL, D_IN), lambda b, s: (0, 0), pipeline_mode=pl.Buffered(1)),
            const((1, N_SHIFT)),
            const((LANES, 2 * D_RWKV)), const((1, 2 * D_RWKV)), const((GATE_LORA, D_RWKV)),
            const((1, D_RWKV)), const((1, D_RWKV)), const((1, D_RWKV)), const((4 * LANES, 2 * LANES)),
            const((1, D_GMLP)), const((1, D_GMLP)), const((GMLP_GROUPS, GCHUNK, GCHUNK)),
            const((GCHUNK, GMLP_GROUPS)),
        ],
        out_specs=[pair_spec] * 8 + [pl.BlockSpec((1, tm, D_GMLP), lambda b, s: (b, s, 0))],
        out_shape=[pair_shape] * 8 + [jax.ShapeDtypeStruct((B, S, D_GMLP), BF16)],
        scratch_shapes=[pltpu.VMEM((8, N_SHIFT), F32), pltpu.VMEM((D_MODEL, D_IN), BF16)],
        compiler_params=pltpu.CompilerParams(dimension_semantics=("arbitrary", "arbitrary"),
                                             vmem_limit_bytes=VMEM_LIMIT),
        name="prep",
    )(x, ln_g, ln_b, w_in, mu, wwa, w0a0, wg, k_k, k_a, r_k, eones, glng, glnb, wsp, bsp)


def _wkv_kernel(r_ref, lw_ref, k_ref, v_ref, a_ref, b_ref, g_ref, bonus_ref, gng_ref, gnb_ref, emean_ref,
                o_ref, h_ref):
    C = WKV_CHUNK
    tb = r_ref.shape[2]

    @pl.when(pl.program_id(2) == 0)
    def _():
        h_ref[...] = jnp.zeros_like(h_ref)

    tok = _iota((C, LANES), 0)
    lane = _iota((C, LANES), 1)
    head0 = lane < HEAD
    strict = tok > lane % HEAD
    incl = tok >= lane % HEAD
    eye_w = (tok == lane % HEAD).astype(F32)
    rr = _iota((LANES, LANES), 0)
    cc = _iota((LANES, LANES), 1)
    eye = (rr == cc).astype(F32)
    same_head = (rr < HEAD) == (cc < HEAD)
    ltri3 = (_iota((C, 3 * C), 0) >= _iota((C, 3 * C), 1) % C).astype(BF16)

    def stack(x):
        xb = x.astype(BF16)
        zero = jnp.zeros_like(xb)
        return jnp.concatenate([jnp.where(head0, xb, zero), jnp.where(head0, zero, xb)], axis=0)

    def stack2(x, y):
        return jnp.concatenate([stack(x), stack(y)], axis=1)

    n_pairs = r_ref.shape[1]
    n_chunks = tb // C
    units = [(q, c) for q in range(n_pairs) for c in range(n_chunks)]

    def load(ref):
        return [ref[0, q, c * C:(c + 1) * C, :] for q, c in units]

    r_, lw_, k_, v_, a_, b_ = (load(ref) for ref in (r_ref, lw_ref, k_ref, v_ref, a_ref, b_ref))
    cum_ = [_dot3_rhs(ltri3, lw) for lw in lw_]
    cend_ = [cum[C - 1:C, :] for cum in cum_]
    at_ = [a * jnp.exp(cum - lw) for a, cum, lw in zip(a_, cum_, lw_)]
    rt_ = [r * jnp.exp(cum) for r, cum in zip(r_, cum_)]
    ginv_ = [jnp.exp(-cum) for cum in cum_]
    gend_ = [jnp.exp(cend - cum) for cend, cum in zip(cend_, cum_)]
    bk_end_ = [jnp.concatenate([b * ge, k * ge], axis=0) for b, k, ge in zip(b_, k_, gend_)]
    vst_ = [stack(v) for v in v_]

    G_ = [_dot_nt(jnp.concatenate([at, rt], axis=0), jnp.concatenate([stack(b * gi), stack(k * gi)], axis=0))
          for at, rt, b, k, gi in zip(at_, rt_, b_, k_, ginv_)]
    n1_ = [jnp.where(strict, G[:C, :LANES], 0.0) for G in G_]
    aak_ = [jnp.where(strict, G[:C, LANES:], 0.0) for G in G_]
    arb_ = [jnp.where(incl, G[C:, :LANES], 0.0) for G in G_]
    ark_ = [jnp.where(incl, G[C:, LANES:], 0.0) for G in G_]
    av_ = [_dot(jnp.concatenate([aak, ark], axis=0), vst) for aak, ark, vst in zip(aak_, ark_, vst_)]

    s1_ = [stack(n1) for n1 in n1_]
    n2_ = [_dot(n1, s1) for n1, s1 in zip(n1_, s1_)]
    x_ = [_dot(n2, jnp.concatenate([s1, stack(n2)], axis=1)) for n2, s1 in zip(n2_, s1_)]
    t_ = [eye_w + n1 + n2 + x[:, :LANES] for n1, n2, x in zip(n1_, n2_, x_)]
    np_ = [x[:, LANES:] for x in x_]
    for _ in range(3):
        x_ = [_dot(npow, stack2(t, npow)) for t, npow in zip(t_, np_)]
        t_ = [t + x[:, :LANES] for t, x in zip(t_, x_)]
        np_ = [x[:, LANES:] for x in x_]
    t_ = [t + _dot(npow, stack(t)) for t, npow in zip(t_, np_)]

    x_ = [_dot(t, stack2(at, av[:C])) for t, at, av in zip(t_, at_, av_)]
    z_ = [_dot(arb, stack2(x[:, :LANES], x[:, LANES:])) for arb, x in zip(arb_, x_)]
    rp_ = [rt + z[:, :LANES] for rt, z in zip(rt_, z_)]
    p3_ = [z[:, LANES:] + av[C:] for z, av in zip(z_, av_)]
    rhs_ = [jnp.concatenate([x, jnp.concatenate([jnp.zeros_like(v), v], axis=1)], axis=0)
            for x, v in zip(x_, v_)]
    mq_ = [_dot(bk_end.T, rhs) for bk_end, rhs in zip(bk_end_, rhs_)]
    m_ = [eye * jnp.exp(cend) + jnp.where(same_head, mq[:, :LANES], 0.0) for cend, mq in zip(cend_, mq_)]
    q_ = [jnp.where(same_head, mq[:, LANES:], 0.0) for mq in mq_]

    pairs = range(n_pairs)
    H_ = [h_ref[q] for q in pairs]
    ys_ = [[] for _ in pairs]
    rm_ = [jnp.concatenate([rp, m], axis=0) for rp, m in zip(rp_, m_)]
    for c in range(n_chunks):
        both_ = [_dot(rm_[q * n_chunks + c], H_[q]) for q in pairs]
        for q in pairs:
            ys_[q].append(both_[q][:C] + p3_[q * n_chunks + c])
        H_ = [both_[q][C:] + q_[q * n_chunks + c] for q in pairs]
    for q in pairs:
        h_ref[q] = H_[q]

    emean = emean_ref[...]
    y_ = [jnp.concatenate(ys, axis=0) for ys in ys_]
    mu_ = [_dot2_lhs(y, emean) for y in y_]
    yc_ = [y - mu for y, mu in zip(y_, mu_)]
    var_ = [_dot2_lhs(yc * yc, emean) for yc in yc_]
    for q in pairs:
        yn = yc_[q] * lax.rsqrt(var_[q] + GN_EPS) * gng_ref[q] + gnb_ref[q]
        o_ref[0, q] = ((yn + bonus_ref[0, q]) * g_ref[0, q]).astype(BF16)


def _wkv(r, lw, k, v, a, b, g, bonus, gn_g, gn_b, emean):
    B, P, S, _ = r.shape
    tb = WKV_TB
    pp = WKV_PAIRS
    seq = pl.BlockSpec((1, pp, tb, LANES), lambda bi, p, s: (bi, p, s, 0))
    par = pl.BlockSpec((pp, 1, LANES), lambda bi, p, s: (p, 0, 0))
    return pl.pallas_call(
        _wkv_kernel,
        grid=(B, P // pp, S // tb),
        in_specs=[seq] * 8 + [par, par, pl.BlockSpec((2 * LANES, LANES), lambda bi, p, s: (0, 0))],
        out_specs=seq,
        out_shape=jax.ShapeDtypeStruct((B, P, S, LANES), BF16),
        scratch_shapes=[pltpu.VMEM((pp, LANES, LANES), F32)],
        compiler_params=pltpu.CompilerParams(dimension_semantics=("arbitrary", "arbitrary", "arbitrary"),
                                             vmem_limit_bytes=VMEM_LIMIT),
        name="wkv",
    )(r, lw, k, v, a, b, g, bonus, gn_g, gn_b, emean)


def _mixer_kernel(x_ref, lng_ref, lnb_ref, ya_ref, yb_ref, wout_ref, l1g_ref, l1b_ref, wr_ref, br_ref, below_ref,
                  base_ref, x1_ref, route_ref, counts_ref, carry_ref, wout_bf_ref):
    tm = x_ref.shape[1]

    @pl.when((pl.program_id(0) == 0) & (pl.program_id(1) == 0))
    def _():
        carry_ref[...] = jnp.zeros_like(carry_ref)
        wout_bf_ref[...] = wout_ref[...].astype(BF16)

    x0 = _layer_norm(x_ref[0], lng_ref[...], lnb_ref[...], LN_EPS)
    ymix = jnp.concatenate([ya_ref[0, p] for p in range(N_PAIRS)] + [yb_ref[0]], axis=-1)
    mix = jnp.dot(ymix, wout_bf_ref[...], preferred_element_type=F32)
    x1 = _layer_norm(ALPHA * x0 + mix, l1g_ref[...], l1b_ref[...], LN_EPS)
    x1b = x1.astype(BF16)
    half = D_MODEL // 2
    lo_bits = lax.bitcast_convert_type(x1b[:, :half].astype(F32), jnp.uint32)
    hi_bits = lax.bitcast_convert_type(x1b[:, half:].astype(F32), jnp.uint32)
    x1_ref[0] = (hi_bits & jnp.uint32(0xFFFF0000)) | (lo_bits >> 16)

    hi, mid = _split2(x1)
    wide = jnp.dot(hi, wr_ref[...], preferred_element_type=F32)
    logits = (wide[:, :LANES] + wide[:, LANES:]
              + jnp.dot(mid, wr_ref[:, :LANES], preferred_element_type=F32)) + br_ref[...]
    lane = _iota((tm, LANES), 1).astype(F32)
    far = float(4 * LANES)
    is_g = jnp.where(lane >= N_EXPERTS, jnp.where(lane < N_EXPERTS + N_GROUPS, 1.0, 0.0), 0.0) > 0.5
    gl = jnp.where(is_g, logits, NEG)
    gmax = jnp.max(gl, axis=-1, keepdims=True)
    gsel = jnp.min(jnp.where(gl == gmax, lane, far), axis=-1, keepdims=True) - N_EXPERTS
    p_group = 1.0 / jnp.sum(jnp.where(is_g, jnp.exp(gl - gmax), 0.0), axis=-1, keepdims=True)
    grp_of_lane = jnp.floor(lane * (1.0 / EXPERTS_PER_GROUP))
    el = jnp.where(grp_of_lane == gsel, logits, NEG)
    v1 = jnp.max(el, axis=-1, keepdims=True)
    i1 = jnp.min(jnp.where(el == v1, lane, far), axis=-1, keepdims=True)
    el2 = jnp.where(lane == i1, NEG, el)
    v2 = jnp.max(el2, axis=-1, keepdims=True)
    i2 = jnp.min(jnp.where(el2 == v2, lane, far), axis=-1, keepdims=True)
    e21 = jnp.exp(v2 - v1)
    w1 = p_group / (1.0 + e21)
    w2 = p_group * e21 / (1.0 + e21)

    oh1 = lane == i1
    oh2 = lane == i2
    below = below_ref[...]
    o1 = jnp.where(oh1, 1.0, 0.0)
    o2 = jnp.where(oh2, 1.0, 0.0)
    c12 = jnp.dot(below, jnp.concatenate([o1, o2], axis=1).astype(BF16), preferred_element_type=F32)
    c1 = c12[:, :LANES]
    c2 = c12[:, LANES:]
    tot1 = jnp.sum(o1, axis=0, keepdims=True)
    carry = carry_ref[0:1, :]
    rank1 = jnp.sum(jnp.where(oh1, c1 + carry, 0.0), axis=-1, keepdims=True)
    rank2 = jnp.sum(jnp.where(oh2, c2 + carry + tot1, 0.0), axis=-1, keepdims=True)
    carry = carry + tot1 + jnp.sum(o2, axis=0, keepdims=True)
    carry_ref[0:1, :] = carry
    counts_ref[...] = jnp.broadcast_to(carry, counts_ref.shape)

    fields = (i1, i2, w1, w2, rank1, rank2)
    route = jnp.zeros((tm, LANES), F32)
    for n, f in enumerate(fields):
        route = jnp.where(lane == n, f, route)
    route_ref[0] = route

    base_ref[0] = ALPHA * x1


def _mixer(x, ln_g, ln_b, ya, yb, w_out, l1g, l1b, wr3, br):
    B, S, _ = x.shape
    tm = MIX_TM
    const = lambda shape: pl.BlockSpec(shape, lambda b, s: (0,) * len(shape))
    row = lambda w: pl.BlockSpec((1, tm, w), lambda b, s: (b, s, 0))
    below = (jnp.arange(tm)[:, None] > jnp.arange(tm)[None, :]).astype(BF16)
    return pl.pallas_call(
        _mixer_kernel,
        grid=(B, S // tm),
        in_specs=[
            row(D_MODEL), const((1, D_MODEL)), const((1, D_MODEL)),
            pl.BlockSpec((1, N_PAIRS, tm, LANES), lambda b, s: (b, 0, s, 0)), row(D_GMLP),
            pl.BlockSpec((D_MODEL, D_MODEL), lambda b, s: (0, 0), pipeline_mode=pl.Buffered(1)),
            const((1, D_MODEL)), const((1, D_MODEL)),
            const((D_MODEL, 2 * LANES)), const((1, LANES)), const((tm, tm)),
        ],
        out_specs=[row(D_MODEL), row(D_MODEL // 2), row(LANES), const((8, LANES))],
        out_shape=[jax.ShapeDtypeStruct((B, S, D_MODEL), F32), jax.ShapeDtypeStruct((B, S, D_MODEL // 2), jnp.uint32),
                   jax.ShapeDtypeStruct((B, S, LANES), F32), jax.ShapeDtypeStruct((8, LANES), F32)],
        scratch_shapes=[pltpu.VMEM((8, LANES), F32), pltpu.VMEM((D_MODEL, D_MODEL), BF16)],
        compiler_params=pltpu.CompilerParams(dimension_semantics=("arbitrary", "arbitrary"),
                                             vmem_limit_bytes=VMEM_LIMIT),
        name="mixer",
    )(x, ln_g, ln_b, ya, yb, w_out, l1g, l1b, wr3, br, below)


def _slots_kernel(route_ref, counts_ref, dest_ref, pend_ref):
    tm = route_ref.shape[0]
    lane = _iota((tm, LANES), 1)
    route = route_ref[...]
    oh1 = lane == route[:, 0:1].astype(jnp.int32)
    oh2 = lane == route[:, 1:2].astype(jnp.int32)

    counts = counts_ref[0:1, :]
    padded = jnp.floor((counts + (EXPERT_ROWS - 1)) * (1.0 / EXPERT_ROWS)) * EXPERT_ROWS
    upper = (_iota((LANES, LANES), 0) <= _iota((LANES, LANES), 1)).astype(BF16)
    pend = _dot3_lhs(jnp.broadcast_to(padded, (8, LANES)), upper)[0:1, :]
    pstart = pend - padded
    d1 = jnp.sum(jnp.where(oh1, pstart, 0.0), axis=-1, keepdims=True) + route[:, 4:5]
    d2 = jnp.sum(jnp.where(oh2, pstart, 0.0), axis=-1, keepdims=True) + route[:, 5:6]
    dest = jnp.where(lane == 0, d1, jnp.where(lane == 1, d2, 0.0))
    dest_ref[...] = jnp.transpose(dest)[0:dest_ref.shape[0], :].astype(jnp.int32)
    pend_ref[...] = jnp.broadcast_to(pend, (8, LANES)).astype(jnp.int32)


def _slots(route, counts):
    T = route.shape[0]
    tm = SLOT_TM
    return pl.pallas_call(
        _slots_kernel,
        grid=(T // tm,),
        in_specs=[pl.BlockSpec((tm, LANES), lambda i: (i, 0)), pl.BlockSpec((8, LANES), lambda i: (0, 0))],
        out_specs=[pl.BlockSpec((8, tm), lambda i: (0, i)),
                   pl.BlockSpec((8, LANES), lambda i: (0, 0))],
        out_shape=[jax.ShapeDtypeStruct((8, T), jnp.int32), jax.ShapeDtypeStruct((8, LANES), jnp.int32)],
        compiler_params=pltpu.CompilerParams(dimension_semantics=("arbitrary",), vmem_limit_bytes=VMEM_LIMIT),
        name="slots",
    )(route, counts)


def _dispatch_kernel(pend_ref, dest0_ref, dest1_ref, x_ref, base_ref, p_ref, wpg_ref, bpg_ref, wpp_ref, xs_ref, resid_ref,
                     zero_ref, wpg_bf_ref, wpp_bf_ref, sem, zsem):
    tm = x_ref.shape[0]
    dest_refs = (dest0_ref, dest1_ref)

    @pl.when(pl.program_id(0) == 0)
    def _():
        wpg_bf_ref[...] = wpg_ref[...].astype(BF16)
        wpp_bf_ref[...] = wpp_ref[...].astype(BF16)
        zero_ref[...] = jnp.zeros_like(zero_ref)

        def tail(e):
            start = pl.multiple_of(jnp.maximum(pend_ref[e] - EXPERT_ROWS, 0), EXPERT_ROWS)
            return pltpu.make_async_copy(zero_ref, xs_ref.at[pl.ds(start, EXPERT_ROWS)], zsem)

        def unused(j):
            return pltpu.make_async_copy(
                zero_ref, xs_ref.at[pl.ds(pl.multiple_of(j * EXPERT_ROWS, EXPERT_ROWS), EXPERT_ROWS)], zsem)

        def start_unused(j, _):
            unused(j).start()
            return 0

        def wait_unused(j, _):
            unused(j).wait()
            return 0

        first_unused = pend_ref[N_EXPERTS - 1] // EXPERT_ROWS
        n_blocks = xs_ref.shape[0] // EXPERT_ROWS
        for e in range(N_EXPERTS):
            tail(e).start()
        lax.fori_loop(first_unused, n_blocks, start_unused, 0)
        for e in range(N_EXPERTS):
            tail(e).wait()
        lax.fori_loop(first_unused, n_blocks, wait_unused, 0)

    for t in range(tm):
        for j in range(TOP_K):
            pltpu.make_async_copy(x_ref.at[pl.ds(t, 1)], xs_ref.at[pl.ds(dest_refs[j][t], 1)],
                                  sem).start(priority=j)

    xw = x_ref[...]
    x_lo = lax.bitcast_convert_type(xw << 16, F32)
    x_hi = lax.bitcast_convert_type(xw & jnp.uint32(0xFFFF0000), F32)
    x1b = jnp.concatenate([x_lo, x_hi], axis=1).astype(BF16)
    gate = _sigmoid(jnp.dot(x1b, wpg_bf_ref[...], preferred_element_type=F32) + bpg_ref[...])
    ple = gate * jnp.dot(p_ref[...].astype(BF16), wpp_bf_ref[...], preferred_element_type=F32)
    resid_ref[...] = base_ref[...] + ple

    for j in range(TOP_K):
        pltpu.make_async_copy(x_ref, xs_ref.at[pl.ds(0, tm)], sem).wait()


def _dispatch(pend, dests, x1, base, p, wpg, bpg, wpp, n_rows):
    T, width = x1.shape
    tm = DISPATCH_TM
    const = lambda shape: pl.BlockSpec(shape, lambda i, pe: (0,) * len(shape))
    resident = lambda shape: pl.BlockSpec(shape, lambda i, pe: (0,) * len(shape), pipeline_mode=pl.Buffered(1))
    tile = lambda w: pl.BlockSpec((tm, w), lambda i, pe: (i, 0))
    index_list = pl.BlockSpec((tm,), lambda i, pe: (i,), memory_space=pltpu.SMEM)
    return pl.pallas_call(
        _dispatch_kernel,
        grid_spec=pltpu.PrefetchScalarGridSpec(
            num_scalar_prefetch=1,
            grid=(T // tm,),
            in_specs=[index_list, index_list,
                      tile(width), tile(D_MODEL), tile(D_PLE),
                      resident((D_MODEL, D_MODEL)), const((1, D_MODEL)), resident((D_PLE, D_MODEL))],
            out_specs=[pl.BlockSpec(memory_space=pl.ANY), tile(D_MODEL)],
            scratch_shapes=[pltpu.VMEM((EXPERT_ROWS, width), x1.dtype), pltpu.VMEM((D_MODEL, D_MODEL), BF16),
                            pltpu.VMEM((D_PLE, D_MODEL), BF16), pltpu.SemaphoreType.DMA,
                            pltpu.SemaphoreType.DMA],
        ),
        out_shape=[jax.ShapeDtypeStruct((n_rows, width), x1.dtype), jax.ShapeDtypeStruct((T, D_MODEL), F32)],
        compiler_params=pltpu.CompilerParams(dimension_semantics=("arbitrary",), vmem_limit_bytes=VMEM_LIMIT),
        name="dispatch",
    )(pend, dests[0], dests[1], x1, base, p, wpg, bpg, wpp)


def _experts_kernel(pend_ref, xs_ref, wg_ref, wu_ref, wd_ref, ys_ref, xbuf_ref, ybuf_ref, wgu_ref, wdb_ref,
                    in_sem, out_sem):
    rows = EXPERT_ROWS
    e = pl.program_id(0)
    first = jnp.where(e == 0, 0, pend_ref[jnp.maximum(e - 1, 0)]) // rows
    last = pend_ref[e] // rows
    n_used = pend_ref[N_EXPERTS - 1] // rows

    def block_rows(ref, b):
        return ref.at[pl.ds(pl.multiple_of(b * rows, rows), rows)]

    depth = xbuf_ref.shape[0]
    row_priority = 1

    def x_copy(b):
        slot = b % depth
        return pltpu.make_async_copy(block_rows(xs_ref, b), xbuf_ref.at[slot], in_sem.at[slot])

    def y_copy(b):
        slot = b % depth
        return pltpu.make_async_copy(ybuf_ref.at[slot], block_rows(ys_ref, b), out_sem.at[slot])

    @pl.when(e == 0)
    def _():
        for ahead in range(depth - 1):
            @pl.when(ahead < n_used)
            def _():
                x_copy(ahead).start(priority=row_priority)

    @pl.when(last > first)
    def _():
        wgu_ref[:, :D_EXPERT] = wg_ref[0].astype(BF16)
        wgu_ref[:, D_EXPERT:] = wu_ref[0].astype(BF16)
        wdb_ref[...] = wd_ref[0].astype(BF16)

        def body(b, _):
            slot = b % depth

            @pl.when(b + depth - 1 < n_used)
            def _():
                x_copy(b + depth - 1).start(priority=row_priority)

            x_copy(b).wait()

            @pl.when(b >= depth)
            def _():
                y_copy(b - depth).wait()

            xw = xbuf_ref[slot]
            x_lo = lax.bitcast_convert_type(xw << 16, F32)
            x_hi = lax.bitcast_convert_type(xw & jnp.uint32(0xFFFF0000), F32)
            xb = jnp.concatenate([x_lo, x_hi], axis=1).astype(BF16)
            h = jnp.dot(xb, wgu_ref[...], preferred_element_type=F32)
            hg = h[:, :D_EXPERT]
            hid = hg * _sigmoid(hg) * h[:, D_EXPERT:]
            ybuf_ref[slot] = jnp.dot(hid.astype(BF16), wdb_ref[...], preferred_element_type=F32)
            y_copy(b).start(priority=row_priority)
            return 0

        lax.fori_loop(first, last, body, 0)

    @pl.when(e == N_EXPERTS - 1)
    def _():
        for back in range(depth, 0, -1):
            @pl.when(n_used >= back)
            def _():
                y_copy(n_used - back).wait()

        ybuf_ref[0] = jnp.zeros(ybuf_ref.shape[1:], F32)

        def unused(b):
            return pltpu.make_async_copy(ybuf_ref.at[0], block_rows(ys_ref, b), out_sem.at[0])

        def start_unused(b, _):
            unused(b).start()
            return 0

        def wait_unused(b, _):
            unused(b).wait()
            return 0

        n_blocks = ys_ref.shape[0] // rows
        lax.fori_loop(n_used, n_blocks, start_unused, 0)
        lax.fori_loop(n_used, n_blocks, wait_unused, 0)


def _experts(pend, xs, wg, wu, wd):
    n_rows = xs.shape[0]
    rows = EXPERT_ROWS
    wspec = lambda shape: pl.BlockSpec((1,) + shape, lambda e, pe: (e, 0, 0))
    return pl.pallas_call(
        _experts_kernel,
        grid_spec=pltpu.PrefetchScalarGridSpec(
            num_scalar_prefetch=1,
            grid=(N_EXPERTS,),
            in_specs=[pl.BlockSpec(memory_space=pl.ANY),
                      wspec((D_MODEL, D_EXPERT)), wspec((D_MODEL, D_EXPERT)), wspec((D_EXPERT, D_MODEL))],
            out_specs=pl.BlockSpec(memory_space=pl.ANY),
            scratch_shapes=[pltpu.VMEM((EXPERT_DEPTH, rows, D_MODEL // 2), jnp.uint32),
                            pltpu.VMEM((EXPERT_DEPTH, rows, D_MODEL), F32),
                            pltpu.VMEM((D_MODEL, 2 * D_EXPERT), BF16), pltpu.VMEM((D_EXPERT, D_MODEL), BF16),
                            pltpu.SemaphoreType.DMA((EXPERT_DEPTH,)), pltpu.SemaphoreType.DMA((EXPERT_DEPTH,))],
        ),
        out_shape=jax.ShapeDtypeStruct((n_rows, D_MODEL), F32),
        compiler_params=pltpu.CompilerParams(dimension_semantics=("arbitrary",), vmem_limit_bytes=VMEM_LIMIT),
        name="experts",
    )(pend, xs, wg, wu, wd)


def _combine_kernel(dest0_ref, dest1_ref, dest0_next_ref, dest1_next_ref, ys_ref, resid_ref, route_ref, lg_ref, lb_ref,
                    o_ref, buf_ref, sem):
    tm = buf_ref.shape[2]
    i = pl.program_id(0)
    dest_refs = (dest0_ref, dest1_ref)
    dest_next_refs = (dest0_next_ref, dest1_next_ref)

    def gather(drefs, offset, s):
        for t in range(tm):
            for j in range(TOP_K):
                pltpu.make_async_copy(ys_ref.at[pl.ds(drefs[j][offset + t], 1)],
                                      buf_ref.at[s, j, pl.ds(t, 1)], sem.at[s]).start(priority=j)

    def drain(s):
        for j in range(TOP_K):
            pltpu.make_async_copy(ys_ref.at[pl.ds(0, tm)], buf_ref.at[s, j], sem.at[s]).wait()

    def finish(s):
        rows = slice(s * tm, (s + 1) * tm)
        drain(s)
        route = route_ref[rows, :]
        ffn = buf_ref[s, 0] * route[:, 2:3] + buf_ref[s, 1] * route[:, 3:4]
        o_ref[rows, :] = _layer_norm(resid_ref[rows, :] + ffn, lg_ref[...], lb_ref[...], LN_EPS)

    @pl.when(i == 0)
    def _():
        gather(dest_refs, 0, 0)

    gather(dest_refs, tm, 1)
    finish(0)
    gather(dest_next_refs, 0, 0)
    finish(1)

    @pl.when(i == pl.num_programs(0) - 1)
    def _():
        drain(0)


def _combine(dests, ys, resid, route, l2g, l2b):
    T = resid.shape[0]
    tm = COMBINE_TM
    nt = T // tm
    tile = lambda w: pl.BlockSpec((2 * tm, w), lambda i: (i, 0))
    const = lambda shape: pl.BlockSpec(shape, lambda i: (0,) * len(shape))
    pair_list = pl.BlockSpec((2 * tm,), lambda i: (i,), memory_space=pltpu.SMEM)
    next_list = pl.BlockSpec((tm,), lambda i: (jnp.minimum(2 * i + 2, nt - 1),), memory_space=pltpu.SMEM)
    return pl.pallas_call(
        _combine_kernel,
        grid=(nt // 2,),
        in_specs=[pair_list, pair_list, next_list, next_list,
                  pl.BlockSpec(memory_space=pl.ANY),
                  tile(D_MODEL), tile(LANES), const((1, D_MODEL)), const((1, D_MODEL))],
        out_specs=pl.BlockSpec((2 * tm, D_MODEL), lambda i: (i, 0)),
        out_shape=jax.ShapeDtypeStruct((T, D_MODEL), F32),
        scratch_shapes=[pltpu.VMEM((2, TOP_K, tm, D_MODEL), F32), pltpu.SemaphoreType.DMA((2,))],
        compiler_params=pltpu.CompilerParams(dimension_semantics=("arbitrary",), vmem_limit_bytes=VMEM_LIMIT),
        name="combine",
    )(dests[0], dests[1], dests[0], dests[1], ys, resid, route, l2g, l2b)


def _block_diag_const(n, blk, val):
    idx = jnp.arange(n) // blk
    return jnp.where(idx[:, None] == idx[None, :], val, 0.0).astype(BF16)


def kernel(x, p, ln_emb_g, ln_emb_b, w_in, mu_shift, w0, w_decay_up, a0, w_iclr_up, w_gate_up, k_k, k_a, r_k, gn_g, gn_b, gmlp_ln_g, gmlp_ln_b, w_spatial, b_spatial, w_out, ln1_g, ln1_b, w_group_router, b_group_router, w_expert_router, b_expert_router, w_exp_gate, w_exp_up, w_exp_down, w_ple_gate, b_ple_gate, w_ple_proj, ln2_g, ln2_b):
    B, S, D = x.shape
    T = B * S
    row = lambda t: t.reshape(1, -1).astype(F32)

    zl = jnp.zeros((DECAY_LORA, D_RWKV), F32)
    wwa = jnp.concatenate([jnp.concatenate([w_decay_up[0], zl], axis=1),
                           jnp.concatenate([zl, w_iclr_up[0]], axis=1)], axis=0).astype(BF16)
    w0a0 = jnp.concatenate([w0[0], a0[0]]).reshape(1, -1)
    eones = jnp.tile(_block_diag_const(2 * LANES, HEAD, 1.0), (2, 1))
    emean = jnp.tile(_block_diag_const(LANES, HEAD, 1.0 / HEAD), (2, 1))

    r, lw, k, v, a, b, g, bonus, yb = _prep(
        x, row(ln_emb_g), row(ln_emb_b), w_in[0], row(mu_shift[0]), wwa, w0a0,
        w_gate_up[0].astype(BF16), row(k_k[0]), row(k_a[0]), row(r_k[0]), eones,
        row(gmlp_ln_g[0]), row(gmlp_ln_b[0]), w_spatial[0], b_spatial[0].T)

    ya = _wkv(r, lw, k, v, a, b, g, bonus, gn_g[0].reshape(N_PAIRS, 1, LANES), gn_b[0].reshape(N_PAIRS, 1, LANES),
              emean)

    wr = jnp.concatenate([w_expert_router[0].reshape(D, N_EXPERTS), w_group_router[0],
                          jnp.zeros((D, LANES - N_EXPERTS - N_GROUPS), F32)], axis=1)
    wr3 = jnp.concatenate(_split2(wr), axis=1)
    br = jnp.concatenate([b_expert_router[0].reshape(-1), b_group_router[0],
                          jnp.zeros((LANES - N_EXPERTS - N_GROUPS,), F32)]).reshape(1, LANES)
    base, x1, route, counts = _mixer(x, row(ln_emb_g), row(ln_emb_b), ya, yb, w_out[0], row(ln1_g[0]),
                                     row(ln1_b[0]), wr3, br)
    base = base.reshape(T, D)
    x1 = x1.reshape(T, D // 2)
    route = route.reshape(T, LANES)

    n_blocks = -(-(T * TOP_K) // EXPERT_ROWS) + N_EXPERTS
    dest, pend = _slots(route, counts)
    dests = (dest[0], dest[1])
    pend = pend[0, :N_EXPERTS]

    xs, resid = _dispatch(pend, dests, x1, base, p[0].reshape(T, D_PLE), w_ple_gate[0],
                          row(b_ple_gate[0]), w_ple_proj[0], n_blocks * EXPERT_ROWS)
    ys = _experts(pend, xs, w_exp_gate[0], w_exp_up[0], w_exp_down[0])
    out = _combine(dests, ys, resid, route, row(ln2_g[0]), row(ln2_b[0]))
    return out.reshape(B, S, D)
```

```python
import math

import jax
import jax.numpy as jnp
from jax import lax
from jax.experimental import pallas as pl
from jax.experimental.pallas import tpu as pltpu

F32 = jnp.float32
BF16 = jnp.bfloat16

D_MODEL = 1024
D_RWKV = 512
HEAD = 64
D_GMLP = 512
GMLP_GROUPS = 4
GROUP_W = 128
GCHUNK = 128
DECAY_LORA = 64
ICLR_LORA = 64
GATE_LORA = 128
N_SHIFT = 3 * D_RWKV + DECAY_LORA + ICLR_LORA + GATE_LORA
D_IN = N_SHIFT + 2 * D_GMLP
D_PLE = 256
N_GROUPS = 4
EXPERTS_PER_GROUP = 8
N_EXPERTS = 32
TOP_K = 2
D_EXPERT = 512
DEPTH = 1
ALPHA = (2.0 * DEPTH) ** 0.25
LN_EPS = 1e-5
GN_EPS = 64e-5
DECAY_SCALE = math.exp(-0.5)

LANES = 128
WKV_CHUNK = 64
N_PAIRS = D_RWKV // LANES
VMEM_LIMIT = 56 * 1024 * 1024

PREP_TM = 512
WKV_TB = 512
WKV_PAIRS = 4
MIX_TM = 512
SLOT_TM = 2048
EXPERT_ROWS = 256
EXPERT_DEPTH = 4
DISPATCH_TM = 1024
COMBINE_TM = 512
NEG = -1e30


def _dot(a, b):
    return jnp.dot(a.astype(BF16), b.astype(BF16), preferred_element_type=F32)


def _dot_nt(a, b):
    return lax.dot_general(a.astype(BF16), b.astype(BF16), (((1,), (1,)), ((), ())),
                           preferred_element_type=F32)


def _split3(x):
    hi = x.astype(BF16)
    r1 = x - hi.astype(F32)
    mid = r1.astype(BF16)
    lo = (r1 - mid.astype(F32)).astype(BF16)
    return hi, mid, lo


def _dot3_lhs(x, w):
    hi, mid, lo = _split3(x)
    w = w.astype(BF16)
    return (jnp.dot(hi, w, preferred_element_type=F32) + jnp.dot(mid, w, preferred_element_type=F32)
            + jnp.dot(lo, w, preferred_element_type=F32))


def _split2(x):
    hi = x.astype(BF16)
    return hi, (x - hi.astype(F32)).astype(BF16)


def _dot2_lhs(x, w2):
    hi, lo = _split2(x)
    return jnp.dot(jnp.concatenate([hi, lo], axis=1), w2, preferred_element_type=F32)


def _dot3_rhs(w3, x):
    hi, mid, lo = _split3(x)
    return jnp.dot(w3, jnp.concatenate([hi, mid, lo], axis=0), preferred_element_type=F32)


def _layer_norm(x, g, b, eps):
    mu = jnp.mean(x, axis=-1, keepdims=True)
    xc = x - mu
    var = jnp.mean(xc * xc, axis=-1, keepdims=True)
    return xc * lax.rsqrt(var + eps) * g + b


def _sigmoid(x):
    return 1.0 / (1.0 + jnp.exp(-x))


def _iota(shape, dim):
    return lax.broadcasted_iota(jnp.int32, shape, dim)


def _prep_kernel(x_ref, lng_ref, lnb_ref, win_ref, mu_ref, wwa_ref, w0a0_ref, wg_ref, kk_ref, ka_ref, rk_ref,
                 eones_ref, glng_ref, glnb_ref, wsp_ref, bsp_ref,
                 r_ref, lw_ref, k_ref, v_ref, a_ref, b_ref, g_ref, bonus_ref, yb_ref, carry_ref, win_bf_ref):
    tm = x_ref.shape[1]

    @pl.when((pl.program_id(0) == 0) & (pl.program_id(1) == 0))
    def _():
        win_bf_ref[...] = win_ref[...].astype(BF16)

    @pl.when(pl.program_id(1) == 0)
    def _():
        carry_ref[...] = jnp.zeros_like(carry_ref)

    x0 = _layer_norm(x_ref[0], lng_ref[...], lnb_ref[...], LN_EPS)
    proj = jnp.dot(x0.astype(BF16), win_bf_ref[...], preferred_element_type=F32)

    h = proj[:, :N_SHIFT]
    rolled = pltpu.roll(h, 1, 0)
    first = _iota((tm, N_SHIFT), 0) == 0
    prev = jnp.where(first, jnp.broadcast_to(carry_ref[0:1, :], (tm, N_SHIFT)), rolled)
    carry_ref[0:1, :] = h[tm - 1:tm, :]
    h = h + (prev - h) * mu_ref[...]

    r = h[:, 0:D_RWKV]
    k = h[:, D_RWKV:2 * D_RWKV]
    v = h[:, 2 * D_RWKV:3 * D_RWKV]
    xwa = h[:, 3 * D_RWKV:3 * D_RWKV + LANES]
    xg = h[:, 3 * D_RWKV + LANES:N_SHIFT]

    lane = _iota((tm, LANES), 1)
    twa = jnp.where(lane < DECAY_LORA, jnp.tanh(xwa), xwa)
    da = _dot(twa, wwa_ref[...]) + w0a0_ref[...]
    logw = -DECAY_SCALE * _sigmoid(da[:, :D_RWKV])
    ag = _sigmoid(da[:, D_RWKV:])
    g = _dot(_sigmoid(xg), wg_ref[...])

    eones2 = eones_ref[...]

    def head_sum(t):
        half = 2 * LANES
        return jnp.concatenate([_dot2_lhs(t[:, :half], eones2), _dot2_lhs(t[:, half:], eones2)], axis=1)

    kk = k * kk_ref[...]
    kk = kk * lax.rsqrt(jnp.maximum(head_sum(kk * kk), 1e-24))
    k = k * (1.0 + (ag - 1.0) * ka_ref[...])
    bonus = head_sum(r * k * rk_ref[...]) * v

    for p in range(N_PAIRS):
        sl = slice(p * LANES, (p + 1) * LANES)
        r_ref[0, p] = r[:, sl]
        lw_ref[0, p] = logw[:, sl]
        k_ref[0, p] = k[:, sl]
        v_ref[0, p] = v[:, sl]
        a_ref[0, p] = -kk[:, sl]
        b_ref[0, p] = (kk * ag)[:, sl]
        g_ref[0, p] = g[:, sl]
        bonus_ref[0, p] = bonus[:, sl]

    zin = proj[:, N_SHIFT:]
    z = 0.5 * zin * (1.0 + lax.erf(zin * (0.5 ** 0.5)))
    zu = z[:, :D_GMLP]
    zv = z[:, D_GMLP:]
    causal = _iota((GCHUNK, GCHUNK), 0) >= _iota((GCHUNK, GCHUNK), 1)
    for gi in range(GMLP_GROUPS):
        gs = slice(gi * GROUP_W, (gi + 1) * GROUP_W)
        zvn = _layer_norm(zv[:, gs], glng_ref[:, gs], glnb_ref[:, gs], LN_EPS)
        ws = jnp.where(causal, wsp_ref[gi], 0.0).astype(BF16)
        bcol = bsp_ref[:, gi:gi + 1]
        chunks = [slice(c * GCHUNK, (c + 1) * GCHUNK) for c in range(tm // GCHUNK)]
        zcat = jnp.concatenate([zvn[ts] for ts in chunks], axis=1).astype(BF16)
        mixed = jnp.dot(ws, zcat, preferred_element_type=F32) + bcol
        for ts in chunks:
            yb_ref[0, ts, gs] = (zu[ts, gs] * mixed[:, ts]).astype(BF16)


def _prep(x, ln_g, ln_b, w_in, mu, wwa, w0a0, wg, k_k, k_a, r_k, eones, glng, glnb, wsp, bsp):
    B, S, _ = x.shape
    tm = PREP_TM
    const = lambda shape: pl.BlockSpec(shape, lambda b, s: (0,) * len(shape))
    pair_spec = pl.BlockSpec((1, N_PAIRS, tm, LANES), lambda b, s: (b, 0, s, 0))
    pair_shape = jax.ShapeDtypeStruct((B, N_PAIRS, S, LANES), F32)
    return pl.pallas_call(
        _prep_kernel,
        grid=(B, S // tm),
        in_specs=[
            pl.BlockSpec((1, tm, D_MODEL), lambda b, s: (b, s, 0)),
            const((1, D_MODEL)), const((1, D_MODEL)),
            pl.BlockSpec((D_MODEL, D_IN), lambda b, s: (0, 0), pipeline_mode=pl.Buffered(1)),
            const((1, N_SHIFT)),
            const((LANES, 2 * D_RWKV)), const((1, 2 * D_RWKV)), const((GATE_LORA, D_RWKV)),
            const((1, D_RWKV)), const((1, D_RWKV)), const((1, D_RWKV)), const((4 * LANES, 2 * LANES)),
            const((1, D_GMLP)), const((1, D_GMLP)), const((GMLP_GROUPS, GCHUNK, GCHUNK)),
            const((GCHUNK, GMLP_GROUPS)),
        ],
        out_specs=[pair_spec] * 8 + [pl.BlockSpec((1, tm, D_GMLP), lambda b, s: (b, s, 0))],
        out_shape=[pair_shape] * 8 + [jax.ShapeDtypeStruct((B, S, D_GMLP), BF16)],
        scratch_shapes=[pltpu.VMEM((8, N_SHIFT), F32), pltpu.VMEM((D_MODEL, D_IN), BF16)],
        compiler_params=pltpu.CompilerParams(dimension_semantics=("arbitrary", "arbitrary"),
                                             vmem_limit_bytes=VMEM_LIMIT),
        name="prep",
    )(x, ln_g, ln_b, w_in, mu, wwa, w0a0, wg, k_k, k_a, r_k, eones, glng, glnb, wsp, bsp)


def _wkv_kernel(r_ref, lw_ref, k_ref, v_ref, a_ref, b_ref, g_ref, bonus_ref, gng_ref, gnb_ref, emean_ref,
                o_ref, h_ref):
    C = WKV_CHUNK
    tb = r_ref.shape[2]

    @pl.when(pl.program_id(2) == 0)
    def _():
        h_ref[...] = jnp.zeros_like(h_ref)

    tok = _iota((C, LANES), 0)
    lane = _iota((C, LANES), 1)
    head0 = lane < HEAD
    strict = tok > lane % HEAD
    incl = tok >= lane % HEAD
    eye_w = (tok == lane % HEAD).astype(F32)
    rr = _iota((LANES, LANES), 0)
    cc = _iota((LANES, LANES), 1)
    eye = (rr == cc).astype(F32)
    same_head = (rr < HEAD) == (cc < HEAD)
    ltri3 = (_iota((C, 3 * C), 0) >= _iota((C, 3 * C), 1) % C).astype(BF16)

    def stack(x):
        xb = x.astype(BF16)
        zero = jnp.zeros_like(xb)
        return jnp.concatenate([jnp.where(head0, xb, zero), jnp.where(head0, zero, xb)], axis=0)

    def stack2(x, y):
        return jnp.concatenate([stack(x), stack(y)], axis=1)

    n_pairs = r_ref.shape[1]
    n_chunks = tb // C
    units = [(q, c) for q in range(n_pairs) for c in range(n_chunks)]

    def load(ref):
        return [ref[0, q, c * C:(c + 1) * C, :] for q, c in units]

    r_, lw_, k_, v_, a_, b_ = (load(ref) for ref in (r_ref, lw_ref, k_ref, v_ref, a_ref, b_ref))
    cum_ = [_dot3_rhs(ltri3, lw) for lw in lw_]
    cend_ = [cum[C - 1:C, :] for cum in cum_]
    at_ = [a * jnp.exp(cum - lw) for a, cum, lw in zip(a_, cum_, lw_)]
    rt_ = [r * jnp.exp(cum) for r, cum in zip(r_, cum_)]
    ginv_ = [jnp.exp(-cum) for cum in cum_]
    gend_ = [jnp.exp(cend - cum) for cend, cum in zip(cend_, cum_)]
    bk_end_ = [jnp.concatenate([b * ge, k * ge], axis=0) for b, k, ge in zip(b_, k_, gend_)]
    vst_ = [stack(v) for v in v_]

    G_ = [_dot_nt(jnp.concatenate([at, rt], axis=0), jnp.concatenate([stack(b * gi), stack(k * gi)], axis=0))
          for at, rt, b, k, gi in zip(at_, rt_, b_, k_, ginv_)]
    n1_ = [jnp.where(strict, G[:C, :LANES], 0.0) for G in G_]
    aak_ = [jnp.where(strict, G[:C, LANES:], 0.0) for G in G_]
    arb_ = [jnp.where(incl, G[C:, :LANES], 0.0) for G in G_]
    ark_ = [jnp.where(incl, G[C:, LANES:], 0.0) for G in G_]
    av_ = [_dot(jnp.concatenate([aak, ark], axis=0), vst) for aak, ark, vst in zip(aak_, ark_, vst_)]

    s1_ = [stack(n1) for n1 in n1_]
    n2_ = [_dot(n1, s1) for n1, s1 in zip(n1_, s1_)]
    x_ = [_dot(n2, jnp.concatenate([s1, stack(n2)], axis=1)) for n2, s1 in zip(n2_, s1_)]
    t_ = [eye_w + n1 + n2 + x[:, :LANES] for n1, n2, x in zip(n1_, n2_, x_)]
    np_ = [x[:, LANES:] for x in x_]
    for _ in range(3):
        x_ = [_dot(npow, stack2(t, npow)) for t, npow in zip(t_, np_)]
        t_ = [t + x[:, :LANES] for t, x in zip(t_, x_)]
        np_ = [x[:, LANES:] for x in x_]
    t_ = [t + _dot(npow, stack(t)) for t, npow in zip(t_, np_)]

    x_ = [_dot(t, stack2(at, av[:C])) for t, at, av in zip(t_, at_, av_)]
    z_ = [_dot(arb, stack2(x[:, :LANES], x[:, LANES:])) for arb, x in zip(arb_, x_)]
    rp_ = [rt + z[:, :LANES] for rt, z in zip(rt_, z_)]
    p3_ = [z[:, LANES:] + av[C:] for z, av in zip(z_, av_)]
    rhs_ = [jnp.concatenate([x, jnp.concatenate([jnp.zeros_like(v), v], axis=1)], axis=0)
            for x, v in zip(x_, v_)]
    mq_ = [_dot(bk_end.T, rhs) for bk_end, rhs in zip(bk_end_, rhs_)]
    m_ = [eye * jnp.exp(cend) + jnp.where(same_head, mq[:, :LANES], 0.0) for cend, mq in zip(cend_, mq_)]
    q_ = [jnp.where(same_head, mq[:, LANES:], 0.0) for mq in mq_]

    pairs = range(n_pairs)
    H_ = [h_ref[q] for q in pairs]
    ys_ = [[] for _ in pairs]
    rm_ = [jnp.concatenate([rp, m], axis=0) for rp, m in zip(rp_, m_)]
    for c in range(n_chunks):
        both_ = [_dot(rm_[q * n_chunks + c], H_[q]) for q in pairs]
        for q in pairs:
            ys_[q].append(both_[q][:C] + p3_[q * n_chunks + c])
        H_ = [both_[q][C:] + q_[q * n_chunks + c] for q in pairs]
    for q in pairs:
        h_ref[q] = H_[q]

    emean = emean_ref[...]
    y_ = [jnp.concatenate(ys, axis=0) for ys in ys_]
    mu_ = [_dot2_lhs(y, emean) for y in y_]
    yc_ = [y - mu for y, mu in zip(y_, mu_)]
    var_ = [_dot2_lhs(yc * yc, emean) for yc in yc_]
    for q in pairs:
        yn = yc_[q] * lax.rsqrt(var_[q] + GN_EPS) * gng_ref[q] + gnb_ref[q]
        o_ref[0, q] = ((yn + bonus_ref[0, q]) * g_ref[0, q]).astype(BF16)


def _wkv(r, lw, k, v, a, b, g, bonus, gn_g, gn_b, emean):
    B, P, S, _ = r.shape
    tb = WKV_TB
    pp = WKV_PAIRS
    seq = pl.BlockSpec((1, pp, tb, LANES), lambda bi, p, s: (bi, p, s, 0))
    par = pl.BlockSpec((pp, 1, LANES), lambda bi, p, s: (p, 0, 0))
    return pl.pallas_call(
        _wkv_kernel,
        grid=(B, P // pp, S // tb),
        in_specs=[seq] * 8 + [par, par, pl.BlockSpec((2 * LANES, LANES), lambda bi, p, s: (0, 0))],
        out_specs=seq,
        out_shape=jax.ShapeDtypeStruct((B, P, S, LANES), BF16),
        scratch_shapes=[pltpu.VMEM((pp, LANES, LANES), F32)],
        compiler_params=pltpu.CompilerParams(dimension_semantics=("arbitrary", "arbitrary", "arbitrary"),
                                             vmem_limit_bytes=VMEM_LIMIT),
        name="wkv",
    )(r, lw, k, v, a, b, g, bonus, gn_g, gn_b, emean)


def _mixer_kernel(x_ref, lng_ref, lnb_ref, ya_ref, yb_ref, wout_ref, l1g_ref, l1b_ref, wr_ref, br_ref, below_ref,
                  base_ref, x1_ref, route_ref, counts_ref, carry_ref, wout_bf_ref):
    tm = x_ref.shape[1]

    @pl.when((pl.program_id(0) == 0) & (pl.program_id(1) == 0))
    def _():
        carry_ref[...] = jnp.zeros_like(carry_ref)
        wout_bf_ref[...] = wout_ref[...].astype(BF16)

    x0 = _layer_norm(x_ref[0], lng_ref[...], lnb_ref[...], LN_EPS)
    ymix = jnp.concatenate([ya_ref[0, p] for p in range(N_PAIRS)] + [yb_ref[0]], axis=-1)
    mix = jnp.dot(ymix, wout_bf_ref[...], preferred_element_type=F32)
    x1 = _layer_norm(ALPHA * x0 + mix, l1g_ref[...], l1b_ref[...], LN_EPS)
    x1b = x1.astype(BF16)
    half = D_MODEL // 2
    lo_bits = lax.bitcast_convert_type(x1b[:, :half].astype(F32), jnp.uint32)
    hi_bits = lax.bitcast_convert_type(x1b[:, half:].astype(F32), jnp.uint32)
    x1_ref[0] = (hi_bits & jnp.uint32(0xFFFF0000)) | (lo_bits >> 16)

    hi, mid = _split2(x1)
    wide = jnp.dot(hi, wr_ref[...], preferred_element_type=F32)
    logits = (wide[:, :LANES] + wide[:, LANES:]
              + jnp.dot(mid, wr_ref[:, :LANES], preferred_element_type=F32)) + br_ref[...]
    lane = _iota((tm, LANES), 1).astype(F32)
    far = float(4 * LANES)
    is_g = jnp.where(lane >= N_EXPERTS, jnp.where(lane < N_EXPERTS + N_GROUPS, 1.0, 0.0), 0.0) > 0.5
    gl = jnp.where(is_g, logits, NEG)
    gmax = jnp.max(gl, axis=-1, keepdims=True)
    gsel = jnp.min(jnp.where(gl == gmax, lane, far), axis=-1, keepdims=True) - N_EXPERTS
    p_group = 1.0 / jnp.sum(jnp.where(is_g, jnp.exp(gl - gmax), 0.0), axis=-1, keepdims=True)
    grp_of_lane = jnp.floor(lane * (1.0 / EXPERTS_PER_GROUP))
    el = jnp.where(grp_of_lane == gsel, logits, NEG)
    v1 = jnp.max(el, axis=-1, keepdims=True)
    i1 = jnp.min(jnp.where(el == v1, lane, far), axis=-1, keepdims=True)
    el2 = jnp.where(lane == i1, NEG, el)
    v2 = jnp.max(el2, axis=-1, keepdims=True)
    i2 = jnp.min(jnp.where(el2 == v2, lane, far), axis=-1, keepdims=True)
    e21 = jnp.exp(v2 - v1)
    w1 = p_group / (1.0 + e21)
    w2 = p_group * e21 / (1.0 + e21)

    oh1 = lane == i1
    oh2 = lane == i2
    below = below_ref[...]
    o1 = jnp.where(oh1, 1.0, 0.0)
    o2 = jnp.where(oh2, 1.0, 0.0)
    c12 = jnp.dot(below, jnp.concatenate([o1, o2], axis=1).astype(BF16), preferred_element_type=F32)
    c1 = c12[:, :LANES]
    c2 = c12[:, LANES:]
    tot1 = jnp.sum(o1, axis=0, keepdims=True)
    carry = carry_ref[0:1, :]
    rank1 = jnp.sum(jnp.where(oh1, c1 + carry, 0.0), axis=-1, keepdims=True)
    rank2 = jnp.sum(jnp.where(oh2, c2 + carry + tot1, 0.0), axis=-1, keepdims=True)
    carry = carry + tot1 + jnp.sum(o2, axis=0, keepdims=True)
    carry_ref[0:1, :] = carry
    counts_ref[...] = jnp.broadcast_to(carry, counts_ref.shape)

    fields = (i1, i2, w1, w2, rank1, rank2)
    route = jnp.zeros((tm, LANES), F32)
    for n, f in enumerate(fields):
        route = jnp.where(lane == n, f, route)
    route_ref[0] = route

    base_ref[0] = ALPHA * x1


def _mixer(x, ln_g, ln_b, ya, yb, w_out, l1g, l1b, wr3, br):
    B, S, _ = x.shape
    tm = MIX_TM
    const = lambda shape: pl.BlockSpec(shape, lambda b, s: (0,) * len(shape))
    row = lambda w: pl.BlockSpec((1, tm, w), lambda b, s: (b, s, 0))
    below = (jnp.arange(tm)[:, None] > jnp.arange(tm)[None, :]).astype(BF16)
    return pl.pallas_call(
        _mixer_kernel,
        grid=(B, S // tm),
        in_specs=[
            row(D_MODEL), const((1, D_MODEL)), const((1, D_MODEL)),
            pl.BlockSpec((1, N_PAIRS, tm, LANES), lambda b, s: (b, 0, s, 0)), row(D_GMLP),
            pl.BlockSpec((D_MODEL, D_MODEL), lambda b, s: (0, 0), pipeline_mode=pl.Buffered(1)),
            const((1, D_MODEL)), const((1, D_MODEL)),
            const((D_MODEL, 2 * LANES)), const((1, LANES)), const((tm, tm)),
        ],
        out_specs=[row(D_MODEL), row(D_MODEL // 2), row(LANES), const((8, LANES))],
        out_shape=[jax.ShapeDtypeStruct((B, S, D_MODEL), F32), jax.ShapeDtypeStruct((B, S, D_MODEL // 2), jnp.uint32),
                   jax.ShapeDtypeStruct((B, S, LANES), F32), jax.ShapeDtypeStruct((8, LANES), F32)],
        scratch_shapes=[pltpu.VMEM((8, LANES), F32), pltpu.VMEM((D_MODEL, D_MODEL), BF16)],
        compiler_params=pltpu.CompilerParams(dimension_semantics=("arbitrary", "arbitrary"),
                                             vmem_limit_bytes=VMEM_LIMIT),
        name="mixer",
    )(x, ln_g, ln_b, ya, yb, w_out, l1g, l1b, wr3, br, below)


def _slots_kernel(route_ref, counts_ref, dest_ref, pend_ref):
    tm = route_ref.shape[0]
    lane = _iota((tm, LANES), 1)
    route = route_ref[...]
    oh1 = lane == route[:, 0:1].astype(jnp.int32)
    oh2 = lane == route[:, 1:2].astype(jnp.int32)

    counts = counts_ref[0:1, :]
    padded = jnp.floor((counts + (EXPERT_ROWS - 1)) * (1.0 / EXPERT_ROWS)) * EXPERT_ROWS
    upper = (_iota((LANES, LANES), 0) <= _iota((LANES, LANES), 1)).astype(BF16)
    pend = _dot3_lhs(jnp.broadcast_to(padded, (8, LANES)), upper)[0:1, :]
    pstart = pend - padded
    d1 = jnp.sum(jnp.where(oh1, pstart, 0.0), axis=-1, keepdims=True) + route[:, 4:5]
    d2 = jnp.sum(jnp.where(oh2, pstart, 0.0), axis=-1, keepdims=True) + route[:, 5:6]
    dest = jnp.where(lane == 0, d1, jnp.where(lane == 1, d2, 0.0))
    dest_ref[...] = jnp.transpose(dest)[0:dest_ref.shape[0], :].astype(jnp.int32)
    pend_ref[...] = jnp.broadcast_to(pend, (8, LANES)).astype(jnp.int32)


def _slots(route, counts):
    T = route.shape[0]
    tm = SLOT_TM
    return pl.pallas_call(
        _slots_kernel,
        grid=(T // tm,),
        in_specs=[pl.BlockSpec((tm, LANES), lambda i: (i, 0)), pl.BlockSpec((8, LANES), lambda i: (0, 0))],
        out_specs=[pl.BlockSpec((8, tm), lambda i: (0, i)),
                   pl.BlockSpec((8, LANES), lambda i: (0, 0))],
        out_shape=[jax.ShapeDtypeStruct((8, T), jnp.int32), jax.ShapeDtypeStruct((8, LANES), jnp.int32)],
        compiler_params=pltpu.CompilerParams(dimension_semantics=("arbitrary",), vmem_limit_bytes=VMEM_LIMIT),
        name="slots",
    )(route, counts)


def _dispatch_kernel(pend_ref, dest0_ref, dest1_ref, x_ref, base_ref, p_ref, wpg_ref, bpg_ref, wpp_ref, xs_ref, resid_ref,
                     zero_ref, wpg_bf_ref, wpp_bf_ref, sem, zsem):
    tm = x_ref.shape[0]
    dest_refs = (dest0_ref, dest1_ref)

    @pl.when(pl.program_id(0) == 0)
    def _():
        wpg_bf_ref[...] = wpg_ref[...].astype(BF16)
        wpp_bf_ref[...] = wpp_ref[...].astype(BF16)
        zero_ref[...] = jnp.zeros_like(zero_ref)

        def tail(e):
            start = pl.multiple_of(jnp.maximum(pend_ref[e] - EXPERT_ROWS, 0), EXPERT_ROWS)
            return pltpu.make_async_copy(zero_ref, xs_ref.at[pl.ds(start, EXPERT_ROWS)], zsem)

        def unused(j):
            return pltpu.make_async_copy(
                zero_ref, xs_ref.at[pl.ds(pl.multiple_of(j * EXPERT_ROWS, EXPERT_ROWS), EXPERT_ROWS)], zsem)

        def start_unused(j, _):
            unused(j).start()
            return 0

        def wait_unused(j, _):
            unused(j).wait()
            return 0

        first_unused = pend_ref[N_EXPERTS - 1] // EXPERT_ROWS
        n_blocks = xs_ref.shape[0] // EXPERT_ROWS
        for e in range(N_EXPERTS):
            tail(e).start()
        lax.fori_loop(first_unused, n_blocks, start_unused, 0)
        for e in range(N_EXPERTS):
            tail(e).wait()
        lax.fori_loop(first_unused, n_blocks, wait_unused, 0)

    for t in range(tm):
        for j in range(TOP_K):
            pltpu.make_async_copy(x_ref.at[pl.ds(t, 1)], xs_ref.at[pl.ds(dest_refs[j][t], 1)],
                                  sem).start(priority=j)

    xw = x_ref[...]
    x_lo = lax.bitcast_convert_type(xw << 16, F32)
    x_hi = lax.bitcast_convert_type(xw & jnp.uint32(0xFFFF0000), F32)
    x1b = jnp.concatenate([x_lo, x_hi], axis=1).astype(BF16)
    gate = _sigmoid(jnp.dot(x1b, wpg_bf_ref[...], preferred_element_type=F32) + bpg_ref[...])
    ple = gate * jnp.dot(p_ref[...].astype(BF16), wpp_bf_ref[...], preferred_element_type=F32)
    resid_ref[...] = base_ref[...] + ple

    for j in range(TOP_K):
        pltpu.make_async_copy(x_ref, xs_ref.at[pl.ds(0, tm)], sem).wait()


def _dispatch(pend, dests, x1, base, p, wpg, bpg, wpp, n_rows):
    T, width = x1.shape
    tm = DISPATCH_TM
    const = lambda shape: pl.BlockSpec(shape, lambda i, pe: (0,) * len(shape))
    resident = lambda shape: pl.BlockSpec(shape, lambda i, pe: (0,) * len(shape), pipeline_mode=pl.Buffered(1))
    tile = lambda w: pl.BlockSpec((tm, w), lambda i, pe: (i, 0))
    index_list = pl.BlockSpec((tm,), lambda i, pe: (i,), memory_space=pltpu.SMEM)
    return pl.pallas_call(
        _dispatch_kernel,
        grid_spec=pltpu.PrefetchScalarGridSpec(
            num_scalar_prefetch=1,
            grid=(T // tm,),
            in_specs=[index_list, index_list,
                      tile(width), tile(D_MODEL), tile(D_PLE),
                      resident((D_MODEL, D_MODEL)), const((1, D_MODEL)), resident((D_PLE, D_MODEL))],
            out_specs=[pl.BlockSpec(memory_space=pl.ANY), tile(D_MODEL)],
            scratch_shapes=[pltpu.VMEM((EXPERT_ROWS, width), x1.dtype), pltpu.VMEM((D_MODEL, D_MODEL), BF16),
                            pltpu.VMEM((D_PLE, D_MODEL), BF16), pltpu.SemaphoreType.DMA,
                            pltpu.SemaphoreType.DMA],
        ),
        out_shape=[jax.ShapeDtypeStruct((n_rows, width), x1.dtype), jax.ShapeDtypeStruct((T, D_MODEL), F32)],
        compiler_params=pltpu.CompilerParams(dimension_semantics=("arbitrary",), vmem_limit_bytes=VMEM_LIMIT),
        name="dispatch",
    )(pend, dests[0], dests[1], x1, base, p, wpg, bpg, wpp)


def _experts_kernel(pend_ref, xs_ref, wg_ref, wu_ref, wd_ref, ys_ref, xbuf_ref, ybuf_ref, wgu_ref, wdb_ref,
                    in_sem, out_sem):
    rows = EXPERT_ROWS
    e = pl.program_id(0)
    first = jnp.where(e == 0, 0, pend_ref[jnp.maximum(e - 1, 0)]) // rows
    last = pend_ref[e] // rows
    n_used = pend_ref[N_EXPERTS - 1] // rows

    def block_rows(ref, b):
        return ref.at[pl.ds(pl.multiple_of(b * rows, rows), rows)]

    depth = xbuf_ref.shape[0]
    row_priority = 1

    def x_copy(b):
        slot = b % depth
        return pltpu.make_async_copy(block_rows(xs_ref, b), xbuf_ref.at[slot], in_sem.at[slot])

    def y_copy(b):
        slot = b % depth
        return pltpu.make_async_copy(ybuf_ref.at[slot], block_rows(ys_ref, b), out_sem.at[slot])

    @pl.when(e == 0)
    def _():
        for ahead in range(depth - 1):
            @pl.when(ahead < n_used)
            def _():
                x_copy(ahead).start(priority=row_priority)

    @pl.when(last > first)
    def _():
        wgu_ref[:, :D_EXPERT] = wg_ref[0].astype(BF16)
        wgu_ref[:, D_EXPERT:] = wu_ref[0].astype(BF16)
        wdb_ref[...] = wd_ref[0].astype(BF16)

        def body(b, _):
            slot = b % depth

            @pl.when(b + depth - 1 < n_used)
            def _():
                x_copy(b + depth - 1).start(priority=row_priority)

            x_copy(b).wait()

            @pl.when(b >= depth)
            def _():
                y_copy(b - depth).wait()

            xw = xbuf_ref[slot]
            x_lo = lax.bitcast_convert_type(xw << 16, F32)
            x_hi = lax.bitcast_convert_type(xw & jnp.uint32(0xFFFF0000), F32)
            xb = jnp.concatenate([x_lo, x_hi], axis=1).astype(BF16)
            h = jnp.dot(xb, wgu_ref[...], preferred_element_type=F32)
            hg = h[:, :D_EXPERT]
            hid = hg * _sigmoid(hg) * h[:, D_EXPERT:]
            ybuf_ref[slot] = jnp.dot(hid.astype(BF16), wdb_ref[...], preferred_element_type=F32)
            y_copy(b).start(priority=row_priority)
            return 0

        lax.fori_loop(first, last, body, 0)

    @pl.when(e == N_EXPERTS - 1)
    def _():
        for back in range(depth, 0, -1):
            @pl.when(n_used >= back)
            def _():
                y_copy(n_used - back).wait()

        ybuf_ref[0] = jnp.zeros(ybuf_ref.shape[1:], F32)

        def unused(b):
            return pltpu.make_async_copy(ybuf_ref.at[0], block_rows(ys_ref, b), out_sem.at[0])

        def start_unused(b, _):
            unused(b).start()
            return 0

        def wait_unused(b, _):
            unused(b).wait()
            return 0

        n_blocks = ys_ref.shape[0] // rows
        lax.fori_loop(n_used, n_blocks, start_unused, 0)
        lax.fori_loop(n_used, n_blocks, wait_unused, 0)


def _experts(pend, xs, wg, wu, wd):
    n_rows = xs.shape[0]
    rows = EXPERT_ROWS
    wspec = lambda shape: pl.BlockSpec((1,) + shape, lambda e, pe: (e, 0, 0))
    return pl.pallas_call(
        _experts_kernel,
        grid_spec=pltpu.PrefetchScalarGridSpec(
            num_scalar_prefetch=1,
            grid=(N_EXPERTS,),
            in_specs=[pl.BlockSpec(memory_space=pl.ANY),
                      wspec((D_MODEL, D_EXPERT)), wspec((D_MODEL, D_EXPERT)), wspec((D_EXPERT, D_MODEL))],
            out_specs=pl.BlockSpec(memory_space=pl.ANY),
            scratch_shapes=[pltpu.VMEM((EXPERT_DEPTH, rows, D_MODEL // 2), jnp.uint32),
                            pltpu.VMEM((EXPERT_DEPTH, rows, D_MODEL), F32),
                            pltpu.VMEM((D_MODEL, 2 * D_EXPERT), BF16), pltpu.VMEM((D_EXPERT, D_MODEL), BF16),
                            pltpu.SemaphoreType.DMA((EXPERT_DEPTH,)), pltpu.SemaphoreType.DMA((EXPERT_DEPTH,))],
        ),
        out_shape=jax.ShapeDtypeStruct((n_rows, D_MODEL), F32),
        compiler_params=pltpu.CompilerParams(dimension_semantics=("arbitrary",), vmem_limit_bytes=VMEM_LIMIT),
        name="experts",
    )(pend, xs, wg, wu, wd)


def _combine_kernel(dest0_ref, dest1_ref, dest0_next_ref, dest1_next_ref, ys_ref, resid_ref, route_ref, lg_ref, lb_ref,
                    o_ref, buf_ref, sem):
    tm = buf_ref.shape[2]
    i = pl.program_id(0)
    dest_refs = (dest0_ref, dest1_ref)
    dest_next_refs = (dest0_next_ref, dest1_next_ref)

    def gather(drefs, offset, s):
        for t in range(tm):
            for j in range(TOP_K):
                pltpu.make_async_copy(ys_ref.at[pl.ds(drefs[j][offset + t], 1)],
                                      buf_ref.at[s, j, pl.ds(t, 1)], sem.at[s]).start(priority=j)

    def drain(s):
        for j in range(TOP_K):
            pltpu.make_async_copy(ys_ref.at[pl.ds(0, tm)], buf_ref.at[s, j], sem.at[s]).wait()

    def finish(s):
        rows = slice(s * tm, (s + 1) * tm)
        drain(s)
        route = route_ref[rows, :]
        ffn = buf_ref[s, 0] * route[:, 2:3] + buf_ref[s, 1] * route[:, 3:4]
        o_ref[rows, :] = _layer_norm(resid_ref[rows, :] + ffn, lg_ref[...], lb_ref[...], LN_EPS)

    @pl.when(i == 0)
    def _():
        gather(dest_refs, 0, 0)

    gather(dest_refs, tm, 1)
    finish(0)
    gather(dest_next_refs, 0, 0)
    finish(1)

    @pl.when(i == pl.num_programs(0) - 1)
    def _():
        drain(0)


def _combine(dests, ys, resid, route, l2g, l2b):
    T = resid.shape[0]
    tm = COMBINE_TM
    nt = T // tm
    tile = lambda w: pl.BlockSpec((2 * tm, w), lambda i: (i, 0))
    const = lambda shape: pl.BlockSpec(shape, lambda i: (0,) * len(shape))
    pair_list = pl.BlockSpec((2 * tm,), lambda i: (i,), memory_space=pltpu.SMEM)
    next_list = pl.BlockSpec((tm,), lambda i: (jnp.minimum(2 * i + 2, nt - 1),), memory_space=pltpu.SMEM)
    return pl.pallas_call(
        _combine_kernel,
        grid=(nt // 2,),
        in_specs=[pair_list, pair_list, next_list, next_list,
                  pl.BlockSpec(memory_space=pl.ANY),
                  tile(D_MODEL), tile(LANES), const((1, D_MODEL)), const((1, D_MODEL))],
        out_specs=pl.BlockSpec((2 * tm, D_MODEL), lambda i: (i, 0)),
        out_shape=jax.ShapeDtypeStruct((T, D_MODEL), F32),
        scratch_shapes=[pltpu.VMEM((2, TOP_K, tm, D_MODEL), F32), pltpu.SemaphoreType.DMA((2,))],
        compiler_params=pltpu.CompilerParams(dimension_semantics=("arbitrary",), vmem_limit_bytes=VMEM_LIMIT),
        name="combine",
    )(dests[0], dests[1], dests[0], dests[1], ys, resid, route, l2g, l2b)


def _block_diag_const(n, blk, val):
    idx = jnp.arange(n) // blk
    return jnp.where(idx[:, None] == idx[None, :], val, 0.0).astype(BF16)


def kernel(x, p, ln_emb_g, ln_emb_b, w_in, mu_shift, w0, w_decay_up, a0, w_iclr_up, w_gate_up, k_k, k_a, r_k, gn_g, gn_b, gmlp_ln_g, gmlp_ln_b, w_spatial, b_spatial, w_out, ln1_g, ln1_b, w_group_router, b_group_router, w_expert_router, b_expert_router, w_exp_gate, w_exp_up, w_exp_down, w_ple_gate, b_ple_gate, w_ple_proj, ln2_g, ln2_b):
    B, S, D = x.shape
    T = B * S
    row = lambda t: t.reshape(1, -1).astype(F32)

    zl = jnp.zeros((DECAY_LORA, D_RWKV), F32)
    wwa = jnp.concatenate([jnp.concatenate([w_decay_up[0], zl], axis=1),
                           jnp.concatenate([zl, w_iclr_up[0]], axis=1)], axis=0).astype(BF16)
    w0a0 = jnp.concatenate([w0[0], a0[0]]).reshape(1, -1)
    eones = jnp.tile(_block_diag_const(2 * LANES, HEAD, 1.0), (2, 1))
    emean = jnp.tile(_block_diag_const(LANES, HEAD, 1.0 / HEAD), (2, 1))

    r, lw, k, v, a, b, g, bonus, yb = _prep(
        x, row(ln_emb_g), row(ln_emb_b), w_in[0], row(mu_shift[0]), wwa, w0a0,
        w_gate_up[0].astype(BF16), row(k_k[0]), row(k_a[0]), row(r_k[0]), eones,
        row(gmlp_ln_g[0]), row(gmlp_ln_b[0]), w_spatial[0], b_spatial[0].T)

    ya = _wkv(r, lw, k, v, a, b, g, bonus, gn_g[0].reshape(N_PAIRS, 1, LANES), gn_b[0].reshape(N_PAIRS, 1, LANES),
              emean)

    wr = jnp.concatenate([w_expert_router[0].reshape(D, N_EXPERTS), w_group_router[0],
                          jnp.zeros((D, LANES - N_EXPERTS - N_GROUPS), F32)], axis=1)
    wr3 = jnp.concatenate(_split2(wr), axis=1)
    br = jnp.concatenate([b_expert_router[0].reshape(-1), b_group_router[0],
                          jnp.zeros((LANES - N_EXPERTS - N_GROUPS,), F32)]).reshape(1, LANES)
    base, x1, route, counts = _mixer(x, row(ln_emb_g), row(ln_emb_b), ya, yb, w_out[0], row(ln1_g[0]),
                                     row(ln1_b[0]), wr3, br)
    base = base.reshape(T, D)
    x1 = x1.reshape(T, D // 2)
    route = route.reshape(T, LANES)

    n_blocks = -(-(T * TOP_K) // EXPERT_ROWS) + N_EXPERTS
    dest, pend = _slots(route, counts)
    dests = (dest[0], dest[1])
    pend = pend[0, :N_EXPERTS]

    xs, resid = _dispatch(pend, dests, x1, base, p[0].reshape(T, D_PLE), w_ple_gate[0],
                          row(b_ple_gate[0]), w_ple_proj[0], n_blocks * EXPERT_ROWS)
    ys = _experts(pend, xs, w_exp_gate[0], w_exp_up[0], w_exp_down[0])
    out = _combine(dests, ys, resid, route, row(ln2_g[0]), row(ln2_b[0]))
    return out.reshape(B, S, D)
```

```python
import math

import jax
import jax.numpy as jnp
from jax import lax
from jax.experimental import pallas as pl
from jax.experimental.pallas import tpu as pltpu

F32 = jnp.float32
BF16 = jnp.bfloat16

D_MODEL = 1024
D_RWKV = 512
HEAD = 64
D_GMLP = 512
GMLP_GROUPS = 4
GROUP_W = 128
GCHUNK = 128
DECAY_LORA = 64
ICLR_LORA = 64
GATE_LORA = 128
N_SHIFT = 3 * D_RWKV + DECAY_LORA + ICLR_LORA + GATE_LORA
D_IN = N_SHIFT + 2 * D_GMLP
D_PLE = 256
N_GROUPS = 4
EXPERTS_PER_GROUP = 8
N_EXPERTS = 32
TOP_K = 2
D_EXPERT = 512
DEPTH = 1
ALPHA = (2.0 * DEPTH) ** 0.25
LN_EPS = 1e-5
GN_EPS = 64e-5
DECAY_SCALE = math.exp(-0.5)

LANES = 128
WKV_CHUNK = 64
N_PAIRS = D_RWKV // LANES
VMEM_LIMIT = 56 * 1024 * 1024

PREP_TM = 512
WKV_TB = 512
WKV_PAIRS = 4
MIX_TM = 512
SLOT_TM = 4096
EXPERT_ROWS = 256
EXPERT_DEPTH = 4
DISPATCH_TM = 1024
COMBINE_TM = 256
NEG = -1e30


def _dot(a, b):
    return jnp.dot(a.astype(BF16), b.astype(BF16), preferred_element_type=F32)


def _dot_nt(a, b):
    return lax.dot_general(a.astype(BF16), b.astype(BF16), (((1,), (1,)), ((), ())),
                           preferred_element_type=F32)


def _split3(x):
    hi = x.astype(BF16)
    r1 = x - hi.astype(F32)
    mid = r1.astype(BF16)
    lo = (r1 - mid.astype(F32)).astype(BF16)
    return hi, mid, lo


def _dot3_lhs(x, w):
    hi, mid, lo = _split3(x)
    w = w.astype(BF16)
    return (jnp.dot(hi, w, preferred_element_type=F32) + jnp.dot(mid, w, preferred_element_type=F32)
            + jnp.dot(lo, w, preferred_element_type=F32))


def _split2(x):
    hi = x.astype(BF16)
    return hi, (x - hi.astype(F32)).astype(BF16)


def _dot2_lhs(x, w2):
    hi, lo = _split2(x)
    return jnp.dot(jnp.concatenate([hi, lo], axis=1), w2, preferred_element_type=F32)


def _dot3_rhs(w3, x):
    hi, mid, lo = _split3(x)
    return jnp.dot(w3, jnp.concatenate([hi, mid, lo], axis=0), preferred_element_type=F32)


def _layer_norm(x, g, b, eps):
    mu = jnp.mean(x, axis=-1, keepdims=True)
    xc = x - mu
    var = jnp.mean(xc * xc, axis=-1, keepdims=True)
    return xc * lax.rsqrt(var + eps) * g + b


def _sigmoid(x):
    return 1.0 / (1.0 + jnp.exp(-x))


def _iota(shape, dim):
    return lax.broadcasted_iota(jnp.int32, shape, dim)


def _prep_kernel(x_ref, lng_ref, lnb_ref, win_ref, mu_ref, wwa_ref, w0a0_ref, wg_ref, kk_ref, ka_ref, rk_ref,
                 eones_ref, glng_ref, glnb_ref, wsp_ref, bsp_ref,
                 r_ref, lw_ref, k_ref, v_ref, a_ref, b_ref, g_ref, bonus_ref, yb_ref, carry_ref, win_bf_ref):
    tm = x_ref.shape[1]

    @pl.when((pl.program_id(0) == 0) & (pl.program_id(1) == 0))
    def _():
        win_bf_ref[...] = win_ref[...].astype(BF16)

    @pl.when(pl.program_id(1) == 0)
    def _():
        carry_ref[...] = jnp.zeros_like(carry_ref)

    x0 = _layer_norm(x_ref[0], lng_ref[...], lnb_ref[...], LN_EPS)
    proj = jnp.dot(x0.astype(BF16), win_bf_ref[...], preferred_element_type=F32)

    h = proj[:, :N_SHIFT]
    rolled = pltpu.roll(h, 1, 0)
    first = _iota((tm, N_SHIFT), 0) == 0
    prev = jnp.where(first, jnp.broadcast_to(carry_ref[0:1, :], (tm, N_SHIFT)), rolled)
    carry_ref[0:1, :] = h[tm - 1:tm, :]
    h = h + (prev - h) * mu_ref[...]

    r = h[:, 0:D_RWKV]
    k = h[:, D_RWKV:2 * D_RWKV]
    v = h[:, 2 * D_RWKV:3 * D_RWKV]
    xwa = h[:, 3 * D_RWKV:3 * D_RWKV + LANES]
    xg = h[:, 3 * D_RWKV + LANES:N_SHIFT]

    lane = _iota((tm, LANES), 1)
    twa = jnp.where(lane < DECAY_LORA, jnp.tanh(xwa), xwa)
    da = _dot(twa, wwa_ref[...]) + w0a0_ref[...]
    logw = -DECAY_SCALE * _sigmoid(da[:, :D_RWKV])
    ag = _sigmoid(da[:, D_RWKV:])
    g = _dot(_sigmoid(xg), wg_ref[...])

    eones2 = eones_ref[...]

    def head_sum(t):
        half = 2 * LANES
        return jnp.concatenate([_dot2_lhs(t[:, :half], eones2), _dot2_lhs(t[:, half:], eones2)], axis=1)

    kk = k * kk_ref[...]
    kk = kk * lax.rsqrt(jnp.maximum(head_sum(kk * kk), 1e-24))
    k = k * (1.0 + (ag - 1.0) * ka_ref[...])
    bonus = head_sum(r * k * rk_ref[...]) * v

    for p in range(N_PAIRS):
        sl = slice(p * LANES, (p + 1) * LANES)
        r_ref[0, p] = r[:, sl]
        lw_ref[0, p] = logw[:, sl]
        k_ref[0, p] = k[:, sl]
        v_ref[0, p] = v[:, sl]
        a_ref[0, p] = -kk[:, sl]
        b_ref[0, p] = (kk * ag)[:, sl]
        g_ref[0, p] = g[:, sl]
        bonus_ref[0, p] = bonus[:, sl]

    zin = proj[:, N_SHIFT:]
    z = 0.5 * zin * (1.0 + lax.erf(zin * (0.5 ** 0.5)))
    zu = z[:, :D_GMLP]
    zv = z[:, D_GMLP:]
    causal = _iota((GCHUNK, GCHUNK), 0) >= _iota((GCHUNK, GCHUNK), 1)
    for gi in range(GMLP_GROUPS):
        gs = slice(gi * GROUP_W, (gi + 1) * GROUP_W)
        zvn = _layer_norm(zv[:, gs], glng_ref[:, gs], glnb_ref[:, gs], LN_EPS)
        ws = jnp.where(causal, wsp_ref[gi], 0.0).astype(BF16)
        bcol = bsp_ref[:, gi:gi + 1]
        chunks = [slice(c * GCHUNK, (c + 1) * GCHUNK) for c in range(tm // GCHUNK)]
        zcat = jnp.concatenate([zvn[ts] for ts in chunks], axis=1).astype(BF16)
        mixed = jnp.dot(ws, zcat, preferred_element_type=F32) + bcol
        for ts in chunks:
            yb_ref[0, ts, gs] = (zu[ts, gs] * mixed[:, ts]).astype(BF16)


def _prep(x, ln_g, ln_b, w_in, mu, wwa, w0a0, wg, k_k, k_a, r_k, eones, glng, glnb, wsp, bsp):
    B, S, _ = x.shape
    tm = PREP_TM
    const = lambda shape: pl.BlockSpec(shape, lambda b, s: (0,) * len(shape))
    pair_spec = pl.BlockSpec((1, N_PAIRS, tm, LANES), lambda b, s: (b, 0, s, 0))
    pair_shape = jax.ShapeDtypeStruct((B, N_PAIRS, S, LANES), F32)
    return pl.pallas_call(
        _prep_kernel,
        grid=(B, S // tm),
        in_specs=[
            pl.BlockSpec((1, tm, D_MODEL), lambda b, s: (b, s, 0)),
            const((1, D_MODEL)), const((1, D_MODEL)),
            pl.BlockSpec((D_MODEL, D_IN), lambda b, s: (0, 0), pipeline_mode=pl.Buffered(1)),
            const((1, N_SHIFT)),
            const((LANES, 2 * D_RWKV)), const((1, 2 * D_RWKV)), const((GATE_LORA, D_RWKV)),
            const((1, D_RWKV)), const((1, D_RWKV)), const((1, D_RWKV)), const((4 * LANES, 2 * LANES)),
            const((1, D_GMLP)), const((1, D_GMLP)), const((GMLP_GROUPS, GCHUNK, GCHUNK)),
            const((GCHUNK, GMLP_GROUPS)),
        ],
        out_specs=[pair_spec] * 8 + [pl.BlockSpec((1, tm, D_GMLP), lambda b, s: (b, s, 0))],
        out_shape=[pair_shape] * 8 + [jax.ShapeDtypeStruct((B, S, D_GMLP), BF16)],
        scratch_shapes=[pltpu.VMEM((8, N_SHIFT), F32), pltpu.VMEM((D_MODEL, D_IN), BF16)],
        compiler_params=pltpu.CompilerParams(dimension_semantics=("arbitrary", "arbitrary"),
                                             vmem_limit_bytes=VMEM_LIMIT),
        name="prep",
    )(x, ln_g, ln_b, w_in, mu, wwa, w0a0, wg, k_k, k_a, r_k, eones, glng, glnb, wsp, bsp)


def _wkv_kernel(r_ref, lw_ref, k_ref, v_ref, a_ref, b_ref, g_ref, bonus_ref, gng_ref, gnb_ref, emean_ref,
                o_ref, h_ref):
    C = WKV_CHUNK
    tb = r_ref.shape[2]

    @pl.when(pl.program_id(2) == 0)
    def _():
        h_ref[...] = jnp.zeros_like(h_ref)

    tok = _iota((C, LANES), 0)
    lane = _iota((C, LANES), 1)
    head0 = lane < HEAD
    strict = tok > lane % HEAD
    incl = tok >= lane % HEAD
    eye_w = (tok == lane % HEAD).astype(F32)
    rr = _iota((LANES, LANES), 0)
    cc = _iota((LANES, LANES), 1)
    eye = (rr == cc).astype(F32)
    same_head = (rr < HEAD) == (cc < HEAD)
    ltri3 = (_iota((C, 3 * C), 0) >= _iota((C, 3 * C), 1) % C).astype(BF16)

    def stack(x):
        xb = x.astype(BF16)
        zero = jnp.zeros_like(xb)
        return jnp.concatenate([jnp.where(head0, xb, zero), jnp.where(head0, zero, xb)], axis=0)

    def stack2(x, y):
        return jnp.concatenate([stack(x), stack(y)], axis=1)

    n_pairs = r_ref.shape[1]
    n_chunks = tb // C
    units = [(q, c) for q in range(n_pairs) for c in range(n_chunks)]

    def load(ref):
        return [ref[0, q, c * C:(c + 1) * C, :] for q, c in units]

    r_, lw_, k_, v_, a_, b_ = (load(ref) for ref in (r_ref, lw_ref, k_ref, v_ref, a_ref, b_ref))
    cum_ = [_dot3_rhs(ltri3, lw) for lw in lw_]
    cend_ = [cum[C - 1:C, :] for cum in cum_]
    at_ = [a * jnp.exp(cum - lw) for a, cum, lw in zip(a_, cum_, lw_)]
    rt_ = [r * jnp.exp(cum) for r, cum in zip(r_, cum_)]
    ginv_ = [jnp.exp(-cum) for cum in cum_]
    gend_ = [jnp.exp(cend - cum) for cend, cum in zip(cend_, cum_)]
    bk_end_ = [jnp.concatenate([b * ge, k * ge], axis=0) for b, k, ge in zip(b_, k_, gend_)]
    vst_ = [stack(v) for v in v_]

    G_ = [_dot_nt(jnp.concatenate([at, rt], axis=0), jnp.concatenate([stack(b * gi), stack(k * gi)], axis=0))
          for at, rt, b, k, gi in zip(at_, rt_, b_, k_, ginv_)]
    n1_ = [jnp.where(strict, G[:C, :LANES], 0.0) for G in G_]
    aak_ = [jnp.where(strict, G[:C, LANES:], 0.0) for G in G_]
    arb_ = [jnp.where(incl, G[C:, :LANES], 0.0) for G in G_]
    ark_ = [jnp.where(incl, G[C:, LANES:], 0.0) for G in G_]
    av_ = [_dot(jnp.concatenate([aak, ark], axis=0), vst) for aak, ark, vst in zip(aak_, ark_, vst_)]

    s1_ = [stack(n1) for n1 in n1_]
    n2_ = [_dot(n1, s1) for n1, s1 in zip(n1_, s1_)]
    x_ = [_dot(n2, jnp.concatenate([s1, stack(n2)], axis=1)) for n2, s1 in zip(n2_, s1_)]
    t_ = [eye_w + n1 + n2 + x[:, :LANES] for n1, n2, x in zip(n1_, n2_, x_)]
    np_ = [x[:, LANES:] for x in x_]
    for _ in range(3):
        x_ = [_dot(npow, stack2(t, npow)) for t, npow in zip(t_, np_)]
        t_ = [t + x[:, :LANES] for t, x in zip(t_, x_)]
        np_ = [x[:, LANES:] for x in x_]
    t_ = [t + _dot(npow, stack(t)) for t, npow in zip(t_, np_)]

    x_ = [_dot(t, stack2(at, av[:C])) for t, at, av in zip(t_, at_, av_)]
    z_ = [_dot(arb, stack2(x[:, :LANES], x[:, LANES:])) for arb, x in zip(arb_, x_)]
    rp_ = [rt + z[:, :LANES] for rt, z in zip(rt_, z_)]
    p3_ = [z[:, LANES:] + av[C:] for z, av in zip(z_, av_)]
    rhs_ = [jnp.concatenate([x, jnp.concatenate([jnp.zeros_like(v), v], axis=1)], axis=0)
            for x, v in zip(x_, v_)]
    mq_ = [_dot(bk_end.T, rhs) for bk_end, rhs in zip(bk_end_, rhs_)]
    m_ = [eye * jnp.exp(cend) + jnp.where(same_head, mq[:, :LANES], 0.0) for cend, mq in zip(cend_, mq_)]
    q_ = [jnp.where(same_head, mq[:, LANES:], 0.0) for mq in mq_]

    pairs = range(n_pairs)
    H_ = [h_ref[q] for q in pairs]
    ys_ = [[] for _ in pairs]
    rm_ = [jnp.concatenate([rp, m], axis=0) for rp, m in zip(rp_, m_)]
    for c in range(n_chunks):
        both_ = [_dot(rm_[q * n_chunks + c], H_[q]) for q in pairs]
        for q in pairs:
            ys_[q].append(both_[q][:C] + p3_[q * n_chunks + c])
        H_ = [both_[q][C:] + q_[q * n_chunks + c] for q in pairs]
    for q in pairs:
        h_ref[q] = H_[q]

    emean = emean_ref[...]
    y_ = [jnp.concatenate(ys, axis=0) for ys in ys_]
    mu_ = [_dot2_lhs(y, emean) for y in y_]
    yc_ = [y - mu for y, mu in zip(y_, mu_)]
    var_ = [_dot2_lhs(yc * yc, emean) for yc in yc_]
    for q in pairs:
        yn = yc_[q] * lax.rsqrt(var_[q] + GN_EPS) * gng_ref[q] + gnb_ref[q]
        o_ref[0, q] = ((yn + bonus_ref[0, q]) * g_ref[0, q]).astype(BF16)


def _wkv(r, lw, k, v, a, b, g, bonus, gn_g, gn_b, emean):
    B, P, S, _ = r.shape
    tb = WKV_TB
    pp = WKV_PAIRS
    seq = pl.BlockSpec((1, pp, tb, LANES), lambda bi, p, s: (bi, p, s, 0))
    par = pl.BlockSpec((pp, 1, LANES), lambda bi, p, s: (p, 0, 0))
    return pl.pallas_call(
        _wkv_kernel,
        grid=(B, P // pp, S // tb),
        in_specs=[seq] * 8 + [par, par, pl.BlockSpec((2 * LANES, LANES), lambda bi, p, s: (0, 0))],
        out_specs=seq,
        out_shape=jax.ShapeDtypeStruct((B, P, S, LANES), BF16),
        scratch_shapes=[pltpu.VMEM((pp, LANES, LANES), F32)],
        compiler_params=pltpu.CompilerParams(dimension_semantics=("arbitrary", "arbitrary", "arbitrary"),
                                             vmem_limit_bytes=VMEM_LIMIT),
        name="wkv",
    )(r, lw, k, v, a, b, g, bonus, gn_g, gn_b, emean)


def _mixer_kernel(x_ref, lng_ref, lnb_ref, ya_ref, yb_ref, wout_ref, l1g_ref, l1b_ref, wr_ref, br_ref, below_ref,
                  base_ref, x1_ref, route_ref, counts_ref, carry_ref, wout_bf_ref):
    tm = x_ref.shape[1]

    @pl.when((pl.program_id(0) == 0) & (pl.program_id(1) == 0))
    def _():
        carry_ref[...] = jnp.zeros_like(carry_ref)
        wout_bf_ref[...] = wout_ref[...].astype(BF16)

    x0 = _layer_norm(x_ref[0], lng_ref[...], lnb_ref[...], LN_EPS)
    ymix = jnp.concatenate([ya_ref[0, p] for p in range(N_PAIRS)] + [yb_ref[0]], axis=-1)
    mix = jnp.dot(ymix, wout_bf_ref[...], preferred_element_type=F32)
    x1 = _layer_norm(ALPHA * x0 + mix, l1g_ref[...], l1b_ref[...], LN_EPS)
    x1b = x1.astype(BF16)
    half = D_MODEL // 2
    lo_bits = lax.bitcast_convert_type(x1b[:, :half].astype(F32), jnp.uint32)
    hi_bits = lax.bitcast_convert_type(x1b[:, half:].astype(F32), jnp.uint32)
    x1_ref[0] = (hi_bits & jnp.uint32(0xFFFF0000)) | (lo_bits >> 16)

    hi, mid = _split2(x1)
    wide = jnp.dot(hi, wr_ref[...], preferred_element_type=F32)
    logits = (wide[:, :LANES] + wide[:, LANES:]
              + jnp.dot(mid, wr_ref[:, :LANES], preferred_element_type=F32)) + br_ref[...]
    lane = _iota((tm, LANES), 1).astype(F32)
    far = float(4 * LANES)
    is_g = jnp.where(lane >= N_EXPERTS, jnp.where(lane < N_EXPERTS + N_GROUPS, 1.0, 0.0), 0.0) > 0.5
    gl = jnp.where(is_g, logits, NEG)
    gmax = jnp.max(gl, axis=-1, keepdims=True)
    gsel = jnp.min(jnp.where(gl == gmax, lane, far), axis=-1, keepdims=True) - N_EXPERTS
    p_group = 1.0 / jnp.sum(jnp.where(is_g, jnp.exp(gl - gmax), 0.0), axis=-1, keepdims=True)
    grp_of_lane = jnp.floor(lane * (1.0 / EXPERTS_PER_GROUP))
    el = jnp.where(grp_of_lane == gsel, logits, NEG)
    v1 = jnp.max(el, axis=-1, keepdims=True)
    i1 = jnp.min(jnp.where(el == v1, lane, far), axis=-1, keepdims=True)
    el2 = jnp.where(lane == i1, NEG, el)
    v2 = jnp.max(el2, axis=-1, keepdims=True)
    i2 = jnp.min(jnp.where(el2 == v2, lane, far), axis=-1, keepdims=True)
    e21 = jnp.exp(v2 - v1)
    w1 = p_group / (1.0 + e21)
    w2 = p_group * e21 / (1.0 + e21)

    oh1 = lane == i1
    oh2 = lane == i2
    below = below_ref[...]
    o1 = jnp.where(oh1, 1.0, 0.0)
    o2 = jnp.where(oh2, 1.0, 0.0)
    c12 = jnp.dot(below, jnp.concatenate([o1, o2], axis=1).astype(BF16), preferred_element_type=F32)
    c1 = c12[:, :LANES]
    c2 = c12[:, LANES:]
    tot1 = jnp.sum(o1, axis=0, keepdims=True)
    carry = carry_ref[0:1, :]
    rank1 = jnp.sum(jnp.where(oh1, c1 + carry, 0.0), axis=-1, keepdims=True)
    rank2 = jnp.sum(jnp.where(oh2, c2 + carry + tot1, 0.0), axis=-1, keepdims=True)
    carry = carry + tot1 + jnp.sum(o2, axis=0, keepdims=True)
    carry_ref[0:1, :] = carry
    counts_ref[...] = jnp.broadcast_to(carry, counts_ref.shape)

    fields = (i1, i2, w1, w2, rank1, rank2)
    route = jnp.zeros((tm, LANES), F32)
    for n, f in enumerate(fields):
        route = jnp.where(lane == n, f, route)
    route_ref[0] = route

    base_ref[0] = ALPHA * x1


def _mixer(x, ln_g, ln_b, ya, yb, w_out, l1g, l1b, wr3, br):
    B, S, _ = x.shape
    tm = MIX_TM
    const = lambda shape: pl.BlockSpec(shape, lambda b, s: (0,) * len(shape))
    row = lambda w: pl.BlockSpec((1, tm, w), lambda b, s: (b, s, 0))
    below = (jnp.arange(tm)[:, None] > jnp.arange(tm)[None, :]).astype(BF16)
    return pl.pallas_call(
        _mixer_kernel,
        grid=(B, S // tm),
        in_specs=[
            row(D_MODEL), const((1, D_MODEL)), const((1, D_MODEL)),
            pl.BlockSpec((1, N_PAIRS, tm, LANES), lambda b, s: (b, 0, s, 0)), row(D_GMLP),
            pl.BlockSpec((D_MODEL, D_MODEL), lambda b, s: (0, 0), pipeline_mode=pl.Buffered(1)),
            const((1, D_MODEL)), const((1, D_MODEL)),
            const((D_MODEL, 2 * LANES)), const((1, LANES)), const((tm, tm)),
        ],
        out_specs=[row(D_MODEL), row(D_MODEL // 2), row(LANES), const((8, LANES))],
        out_shape=[jax.ShapeDtypeStruct((B, S, D_MODEL), F32), jax.ShapeDtypeStruct((B, S, D_MODEL // 2), jnp.uint32),
                   jax.ShapeDtypeStruct((B, S, LANES), F32), jax.ShapeDtypeStruct((8, LANES), F32)],
        scratch_shapes=[pltpu.VMEM((8, LANES), F32), pltpu.VMEM((D_MODEL, D_MODEL), BF16)],
        compiler_params=pltpu.CompilerParams(dimension_semantics=("arbitrary", "arbitrary"),
                                             vmem_limit_bytes=VMEM_LIMIT),
        name="mixer",
    )(x, ln_g, ln_b, ya, yb, w_out, l1g, l1b, wr3, br, below)


def _slots_kernel(route_ref, counts_ref, dest_ref, pend_ref):
    tm = route_ref.shape[0]
    lane = _iota((tm, LANES), 1)
    route = route_ref[...]
    oh1 = lane == route[:, 0:1].astype(jnp.int32)
    oh2 = lane == route[:, 1:2].astype(jnp.int32)

    counts = counts_ref[0:1, :]
    padded = jnp.floor((counts + (EXPERT_ROWS - 1)) * (1.0 / EXPERT_ROWS)) * EXPERT_ROWS
    upper = (_iota((LANES, LANES), 0) <= _iota((LANES, LANES), 1)).astype(BF16)
    pend = _dot3_lhs(jnp.broadcast_to(padded, (8, LANES)), upper)[0:1, :]
    pstart = pend - padded
    d1 = jnp.sum(jnp.where(oh1, pstart, 0.0), axis=-1, keepdims=True) + route[:, 4:5]
    d2 = jnp.sum(jnp.where(oh2, pstart, 0.0), axis=-1, keepdims=True) + route[:, 5:6]
    dest = jnp.where(lane == 0, d1, jnp.where(lane == 1, d2, 0.0))
    dest_ref[...] = jnp.transpose(dest)[0:dest_ref.shape[0], :].astype(jnp.int32)
    pend_ref[...] = jnp.broadcast_to(pend, (8, LANES)).astype(jnp.int32)


def _slots(route, counts):
    T = route.shape[0]
    tm = SLOT_TM
    return pl.pallas_call(
        _slots_kernel,
        grid=(T // tm,),
        in_specs=[pl.BlockSpec((tm, LANES), lambda i: (i, 0)), pl.BlockSpec((8, LANES), lambda i: (0, 0))],
        out_specs=[pl.BlockSpec((8, tm), lambda i: (0, i)),
                   pl.BlockSpec((8, LANES), lambda i: (0, 0))],
        out_shape=[jax.ShapeDtypeStruct((8, T), jnp.int32), jax.ShapeDtypeStruct((8, LANES), jnp.int32)],
        compiler_params=pltpu.CompilerParams(dimension_semantics=("arbitrary",), vmem_limit_bytes=VMEM_LIMIT),
        name="slots",
    )(route, counts)


def _dispatch_kernel(pend_ref, dest0_ref, dest1_ref, x_ref, base_ref, p_ref, wpg_ref, bpg_ref, wpp_ref, xs_ref, resid_ref,
                     zero_ref, wpg_bf_ref, wpp_bf_ref, sem, zsem):
    tm = x_ref.shape[0]
    dest_refs = (dest0_ref, dest1_ref)

    @pl.when(pl.program_id(0) == 0)
    def _():
        wpg_bf_ref[...] = wpg_ref[...].astype(BF16)
        wpp_bf_ref[...] = wpp_ref[...].astype(BF16)
        zero_ref[...] = jnp.zeros_like(zero_ref)

        def tail(e):
            start = pl.multiple_of(jnp.maximum(pend_ref[e] - EXPERT_ROWS, 0), EXPERT_ROWS)
            return pltpu.make_async_copy(zero_ref, xs_ref.at[pl.ds(start, EXPERT_ROWS)], zsem)

        def unused(j):
            return pltpu.make_async_copy(
                zero_ref, xs_ref.at[pl.ds(pl.multiple_of(j * EXPERT_ROWS, EXPERT_ROWS), EXPERT_ROWS)], zsem)

        def start_unused(j, _):
            unused(j).start()
            return 0

        def wait_unused(j, _):
            unused(j).wait()
            return 0

        first_unused = pend_ref[N_EXPERTS - 1] // EXPERT_ROWS
        n_blocks = xs_ref.shape[0] // EXPERT_ROWS
        for e in range(N_EXPERTS):
            tail(e).start()
        lax.fori_loop(first_unused, n_blocks, start_unused, 0)
        for e in range(N_EXPERTS):
            tail(e).wait()
        lax.fori_loop(first_unused, n_blocks, wait_unused, 0)

    for t in range(tm):
        for j in range(TOP_K):
            pltpu.make_async_copy(x_ref.at[pl.ds(t, 1)], xs_ref.at[pl.ds(dest_refs[j][t], 1)],
                                  sem).start(priority=j)

    xw = x_ref[...]
    x_lo = lax.bitcast_convert_type(xw << 16, F32)
    x_hi = lax.bitcast_convert_type(xw & jnp.uint32(0xFFFF0000), F32)
    x1b = jnp.concatenate([x_lo, x_hi], axis=1).astype(BF16)
    gate = _sigmoid(jnp.dot(x1b, wpg_bf_ref[...], preferred_element_type=F32) + bpg_ref[...])
    ple = gate * jnp.dot(p_ref[...].astype(BF16), wpp_bf_ref[...], preferred_element_type=F32)
    resid_ref[...] = base_ref[...] + ple

    for j in range(TOP_K):
        pltpu.make_async_copy(x_ref, xs_ref.at[pl.ds(0, tm)], sem).wait()


def _dispatch(pend, dests, x1, base, p, wpg, bpg, wpp, n_rows):
    T, width = x1.shape
    tm = DISPATCH_TM
    const = lambda shape: pl.BlockSpec(shape, lambda i, pe: (0,) * len(shape))
    resident = lambda shape: pl.BlockSpec(shape, lambda i, pe: (0,) * len(shape), pipeline_mode=pl.Buffered(1))
    tile = lambda w: pl.BlockSpec((tm, w), lambda i, pe: (i, 0))
    index_list = pl.BlockSpec((tm,), lambda i, pe: (i,), memory_space=pltpu.SMEM)
    return pl.pallas_call(
        _dispatch_kernel,
        grid_spec=pltpu.PrefetchScalarGridSpec(
            num_scalar_prefetch=1,
            grid=(T // tm,),
            in_specs=[index_list, index_list,
                      tile(width), tile(D_MODEL), tile(D_PLE),
                      resident((D_MODEL, D_MODEL)), const((1, D_MODEL)), resident((D_PLE, D_MODEL))],
            out_specs=[pl.BlockSpec(memory_space=pl.ANY), tile(D_MODEL)],
            scratch_shapes=[pltpu.VMEM((EXPERT_ROWS, width), x1.dtype), pltpu.VMEM((D_MODEL, D_MODEL), BF16),
                            pltpu.VMEM((D_PLE, D_MODEL), BF16), pltpu.SemaphoreType.DMA,
                            pltpu.SemaphoreType.DMA],
        ),
        out_shape=[jax.ShapeDtypeStruct((n_rows, width), x1.dtype), jax.ShapeDtypeStruct((T, D_MODEL), F32)],
        compiler_params=pltpu.CompilerParams(dimension_semantics=("arbitrary",), vmem_limit_bytes=VMEM_LIMIT),
        name="dispatch",
    )(pend, dests[0], dests[1], x1, base, p, wpg, bpg, wpp)


def _experts_kernel(pend_ref, xs_ref, wg_ref, wu_ref, wd_ref, ys_ref, xbuf_ref, ybuf_ref, wgu_ref, wdb_ref,
                    in_sem, out_sem):
    rows = EXPERT_ROWS
    e = pl.program_id(0)
    first = jnp.where(e == 0, 0, pend_ref[jnp.maximum(e - 1, 0)]) // rows
    last = pend_ref[e] // rows
    n_used = pend_ref[N_EXPERTS - 1] // rows

    def block_rows(ref, b):
        return ref.at[pl.ds(pl.multiple_of(b * rows, rows), rows)]

    depth = xbuf_ref.shape[0]
    row_priority = 1

    def x_copy(b):
        slot = b % depth
        return pltpu.make_async_copy(block_rows(xs_ref, b), xbuf_ref.at[slot], in_sem.at[slot])

    def y_copy(b):
        slot = b % depth
        return pltpu.make_async_copy(ybuf_ref.at[slot], block_rows(ys_ref, b), out_sem.at[slot])

    @pl.when(e == 0)
    def _():
        for ahead in range(depth - 1):
            @pl.when(ahead < n_used)
            def _():
                x_copy(ahead).start(priority=row_priority)

    @pl.when(last > first)
    def _():
        wgu_ref[:, :D_EXPERT] = wg_ref[0].astype(BF16)
        wgu_ref[:, D_EXPERT:] = wu_ref[0].astype(BF16)
        wdb_ref[...] = wd_ref[0].astype(BF16)

        def body(b, _):
            slot = b % depth

            @pl.when(b + depth - 1 < n_used)
            def _():
                x_copy(b + depth - 1).start(priority=row_priority)

            x_copy(b).wait()

            @pl.when(b >= depth)
            def _():
                y_copy(b - depth).wait()

            xw = xbuf_ref[slot]
            x_lo = lax.bitcast_convert_type(xw << 16, F32)
            x_hi = lax.bitcast_convert_type(xw & jnp.uint32(0xFFFF0000), F32)
            xb = jnp.concatenate([x_lo, x_hi], axis=1).astype(BF16)
            h = jnp.dot(xb, wgu_ref[...], preferred_element_type=F32)
            hg = h[:, :D_EXPERT]
            hid = hg * _sigmoid(hg) * h[:, D_EXPERT:]
            ybuf_ref[slot] = jnp.dot(hid.astype(BF16), wdb_ref[...], preferred_element_type=F32)
            y_copy(b).start(priority=row_priority)
            return 0

        lax.fori_loop(first, last, body, 0)

    @pl.when(e == N_EXPERTS - 1)
    def _():
        for back in range(depth, 0, -1):
            @pl.when(n_used >= back)
            def _():
                y_copy(n_used - back).wait()

        ybuf_ref[0] = jnp.zeros(ybuf_ref.shape[1:], F32)

        def unused(b):
            return pltpu.make_async_copy(ybuf_ref.at[0], block_rows(ys_ref, b), out_sem.at[0])

        def start_unused(b, _):
            unused(b).start()
            return 0

        def wait_unused(b, _):
            unused(b).wait()
            return 0

        n_blocks = ys_ref.shape[0] // rows
        lax.fori_loop(n_used, n_blocks, start_unused, 0)
        lax.fori_loop(n_used, n_blocks, wait_unused, 0)


def _experts(pend, xs, wg, wu, wd):
    n_rows = xs.shape[0]
    rows = EXPERT_ROWS
    wspec = lambda shape: pl.BlockSpec((1,) + shape, lambda e, pe: (e, 0, 0))
    return pl.pallas_call(
        _experts_kernel,
        grid_spec=pltpu.PrefetchScalarGridSpec(
            num_scalar_prefetch=1,
            grid=(N_EXPERTS,),
            in_specs=[pl.BlockSpec(memory_space=pl.ANY),
                      wspec((D_MODEL, D_EXPERT)), wspec((D_MODEL, D_EXPERT)), wspec((D_EXPERT, D_MODEL))],
            out_specs=pl.BlockSpec(memory_space=pl.ANY),
            scratch_shapes=[pltpu.VMEM((EXPERT_DEPTH, rows, D_MODEL // 2), jnp.uint32),
                            pltpu.VMEM((EXPERT_DEPTH, rows, D_MODEL), F32),
                            pltpu.VMEM((D_MODEL, 2 * D_EXPERT), BF16), pltpu.VMEM((D_EXPERT, D_MODEL), BF16),
                            pltpu.SemaphoreType.DMA((EXPERT_DEPTH,)), pltpu.SemaphoreType.DMA((EXPERT_DEPTH,))],
        ),
        out_shape=jax.ShapeDtypeStruct((n_rows, D_MODEL), F32),
        compiler_params=pltpu.CompilerParams(dimension_semantics=("arbitrary",), vmem_limit_bytes=VMEM_LIMIT),
        name="experts",
    )(pend, xs, wg, wu, wd)


def _combine_kernel(dest0_ref, dest1_ref, dest0_next_ref, dest1_next_ref, ys_ref, resid_ref, route_ref, lg_ref, lb_ref,
                    o_ref, buf_ref, sem):
    tm = buf_ref.shape[2]
    i = pl.program_id(0)
    dest_refs = (dest0_ref, dest1_ref)
    dest_next_refs = (dest0_next_ref, dest1_next_ref)

    def gather(drefs, offset, s):
        for t in range(tm):
            for j in range(TOP_K):
                pltpu.make_async_copy(ys_ref.at[pl.ds(drefs[j][offset + t], 1)],
                                      buf_ref.at[s, j, pl.ds(t, 1)], sem.at[s]).start(priority=j)

    def drain(s):
        for j in range(TOP_K):
            pltpu.make_async_copy(ys_ref.at[pl.ds(0, tm)], buf_ref.at[s, j], sem.at[s]).wait()

    def finish(s):
        rows = slice(s * tm, (s + 1) * tm)
        drain(s)
        route = route_ref[rows, :]
        ffn = buf_ref[s, 0] * route[:, 2:3] + buf_ref[s, 1] * route[:, 3:4]
        o_ref[rows, :] = _layer_norm(resid_ref[rows, :] + ffn, lg_ref[...], lb_ref[...], LN_EPS)

    @pl.when(i == 0)
    def _():
        gather(dest_refs, 0, 0)

    gather(dest_refs, tm, 1)
    finish(0)
    gather(dest_next_refs, 0, 0)
    finish(1)

    @pl.when(i == pl.num_programs(0) - 1)
    def _():
        drain(0)


def _combine(dests, ys, resid, route, l2g, l2b):
    T = resid.shape[0]
    tm = COMBINE_TM
    nt = T // tm
    tile = lambda w: pl.BlockSpec((2 * tm, w), lambda i: (i, 0))
    const = lambda shape: pl.BlockSpec(shape, lambda i: (0,) * len(shape))
    pair_list = pl.BlockSpec((2 * tm,), lambda i: (i,), memory_space=pltpu.SMEM)
    next_list = pl.BlockSpec((tm,), lambda i: (jnp.minimum(2 * i + 2, nt - 1),), memory_space=pltpu.SMEM)
    return pl.pallas_call(
        _combine_kernel,
        grid=(nt // 2,),
        in_specs=[pair_list, pair_list, next_list, next_list,
                  pl.BlockSpec(memory_space=pl.ANY),
                  tile(D_MODEL), tile(LANES), const((1, D_MODEL)), const((1, D_MODEL))],
        out_specs=pl.BlockSpec((2 * tm, D_MODEL), lambda i: (i, 0)),
        out_shape=jax.ShapeDtypeStruct((T, D_MODEL), F32),
        scratch_shapes=[pltpu.VMEM((2, TOP_K, tm, D_MODEL), F32), pltpu.SemaphoreType.DMA((2,))],
        compiler_params=pltpu.CompilerParams(dimension_semantics=("arbitrary",), vmem_limit_bytes=VMEM_LIMIT),
        name="combine",
    )(dests[0], dests[1], dests[0], dests[1], ys, resid, route, l2g, l2b)


def _block_diag_const(n, blk, val):
    idx = jnp.arange(n) // blk
    return jnp.where(idx[:, None] == idx[None, :], val, 0.0).astype(BF16)


def kernel(x, p, ln_emb_g, ln_emb_b, w_in, mu_shift, w0, w_decay_up, a0, w_iclr_up, w_gate_up, k_k, k_a, r_k, gn_g, gn_b, gmlp_ln_g, gmlp_ln_b, w_spatial, b_spatial, w_out, ln1_g, ln1_b, w_group_router, b_group_router, w_expert_router, b_expert_router, w_exp_gate, w_exp_up, w_exp_down, w_ple_gate, b_ple_gate, w_ple_proj, ln2_g, ln2_b):
    B, S, D = x.shape
    T = B * S
    row = lambda t: t.reshape(1, -1).astype(F32)

    zl = jnp.zeros((DECAY_LORA, D_RWKV), F32)
    wwa = jnp.concatenate([jnp.concatenate([w_decay_up[0], zl], axis=1),
                           jnp.concatenate([zl, w_iclr_up[0]], axis=1)], axis=0).astype(BF16)
    w0a0 = jnp.concatenate([w0[0], a0[0]]).reshape(1, -1)
    eones = jnp.tile(_block_diag_const(2 * LANES, HEAD, 1.0), (2, 1))
    emean = jnp.tile(_block_diag_const(LANES, HEAD, 1.0 / HEAD), (2, 1))

    r, lw, k, v, a, b, g, bonus, yb = _prep(
        x, row(ln_emb_g), row(ln_emb_b), w_in[0], row(mu_shift[0]), wwa, w0a0,
        w_gate_up[0].astype(BF16), row(k_k[0]), row(k_a[0]), row(r_k[0]), eones,
        row(gmlp_ln_g[0]), row(gmlp_ln_b[0]), w_spatial[0], b_spatial[0].T)

    ya = _wkv(r, lw, k, v, a, b, g, bonus, gn_g[0].reshape(N_PAIRS, 1, LANES), gn_b[0].reshape(N_PAIRS, 1, LANES),
              emean)

    wr = jnp.concatenate([w_expert_router[0].reshape(D, N_EXPERTS), w_group_router[0],
                          jnp.zeros((D, LANES - N_EXPERTS - N_GROUPS), F32)], axis=1)
    wr3 = jnp.concatenate(_split2(wr), axis=1)
    br = jnp.concatenate([b_expert_router[0].reshape(-1), b_group_router[0],
                          jnp.zeros((LANES - N_EXPERTS - N_GROUPS,), F32)]).reshape(1, LANES)
    base, x1, route, counts = _mixer(x, row(ln_emb_g), row(ln_emb_b), ya, yb, w_out[0], row(ln1_g[0]),
                                     row(ln1_b[0]), wr3, br)
    base = base.reshape(T, D)
    x1 = x1.reshape(T, D // 2)
    route = route.reshape(T, LANES)

    n_blocks = -(-(T * TOP_K) // EXPERT_ROWS) + N_EXPERTS
    dest, pend = _slots(route, counts)
    dests = (dest[0], dest[1])
    pend = pend[0, :N_EXPERTS]

    xs, resid = _dispatch(pend, dests, x1, base, p[0].reshape(T, D_PLE), w_ple_gate[0],
                          row(b_ple_gate[0]), w_ple_proj[0], n_blocks * EXPERT_ROWS)
    ys = _experts(pend, xs, w_exp_gate[0], w_exp_up[0], w_exp_down[0])
    out = _combine(dests, ys, resid, route, row(ln2_g[0]), row(ln2_b[0]))
    return out.reshape(B, S, D)
```

```python
import math

import jax
import jax.numpy as jnp
from jax import lax
from jax.experimental import pallas as pl
from jax.experimental.pallas import tpu as pltpu

F32 = jnp.float32
BF16 = jnp.bfloat16

D_MODEL = 1024
D_RWKV = 512
HEAD = 64
D_GMLP = 512
GMLP_GROUPS = 4
GROUP_W = 128
GCHUNK = 128
DECAY_LORA = 64
ICLR_LORA = 64
GATE_LORA = 128
N_SHIFT = 3 * D_RWKV + DECAY_LORA + ICLR_LORA + GATE_LORA
D_IN = N_SHIFT + 2 * D_GMLP
D_PLE = 256
N_GROUPS = 4
EXPERTS_PER_GROUP = 8
N_EXPERTS = 32
TOP_K = 2
D_EXPERT = 512
DEPTH = 1
ALPHA = (2.0 * DEPTH) ** 0.25
LN_EPS = 1e-5
GN_EPS = 64e-5
DECAY_SCALE = math.exp(-0.5)

LANES = 128
WKV_CHUNK = 64
N_PAIRS = D_RWKV // LANES
VMEM_LIMIT = 56 * 1024 * 1024

PREP_TM = 512
WKV_TB = 512
WKV_PAIRS = 4
MIX_TM = 512
SLOT_TM = 4096
EXPERT_ROWS = 256
EXPERT_DEPTH = 4
DISPATCH_TM = 1024
COMBINE_TM = 256
NEG = -1e30


def _dot(a, b):
    return jnp.dot(a.astype(BF16), b.astype(BF16), preferred_element_type=F32)


def _dot_nt(a, b):
    return lax.dot_general(a.astype(BF16), b.astype(BF16), (((1,), (1,)), ((), ())),
                           preferred_element_type=F32)


def _split3(x):
    hi = x.astype(BF16)
    r1 = x - hi.astype(F32)
    mid = r1.astype(BF16)
    lo = (r1 - mid.astype(F32)).astype(BF16)
    return hi, mid, lo


def _dot3_lhs(x, w):
    hi, mid, lo = _split3(x)
    w = w.astype(BF16)
    return (jnp.dot(hi, w, preferred_element_type=F32) + jnp.dot(mid, w, preferred_element_type=F32)
            + jnp.dot(lo, w, preferred_element_type=F32))


def _split2(x):
    hi = x.astype(BF16)
    return hi, (x - hi.astype(F32)).astype(BF16)


def _dot2_lhs(x, w2):
    hi, lo = _split2(x)
    return jnp.dot(jnp.concatenate([hi, lo], axis=1), w2, preferred_element_type=F32)


def _dot3_rhs(w3, x):
    hi, mid, lo = _split3(x)
    return jnp.dot(w3, jnp.concatenate([hi, mid, lo], axis=0), preferred_element_type=F32)


def _layer_norm(x, g, b, eps):
    mu = jnp.mean(x, axis=-1, keepdims=True)
    xc = x - mu
    var = jnp.mean(xc * xc, axis=-1, keepdims=True)
    return xc * lax.rsqrt(var + eps) * g + b


def _sigmoid(x):
    return 1.0 / (1.0 + jnp.exp(-x))


def _iota(shape, dim):
    return lax.broadcasted_iota(jnp.int32, shape, dim)


def _prep_kernel(x_ref, lng_ref, lnb_ref, win_ref, mu_ref, wdec_ref, wiclr_ref, w0_ref, a0_ref, wgate_ref, kk_ref, ka_ref,
                 rk_ref, eones_ref, glng_ref, glnb_ref, wsp_ref, bsp_ref,
                 r_ref, lw_ref, k_ref, v_ref, a_ref, b_ref, g_ref, bonus_ref, yb_ref, carry_ref, win_bf_ref,
                 wwa_ref, wg_ref):
    tm = x_ref.shape[1]

    @pl.when((pl.program_id(0) == 0) & (pl.program_id(1) == 0))
    def _():
        win_bf_ref[...] = win_ref[...].astype(BF16)
        wg_ref[...] = wgate_ref[...].astype(BF16)
        wwa_ref[...] = jnp.zeros_like(wwa_ref)
        wwa_ref[0:DECAY_LORA, 0:D_RWKV] = wdec_ref[...].astype(BF16)
        wwa_ref[DECAY_LORA:, D_RWKV:] = wiclr_ref[...].astype(BF16)

    @pl.when(pl.program_id(1) == 0)
    def _():
        carry_ref[...] = jnp.zeros_like(carry_ref)

    x0 = _layer_norm(x_ref[0], lng_ref[...], lnb_ref[...], LN_EPS)
    proj = jnp.dot(x0.astype(BF16), win_bf_ref[...], preferred_element_type=F32)

    h = proj[:, :N_SHIFT]
    rolled = pltpu.roll(h, 1, 0)
    first = _iota((tm, N_SHIFT), 0) == 0
    prev = jnp.where(first, jnp.broadcast_to(carry_ref[0:1, :], (tm, N_SHIFT)), rolled)
    carry_ref[0:1, :] = h[tm - 1:tm, :]
    h = h + (prev - h) * mu_ref[...]

    r = h[:, 0:D_RWKV]
    k = h[:, D_RWKV:2 * D_RWKV]
    v = h[:, 2 * D_RWKV:3 * D_RWKV]
    xwa = h[:, 3 * D_RWKV:3 * D_RWKV + LANES]
    xg = h[:, 3 * D_RWKV + LANES:N_SHIFT]

    lane = _iota((tm, LANES), 1)
    twa = jnp.where(lane < DECAY_LORA, jnp.tanh(xwa), xwa)
    da = _dot(twa, wwa_ref[...])
    logw = -DECAY_SCALE * _sigmoid(da[:, :D_RWKV] + w0_ref[...])
    ag = _sigmoid(da[:, D_RWKV:] + a0_ref[...])
    g = _dot(_sigmoid(xg), wg_ref[...])

    eones2 = eones_ref[...]

    def head_sum(t):
        half = 2 * LANES
        return jnp.concatenate([_dot2_lhs(t[:, :half], eones2), _dot2_lhs(t[:, half:], eones2)], axis=1)

    kk = k * kk_ref[...]
    kk = kk * lax.rsqrt(jnp.maximum(head_sum(kk * kk), 1e-24))
    k = k * (1.0 + (ag - 1.0) * ka_ref[...])
    bonus = head_sum(r * k * rk_ref[...]) * v

    for p in range(N_PAIRS):
        sl = slice(p * LANES, (p + 1) * LANES)
        r_ref[0, p] = r[:, sl]
        lw_ref[0, p] = logw[:, sl]
        k_ref[0, p] = k[:, sl]
        v_ref[0, p] = v[:, sl]
        a_ref[0, p] = -kk[:, sl]
        b_ref[0, p] = (kk * ag)[:, sl]
        g_ref[0, p] = g[:, sl]
        bonus_ref[0, p] = bonus[:, sl]

    zin = proj[:, N_SHIFT:]
    z = 0.5 * zin * (1.0 + lax.erf(zin * (0.5 ** 0.5)))
    zu = z[:, :D_GMLP]
    zv = z[:, D_GMLP:]
    causal = _iota((GCHUNK, GCHUNK), 0) >= _iota((GCHUNK, GCHUNK), 1)
    for gi in range(GMLP_GROUPS):
        gs = slice(gi * GROUP_W, (gi + 1) * GROUP_W)
        zvn = _layer_norm(zv[:, gs], glng_ref[:, gs], glnb_ref[:, gs], LN_EPS)
        ws = jnp.where(causal, wsp_ref[gi], 0.0).astype(BF16)
        bcol = bsp_ref[:, gi:gi + 1]
        chunks = [slice(c * GCHUNK, (c + 1) * GCHUNK) for c in range(tm // GCHUNK)]
        zcat = jnp.concatenate([zvn[ts] for ts in chunks], axis=1).astype(BF16)
        mixed = jnp.dot(ws, zcat, preferred_element_type=F32) + bcol
        for ts in chunks:
            yb_ref[0, ts, gs] = (zu[ts, gs] * mixed[:, ts]).astype(BF16)


def _prep(x, ln_g, ln_b, w_in, mu, w_dec, w_iclr, w0, a0, w_gate, k_k, k_a, r_k, eones, glng, glnb, wsp, bsp):
    B, S, _ = x.shape
    tm = PREP_TM
    const = lambda shape: pl.BlockSpec(shape, lambda b, s: (0,) * len(shape))
    pair_spec = pl.BlockSpec((1, N_PAIRS, tm, LANES), lambda b, s: (b, 0, s, 0))
    pair_shape = jax.ShapeDtypeStruct((B, N_PAIRS, S, LANES), F32)
    return pl.pallas_call(
        _prep_kernel,
        grid=(B, S // tm),
        in_specs=[
            pl.BlockSpec((1, tm, D_MODEL), lambda b, s: (b, s, 0)),
            const((1, D_MODEL)), const((1, D_MODEL)),
            pl.BlockSpec((D_MODEL, D_IN), lambda b, s: (0, 0), pipeline_mode=pl.Buffered(1)),
            const((1, N_SHIFT)),
            const((DECAY_LORA, D_RWKV)), const((ICLR_LORA, D_RWKV)), const((1, D_RWKV)), const((1, D_RWKV)),
            const((GATE_LORA, D_RWKV)),
            const((1, D_RWKV)), const((1, D_RWKV)), const((1, D_RWKV)), const((4 * LANES, 2 * LANES)),
            const((1, D_GMLP)), const((1, D_GMLP)), const((GMLP_GROUPS, GCHUNK, GCHUNK)),
            const((GCHUNK, GMLP_GROUPS)),
        ],
        out_specs=[pair_spec] * 8 + [pl.BlockSpec((1, tm, D_GMLP), lambda b, s: (b, s, 0))],
        out_shape=[pair_shape] * 8 + [jax.ShapeDtypeStruct((B, S, D_GMLP), BF16)],
        scratch_shapes=[pltpu.VMEM((8, N_SHIFT), F32), pltpu.VMEM((D_MODEL, D_IN), BF16),
                        pltpu.VMEM((LANES, 2 * D_RWKV), BF16), pltpu.VMEM((GATE_LORA, D_RWKV), BF16)],
        compiler_params=pltpu.CompilerParams(dimension_semantics=("arbitrary", "arbitrary"),
                                             vmem_limit_bytes=VMEM_LIMIT),
        name="prep",
    )(x, ln_g, ln_b, w_in, mu, w_dec, w_iclr, w0, a0, w_gate, k_k, k_a, r_k, eones, glng, glnb, wsp, bsp)


def _wkv_kernel(r_ref, lw_ref, k_ref, v_ref, a_ref, b_ref, g_ref, bonus_ref, gng_ref, gnb_ref, emean_ref,
                o_ref, h_ref):
    C = WKV_CHUNK
    tb = r_ref.shape[2]

    @pl.when(pl.program_id(2) == 0)
    def _():
        h_ref[...] = jnp.zeros_like(h_ref)

    tok = _iota((C, LANES), 0)
    lane = _iota((C, LANES), 1)
    head0 = lane < HEAD
    strict = tok > lane % HEAD
    incl = tok >= lane % HEAD
    eye_w = (tok == lane % HEAD).astype(F32)
    rr = _iota((LANES, LANES), 0)
    cc = _iota((LANES, LANES), 1)
    eye = (rr == cc).astype(F32)
    same_head = (rr < HEAD) == (cc < HEAD)
    ltri3 = (_iota((C, 3 * C), 0) >= _iota((C, 3 * C), 1) % C).astype(BF16)

    def stack(x):
        xb = x.astype(BF16)
        zero = jnp.zeros_like(xb)
        return jnp.concatenate([jnp.where(head0, xb, zero), jnp.where(head0, zero, xb)], axis=0)

    def stack2(x, y):
        return jnp.concatenate([stack(x), stack(y)], axis=1)

    n_pairs = r_ref.shape[1]
    n_chunks = tb // C
    units = [(q, c) for q in range(n_pairs) for c in range(n_chunks)]

    def load(ref):
        return [ref[0, q, c * C:(c + 1) * C, :] for q, c in units]

    r_, lw_, k_, v_, a_, b_ = (load(ref) for ref in (r_ref, lw_ref, k_ref, v_ref, a_ref, b_ref))
    cum_ = [_dot3_rhs(ltri3, lw) for lw in lw_]
    cend_ = [cum[C - 1:C, :] for cum in cum_]
    at_ = [a * jnp.exp(cum - lw) for a, cum, lw in zip(a_, cum_, lw_)]
    rt_ = [r * jnp.exp(cum) for r, cum in zip(r_, cum_)]
    ginv_ = [jnp.exp(-cum) for cum in cum_]
    gend_ = [jnp.exp(cend - cum) for cend, cum in zip(cend_, cum_)]
    bk_end_ = [jnp.concatenate([b * ge, k * ge], axis=0) for b, k, ge in zip(b_, k_, gend_)]
    vst_ = [stack(v) for v in v_]

    G_ = [_dot_nt(jnp.concatenate([at, rt], axis=0), jnp.concatenate([stack(b * gi), stack(k * gi)], axis=0))
          for at, rt, b, k, gi in zip(at_, rt_, b_, k_, ginv_)]
    n1_ = [jnp.where(strict, G[:C, :LANES], 0.0) for G in G_]
    aak_ = [jnp.where(strict, G[:C, LANES:], 0.0) for G in G_]
    arb_ = [jnp.where(incl, G[C:, :LANES], 0.0) for G in G_]
    ark_ = [jnp.where(incl, G[C:, LANES:], 0.0) for G in G_]
    av_ = [_dot(jnp.concatenate([aak, ark], axis=0), vst) for aak, ark, vst in zip(aak_, ark_, vst_)]

    s1_ = [stack(n1) for n1 in n1_]
    n2_ = [_dot(n1, s1) for n1, s1 in zip(n1_, s1_)]
    x_ = [_dot(n2, jnp.concatenate([s1, stack(n2)], axis=1)) for n2, s1 in zip(n2_, s1_)]
    t_ = [eye_w + n1 + n2 + x[:, :LANES] for n1, n2, x in zip(n1_, n2_, x_)]
    np_ = [x[:, LANES:] for x in x_]
    for _ in range(3):
        x_ = [_dot(npow, stack2(t, npow)) for t, npow in zip(t_, np_)]
        t_ = [t + x[:, :LANES] for t, x in zip(t_, x_)]
        np_ = [x[:, LANES:] for x in x_]
    t_ = [t + _dot(npow, stack(t)) for t, npow in zip(t_, np_)]

    x_ = [_dot(t, stack2(at, av[:C])) for t, at, av in zip(t_, at_, av_)]
    z_ = [_dot(arb, stack2(x[:, :LANES], x[:, LANES:])) for arb, x in zip(arb_, x_)]
    rp_ = [rt + z[:, :LANES] for rt, z in zip(rt_, z_)]
    p3_ = [z[:, LANES:] + av[C:] for z, av in zip(z_, av_)]
    rhs_ = [jnp.concatenate([x, jnp.concatenate([jnp.zeros_like(v), v], axis=1)], axis=0)
            for x, v in zip(x_, v_)]
    mq_ = [_dot(bk_end.T, rhs) for bk_end, rhs in zip(bk_end_, rhs_)]
    m_ = [eye * jnp.exp(cend) + jnp.where(same_head, mq[:, :LANES], 0.0) for cend, mq in zip(cend_, mq_)]
    q_ = [jnp.where(same_head, mq[:, LANES:], 0.0) for mq in mq_]

    pairs = range(n_pairs)
    H_ = [h_ref[q] for q in pairs]
    ys_ = [[] for _ in pairs]
    rm_ = [jnp.concatenate([rp, m], axis=0) for rp, m in zip(rp_, m_)]
    for c in range(n_chunks):
        both_ = [_dot(rm_[q * n_chunks + c], H_[q]) for q in pairs]
        for q in pairs:
            ys_[q].append(both_[q][:C] + p3_[q * n_chunks + c])
        H_ = [both_[q][C:] + q_[q * n_chunks + c] for q in pairs]
    for q in pairs:
        h_ref[q] = H_[q]

    emean = emean_ref[...]
    y_ = [jnp.concatenate(ys, axis=0) for ys in ys_]
    mu_ = [_dot2_lhs(y, emean) for y in y_]
    yc_ = [y - mu for y, mu in zip(y_, mu_)]
    var_ = [_dot2_lhs(yc * yc, emean) for yc in yc_]
    for q in pairs:
        lanes_q = slice(q * LANES, (q + 1) * LANES)
        yn = yc_[q] * lax.rsqrt(var_[q] + GN_EPS) * gng_ref[:, lanes_q] + gnb_ref[:, lanes_q]
        o_ref[0, q] = ((yn + bonus_ref[0, q]) * g_ref[0, q]).astype(BF16)


def _wkv(r, lw, k, v, a, b, g, bonus, gn_g, gn_b, emean):
    B, P, S, _ = r.shape
    tb = WKV_TB
    pp = WKV_PAIRS
    seq = pl.BlockSpec((1, pp, tb, LANES), lambda bi, p, s: (bi, p, s, 0))
    par = pl.BlockSpec((1, pp * LANES), lambda bi, p, s: (0, p))
    return pl.pallas_call(
        _wkv_kernel,
        grid=(B, P // pp, S // tb),
        in_specs=[seq] * 8 + [par, par, pl.BlockSpec((2 * LANES, LANES), lambda bi, p, s: (0, 0))],
        out_specs=seq,
        out_shape=jax.ShapeDtypeStruct((B, P, S, LANES), BF16),
        scratch_shapes=[pltpu.VMEM((pp, LANES, LANES), F32)],
        compiler_params=pltpu.CompilerParams(dimension_semantics=("arbitrary", "arbitrary", "arbitrary"),
                                             vmem_limit_bytes=VMEM_LIMIT),
        name="wkv",
    )(r, lw, k, v, a, b, g, bonus, gn_g, gn_b, emean)


def _mixer_kernel(x_ref, lng_ref, lnb_ref, ya_ref, yb_ref, wout_ref, l1g_ref, l1b_ref, wr_ref, br_ref, below_ref,
                  base_ref, x1_ref, route_ref, counts_ref, carry_ref, wout_bf_ref):
    tm = x_ref.shape[1]

    @pl.when((pl.program_id(0) == 0) & (pl.program_id(1) == 0))
    def _():
        carry_ref[...] = jnp.zeros_like(carry_ref)
        wout_bf_ref[...] = wout_ref[...].astype(BF16)

    x0 = _layer_norm(x_ref[0], lng_ref[...], lnb_ref[...], LN_EPS)
    ymix = jnp.concatenate([ya_ref[0, p] for p in range(N_PAIRS)] + [yb_ref[0]], axis=-1)
    mix = jnp.dot(ymix, wout_bf_ref[...], preferred_element_type=F32)
    x1 = _layer_norm(ALPHA * x0 + mix, l1g_ref[...], l1b_ref[...], LN_EPS)
    x1b = x1.astype(BF16)
    half = D_MODEL // 2
    lo_bits = lax.bitcast_convert_type(x1b[:, :half].astype(F32), jnp.uint32)
    hi_bits = lax.bitcast_convert_type(x1b[:, half:].astype(F32), jnp.uint32)
    x1_ref[0] = (hi_bits & jnp.uint32(0xFFFF0000)) | (lo_bits >> 16)

    hi, mid = _split2(x1)
    wide = jnp.dot(hi, wr_ref[...], preferred_element_type=F32)
    logits = (wide[:, :LANES] + wide[:, LANES:]
              + jnp.dot(mid, wr_ref[:, :LANES], preferred_element_type=F32)) + br_ref[...]
    lane = _iota((tm, LANES), 1).astype(F32)
    far = float(4 * LANES)
    is_g = jnp.where(lane >= N_EXPERTS, jnp.where(lane < N_EXPERTS + N_GROUPS, 1.0, 0.0), 0.0) > 0.5
    gl = jnp.where(is_g, logits, NEG)
    gmax = jnp.max(gl, axis=-1, keepdims=True)
    gsel = jnp.min(jnp.where(gl == gmax, lane, far), axis=-1, keepdims=True) - N_EXPERTS
    p_group = 1.0 / jnp.sum(jnp.where(is_g, jnp.exp(gl - gmax), 0.0), axis=-1, keepdims=True)
    grp_of_lane = jnp.floor(lane * (1.0 / EXPERTS_PER_GROUP))
    el = jnp.where(grp_of_lane == gsel, logits, NEG)
    v1 = jnp.max(el, axis=-1, keepdims=True)
    i1 = jnp.min(jnp.where(el == v1, lane, far), axis=-1, keepdims=True)
    el2 = jnp.where(lane == i1, NEG, el)
    v2 = jnp.max(el2, axis=-1, keepdims=True)
    i2 = jnp.min(jnp.where(el2 == v2, lane, far), axis=-1, keepdims=True)
    e21 = jnp.exp(v2 - v1)
    w1 = p_group / (1.0 + e21)
    w2 = p_group * e21 / (1.0 + e21)

    oh1 = lane == i1
    oh2 = lane == i2
    below = below_ref[...]
    o1 = jnp.where(oh1, 1.0, 0.0)
    o2 = jnp.where(oh2, 1.0, 0.0)
    c12 = jnp.dot(below, jnp.concatenate([o1, o2], axis=1).astype(BF16), preferred_element_type=F32)
    c1 = c12[:, :LANES]
    c2 = c12[:, LANES:]
    tot1 = jnp.sum(o1, axis=0, keepdims=True)
    carry = carry_ref[0:1, :]
    rank1 = jnp.sum(jnp.where(oh1, c1 + carry, 0.0), axis=-1, keepdims=True)
    rank2 = jnp.sum(jnp.where(oh2, c2 + carry + tot1, 0.0), axis=-1, keepdims=True)
    carry = carry + tot1 + jnp.sum(o2, axis=0, keepdims=True)
    carry_ref[0:1, :] = carry
    counts_ref[...] = jnp.broadcast_to(carry, counts_ref.shape)

    fields = (i1, i2, w1, w2, rank1, rank2)
    route = jnp.zeros((tm, LANES), F32)
    for n, f in enumerate(fields):
        route = jnp.where(lane == n, f, route)
    route_ref[0] = route

    base_ref[0] = ALPHA * x1


def _mixer(x, ln_g, ln_b, ya, yb, w_out, l1g, l1b, wr3, br):
    B, S, _ = x.shape
    tm = MIX_TM
    const = lambda shape: pl.BlockSpec(shape, lambda b, s: (0,) * len(shape))
    row = lambda w: pl.BlockSpec((1, tm, w), lambda b, s: (b, s, 0))
    below = (jnp.arange(tm)[:, None] > jnp.arange(tm)[None, :]).astype(BF16)
    return pl.pallas_call(
        _mixer_kernel,
        grid=(B, S // tm),
        in_specs=[
            row(D_MODEL), const((1, D_MODEL)), const((1, D_MODEL)),
            pl.BlockSpec((1, N_PAIRS, tm, LANES), lambda b, s: (b, 0, s, 0)), row(D_GMLP),
            pl.BlockSpec((D_MODEL, D_MODEL), lambda b, s: (0, 0), pipeline_mode=pl.Buffered(1)),
            const((1, D_MODEL)), const((1, D_MODEL)),
            const((D_MODEL, 2 * LANES)), const((1, LANES)), const((tm, tm)),
        ],
        out_specs=[row(D_MODEL), row(D_MODEL // 2), row(LANES), const((8, LANES))],
        out_shape=[jax.ShapeDtypeStruct((B, S, D_MODEL), F32), jax.ShapeDtypeStruct((B, S, D_MODEL // 2), jnp.uint32),
                   jax.ShapeDtypeStruct((B, S, LANES), F32), jax.ShapeDtypeStruct((8, LANES), F32)],
        scratch_shapes=[pltpu.VMEM((8, LANES), F32), pltpu.VMEM((D_MODEL, D_MODEL), BF16)],
        compiler_params=pltpu.CompilerParams(dimension_semantics=("arbitrary", "arbitrary"),
                                             vmem_limit_bytes=VMEM_LIMIT),
        name="mixer",
    )(x, ln_g, ln_b, ya, yb, w_out, l1g, l1b, wr3, br, below)


def _slots_kernel(route_ref, counts_ref, dest_ref, pend_ref):
    tm = route_ref.shape[0]
    lane = _iota((tm, LANES), 1)
    route = route_ref[...]
    oh1 = lane == route[:, 0:1].astype(jnp.int32)
    oh2 = lane == route[:, 1:2].astype(jnp.int32)

    counts = counts_ref[0:1, :]
    padded = jnp.floor((counts + (EXPERT_ROWS - 1)) * (1.0 / EXPERT_ROWS)) * EXPERT_ROWS
    upper = (_iota((LANES, LANES), 0) <= _iota((LANES, LANES), 1)).astype(BF16)
    pend = _dot3_lhs(jnp.broadcast_to(padded, (8, LANES)), upper)[0:1, :]
    pstart = pend - padded
    d1 = jnp.sum(jnp.where(oh1, pstart, 0.0), axis=-1, keepdims=True) + route[:, 4:5]
    d2 = jnp.sum(jnp.where(oh2, pstart, 0.0), axis=-1, keepdims=True) + route[:, 5:6]
    dest = jnp.where(lane == 0, d1, jnp.where(lane == 1, d2, 0.0))
    dest_ref[...] = jnp.transpose(dest)[0:dest_ref.shape[0], :].astype(jnp.int32)
    pend_ref[...] = jnp.broadcast_to(pend, (8, LANES)).astype(jnp.int32)


def _slots(route, counts):
    T = route.shape[0]
    tm = SLOT_TM
    return pl.pallas_call(
        _slots_kernel,
        grid=(T // tm,),
        in_specs=[pl.BlockSpec((tm, LANES), lambda i: (i, 0)), pl.BlockSpec((8, LANES), lambda i: (0, 0))],
        out_specs=[pl.BlockSpec((8, tm), lambda i: (0, i)),
                   pl.BlockSpec((8, LANES), lambda i: (0, 0))],
        out_shape=[jax.ShapeDtypeStruct((8, T), jnp.int32), jax.ShapeDtypeStruct((8, LANES), jnp.int32)],
        compiler_params=pltpu.CompilerParams(dimension_semantics=("arbitrary",), vmem_limit_bytes=VMEM_LIMIT),
        name="slots",
    )(route, counts)


def _dispatch_kernel(pend_ref, dest0_ref, dest1_ref, x_ref, base_ref, p_ref, wpg_ref, bpg_ref, wpp_ref, xs_ref, resid_ref,
                     zero_ref, wpg_bf_ref, wpp_bf_ref, sem, zsem):
    tm = x_ref.shape[0]
    dest_refs = (dest0_ref, dest1_ref)

    @pl.when(pl.program_id(0) == 0)
    def _():
        wpg_bf_ref[...] = wpg_ref[...].astype(BF16)
        wpp_bf_ref[...] = wpp_ref[...].astype(BF16)
        zero_ref[...] = jnp.zeros_like(zero_ref)

        def tail(e):
            start = pl.multiple_of(jnp.maximum(pend_ref[e] - EXPERT_ROWS, 0), EXPERT_ROWS)
            return pltpu.make_async_copy(zero_ref, xs_ref.at[pl.ds(start, EXPERT_ROWS)], zsem)

        def unused(j):
            return pltpu.make_async_copy(
                zero_ref, xs_ref.at[pl.ds(pl.multiple_of(j * EXPERT_ROWS, EXPERT_ROWS), EXPERT_ROWS)], zsem)

        def start_unused(j, _):
            unused(j).start()
            return 0

        def wait_unused(j, _):
            unused(j).wait()
            return 0

        first_unused = pend_ref[N_EXPERTS - 1] // EXPERT_ROWS
        n_blocks = xs_ref.shape[0] // EXPERT_ROWS
        for e in range(N_EXPERTS):
            tail(e).start()
        lax.fori_loop(first_unused, n_blocks, start_unused, 0)
        for e in range(N_EXPERTS):
            tail(e).wait()
        lax.fori_loop(first_unused, n_blocks, wait_unused, 0)

    for t in range(tm):
        for j in range(TOP_K):
            pltpu.make_async_copy(x_ref.at[pl.ds(t, 1)], xs_ref.at[pl.ds(dest_refs[j][t], 1)],
                                  sem).start(priority=j)

    xw = x_ref[...]
    x_lo = lax.bitcast_convert_type(xw << 16, F32)
    x_hi = lax.bitcast_convert_type(xw & jnp.uint32(0xFFFF0000), F32)
    x1b = jnp.concatenate([x_lo, x_hi], axis=1).astype(BF16)
    gate = _sigmoid(jnp.dot(x1b, wpg_bf_ref[...], preferred_element_type=F32) + bpg_ref[...])
    ple = gate * jnp.dot(p_ref[...].astype(BF16), wpp_bf_ref[...], preferred_element_type=F32)
    resid_ref[...] = base_ref[...] + ple

    for j in range(TOP_K):
        pltpu.make_async_copy(x_ref, xs_ref.at[pl.ds(0, tm)], sem).wait()


def _dispatch(pend, dests, x1, base, p, wpg, bpg, wpp, n_rows):
    T, width = x1.shape
    tm = DISPATCH_TM
    const = lambda shape: pl.BlockSpec(shape, lambda i, pe: (0,) * len(shape))
    resident = lambda shape: pl.BlockSpec(shape, lambda i, pe: (0,) * len(shape), pipeline_mode=pl.Buffered(1))
    tile = lambda w: pl.BlockSpec((tm, w), lambda i, pe: (i, 0))
    index_list = pl.BlockSpec((tm,), lambda i, pe: (i,), memory_space=pltpu.SMEM)
    return pl.pallas_call(
        _dispatch_kernel,
        grid_spec=pltpu.PrefetchScalarGridSpec(
            num_scalar_prefetch=1,
            grid=(T // tm,),
            in_specs=[index_list, index_list,
                      tile(width), tile(D_MODEL), tile(D_PLE),
                      resident((D_MODEL, D_MODEL)), const((1, D_MODEL)), resident((D_PLE, D_MODEL))],
            out_specs=[pl.BlockSpec(memory_space=pl.ANY), tile(D_MODEL)],
            scratch_shapes=[pltpu.VMEM((EXPERT_ROWS, width), x1.dtype), pltpu.VMEM((D_MODEL, D_MODEL), BF16),
                            pltpu.VMEM((D_PLE, D_MODEL), BF16), pltpu.SemaphoreType.DMA,
                            pltpu.SemaphoreType.DMA],
        ),
        out_shape=[jax.ShapeDtypeStruct((n_rows, width), x1.dtype), jax.ShapeDtypeStruct((T, D_MODEL), F32)],
        compiler_params=pltpu.CompilerParams(dimension_semantics=("arbitrary",), vmem_limit_bytes=VMEM_LIMIT),
        name="dispatch",
    )(pend, dests[0], dests[1], x1, base, p, wpg, bpg, wpp)


def _experts_kernel(pend_ref, xs_ref, wg_ref, wu_ref, wd_ref, ys_ref, xbuf_ref, ybuf_ref, wgu_ref, wdb_ref,
                    in_sem, out_sem):
    rows = EXPERT_ROWS
    e = pl.program_id(0)
    first = jnp.where(e == 0, 0, pend_ref[jnp.maximum(e - 1, 0)]) // rows
    last = pend_ref[e] // rows
    n_used = pend_ref[N_EXPERTS - 1] // rows

    def block_rows(ref, b):
        return ref.at[pl.ds(pl.multiple_of(b * rows, rows), rows)]

    depth = xbuf_ref.shape[0]
    row_priority = 1

    def x_copy(b):
        slot = b % depth
        return pltpu.make_async_copy(block_rows(xs_ref, b), xbuf_ref.at[slot], in_sem.at[slot])

    def y_copy(b):
        slot = b % depth
        return pltpu.make_async_copy(ybuf_ref.at[slot], block_rows(ys_ref, b), out_sem.at[slot])

    @pl.when(e == 0)
    def _():
        for ahead in range(depth - 1):
            @pl.when(ahead < n_used)
            def _():
                x_copy(ahead).start(priority=row_priority)

    @pl.when(last > first)
    def _():
        wgu_ref[:, :D_EXPERT] = wg_ref[0].astype(BF16)
        wgu_ref[:, D_EXPERT:] = wu_ref[0].astype(BF16)
        wdb_ref[...] = wd_ref[0].astype(BF16)

        def body(b, _):
            slot = b % depth

            @pl.when(b + depth - 1 < n_used)
            def _():
                x_copy(b + depth - 1).start(priority=row_priority)

            x_copy(b).wait()

            @pl.when(b >= depth)
            def _():
                y_copy(b - depth).wait()

            xw = xbuf_ref[slot]
            x_lo = lax.bitcast_convert_type(xw << 16, F32)
            x_hi = lax.bitcast_convert_type(xw & jnp.uint32(0xFFFF0000), F32)
            xb = jnp.concatenate([x_lo, x_hi], axis=1).astype(BF16)
            h = jnp.dot(xb, wgu_ref[...], preferred_element_type=F32)
            hg = h[:, :D_EXPERT]
            hid = hg * _sigmoid(hg) * h[:, D_EXPERT:]
            ybuf_ref[slot] = jnp.dot(hid.astype(BF16), wdb_ref[...], preferred_element_type=F32)
            y_copy(b).start(priority=row_priority)
            return 0

        lax.fori_loop(first, last, body, 0)

    @pl.when(e == N_EXPERTS - 1)
    def _():
        for back in range(depth, 0, -1):
            @pl.when(n_used >= back)
            def _():
                y_copy(n_used - back).wait()

        ybuf_ref[0] = jnp.zeros(ybuf_ref.shape[1:], F32)

        def unused(b):
            return pltpu.make_async_copy(ybuf_ref.at[0], block_rows(ys_ref, b), out_sem.at[0])

        def start_unused(b, _):
            unused(b).start()
            return 0

        def wait_unused(b, _):
            unused(b).wait()
            return 0

        n_blocks = ys_ref.shape[0] // rows
        lax.fori_loop(n_used, n_blocks, start_unused, 0)
        lax.fori_loop(n_used, n_blocks, wait_unused, 0)


def _experts(pend, xs, wg, wu, wd):
    n_rows = xs.shape[0]
    rows = EXPERT_ROWS
    wspec = lambda shape: pl.BlockSpec((1,) + shape, lambda e, pe: (e, 0, 0))
    return pl.pallas_call(
        _experts_kernel,
        grid_spec=pltpu.PrefetchScalarGridSpec(
            num_scalar_prefetch=1,
            grid=(N_EXPERTS,),
            in_specs=[pl.BlockSpec(memory_space=pl.ANY),
                      wspec((D_MODEL, D_EXPERT)), wspec((D_MODEL, D_EXPERT)), wspec((D_EXPERT, D_MODEL))],
            out_specs=pl.BlockSpec(memory_space=pl.ANY),
            scratch_shapes=[pltpu.VMEM((EXPERT_DEPTH, rows, D_MODEL // 2), jnp.uint32),
                            pltpu.VMEM((EXPERT_DEPTH, rows, D_MODEL), F32),
                            pltpu.VMEM((D_MODEL, 2 * D_EXPERT), BF16), pltpu.VMEM((D_EXPERT, D_MODEL), BF16),
                            pltpu.SemaphoreType.DMA((EXPERT_DEPTH,)), pltpu.SemaphoreType.DMA((EXPERT_DEPTH,))],
        ),
        out_shape=jax.ShapeDtypeStruct((n_rows, D_MODEL), F32),
        compiler_params=pltpu.CompilerParams(dimension_semantics=("arbitrary",), vmem_limit_bytes=VMEM_LIMIT),
        name="experts",
    )(pend, xs, wg, wu, wd)


def _combine_kernel(dest0_ref, dest1_ref, dest0_next_ref, dest1_next_ref, ys_ref, resid_ref, route_ref, lg_ref, lb_ref,
                    o_ref, buf_ref, sem):
    tm = buf_ref.shape[2]
    i = pl.program_id(0)
    dest_refs = (dest0_ref, dest1_ref)
    dest_next_refs = (dest0_next_ref, dest1_next_ref)

    def gather(drefs, offset, s):
        for t in range(tm):
            for j in range(TOP_K):
                pltpu.make_async_copy(ys_ref.at[pl.ds(drefs[j][offset + t], 1)],
                                      buf_ref.at[s, j, pl.ds(t, 1)], sem.at[s]).start(priority=j)

    def drain(s):
        for j in range(TOP_K):
            pltpu.make_async_copy(ys_ref.at[pl.ds(0, tm)], buf_ref.at[s, j], sem.at[s]).wait()

    def finish(s):
        rows = slice(s * tm, (s + 1) * tm)
        drain(s)
        route = route_ref[rows, :]
        ffn = buf_ref[s, 0] * route[:, 2:3] + buf_ref[s, 1] * route[:, 3:4]
        o_ref[rows, :] = _layer_norm(resid_ref[rows, :] + ffn, lg_ref[...], lb_ref[...], LN_EPS)

    @pl.when(i == 0)
    def _():
        gather(dest_refs, 0, 0)

    gather(dest_refs, tm, 1)
    finish(0)
    gather(dest_next_refs, 0, 0)
    finish(1)

    @pl.when(i == pl.num_programs(0) - 1)
    def _():
        drain(0)


def _combine(dests, ys, resid, route, l2g, l2b):
    T = resid.shape[0]
    tm = COMBINE_TM
    nt = T // tm
    tile = lambda w: pl.BlockSpec((2 * tm, w), lambda i: (i, 0))
    const = lambda shape: pl.BlockSpec(shape, lambda i: (0,) * len(shape))
    pair_list = pl.BlockSpec((2 * tm,), lambda i: (i,), memory_space=pltpu.SMEM)
    next_list = pl.BlockSpec((tm,), lambda i: (jnp.minimum(2 * i + 2, nt - 1),), memory_space=pltpu.SMEM)
    return pl.pallas_call(
        _combine_kernel,
        grid=(nt // 2,),
        in_specs=[pair_list, pair_list, next_list, next_list,
                  pl.BlockSpec(memory_space=pl.ANY),
                  tile(D_MODEL), tile(LANES), const((1, D_MODEL)), const((1, D_MODEL))],
        out_specs=pl.BlockSpec((2 * tm, D_MODEL), lambda i: (i, 0)),
        out_shape=jax.ShapeDtypeStruct((T, D_MODEL), F32),
        scratch_shapes=[pltpu.VMEM((2, TOP_K, tm, D_MODEL), F32), pltpu.SemaphoreType.DMA((2,))],
        compiler_params=pltpu.CompilerParams(dimension_semantics=("arbitrary",), vmem_limit_bytes=VMEM_LIMIT),
        name="combine",
    )(dests[0], dests[1], dests[0], dests[1], ys, resid, route, l2g, l2b)


def _block_diag_const(n, blk, val):
    idx = jnp.arange(n) // blk
    return jnp.where(idx[:, None] == idx[None, :], val, 0.0).astype(BF16)


def kernel(x, p, ln_emb_g, ln_emb_b, w_in, mu_shift, w0, w_decay_up, a0, w_iclr_up, w_gate_up, k_k, k_a, r_k, gn_g, gn_b, gmlp_ln_g, gmlp_ln_b, w_spatial, b_spatial, w_out, ln1_g, ln1_b, w_group_router, b_group_router, w_expert_router, b_expert_router, w_exp_gate, w_exp_up, w_exp_down, w_ple_gate, b_ple_gate, w_ple_proj, ln2_g, ln2_b):
    B, S, D = x.shape
    T = B * S
    row = lambda t: t.reshape(1, -1).astype(F32)

    eones = jnp.tile(_block_diag_const(2 * LANES, HEAD, 1.0), (2, 1))
    emean = jnp.tile(_block_diag_const(LANES, HEAD, 1.0 / HEAD), (2, 1))

    r, lw, k, v, a, b, g, bonus, yb = _prep(
        x, row(ln_emb_g), row(ln_emb_b), w_in[0], row(mu_shift[0]), w_decay_up[0], w_iclr_up[0], row(w0[0]), row(a0[0]),
        w_gate_up[0], row(k_k[0]), row(k_a[0]), row(r_k[0]), eones,
        row(gmlp_ln_g[0]), row(gmlp_ln_b[0]), w_spatial[0], b_spatial[0].T)

    ya = _wkv(r, lw, k, v, a, b, g, bonus, row(gn_g[0]), row(gn_b[0]), emean)

    wr = jnp.concatenate([w_expert_router[0].reshape(D, N_EXPERTS), w_group_router[0],
                          jnp.zeros((D, LANES - N_EXPERTS - N_GROUPS), F32)], axis=1)
    wr3 = jnp.concatenate(_split2(wr), axis=1)
    br = jnp.concatenate([b_expert_router[0].reshape(-1), b_group_router[0],
                          jnp.zeros((LANES - N_EXPERTS - N_GROUPS,), F32)]).reshape(1, LANES)
    base, x1, route, counts = _mixer(x, row(ln_emb_g), row(ln_emb_b), ya, yb, w_out[0], row(ln1_g[0]),
                                     row(ln1_b[0]), wr3, br)
    base = base.reshape(T, D)
    x1 = x1.reshape(T, D // 2)
    route = route.reshape(T, LANES)

    n_blocks = -(-(T * TOP_K) // EXPERT_ROWS) + N_EXPERTS
    dest, pend = _slots(route, counts)
    dests = (dest[0], dest[1])
    pend = pend[0, :N_EXPERTS]

    xs, resid = _dispatch(pend, dests, x1, base, p[0].reshape(T, D_PLE), w_ple_gate[0],
                          row(b_ple_gate[0]), w_ple_proj[0], n_blocks * EXPERT_ROWS)
    ys = _experts(pend, xs, w_exp_gate[0], w_exp_up[0], w_exp_down[0])
    out = _combine(dests, ys, resid, route, row(ln2_g[0]), row(ln2_b[0]))
    return out.reshape(B, S, D)
```

```python
import math

import jax
import jax.numpy as jnp
from jax import lax
from jax.experimental import pallas as pl
from jax.experimental.pallas import tpu as pltpu

F32 = jnp.float32
BF16 = jnp.bfloat16

D_MODEL = 1024
D_RWKV = 512
HEAD = 64
D_GMLP = 512
GMLP_GROUPS = 4
GROUP_W = 128
GCHUNK = 128
DECAY_LORA = 64
ICLR_LORA = 64
GATE_LORA = 128
N_SHIFT = 3 * D_RWKV + DECAY_LORA + ICLR_LORA + GATE_LORA
D_IN = N_SHIFT + 2 * D_GMLP
D_PLE = 256
N_GROUPS = 4
EXPERTS_PER_GROUP = 8
N_EXPERTS = 32
TOP_K = 2
D_EXPERT = 512
DEPTH = 1
ALPHA = (2.0 * DEPTH) ** 0.25
LN_EPS = 1e-5
GN_EPS = 64e-5
DECAY_SCALE = math.exp(-0.5)

LANES = 128
WKV_CHUNK = 64
N_PAIRS = D_RWKV // LANES
VMEM_LIMIT = 56 * 1024 * 1024

PREP_TM = 512
WKV_TB = 1024
WKV_PAIRS = 4
MIX_TM = 512
SLOT_TM = 4096
EXPERT_ROWS = 256
EXPERT_DEPTH = 4
DISPATCH_TM = 1024
COMBINE_TM = 256
NEG = -1e30


def _dot(a, b):
    return jnp.dot(a.astype(BF16), b.astype(BF16), preferred_element_type=F32)


def _dot_nt(a, b):
    return lax.dot_general(a.astype(BF16), b.astype(BF16), (((1,), (1,)), ((), ())),
                           preferred_element_type=F32)


def _split3(x):
    hi = x.astype(BF16)
    r1 = x - hi.astype(F32)
    mid = r1.astype(BF16)
    lo = (r1 - mid.astype(F32)).astype(BF16)
    return hi, mid, lo


def _dot3_lhs(x, w):
    hi, mid, lo = _split3(x)
    w = w.astype(BF16)
    return (jnp.dot(hi, w, preferred_element_type=F32) + jnp.dot(mid, w, preferred_element_type=F32)
            + jnp.dot(lo, w, preferred_element_type=F32))


def _split2(x):
    hi = x.astype(BF16)
    return hi, (x - hi.astype(F32)).astype(BF16)


def _dot2_lhs(x, w2):
    hi, lo = _split2(x)
    return jnp.dot(jnp.concatenate([hi, lo], axis=1), w2, preferred_element_type=F32)


def _dot3_rhs(w3, x):
    hi, mid, lo = _split3(x)
    return jnp.dot(w3, jnp.concatenate([hi, mid, lo], axis=0), preferred_element_type=F32)


def _layer_norm(x, g, b, eps):
    mu = jnp.mean(x, axis=-1, keepdims=True)
    xc = x - mu
    var = jnp.mean(xc * xc, axis=-1, keepdims=True)
    return xc * lax.rsqrt(var + eps) * g + b


def _sigmoid(x):
    return 1.0 / (1.0 + jnp.exp(-x))


def _iota(shape, dim):
    return lax.broadcasted_iota(jnp.int32, shape, dim)


def _prep_kernel(x_ref, lng_ref, lnb_ref, win_ref, mu_ref, wdec_ref, wiclr_ref, w0_ref, a0_ref, wgate_ref, kk_ref, ka_ref,
                 rk_ref, eones_ref, glng_ref, glnb_ref, wsp_ref, bsp_ref,
                 r_ref, lw_ref, k_ref, v_ref, a_ref, b_ref, g_ref, bonus_ref, yb_ref, carry_ref, win_bf_ref,
                 wwa_ref, wg_ref):
    tm = x_ref.shape[1]

    @pl.when((pl.program_id(0) == 0) & (pl.program_id(1) == 0))
    def _():
        win_bf_ref[...] = win_ref[...].astype(BF16)
        wg_ref[...] = wgate_ref[...].astype(BF16)
        wwa_ref[...] = jnp.zeros_like(wwa_ref)
        wwa_ref[0:DECAY_LORA, 0:D_RWKV] = wdec_ref[...].astype(BF16)
        wwa_ref[DECAY_LORA:, D_RWKV:] = wiclr_ref[...].astype(BF16)

    @pl.when(pl.program_id(1) == 0)
    def _():
        carry_ref[...] = jnp.zeros_like(carry_ref)

    x0 = _layer_norm(x_ref[0], lng_ref[...], lnb_ref[...], LN_EPS)
    proj = jnp.dot(x0.astype(BF16), win_bf_ref[...], preferred_element_type=F32)

    h = proj[:, :N_SHIFT]
    rolled = pltpu.roll(h, 1, 0)
    first = _iota((tm, N_SHIFT), 0) == 0
    prev = jnp.where(first, jnp.broadcast_to(carry_ref[0:1, :], (tm, N_SHIFT)), rolled)
    carry_ref[0:1, :] = h[tm - 1:tm, :]
    h = h + (prev - h) * mu_ref[...]

    r = h[:, 0:D_RWKV]
    k = h[:, D_RWKV:2 * D_RWKV]
    v = h[:, 2 * D_RWKV:3 * D_RWKV]
    xwa = h[:, 3 * D_RWKV:3 * D_RWKV + LANES]
    xg = h[:, 3 * D_RWKV + LANES:N_SHIFT]

    lane = _iota((tm, LANES), 1)
    twa = jnp.where(lane < DECAY_LORA, jnp.tanh(xwa), xwa)
    da = _dot(twa, wwa_ref[...])
    logw = -DECAY_SCALE * _sigmoid(da[:, :D_RWKV] + w0_ref[...])
    ag = _sigmoid(da[:, D_RWKV:] + a0_ref[...])
    g = _dot(_sigmoid(xg), wg_ref[...])

    eones2 = eones_ref[...]

    def head_sum(t):
        half = 2 * LANES
        return jnp.concatenate([_dot2_lhs(t[:, :half], eones2), _dot2_lhs(t[:, half:], eones2)], axis=1)

    kk = k * kk_ref[...]
    kk = kk * lax.rsqrt(jnp.maximum(head_sum(kk * kk), 1e-24))
    k = k * (1.0 + (ag - 1.0) * ka_ref[...])
    bonus = head_sum(r * k * rk_ref[...]) * v

    for p in range(N_PAIRS):
        sl = slice(p * LANES, (p + 1) * LANES)
        r_ref[0, p] = r[:, sl]
        lw_ref[0, p] = logw[:, sl]
        k_ref[0, p] = k[:, sl]
        v_ref[0, p] = v[:, sl]
        a_ref[0, p] = -kk[:, sl]
        b_ref[0, p] = (kk * ag)[:, sl]
        g_ref[0, p] = g[:, sl]
        bonus_ref[0, p] = bonus[:, sl]

    zin = proj[:, N_SHIFT:]
    z = 0.5 * zin * (1.0 + lax.erf(zin * (0.5 ** 0.5)))
    zu = z[:, :D_GMLP]
    zv = z[:, D_GMLP:]
    causal = _iota((GCHUNK, GCHUNK), 0) >= _iota((GCHUNK, GCHUNK), 1)
    for gi in range(GMLP_GROUPS):
        gs = slice(gi * GROUP_W, (gi + 1) * GROUP_W)
        zvn = _layer_norm(zv[:, gs], glng_ref[:, gs], glnb_ref[:, gs], LN_EPS)
        ws = jnp.where(causal, wsp_ref[gi], 0.0).astype(BF16)
        bcol = bsp_ref[:, gi:gi + 1]
        chunks = [slice(c * GCHUNK, (c + 1) * GCHUNK) for c in range(tm // GCHUNK)]
        zcat = jnp.concatenate([zvn[ts] for ts in chunks], axis=1).astype(BF16)
        mixed = jnp.dot(ws, zcat, preferred_element_type=F32) + bcol
        for ts in chunks:
            yb_ref[0, ts, gs] = (zu[ts, gs] * mixed[:, ts]).astype(BF16)


def _prep(x, ln_g, ln_b, w_in, mu, w_dec, w_iclr, w0, a0, w_gate, k_k, k_a, r_k, eones, glng, glnb, wsp, bsp):
    B, S, _ = x.shape
    tm = PREP_TM
    const = lambda shape: pl.BlockSpec(shape, lambda b, s: (0,) * len(shape))
    pair_spec = pl.BlockSpec((1, N_PAIRS, tm, LANES), lambda b, s: (b, 0, s, 0))
    pair_shape = jax.ShapeDtypeStruct((B, N_PAIRS, S, LANES), F32)
    return pl.pallas_call(
        _prep_kernel,
        grid=(B, S // tm),
        in_specs=[
            pl.BlockSpec((1, tm, D_MODEL), lambda b, s: (b, s, 0)),
            const((1, D_MODEL)), const((1, D_MODEL)),
            pl.BlockSpec((D_MODEL, D_IN), lambda b, s: (0, 0), pipeline_mode=pl.Buffered(1)),
            const((1, N_SHIFT)),
            const((DECAY_LORA, D_RWKV)), const((ICLR_LORA, D_RWKV)), const((1, D_RWKV)), const((1, D_RWKV)),
            const((GATE_LORA, D_RWKV)),
            const((1, D_RWKV)), const((1, D_RWKV)), const((1, D_RWKV)), const((4 * LANES, 2 * LANES)),
            const((1, D_GMLP)), const((1, D_GMLP)), const((GMLP_GROUPS, GCHUNK, GCHUNK)),
            const((GCHUNK, GMLP_GROUPS)),
        ],
        out_specs=[pair_spec] * 8 + [pl.BlockSpec((1, tm, D_GMLP), lambda b, s: (b, s, 0))],
        out_shape=[pair_shape] * 8 + [jax.ShapeDtypeStruct((B, S, D_GMLP), BF16)],
        scratch_shapes=[pltpu.VMEM((8, N_SHIFT), F32), pltpu.VMEM((D_MODEL, D_IN), BF16),
                        pltpu.VMEM((LANES, 2 * D_RWKV), BF16), pltpu.VMEM((GATE_LORA, D_RWKV), BF16)],
        compiler_params=pltpu.CompilerParams(dimension_semantics=("arbitrary", "arbitrary"),
                                             vmem_limit_bytes=VMEM_LIMIT),
        name="prep",
    )(x, ln_g, ln_b, w_in, mu, w_dec, w_iclr, w0, a0, w_gate, k_k, k_a, r_k, eones, glng, glnb, wsp, bsp)


def _wkv_kernel(r_ref, lw_ref, k_ref, v_ref, a_ref, b_ref, g_ref, bonus_ref, gng_ref, gnb_ref, emean_ref,
                o_ref, h_ref):
    C = WKV_CHUNK
    tb = r_ref.shape[2]

    @pl.when(pl.program_id(2) == 0)
    def _():
        h_ref[...] = jnp.zeros_like(h_ref)

    tok = _iota((C, LANES), 0)
    lane = _iota((C, LANES), 1)
    head0 = lane < HEAD
    strict = tok > lane % HEAD
    incl = tok >= lane % HEAD
    eye_w = (tok == lane % HEAD).astype(F32)
    rr = _iota((LANES, LANES), 0)
    cc = _iota((LANES, LANES), 1)
    eye = (rr == cc).astype(F32)
    same_head = (rr < HEAD) == (cc < HEAD)
    ltri3 = (_iota((C, 3 * C), 0) >= _iota((C, 3 * C), 1) % C).astype(BF16)

    def stack(x):
        xb = x.astype(BF16)
        zero = jnp.zeros_like(xb)
        return jnp.concatenate([jnp.where(head0, xb, zero), jnp.where(head0, zero, xb)], axis=0)

    def stack2(x, y):
        return jnp.concatenate([stack(x), stack(y)], axis=1)

    n_pairs = r_ref.shape[1]
    n_chunks = tb // C
    units = [(q, c) for q in range(n_pairs) for c in range(n_chunks)]

    def load(ref):
        return [ref[0, q, c * C:(c + 1) * C, :] for q, c in units]

    r_, lw_, k_, v_, a_, b_ = (load(ref) for ref in (r_ref, lw_ref, k_ref, v_ref, a_ref, b_ref))
    cum_ = [_dot3_rhs(ltri3, lw) for lw in lw_]
    cend_ = [cum[C - 1:C, :] for cum in cum_]
    at_ = [a * jnp.exp(cum - lw) for a, cum, lw in zip(a_, cum_, lw_)]
    rt_ = [r * jnp.exp(cum) for r, cum in zip(r_, cum_)]
    ginv_ = [jnp.exp(-cum) for cum in cum_]
    gend_ = [jnp.exp(cend - cum) for cend, cum in zip(cend_, cum_)]
    bk_end_ = [jnp.concatenate([b * ge, k * ge], axis=0) for b, k, ge in zip(b_, k_, gend_)]
    vst_ = [stack(v) for v in v_]

    G_ = [_dot_nt(jnp.concatenate([at, rt], axis=0), jnp.concatenate([stack(b * gi), stack(k * gi)], axis=0))
          for at, rt, b, k, gi in zip(at_, rt_, b_, k_, ginv_)]
    n1_ = [jnp.where(strict, G[:C, :LANES], 0.0) for G in G_]
    aak_ = [jnp.where(strict, G[:C, LANES:], 0.0) for G in G_]
    arb_ = [jnp.where(incl, G[C:, :LANES], 0.0) for G in G_]
    ark_ = [jnp.where(incl, G[C:, LANES:], 0.0) for G in G_]
    av_ = [_dot(jnp.concatenate([aak, ark], axis=0), vst) for aak, ark, vst in zip(aak_, ark_, vst_)]

    s1_ = [stack(n1) for n1 in n1_]
    n2_ = [_dot(n1, s1) for n1, s1 in zip(n1_, s1_)]
    x_ = [_dot(n2, jnp.concatenate([s1, stack(n2)], axis=1)) for n2, s1 in zip(n2_, s1_)]
    t_ = [eye_w + n1 + n2 + x[:, :LANES] for n1, n2, x in zip(n1_, n2_, x_)]
    np_ = [x[:, LANES:] for x in x_]
    for _ in range(3):
        x_ = [_dot(npow, stack2(t, npow)) for t, npow in zip(t_, np_)]
        t_ = [t + x[:, :LANES] for t, x in zip(t_, x_)]
        np_ = [x[:, LANES:] for x in x_]
    t_ = [t + _dot(npow, stack(t)) for t, npow in zip(t_, np_)]

    x_ = [_dot(t, stack2(at, av[:C])) for t, at, av in zip(t_, at_, av_)]
    z_ = [_dot(arb, stack2(x[:, :LANES], x[:, LANES:])) for arb, x in zip(arb_, x_)]
    rp_ = [rt + z[:, :LANES] for rt, z in zip(rt_, z_)]
    p3_ = [z[:, LANES:] + av[C:] for z, av in zip(z_, av_)]
    rhs_ = [jnp.concatenate([x, jnp.concatenate([jnp.zeros_like(v), v], axis=1)], axis=0)
            for x, v in zip(x_, v_)]
    mq_ = [_dot(bk_end.T, rhs) for bk_end, rhs in zip(bk_end_, rhs_)]
    m_ = [eye * jnp.exp(cend) + jnp.where(same_head, mq[:, :LANES], 0.0) for cend, mq in zip(cend_, mq_)]
    q_ = [jnp.where(same_head, mq[:, LANES:], 0.0) for mq in mq_]

    pairs = range(n_pairs)
    H_ = [h_ref[q] for q in pairs]
    ys_ = [[] for _ in pairs]
    rm_ = [jnp.concatenate([rp, m], axis=0) for rp, m in zip(rp_, m_)]
    for c in range(n_chunks):
        both_ = [_dot(rm_[q * n_chunks + c], H_[q]) for q in pairs]
        for q in pairs:
            ys_[q].append(both_[q][:C] + p3_[q * n_chunks + c])
        H_ = [both_[q][C:] + q_[q * n_chunks + c] for q in pairs]
    for q in pairs:
        h_ref[q] = H_[q]

    emean = emean_ref[...]
    y_ = [jnp.concatenate(ys, axis=0) for ys in ys_]
    mu_ = [_dot2_lhs(y, emean) for y in y_]
    yc_ = [y - mu for y, mu in zip(y_, mu_)]
    var_ = [_dot2_lhs(yc * yc, emean) for yc in yc_]
    for q in pairs:
        lanes_q = slice(q * LANES, (q + 1) * LANES)
        yn = yc_[q] * lax.rsqrt(var_[q] + GN_EPS) * gng_ref[:, lanes_q] + gnb_ref[:, lanes_q]
        o_ref[0, q] = ((yn + bonus_ref[0, q]) * g_ref[0, q]).astype(BF16)


def _wkv(r, lw, k, v, a, b, g, bonus, gn_g, gn_b, emean):
    B, P, S, _ = r.shape
    tb = WKV_TB
    pp = WKV_PAIRS
    seq = pl.BlockSpec((1, pp, tb, LANES), lambda bi, p, s: (bi, p, s, 0))
    par = pl.BlockSpec((1, pp * LANES), lambda bi, p, s: (0, p))
    return pl.pallas_call(
        _wkv_kernel,
        grid=(B, P // pp, S // tb),
        in_specs=[seq] * 8 + [par, par, pl.BlockSpec((2 * LANES, LANES), lambda bi, p, s: (0, 0))],
        out_specs=seq,
        out_shape=jax.ShapeDtypeStruct((B, P, S, LANES), BF16),
        scratch_shapes=[pltpu.VMEM((pp, LANES, LANES), F32)],
        compiler_params=pltpu.CompilerParams(dimension_semantics=("arbitrary", "arbitrary", "arbitrary"),
                                             vmem_limit_bytes=VMEM_LIMIT),
        name="wkv",
    )(r, lw, k, v, a, b, g, bonus, gn_g, gn_b, emean)


def _mixer_kernel(x_ref, lng_ref, lnb_ref, ya_ref, yb_ref, wout_ref, l1g_ref, l1b_ref, wr_ref, br_ref, below_ref,
                  base_ref, x1_ref, route_ref, counts_ref, carry_ref, wout_bf_ref):
    tm = x_ref.shape[1]

    @pl.when((pl.program_id(0) == 0) & (pl.program_id(1) == 0))
    def _():
        carry_ref[...] = jnp.zeros_like(carry_ref)
        wout_bf_ref[...] = wout_ref[...].astype(BF16)

    x0 = _layer_norm(x_ref[0], lng_ref[...], lnb_ref[...], LN_EPS)
    ymix = jnp.concatenate([ya_ref[0, p] for p in range(N_PAIRS)] + [yb_ref[0]], axis=-1)
    mix = jnp.dot(ymix, wout_bf_ref[...], preferred_element_type=F32)
    x1 = _layer_norm(ALPHA * x0 + mix, l1g_ref[...], l1b_ref[...], LN_EPS)
    x1b = x1.astype(BF16)
    half = D_MODEL // 2
    lo_bits = lax.bitcast_convert_type(x1b[:, :half].astype(F32), jnp.uint32)
    hi_bits = lax.bitcast_convert_type(x1b[:, half:].astype(F32), jnp.uint32)
    x1_ref[0] = (hi_bits & jnp.uint32(0xFFFF0000)) | (lo_bits >> 16)

    hi, mid = _split2(x1)
    wide = jnp.dot(hi, wr_ref[...], preferred_element_type=F32)
    logits = (wide[:, :LANES] + wide[:, LANES:]
              + jnp.dot(mid, wr_ref[:, :LANES], preferred_element_type=F32)) + br_ref[...]
    lane = _iota((tm, LANES), 1).astype(F32)
    far = float(4 * LANES)
    is_g = jnp.where(lane >= N_EXPERTS, jnp.where(lane < N_EXPERTS + N_GROUPS, 1.0, 0.0), 0.0) > 0.5
    gl = jnp.where(is_g, logits, NEG)
    gmax = jnp.max(gl, axis=-1, keepdims=True)
    gsel = jnp.min(jnp.where(gl == gmax, lane, far), axis=-1, keepdims=True) - N_EXPERTS
    p_group = 1.0 / jnp.sum(jnp.where(is_g, jnp.exp(gl - gmax), 0.0), axis=-1, keepdims=True)
    grp_of_lane = jnp.floor(lane * (1.0 / EXPERTS_PER_GROUP))
    el = jnp.where(grp_of_lane == gsel, logits, NEG)
    v1 = jnp.max(el, axis=-1, keepdims=True)
    i1 = jnp.min(jnp.where(el == v1, lane, far), axis=-1, keepdims=True)
    el2 = jnp.where(lane == i1, NEG, el)
    v2 = jnp.max(el2, axis=-1, keepdims=True)
    i2 = jnp.min(jnp.where(el2 == v2, lane, far), axis=-1, keepdims=True)
    e21 = jnp.exp(v2 - v1)
    w1 = p_group / (1.0 + e21)
    w2 = p_group * e21 / (1.0 + e21)

    oh1 = lane == i1
    oh2 = lane == i2
    below = below_ref[...]
    o1 = jnp.where(oh1, 1.0, 0.0)
    o2 = jnp.where(oh2, 1.0, 0.0)
    c12 = jnp.dot(below, jnp.concatenate([o1, o2], axis=1).astype(BF16), preferred_element_type=F32)
    c1 = c12[:, :LANES]
    c2 = c12[:, LANES:]
    tot1 = jnp.sum(o1, axis=0, keepdims=True)
    carry = carry_ref[0:1, :]
    rank1 = jnp.sum(jnp.where(oh1, c1 + carry, 0.0), axis=-1, keepdims=True)
    rank2 = jnp.sum(jnp.where(oh2, c2 + carry + tot1, 0.0), axis=-1, keepdims=True)
    carry = carry + tot1 + jnp.sum(o2, axis=0, keepdims=True)
    carry_ref[0:1, :] = carry
    counts_ref[...] = jnp.broadcast_to(carry, counts_ref.shape)

    fields = (i1, i2, w1, w2, rank1, rank2)
    route = jnp.zeros((tm, LANES), F32)
    for n, f in enumerate(fields):
        route = jnp.where(lane == n, f, route)
    route_ref[0] = route

    base_ref[0] = ALPHA * x1


def _mixer(x, ln_g, ln_b, ya, yb, w_out, l1g, l1b, wr3, br):
    B, S, _ = x.shape
    tm = MIX_TM
    const = lambda shape: pl.BlockSpec(shape, lambda b, s: (0,) * len(shape))
    row = lambda w: pl.BlockSpec((1, tm, w), lambda b, s: (b, s, 0))
    below = (jnp.arange(tm)[:, None] > jnp.arange(tm)[None, :]).astype(BF16)
    return pl.pallas_call(
        _mixer_kernel,
        grid=(B, S // tm),
        in_specs=[
            row(D_MODEL), const((1, D_MODEL)), const((1, D_MODEL)),
            pl.BlockSpec((1, N_PAIRS, tm, LANES), lambda b, s: (b, 0, s, 0)), row(D_GMLP),
            pl.BlockSpec((D_MODEL, D_MODEL), lambda b, s: (0, 0), pipeline_mode=pl.Buffered(1)),
            const((1, D_MODEL)), const((1, D_MODEL)),
            const((D_MODEL, 2 * LANES)), const((1, LANES)), const((tm, tm)),
        ],
        out_specs=[row(D_MODEL), row(D_MODEL // 2), row(LANES), const((8, LANES))],
        out_shape=[jax.ShapeDtypeStruct((B, S, D_MODEL), F32), jax.ShapeDtypeStruct((B, S, D_MODEL // 2), jnp.uint32),
                   jax.ShapeDtypeStruct((B, S, LANES), F32), jax.ShapeDtypeStruct((8, LANES), F32)],
        scratch_shapes=[pltpu.VMEM((8, LANES), F32), pltpu.VMEM((D_MODEL, D_MODEL), BF16)],
        compiler_params=pltpu.CompilerParams(dimension_semantics=("arbitrary", "arbitrary"),
                                             vmem_limit_bytes=VMEM_LIMIT),
        name="mixer",
    )(x, ln_g, ln_b, ya, yb, w_out, l1g, l1b, wr3, br, below)


def _slots_kernel(route_ref, counts_ref, dest_ref, pend_ref):
    tm = route_ref.shape[0]
    lane = _iota((tm, LANES), 1)
    route = route_ref[...]
    oh1 = lane == route[:, 0:1].astype(jnp.int32)
    oh2 = lane == route[:, 1:2].astype(jnp.int32)

    counts = counts_ref[0:1, :]
    padded = jnp.floor((counts + (EXPERT_ROWS - 1)) * (1.0 / EXPERT_ROWS)) * EXPERT_ROWS
    upper = (_iota((LANES, LANES), 0) <= _iota((LANES, LANES), 1)).astype(BF16)
    pend = _dot3_lhs(jnp.broadcast_to(padded, (8, LANES)), upper)[0:1, :]
    pstart = pend - padded
    d1 = jnp.sum(jnp.where(oh1, pstart, 0.0), axis=-1, keepdims=True) + route[:, 4:5]
    d2 = jnp.sum(jnp.where(oh2, pstart, 0.0), axis=-1, keepdims=True) + route[:, 5:6]
    dest = jnp.where(lane == 0, d1, jnp.where(lane == 1, d2, 0.0))
    dest_ref[...] = jnp.transpose(dest)[0:dest_ref.shape[0], :].astype(jnp.int32)
    pend_ref[...] = jnp.broadcast_to(pend, (8, LANES)).astype(jnp.int32)


def _slots(route, counts):
    T = route.shape[0]
    tm = SLOT_TM
    return pl.pallas_call(
        _slots_kernel,
        grid=(T // tm,),
        in_specs=[pl.BlockSpec((tm, LANES), lambda i: (i, 0)), pl.BlockSpec((8, LANES), lambda i: (0, 0))],
        out_specs=[pl.BlockSpec((8, tm), lambda i: (0, i)),
                   pl.BlockSpec((8, LANES), lambda i: (0, 0))],
        out_shape=[jax.ShapeDtypeStruct((8, T), jnp.int32), jax.ShapeDtypeStruct((8, LANES), jnp.int32)],
        compiler_params=pltpu.CompilerParams(dimension_semantics=("arbitrary",), vmem_limit_bytes=VMEM_LIMIT),
        name="slots",
    )(route, counts)


def _dispatch_kernel(pend_ref, dest0_ref, dest1_ref, x_ref, base_ref, p_ref, wpg_ref, bpg_ref, wpp_ref, xs_ref, resid_ref,
                     zero_ref, wpg_bf_ref, wpp_bf_ref, sem, zsem):
    tm = x_ref.shape[0]
    dest_refs = (dest0_ref, dest1_ref)

    @pl.when(pl.program_id(0) == 0)
    def _():
        wpg_bf_ref[...] = wpg_ref[...].astype(BF16)
        wpp_bf_ref[...] = wpp_ref[...].astype(BF16)
        zero_ref[...] = jnp.zeros_like(zero_ref)

        def tail(e):
            start = pl.multiple_of(jnp.maximum(pend_ref[e] - EXPERT_ROWS, 0), EXPERT_ROWS)
            return pltpu.make_async_copy(zero_ref, xs_ref.at[pl.ds(start, EXPERT_ROWS)], zsem)

        def unused(j):
            return pltpu.make_async_copy(
                zero_ref, xs_ref.at[pl.ds(pl.multiple_of(j * EXPERT_ROWS, EXPERT_ROWS), EXPERT_ROWS)], zsem)

        def start_unused(j, _):
            unused(j).start()
            return 0

        def wait_unused(j, _):
            unused(j).wait()
            return 0

        first_unused = pend_ref[N_EXPERTS - 1] // EXPERT_ROWS
        n_blocks = xs_ref.shape[0] // EXPERT_ROWS
        for e in range(N_EXPERTS):
            tail(e).start()
        lax.fori_loop(first_unused, n_blocks, start_unused, 0)
        for e in range(N_EXPERTS):
            tail(e).wait()
        lax.fori_loop(first_unused, n_blocks, wait_unused, 0)

    for t in range(tm):
        for j in range(TOP_K):
            pltpu.make_async_copy(x_ref.at[pl.ds(t, 1)], xs_ref.at[pl.ds(dest_refs[j][t], 1)],
                                  sem).start(priority=j)

    xw = x_ref[...]
    x_lo = lax.bitcast_convert_type(xw << 16, F32)
    x_hi = lax.bitcast_convert_type(xw & jnp.uint32(0xFFFF0000), F32)
    x1b = jnp.concatenate([x_lo, x_hi], axis=1).astype(BF16)
    gate = _sigmoid(jnp.dot(x1b, wpg_bf_ref[...], preferred_element_type=F32) + bpg_ref[...])
    ple = gate * jnp.dot(p_ref[...].astype(BF16), wpp_bf_ref[...], preferred_element_type=F32)
    resid_ref[...] = base_ref[...] + ple

    for j in range(TOP_K):
        pltpu.make_async_copy(x_ref, xs_ref.at[pl.ds(0, tm)], sem).wait()


def _dispatch(pend, dests, x1, base, p, wpg, bpg, wpp, n_rows):
    T, width = x1.shape
    tm = DISPATCH_TM
    const = lambda shape: pl.BlockSpec(shape, lambda i, pe: (0,) * len(shape))
    resident = lambda shape: pl.BlockSpec(shape, lambda i, pe: (0,) * len(shape), pipeline_mode=pl.Buffered(1))
    tile = lambda w: pl.BlockSpec((tm, w), lambda i, pe: (i, 0))
    index_list = pl.BlockSpec((tm,), lambda i, pe: (i,), memory_space=pltpu.SMEM)
    return pl.pallas_call(
        _dispatch_kernel,
        grid_spec=pltpu.PrefetchScalarGridSpec(
            num_scalar_prefetch=1,
            grid=(T // tm,),
            in_specs=[index_list, index_list,
                      tile(width), tile(D_MODEL), tile(D_PLE),
                      resident((D_MODEL, D_MODEL)), const((1, D_MODEL)), resident((D_PLE, D_MODEL))],
            out_specs=[pl.BlockSpec(memory_space=pl.ANY), tile(D_MODEL)],
            scratch_shapes=[pltpu.VMEM((EXPERT_ROWS, width), x1.dtype), pltpu.VMEM((D_MODEL, D_MODEL), BF16),
                            pltpu.VMEM((D_PLE, D_MODEL), BF16), pltpu.SemaphoreType.DMA,
                            pltpu.SemaphoreType.DMA],
        ),
        out_shape=[jax.ShapeDtypeStruct((n_rows, width), x1.dtype), jax.ShapeDtypeStruct((T, D_MODEL), F32)],
        compiler_params=pltpu.CompilerParams(dimension_semantics=("arbitrary",), vmem_limit_bytes=VMEM_LIMIT),
        name="dispatch",
    )(pend, dests[0], dests[1], x1, base, p, wpg, bpg, wpp)


def _experts_kernel(pend_ref, xs_ref, wg_ref, wu_ref, wd_ref, ys_ref, xbuf_ref, ybuf_ref, wgu_ref, wdb_ref,
                    in_sem, out_sem):
    rows = EXPERT_ROWS
    e = pl.program_id(0)
    first = jnp.where(e == 0, 0, pend_ref[jnp.maximum(e - 1, 0)]) // rows
    last = pend_ref[e] // rows
    n_used = pend_ref[N_EXPERTS - 1] // rows

    def block_rows(ref, b):
        return ref.at[pl.ds(pl.multiple_of(b * rows, rows), rows)]

    depth = xbuf_ref.shape[0]
    row_priority = 1

    def x_copy(b):
        slot = b % depth
        return pltpu.make_async_copy(block_rows(xs_ref, b), xbuf_ref.at[slot], in_sem.at[slot])

    def y_copy(b):
        slot = b % depth
        return pltpu.make_async_copy(ybuf_ref.at[slot], block_rows(ys_ref, b), out_sem.at[slot])

    @pl.when(e == 0)
    def _():
        for ahead in range(depth - 1):
            @pl.when(ahead < n_used)
            def _():
                x_copy(ahead).start(priority=row_priority)

    @pl.when(last > first)
    def _():
        wgu_ref[:, :D_EXPERT] = wg_ref[0].astype(BF16)
        wgu_ref[:, D_EXPERT:] = wu_ref[0].astype(BF16)
        wdb_ref[...] = wd_ref[0].astype(BF16)

        def body(b, _):
            slot = b % depth

            @pl.when(b + depth - 1 < n_used)
            def _():
                x_copy(b + depth - 1).start(priority=row_priority)

            x_copy(b).wait()

            @pl.when(b >= depth)
            def _():
                y_copy(b - depth).wait()

            xw = xbuf_ref[slot]
            x_lo = lax.bitcast_convert_type(xw << 16, F32)
            x_hi = lax.bitcast_convert_type(xw & jnp.uint32(0xFFFF0000), F32)
            xb = jnp.concatenate([x_lo, x_hi], axis=1).astype(BF16)
            h = jnp.dot(xb, wgu_ref[...], preferred_element_type=F32)
            hg = h[:, :D_EXPERT]
            hid = hg * _sigmoid(hg) * h[:, D_EXPERT:]
            ybuf_ref[slot] = jnp.dot(hid.astype(BF16), wdb_ref[...], preferred_element_type=F32)
            y_copy(b).start(priority=row_priority)
            return 0

        lax.fori_loop(first, last, body, 0)

    @pl.when(e == N_EXPERTS - 1)
    def _():
        for back in range(depth, 0, -1):
            @pl.when(n_used >= back)
            def _():
                y_copy(n_used - back).wait()

        ybuf_ref[0] = jnp.zeros(ybuf_ref.shape[1:], F32)

        def unused(b):
            return pltpu.make_async_copy(ybuf_ref.at[0], block_rows(ys_ref, b), out_sem.at[0])

        def start_unused(b, _):
            unused(b).start()
            return 0

        def wait_unused(b, _):
            unused(b).wait()
            return 0

        n_blocks = ys_ref.shape[0] // rows
        lax.fori_loop(n_used, n_blocks, start_unused, 0)
        lax.fori_loop(n_used, n_blocks, wait_unused, 0)


def _experts(pend, xs, wg, wu, wd):
    n_rows = xs.shape[0]
    rows = EXPERT_ROWS
    wspec = lambda shape: pl.BlockSpec((1,) + shape, lambda e, pe: (e, 0, 0))
    return pl.pallas_call(
        _experts_kernel,
        grid_spec=pltpu.PrefetchScalarGridSpec(
            num_scalar_prefetch=1,
            grid=(N_EXPERTS,),
            in_specs=[pl.BlockSpec(memory_space=pl.ANY),
                      wspec((D_MODEL, D_EXPERT)), wspec((D_MODEL, D_EXPERT)), wspec((D_EXPERT, D_MODEL))],
            out_specs=pl.BlockSpec(memory_space=pl.ANY),
            scratch_shapes=[pltpu.VMEM((EXPERT_DEPTH, rows, D_MODEL // 2), jnp.uint32),
                            pltpu.VMEM((EXPERT_DEPTH, rows, D_MODEL), F32),
                            pltpu.VMEM((D_MODEL, 2 * D_EXPERT), BF16), pltpu.VMEM((D_EXPERT, D_MODEL), BF16),
                            pltpu.SemaphoreType.DMA((EXPERT_DEPTH,)), pltpu.SemaphoreType.DMA((EXPERT_DEPTH,))],
        ),
        out_shape=jax.ShapeDtypeStruct((n_rows, D_MODEL), F32),
        compiler_params=pltpu.CompilerParams(dimension_semantics=("arbitrary",), vmem_limit_bytes=VMEM_LIMIT),
        name="experts",
    )(pend, xs, wg, wu, wd)


def _combine_kernel(dest0_ref, dest1_ref, dest0_next_ref, dest1_next_ref, ys_ref, resid_ref, route_ref, lg_ref, lb_ref,
                    o_ref, buf_ref, sem):
    tm = buf_ref.shape[2]
    i = pl.program_id(0)
    dest_refs = (dest0_ref, dest1_ref)
    dest_next_refs = (dest0_next_ref, dest1_next_ref)

    def gather(drefs, offset, s):
        for t in range(tm):
            for j in range(TOP_K):
                pltpu.make_async_copy(ys_ref.at[pl.ds(drefs[j][offset + t], 1)],
                                      buf_ref.at[s, j, pl.ds(t, 1)], sem.at[s]).start(priority=j)

    def drain(s):
        for j in range(TOP_K):
            pltpu.make_async_copy(ys_ref.at[pl.ds(0, tm)], buf_ref.at[s, j], sem.at[s]).wait()

    def finish(s):
        rows = slice(s * tm, (s + 1) * tm)
        drain(s)
        route = route_ref[rows, :]
        ffn = buf_ref[s, 0] * route[:, 2:3] + buf_ref[s, 1] * route[:, 3:4]
        o_ref[rows, :] = _layer_norm(resid_ref[rows, :] + ffn, lg_ref[...], lb_ref[...], LN_EPS)

    @pl.when(i == 0)
    def _():
        gather(dest_refs, 0, 0)

    gather(dest_refs, tm, 1)
    finish(0)
    gather(dest_next_refs, 0, 0)
    finish(1)

    @pl.when(i == pl.num_programs(0) - 1)
    def _():
        drain(0)


def _combine(dests, ys, resid, route, l2g, l2b):
    T = resid.shape[0]
    tm = COMBINE_TM
    nt = T // tm
    tile = lambda w: pl.BlockSpec((2 * tm, w), lambda i: (i, 0))
    const = lambda shape: pl.BlockSpec(shape, lambda i: (0,) * len(shape))
    pair_list = pl.BlockSpec((2 * tm,), lambda i: (i,), memory_space=pltpu.SMEM)
    next_list = pl.BlockSpec((tm,), lambda i: (jnp.minimum(2 * i + 2, nt - 1),), memory_space=pltpu.SMEM)
    return pl.pallas_call(
        _combine_kernel,
        grid=(nt // 2,),
        in_specs=[pair_list, pair_list, next_list, next_list,
                  pl.BlockSpec(memory_space=pl.ANY),
                  tile(D_MODEL), tile(LANES), const((1, D_MODEL)), const((1, D_MODEL))],
        out_specs=pl.BlockSpec((2 * tm, D_MODEL), lambda i: (i, 0)),
        out_shape=jax.ShapeDtypeStruct((T, D_MODEL), F32),
        scratch_shapes=[pltpu.VMEM((2, TOP_K, tm, D_MODEL), F32), pltpu.SemaphoreType.DMA((2,))],
        compiler_params=pltpu.CompilerParams(dimension_semantics=("arbitrary",), vmem_limit_bytes=VMEM_LIMIT),
        name="combine",
    )(dests[0], dests[1], dests[0], dests[1], ys, resid, route, l2g, l2b)


def _block_diag_const(n, blk, val):
    idx = jnp.arange(n) // blk
    return jnp.where(idx[:, None] == idx[None, :], val, 0.0).astype(BF16)


def kernel(x, p, ln_emb_g, ln_emb_b, w_in, mu_shift, w0, w_decay_up, a0, w_iclr_up, w_gate_up, k_k, k_a, r_k, gn_g, gn_b, gmlp_ln_g, gmlp_ln_b, w_spatial, b_spatial, w_out, ln1_g, ln1_b, w_group_router, b_group_router, w_expert_router, b_expert_router, w_exp_gate, w_exp_up, w_exp_down, w_ple_gate, b_ple_gate, w_ple_proj, ln2_g, ln2_b):
    B, S, D = x.shape
    T = B * S
    row = lambda t: t.reshape(1, -1).astype(F32)

    eones = jnp.tile(_block_diag_const(2 * LANES, HEAD, 1.0), (2, 1))
    emean = jnp.tile(_block_diag_const(LANES, HEAD, 1.0 / HEAD), (2, 1))

    r, lw, k, v, a, b, g, bonus, yb = _prep(
        x, row(ln_emb_g), row(ln_emb_b), w_in[0], row(mu_shift[0]), w_decay_up[0], w_iclr_up[0], row(w0[0]), row(a0[0]),
        w_gate_up[0], row(k_k[0]), row(k_a[0]), row(r_k[0]), eones,
        row(gmlp_ln_g[0]), row(gmlp_ln_b[0]), w_spatial[0], b_spatial[0].T)

    ya = _wkv(r, lw, k, v, a, b, g, bonus, row(gn_g[0]), row(gn_b[0]), emean)

    wr = jnp.concatenate([w_expert_router[0].reshape(D, N_EXPERTS), w_group_router[0],
                          jnp.zeros((D, LANES - N_EXPERTS - N_GROUPS), F32)], axis=1)
    wr3 = jnp.concatenate(_split2(wr), axis=1)
    br = jnp.concatenate([b_expert_router[0].reshape(-1), b_group_router[0],
                          jnp.zeros((LANES - N_EXPERTS - N_GROUPS,), F32)]).reshape(1, LANES)
    base, x1, route, counts = _mixer(x, row(ln_emb_g), row(ln_emb_b), ya, yb, w_out[0], row(ln1_g[0]),
                                     row(ln1_b[0]), wr3, br)
    base = base.reshape(T, D)
    x1 = x1.reshape(T, D // 2)
    route = route.reshape(T, LANES)

    n_blocks = -(-(T * TOP_K) // EXPERT_ROWS) + N_EXPERTS
    dest, pend = _slots(route, counts)
    dests = (dest[0], dest[1])
    pend = pend[0, :N_EXPERTS]

    xs, resid = _dispatch(pend, dests, x1, base, p[0].reshape(T, D_PLE), w_ple_gate[0],
                          row(b_ple_gate[0]), w_ple_proj[0], n_blocks * EXPERT_ROWS)
    ys = _experts(pend, xs, w_exp_gate[0], w_exp_up[0], w_exp_down[0])
    out = _combine(dests, ys, resid, route, row(ln2_g[0]), row(ln2_b[0]))
    return out.reshape(B, S, D)
```
